```python
import jax, jax.numpy as jnp
from jax import lax
import numpy as np

D_MODEL = 1024
BATCH = 8
SEQ = 4096
DEPTH = 4

HEAD_DIM = 64
N_ATT_HEADS = 8
D_ATT = N_ATT_HEADS * HEAD_DIM
N_CONV_GROUPS = 4
D_CONV = N_CONV_GROUPS * HEAD_DIM
N_SGU_GROUPS = 4
D_SGU = N_SGU_GROUPS * HEAD_DIM
SGU_CHUNK = 128
Q_BLOCK = 128
CONV_WIDTH = 3
D_FF = 2816
N_BRANCHES = 3
RMS_EPS = 1e-6
LN_EPS = 1e-5
IN_WIDTHS = (D_ATT, D_ATT, D_ATT, N_ATT_HEADS, D_CONV, D_CONV, D_CONV, D_SGU, D_SGU, N_BRANCHES * D_MODEL)
IN_WIDTH = 3 * D_ATT + N_ATT_HEADS + 3 * D_CONV + 2 * D_SGU + N_BRANCHES * D_MODEL

kernel_name = "fox_shortconv_sgu_gated_hybrid"


def rms_norm(x, g):
    xf = x.astype(jnp.float32)
    y = xf * lax.rsqrt(jnp.mean(xf * xf, axis=-1, keepdims=True) + RMS_EPS)
    return (y * g.astype(jnp.float32)).astype(x.dtype)


def layer_norm(x, g, b):
    xf = x.astype(jnp.float32)
    mu = jnp.mean(xf, axis=-1, keepdims=True)
    xc = xf - mu
    var = jnp.mean(xc * xc, axis=-1, keepdims=True)
    y = xc * lax.rsqrt(var + LN_EPS) * g.astype(jnp.float32) + b.astype(jnp.float32)
    return y.astype(x.dtype)


def split_cols(h, widths):
    offs = np.cumsum(np.array(widths))[:-1].tolist()
    return jnp.split(h, offs, axis=-1)


def causal_dwconv(x, w):
    K = w.shape[0]
    S = x.shape[1]
    xp = jnp.pad(x, ((0, 0), (K - 1, 0), (0, 0)))
    y = xp[:, K - 1:K - 1 + S] * w[K - 1]
    for k in range(K - 1):
        y = y + xp[:, k:k + S] * w[k]
    return y


def fox_attention(q, k, v, logf):
    S = q.shape[1]
    scale = HEAD_DIM ** -0.5
    c = jnp.cumsum(logf, axis=1).transpose(0, 2, 1)
    outs = []
    for i in range(S // Q_BLOCK):
        q0 = i * Q_BLOCK
        q1 = q0 + Q_BLOCK
        s = jnp.einsum('bqhd,bkhd->bhqk', q[:, q0:q1], k[:, :q1]).astype(jnp.float32) * scale
        bias = c[:, :, q0:q1, None] - c[:, :, None, :q1]
        mask = jnp.arange(q0, q1)[:, None] >= jnp.arange(q1)[None, :]
        s = jnp.where(mask, s + bias, -jnp.inf)
        p = jax.nn.softmax(s, axis=-1).astype(v.dtype)
        outs.append(jnp.einsum('bhqk,bkhd->bqhd', p, v[:, :q1]))
    return jnp.concatenate(outs, axis=1)


def short_conv_mixer(b_gate, c_gate, h, conv_w):
    return b_gate * causal_dwconv(c_gate * h, conv_w)


def chunked_sgu(u, v, ln_g, ln_b, w_s, b_s):
    B_, S, _ = u.shape
    u = jax.nn.gelu(u, approximate=True)
    v = layer_norm(jax.nn.gelu(v, approximate=True), ln_g, ln_b)
    n = S // SGU_CHUNK
    vc = v.reshape(B_, n, SGU_CHUNK, N_SGU_GROUPS, HEAD_DIM)
    mask = jnp.tril(jnp.ones((SGU_CHUNK, SGU_CHUNK), w_s.dtype))
    mixed = jnp.einsum('gts,bnsgd->bntgd', w_s * mask, vc) + b_s.T[:, :, None]
    return u * mixed.reshape(B_, S, D_SGU)


def conv_gated_ffn(x, w_up, conv_w, w_down):
    h = causal_dwconv(x @ w_up, conv_w)
    a, b = jnp.split(h, 2, axis=-1)
    return (jax.nn.gelu(a, approximate=True) * b) @ w_down


def _fwd_setup_inputs(seed: int = 0) -> dict:
    key = jax.random.key(seed)
    ks = jax.random.split(key, 24)
    f32 = jnp.float32

    def nrm(k, shape, scale):
        return jax.random.normal(k, shape, f32) * scale

    L = DEPTH
    return {
        "x": nrm(ks[0], (BATCH, SEQ, D_MODEL), 1.0),
        "pre_mix_g": 1.0 + nrm(ks[1], (L, D_MODEL), 0.02),
        "post_mix_g": 1.0 + nrm(ks[2], (L, D_MODEL), 0.02),
        "pre_ffn_g": 1.0 + nrm(ks[3], (L, D_MODEL), 0.02),
        "post_ffn_g": 1.0 + nrm(ks[4], (L, D_MODEL), 0.02),
        "w_in": nrm(ks[5], (L, D_MODEL, IN_WIDTH), D_MODEL ** -0.5),
        "b_forget": jnp.linspace(1.0, 6.0, N_ATT_HEADS, dtype=f32)[None, :] + nrm(ks[6], (L, N_ATT_HEADS), 0.1),
        "b_gate": nrm(ks[7], (L, N_BRANCHES, D_MODEL), 0.02),
        "conv_mix_w": nrm(ks[8], (L, CONV_WIDTH, D_CONV), CONV_WIDTH ** -0.5),
        "sgu_ln_g": 1.0 + nrm(ks[9], (L, D_SGU), 0.02),
        "sgu_ln_b": nrm(ks[10], (L, D_SGU), 0.02),
        "sgu_w": nrm(ks[11], (L, N_SGU_GROUPS, SGU_CHUNK, SGU_CHUNK), SGU_CHUNK ** -0.5),
        "sgu_b": 1.0 + nrm(ks[12], (L, N_SGU_GROUPS, SGU_CHUNK), 0.02),
        "w_branch_att": nrm(ks[13], (L, D_ATT, D_MODEL), D_ATT ** -0.5),
        "w_branch_conv": nrm(ks[14], (L, D_CONV, D_MODEL), D_CONV ** -0.5),
        "w_branch_sgu": nrm(ks[15], (L, D_SGU, D_MODEL), D_SGU ** -0.5),
        "w_out": nrm(ks[16], (L, D_MODEL, D_MODEL), D_MODEL ** -0.5),
        "w_ffn_up": nrm(ks[17], (L, D_MODEL, 2 * D_FF), D_MODEL ** -0.5),
        "conv_ffn_w": nrm(ks[18], (L, CONV_WIDTH, 2 * D_FF), CONV_WIDTH ** -0.5),
        "w_ffn_down": nrm(ks[19], (L, D_FF, D_MODEL), D_FF ** -0.5),
    }


def _fwd_reference(x, pre_mix_g, post_mix_g, pre_ffn_g, post_ffn_g, w_in, b_forget, b_gate,
              conv_mix_w, sgu_ln_g, sgu_ln_b, sgu_w, sgu_b, w_branch_att, w_branch_conv,
              w_branch_sgu, w_out, w_ffn_up, conv_ffn_w, w_ffn_down):
    B_, S, D = x.shape
    for l in range(DEPTH):
        xn = rms_norm(x, pre_mix_g[l])
        h = xn @ w_in[l]
        q, k, v, f_logit, bg, cg, hc, u, vs, g_logit = split_cols(h, IN_WIDTHS)
        q = q.reshape(B_, S, N_ATT_HEADS, HEAD_DIM)
        k = k.reshape(B_, S, N_ATT_HEADS, HEAD_DIM)
        v = v.reshape(B_, S, N_ATT_HEADS, HEAD_DIM)
        logf = jax.nn.log_sigmoid((f_logit + b_forget[l]).astype(jnp.float32))
        y_att = fox_attention(q, k, v, logf).reshape(B_, S, D_ATT) @ w_branch_att[l]
        y_conv = short_conv_mixer(bg, cg, hc, conv_mix_w[l]) @ w_branch_conv[l]
        y_sgu = chunked_sgu(u, vs, sgu_ln_g[l], sgu_ln_b[l], sgu_w[l], sgu_b[l]) @ w_branch_sgu[l]
        gates = jax.nn.sigmoid(g_logit.reshape(B_, S, N_BRANCHES, D) + b_gate[l])
        merged = gates[:, :, 0] * y_att + gates[:, :, 1] * y_conv + gates[:, :, 2] * y_sgu
        x = x + rms_norm(merged @ w_out[l], post_mix_g[l])
        xn = rms_norm(x, pre_ffn_g[l])
        x = x + rms_norm(conv_gated_ffn(xn, w_ffn_up[l], conv_ffn_w[l], w_ffn_down[l]), post_ffn_g[l])
    return x


import jax as _jax
import jax.numpy as _jnp

TWIN_FORMAT = 'train_step'
FWD_PARAMS = ['x', 'pre_mix_g', 'post_mix_g', 'pre_ffn_g', 'post_ffn_g', 'w_in', 'b_forget', 'b_gate', 'conv_mix_w', 'sgu_ln_g', 'sgu_ln_b', 'sgu_w', 'sgu_b', 'w_branch_att', 'w_branch_conv', 'w_branch_sgu', 'w_out', 'w_ffn_up', 'conv_ffn_w', 'w_ffn_down']
TWIN_WEIGHTS = ['pre_mix_g', 'post_mix_g', 'pre_ffn_g', 'post_ffn_g', 'w_in', 'b_forget', 'b_gate', 'conv_mix_w', 'sgu_ln_g', 'sgu_ln_b', 'sgu_w', 'sgu_b', 'w_branch_att', 'w_branch_conv', 'w_branch_sgu', 'w_out', 'w_ffn_up', 'conv_ffn_w', 'w_ffn_down']
TWIN_DIFF_INPUT = 'x'
TWIN_INPUTS = ['x', 'pre_mix_g', 'post_mix_g', 'pre_ffn_g', 'post_ffn_g', 'w_in', 'b_forget', 'b_gate', 'conv_mix_w', 'sgu_ln_g', 'sgu_ln_b', 'sgu_w', 'sgu_b', 'w_branch_att', 'w_branch_conv', 'w_branch_sgu', 'w_out', 'w_ffn_up', 'conv_ffn_w', 'w_ffn_down', 'loss_target', 'm_pre_mix_g', 'm_post_mix_g', 'm_pre_ffn_g', 'm_post_ffn_g', 'm_w_in', 'm_b_forget', 'm_b_gate', 'm_conv_mix_w', 'm_sgu_ln_g', 'm_sgu_ln_b', 'm_sgu_w', 'm_sgu_b', 'm_w_branch_att', 'm_w_branch_conv', 'm_w_branch_sgu', 'm_w_out', 'm_w_ffn_up', 'm_conv_ffn_w', 'm_w_ffn_down', 'v_pre_mix_g', 'v_post_mix_g', 'v_pre_ffn_g', 'v_post_ffn_g', 'v_w_in', 'v_b_forget', 'v_b_gate', 'v_conv_mix_w', 'v_sgu_ln_g', 'v_sgu_ln_b', 'v_sgu_w', 'v_sgu_b', 'v_w_branch_att', 'v_w_branch_conv', 'v_w_branch_sgu', 'v_w_out', 'v_w_ffn_up', 'v_conv_ffn_w', 'v_w_ffn_down']
TWIN_OUTPUTS = ['loss', 'grad_x', 'grad_pre_mix_g', 'grad_post_mix_g', 'grad_pre_ffn_g', 'grad_post_ffn_g', 'grad_w_in', 'grad_b_forget', 'grad_b_gate', 'grad_conv_mix_w', 'grad_sgu_ln_g', 'grad_sgu_ln_b', 'grad_sgu_w', 'grad_sgu_b', 'grad_w_branch_att', 'grad_w_branch_conv', 'grad_w_branch_sgu', 'grad_w_out', 'grad_w_ffn_up', 'grad_conv_ffn_w', 'grad_w_ffn_down', 'delta_pre_mix_g', 'delta_post_mix_g', 'delta_pre_ffn_g', 'delta_post_ffn_g', 'delta_w_in', 'delta_b_forget', 'delta_b_gate', 'delta_conv_mix_w', 'delta_sgu_ln_g', 'delta_sgu_ln_b', 'delta_sgu_w', 'delta_sgu_b', 'delta_w_branch_att', 'delta_w_branch_conv', 'delta_w_branch_sgu', 'delta_w_out', 'delta_w_ffn_up', 'delta_conv_ffn_w', 'delta_w_ffn_down', 'new_m_pre_mix_g', 'new_m_post_mix_g', 'new_m_pre_ffn_g', 'new_m_post_ffn_g', 'new_m_w_in', 'new_m_b_forget', 'new_m_b_gate', 'new_m_conv_mix_w', 'new_m_sgu_ln_g', 'new_m_sgu_ln_b', 'new_m_sgu_w', 'new_m_sgu_b', 'new_m_w_branch_att', 'new_m_w_branch_conv', 'new_m_w_branch_sgu', 'new_m_w_out', 'new_m_w_ffn_up', 'new_m_conv_ffn_w', 'new_m_w_ffn_down', 'new_v_pre_mix_g', 'new_v_post_mix_g', 'new_v_pre_ffn_g', 'new_v_post_ffn_g', 'new_v_w_in', 'new_v_b_forget', 'new_v_b_gate', 'new_v_conv_mix_w', 'new_v_sgu_ln_g', 'new_v_sgu_ln_b', 'new_v_sgu_w', 'new_v_sgu_b', 'new_v_w_branch_att', 'new_v_w_branch_conv', 'new_v_w_branch_sgu', 'new_v_w_out', 'new_v_w_ffn_up', 'new_v_conv_ffn_w', 'new_v_w_ffn_down']
TWIN_LEAF_KINDS = {'loss': 'loss', 'grad_x': 'grad_x', 'grad_pre_mix_g': 'grad_w', 'grad_post_mix_g': 'grad_w', 'grad_pre_ffn_g': 'grad_w', 'grad_post_ffn_g': 'grad_w', 'grad_w_in': 'grad_w', 'grad_b_forget': 'grad_w', 'grad_b_gate': 'grad_w', 'grad_conv_mix_w': 'grad_w', 'grad_sgu_ln_g': 'grad_w', 'grad_sgu_ln_b': 'grad_w', 'grad_sgu_w': 'grad_w', 'grad_sgu_b': 'grad_w', 'grad_w_branch_att': 'grad_w', 'grad_w_branch_conv': 'grad_w', 'grad_w_branch_sgu': 'grad_w', 'grad_w_out': 'grad_w', 'grad_w_ffn_up': 'grad_w', 'grad_conv_ffn_w': 'grad_w', 'grad_w_ffn_down': 'grad_w', 'delta_pre_mix_g': 'delta_w', 'delta_post_mix_g': 'delta_w', 'delta_pre_ffn_g': 'delta_w', 'delta_post_ffn_g': 'delta_w', 'delta_w_in': 'delta_w', 'delta_b_forget': 'delta_w', 'delta_b_gate': 'delta_w', 'delta_conv_mix_w': 'delta_w', 'delta_sgu_ln_g': 'delta_w', 'delta_sgu_ln_b': 'delta_w', 'delta_sgu_w': 'delta_w', 'delta_sgu_b': 'delta_w', 'delta_w_branch_att': 'delta_w', 'delta_w_branch_conv': 'delta_w', 'delta_w_branch_sgu': 'delta_w', 'delta_w_out': 'delta_w', 'delta_w_ffn_up': 'delta_w', 'delta_conv_ffn_w': 'delta_w', 'delta_w_ffn_down': 'delta_w', 'new_m_pre_mix_g': 'new_m', 'new_m_post_mix_g': 'new_m', 'new_m_pre_ffn_g': 'new_m', 'new_m_post_ffn_g': 'new_m', 'new_m_w_in': 'new_m', 'new_m_b_forget': 'new_m', 'new_m_b_gate': 'new_m', 'new_m_conv_mix_w': 'new_m', 'new_m_sgu_ln_g': 'new_m', 'new_m_sgu_ln_b': 'new_m', 'new_m_sgu_w': 'new_m', 'new_m_sgu_b': 'new_m', 'new_m_w_branch_att': 'new_m', 'new_m_w_branch_conv': 'new_m', 'new_m_w_branch_sgu': 'new_m', 'new_m_w_out': 'new_m', 'new_m_w_ffn_up': 'new_m', 'new_m_conv_ffn_w': 'new_m', 'new_m_w_ffn_down': 'new_m', 'new_v_pre_mix_g': 'new_v', 'new_v_post_mix_g': 'new_v', 'new_v_pre_ffn_g': 'new_v', 'new_v_post_ffn_g': 'new_v', 'new_v_w_in': 'new_v', 'new_v_b_forget': 'new_v', 'new_v_b_gate': 'new_v', 'new_v_conv_mix_w': 'new_v', 'new_v_sgu_ln_g': 'new_v', 'new_v_sgu_ln_b': 'new_v', 'new_v_sgu_w': 'new_v', 'new_v_sgu_b': 'new_v', 'new_v_w_branch_att': 'new_v', 'new_v_w_branch_conv': 'new_v', 'new_v_w_branch_sgu': 'new_v', 'new_v_w_out': 'new_v', 'new_v_w_ffn_up': 'new_v', 'new_v_conv_ffn_w': 'new_v', 'new_v_w_ffn_down': 'new_v'}


def _forward(args):
    return _fwd_reference(*[args[k] for k in FWD_PARAMS])


def _output_shape():
    def fwd():
        inp = _fwd_setup_inputs(0)
        return _fwd_reference(*[inp[k] for k in FWD_PARAMS])
    out = _jax.eval_shape(fwd)
    return out.shape, out.dtype

N_MICROBATCH = 1
ADAM_LR = 0.001
ADAM_B1 = 0.9
ADAM_B2 = 0.999
ADAM_EPS = 1e-08
ADAM_WD = 0.01
ADAM_STEP = 10
PER_EXAMPLE_BATCH_AXIS = {'x': 0, 'loss_target': 0}
SHARED_INPUTS = []
_WEIGHT_DTYPES = {'pre_mix_g': _jnp.float32, 'post_mix_g': _jnp.float32, 'pre_ffn_g': _jnp.float32, 'post_ffn_g': _jnp.float32, 'w_in': _jnp.float32, 'b_forget': _jnp.float32, 'b_gate': _jnp.float32, 'conv_mix_w': _jnp.float32, 'sgu_ln_g': _jnp.float32, 'sgu_ln_b': _jnp.float32, 'sgu_w': _jnp.float32, 'sgu_b': _jnp.float32, 'w_branch_att': _jnp.float32, 'w_branch_conv': _jnp.float32, 'w_branch_sgu': _jnp.float32, 'w_out': _jnp.float32, 'w_ffn_up': _jnp.float32, 'conv_ffn_w': _jnp.float32, 'w_ffn_down': _jnp.float32}
MOMENT_SCALE = {'pre_mix_g': 2.587173e+00, 'post_mix_g': 3.200702e+01, 'pre_ffn_g': 1.866870e+00, 'post_ffn_g': 3.195752e+01, 'w_in': 1.083201e+00, 'b_forget': 2.455333e+00, 'b_gate': 7.890810e-01, 'conv_mix_w': 2.406509e+00, 'sgu_ln_g': 1.015676e+00, 'sgu_ln_b': 9.795623e-01, 'sgu_w': 7.002740e-01, 'sgu_b': 1.019122e+00, 'w_branch_att': 1.071353e+00, 'w_branch_conv': 1.146362e+00, 'w_branch_sgu': 3.077623e+00, 'w_out': 3.284489e+00, 'w_ffn_up': 8.005991e-01, 'conv_ffn_w': 8.714706e-01, 'w_ffn_down': 1.513605e+00}


def _to_microbatches(a, axis):
    t = _jnp.moveaxis(a, axis, 0)
    t = t.reshape((N_MICROBATCH, t.shape[0] // N_MICROBATCH) + t.shape[1:])
    return _jnp.moveaxis(t, 1, axis + 1)


def setup_inputs(seed: int = 0) -> dict:
    inp = _fwd_setup_inputs(seed)
    key = _jax.random.fold_in(_jax.random.key(seed), 7919)
    shape, _ = _output_shape()
    out = dict(inp)
    out["loss_target"] = _jax.random.normal(_jax.random.fold_in(key, 0), shape, _jnp.float32)
    for i, name in enumerate(TWIN_WEIGHTS):
        w = inp[name].astype(_jnp.float32)
        if MOMENT_SCALE is None:
            s = _jnp.sqrt(_jnp.mean(_jnp.square(w)) + 1e-30)
        else:
            s = MOMENT_SCALE[name]
        km, kv = _jax.random.split(_jax.random.fold_in(key, i + 1))
        out[name] = w
        out["m_" + name] = s * _jax.random.normal(km, w.shape, _jnp.float32)
        out["v_" + name] = (s * s) * _jax.random.uniform(kv, w.shape, _jnp.float32, 0.5, 1.5)
    if N_MICROBATCH > 1:
        for name, axis in PER_EXAMPLE_BATCH_AXIS.items():
            out[name] = _to_microbatches(out[name], axis)
    return {'x': out['x'], 'pre_mix_g': out['pre_mix_g'], 'post_mix_g': out['post_mix_g'], 'pre_ffn_g': out['pre_ffn_g'], 'post_ffn_g': out['post_ffn_g'], 'w_in': out['w_in'], 'b_forget': out['b_forget'], 'b_gate': out['b_gate'], 'conv_mix_w': out['conv_mix_w'], 'sgu_ln_g': out['sgu_ln_g'], 'sgu_ln_b': out['sgu_ln_b'], 'sgu_w': out['sgu_w'], 'sgu_b': out['sgu_b'], 'w_branch_att': out['w_branch_att'], 'w_branch_conv': out['w_branch_conv'], 'w_branch_sgu': out['w_branch_sgu'], 'w_out': out['w_out'], 'w_ffn_up': out['w_ffn_up'], 'conv_ffn_w': out['conv_ffn_w'], 'w_ffn_down': out['w_ffn_down'], 'loss_target': out['loss_target'], 'm_pre_mix_g': out['m_pre_mix_g'], 'm_post_mix_g': out['m_post_mix_g'], 'm_pre_ffn_g': out['m_pre_ffn_g'], 'm_post_ffn_g': out['m_post_ffn_g'], 'm_w_in': out['m_w_in'], 'm_b_forget': out['m_b_forget'], 'm_b_gate': out['m_b_gate'], 'm_conv_mix_w': out['m_conv_mix_w'], 'm_sgu_ln_g': out['m_sgu_ln_g'], 'm_sgu_ln_b': out['m_sgu_ln_b'], 'm_sgu_w': out['m_sgu_w'], 'm_sgu_b': out['m_sgu_b'], 'm_w_branch_att': out['m_w_branch_att'], 'm_w_branch_conv': out['m_w_branch_conv'], 'm_w_branch_sgu': out['m_w_branch_sgu'], 'm_w_out': out['m_w_out'], 'm_w_ffn_up': out['m_w_ffn_up'], 'm_conv_ffn_w': out['m_conv_ffn_w'], 'm_w_ffn_down': out['m_w_ffn_down'], 'v_pre_mix_g': out['v_pre_mix_g'], 'v_post_mix_g': out['v_post_mix_g'], 'v_pre_ffn_g': out['v_pre_ffn_g'], 'v_post_ffn_g': out['v_post_ffn_g'], 'v_w_in': out['v_w_in'], 'v_b_forget': out['v_b_forget'], 'v_b_gate': out['v_b_gate'], 'v_conv_mix_w': out['v_conv_mix_w'], 'v_sgu_ln_g': out['v_sgu_ln_g'], 'v_sgu_ln_b': out['v_sgu_ln_b'], 'v_sgu_w': out['v_sgu_w'], 'v_sgu_b': out['v_sgu_b'], 'v_w_branch_att': out['v_w_branch_att'], 'v_w_branch_conv': out['v_w_branch_conv'], 'v_w_branch_sgu': out['v_w_branch_sgu'], 'v_w_out': out['v_w_out'], 'v_w_ffn_up': out['v_w_ffn_up'], 'v_conv_ffn_w': out['v_conv_ffn_w'], 'v_w_ffn_down': out['v_w_ffn_down']}


def _loss(weights, diff, rest, loss_target):
    with _jax.named_scope("forward"):
        args = {**rest, TWIN_DIFF_INPUT: diff, **{k: w.astype(_WEIGHT_DTYPES[k]) for k, w in weights.items()}}
        y = _forward(args)
    with _jax.named_scope("loss_head"):
        err = _jnp.square(y.astype(_jnp.float32) - loss_target)
        return 0.5 * _jnp.sum(_jnp.mean(err, axis=-1)) if err.ndim else 0.5 * err


def _adamw(w, g, m, v):
    m = ADAM_B1 * m + (1.0 - ADAM_B1) * g
    v = ADAM_B2 * v + (1.0 - ADAM_B2) * _jnp.square(g)
    m_hat = m / (1.0 - ADAM_B1 ** ADAM_STEP)
    v_hat = v / (1.0 - ADAM_B2 ** ADAM_STEP)
    delta = -ADAM_LR * (m_hat / (_jnp.sqrt(v_hat) + ADAM_EPS) + ADAM_WD * w)
    return delta, m, v


def reference(x, pre_mix_g, post_mix_g, pre_ffn_g, post_ffn_g, w_in, b_forget, b_gate, conv_mix_w, sgu_ln_g, sgu_ln_b, sgu_w, sgu_b, w_branch_att, w_branch_conv, w_branch_sgu, w_out, w_ffn_up, conv_ffn_w, w_ffn_down, loss_target, m_pre_mix_g, m_post_mix_g, m_pre_ffn_g, m_post_ffn_g, m_w_in, m_b_forget, m_b_gate, m_conv_mix_w, m_sgu_ln_g, m_sgu_ln_b, m_sgu_w, m_sgu_b, m_w_branch_att, m_w_branch_conv, m_w_branch_sgu, m_w_out, m_w_ffn_up, m_conv_ffn_w, m_w_ffn_down, v_pre_mix_g, v_post_mix_g, v_pre_ffn_g, v_post_ffn_g, v_w_in, v_b_forget, v_b_gate, v_conv_mix_w, v_sgu_ln_g, v_sgu_ln_b, v_sgu_w, v_sgu_b, v_w_branch_att, v_w_branch_conv, v_w_branch_sgu, v_w_out, v_w_ffn_up, v_conv_ffn_w, v_w_ffn_down):
    given = dict(x=x, pre_mix_g=pre_mix_g, post_mix_g=post_mix_g, pre_ffn_g=pre_ffn_g, post_ffn_g=post_ffn_g, w_in=w_in, b_forget=b_forget, b_gate=b_gate, conv_mix_w=conv_mix_w, sgu_ln_g=sgu_ln_g, sgu_ln_b=sgu_ln_b, sgu_w=sgu_w, sgu_b=sgu_b, w_branch_att=w_branch_att, w_branch_conv=w_branch_conv, w_branch_sgu=w_branch_sgu, w_out=w_out, w_ffn_up=w_ffn_up, conv_ffn_w=conv_ffn_w, w_ffn_down=w_ffn_down, loss_target=loss_target, m_pre_mix_g=m_pre_mix_g, m_post_mix_g=m_post_mix_g, m_pre_ffn_g=m_pre_ffn_g, m_post_ffn_g=m_post_ffn_g, m_w_in=m_w_in, m_b_forget=m_b_forget, m_b_gate=m_b_gate, m_conv_mix_w=m_conv_mix_w, m_sgu_ln_g=m_sgu_ln_g, m_sgu_ln_b=m_sgu_ln_b, m_sgu_w=m_sgu_w, m_sgu_b=m_sgu_b, m_w_branch_att=m_w_branch_att, m_w_branch_conv=m_w_branch_conv, m_w_branch_sgu=m_w_branch_sgu, m_w_out=m_w_out, m_w_ffn_up=m_w_ffn_up, m_conv_ffn_w=m_conv_ffn_w, m_w_ffn_down=m_w_ffn_down, v_pre_mix_g=v_pre_mix_g, v_post_mix_g=v_post_mix_g, v_pre_ffn_g=v_pre_ffn_g, v_post_ffn_g=v_post_ffn_g, v_w_in=v_w_in, v_b_forget=v_b_forget, v_b_gate=v_b_gate, v_conv_mix_w=v_conv_mix_w, v_sgu_ln_g=v_sgu_ln_g, v_sgu_ln_b=v_sgu_ln_b, v_sgu_w=v_sgu_w, v_sgu_b=v_sgu_b, v_w_branch_att=v_w_branch_att, v_w_branch_conv=v_w_branch_conv, v_w_branch_sgu=v_w_branch_sgu, v_w_out=v_w_out, v_w_ffn_up=v_w_ffn_up, v_conv_ffn_w=v_conv_ffn_w, v_w_ffn_down=v_w_ffn_down)
    weights = {n: given[n] for n in TWIN_WEIGHTS}
    shared = {n: given[n] for n in SHARED_INPUTS}
    per_example = {n: given[n] for n in ['x']}
    grad_fn = _jax.value_and_grad(_loss, argnums=(0, 1))

    def one_microbatch(ex, loss_target):
        ex = dict(ex)
        diff = ex.pop(TWIN_DIFF_INPUT)
        return grad_fn(weights, diff, {**shared, **ex}, loss_target)

    if N_MICROBATCH == 1:
        loss, (grad_w, grad_x) = one_microbatch(per_example, given["loss_target"])
    else:
        def body(carry, xs):
            loss_sum, grad_sum = carry
            l_k, (gw_k, gx_k) = one_microbatch(xs[0], xs[1])
            with _jax.named_scope("update"):
                return (loss_sum + l_k, _jax.tree.map(_jnp.add, grad_sum, gw_k)), gx_k

        init = (_jnp.zeros((), _jnp.float32), _jax.tree.map(_jnp.zeros_like, weights))
        (loss, grad_w), grad_x = _jax.lax.scan(body, init, (per_example, given["loss_target"]))
    with _jax.named_scope("update"):
        delta_w, new_m, new_v = {}, {}, {}
        for n in TWIN_WEIGHTS:
            delta_w[n], new_m[n], new_v[n] = _adamw(weights[n], grad_w[n], given["m_" + n], given["v_" + n])
    return (loss, grad_x, *[grad_w[n] for n in TWIN_WEIGHTS], *[delta_w[n] for n in TWIN_WEIGHTS],
            *[new_m[n] for n in TWIN_WEIGHTS], *[new_v[n] for n in TWIN_WEIGHTS])
```

```python
import functools
import math

import jax
import jax.numpy as jnp
from jax import lax
from jax.experimental import pallas as pl
from jax.experimental.pallas import tpu as pltpu

F32 = jnp.float32
BF16 = jnp.bfloat16

N_DEV = 8
HEAD_DIM = 64
N_HEADS = 8
D_ATT = 512
D_CONV = 256
D_SGU = 256
N_SGU_GROUPS = 4
SGU_CHUNK = 128
RMS_EPS = 1e-6
LN_EPS = 1e-5
ADAM_LR = 0.001
ADAM_B1 = 0.9
ADAM_B2 = 0.999
ADAM_EPS = 1e-08
ADAM_WD = 0.01
ADAM_STEP = 10
LANES = 128
VMEM_LIMIT = 56 * 1024 * 1024
NEG = -1e30
MESH = pl.DeviceIdType.MESH


def _params(*sem):
    return pltpu.CompilerParams(dimension_semantics=sem if sem else None, vmem_limit_bytes=VMEM_LIMIT)


def _tile(n, cap):
    if n <= cap:
        return n
    t = cap - cap % LANES
    while n % t:
        t -= LANES
    return t


def _gelu(x):
    return 0.5 * x * (1.0 + jnp.tanh(math.sqrt(2.0 / math.pi) * (x + 0.044715 * (x * x * x))))


def _rms(x, g):
    r = lax.rsqrt(jnp.mean(x * x, axis=-1, keepdims=True) + RMS_EPS)
    return x * r * g


def _layer_norm(x, g, b):
    mu = jnp.mean(x, axis=-1, keepdims=True)
    xc = x - mu
    var = jnp.mean(xc * xc, axis=-1, keepdims=True)
    return xc * lax.rsqrt(var + LN_EPS) * g + b


def _shift_down(x, k, rows):
    return jnp.where(rows >= k, pltpu.roll(x, k, 0), 0.0)


def _shift_up(x, k, rows):
    s = x.shape[0]
    return jnp.where(rows < s - k, pltpu.roll(x, s - k, 0), 0.0)


def _conv3(x, w_ref, rows):
    return w_ref[2:3, :] * x + w_ref[1:2, :] * _shift_down(x, 1, rows) + w_ref[0:1, :] * _shift_down(x, 2, rows)


def _conv3_t(dy, w_ref, rows):
    return w_ref[2:3, :] * dy + w_ref[1:2, :] * _shift_up(dy, 1, rows) + w_ref[0:1, :] * _shift_up(dy, 2, rows)


def _conv3_dw(dy, x, rows):
    d2 = jnp.sum(dy * x, axis=0, keepdims=True)
    d1 = jnp.sum(dy * _shift_down(x, 1, rows), axis=0, keepdims=True)
    d0 = jnp.sum(dy * _shift_down(x, 2, rows), axis=0, keepdims=True)
    sub = lax.broadcasted_iota(jnp.int32, (8, x.shape[1]), 0)
    return jnp.where(sub == 0, d0, jnp.where(sub == 1, d1, jnp.where(sub == 2, d2, 0.0)))


def _mm(a, b, form, out_dtype, name):
    if form == "nn":
        (m, k), n = a.shape, b.shape[1]
    elif form == "nt":
        (m, k), n = a.shape, b.shape[0]
    else:
        (k, m), n = a.shape, b.shape[1]
    tm, tn = _tile(m, 512), _tile(n, 512)
    dims = {"nn": (((1,), (0,)), ((), ())), "nt": (((1,), (1,)), ((), ())), "tn": (((0,), (0,)), ((), ()))}[form]

    def body(a_ref, b_ref, o_ref):
        o_ref[...] = lax.dot_general(a_ref[...], b_ref[...], dims, preferred_element_type=F32).astype(o_ref.dtype)

    a_spec = pl.BlockSpec((k, tm), lambda i, j: (0, i)) if form == "tn" else pl.BlockSpec((tm, k), lambda i, j: (i, 0))
    b_spec = pl.BlockSpec((tn, k), lambda i, j: (j, 0)) if form == "nt" else pl.BlockSpec((k, tn), lambda i, j: (0, j))
    return pl.pallas_call(
        body, name=name, grid=(m // tm, n // tn),
        in_specs=[a_spec, b_spec], out_specs=pl.BlockSpec((tm, tn), lambda i, j: (i, j)),
        out_shape=jax.ShapeDtypeStruct((m, n), out_dtype),
        compiler_params=_params("parallel", "arbitrary"),
    )(a, b)


def _prenorm(x, g, name):
    s, d = x.shape
    tm = _tile(s, 512)

    def body(x_ref, g_ref, o_ref):
        o_ref[...] = _rms(x_ref[...], g_ref[...]).astype(BF16)

    return pl.pallas_call(
        body, name=name, grid=(s // tm,),
        in_specs=[pl.BlockSpec((tm, d), lambda i: (i, 0)), pl.BlockSpec((1, d), lambda i: (0, 0))],
        out_specs=pl.BlockSpec((tm, d), lambda i: (i, 0)),
        out_shape=jax.ShapeDtypeStruct((s, d), BF16), compiler_params=_params("parallel"),
    )(x, g)


def _postnorm(x, o, g_post, g_next, name):
    s, d = x.shape
    tm = _tile(s, 512)

    def body(x_ref, o_ref, gp_ref, gn_ref, x1_ref, xn_ref):
        x1 = x_ref[...] + _rms(o_ref[...], gp_ref[...])
        x1_ref[...] = x1
        xn_ref[...] = _rms(x1, gn_ref[...]).astype(BF16)

    row = pl.BlockSpec((tm, d), lambda i: (i, 0))
    vec = pl.BlockSpec((1, d), lambda i: (0, 0))
    return pl.pallas_call(
        body, name=name, grid=(s // tm,), in_specs=[row, row, vec, vec], out_specs=[row, row],
        out_shape=[jax.ShapeDtypeStruct((s, d), F32), jax.ShapeDtypeStruct((s, d), BF16)],
        compiler_params=_params("parallel"),
    )(x, o, g_post, g_next)


def _postnorm_loss(x, o, g_post, target, name):
    s, d = x.shape
    tm = _tile(s, 512)

    def body(x_ref, o_ref, gp_ref, t_ref, dy_ref, acc_ref):
        e = x_ref[...] + _rms(o_ref[...], gp_ref[...]) - t_ref[...]
        dy_ref[...] = e / d

        @pl.when(pl.program_id(0) == 0)
        def _():
            acc_ref[...] = jnp.zeros_like(acc_ref)

        acc_ref[...] += jnp.sum(jnp.sum(e * e, axis=1, keepdims=True), axis=0, keepdims=True)

    row = pl.BlockSpec((tm, d), lambda i: (i, 0))
    return pl.pallas_call(
        body, name=name, grid=(s // tm,),
        in_specs=[row, row, pl.BlockSpec((1, d), lambda i: (0, 0)), row],
        out_specs=[row, pl.BlockSpec((1, LANES), lambda i: (0, 0))],
        out_shape=[jax.ShapeDtypeStruct((s, d), F32), jax.ShapeDtypeStruct((1, LANES), F32)],
        compiler_params=_params("arbitrary"),
    )(x, o, g_post, target)


def _postnorm_bwd(o, g, dx, name):
    s, d = o.shape
    tm = _tile(s, 512)

    def body(o_ref, g_ref, dx_ref, do_ref, dg_ref):
        _, vjp = jax.vjp(_rms, o_ref[...], g_ref[...])
        d_o, dg = vjp(dx_ref[...])
        do_ref[...] = d_o.astype(BF16)

        @pl.when(pl.program_id(0) == 0)
        def _():
            dg_ref[...] = jnp.zeros_like(dg_ref)

        dg_ref[...] += dg

    row = pl.BlockSpec((tm, d), lambda i: (i, 0))
    vec = pl.BlockSpec((1, d), lambda i: (0, 0))
    return pl.pallas_call(
        body, name=name, grid=(s // tm,), in_specs=[row, vec, row], out_specs=[row, vec],
        out_shape=[jax.ShapeDtypeStruct((s, d), BF16), jax.ShapeDtypeStruct((1, d), F32)],
        compiler_params=_params("arbitrary"),
    )(o, g, dx)


def _prenorm_bwd(x, g, dxn, dres, name):
    s, d = x.shape
    tm = _tile(s, 512)

    def body(x_ref, g_ref, dxn_ref, dres_ref, dx_ref, dg_ref):
        _, vjp = jax.vjp(_rms, x_ref[...], g_ref[...])
        dx, dg = vjp(dxn_ref[...])
        dx_ref[...] = dres_ref[...] + dx

        @pl.when(pl.program_id(0) == 0)
        def _():
            dg_ref[...] = jnp.zeros_like(dg_ref)

        dg_ref[...] += dg

    row = pl.BlockSpec((tm, d), lambda i: (i, 0))
    vec = pl.BlockSpec((1, d), lambda i: (0, 0))
    return pl.pallas_call(
        body, name=name, grid=(s // tm,), in_specs=[row, vec, row, row], out_specs=[row, vec],
        out_shape=[jax.ShapeDtypeStruct((s, d), F32), jax.ShapeDtypeStruct((1, d), F32)],
        compiler_params=_params("arbitrary"),
    )(x, g, dxn, dres)


def _log_sigmoid(z):
    return jnp.minimum(z, 0.0) - jnp.log(1.0 + jnp.exp(-jnp.abs(z)))


def _forget_prep(h, bf_pad, fblk, name):
    s = h.shape[0]

    def body(f_ref, b_ref, c_ref, ct_ref):
        c = _log_sigmoid(f_ref[...] + b_ref[...])
        rows = lax.broadcasted_iota(jnp.int32, c.shape, 0)
        k = 1
        while k < s:
            c = c + _shift_down(c, k, rows)
            k *= 2
        c_ref[...] = c
        ct_ref[...] = jnp.transpose(c)[0:8, :]

    return pl.pallas_call(
        body, name=name, grid=(1,),
        in_specs=[pl.BlockSpec((s, LANES), lambda i: (0, fblk)), pl.BlockSpec((1, LANES), lambda i: (0, 0))],
        out_specs=[pl.BlockSpec((s, LANES), lambda i: (0, 0)), pl.BlockSpec((8, s), lambda i: (0, 0))],
        out_shape=[jax.ShapeDtypeStruct((s, LANES), F32), jax.ShapeDtypeStruct((8, s), F32)],
        compiler_params=_params("arbitrary"),
    )(h, bf_pad)


def _forget_prep_bwd(h, bf_pad, dct, dcq, fblk, name):
    s = h.shape[0]

    def body(f_ref, b_ref, dct_ref, dcq_ref, df_ref, db_ref):
        dc = dcq_ref[...] + jnp.transpose(jnp.concatenate([dct_ref[...], jnp.zeros((LANES - 8, s), F32)], axis=0))
        rows = lax.broadcasted_iota(jnp.int32, dc.shape, 0)
        k = 1
        while k < s:
            dc = dc + _shift_up(dc, k, rows)
            k *= 2
        z = f_ref[...] + b_ref[...]
        lane = lax.broadcasted_iota(jnp.int32, dc.shape, 1)
        df = jnp.where(lane < N_HEADS, dc * jax.nn.sigmoid(-z), 0.0)
        df_ref[...] = df
        db_ref[...] = jnp.sum(df, axis=0, keepdims=True)

    return pl.pallas_call(
        body, name=name, grid=(1,),
        in_specs=[pl.BlockSpec((s, LANES), lambda i: (0, fblk)), pl.BlockSpec((1, LANES), lambda i: (0, 0)),
                  pl.BlockSpec((8, s), lambda i: (0, 0)), pl.BlockSpec((s, LANES), lambda i: (0, 0))],
        out_specs=[pl.BlockSpec((s, LANES), lambda i: (0, 0)), pl.BlockSpec((1, LANES), lambda i: (0, 0))],
        out_shape=[jax.ShapeDtypeStruct((s, LANES), F32), jax.ShapeDtypeStruct((1, LANES), F32)],
        compiler_params=_params("arbitrary"),
    )(h, bf_pad, dct, dcq)


def _pick_lane(blk, idx):
    lane = lax.broadcasted_iota(jnp.int32, blk.shape, 1)
    return jnp.sum(jnp.where(lane == idx, blk, 0.0), axis=1, keepdims=True)


def _pick_row(blk, idx):
    sub = lax.broadcasted_iota(jnp.int32, blk.shape, 0)
    return jnp.sum(jnp.where(sub == idx, blk, 0.0), axis=0, keepdims=True)


def _attention_fwd(h, c, ct, qblk, name):
    s = h.shape[0]
    t = _tile(s, 512)
    nq = s // t
    scale = HEAD_DIM ** -0.5
    nt_dims = (((1,), (1,)), ((), ()))

    def body(q_ref, k_ref, v_ref, c_ref, ct_ref, o_ref, lse_ref):
        p = pl.program_id(0)
        i = pl.program_id(1)
        lane = lax.broadcasted_iota(jnp.int32, (1, LANES), 1)
        first = lane < HEAD_DIM
        q = q_ref[...] * scale
        qa = jnp.where(first, q, 0.0).astype(BF16)
        qb = jnp.where(first, 0.0, q).astype(BF16)
        cblk = c_ref[...]
        cta = _pick_lane(cblk, 2 * p)
        ctb = _pick_lane(cblk, 2 * p + 1)
        rid = lax.broadcasted_iota(jnp.int32, (t, t), 0) + i * t
        cid0 = lax.broadcasted_iota(jnp.int32, (t, t), 1)

        def step(j, carry):
            ma, la, mb, lb, acc = carry
            off = pl.multiple_of(j * t, t)
            k = k_ref[pl.ds(off, t), :].astype(BF16)
            v = v_ref[pl.ds(off, t), :].astype(BF16)
            crow = ct_ref[:, pl.ds(off, t)]
            keep = rid >= cid0 + j * t

            def one(qh, cth, hd, m_old, l_old):
                sc = lax.dot_general(qh, k, nt_dims, preferred_element_type=F32) + (cth - _pick_row(crow, hd))
                sc = jnp.where(keep, sc, NEG)
                m_new = jnp.maximum(m_old, jnp.max(sc, axis=1, keepdims=True))
                pr = jnp.exp(sc - m_new)
                alpha = jnp.exp(m_old - m_new)
                l_new = alpha * l_old + jnp.sum(pr, axis=1, keepdims=True)
                pv = jnp.dot(pr.astype(BF16), v, preferred_element_type=F32)
                return m_new, l_new, alpha, pv

            ma2, la2, aa, pva = one(qa, cta, 2 * p, ma, la)
            mb2, lb2, ab, pvb = one(qb, ctb, 2 * p + 1, mb, lb)
            acc = jnp.where(first, aa * acc + pva, ab * acc + pvb)
            return ma2, la2, mb2, lb2, acc

        init = (jnp.full((t, 1), NEG, F32), jnp.zeros((t, 1), F32), jnp.full((t, 1), NEG, F32),
                jnp.zeros((t, 1), F32), jnp.zeros((t, LANES), F32))
        ma, la, mb, lb, acc = lax.fori_loop(0, i + 1, step, init)
        o_ref[...] = (acc / jnp.where(first, la, lb)).astype(BF16)
        lse_ref[0] = jnp.broadcast_to(ma + jnp.log(la), (t, LANES))
        lse_ref[1] = jnp.broadcast_to(mb + jnp.log(lb), (t, LANES))

    return pl.pallas_call(
        body, name=name, grid=(N_HEADS // 2, nq),
        in_specs=[pl.BlockSpec((t, LANES), lambda p, i: (i, qblk + p)),
                  pl.BlockSpec((s, LANES), lambda p, i: (0, qblk + 4 + p)),
                  pl.BlockSpec((s, LANES), lambda p, i: (0, qblk + 8 + p)),
                  pl.BlockSpec((t, LANES), lambda p, i: (i, 0)),
                  pl.BlockSpec((8, s), lambda p, i: (0, 0))],
        out_specs=[pl.BlockSpec((t, LANES), lambda p, i: (i, p)),
                   pl.BlockSpec((2, t, LANES), lambda p, i: (p, i, 0))],
        out_shape=[jax.ShapeDtypeStruct((s, D_ATT), BF16), jax.ShapeDtypeStruct((N_HEADS, s, LANES), F32)],
        compiler_params=_params("parallel", "arbitrary"),
    )(h, h, h, c, ct)


def _attention_bwd(h, c, ct, lse, att, datt, qblk, name):
    s = h.shape[0]
    t = _tile(s, 512)
    nq = s // t
    scale = HEAD_DIM ** -0.5
    nt_dims = (((1,), (1,)), ((), ()))
    tn_dims = (((0,), (0,)), ((), ()))

    def body(q_ref, k_ref, v_ref, c_ref, ct_ref, lse_ref, o_ref, do_ref, dq_ref, dk_ref, dv_ref, dct_ref, dcq_ref):
        p = pl.program_id(0)
        j = pl.program_id(1)
        lane = lax.broadcasted_iota(jnp.int32, (1, LANES), 1)
        first = lane < HEAD_DIM
        kf = k_ref[...]
        vf = v_ref[...]
        k = kf.astype(BF16)
        ka = jnp.where(first, kf, 0.0).astype(BF16)
        kb = jnp.where(first, 0.0, kf).astype(BF16)
        va = jnp.where(first, vf, 0.0).astype(BF16)
        vb = jnp.where(first, 0.0, vf).astype(BF16)
        crow = ct_ref[...]
        csa = _pick_row(crow, 2 * p)
        csb = _pick_row(crow, 2 * p + 1)
        rid0 = lax.broadcasted_iota(jnp.int32, (t, t), 0)
        cid = lax.broadcasted_iota(jnp.int32, (t, t), 1) + j * t

        @pl.when(j == 0)
        def _():
            dq_ref[...] = jnp.zeros_like(dq_ref)
            dcq_ref[...] = jnp.zeros_like(dcq_ref)

        def step(i, carry):
            dka, dkb, dva, dvb, dca, dcb = carry
            off = pl.multiple_of(i * t, t)
            rows = pl.ds(off, t)
            q = (q_ref[rows, :] * scale).astype(BF16)
            dof = do_ref[rows, :]
            do = dof.astype(BF16)
            prod = dof * o_ref[rows, :].astype(F32)
            cblk = c_ref[rows, :]
            keep = rid0 + i * t >= cid

            def one(kh, vh, hd, csh, lse_h):
                sc = lax.dot_general(q, kh, nt_dims, preferred_element_type=F32)
                sc = sc + (_pick_lane(cblk, hd) - csh) - jnp.max(lse_h, axis=1, keepdims=True)
                pr = jnp.exp(jnp.where(keep, sc, NEG))
                dp = lax.dot_general(do, vh, nt_dims, preferred_element_type=F32)
                return pr, dp

            pra, dpa = one(ka, va, 2 * p, csa, lse_ref[0, rows, :])
            prb, dpb = one(kb, vb, 2 * p + 1, csb, lse_ref[1, rows, :])
            dela = jnp.sum(jnp.where(first, prod, 0.0), axis=1, keepdims=True)
            delb = jnp.sum(jnp.where(first, 0.0, prod), axis=1, keepdims=True)
            dsa = pra * (dpa - dela)
            dsb = prb * (dpb - delb)
            dsa16 = dsa.astype(BF16)
            dsb16 = dsb.astype(BF16)
            dva = dva + lax.dot_general(pra.astype(BF16), do, tn_dims, preferred_element_type=F32)
            dvb = dvb + lax.dot_general(prb.astype(BF16), do, tn_dims, preferred_element_type=F32)
            dka = dka + lax.dot_general(dsa16, q, tn_dims, preferred_element_type=F32)
            dkb = dkb + lax.dot_general(dsb16, q, tn_dims, preferred_element_type=F32)
            dqa = jnp.dot(dsa16, k, preferred_element_type=F32)
            dqb = jnp.dot(dsb16, k, preferred_element_type=F32)
            dq_ref[rows, :] += scale * jnp.where(first, dqa, dqb)
            dca = dca - jnp.sum(dsa, axis=0, keepdims=True)
            dcb = dcb - jnp.sum(dsb, axis=0, keepdims=True)
            dcq_ref[rows, :] += jnp.where(lane == 0, jnp.sum(dsa, axis=1, keepdims=True),
                                          jnp.where(lane == 1, jnp.sum(dsb, axis=1, keepdims=True), 0.0))
            return dka, dkb, dva, dvb, dca, dcb

        z = jnp.zeros((t, LANES), F32)
        zr = jnp.zeros((1, t), F32)
        dka, dkb, dva, dvb, dca, dcb = lax.fori_loop(j, nq, step, (z, z, z, z, zr, zr))
        dk_ref[...] = jnp.where(first, dka, dkb)
        dv_ref[...] = jnp.where(first, dva, dvb)
        sub = lax.broadcasted_iota(jnp.int32, (8, t), 0)
        dct_ref[...] = jnp.where(sub == 0, dca, jnp.where(sub == 1, dcb, 0.0))

    full = lambda blk: pl.BlockSpec((s, LANES), blk)
    return pl.pallas_call(
        body, name=name, grid=(N_HEADS // 2, nq),
        in_specs=[full(lambda p, j: (0, qblk + p)),
                  pl.BlockSpec((t, LANES), lambda p, j: (j, qblk + 4 + p)),
                  pl.BlockSpec((t, LANES), lambda p, j: (j, qblk + 8 + p)),
                  full(lambda p, j: (0, 0)),
                  pl.BlockSpec((8, t), lambda p, j: (0, j)),
                  pl.BlockSpec((2, s, LANES), lambda p, j: (p, 0, 0)),
                  full(lambda p, j: (0, p)),
                  full(lambda p, j: (0, p))],
        out_specs=[full(lambda p, j: (0, p)),
                   pl.BlockSpec((t, LANES), lambda p, j: (j, p)),
                   pl.BlockSpec((t, LANES), lambda p, j: (j, p)),
                   pl.BlockSpec((None, 8, t), lambda p, j: (p, 0, j)),
                   pl.BlockSpec((None, s, LANES), lambda p, j: (p, 0, 0))],
        out_shape=[jax.ShapeDtypeStruct((s, D_ATT), F32), jax.ShapeDtypeStruct((s, D_ATT), F32),
                   jax.ShapeDtypeStruct((s, D_ATT), F32), jax.ShapeDtypeStruct((N_HEADS // 2, 8, s), F32),
                   jax.ShapeDtypeStruct((N_HEADS // 2, s, LANES), F32)],
        compiler_params=_params("arbitrary", "arbitrary"),
    )(h, h, h, c, ct, lse, att, datt)


def _sconv_fwd(h, w, bgblk, name):
    s = h.shape[0]
    nblk = D_CONV // LANES

    def body(bg_ref, cg_ref, hc_ref, w_ref, y_ref):
        rows = lax.broadcasted_iota(jnp.int32, (s, LANES), 0)
        y_ref[...] = (bg_ref[...] * _conv3(cg_ref[...] * hc_ref[...], w_ref, rows)).astype(BF16)

    col = lambda base: pl.BlockSpec((s, LANES), lambda j: (0, base + j))
    return pl.pallas_call(
        body, name=name, grid=(nblk,),
        in_specs=[col(bgblk), col(bgblk + nblk), col(bgblk + 2 * nblk), pl.BlockSpec((3, LANES), lambda j: (0, j))],
        out_specs=pl.BlockSpec((s, LANES), lambda j: (0, j)),
        out_shape=jax.ShapeDtypeStruct((s, D_CONV), BF16), compiler_params=_params("parallel"),
    )(h, h, h, w)


def _sconv_bwd(h, w, dy, bgblk, name):
    s = h.shape[0]
    nblk = D_CONV // LANES

    def body(bg_ref, cg_ref, hc_ref, w_ref, dy_ref, dbg_ref, dcg_ref, dhc_ref, dw_ref):
        rows = lax.broadcasted_iota(jnp.int32, (s, LANES), 0)
        cg, hc, dy, w = cg_ref[...], hc_ref[...], dy_ref[...], w_ref
        xin = cg * hc
        dbg_ref[...] = (dy * _conv3(xin, w, rows)).astype(BF16)
        dconv = dy * bg_ref[...]
        dxin = _conv3_t(dconv, w, rows)
        dcg_ref[...] = (dxin * hc).astype(BF16)
        dhc_ref[...] = (dxin * cg).astype(BF16)
        dw_ref[...] = _conv3_dw(dconv, xin, rows)

    col = lambda base: pl.BlockSpec((s, LANES), lambda j: (0, base + j))
    return pl.pallas_call(
        body, name=name, grid=(nblk,),
        in_specs=[col(bgblk), col(bgblk + nblk), col(bgblk + 2 * nblk), pl.BlockSpec((3, LANES), lambda j: (0, j)), col(0)],
        out_specs=[col(0), col(0), col(0), pl.BlockSpec((8, LANES), lambda j: (0, j))],
        out_shape=[jax.ShapeDtypeStruct((s, D_CONV), BF16)] * 3 + [jax.ShapeDtypeStruct((8, D_CONV), F32)],
        compiler_params=_params("parallel"),
    )(h, h, h, w, dy)


def _sgu_group_masks():
    lane = lax.broadcasted_iota(jnp.int32, (1, D_SGU), 1)
    return [(lane // HEAD_DIM) == g for g in range(N_SGU_GROUPS)]


def _sgu_tril():
    r = lax.broadcasted_iota(jnp.int32, (SGU_CHUNK, SGU_CHUNK), 0)
    c = lax.broadcasted_iota(jnp.int32, (SGU_CHUNK, SGU_CHUNK), 1)
    return r >= c


def _sgu_fwd(h, ln_g, ln_b, w_s, b_full, ublk, name):
    s = h.shape[0]
    tr = _tile(s, 512)
    nch = tr // SGU_CHUNK

    def body(u_ref, v_ref, g_ref, b_ref, w_ref, bf_ref, y_ref):
        masks = _sgu_group_masks()
        tril = _sgu_tril()
        wm = [jnp.where(tril, w_ref[g], 0.0).astype(BF16) for g in range(N_SGU_GROUPS)]
        vn = _layer_norm(_gelu(v_ref[...]), g_ref[...], b_ref[...])
        for ch in range(nch):
            rows = pl.ds(ch * SGU_CHUNK, SGU_CHUNK)
            vc = vn[ch * SGU_CHUNK:(ch + 1) * SGU_CHUNK, :]
            mixed = bf_ref[...]
            for g in range(N_SGU_GROUPS):
                mixed = mixed + jnp.dot(wm[g], jnp.where(masks[g], vc, 0.0).astype(BF16), preferred_element_type=F32)
            y_ref[rows, :] = (_gelu(u_ref[rows, :]) * mixed).astype(BF16)

    row = lambda blk: pl.BlockSpec((tr, D_SGU), lambda i: (i, blk))
    vec = pl.BlockSpec((1, D_SGU), lambda i: (0, 0))
    return pl.pallas_call(
        body, name=name, grid=(s // tr,),
        in_specs=[row(ublk), row(ublk + 1), vec, vec,
                  pl.BlockSpec((N_SGU_GROUPS, SGU_CHUNK, SGU_CHUNK), lambda i: (0, 0, 0)),
                  pl.BlockSpec((SGU_CHUNK, D_SGU), lambda i: (0, 0))],
        out_specs=row(0), out_shape=jax.ShapeDtypeStruct((s, D_SGU), BF16), compiler_params=_params("parallel"),
    )(h, h, ln_g, ln_b, w_s, b_full)


def _sgu_bwd(h, ln_g, ln_b, w_s, b_full, dy, ublk, name):
    s = h.shape[0]
    tr = _tile(s, 512)
    nch = tr // SGU_CHUNK
    nt_dims = (((1,), (1,)), ((), ()))

    def norm(v, g, b):
        return _layer_norm(_gelu(v), g, b)

    def body(u_ref, v_ref, g_ref, b_ref, w_ref, bf_ref, dy_ref, du_ref, dv_ref, dg_ref, db_ref, dw_ref, dbf_ref):
        masks = _sgu_group_masks()
        tril = _sgu_tril()
        wf = [jnp.where(tril, w_ref[g], 0.0) for g in range(N_SGU_GROUPS)]
        wm = [w.astype(BF16) for w in wf]
        wmt = [jnp.transpose(w).astype(BF16) for w in wf]
        vn, vjp = jax.vjp(norm, v_ref[...], g_ref[...], b_ref[...])

        @pl.when(pl.program_id(0) == 0)
        def _():
            dg_ref[...] = jnp.zeros_like(dg_ref)
            db_ref[...] = jnp.zeros_like(db_ref)
            dw_ref[...] = jnp.zeros_like(dw_ref)
            dbf_ref[...] = jnp.zeros_like(dbf_ref)

        dvn_parts = []
        for ch in range(nch):
            rows = pl.ds(ch * SGU_CHUNK, SGU_CHUNK)
            vc = vn[ch * SGU_CHUNK:(ch + 1) * SGU_CHUNK, :]
            vc16 = vc.astype(BF16)
            mixed = bf_ref[...]
            for g in range(N_SGU_GROUPS):
                mixed = mixed + jnp.dot(wm[g], jnp.where(masks[g], vc, 0.0).astype(BF16), preferred_element_type=F32)
            dy = dy_ref[rows, :]
            ug, gvjp = jax.vjp(_gelu, u_ref[rows, :])
            du_ref[rows, :] = gvjp(dy * mixed)[0].astype(BF16)
            dmixed = dy * ug
            dbf_ref[...] += dmixed
            dvc = jnp.zeros((SGU_CHUNK, D_SGU), F32)
            for g in range(N_SGU_GROUPS):
                dm16 = jnp.where(masks[g], dmixed, 0.0).astype(BF16)
                dw_ref[g] += jnp.where(tril, lax.dot_general(dm16, vc16, nt_dims, preferred_element_type=F32), 0.0)
                dvc = dvc + jnp.dot(wmt[g], dm16, preferred_element_type=F32)
            dvn_parts.append(dvc)
        dv, dg, db = vjp(jnp.concatenate(dvn_parts, axis=0))
        dv_ref[...] = dv.astype(BF16)
        dg_ref[...] += dg
        db_ref[...] += db

    row = lambda blk: pl.BlockSpec((tr, D_SGU), lambda i: (i, blk))
    vec = pl.BlockSpec((1, D_SGU), lambda i: (0, 0))
    wsp = pl.BlockSpec((N_SGU_GROUPS, SGU_CHUNK, SGU_CHUNK), lambda i: (0, 0, 0))
    bsp = pl.BlockSpec((SGU_CHUNK, D_SGU), lambda i: (0, 0))
    return pl.pallas_call(
        body, name=name, grid=(s // tr,),
        in_specs=[row(ublk), row(ublk + 1), vec, vec, wsp, bsp, row(0)],
        out_specs=[row(0), row(0), vec, vec, wsp, bsp],
        out_shape=[jax.ShapeDtypeStruct((s, D_SGU), BF16), jax.ShapeDtypeStruct((s, D_SGU), BF16),
                   jax.ShapeDtypeStruct((1, D_SGU), F32), jax.ShapeDtypeStruct((1, D_SGU), F32),
                   jax.ShapeDtypeStruct((N_SGU_GROUPS, SGU_CHUNK, SGU_CHUNK), F32),
                   jax.ShapeDtypeStruct((SGU_CHUNK, D_SGU), F32)],
        compiler_params=_params("arbitrary"),
    )(h, h, ln_g, ln_b, w_s, b_full, dy)


def _merge_fwd(h, b_gate, att, yc, ys, wa, wc, ws, name):
    s, d = att.shape[0], wa.shape[1]
    tm, tn = _tile(s, 512), _tile(d, 512)
    nj = d // tn

    def body(g0_ref, g1_ref, g2_ref, bg_ref, a_ref, c_ref, s_ref, wa_ref, wc_ref, ws_ref, o_ref):
        acc = jax.nn.sigmoid(g0_ref[...] + bg_ref[0:1, :]) * jnp.dot(a_ref[...], wa_ref[...], preferred_element_type=F32)
        acc += jax.nn.sigmoid(g1_ref[...] + bg_ref[1:2, :]) * jnp.dot(c_ref[...], wc_ref[...], preferred_element_type=F32)
        acc += jax.nn.sigmoid(g2_ref[...] + bg_ref[2:3, :]) * jnp.dot(s_ref[...], ws_ref[...], preferred_element_type=F32)
        o_ref[...] = acc.astype(BF16)

    gate = lambda b: pl.BlockSpec((tm, tn), lambda j, i: (i, b * nj + j))
    act = lambda k: pl.BlockSpec((tm, k), lambda j, i: (i, 0))
    wgt = lambda k: pl.BlockSpec((k, tn), lambda j, i: (0, j))
    return pl.pallas_call(
        body, name=name, grid=(nj, s // tm),
        in_specs=[gate(0), gate(1), gate(2), pl.BlockSpec((3, tn), lambda j, i: (0, j)),
                  act(D_ATT), act(D_CONV), act(D_SGU), wgt(D_ATT), wgt(D_CONV), wgt(D_SGU)],
        out_specs=pl.BlockSpec((tm, tn), lambda j, i: (i, j)),
        out_shape=jax.ShapeDtypeStruct((s, d), BF16), compiler_params=_params("parallel", "arbitrary"),
    )(h, h, h, b_gate, att, yc, ys, wa, wc, ws)


def _merge_bwd(h, b_gate, att, yc, ys, wa, wc, ws, dm, name):
    s, d = att.shape[0], wa.shape[1]
    tm, tn = _tile(s, 512), _tile(d, 512)
    nj = d // tn

    def body(g0_ref, g1_ref, g2_ref, bg_ref, a_ref, c_ref, s_ref, wa_ref, wc_ref, ws_ref, dm_ref,
             dya_ref, dyc_ref, dys_ref, dgl_ref, dbg_ref):
        dm = dm_ref[...]
        sums = []
        for b, (g_ref, x_ref, w_ref, dy_ref) in enumerate(((g0_ref, a_ref, wa_ref, dya_ref), (g1_ref, c_ref, wc_ref, dyc_ref),
                                                          (g2_ref, s_ref, ws_ref, dys_ref))):
            gate = jax.nn.sigmoid(g_ref[...] + bg_ref[b:b + 1, :])
            y = jnp.dot(x_ref[...], w_ref[...], preferred_element_type=F32)
            dy_ref[...] = (dm * gate).astype(BF16)
            dgl = dm * y * gate * (1.0 - gate)
            dgl_ref[b] = dgl.astype(BF16)
            sums.append(jnp.sum(dgl, axis=0, keepdims=True))
        sub = lax.broadcasted_iota(jnp.int32, (3, tn), 0)
        part = jnp.where(sub == 0, sums[0], jnp.where(sub == 1, sums[1], sums[2]))

        @pl.when(pl.program_id(1) == 0)
        def _():
            dbg_ref[...] = jnp.zeros_like(dbg_ref)

        dbg_ref[...] += part

    gate = lambda b: pl.BlockSpec((tm, tn), lambda j, i: (i, b * nj + j))
    act = lambda k: pl.BlockSpec((tm, k), lambda j, i: (i, 0))
    wgt = lambda k: pl.BlockSpec((k, tn), lambda j, i: (0, j))
    tile = pl.BlockSpec((tm, tn), lambda j, i: (i, j))
    return pl.pallas_call(
        body, name=name, grid=(nj, s // tm),
        in_specs=[gate(0), gate(1), gate(2), pl.BlockSpec((3, tn), lambda j, i: (0, j)),
                  act(D_ATT), act(D_CONV), act(D_SGU), wgt(D_ATT), wgt(D_CONV), wgt(D_SGU), tile],
        out_specs=[tile, tile, tile, pl.BlockSpec((3, tm, tn), lambda j, i: (0, i, j)),
                   pl.BlockSpec((3, tn), lambda j, i: (0, j))],
        out_shape=[jax.ShapeDtypeStruct((s, d), BF16)] * 3 + [jax.ShapeDtypeStruct((3, s, d), BF16),
                                                              jax.ShapeDtypeStruct((3, d), F32)],
        compiler_params=_params("parallel", "arbitrary"),
    )(h, h, h, b_gate, att, yc, ys, wa, wc, ws, dm)


def _ffn_act_fwd(hh, cw, name):
    s, dff = hh.shape[0], hh.shape[1] // 2
    nblk = dff // LANES

    def body(a_ref, b_ref, wa_ref, wb_ref, z_ref):
        rows = lax.broadcasted_iota(jnp.int32, (s, LANES), 0)
        z_ref[...] = (_gelu(_conv3(a_ref[...], wa_ref, rows)) * _conv3(b_ref[...], wb_ref, rows)).astype(BF16)

    col = lambda base: pl.BlockSpec((s, LANES), lambda j: (0, base + j))
    wsp = lambda base: pl.BlockSpec((3, LANES), lambda j: (0, base + j))
    return pl.pallas_call(
        body, name=name, grid=(nblk,), in_specs=[col(0), col(nblk), wsp(0), wsp(nblk)], out_specs=col(0),
        out_shape=jax.ShapeDtypeStruct((s, dff), BF16), compiler_params=_params("parallel"),
    )(hh, hh, cw, cw)


def _ffn_act_bwd(hh, cw, dz, name):
    s, dff = hh.shape[0], hh.shape[1] // 2
    nblk = dff // LANES

    def body(a_ref, b_ref, wa_ref, wb_ref, dz_ref, da_ref, db_ref, dwa_ref, dwb_ref):
        rows = lax.broadcasted_iota(jnp.int32, (s, LANES), 0)
        a, b, wa, wb, dz = a_ref[...], b_ref[...], wa_ref, wb_ref, dz_ref[...]
        ga, gvjp = jax.vjp(_gelu, _conv3(a, wa, rows))
        dca = gvjp(dz * _conv3(b, wb, rows))[0]
        dcb = dz * ga
        da_ref[...] = _conv3_t(dca, wa, rows).astype(BF16)
        db_ref[...] = _conv3_t(dcb, wb, rows).astype(BF16)
        dwa_ref[...] = _conv3_dw(dca, a, rows)
        dwb_ref[...] = _conv3_dw(dcb, b, rows)

    col = lambda base: pl.BlockSpec((s, LANES), lambda j: (0, base + j))
    wsp = lambda base: pl.BlockSpec((3, LANES), lambda j: (0, base + j))
    w8 = lambda base: pl.BlockSpec((8, LANES), lambda j: (0, base + j))
    return pl.pallas_call(
        body, name=name, grid=(nblk,), in_specs=[col(0), col(nblk), wsp(0), wsp(nblk), col(0)],
        out_specs=[col(0), col(0), w8(0), w8(0)],
        out_shape=[jax.ShapeDtypeStruct((s, dff), BF16)] * 2 + [jax.ShapeDtypeStruct((8, dff), F32)] * 2,
        compiler_params=_params("parallel"),
    )(hh, hh, cw, cw, dz)


ANY = pl.BlockSpec(memory_space=pl.ANY)


def _place():
    return lax.axis_index("x"), lax.axis_index("y"), lax.axis_index("c")


def _all_gather(arrs, name):
    n = len(arrs)

    def body(*refs):
        ins, outs = refs[:n], refs[n:2 * n]
        send_sems, recv_sems, local_sems = refs[2 * n:]
        x, y, c = _place()
        me, sibling = (x, y, c), (x, y, 1 - c)
        chips = [(1 - x, y), (x, 1 - y), (1 - x, 1 - y)]

        def slab(a, dev):
            return outs[a].at[4 * dev[0] + 2 * dev[1] + dev[2]]

        def copy(a, k, block, to, src=None):
            return pltpu.make_async_remote_copy(
                src_ref=slab(a, block) if src is None else src, dst_ref=slab(a, block),
                send_sem=send_sems.at[7 * a + k], recv_sem=recv_sems.at[7 * a + k], device_id=to, device_id_type=MESH)

        mine = [pltpu.make_async_copy(ins[a], slab(a, me), local_sems.at[a]) for a in range(n)]
        for cp in mine:
            cp.start()
        first = []
        for a in range(n):
            first.append(copy(a, 0, me, sibling, src=ins[a]))
            first += [copy(a, 1 + j, me, (*chip, c), src=ins[a]) for j, chip in enumerate(chips)]
        for cp in first:
            cp.start()
        passed = []
        for a in range(n):
            for j, chip in enumerate(chips):
                copy(a, 1 + j, (*chip, c), me).wait_recv()
                fwd = copy(a, 4 + j, (*chip, c), sibling)
                fwd.start()
                passed.append(fwd)
        for a in range(n):
            copy(a, 0, sibling, me).wait_recv()
            for j, chip in enumerate(chips):
                copy(a, 4 + j, (*chip, 1 - c), me).wait_recv()
        for cp in first + passed:
            cp.wait_send()
        for cp in mine:
            cp.wait()

    return pl.pallas_call(
        body, name=name, in_specs=[ANY] * n, out_specs=[ANY] * n,
        out_shape=[jax.ShapeDtypeStruct((N_DEV,) + a.shape, a.dtype) for a in arrs],
        scratch_shapes=[pltpu.SemaphoreType.DMA((7 * n,)), pltpu.SemaphoreType.DMA((7 * n,)), pltpu.SemaphoreType.DMA((n,))],
    )(*arrs)


def _all_to_all(arrs, name):
    n = len(arrs)

    def body(*refs):
        ins, outs = refs[:n], refs[n:2 * n]
        send_sems, recv_sems, local_sems = refs[2 * n:]
        x, y, c = _place()
        me = 4 * x + 2 * y + c
        peers = []
        for r in range(1, N_DEV):
            px = 1 - x if r & 4 else x
            py = 1 - y if r & 2 else y
            pc = 1 - c if r & 1 else c
            peers.append((px, py, pc))

        def copy(a, k, peer):
            return pltpu.make_async_remote_copy(
                src_ref=ins[a].at[4 * peer[0] + 2 * peer[1] + peer[2]], dst_ref=outs[a].at[me],
                send_sem=send_sems.at[7 * a + k], recv_sem=recv_sems.at[7 * a + k], device_id=peer, device_id_type=MESH)

        def landed(a, k, peer):
            return pltpu.make_async_remote_copy(
                src_ref=ins[a].at[me], dst_ref=outs[a].at[4 * peer[0] + 2 * peer[1] + peer[2]],
                send_sem=send_sems.at[7 * a + k], recv_sem=recv_sems.at[7 * a + k], device_id=peer, device_id_type=MESH)

        mine = [pltpu.make_async_copy(ins[a].at[me], outs[a].at[me], local_sems.at[a]) for a in range(n)]
        for cp in mine:
            cp.start()
        sends = [copy(a, k, peer) for a in range(n) for k, peer in enumerate(peers)]
        for cp in sends:
            cp.start()
        for a in range(n):
            for k, peer in enumerate(peers):
                landed(a, k, peer).wait_recv()
        for cp in sends:
            cp.wait_send()
        for cp in mine:
            cp.wait()

    return pl.pallas_call(
        body, name=name, in_specs=[ANY] * n, out_specs=[ANY] * n,
        out_shape=[jax.ShapeDtypeStruct(a.shape, a.dtype) for a in arrs],
        scratch_shapes=[pltpu.SemaphoreType.DMA((7 * n,)), pltpu.SemaphoreType.DMA((7 * n,)), pltpu.SemaphoreType.DMA((n,))],
    )(*arrs)


def _adamw(slabs, w, m, v, name):
    nl, r, c = w.shape
    tr = r
    if r * c * 4 > (1 << 19):
        tr = 8
        for cand in range(8, r, 8):
            if r % cand == 0 and cand * c * 4 <= (1 << 19):
                tr = cand

    def body(s_ref, w_ref, m_ref, v_ref, g_ref, d_ref, nm_ref, nv_ref):
        g = s_ref[0].astype(F32)
        for q in range(1, N_DEV):
            g = g + s_ref[q].astype(F32)
        m_new = ADAM_B1 * m_ref[...] + (1.0 - ADAM_B1) * g
        v_new = ADAM_B2 * v_ref[...] + (1.0 - ADAM_B2) * (g * g)
        m_hat = m_new / (1.0 - ADAM_B1 ** ADAM_STEP)
        v_hat = v_new / (1.0 - ADAM_B2 ** ADAM_STEP)
        g_ref[...] = g
        d_ref[...] = -ADAM_LR * (m_hat / (jnp.sqrt(v_hat) + ADAM_EPS) + ADAM_WD * w_ref[...])
        nm_ref[...] = m_new
        nv_ref[...] = v_new

    blk = pl.BlockSpec((None, tr, c), lambda l, i: (l, i, 0))
    return pl.pallas_call(
        body, name=name, grid=(nl, r // tr),
        in_specs=[pl.BlockSpec((N_DEV, None, tr, c), lambda l, i: (0, l, i, 0)), blk, blk, blk],
        out_specs=[blk] * 4, out_shape=[jax.ShapeDtypeStruct(w.shape, F32)] * 4,
        compiler_params=_params("parallel", "parallel"),
    )(slabs, w, m, v)


def _layout(d):
    off = {"gate": 0, "q": 3 * d}
    off["bg"] = off["q"] + 3 * D_ATT
    off["u"] = off["bg"] + 3 * D_CONV
    off["f"] = off["u"] + 2 * D_SGU
    width = -(-(off["f"] + LANES) // 512) * 512
    return off, width


def _pad_w_in(w, d):
    off, width = _layout(d)
    nqkv, nrest = 3 * D_ATT, 3 * D_CONV + 2 * D_SGU
    pad = jnp.zeros((w.shape[0], width - off["f"] - N_HEADS), w.dtype)
    return jnp.concatenate([w[:, nqkv + N_HEADS + nrest:], w[:, :nqkv], w[:, nqkv + N_HEADS:nqkv + N_HEADS + nrest],
                            w[:, nqkv:nqkv + N_HEADS], pad], axis=1)


def _unpad_w_in(wp, d):
    off, _ = _layout(d)
    return jnp.concatenate([wp[:, off["q"]:off["bg"]], wp[:, off["f"]:off["f"] + N_HEADS], wp[:, off["bg"]:off["f"]],
                            wp[:, :off["q"]]], axis=1)


def _cols_from_slabs(g):
    return jnp.transpose(g, (1, 0, 2)).reshape(g.shape[1], N_DEV * g.shape[2])


def _cols_to_slabs(w):
    r, c = w.shape[0], w.shape[1] // N_DEV
    return jnp.transpose(w.reshape(r, N_DEV, c), (1, 0, 2))


def kernel(x, pre_mix_g, post_mix_g, pre_ffn_g, post_ffn_g, w_in, b_forget, b_gate, conv_mix_w, sgu_ln_g, sgu_ln_b, sgu_w, sgu_b, w_branch_att, w_branch_conv, w_branch_sgu, w_out, w_ffn_up, conv_ffn_w, w_ffn_down, loss_target, m_pre_mix_g, m_post_mix_g, m_pre_ffn_g, m_post_ffn_g, m_w_in, m_b_forget, m_b_gate, m_conv_mix_w, m_sgu_ln_g, m_sgu_ln_b, m_sgu_w, m_sgu_b, m_w_branch_att, m_w_branch_conv, m_w_branch_sgu, m_w_out, m_w_ffn_up, m_conv_ffn_w, m_w_ffn_down, v_pre_mix_g, v_post_mix_g, v_pre_ffn_g, v_post_ffn_g, v_w_in, v_b_forget, v_b_gate, v_conv_mix_w, v_sgu_ln_g, v_sgu_ln_b, v_sgu_w, v_sgu_b, v_w_branch_att, v_w_branch_conv, v_w_branch_sgu, v_w_out, v_w_ffn_up, v_conv_ffn_w, v_w_ffn_down):
    depth = w_in.shape[0]
    s, d = x.shape[1], x.shape[2]
    dff = w_ffn_down.shape[1] * N_DEV
    off, _ = _layout(d)
    qblk, bgblk, ublk, fblk = off["q"] // LANES, off["bg"] // LANES, off["u"] // D_SGU, off["f"] // LANES
    x0 = x.reshape(s, d)
    target = loss_target.reshape(s, d)
    ncm, ncf = conv_mix_w.shape[2], conv_ffn_w.shape[2]

    weights = []
    for l in range(depth):
        small = jnp.concatenate([b_gate[l], conv_mix_w[l], conv_ffn_w[l]], axis=1)
        shards = [w_in[l].astype(BF16), w_branch_att[l].astype(BF16), w_branch_conv[l].astype(BF16),
                  w_branch_sgu[l].astype(BF16), w_out[l].astype(BF16), w_ffn_up[l].astype(BF16),
                  w_ffn_down[l].astype(BF16), small]
        g_in, g_a, g_c, g_s, g_o, g_up, g_dn, g_small = _all_gather(shards, name="gather_weights")
        g_small = _cols_from_slabs(g_small).reshape(3, N_DEV, -1)
        lo = d // N_DEV
        weights.append(dict(
            w_in=_pad_w_in(_cols_from_slabs(g_in), d), wa=_cols_from_slabs(g_a), wc=_cols_from_slabs(g_c),
            ws=_cols_from_slabs(g_s), w_out=g_o.reshape(d, d), w_up=_cols_from_slabs(g_up), w_dn=g_dn.reshape(dff, d),
            b_gate=g_small[:, :, :lo].reshape(3, d), cmw=g_small[:, :, lo:lo + ncm].reshape(3, D_CONV),
            cfw=g_small[:, :, lo + ncm:].reshape(3, 2 * dff)))

    def bfull(l):
        return jnp.repeat(jnp.transpose(sgu_b[l]), HEAD_DIM, axis=1)

    def bf_pad(l):
        return jnp.pad(b_forget[l], (0, LANES - N_HEADS)).reshape(1, LANES)

    saved = []
    xin = x0
    xn = _prenorm(x0, pre_mix_g[0:1], name="prenorm_first")
    loss_acc = dy = None
    for l in range(depth):
        w = weights[l]
        h = _mm(xn, w["w_in"], "nn", F32, name="proj_in")
        c, ct = _forget_prep(h, bf_pad(l), fblk, name="forget_prep")
        att, lse = _attention_fwd(h, c, ct, qblk, name="attention_fwd")
        yc = _sconv_fwd(h, w["cmw"], bgblk, name="sconv_fwd")
        ys = _sgu_fwd(h, sgu_ln_g[l:l + 1], sgu_ln_b[l:l + 1], sgu_w[l], bfull(l), ublk, name="sgu_fwd")
        merged = _merge_fwd(h, w["b_gate"], att, yc, ys, w["wa"], w["wc"], w["ws"], name="merge_fwd")
        o = _mm(merged, w["w_out"], "nn", F32, name="proj_out")
        x1, xn2 = _postnorm(xin, o, post_mix_g[l:l + 1], pre_ffn_g[l:l + 1], name="postnorm_mix")
        hh = _mm(xn2, w["w_up"], "nn", F32, name="ffn_up")
        z = _ffn_act_fwd(hh, w["cfw"], name="ffn_act_fwd")
        f = _mm(z, w["w_dn"], "nn", F32, name="ffn_down")
        saved.append(dict(xin=xin, xn=xn, h=h, c=c, ct=ct, lse=lse, att=att, yc=yc, ys=ys, merged=merged, o=o, x1=x1,
                          xn2=xn2, hh=hh, z=z, f=f))
        if l + 1 < depth:
            xin, xn = _postnorm(x1, f, post_ffn_g[l:l + 1], pre_mix_g[l + 1:l + 2], name="postnorm_ffn")
        else:
            dy, loss_acc = _postnorm_loss(x1, f, post_ffn_g[l:l + 1], target, name="postnorm_loss")
    loss = lax.psum(loss_acc[0, 0] * (0.5 / d), ("x", "y", "c"))

    big = {k: [None] * depth for k in ("w_in", "wa", "wc", "ws", "w_out", "w_up", "w_dn", "small")}
    rep = {k: [None] * depth for k in ("pre_mix_g", "post_mix_g", "pre_ffn_g", "post_ffn_g", "b_forget", "sgu_ln_g",
                                       "sgu_ln_b", "sgu_w", "sgu_b")}
    dx = dy
    for l in reversed(range(depth)):
        w, a = weights[l], saved[l]
        df, rep["post_ffn_g"][l] = _postnorm_bwd(a["f"], post_ffn_g[l:l + 1], dx, name="postnorm_bwd")
        dz = _mm(df, w["w_dn"], "nt", F32, name="ffn_down_dx")
        g_dn = _mm(a["z"], df, "tn", BF16, name="ffn_down_dw")
        dha, dhb, dcwa, dcwb = _ffn_act_bwd(a["hh"], w["cfw"], dz, name="ffn_act_bwd")
        dhh = jnp.concatenate([dha, dhb], axis=1)
        dcfw = jnp.concatenate([dcwa[0:3], dcwb[0:3]], axis=1)
        dxn2 = _mm(dhh, w["w_up"], "nt", F32, name="ffn_up_dx")
        g_up = _mm(a["xn2"], dhh, "tn", BF16, name="ffn_up_dw")
        dx1, rep["pre_ffn_g"][l] = _prenorm_bwd(a["x1"], pre_ffn_g[l:l + 1], dxn2, dx, name="prenorm_bwd")
        do, rep["post_mix_g"][l] = _postnorm_bwd(a["o"], post_mix_g[l:l + 1], dx1, name="postnorm_bwd")
        dmerged = _mm(do, w["w_out"], "nt", F32, name="proj_out_dx")
        g_o = _mm(a["merged"], do, "tn", BF16, name="proj_out_dw")
        dya, dyc, dys, dgl, dbg = _merge_bwd(a["h"], w["b_gate"], a["att"], a["yc"], a["ys"], w["wa"], w["wc"], w["ws"],
                                             dmerged, name="merge_bwd")
        datt = _mm(dya, w["wa"], "nt", F32, name="branch_att_dx")
        dconv = _mm(dyc, w["wc"], "nt", F32, name="branch_conv_dx")
        dsgu = _mm(dys, w["ws"], "nt", F32, name="branch_sgu_dx")
        g_a = _mm(a["att"], dya, "tn", BF16, name="branch_att_dw")
        g_c = _mm(a["yc"], dyc, "tn", BF16, name="branch_conv_dw")
        g_s = _mm(a["ys"], dys, "tn", BF16, name="branch_sgu_dw")
        dq, dk, dv, dct4, dcq4 = _attention_bwd(a["h"], a["c"], a["ct"], a["lse"], a["att"], datt, qblk,
                                                name="attention_bwd")
        dcq = jnp.pad(jnp.transpose(dcq4[:, :, 0:2], (1, 0, 2)).reshape(s, N_HEADS), ((0, 0), (0, LANES - N_HEADS)))
        dfl, dbf = _forget_prep_bwd(a["h"], bf_pad(l), dct4[:, 0:2, :].reshape(N_HEADS, s), dcq, fblk,
                                    name="forget_prep_bwd")
        rep["b_forget"][l] = dbf[0, :N_HEADS]
        dbgate, dcg, dhc, dcmw = _sconv_bwd(a["h"], w["cmw"], dconv, bgblk, name="sconv_bwd")
        du, dvs, dlg, dlb, dsw, dbfull = _sgu_bwd(a["h"], sgu_ln_g[l:l + 1], sgu_ln_b[l:l + 1], sgu_w[l], bfull(l), dsgu,
                                                  ublk, name="sgu_bwd")
        rep["sgu_ln_g"][l], rep["sgu_ln_b"][l], rep["sgu_w"][l] = dlg, dlb, dsw
        rep["sgu_b"][l] = jnp.transpose(jnp.sum(dbfull.reshape(SGU_CHUNK, N_SGU_GROUPS, HEAD_DIM), axis=2))
        dh = jnp.concatenate([dgl[0], dgl[1], dgl[2], dq.astype(BF16), dk.astype(BF16), dv.astype(BF16), dbgate, dcg, dhc,
                              du, dvs, dfl.astype(BF16), jnp.zeros((s, w["w_in"].shape[1] - off["f"] - LANES), BF16)], axis=1)
        dxn = _mm(dh, w["w_in"], "nt", F32, name="proj_in_dx")
        g_in = _mm(a["xn"], dh, "tn", BF16, name="proj_in_dw")
        dx, rep["pre_mix_g"][l] = _prenorm_bwd(a["xin"], pre_mix_g[l:l + 1], dxn, dx1, name="prenorm_bwd")
        sends = [_cols_to_slabs(_unpad_w_in(g_in, d)), _cols_to_slabs(g_a), _cols_to_slabs(g_c), _cols_to_slabs(g_s),
                 g_o.reshape(N_DEV, d // N_DEV, d), _cols_to_slabs(g_up), g_dn.reshape(N_DEV, dff // N_DEV, d),
                 jnp.concatenate([_cols_to_slabs(dbg), _cols_to_slabs(dcmw[0:3]), _cols_to_slabs(dcfw)], axis=2)]
        got = _all_to_all(sends, name="scatter_grads")
        for k, g in zip(("w_in", "wa", "wc", "ws", "w_out", "w_up", "w_dn", "small"), got):
            big[k][l] = g

    def stack(parts):
        return jnp.stack(parts, axis=1)

    outs = {}

    def update(name_, slabs, w_, m_, v_):
        shp = w_.shape
        w3 = w_.reshape((shp[0], -1, shp[-1])) if w_.ndim >= 3 else w_.reshape((1,) + shp)
        g, dl, nm, nv = _adamw(slabs.reshape((N_DEV,) + w3.shape), w3, m_.reshape(w3.shape), v_.reshape(w3.shape),
                               name="adamw_" + name_)
        outs[name_] = tuple(t.reshape(shp) for t in (g, dl, nm, nv))

    update("w_in", stack(big["w_in"]), w_in, m_w_in, v_w_in)
    update("w_branch_att", stack(big["wa"]), w_branch_att, m_w_branch_att, v_w_branch_att)
    update("w_branch_conv", stack(big["wc"]), w_branch_conv, m_w_branch_conv, v_w_branch_conv)
    update("w_branch_sgu", stack(big["ws"]), w_branch_sgu, m_w_branch_sgu, v_w_branch_sgu)
    update("w_out", stack(big["w_out"]), w_out, m_w_out, v_w_out)
    update("w_ffn_up", stack(big["w_up"]), w_ffn_up, m_w_ffn_up, v_w_ffn_up)
    update("w_ffn_down", stack(big["w_dn"]), w_ffn_down, m_w_ffn_down, v_w_ffn_down)
    small = stack(big["small"])
    lo = d // N_DEV
    update("b_gate", small[..., :lo], b_gate, m_b_gate, v_b_gate)
    update("conv_mix_w", small[..., lo:lo + ncm], conv_mix_w, m_conv_mix_w, v_conv_mix_w)
    update("conv_ffn_w", small[..., lo + ncm:], conv_ffn_w, m_conv_ffn_w, v_conv_ffn_w)

    rep_names = ("pre_mix_g", "post_mix_g", "pre_ffn_g", "post_ffn_g", "b_forget", "sgu_ln_g", "sgu_ln_b", "sgu_w", "sgu_b")
    rep_w = dict(pre_mix_g=(pre_mix_g, m_pre_mix_g, v_pre_mix_g), post_mix_g=(post_mix_g, m_post_mix_g, v_post_mix_g),
                 pre_ffn_g=(pre_ffn_g, m_pre_ffn_g, v_pre_ffn_g), post_ffn_g=(post_ffn_g, m_post_ffn_g, v_post_ffn_g),
                 b_forget=(b_forget, m_b_forget, v_b_forget), sgu_ln_g=(sgu_ln_g, m_sgu_ln_g, v_sgu_ln_g),
                 sgu_ln_b=(sgu_ln_b, m_sgu_ln_b, v_sgu_ln_b), sgu_w=(sgu_w, m_sgu_w, v_sgu_w), sgu_b=(sgu_b, m_sgu_b, v_sgu_b))

    def pack(parts):
        flat = jnp.concatenate([p.reshape(-1) for p in parts])
        return jnp.pad(flat, (0, -flat.shape[0] % (8 * LANES))).reshape(-1, LANES)

    part = pack([jnp.stack([g.reshape(rep_w[k][0].shape[1:]) for g in rep[k]]) for k in rep_names])
    (gathered,) = _all_gather([part], name="gather_small_grads")
    packed = [pack([rep_w[k][i] for k in rep_names]) for i in range(3)]
    res = _adamw(gathered.reshape(N_DEV, 1, -1, LANES), *[p.reshape(1, -1, LANES) for p in packed], name="adamw_replicated")
    pos = 0
    for k in rep_names:
        shp = rep_w[k][0].shape
        size = math.prod(shp)
        outs[k] = tuple(t.reshape(-1)[pos:pos + size].reshape(shp) for t in res)
        pos += size

    order = ("pre_mix_g", "post_mix_g", "pre_ffn_g", "post_ffn_g", "w_in", "b_forget", "b_gate", "conv_mix_w", "sgu_ln_g",
             "sgu_ln_b", "sgu_w", "sgu_b", "w_branch_att", "w_branch_conv", "w_branch_sgu", "w_out", "w_ffn_up",
             "conv_ffn_w", "w_ffn_down")
    grad_x = dx.reshape(x.shape)
    return (loss, grad_x, *[outs[k][0] for k in order], *[outs[k][1] for k in order], *[outs[k][2] for k in order],
            *[outs[k][3] for k in order])
```

```python
import functools
import math

import jax
import jax.numpy as jnp
from jax import lax
from jax.experimental import pallas as pl
from jax.experimental.pallas import tpu as pltpu

F32 = jnp.float32
BF16 = jnp.bfloat16

N_DEV = 8
HEAD_DIM = 64
N_HEADS = 8
D_ATT = 512
D_CONV = 256
D_SGU = 256
N_SGU_GROUPS = 4
SGU_CHUNK = 128
RMS_EPS = 1e-6
LN_EPS = 1e-5
ADAM_LR = 0.001
ADAM_B1 = 0.9
ADAM_B2 = 0.999
ADAM_EPS = 1e-08
ADAM_WD = 0.01
ADAM_STEP = 10
LANES = 128
VMEM_LIMIT = 56 * 1024 * 1024
NEG = -1e30
MESH = pl.DeviceIdType.MESH


def _params(*sem):
    return pltpu.CompilerParams(dimension_semantics=sem if sem else None, vmem_limit_bytes=VMEM_LIMIT)


def _tile(n, cap):
    if n <= cap:
        return n
    t = cap - cap % LANES
    while n % t:
        t -= LANES
    return t


def _gelu(x):
    return 0.5 * x * (1.0 + jnp.tanh(math.sqrt(2.0 / math.pi) * (x + 0.044715 * (x * x * x))))


def _rms(x, g):
    r = lax.rsqrt(jnp.mean(x * x, axis=-1, keepdims=True) + RMS_EPS)
    return x * r * g


def _layer_norm(x, g, b):
    mu = jnp.mean(x, axis=-1, keepdims=True)
    xc = x - mu
    var = jnp.mean(xc * xc, axis=-1, keepdims=True)
    return xc * lax.rsqrt(var + LN_EPS) * g + b


def _shift_down(x, k, rows):
    return jnp.where(rows >= k, pltpu.roll(x, k, 0), 0.0)


def _shift_up(x, k, rows):
    s = x.shape[0]
    return jnp.where(rows < s - k, pltpu.roll(x, s - k, 0), 0.0)


def _conv3(x, w_ref, rows):
    return w_ref[2:3, :] * x + w_ref[1:2, :] * _shift_down(x, 1, rows) + w_ref[0:1, :] * _shift_down(x, 2, rows)


def _conv3_t(dy, w_ref, rows):
    return w_ref[2:3, :] * dy + w_ref[1:2, :] * _shift_up(dy, 1, rows) + w_ref[0:1, :] * _shift_up(dy, 2, rows)


def _conv3_dw(dy, x, rows):
    d2 = jnp.sum(dy * x, axis=0, keepdims=True)
    d1 = jnp.sum(dy * _shift_down(x, 1, rows), axis=0, keepdims=True)
    d0 = jnp.sum(dy * _shift_down(x, 2, rows), axis=0, keepdims=True)
    sub = lax.broadcasted_iota(jnp.int32, (8, x.shape[1]), 0)
    return jnp.where(sub == 0, d0, jnp.where(sub == 1, d1, jnp.where(sub == 2, d2, 0.0)))


MM_VMEM_BUDGET = 40 * 1024 * 1024
MM_TILE_CAP = 1408


def _mm_tiles(m, n, k, out_bytes):
    def edges(d):
        return [t for t in range(LANES, min(d, MM_TILE_CAP) + 1, LANES) if d % t == 0] or [d]

    best = None
    for tm in edges(m):
        for tn in edges(n):
            if 2 * (2 * k * (tm + tn) + tm * tn * out_bytes) > MM_VMEM_BUDGET:
                continue
            for a_outer in (True, False):
                reads = k * m + (m // tm) * k * n if a_outer else k * n + (n // tn) * k * m
                traffic = 2 * reads + m * n * out_bytes
                key = (traffic, -tm * tn)
                if best is None or key < best[0]:
                    best = (key, (tm, tn, a_outer))
    return best[1]


def _mm(a, b, form, out_dtype, name):
    if form == "nn":
        (m, k), n = a.shape, b.shape[1]
    elif form == "nt":
        (m, k), n = a.shape, b.shape[0]
    else:
        (k, m), n = a.shape, b.shape[1]
    tm, tn, a_outer = _mm_tiles(m, n, k, jnp.dtype(out_dtype).itemsize)
    dims = {"nn": (((1,), (0,)), ((), ())), "nt": (((1,), (1,)), ((), ())), "tn": (((0,), (0,)), ((), ()))}[form]

    def body(a_ref, b_ref, o_ref):
        o_ref[...] = lax.dot_general(a_ref[...], b_ref[...], dims, preferred_element_type=F32).astype(o_ref.dtype)

    ij = (lambda g0, g1: (g0, g1)) if a_outer else (lambda g0, g1: (g1, g0))
    a_spec = (pl.BlockSpec((k, tm), lambda g0, g1: (0, ij(g0, g1)[0])) if form == "tn"
              else pl.BlockSpec((tm, k), lambda g0, g1: (ij(g0, g1)[0], 0)))
    b_spec = (pl.BlockSpec((tn, k), lambda g0, g1: (ij(g0, g1)[1], 0)) if form == "nt"
              else pl.BlockSpec((k, tn), lambda g0, g1: (0, ij(g0, g1)[1])))
    return pl.pallas_call(
        body, name=name, grid=(m // tm, n // tn) if a_outer else (n // tn, m // tm),
        in_specs=[a_spec, b_spec], out_specs=pl.BlockSpec((tm, tn), lambda g0, g1: ij(g0, g1)),
        out_shape=jax.ShapeDtypeStruct((m, n), out_dtype),
        compiler_params=_params("parallel", "arbitrary"),
    )(a, b)


def _prenorm(x, g, name):
    s, d = x.shape
    tm = _tile(s, 512)

    def body(x_ref, g_ref, o_ref):
        o_ref[...] = _rms(x_ref[...], g_ref[...]).astype(BF16)

    return pl.pallas_call(
        body, name=name, grid=(s // tm,),
        in_specs=[pl.BlockSpec((tm, d), lambda i: (i, 0)), pl.BlockSpec((1, d), lambda i: (0, 0))],
        out_specs=pl.BlockSpec((tm, d), lambda i: (i, 0)),
        out_shape=jax.ShapeDtypeStruct((s, d), BF16), compiler_params=_params("parallel"),
    )(x, g)


def _postnorm(x, o, g_post, g_next, name):
    s, d = x.shape
    tm = _tile(s, 512)

    def body(x_ref, o_ref, gp_ref, gn_ref, x1_ref, xn_ref):
        x1 = x_ref[...] + _rms(o_ref[...], gp_ref[...])
        x1_ref[...] = x1
        xn_ref[...] = _rms(x1, gn_ref[...]).astype(BF16)

    row = pl.BlockSpec((tm, d), lambda i: (i, 0))
    vec = pl.BlockSpec((1, d), lambda i: (0, 0))
    return pl.pallas_call(
        body, name=name, grid=(s // tm,), in_specs=[row, row, vec, vec], out_specs=[row, row],
        out_shape=[jax.ShapeDtypeStruct((s, d), F32), jax.ShapeDtypeStruct((s, d), BF16)],
        compiler_params=_params("parallel"),
    )(x, o, g_post, g_next)


def _postnorm_loss(x, o, g_post, target, name):
    s, d = x.shape
    tm = _tile(s, 512)

    def body(x_ref, o_ref, gp_ref, t_ref, dy_ref, acc_ref):
        e = x_ref[...] + _rms(o_ref[...], gp_ref[...]) - t_ref[...]
        dy_ref[...] = e / d

        @pl.when(pl.program_id(0) == 0)
        def _():
            acc_ref[...] = jnp.zeros_like(acc_ref)

        acc_ref[...] += jnp.sum(jnp.sum(e * e, axis=1, keepdims=True), axis=0, keepdims=True)

    row = pl.BlockSpec((tm, d), lambda i: (i, 0))
    return pl.pallas_call(
        body, name=name, grid=(s // tm,),
        in_specs=[row, row, pl.BlockSpec((1, d), lambda i: (0, 0)), row],
        out_specs=[row, pl.BlockSpec((1, LANES), lambda i: (0, 0))],
        out_shape=[jax.ShapeDtypeStruct((s, d), F32), jax.ShapeDtypeStruct((1, LANES), F32)],
        compiler_params=_params("arbitrary"),
    )(x, o, g_post, target)


def _postnorm_bwd(o, g, dx, name):
    s, d = o.shape
    tm = _tile(s, 512)

    def body(o_ref, g_ref, dx_ref, do_ref, dg_ref):
        _, vjp = jax.vjp(_rms, o_ref[...], g_ref[...])
        d_o, dg = vjp(dx_ref[...])
        do_ref[...] = d_o.astype(BF16)

        @pl.when(pl.program_id(0) == 0)
        def _():
            dg_ref[...] = jnp.zeros_like(dg_ref)

        dg_ref[...] += dg

    row = pl.BlockSpec((tm, d), lambda i: (i, 0))
    vec = pl.BlockSpec((1, d), lambda i: (0, 0))
    return pl.pallas_call(
        body, name=name, grid=(s // tm,), in_specs=[row, vec, row], out_specs=[row, vec],
        out_shape=[jax.ShapeDtypeStruct((s, d), BF16), jax.ShapeDtypeStruct((1, d), F32)],
        compiler_params=_params("arbitrary"),
    )(o, g, dx)


def _prenorm_bwd(x, g, dxn, dres, name):
    s, d = x.shape
    tm = _tile(s, 512)

    def body(x_ref, g_ref, dxn_ref, dres_ref, dx_ref, dg_ref):
        _, vjp = jax.vjp(_rms, x_ref[...], g_ref[...])
        dx, dg = vjp(dxn_ref[...])
        dx_ref[...] = dres_ref[...] + dx

        @pl.when(pl.program_id(0) == 0)
        def _():
            dg_ref[...] = jnp.zeros_like(dg_ref)

        dg_ref[...] += dg

    row = pl.BlockSpec((tm, d), lambda i: (i, 0))
    vec = pl.BlockSpec((1, d), lambda i: (0, 0))
    return pl.pallas_call(
        body, name=name, grid=(s // tm,), in_specs=[row, vec, row, row], out_specs=[row, vec],
        out_shape=[jax.ShapeDtypeStruct((s, d), F32), jax.ShapeDtypeStruct((1, d), F32)],
        compiler_params=_params("arbitrary"),
    )(x, g, dxn, dres)


def _log_sigmoid(z):
    return jnp.minimum(z, 0.0) - jnp.log(1.0 + jnp.exp(-jnp.abs(z)))


def _forget_prep(h, bf_pad, fblk, name):
    s = h.shape[0]

    def body(f_ref, b_ref, c_ref, ct_ref):
        c = _log_sigmoid(f_ref[...] + b_ref[...])
        rows = lax.broadcasted_iota(jnp.int32, c.shape, 0)
        k = 1
        while k < s:
            c = c + _shift_down(c, k, rows)
            k *= 2
        c_ref[...] = c
        ct_ref[...] = jnp.transpose(c)[0:8, :]

    return pl.pallas_call(
        body, name=name, grid=(1,),
        in_specs=[pl.BlockSpec((s, LANES), lambda i: (0, fblk)), pl.BlockSpec((1, LANES), lambda i: (0, 0))],
        out_specs=[pl.BlockSpec((s, LANES), lambda i: (0, 0)), pl.BlockSpec((8, s), lambda i: (0, 0))],
        out_shape=[jax.ShapeDtypeStruct((s, LANES), F32), jax.ShapeDtypeStruct((8, s), F32)],
        compiler_params=_params("arbitrary"),
    )(h, bf_pad)


def _forget_prep_bwd(h, bf_pad, dct, dcq, fblk, name):
    s = h.shape[0]

    def body(f_ref, b_ref, dct_ref, dcq_ref, df_ref, db_ref):
        dc = dcq_ref[...] + jnp.transpose(jnp.concatenate([dct_ref[...], jnp.zeros((LANES - 8, s), F32)], axis=0))
        rows = lax.broadcasted_iota(jnp.int32, dc.shape, 0)
        k = 1
        while k < s:
            dc = dc + _shift_up(dc, k, rows)
            k *= 2
        z = f_ref[...] + b_ref[...]
        lane = lax.broadcasted_iota(jnp.int32, dc.shape, 1)
        df = jnp.where(lane < N_HEADS, dc * jax.nn.sigmoid(-z), 0.0)
        df_ref[...] = df
        db_ref[...] = jnp.sum(df, axis=0, keepdims=True)

    return pl.pallas_call(
        body, name=name, grid=(1,),
        in_specs=[pl.BlockSpec((s, LANES), lambda i: (0, fblk)), pl.BlockSpec((1, LANES), lambda i: (0, 0)),
                  pl.BlockSpec((8, s), lambda i: (0, 0)), pl.BlockSpec((s, LANES), lambda i: (0, 0))],
        out_specs=[pl.BlockSpec((s, LANES), lambda i: (0, 0)), pl.BlockSpec((1, LANES), lambda i: (0, 0))],
        out_shape=[jax.ShapeDtypeStruct((s, LANES), F32), jax.ShapeDtypeStruct((1, LANES), F32)],
        compiler_params=_params("arbitrary"),
    )(h, bf_pad, dct, dcq)


def _pick_lane(blk, idx):
    lane = lax.broadcasted_iota(jnp.int32, blk.shape, 1)
    return jnp.sum(jnp.where(lane == idx, blk, 0.0), axis=1, keepdims=True)


def _pick_row(blk, idx):
    sub = lax.broadcasted_iota(jnp.int32, blk.shape, 0)
    return jnp.sum(jnp.where(sub == idx, blk, 0.0), axis=0, keepdims=True)


def _attention_fwd(h, c, ct, qblk, name):
    s = h.shape[0]
    t = _tile(s, 512)
    nq = s // t
    scale = HEAD_DIM ** -0.5
    nt_dims = (((1,), (1,)), ((), ()))

    def body(q_ref, k_ref, v_ref, c_ref, ct_ref, o_ref, lse_ref):
        p = pl.program_id(0)
        i = pl.program_id(1)
        lane = lax.broadcasted_iota(jnp.int32, (1, LANES), 1)
        first = lane < HEAD_DIM
        q = q_ref[...] * scale
        qa = jnp.where(first, q, 0.0).astype(BF16)
        qb = jnp.where(first, 0.0, q).astype(BF16)
        cblk = c_ref[...]
        cta = _pick_lane(cblk, 2 * p)
        ctb = _pick_lane(cblk, 2 * p + 1)

        def step(j, carry, diagonal):
            ma, la, mb, lb, acc = carry
            off = pl.multiple_of(j * t, t)
            k = k_ref[pl.ds(off, t), :].astype(BF16)
            v = v_ref[pl.ds(off, t), :].astype(BF16)
            crow = ct_ref[:, pl.ds(off, t)]

            def one(qh, cth, hd, m_old, l_old):
                sc = lax.dot_general(qh, k, nt_dims, preferred_element_type=F32) - _pick_row(crow, hd)
                if diagonal:
                    keep = lax.broadcasted_iota(jnp.int32, (t, t), 0) >= lax.broadcasted_iota(jnp.int32, (t, t), 1)
                    sc = jnp.where(keep, sc, NEG)
                m_new = jnp.maximum(m_old, jnp.max(sc, axis=1, keepdims=True) + cth)
                pr = jnp.exp(sc - (m_new - cth))
                alpha = jnp.exp(m_old - m_new)
                l_new = alpha * l_old + jnp.sum(pr, axis=1, keepdims=True)
                pv = jnp.dot(pr.astype(BF16), v, preferred_element_type=F32)
                return m_new, l_new, alpha, pv

            ma2, la2, aa, pva = one(qa, cta, 2 * p, ma, la)
            mb2, lb2, ab, pvb = one(qb, ctb, 2 * p + 1, mb, lb)
            acc = jnp.where(first, aa * acc + pva, ab * acc + pvb)
            return ma2, la2, mb2, lb2, acc

        init = (jnp.full((t, 1), NEG, F32), jnp.zeros((t, 1), F32), jnp.full((t, 1), NEG, F32),
                jnp.zeros((t, 1), F32), jnp.zeros((t, LANES), F32))
        carry = lax.fori_loop(0, i, lambda j, carry: step(j, carry, False), init)
        ma, la, mb, lb, acc = step(i, carry, True)
        o_ref[...] = (acc / jnp.where(first, la, lb)).astype(BF16)
        lse_ref[0] = jnp.broadcast_to(ma + jnp.log(la), (t, LANES))
        lse_ref[1] = jnp.broadcast_to(mb + jnp.log(lb), (t, LANES))

    return pl.pallas_call(
        body, name=name, grid=(N_HEADS // 2, nq),
        in_specs=[pl.BlockSpec((t, LANES), lambda p, i: (i, qblk + p)),
                  pl.BlockSpec((s, LANES), lambda p, i: (0, qblk + 4 + p)),
                  pl.BlockSpec((s, LANES), lambda p, i: (0, qblk + 8 + p)),
                  pl.BlockSpec((t, LANES), lambda p, i: (i, 0)),
                  pl.BlockSpec((8, s), lambda p, i: (0, 0))],
        out_specs=[pl.BlockSpec((t, LANES), lambda p, i: (i, p)),
                   pl.BlockSpec((2, t, LANES), lambda p, i: (p, i, 0))],
        out_shape=[jax.ShapeDtypeStruct((s, D_ATT), BF16), jax.ShapeDtypeStruct((N_HEADS, s, LANES), F32)],
        compiler_params=_params("parallel", "arbitrary"),
    )(h, h, h, c, ct)


def _attention_bwd(h, c, ct, lse, att, datt, qblk, name):
    s = h.shape[0]
    t = _tile(s, 512)
    nq = s // t
    scale = HEAD_DIM ** -0.5
    nt_dims = (((1,), (1,)), ((), ()))
    tn_dims = (((0,), (0,)), ((), ()))

    def body(q_ref, k_ref, v_ref, c_ref, ct_ref, lse_ref, o_ref, do_ref, dq_ref, dk_ref, dv_ref, dct_ref, dcq_ref):
        p = pl.program_id(0)
        j = pl.program_id(1)
        lane = lax.broadcasted_iota(jnp.int32, (1, LANES), 1)
        first = lane < HEAD_DIM
        kf = k_ref[...]
        vf = v_ref[...]
        k = kf.astype(BF16)
        ka = jnp.where(first, kf, 0.0).astype(BF16)
        kb = jnp.where(first, 0.0, kf).astype(BF16)
        va = jnp.where(first, vf, 0.0).astype(BF16)
        vb = jnp.where(first, 0.0, vf).astype(BF16)
        crow = ct_ref[...]
        csa = _pick_row(crow, 2 * p)
        csb = _pick_row(crow, 2 * p + 1)

        @pl.when(j == 0)
        def _():
            dq_ref[...] = jnp.zeros_like(dq_ref)
            dcq_ref[...] = jnp.zeros_like(dcq_ref)

        def step(i, carry, diagonal):
            dka, dkb, dva, dvb, dca, dcb = carry
            off = pl.multiple_of(i * t, t)
            rows = pl.ds(off, t)
            q = (q_ref[rows, :] * scale).astype(BF16)
            dof = do_ref[rows, :]
            do = dof.astype(BF16)
            prod = dof * o_ref[rows, :].astype(F32)
            cblk = c_ref[rows, :]

            def one(kh, vh, hd, csh, lse_h):
                sc = lax.dot_general(q, kh, nt_dims, preferred_element_type=F32) - csh
                if diagonal:
                    keep = lax.broadcasted_iota(jnp.int32, (t, t), 0) >= lax.broadcasted_iota(jnp.int32, (t, t), 1)
                    sc = jnp.where(keep, sc, NEG)
                pr = jnp.exp(sc - (jnp.max(lse_h, axis=1, keepdims=True) - _pick_lane(cblk, hd)))
                dp = lax.dot_general(do, vh, nt_dims, preferred_element_type=F32)
                return pr, dp

            pra, dpa = one(ka, va, 2 * p, csa, lse_ref[0, rows, :])
            prb, dpb = one(kb, vb, 2 * p + 1, csb, lse_ref[1, rows, :])
            dela = jnp.sum(jnp.where(first, prod, 0.0), axis=1, keepdims=True)
            delb = jnp.sum(jnp.where(first, 0.0, prod), axis=1, keepdims=True)
            dsa = pra * (dpa - dela)
            dsb = prb * (dpb - delb)
            dsa16 = dsa.astype(BF16)
            dsb16 = dsb.astype(BF16)
            dva = dva + lax.dot_general(pra.astype(BF16), do, tn_dims, preferred_element_type=F32)
            dvb = dvb + lax.dot_general(prb.astype(BF16), do, tn_dims, preferred_element_type=F32)
            dka = dka + lax.dot_general(dsa16, q, tn_dims, preferred_element_type=F32)
            dkb = dkb + lax.dot_general(dsb16, q, tn_dims, preferred_element_type=F32)
            dqa = jnp.dot(dsa16, k, preferred_element_type=F32)
            dqb = jnp.dot(dsb16, k, preferred_element_type=F32)
            dq_ref[rows, :] += scale * jnp.where(first, dqa, dqb)
            dca = dca - jnp.sum(dsa, axis=0, keepdims=True)
            dcb = dcb - jnp.sum(dsb, axis=0, keepdims=True)
            dcq_ref[rows, :] += jnp.where(lane == 0, jnp.sum(dsa, axis=1, keepdims=True),
                                          jnp.where(lane == 1, jnp.sum(dsb, axis=1, keepdims=True), 0.0))
            return dka, dkb, dva, dvb, dca, dcb

        z = jnp.zeros((t, LANES), F32)
        zr = jnp.zeros((1, t), F32)
        carry = step(j, (z, z, z, z, zr, zr), True)
        dka, dkb, dva, dvb, dca, dcb = lax.fori_loop(j + 1, nq, lambda i, carry: step(i, carry, False), carry)
        dk_ref[...] = jnp.where(first, dka, dkb)
        dv_ref[...] = jnp.where(first, dva, dvb)
        sub = lax.broadcasted_iota(jnp.int32, (8, t), 0)
        dct_ref[...] = jnp.where(sub == 0, dca, jnp.where(sub == 1, dcb, 0.0))

    full = lambda blk: pl.BlockSpec((s, LANES), blk)
    return pl.pallas_call(
        body, name=name, grid=(N_HEADS // 2, nq),
        in_specs=[full(lambda p, j: (0, qblk + p)),
                  pl.BlockSpec((t, LANES), lambda p, j: (j, qblk + 4 + p)),
                  pl.BlockSpec((t, LANES), lambda p, j: (j, qblk + 8 + p)),
                  full(lambda p, j: (0, 0)),
                  pl.BlockSpec((8, t), lambda p, j: (0, j)),
                  pl.BlockSpec((2, s, LANES), lambda p, j: (p, 0, 0)),
                  full(lambda p, j: (0, p)),
                  full(lambda p, j: (0, p))],
        out_specs=[full(lambda p, j: (0, p)),
                   pl.BlockSpec((t, LANES), lambda p, j: (j, p)),
                   pl.BlockSpec((t, LANES), lambda p, j: (j, p)),
                   pl.BlockSpec((None, 8, t), lambda p, j: (p, 0, j)),
                   pl.BlockSpec((None, s, LANES), lambda p, j: (p, 0, 0))],
        out_shape=[jax.ShapeDtypeStruct((s, D_ATT), F32), jax.ShapeDtypeStruct((s, D_ATT), F32),
                   jax.ShapeDtypeStruct((s, D_ATT), F32), jax.ShapeDtypeStruct((N_HEADS // 2, 8, s), F32),
                   jax.ShapeDtypeStruct((N_HEADS // 2, s, LANES), F32)],
        compiler_params=_params("arbitrary", "arbitrary"),
    )(h, h, h, c, ct, lse, att, datt)


def _sconv_fwd(h, w, bgblk, name):
    s = h.shape[0]
    nblk = D_CONV // LANES

    def body(bg_ref, cg_ref, hc_ref, w_ref, y_ref):
        rows = lax.broadcasted_iota(jnp.int32, (s, LANES), 0)
        y_ref[...] = (bg_ref[...] * _conv3(cg_ref[...] * hc_ref[...], w_ref, rows)).astype(BF16)

    col = lambda base: pl.BlockSpec((s, LANES), lambda j: (0, base + j))
    return pl.pallas_call(
        body, name=name, grid=(nblk,),
        in_specs=[col(bgblk), col(bgblk + nblk), col(bgblk + 2 * nblk), pl.BlockSpec((3, LANES), lambda j: (0, j))],
        out_specs=pl.BlockSpec((s, LANES), lambda j: (0, j)),
        out_shape=jax.ShapeDtypeStruct((s, D_CONV), BF16), compiler_params=_params("parallel"),
    )(h, h, h, w)


def _sconv_bwd(h, w, dy, bgblk, name):
    s = h.shape[0]
    nblk = D_CONV // LANES

    def body(bg_ref, cg_ref, hc_ref, w_ref, dy_ref, dbg_ref, dcg_ref, dhc_ref, dw_ref):
        rows = lax.broadcasted_iota(jnp.int32, (s, LANES), 0)
        cg, hc, dy, w = cg_ref[...], hc_ref[...], dy_ref[...], w_ref
        xin = cg * hc
        dbg_ref[...] = (dy * _conv3(xin, w, rows)).astype(BF16)
        dconv = dy * bg_ref[...]
        dxin = _conv3_t(dconv, w, rows)
        dcg_ref[...] = (dxin * hc).astype(BF16)
        dhc_ref[...] = (dxin * cg).astype(BF16)
        dw_ref[...] = _conv3_dw(dconv, xin, rows)

    col = lambda base: pl.BlockSpec((s, LANES), lambda j: (0, base + j))
    return pl.pallas_call(
        body, name=name, grid=(nblk,),
        in_specs=[col(bgblk), col(bgblk + nblk), col(bgblk + 2 * nblk), pl.BlockSpec((3, LANES), lambda j: (0, j)), col(0)],
        out_specs=[col(0), col(0), col(0), pl.BlockSpec((8, LANES), lambda j: (0, j))],
        out_shape=[jax.ShapeDtypeStruct((s, D_CONV), BF16)] * 3 + [jax.ShapeDtypeStruct((8, D_CONV), F32)],
        compiler_params=_params("parallel"),
    )(h, h, h, w, dy)


def _sgu_group_masks():
    lane = lax.broadcasted_iota(jnp.int32, (1, D_SGU), 1)
    return [(lane // HEAD_DIM) == g for g in range(N_SGU_GROUPS)]


def _sgu_tril():
    r = lax.broadcasted_iota(jnp.int32, (SGU_CHUNK, SGU_CHUNK), 0)
    c = lax.broadcasted_iota(jnp.int32, (SGU_CHUNK, SGU_CHUNK), 1)
    return r >= c


def _sgu_fwd(h, ln_g, ln_b, w_s, b_full, ublk, name):
    s = h.shape[0]
    tr = _tile(s, 512)
    nch = tr // SGU_CHUNK

    def body(u_ref, v_ref, g_ref, b_ref, w_ref, bf_ref, y_ref):
        masks = _sgu_group_masks()
        tril = _sgu_tril()
        wm = [jnp.where(tril, w_ref[g], 0.0).astype(BF16) for g in range(N_SGU_GROUPS)]
        vn = _layer_norm(_gelu(v_ref[...]), g_ref[...], b_ref[...])
        for ch in range(nch):
            rows = pl.ds(ch * SGU_CHUNK, SGU_CHUNK)
            vc = vn[ch * SGU_CHUNK:(ch + 1) * SGU_CHUNK, :]
            mixed = bf_ref[...]
            for g in range(N_SGU_GROUPS):
                mixed = mixed + jnp.dot(wm[g], jnp.where(masks[g], vc, 0.0).astype(BF16), preferred_element_type=F32)
            y_ref[rows, :] = (_gelu(u_ref[rows, :]) * mixed).astype(BF16)

    row = lambda blk: pl.BlockSpec((tr, D_SGU), lambda i: (i, blk))
    vec = pl.BlockSpec((1, D_SGU), lambda i: (0, 0))
    return pl.pallas_call(
        body, name=name, grid=(s // tr,),
        in_specs=[row(ublk), row(ublk + 1), vec, vec,
                  pl.BlockSpec((N_SGU_GROUPS, SGU_CHUNK, SGU_CHUNK), lambda i: (0, 0, 0)),
                  pl.BlockSpec((SGU_CHUNK, D_SGU), lambda i: (0, 0))],
        out_specs=row(0), out_shape=jax.ShapeDtypeStruct((s, D_SGU), BF16), compiler_params=_params("parallel"),
    )(h, h, ln_g, ln_b, w_s, b_full)


def _sgu_bwd(h, ln_g, ln_b, w_s, b_full, dy, ublk, name):
    s = h.shape[0]
    tr = _tile(s, 512)
    nch = tr // SGU_CHUNK
    nt_dims = (((1,), (1,)), ((), ()))

    def norm(v, g, b):
        return _layer_norm(_gelu(v), g, b)

    def body(u_ref, v_ref, g_ref, b_ref, w_ref, bf_ref, dy_ref, du_ref, dv_ref, dg_ref, db_ref, dw_ref, dbf_ref):
        masks = _sgu_group_masks()
        tril = _sgu_tril()
        wf = [jnp.where(tril, w_ref[g], 0.0) for g in range(N_SGU_GROUPS)]
        wm = [w.astype(BF16) for w in wf]
        wmt = [jnp.transpose(w).astype(BF16) for w in wf]
        vn, vjp = jax.vjp(norm, v_ref[...], g_ref[...], b_ref[...])

        @pl.when(pl.program_id(0) == 0)
        def _():
            dg_ref[...] = jnp.zeros_like(dg_ref)
            db_ref[...] = jnp.zeros_like(db_ref)
            dw_ref[...] = jnp.zeros_like(dw_ref)
            dbf_ref[...] = jnp.zeros_like(dbf_ref)

        dvn_parts = []
        for ch in range(nch):
            rows = pl.ds(ch * SGU_CHUNK, SGU_CHUNK)
            vc = vn[ch * SGU_CHUNK:(ch + 1) * SGU_CHUNK, :]
            vc16 = vc.astype(BF16)
            mixed = bf_ref[...]
            for g in range(N_SGU_GROUPS):
                mixed = mixed + jnp.dot(wm[g], jnp.where(masks[g], vc, 0.0).astype(BF16), preferred_element_type=F32)
            dy = dy_ref[rows, :]
            ug, gvjp = jax.vjp(_gelu, u_ref[rows, :])
            du_ref[rows, :] = gvjp(dy * mixed)[0].astype(BF16)
            dmixed = dy * ug
            dbf_ref[...] += dmixed
            dvc = jnp.zeros((SGU_CHUNK, D_SGU), F32)
            for g in range(N_SGU_GROUPS):
                dm16 = jnp.where(masks[g], dmixed, 0.0).astype(BF16)
                dw_ref[g] += jnp.where(tril, lax.dot_general(dm16, vc16, nt_dims, preferred_element_type=F32), 0.0)
                dvc = dvc + jnp.dot(wmt[g], dm16, preferred_element_type=F32)
            dvn_parts.append(dvc)
        dv, dg, db = vjp(jnp.concatenate(dvn_parts, axis=0))
        dv_ref[...] = dv.astype(BF16)
        dg_ref[...] += dg
        db_ref[...] += db

    row = lambda blk: pl.BlockSpec((tr, D_SGU), lambda i: (i, blk))
    vec = pl.BlockSpec((1, D_SGU), lambda i: (0, 0))
    wsp = pl.BlockSpec((N_SGU_GROUPS, SGU_CHUNK, SGU_CHUNK), lambda i: (0, 0, 0))
    bsp = pl.BlockSpec((SGU_CHUNK, D_SGU), lambda i: (0, 0))
    return pl.pallas_call(
        body, name=name, grid=(s // tr,),
        in_specs=[row(ublk), row(ublk + 1), vec, vec, wsp, bsp, row(0)],
        out_specs=[row(0), row(0), vec, vec, wsp, bsp],
        out_shape=[jax.ShapeDtypeStruct((s, D_SGU), BF16), jax.ShapeDtypeStruct((s, D_SGU), BF16),
                   jax.ShapeDtypeStruct((1, D_SGU), F32), jax.ShapeDtypeStruct((1, D_SGU), F32),
                   jax.ShapeDtypeStruct((N_SGU_GROUPS, SGU_CHUNK, SGU_CHUNK), F32),
                   jax.ShapeDtypeStruct((SGU_CHUNK, D_SGU), F32)],
        compiler_params=_params("arbitrary"),
    )(h, h, ln_g, ln_b, w_s, b_full, dy)


def _merge_fwd(h, b_gate, att, yc, ys, wa, wc, ws, name):
    s, d = att.shape[0], wa.shape[1]
    tm, tn = _tile(s, 512), _tile(d, 512)
    nj = d // tn

    def body(g0_ref, g1_ref, g2_ref, bg_ref, a_ref, c_ref, s_ref, wa_ref, wc_ref, ws_ref, o_ref):
        acc = jax.nn.sigmoid(g0_ref[...] + bg_ref[0:1, :]) * jnp.dot(a_ref[...], wa_ref[...], preferred_element_type=F32)
        acc += jax.nn.sigmoid(g1_ref[...] + bg_ref[1:2, :]) * jnp.dot(c_ref[...], wc_ref[...], preferred_element_type=F32)
        acc += jax.nn.sigmoid(g2_ref[...] + bg_ref[2:3, :]) * jnp.dot(s_ref[...], ws_ref[...], preferred_element_type=F32)
        o_ref[...] = acc.astype(BF16)

    gate = lambda b: pl.BlockSpec((tm, tn), lambda j, i: (i, b * nj + j))
    act = lambda k: pl.BlockSpec((tm, k), lambda j, i: (i, 0))
    wgt = lambda k: pl.BlockSpec((k, tn), lambda j, i: (0, j))
    return pl.pallas_call(
        body, name=name, grid=(nj, s // tm),
        in_specs=[gate(0), gate(1), gate(2), pl.BlockSpec((3, tn), lambda j, i: (0, j)),
                  act(D_ATT), act(D_CONV), act(D_SGU), wgt(D_ATT), wgt(D_CONV), wgt(D_SGU)],
        out_specs=pl.BlockSpec((tm, tn), lambda j, i: (i, j)),
        out_shape=jax.ShapeDtypeStruct((s, d), BF16), compiler_params=_params("parallel", "arbitrary"),
    )(h, h, h, b_gate, att, yc, ys, wa, wc, ws)


def _merge_bwd(h, b_gate, att, yc, ys, wa, wc, ws, dm, name):
    s, d = att.shape[0], wa.shape[1]
    tm, tn = _tile(s, 512), _tile(d, 512)
    nj = d // tn

    def body(g0_ref, g1_ref, g2_ref, bg_ref, a_ref, c_ref, s_ref, wa_ref, wc_ref, ws_ref, dm_ref,
             dya_ref, dyc_ref, dys_ref, dgl_ref, dbg_ref):
        dm = dm_ref[...]
        sums = []
        for b, (g_ref, x_ref, w_ref, dy_ref) in enumerate(((g0_ref, a_ref, wa_ref, dya_ref), (g1_ref, c_ref, wc_ref, dyc_ref),
                                                          (g2_ref, s_ref, ws_ref, dys_ref))):
            gate = jax.nn.sigmoid(g_ref[...] + bg_ref[b:b + 1, :])
            y = jnp.dot(x_ref[...], w_ref[...], preferred_element_type=F32)
            dy_ref[...] = (dm * gate).astype(BF16)
            dgl = dm * y * gate * (1.0 - gate)
            dgl_ref[b] = dgl.astype(BF16)
            sums.append(jnp.sum(dgl, axis=0, keepdims=True))
        sub = lax.broadcasted_iota(jnp.int32, (3, tn), 0)
        part = jnp.where(sub == 0, sums[0], jnp.where(sub == 1, sums[1], sums[2]))

        @pl.when(pl.program_id(1) == 0)
        def _():
            dbg_ref[...] = jnp.zeros_like(dbg_ref)

        dbg_ref[...] += part

    gate = lambda b: pl.BlockSpec((tm, tn), lambda j, i: (i, b * nj + j))
    act = lambda k: pl.BlockSpec((tm, k), lambda j, i: (i, 0))
    wgt = lambda k: pl.BlockSpec((k, tn), lambda j, i: (0, j))
    tile = pl.BlockSpec((tm, tn), lambda j, i: (i, j))
    return pl.pallas_call(
        body, name=name, grid=(nj, s // tm),
        in_specs=[gate(0), gate(1), gate(2), pl.BlockSpec((3, tn), lambda j, i: (0, j)),
                  act(D_ATT), act(D_CONV), act(D_SGU), wgt(D_ATT), wgt(D_CONV), wgt(D_SGU), tile],
        out_specs=[tile, tile, tile, pl.BlockSpec((3, tm, tn), lambda j, i: (0, i, j)),
                   pl.BlockSpec((3, tn), lambda j, i: (0, j))],
        out_shape=[jax.ShapeDtypeStruct((s, d), BF16)] * 3 + [jax.ShapeDtypeStruct((3, s, d), BF16),
                                                              jax.ShapeDtypeStruct((3, d), F32)],
        compiler_params=_params("parallel", "arbitrary"),
    )(h, h, h, b_gate, att, yc, ys, wa, wc, ws, dm)


def _ffn_act_fwd(hh, cw, name):
    s, dff = hh.shape[0], hh.shape[1] // 2
    nblk = dff // LANES

    def body(a_ref, b_ref, wa_ref, wb_ref, z_ref):
        rows = lax.broadcasted_iota(jnp.int32, (s, LANES), 0)
        z_ref[...] = (_gelu(_conv3(a_ref[...], wa_ref, rows)) * _conv3(b_ref[...], wb_ref, rows)).astype(BF16)

    col = lambda base: pl.BlockSpec((s, LANES), lambda j: (0, base + j))
    wsp = lambda base: pl.BlockSpec((3, LANES), lambda j: (0, base + j))
    return pl.pallas_call(
        body, name=name, grid=(nblk,), in_specs=[col(0), col(nblk), wsp(0), wsp(nblk)], out_specs=col(0),
        out_shape=jax.ShapeDtypeStruct((s, dff), BF16), compiler_params=_params("parallel"),
    )(hh, hh, cw, cw)


def _ffn_act_bwd(hh, cw, dz, name):
    s, dff = hh.shape[0], hh.shape[1] // 2
    nblk = dff // LANES

    def body(a_ref, b_ref, wa_ref, wb_ref, dz_ref, da_ref, db_ref, dwa_ref, dwb_ref):
        rows = lax.broadcasted_iota(jnp.int32, (s, LANES), 0)
        a, b, wa, wb, dz = a_ref[...], b_ref[...], wa_ref, wb_ref, dz_ref[...]
        ga, gvjp = jax.vjp(_gelu, _conv3(a, wa, rows))
        dca = gvjp(dz * _conv3(b, wb, rows))[0]
        dcb = dz * ga
        da_ref[...] = _conv3_t(dca, wa, rows).astype(BF16)
        db_ref[...] = _conv3_t(dcb, wb, rows).astype(BF16)
        dwa_ref[...] = _conv3_dw(dca, a, rows)
        dwb_ref[...] = _conv3_dw(dcb, b, rows)

    col = lambda base: pl.BlockSpec((s, LANES), lambda j: (0, base + j))
    wsp = lambda base: pl.BlockSpec((3, LANES), lambda j: (0, base + j))
    w8 = lambda base: pl.BlockSpec((8, LANES), lambda j: (0, base + j))
    return pl.pallas_call(
        body, name=name, grid=(nblk,), in_specs=[col(0), col(nblk), wsp(0), wsp(nblk), col(0)],
        out_specs=[col(0), col(0), w8(0), w8(0)],
        out_shape=[jax.ShapeDtypeStruct((s, dff), BF16)] * 2 + [jax.ShapeDtypeStruct((8, dff), F32)] * 2,
        compiler_params=_params("parallel"),
    )(hh, hh, cw, cw, dz)


ANY = pl.BlockSpec(memory_space=pl.ANY)


def _place():
    return lax.axis_index("x"), lax.axis_index("y"), lax.axis_index("c")


def _all_gather(arrs, name):
    n = len(arrs)

    def body(*refs):
        ins, outs = refs[:n], refs[n:2 * n]
        send_sems, recv_sems, local_sems = refs[2 * n:]
        x, y, c = _place()
        me, sibling = (x, y, c), (x, y, 1 - c)
        chips = [(1 - x, y), (x, 1 - y), (1 - x, 1 - y)]

        def slab(a, dev):
            return outs[a].at[4 * dev[0] + 2 * dev[1] + dev[2]]

        def copy(a, k, block, to, src=None):
            return pltpu.make_async_remote_copy(
                src_ref=slab(a, block) if src is None else src, dst_ref=slab(a, block),
                send_sem=send_sems.at[7 * a + k], recv_sem=recv_sems.at[7 * a + k], device_id=to, device_id_type=MESH)

        mine = [pltpu.make_async_copy(ins[a], slab(a, me), local_sems.at[a]) for a in range(n)]
        for cp in mine:
            cp.start()
        first = []
        for a in range(n):
            first.append(copy(a, 0, me, sibling, src=ins[a]))
            first += [copy(a, 1 + j, me, (*chip, c), src=ins[a]) for j, chip in enumerate(chips)]
        for cp in first:
            cp.start()
        passed = []
        for a in range(n):
            for j, chip in enumerate(chips):
                copy(a, 1 + j, (*chip, c), me).wait_recv()
                fwd = copy(a, 4 + j, (*chip, c), sibling)
                fwd.start()
                passed.append(fwd)
        for a in range(n):
            copy(a, 0, sibling, me).wait_recv()
            for j, chip in enumerate(chips):
                copy(a, 4 + j, (*chip, 1 - c), me).wait_recv()
        for cp in first + passed:
            cp.wait_send()
        for cp in mine:
            cp.wait()

    return pl.pallas_call(
        body, name=name, in_specs=[ANY] * n, out_specs=[ANY] * n,
        out_shape=[jax.ShapeDtypeStruct((N_DEV,) + a.shape, a.dtype) for a in arrs],
        scratch_shapes=[pltpu.SemaphoreType.DMA((7 * n,)), pltpu.SemaphoreType.DMA((7 * n,)), pltpu.SemaphoreType.DMA((n,))],
    )(*arrs)


HBM = pl.BlockSpec(memory_space=pltpu.HBM)
SEM = pl.BlockSpec(memory_space=pltpu.SEMAPHORE)
EFFECT = pltpu.SideEffectType.DATAFLOW_SIDE_EFFECTING


def _slot(dev):
    return 4 * dev[0] + 2 * dev[1] + dev[2]


def _exchange_copies(src_refs, land_refs, send_sems, recv_sems, src_view, land_view):
    x, y, c = _place()
    me = (x, y, c)
    peers = [(1 - x if r & 4 else x, 1 - y if r & 2 else y, 1 - c if r & 1 else c) for r in range(1, N_DEV)]
    sends, lands = [], []
    for a, (src, land) in enumerate(zip(src_refs, land_refs)):
        for k, peer in enumerate(peers):
            sems = dict(send_sem=send_sems.at[7 * a + k], recv_sem=recv_sems.at[7 * a + k], device_id=peer,
                        device_id_type=MESH)
            sends.append(pltpu.make_async_remote_copy(src_ref=src_view(src, _slot(peer)),
                                                      dst_ref=land_view(land, _slot(me)), **sems))
            lands.append(pltpu.make_async_remote_copy(src_ref=src_view(src, _slot(me)),
                                                      dst_ref=land_view(land, _slot(peer)), **sems))
    return sends, lands


def _exchange_start(srcs, lands, after, src_view, land_view, name):
    n = len(srcs)

    def body(*refs):
        src_refs, land_refs = refs[:n], refs[n:2 * n]
        send_sems, recv_sems = refs[2 * n + 1], refs[2 * n + 2]
        token = refs[-1]
        sends, _ = _exchange_copies(src_refs, land_refs, send_sems, recv_sems, src_view, land_view)
        for cp in sends:
            cp.start()
        token[...] = jnp.zeros_like(token)

    thru = [pltpu.HBM(a.shape, a.dtype) for a in list(srcs) + list(lands)]
    outs = pl.pallas_call(
        body, name=name,
        out_shape=(pltpu.SemaphoreType.DMA((7 * n,)), pltpu.SemaphoreType.DMA((7 * n,)), *thru,
                   jax.ShapeDtypeStruct((8, LANES), F32)),
        in_specs=[HBM] * (2 * n) + [ANY],
        out_specs=(SEM, SEM, *([HBM] * (2 * n)), pl.BlockSpec(memory_space=pltpu.VMEM)),
        input_output_aliases={i: 2 + i for i in range(2 * n)},
        compiler_params=pltpu.CompilerParams(has_side_effects=EFFECT),
    )(*[pltpu.with_memory_space_constraint(a, pltpu.HBM) for a in list(srcs) + list(lands)], after)
    return outs[0], outs[1], list(outs[2:2 + n]), list(outs[2 + n:2 + 2 * n]), outs[-1]


def _exchange_wait(send_sems, recv_sems, srcs, lands, after, src_view, land_view, name):
    n = len(srcs)

    def body(*refs):
        src_refs, land_refs = refs[:n], refs[n:2 * n]
        sends, landed = _exchange_copies(src_refs, land_refs, refs[2 * n], refs[2 * n + 1], src_view, land_view)
        for cp in sends:
            cp.wait_send()
        for cp in landed:
            cp.wait_recv()

    outs = pl.pallas_call(
        body, name=name, out_shape=[pltpu.HBM(a.shape, a.dtype) for a in list(srcs) + list(lands)],
        in_specs=[HBM] * (2 * n) + [SEM, SEM, ANY], out_specs=[HBM] * (2 * n),
        input_output_aliases={i: i for i in range(2 * n)},
        compiler_params=pltpu.CompilerParams(has_side_effects=EFFECT),
    )(*srcs, *lands, send_sems, recv_sems, after)
    return list(outs[:n]), list(outs[n:])


ADAMW_BLOCK_BYTES = 1 << 19
PACK_ROWS = 256


def _adamw(slabs, w, m, v, name):
    nl, r, c = w.shape
    tr = r
    if r * c * 4 > ADAMW_BLOCK_BYTES:
        tr = 8
        for cand in range(8, r, 8):
            if r % cand == 0 and cand * c * 4 <= ADAMW_BLOCK_BYTES:
                tr = cand

    def body(s_ref, w_ref, m_ref, v_ref, g_ref, d_ref, nm_ref, nv_ref):
        g = s_ref[0].astype(F32)
        for q in range(1, N_DEV):
            g = g + s_ref[q].astype(F32)
        m_new = ADAM_B1 * m_ref[...] + (1.0 - ADAM_B1) * g
        v_new = ADAM_B2 * v_ref[...] + (1.0 - ADAM_B2) * (g * g)
        m_hat = m_new / (1.0 - ADAM_B1 ** ADAM_STEP)
        v_hat = v_new / (1.0 - ADAM_B2 ** ADAM_STEP)
        g_ref[...] = g
        d_ref[...] = -ADAM_LR * (m_hat / (jnp.sqrt(v_hat) + ADAM_EPS) + ADAM_WD * w_ref[...])
        nm_ref[...] = m_new
        nv_ref[...] = v_new

    blk = pl.BlockSpec((None, tr, c), lambda l, i: (l, i, 0))
    return pl.pallas_call(
        body, name=name, grid=(nl, r // tr),
        in_specs=[pl.BlockSpec((N_DEV, None, tr, c), lambda l, i: (0, l, i, 0)), blk, blk, blk],
        out_specs=[blk] * 4, out_shape=[jax.ShapeDtypeStruct(w.shape, F32)] * 4,
        compiler_params=_params("parallel", "parallel"),
    )(slabs, w, m, v)


def _layout(d):
    off = {"gate": 0, "q": 3 * d}
    off["bg"] = off["q"] + 3 * D_ATT
    off["u"] = off["bg"] + 3 * D_CONV
    off["f"] = off["u"] + 2 * D_SGU
    width = -(-(off["f"] + LANES) // 512) * 512
    return off, width


def _pad_w_in(w, d, token):
    off, width = _layout(d)
    nqkv, nrest = 3 * D_ATT, 3 * D_CONV + 2 * D_SGU
    pad = jnp.zeros((w.shape[0], width - off["f"] - N_HEADS), w.dtype) + token[0, 0].astype(w.dtype)
    return jnp.concatenate([w[:, nqkv + N_HEADS + nrest:], w[:, :nqkv], w[:, nqkv + N_HEADS:nqkv + N_HEADS + nrest],
                            w[:, nqkv:nqkv + N_HEADS], pad], axis=1)


def _unpad_w_in(wp, d):
    off, _ = _layout(d)
    return jnp.concatenate([wp[:, off["q"]:off["bg"]], wp[:, off["f"]:off["f"] + N_HEADS], wp[:, off["bg"]:off["f"]],
                            wp[:, :off["q"]]], axis=1)


def _cols_from_slabs(g):
    return jnp.transpose(g, (1, 0, 2)).reshape(g.shape[1], N_DEV * g.shape[2])


def _cols_to_slabs(w):
    r, c = w.shape[0], w.shape[1] // N_DEV
    return jnp.transpose(w.reshape(r, N_DEV, c), (1, 0, 2))


def kernel(x, pre_mix_g, post_mix_g, pre_ffn_g, post_ffn_g, w_in, b_forget, b_gate, conv_mix_w, sgu_ln_g, sgu_ln_b, sgu_w, sgu_b, w_branch_att, w_branch_conv, w_branch_sgu, w_out, w_ffn_up, conv_ffn_w, w_ffn_down, loss_target, m_pre_mix_g, m_post_mix_g, m_pre_ffn_g, m_post_ffn_g, m_w_in, m_b_forget, m_b_gate, m_conv_mix_w, m_sgu_ln_g, m_sgu_ln_b, m_sgu_w, m_sgu_b, m_w_branch_att, m_w_branch_conv, m_w_branch_sgu, m_w_out, m_w_ffn_up, m_conv_ffn_w, m_w_ffn_down, v_pre_mix_g, v_post_mix_g, v_pre_ffn_g, v_post_ffn_g, v_w_in, v_b_forget, v_b_gate, v_conv_mix_w, v_sgu_ln_g, v_sgu_ln_b, v_sgu_w, v_sgu_b, v_w_branch_att, v_w_branch_conv, v_w_branch_sgu, v_w_out, v_w_ffn_up, v_conv_ffn_w, v_w_ffn_down):
    depth = w_in.shape[0]
    s, d = x.shape[1], x.shape[2]
    dff = w_ffn_down.shape[1] * N_DEV
    off, _ = _layout(d)
    qblk, bgblk, ublk, fblk = off["q"] // LANES, off["bg"] // LANES, off["u"] // D_SGU, off["f"] // LANES
    x0 = x.reshape(s, d)
    target = loss_target.reshape(s, d)
    ncm, ncf = conv_mix_w.shape[2], conv_ffn_w.shape[2]

    me = 4 * lax.axis_index("x") + 2 * lax.axis_index("y") + lax.axis_index("c")
    lo = d // N_DEV

    whole = lambda ref, slot: ref
    slab = lambda ref, slot: ref.at[slot]

    def gather_start(l, after):
        small = jnp.concatenate([b_gate[l], conv_mix_w[l], conv_ffn_w[l]], axis=1)
        shards = [w_in[l].astype(BF16), w_branch_att[l].astype(BF16), w_branch_conv[l].astype(BF16),
                  w_branch_sgu[l].astype(BF16), w_out[l].astype(BF16), w_ffn_up[l].astype(BF16),
                  w_ffn_down[l].astype(BF16), small]
        lands = [lax.empty((N_DEV,) + a.shape, a.dtype) for a in shards]
        return _exchange_start(shards, lands, after, whole, slab, name=f"gather_start_{l}")

    def gather_finish(l, started, after):
        send_sems, recv_sems, shards, lands, _ = started
        shards, lands = _exchange_wait(send_sems, recv_sems, shards, lands, after, whole, slab, name=f"gather_wait_{l}")
        following = gather_start(l + 1, lands[0]) if l + 1 < depth else None
        token = following[4] if following else jnp.zeros((8, LANES), F32)
        g_in, g_a, g_c, g_s, g_o, g_up, g_dn, g_small = [
            lax.dynamic_update_index_in_dim(land, shard, me, 0) for land, shard in zip(lands, shards)]
        g_small = _cols_from_slabs(g_small).reshape(3, N_DEV, -1)
        return following, dict(
            w_in=_pad_w_in(_cols_from_slabs(g_in), d, token), wa=_cols_from_slabs(g_a), wc=_cols_from_slabs(g_c),
            ws=_cols_from_slabs(g_s), w_out=g_o.reshape(d, d), w_up=_cols_from_slabs(g_up), w_dn=g_dn.reshape(dff, d),
            b_gate=g_small[:, :, :lo].reshape(3, d), cmw=g_small[:, :, lo:lo + ncm].reshape(3, D_CONV),
            cfw=g_small[:, :, lo + ncm:].reshape(3, 2 * dff))

    def bfull(l):
        return jnp.repeat(jnp.transpose(sgu_b[l]), HEAD_DIM, axis=1)

    def bf_pad(l):
        return jnp.pad(b_forget[l], (0, LANES - N_HEADS)).reshape(1, LANES)

    saved = []
    weights = []
    xin = x0
    xn = _prenorm(x0, pre_mix_g[0:1], name="prenorm_first")
    loss_acc = dy = None
    started = gather_start(0, x0)
    for l in range(depth):
        started, w = gather_finish(l, started, xin)
        weights.append(w)
        h = _mm(xn, w["w_in"], "nn", F32, name="proj_in")
        c, ct = _forget_prep(h, bf_pad(l), fblk, name="forget_prep")
        att, lse = _attention_fwd(h, c, ct, qblk, name="attention_fwd")
        yc = _sconv_fwd(h, w["cmw"], bgblk, name="sconv_fwd")
        ys = _sgu_fwd(h, sgu_ln_g[l:l + 1], sgu_ln_b[l:l + 1], sgu_w[l], bfull(l), ublk, name="sgu_fwd")
        merged = _merge_fwd(h, w["b_gate"], att, yc, ys, w["wa"], w["wc"], w["ws"], name="merge_fwd")
        o = _mm(merged, w["w_out"], "nn", F32, name="proj_out")
        x1, xn2 = _postnorm(xin, o, post_mix_g[l:l + 1], pre_ffn_g[l:l + 1], name="postnorm_mix")
        hh = _mm(xn2, w["w_up"], "nn", F32, name="ffn_up")
        z = _ffn_act_fwd(hh, w["cfw"], name="ffn_act_fwd")
        f = _mm(z, w["w_dn"], "nn", F32, name="ffn_down")
        saved.append(dict(xin=xin, xn=xn, h=h, c=c, ct=ct, lse=lse, att=att, yc=yc, ys=ys, merged=merged, o=o, x1=x1,
                          xn2=xn2, hh=hh, z=z, f=f))
        if l + 1 < depth:
            xin, xn = _postnorm(x1, f, post_ffn_g[l:l + 1], pre_mix_g[l + 1:l + 2], name="postnorm_ffn")
        else:
            dy, loss_acc = _postnorm_loss(x1, f, post_ffn_g[l:l + 1], target, name="postnorm_loss")
    loss = lax.psum(loss_acc[0, 0] * (0.5 / d), ("x", "y", "c"))

    rep = {k: [None] * depth for k in ("pre_mix_g", "post_mix_g", "pre_ffn_g", "post_ffn_g", "b_forget", "sgu_ln_g",
                                       "sgu_ln_b", "sgu_w", "sgu_b")}
    nsmall = lo + ncm + ncf
    lands = [lax.empty((N_DEV, depth) + shp, dt) for shp, dt in (
        (w_in.shape[1:], BF16), (w_branch_att.shape[1:], BF16), (w_branch_conv.shape[1:], BF16),
        (w_branch_sgu.shape[1:], BF16), (w_out.shape[1:], BF16), (w_ffn_up.shape[1:], BF16),
        (w_ffn_down.shape[1:], BF16), ((3, nsmall), F32))]
    scatters = [None] * depth
    token = jnp.zeros((8, LANES), F32)
    dx = dy
    for l in reversed(range(depth)):
        w, a = weights[l], saved[l]
        df, rep["post_ffn_g"][l] = _postnorm_bwd(a["f"], post_ffn_g[l:l + 1] + token[0, 0], dx, name="postnorm_bwd")
        dz = _mm(df, w["w_dn"], "nt", F32, name="ffn_down_dx")
        g_dn = _mm(a["z"], df, "tn", BF16, name="ffn_down_dw")
        dha, dhb, dcwa, dcwb = _ffn_act_bwd(a["hh"], w["cfw"], dz, name="ffn_act_bwd")
        dhh = jnp.concatenate([dha, dhb], axis=1)
        dcfw = jnp.concatenate([dcwa[0:3], dcwb[0:3]], axis=1)
        dxn2 = _mm(dhh, w["w_up"], "nt", F32, name="ffn_up_dx")
        g_up = _mm(a["xn2"], dhh, "tn", BF16, name="ffn_up_dw")
        dx1, rep["pre_ffn_g"][l] = _prenorm_bwd(a["x1"], pre_ffn_g[l:l + 1], dxn2, dx, name="prenorm_bwd")
        do, rep["post_mix_g"][l] = _postnorm_bwd(a["o"], post_mix_g[l:l + 1], dx1, name="postnorm_bwd")
        dmerged = _mm(do, w["w_out"], "nt", F32, name="proj_out_dx")
        g_o = _mm(a["merged"], do, "tn", BF16, name="proj_out_dw")
        dya, dyc, dys, dgl, dbg = _merge_bwd(a["h"], w["b_gate"], a["att"], a["yc"], a["ys"], w["wa"], w["wc"], w["ws"],
                                             dmerged, name="merge_bwd")
        datt = _mm(dya, w["wa"], "nt", F32, name="branch_att_dx")
        dconv = _mm(dyc, w["wc"], "nt", F32, name="branch_conv_dx")
        dsgu = _mm(dys, w["ws"], "nt", F32, name="branch_sgu_dx")
        g_a = _mm(a["att"], dya, "tn", BF16, name="branch_att_dw")
        g_c = _mm(a["yc"], dyc, "tn", BF16, name="branch_conv_dw")
        g_s = _mm(a["ys"], dys, "tn", BF16, name="branch_sgu_dw")
        dq, dk, dv, dct4, dcq4 = _attention_bwd(a["h"], a["c"], a["ct"], a["lse"], a["att"], datt, qblk,
                                                name="attention_bwd")
        dcq = jnp.pad(jnp.transpose(dcq4[:, :, 0:2], (1, 0, 2)).reshape(s, N_HEADS), ((0, 0), (0, LANES - N_HEADS)))
        dfl, dbf = _forget_prep_bwd(a["h"], bf_pad(l), dct4[:, 0:2, :].reshape(N_HEADS, s), dcq, fblk,
                                    name="forget_prep_bwd")
        rep["b_forget"][l] = dbf[0, :N_HEADS]
        dbgate, dcg, dhc, dcmw = _sconv_bwd(a["h"], w["cmw"], dconv, bgblk, name="sconv_bwd")
        du, dvs, dlg, dlb, dsw, dbfull = _sgu_bwd(a["h"], sgu_ln_g[l:l + 1], sgu_ln_b[l:l + 1], sgu_w[l], bfull(l), dsgu,
                                                  ublk, name="sgu_bwd")
        rep["sgu_ln_g"][l], rep["sgu_ln_b"][l], rep["sgu_w"][l] = dlg, dlb, dsw
        rep["sgu_b"][l] = jnp.transpose(jnp.sum(dbfull.reshape(SGU_CHUNK, N_SGU_GROUPS, HEAD_DIM), axis=2))
        dh = jnp.concatenate([dgl[0], dgl[1], dgl[2], dq.astype(BF16), dk.astype(BF16), dv.astype(BF16), dbgate, dcg, dhc,
                              du, dvs, dfl.astype(BF16), jnp.zeros((s, w["w_in"].shape[1] - off["f"] - LANES), BF16)], axis=1)
        dxn = _mm(dh, w["w_in"], "nt", F32, name="proj_in_dx")
        g_in = _mm(a["xn"], dh, "tn", BF16, name="proj_in_dw")
        dx, rep["pre_mix_g"][l] = _prenorm_bwd(a["xin"], pre_mix_g[l:l + 1], dxn, dx1, name="prenorm_bwd")
        sends = [_cols_to_slabs(_unpad_w_in(g_in, d)), _cols_to_slabs(g_a), _cols_to_slabs(g_c), _cols_to_slabs(g_s),
                 g_o.reshape(N_DEV, d // N_DEV, d), _cols_to_slabs(g_up), g_dn.reshape(N_DEV, dff // N_DEV, d),
                 jnp.concatenate([_cols_to_slabs(dbg), _cols_to_slabs(dcmw[0:3]), _cols_to_slabs(dcfw)], axis=2)]
        layer_slab = lambda ref, slot, l=l: ref.at[slot, l]
        send_sems, recv_sems, sends, lands, token = _exchange_start(sends, lands, dx, slab, layer_slab,
                                                                    name=f"scatter_start_{l}")
        scatters[l] = (send_sems, recv_sems, sends, layer_slab)

    for l in range(depth):
        send_sems, recv_sems, sends, layer_slab = scatters[l]
        sends, lands = _exchange_wait(send_sems, recv_sems, sends, lands, dx, slab, layer_slab, name=f"scatter_wait_{l}")
        scatters[l] = sends
    for l in range(depth):
        lands = [lax.dynamic_update_slice(land, lax.dynamic_index_in_dim(own, me, 0, keepdims=True)[:, None],
                                          (me, l) + (0,) * (land.ndim - 2)) for land, own in zip(lands, scatters[l])]
    big = dict(zip(("w_in", "wa", "wc", "ws", "w_out", "w_up", "w_dn", "small"), lands))

    outs = {}

    def update(name_, slabs, w_, m_, v_):
        shp = w_.shape
        w3 = w_.reshape((shp[0], -1, shp[-1])) if w_.ndim >= 3 else w_.reshape((1,) + shp)
        g, dl, nm, nv = _adamw(slabs.reshape((N_DEV,) + w3.shape), w3, m_.reshape(w3.shape), v_.reshape(w3.shape),
                               name="adamw_" + name_)
        outs[name_] = tuple(t.reshape(shp) for t in (g, dl, nm, nv))

    update("w_in", big["w_in"], w_in, m_w_in, v_w_in)
    update("w_branch_att", big["wa"], w_branch_att, m_w_branch_att, v_w_branch_att)
    update("w_branch_conv", big["wc"], w_branch_conv, m_w_branch_conv, v_w_branch_conv)
    update("w_branch_sgu", big["ws"], w_branch_sgu, m_w_branch_sgu, v_w_branch_sgu)
    update("w_out", big["w_out"], w_out, m_w_out, v_w_out)
    update("w_ffn_up", big["w_up"], w_ffn_up, m_w_ffn_up, v_w_ffn_up)
    update("w_ffn_down", big["w_dn"], w_ffn_down, m_w_ffn_down, v_w_ffn_down)
    small = big["small"]
    update("b_gate", small[..., :lo], b_gate, m_b_gate, v_b_gate)
    update("conv_mix_w", small[..., lo:lo + ncm], conv_mix_w, m_conv_mix_w, v_conv_mix_w)
    update("conv_ffn_w", small[..., lo + ncm:], conv_ffn_w, m_conv_ffn_w, v_conv_ffn_w)

    rep_names = ("pre_mix_g", "post_mix_g", "pre_ffn_g", "post_ffn_g", "b_forget", "sgu_ln_g", "sgu_ln_b", "sgu_w", "sgu_b")
    rep_w = dict(pre_mix_g=(pre_mix_g, m_pre_mix_g, v_pre_mix_g), post_mix_g=(post_mix_g, m_post_mix_g, v_post_mix_g),
                 pre_ffn_g=(pre_ffn_g, m_pre_ffn_g, v_pre_ffn_g), post_ffn_g=(post_ffn_g, m_post_ffn_g, v_post_ffn_g),
                 b_forget=(b_forget, m_b_forget, v_b_forget), sgu_ln_g=(sgu_ln_g, m_sgu_ln_g, v_sgu_ln_g),
                 sgu_ln_b=(sgu_ln_b, m_sgu_ln_b, v_sgu_ln_b), sgu_w=(sgu_w, m_sgu_w, v_sgu_w), sgu_b=(sgu_b, m_sgu_b, v_sgu_b))

    def pack(parts):
        flat = jnp.concatenate([p.reshape(-1) for p in parts])
        return jnp.pad(flat, (0, -flat.shape[0] % (PACK_ROWS * LANES))).reshape(-1, LANES)

    part = pack([jnp.stack([g.reshape(rep_w[k][0].shape[1:]) for g in rep[k]]) for k in rep_names])
    (gathered,) = _all_gather([part], name="gather_small_grads")
    packed = [pack([rep_w[k][i] for k in rep_names]) for i in range(3)]
    res = _adamw(gathered.reshape(N_DEV, 1, -1, LANES), *[p.reshape(1, -1, LANES) for p in packed], name="adamw_replicated")
    pos = 0
    for k in rep_names:
        shp = rep_w[k][0].shape
        size = math.prod(shp)
        outs[k] = tuple(t.reshape(-1)[pos:pos + size].reshape(shp) for t in res)
        pos += size

    order = ("pre_mix_g", "post_mix_g", "pre_ffn_g", "post_ffn_g", "w_in", "b_forget", "b_gate", "conv_mix_w", "sgu_ln_g",
             "sgu_ln_b", "sgu_w", "sgu_b", "w_branch_att", "w_branch_conv", "w_branch_sgu", "w_out", "w_ffn_up",
             "conv_ffn_w", "w_ffn_down")
    grad_x = dx.reshape(x.shape)
    return (loss, grad_x, *[outs[k][0] for k in order], *[outs[k][1] for k in order], *[outs[k][2] for k in order],
            *[outs[k][3] for k in order])
```

```python
import functools
import math

import jax
import jax.numpy as jnp
from jax import lax
from jax.experimental import pallas as pl
from jax.experimental.pallas import tpu as pltpu

F32 = jnp.float32
BF16 = jnp.bfloat16

N_DEV = 8
HEAD_DIM = 64
N_HEADS = 8
D_ATT = 512
D_CONV = 256
D_SGU = 256
N_SGU_GROUPS = 4
SGU_CHUNK = 128
RMS_EPS = 1e-6
LN_EPS = 1e-5
ADAM_LR = 0.001
ADAM_B1 = 0.9
ADAM_B2 = 0.999
ADAM_EPS = 1e-08
ADAM_WD = 0.01
ADAM_STEP = 10
LANES = 128
VMEM_LIMIT = 56 * 1024 * 1024
NEG = -1e30
MESH = pl.DeviceIdType.MESH


def _params(*sem):
    return pltpu.CompilerParams(dimension_semantics=sem if sem else None, vmem_limit_bytes=VMEM_LIMIT)


def _tile(n, cap):
    if n <= cap:
        return n
    t = cap - cap % LANES
    while n % t:
        t -= LANES
    return t


def _gelu(x):
    return 0.5 * x * (1.0 + jnp.tanh(math.sqrt(2.0 / math.pi) * (x + 0.044715 * (x * x * x))))


def _rms(x, g):
    r = lax.rsqrt(jnp.mean(x * x, axis=-1, keepdims=True) + RMS_EPS)
    return x * r * g


def _layer_norm(x, g, b):
    mu = jnp.mean(x, axis=-1, keepdims=True)
    xc = x - mu
    var = jnp.mean(xc * xc, axis=-1, keepdims=True)
    return xc * lax.rsqrt(var + LN_EPS) * g + b


def _shift_down(x, k, rows):
    return jnp.where(rows >= k, pltpu.roll(x, k, 0), 0.0)


def _shift_up(x, k, rows):
    s = x.shape[0]
    return jnp.where(rows < s - k, pltpu.roll(x, s - k, 0), 0.0)


def _conv3(x, w_ref, rows):
    return w_ref[2:3, :] * x + w_ref[1:2, :] * _shift_down(x, 1, rows) + w_ref[0:1, :] * _shift_down(x, 2, rows)


def _conv3_t(dy, w_ref, rows):
    return w_ref[2:3, :] * dy + w_ref[1:2, :] * _shift_up(dy, 1, rows) + w_ref[0:1, :] * _shift_up(dy, 2, rows)


def _conv3_dw(dy, x, rows):
    d2 = jnp.sum(dy * x, axis=0, keepdims=True)
    d1 = jnp.sum(dy * _shift_down(x, 1, rows), axis=0, keepdims=True)
    d0 = jnp.sum(dy * _shift_down(x, 2, rows), axis=0, keepdims=True)
    sub = lax.broadcasted_iota(jnp.int32, (8, x.shape[1]), 0)
    return jnp.where(sub == 0, d0, jnp.where(sub == 1, d1, jnp.where(sub == 2, d2, 0.0)))


MM_VMEM_BUDGET = 40 * 1024 * 1024
MM_TILE_CAP = 1408


def _mm_tiles(m, n, k, out_bytes):
    def edges(d):
        return [t for t in range(LANES, min(d, MM_TILE_CAP) + 1, LANES) if d % t == 0] or [d]

    best = None
    for tm in edges(m):
        for tn in edges(n):
            if 2 * (2 * k * (tm + tn) + tm * tn * out_bytes) > MM_VMEM_BUDGET:
                continue
            for a_outer in (True, False):
                reads = k * m + (m // tm) * k * n if a_outer else k * n + (n // tn) * k * m
                traffic = 2 * reads + m * n * out_bytes
                key = (traffic, -tm * tn)
                if best is None or key < best[0]:
                    best = (key, (tm, tn, a_outer))
    return best[1]


def _mm(a, b, form, out_dtype, name):
    if form == "nn":
        (m, k), n = a.shape, b.shape[1]
    elif form == "nt":
        (m, k), n = a.shape, b.shape[0]
    else:
        (k, m), n = a.shape, b.shape[1]
    tm, tn, a_outer = _mm_tiles(m, n, k, jnp.dtype(out_dtype).itemsize)
    dims = {"nn": (((1,), (0,)), ((), ())), "nt": (((1,), (1,)), ((), ())), "tn": (((0,), (0,)), ((), ()))}[form]

    def body(a_ref, b_ref, o_ref):
        o_ref[...] = lax.dot_general(a_ref[...], b_ref[...], dims, preferred_element_type=F32).astype(o_ref.dtype)

    ij = (lambda g0, g1: (g0, g1)) if a_outer else (lambda g0, g1: (g1, g0))
    a_spec = (pl.BlockSpec((k, tm), lambda g0, g1: (0, ij(g0, g1)[0])) if form == "tn"
              else pl.BlockSpec((tm, k), lambda g0, g1: (ij(g0, g1)[0], 0)))
    b_spec = (pl.BlockSpec((tn, k), lambda g0, g1: (ij(g0, g1)[1], 0)) if form == "nt"
              else pl.BlockSpec((k, tn), lambda g0, g1: (0, ij(g0, g1)[1])))
    return pl.pallas_call(
        body, name=name, grid=(m // tm, n // tn) if a_outer else (n // tn, m // tm),
        in_specs=[a_spec, b_spec], out_specs=pl.BlockSpec((tm, tn), lambda g0, g1: ij(g0, g1)),
        out_shape=jax.ShapeDtypeStruct((m, n), out_dtype),
        compiler_params=_params("parallel", "arbitrary"),
    )(a, b)


def _prenorm(x, g, name):
    s, d = x.shape
    tm = _tile(s, 512)

    def body(x_ref, g_ref, o_ref):
        o_ref[...] = _rms(x_ref[...], g_ref[...]).astype(BF16)

    return pl.pallas_call(
        body, name=name, grid=(s // tm,),
        in_specs=[pl.BlockSpec((tm, d), lambda i: (i, 0)), pl.BlockSpec((1, d), lambda i: (0, 0))],
        out_specs=pl.BlockSpec((tm, d), lambda i: (i, 0)),
        out_shape=jax.ShapeDtypeStruct((s, d), BF16), compiler_params=_params("parallel"),
    )(x, g)


def _postnorm(x, o, g_post, g_next, name):
    s, d = x.shape
    tm = _tile(s, 512)

    def body(x_ref, o_ref, gp_ref, gn_ref, x1_ref, xn_ref):
        x1 = x_ref[...] + _rms(o_ref[...], gp_ref[...])
        x1_ref[...] = x1
        xn_ref[...] = _rms(x1, gn_ref[...]).astype(BF16)

    row = pl.BlockSpec((tm, d), lambda i: (i, 0))
    vec = pl.BlockSpec((1, d), lambda i: (0, 0))
    return pl.pallas_call(
        body, name=name, grid=(s // tm,), in_specs=[row, row, vec, vec], out_specs=[row, row],
        out_shape=[jax.ShapeDtypeStruct((s, d), F32), jax.ShapeDtypeStruct((s, d), BF16)],
        compiler_params=_params("parallel"),
    )(x, o, g_post, g_next)


def _postnorm_loss(x, o, g_post, target, name):
    s, d = x.shape
    tm = _tile(s, 512)

    def body(x_ref, o_ref, gp_ref, t_ref, dy_ref, acc_ref):
        e = x_ref[...] + _rms(o_ref[...], gp_ref[...]) - t_ref[...]
        dy_ref[...] = e / d

        @pl.when(pl.program_id(0) == 0)
        def _():
            acc_ref[...] = jnp.zeros_like(acc_ref)

        acc_ref[...] += jnp.sum(jnp.sum(e * e, axis=1, keepdims=True), axis=0, keepdims=True)

    row = pl.BlockSpec((tm, d), lambda i: (i, 0))
    return pl.pallas_call(
        body, name=name, grid=(s // tm,),
        in_specs=[row, row, pl.BlockSpec((1, d), lambda i: (0, 0)), row],
        out_specs=[row, pl.BlockSpec((1, LANES), lambda i: (0, 0))],
        out_shape=[jax.ShapeDtypeStruct((s, d), F32), jax.ShapeDtypeStruct((1, LANES), F32)],
        compiler_params=_params("arbitrary"),
    )(x, o, g_post, target)


def _postnorm_bwd(o, g, dx, name):
    s, d = o.shape
    tm = _tile(s, 512)

    def body(o_ref, g_ref, dx_ref, do_ref, dg_ref):
        _, vjp = jax.vjp(_rms, o_ref[...], g_ref[...])
        d_o, dg = vjp(dx_ref[...])
        do_ref[...] = d_o.astype(BF16)

        @pl.when(pl.program_id(0) == 0)
        def _():
            dg_ref[...] = jnp.zeros_like(dg_ref)

        dg_ref[...] += dg

    row = pl.BlockSpec((tm, d), lambda i: (i, 0))
    vec = pl.BlockSpec((1, d), lambda i: (0, 0))
    return pl.pallas_call(
        body, name=name, grid=(s // tm,), in_specs=[row, vec, row], out_specs=[row, vec],
        out_shape=[jax.ShapeDtypeStruct((s, d), BF16), jax.ShapeDtypeStruct((1, d), F32)],
        compiler_params=_params("arbitrary"),
    )(o, g, dx)


def _prenorm_bwd(x, g, dxn, dres, name):
    s, d = x.shape
    tm = _tile(s, 512)

    def body(x_ref, g_ref, dxn_ref, dres_ref, dx_ref, dg_ref):
        _, vjp = jax.vjp(_rms, x_ref[...], g_ref[...])
        dx, dg = vjp(dxn_ref[...])
        dx_ref[...] = dres_ref[...] + dx

        @pl.when(pl.program_id(0) == 0)
        def _():
            dg_ref[...] = jnp.zeros_like(dg_ref)

        dg_ref[...] += dg

    row = pl.BlockSpec((tm, d), lambda i: (i, 0))
    vec = pl.BlockSpec((1, d), lambda i: (0, 0))
    return pl.pallas_call(
        body, name=name, grid=(s // tm,), in_specs=[row, vec, row, row], out_specs=[row, vec],
        out_shape=[jax.ShapeDtypeStruct((s, d), F32), jax.ShapeDtypeStruct((1, d), F32)],
        compiler_params=_params("arbitrary"),
    )(x, g, dxn, dres)


def _log_sigmoid(z):
    return jnp.minimum(z, 0.0) - jnp.log(1.0 + jnp.exp(-jnp.abs(z)))


def _forget_prep(h, bf_pad, fblk, name):
    s = h.shape[0]

    def body(f_ref, b_ref, c_ref, ct_ref):
        c = _log_sigmoid(f_ref[...] + b_ref[...])
        rows = lax.broadcasted_iota(jnp.int32, c.shape, 0)
        k = 1
        while k < s:
            c = c + _shift_down(c, k, rows)
            k *= 2
        c_ref[...] = c
        ct_ref[...] = jnp.transpose(c)[0:8, :]

    return pl.pallas_call(
        body, name=name, grid=(1,),
        in_specs=[pl.BlockSpec((s, LANES), lambda i: (0, fblk)), pl.BlockSpec((1, LANES), lambda i: (0, 0))],
        out_specs=[pl.BlockSpec((s, LANES), lambda i: (0, 0)), pl.BlockSpec((8, s), lambda i: (0, 0))],
        out_shape=[jax.ShapeDtypeStruct((s, LANES), F32), jax.ShapeDtypeStruct((8, s), F32)],
        compiler_params=_params("arbitrary"),
    )(h, bf_pad)


def _forget_prep_bwd(h, bf_pad, dct, dcq, fblk, name):
    s = h.shape[0]

    def body(f_ref, b_ref, dct_ref, dcq_ref, df_ref, db_ref):
        dc = dcq_ref[...] + jnp.transpose(jnp.concatenate([dct_ref[...], jnp.zeros((LANES - 8, s), F32)], axis=0))
        rows = lax.broadcasted_iota(jnp.int32, dc.shape, 0)
        k = 1
        while k < s:
            dc = dc + _shift_up(dc, k, rows)
            k *= 2
        z = f_ref[...] + b_ref[...]
        lane = lax.broadcasted_iota(jnp.int32, dc.shape, 1)
        df = jnp.where(lane < N_HEADS, dc * jax.nn.sigmoid(-z), 0.0)
        df_ref[...] = df
        db_ref[...] = jnp.sum(df, axis=0, keepdims=True)

    return pl.pallas_call(
        body, name=name, grid=(1,),
        in_specs=[pl.BlockSpec((s, LANES), lambda i: (0, fblk)), pl.BlockSpec((1, LANES), lambda i: (0, 0)),
                  pl.BlockSpec((8, s), lambda i: (0, 0)), pl.BlockSpec((s, LANES), lambda i: (0, 0))],
        out_specs=[pl.BlockSpec((s, LANES), lambda i: (0, 0)), pl.BlockSpec((1, LANES), lambda i: (0, 0))],
        out_shape=[jax.ShapeDtypeStruct((s, LANES), F32), jax.ShapeDtypeStruct((1, LANES), F32)],
        compiler_params=_params("arbitrary"),
    )(h, bf_pad, dct, dcq)


def _pick_lane(blk, idx):
    lane = lax.broadcasted_iota(jnp.int32, blk.shape, 1)
    return jnp.sum(jnp.where(lane == idx, blk, 0.0), axis=1, keepdims=True)


def _pick_row(blk, idx):
    sub = lax.broadcasted_iota(jnp.int32, blk.shape, 0)
    return jnp.sum(jnp.where(sub == idx, blk, 0.0), axis=0, keepdims=True)


def _attention_fwd(h, c, ct, qblk, name):
    s = h.shape[0]
    t = _tile(s, 512)
    nq = s // t
    scale = HEAD_DIM ** -0.5
    nt_dims = (((1,), (1,)), ((), ()))

    def body(q_ref, k_ref, v_ref, c_ref, ct_ref, o_ref, lse_ref):
        p = pl.program_id(0)
        i = pl.program_id(1)
        lane = lax.broadcasted_iota(jnp.int32, (1, LANES), 1)
        first = lane < HEAD_DIM
        q = q_ref[...] * scale
        qa = jnp.where(first, q, 0.0).astype(BF16)
        qb = jnp.where(first, 0.0, q).astype(BF16)
        cblk = c_ref[...]
        cta = _pick_lane(cblk, 2 * p)
        ctb = _pick_lane(cblk, 2 * p + 1)

        def step(j, carry, diagonal):
            ma, la, mb, lb, acc = carry
            off = pl.multiple_of(j * t, t)
            k = k_ref[pl.ds(off, t), :].astype(BF16)
            v = v_ref[pl.ds(off, t), :].astype(BF16)
            crow = ct_ref[:, pl.ds(off, t)]

            def one(qh, cth, hd, m_old, l_old):
                sc = lax.dot_general(qh, k, nt_dims, preferred_element_type=F32) - _pick_row(crow, hd)
                if diagonal:
                    keep = lax.broadcasted_iota(jnp.int32, (t, t), 0) >= lax.broadcasted_iota(jnp.int32, (t, t), 1)
                    sc = jnp.where(keep, sc, NEG)
                m_new = jnp.maximum(m_old, jnp.max(sc, axis=1, keepdims=True) + cth)
                pr = jnp.exp(sc - (m_new - cth))
                alpha = jnp.exp(m_old - m_new)
                l_new = alpha * l_old + jnp.sum(pr, axis=1, keepdims=True)
                pv = jnp.dot(pr.astype(BF16), v, preferred_element_type=F32)
                return m_new, l_new, alpha, pv

            ma2, la2, aa, pva = one(qa, cta, 2 * p, ma, la)
            mb2, lb2, ab, pvb = one(qb, ctb, 2 * p + 1, mb, lb)
            acc = jnp.where(first, aa * acc + pva, ab * acc + pvb)
            return ma2, la2, mb2, lb2, acc

        init = (jnp.full((t, 1), NEG, F32), jnp.zeros((t, 1), F32), jnp.full((t, 1), NEG, F32),
                jnp.zeros((t, 1), F32), jnp.zeros((t, LANES), F32))
        carry = lax.fori_loop(0, i, lambda j, carry: step(j, carry, False), init)
        ma, la, mb, lb, acc = step(i, carry, True)
        o_ref[...] = (acc / jnp.where(first, la, lb)).astype(BF16)
        lse_ref[0] = jnp.broadcast_to(ma + jnp.log(la), (t, LANES))
        lse_ref[1] = jnp.broadcast_to(mb + jnp.log(lb), (t, LANES))

    return pl.pallas_call(
        body, name=name, grid=(N_HEADS // 2, nq),
        in_specs=[pl.BlockSpec((t, LANES), lambda p, i: (i, qblk + p)),
                  pl.BlockSpec((s, LANES), lambda p, i: (0, qblk + 4 + p)),
                  pl.BlockSpec((s, LANES), lambda p, i: (0, qblk + 8 + p)),
                  pl.BlockSpec((t, LANES), lambda p, i: (i, 0)),
                  pl.BlockSpec((8, s), lambda p, i: (0, 0))],
        out_specs=[pl.BlockSpec((t, LANES), lambda p, i: (i, p)),
                   pl.BlockSpec((2, t, LANES), lambda p, i: (p, i, 0))],
        out_shape=[jax.ShapeDtypeStruct((s, D_ATT), BF16), jax.ShapeDtypeStruct((N_HEADS, s, LANES), F32)],
        compiler_params=_params("parallel", "arbitrary"),
    )(h, h, h, c, ct)


def _attention_bwd(h, c, ct, lse, att, datt, qblk, name):
    s = h.shape[0]
    t = _tile(s, 512)
    nq = s // t
    scale = HEAD_DIM ** -0.5
    nt_dims = (((1,), (1,)), ((), ()))
    tn_dims = (((0,), (0,)), ((), ()))

    def body(q_ref, k_ref, v_ref, c_ref, ct_ref, lse_ref, o_ref, do_ref, dq_ref, dk_ref, dv_ref, dct_ref, dcq_ref):
        p = pl.program_id(0)
        j = pl.program_id(1)
        lane = lax.broadcasted_iota(jnp.int32, (1, LANES), 1)
        first = lane < HEAD_DIM
        kf = k_ref[...]
        vf = v_ref[...]
        k = kf.astype(BF16)
        ka = jnp.where(first, kf, 0.0).astype(BF16)
        kb = jnp.where(first, 0.0, kf).astype(BF16)
        va = jnp.where(first, vf, 0.0).astype(BF16)
        vb = jnp.where(first, 0.0, vf).astype(BF16)
        crow = ct_ref[...]
        csa = _pick_row(crow, 2 * p)
        csb = _pick_row(crow, 2 * p + 1)

        @pl.when(j == 0)
        def _():
            dq_ref[...] = jnp.zeros_like(dq_ref)
            dcq_ref[...] = jnp.zeros_like(dcq_ref)

        def step(i, carry, diagonal):
            dka, dkb, dva, dvb, dca, dcb = carry
            off = pl.multiple_of(i * t, t)
            rows = pl.ds(off, t)
            q = (q_ref[rows, :] * scale).astype(BF16)
            dof = do_ref[rows, :]
            do = dof.astype(BF16)
            prod = dof * o_ref[rows, :].astype(F32)
            cblk = c_ref[rows, :]

            def one(kh, vh, hd, csh, lse_h):
                sc = lax.dot_general(q, kh, nt_dims, preferred_element_type=F32) - csh
                if diagonal:
                    keep = lax.broadcasted_iota(jnp.int32, (t, t), 0) >= lax.broadcasted_iota(jnp.int32, (t, t), 1)
                    sc = jnp.where(keep, sc, NEG)
                pr = jnp.exp(sc - (jnp.max(lse_h, axis=1, keepdims=True) - _pick_lane(cblk, hd)))
                dp = lax.dot_general(do, vh, nt_dims, preferred_element_type=F32)
                return pr, dp

            pra, dpa = one(ka, va, 2 * p, csa, lse_ref[0, rows, :])
            prb, dpb = one(kb, vb, 2 * p + 1, csb, lse_ref[1, rows, :])
            dela = jnp.sum(jnp.where(first, prod, 0.0), axis=1, keepdims=True)
            delb = jnp.sum(jnp.where(first, 0.0, prod), axis=1, keepdims=True)
            dsa = pra * (dpa - dela)
            dsb = prb * (dpb - delb)
            dsa16 = dsa.astype(BF16)
            dsb16 = dsb.astype(BF16)
            dva = dva + lax.dot_general(pra.astype(BF16), do, tn_dims, preferred_element_type=F32)
            dvb = dvb + lax.dot_general(prb.astype(BF16), do, tn_dims, preferred_element_type=F32)
            dka = dka + lax.dot_general(dsa16, q, tn_dims, preferred_element_type=F32)
            dkb = dkb + lax.dot_general(dsb16, q, tn_dims, preferred_element_type=F32)
            dqa = jnp.dot(dsa16, k, preferred_element_type=F32)
            dqb = jnp.dot(dsb16, k, preferred_element_type=F32)
            dq_ref[rows, :] += scale * jnp.where(first, dqa, dqb)
            dca = dca - jnp.sum(dsa, axis=0, keepdims=True)
            dcb = dcb - jnp.sum(dsb, axis=0, keepdims=True)
            dcq_ref[rows, :] += jnp.where(lane == 0, jnp.sum(dsa, axis=1, keepdims=True),
                                          jnp.where(lane == 1, jnp.sum(dsb, axis=1, keepdims=True), 0.0))
            return dka, dkb, dva, dvb, dca, dcb

        z = jnp.zeros((t, LANES), F32)
        zr = jnp.zeros((1, t), F32)
        carry = step(j, (z, z, z, z, zr, zr), True)
        dka, dkb, dva, dvb, dca, dcb = lax.fori_loop(j + 1, nq, lambda i, carry: step(i, carry, False), carry)
        dk_ref[...] = jnp.where(first, dka, dkb)
        dv_ref[...] = jnp.where(first, dva, dvb)
        sub = lax.broadcasted_iota(jnp.int32, (8, t), 0)
        dct_ref[...] = jnp.where(sub == 0, dca, jnp.where(sub == 1, dcb, 0.0))

    full = lambda blk: pl.BlockSpec((s, LANES), blk)
    return pl.pallas_call(
        body, name=name, grid=(N_HEADS // 2, nq),
        in_specs=[full(lambda p, j: (0, qblk + p)),
                  pl.BlockSpec((t, LANES), lambda p, j: (j, qblk + 4 + p)),
                  pl.BlockSpec((t, LANES), lambda p, j: (j, qblk + 8 + p)),
                  full(lambda p, j: (0, 0)),
                  pl.BlockSpec((8, t), lambda p, j: (0, j)),
                  pl.BlockSpec((2, s, LANES), lambda p, j: (p, 0, 0)),
                  full(lambda p, j: (0, p)),
                  full(lambda p, j: (0, p))],
        out_specs=[full(lambda p, j: (0, p)),
                   pl.BlockSpec((t, LANES), lambda p, j: (j, p)),
                   pl.BlockSpec((t, LANES), lambda p, j: (j, p)),
                   pl.BlockSpec((None, 8, t), lambda p, j: (p, 0, j)),
                   pl.BlockSpec((None, s, LANES), lambda p, j: (p, 0, 0))],
        out_shape=[jax.ShapeDtypeStruct((s, D_ATT), F32), jax.ShapeDtypeStruct((s, D_ATT), F32),
                   jax.ShapeDtypeStruct((s, D_ATT), F32), jax.ShapeDtypeStruct((N_HEADS // 2, 8, s), F32),
                   jax.ShapeDtypeStruct((N_HEADS // 2, s, LANES), F32)],
        compiler_params=_params("arbitrary", "arbitrary"),
    )(h, h, h, c, ct, lse, att, datt)


def _sconv_fwd(h, w, bgblk, name):
    s = h.shape[0]
    nblk = D_CONV // LANES

    def body(bg_ref, cg_ref, hc_ref, w_ref, y_ref):
        rows = lax.broadcasted_iota(jnp.int32, (s, LANES), 0)
        y_ref[...] = (bg_ref[...] * _conv3(cg_ref[...] * hc_ref[...], w_ref, rows)).astype(BF16)

    col = lambda base: pl.BlockSpec((s, LANES), lambda j: (0, base + j))
    return pl.pallas_call(
        body, name=name, grid=(nblk,),
        in_specs=[col(bgblk), col(bgblk + nblk), col(bgblk + 2 * nblk), pl.BlockSpec((3, LANES), lambda j: (0, j))],
        out_specs=pl.BlockSpec((s, LANES), lambda j: (0, j)),
        out_shape=jax.ShapeDtypeStruct((s, D_CONV), BF16), compiler_params=_params("parallel"),
    )(h, h, h, w)


def _sconv_bwd(h, w, dy, bgblk, name):
    s = h.shape[0]
    nblk = D_CONV // LANES

    def body(bg_ref, cg_ref, hc_ref, w_ref, dy_ref, dbg_ref, dcg_ref, dhc_ref, dw_ref):
        rows = lax.broadcasted_iota(jnp.int32, (s, LANES), 0)
        cg, hc, dy, w = cg_ref[...], hc_ref[...], dy_ref[...], w_ref
        xin = cg * hc
        dbg_ref[...] = (dy * _conv3(xin, w, rows)).astype(BF16)
        dconv = dy * bg_ref[...]
        dxin = _conv3_t(dconv, w, rows)
        dcg_ref[...] = (dxin * hc).astype(BF16)
        dhc_ref[...] = (dxin * cg).astype(BF16)
        dw_ref[...] = _conv3_dw(dconv, xin, rows)

    col = lambda base: pl.BlockSpec((s, LANES), lambda j: (0, base + j))
    return pl.pallas_call(
        body, name=name, grid=(nblk,),
        in_specs=[col(bgblk), col(bgblk + nblk), col(bgblk + 2 * nblk), pl.BlockSpec((3, LANES), lambda j: (0, j)), col(0)],
        out_specs=[col(0), col(0), col(0), pl.BlockSpec((8, LANES), lambda j: (0, j))],
        out_shape=[jax.ShapeDtypeStruct((s, D_CONV), BF16)] * 3 + [jax.ShapeDtypeStruct((8, D_CONV), F32)],
        compiler_params=_params("parallel"),
    )(h, h, h, w, dy)


def _sgu_group_masks():
    lane = lax.broadcasted_iota(jnp.int32, (1, D_SGU), 1)
    return [(lane // HEAD_DIM) == g for g in range(N_SGU_GROUPS)]


def _sgu_tril():
    r = lax.broadcasted_iota(jnp.int32, (SGU_CHUNK, SGU_CHUNK), 0)
    c = lax.broadcasted_iota(jnp.int32, (SGU_CHUNK, SGU_CHUNK), 1)
    return r >= c


def _sgu_fwd(h, ln_g, ln_b, w_s, b_full, ublk, name):
    s = h.shape[0]
    tr = _tile(s, 512)
    nch = tr // SGU_CHUNK

    def body(u_ref, v_ref, g_ref, b_ref, w_ref, bf_ref, y_ref):
        masks = _sgu_group_masks()
        tril = _sgu_tril()
        wm = [jnp.where(tril, w_ref[g], 0.0).astype(BF16) for g in range(N_SGU_GROUPS)]
        vn = _layer_norm(_gelu(v_ref[...]), g_ref[...], b_ref[...])
        for ch in range(nch):
            rows = pl.ds(ch * SGU_CHUNK, SGU_CHUNK)
            vc = vn[ch * SGU_CHUNK:(ch + 1) * SGU_CHUNK, :]
            mixed = bf_ref[...]
            for g in range(N_SGU_GROUPS):
                mixed = mixed + jnp.dot(wm[g], jnp.where(masks[g], vc, 0.0).astype(BF16), preferred_element_type=F32)
            y_ref[rows, :] = (_gelu(u_ref[rows, :]) * mixed).astype(BF16)

    row = lambda blk: pl.BlockSpec((tr, D_SGU), lambda i: (i, blk))
    vec = pl.BlockSpec((1, D_SGU), lambda i: (0, 0))
    return pl.pallas_call(
        body, name=name, grid=(s // tr,),
        in_specs=[row(ublk), row(ublk + 1), vec, vec,
                  pl.BlockSpec((N_SGU_GROUPS, SGU_CHUNK, SGU_CHUNK), lambda i: (0, 0, 0)),
                  pl.BlockSpec((SGU_CHUNK, D_SGU), lambda i: (0, 0))],
        out_specs=row(0), out_shape=jax.ShapeDtypeStruct((s, D_SGU), BF16), compiler_params=_params("parallel"),
    )(h, h, ln_g, ln_b, w_s, b_full)


def _sgu_bwd(h, ln_g, ln_b, w_s, b_full, dy, ublk, name):
    s = h.shape[0]
    tr = _tile(s, 512)
    nch = tr // SGU_CHUNK
    nt_dims = (((1,), (1,)), ((), ()))

    def norm(v, g, b):
        return _layer_norm(_gelu(v), g, b)

    def body(u_ref, v_ref, g_ref, b_ref, w_ref, bf_ref, dy_ref, du_ref, dv_ref, dg_ref, db_ref, dw_ref, dbf_ref):
        masks = _sgu_group_masks()
        tril = _sgu_tril()
        wf = [jnp.where(tril, w_ref[g], 0.0) for g in range(N_SGU_GROUPS)]
        wm = [w.astype(BF16) for w in wf]
        wmt = [jnp.transpose(w).astype(BF16) for w in wf]
        vn, vjp = jax.vjp(norm, v_ref[...], g_ref[...], b_ref[...])

        @pl.when(pl.program_id(0) == 0)
        def _():
            dg_ref[...] = jnp.zeros_like(dg_ref)
            db_ref[...] = jnp.zeros_like(db_ref)
            dw_ref[...] = jnp.zeros_like(dw_ref)
            dbf_ref[...] = jnp.zeros_like(dbf_ref)

        dvn_parts = []
        for ch in range(nch):
            rows = pl.ds(ch * SGU_CHUNK, SGU_CHUNK)
            vc = vn[ch * SGU_CHUNK:(ch + 1) * SGU_CHUNK, :]
            vc16 = vc.astype(BF16)
            mixed = bf_ref[...]
            for g in range(N_SGU_GROUPS):
                mixed = mixed + jnp.dot(wm[g], jnp.where(masks[g], vc, 0.0).astype(BF16), preferred_element_type=F32)
            dy = dy_ref[rows, :]
            ug, gvjp = jax.vjp(_gelu, u_ref[rows, :])
            du_ref[rows, :] = gvjp(dy * mixed)[0].astype(BF16)
            dmixed = dy * ug
            dbf_ref[...] += dmixed
            dvc = jnp.zeros((SGU_CHUNK, D_SGU), F32)
            for g in range(N_SGU_GROUPS):
                dm16 = jnp.where(masks[g], dmixed, 0.0).astype(BF16)
                dw_ref[g] += jnp.where(tril, lax.dot_general(dm16, vc16, nt_dims, preferred_element_type=F32), 0.0)
                dvc = dvc + jnp.dot(wmt[g], dm16, preferred_element_type=F32)
            dvn_parts.append(dvc)
        dv, dg, db = vjp(jnp.concatenate(dvn_parts, axis=0))
        dv_ref[...] = dv.astype(BF16)
        dg_ref[...] += dg
        db_ref[...] += db

    row = lambda blk: pl.BlockSpec((tr, D_SGU), lambda i: (i, blk))
    vec = pl.BlockSpec((1, D_SGU), lambda i: (0, 0))
    wsp = pl.BlockSpec((N_SGU_GROUPS, SGU_CHUNK, SGU_CHUNK), lambda i: (0, 0, 0))
    bsp = pl.BlockSpec((SGU_CHUNK, D_SGU), lambda i: (0, 0))
    return pl.pallas_call(
        body, name=name, grid=(s // tr,),
        in_specs=[row(ublk), row(ublk + 1), vec, vec, wsp, bsp, row(0)],
        out_specs=[row(0), row(0), vec, vec, wsp, bsp],
        out_shape=[jax.ShapeDtypeStruct((s, D_SGU), BF16), jax.ShapeDtypeStruct((s, D_SGU), BF16),
                   jax.ShapeDtypeStruct((1, D_SGU), F32), jax.ShapeDtypeStruct((1, D_SGU), F32),
                   jax.ShapeDtypeStruct((N_SGU_GROUPS, SGU_CHUNK, SGU_CHUNK), F32),
                   jax.ShapeDtypeStruct((SGU_CHUNK, D_SGU), F32)],
        compiler_params=_params("arbitrary"),
    )(h, h, ln_g, ln_b, w_s, b_full, dy)


def _merge_fwd(h, b_gate, att, yc, ys, wa, wc, ws, name):
    s, d = att.shape[0], wa.shape[1]
    tm, tn = _tile(s, 512), _tile(d, 512)
    nj = d // tn

    def body(g0_ref, g1_ref, g2_ref, bg_ref, a_ref, c_ref, s_ref, wa_ref, wc_ref, ws_ref, o_ref):
        acc = jax.nn.sigmoid(g0_ref[...] + bg_ref[0:1, :]) * jnp.dot(a_ref[...], wa_ref[...], preferred_element_type=F32)
        acc += jax.nn.sigmoid(g1_ref[...] + bg_ref[1:2, :]) * jnp.dot(c_ref[...], wc_ref[...], preferred_element_type=F32)
        acc += jax.nn.sigmoid(g2_ref[...] + bg_ref[2:3, :]) * jnp.dot(s_ref[...], ws_ref[...], preferred_element_type=F32)
        o_ref[...] = acc.astype(BF16)

    gate = lambda b: pl.BlockSpec((tm, tn), lambda j, i: (i, b * nj + j))
    act = lambda k: pl.BlockSpec((tm, k), lambda j, i: (i, 0))
    wgt = lambda k: pl.BlockSpec((k, tn), lambda j, i: (0, j))
    return pl.pallas_call(
        body, name=name, grid=(nj, s // tm),
        in_specs=[gate(0), gate(1), gate(2), pl.BlockSpec((3, tn), lambda j, i: (0, j)),
                  act(D_ATT), act(D_CONV), act(D_SGU), wgt(D_ATT), wgt(D_CONV), wgt(D_SGU)],
        out_specs=pl.BlockSpec((tm, tn), lambda j, i: (i, j)),
        out_shape=jax.ShapeDtypeStruct((s, d), BF16), compiler_params=_params("parallel", "arbitrary"),
    )(h, h, h, b_gate, att, yc, ys, wa, wc, ws)


def _merge_bwd(h, b_gate, att, yc, ys, wa, wc, ws, dm, name):
    s, d = att.shape[0], wa.shape[1]
    tm, tn = _tile(s, 512), _tile(d, 512)
    nj = d // tn

    def body(g0_ref, g1_ref, g2_ref, bg_ref, a_ref, c_ref, s_ref, wa_ref, wc_ref, ws_ref, dm_ref,
             dya_ref, dyc_ref, dys_ref, dgl_ref, dbg_ref):
        dm = dm_ref[...]
        sums = []
        for b, (g_ref, x_ref, w_ref, dy_ref) in enumerate(((g0_ref, a_ref, wa_ref, dya_ref), (g1_ref, c_ref, wc_ref, dyc_ref),
                                                          (g2_ref, s_ref, ws_ref, dys_ref))):
            gate = jax.nn.sigmoid(g_ref[...] + bg_ref[b:b + 1, :])
            y = jnp.dot(x_ref[...], w_ref[...], preferred_element_type=F32)
            dy_ref[...] = (dm * gate).astype(BF16)
            dgl = dm * y * gate * (1.0 - gate)
            dgl_ref[b] = dgl.astype(BF16)
            sums.append(jnp.sum(dgl, axis=0, keepdims=True))
        sub = lax.broadcasted_iota(jnp.int32, (3, tn), 0)
        part = jnp.where(sub == 0, sums[0], jnp.where(sub == 1, sums[1], sums[2]))

        @pl.when(pl.program_id(1) == 0)
        def _():
            dbg_ref[...] = jnp.zeros_like(dbg_ref)

        dbg_ref[...] += part

    gate = lambda b: pl.BlockSpec((tm, tn), lambda j, i: (i, b * nj + j))
    act = lambda k: pl.BlockSpec((tm, k), lambda j, i: (i, 0))
    wgt = lambda k: pl.BlockSpec((k, tn), lambda j, i: (0, j))
    tile = pl.BlockSpec((tm, tn), lambda j, i: (i, j))
    return pl.pallas_call(
        body, name=name, grid=(nj, s // tm),
        in_specs=[gate(0), gate(1), gate(2), pl.BlockSpec((3, tn), lambda j, i: (0, j)),
                  act(D_ATT), act(D_CONV), act(D_SGU), wgt(D_ATT), wgt(D_CONV), wgt(D_SGU), tile],
        out_specs=[tile, tile, tile, pl.BlockSpec((3, tm, tn), lambda j, i: (0, i, j)),
                   pl.BlockSpec((3, tn), lambda j, i: (0, j))],
        out_shape=[jax.ShapeDtypeStruct((s, d), BF16)] * 3 + [jax.ShapeDtypeStruct((3, s, d), BF16),
                                                              jax.ShapeDtypeStruct((3, d), F32)],
        compiler_params=_params("parallel", "arbitrary"),
    )(h, h, h, b_gate, att, yc, ys, wa, wc, ws, dm)


def _ffn_act_fwd(hh, cw, name):
    s, dff = hh.shape[0], hh.shape[1] // 2
    nblk = dff // LANES

    def body(a_ref, b_ref, wa_ref, wb_ref, z_ref):
        rows = lax.broadcasted_iota(jnp.int32, (s, LANES), 0)
        z_ref[...] = (_gelu(_conv3(a_ref[...], wa_ref, rows)) * _conv3(b_ref[...], wb_ref, rows)).astype(BF16)

    col = lambda base: pl.BlockSpec((s, LANES), lambda j: (0, base + j))
    wsp = lambda base: pl.BlockSpec((3, LANES), lambda j: (0, base + j))
    return pl.pallas_call(
        body, name=name, grid=(nblk,), in_specs=[col(0), col(nblk), wsp(0), wsp(nblk)], out_specs=col(0),
        out_shape=jax.ShapeDtypeStruct((s, dff), BF16), compiler_params=_params("parallel"),
    )(hh, hh, cw, cw)


def _ffn_act_bwd(hh, cw, dz, name):
    s, dff = hh.shape[0], hh.shape[1] // 2
    nblk = dff // LANES

    def body(a_ref, b_ref, wa_ref, wb_ref, dz_ref, da_ref, db_ref, dwa_ref, dwb_ref):
        rows = lax.broadcasted_iota(jnp.int32, (s, LANES), 0)
        a, b, wa, wb, dz = a_ref[...], b_ref[...], wa_ref, wb_ref, dz_ref[...]
        ga, gvjp = jax.vjp(_gelu, _conv3(a, wa, rows))
        dca = gvjp(dz * _conv3(b, wb, rows))[0]
        dcb = dz * ga
        da_ref[...] = _conv3_t(dca, wa, rows).astype(BF16)
        db_ref[...] = _conv3_t(dcb, wb, rows).astype(BF16)
        dwa_ref[...] = _conv3_dw(dca, a, rows)
        dwb_ref[...] = _conv3_dw(dcb, b, rows)

    col = lambda base: pl.BlockSpec((s, LANES), lambda j: (0, base + j))
    wsp = lambda base: pl.BlockSpec((3, LANES), lambda j: (0, base + j))
    w8 = lambda base: pl.BlockSpec((8, LANES), lambda j: (0, base + j))
    return pl.pallas_call(
        body, name=name, grid=(nblk,), in_specs=[col(0), col(nblk), wsp(0), wsp(nblk), col(0)],
        out_specs=[col(0), col(0), w8(0), w8(0)],
        out_shape=[jax.ShapeDtypeStruct((s, dff), BF16)] * 2 + [jax.ShapeDtypeStruct((8, dff), F32)] * 2,
        compiler_params=_params("parallel"),
    )(hh, hh, cw, cw, dz)


ANY = pl.BlockSpec(memory_space=pl.ANY)


def _place():
    return lax.axis_index("x"), lax.axis_index("y"), lax.axis_index("c")


def _all_gather(arrs, name):
    n = len(arrs)

    def body(*refs):
        ins, outs = refs[:n], refs[n:2 * n]
        send_sems, recv_sems, local_sems = refs[2 * n:]
        x, y, c = _place()
        me, sibling = (x, y, c), (x, y, 1 - c)
        chips = [(1 - x, y), (x, 1 - y), (1 - x, 1 - y)]

        def slab(a, dev):
            return outs[a].at[4 * dev[0] + 2 * dev[1] + dev[2]]

        def copy(a, k, block, to, src=None):
            return pltpu.make_async_remote_copy(
                src_ref=slab(a, block) if src is None else src, dst_ref=slab(a, block),
                send_sem=send_sems.at[7 * a + k], recv_sem=recv_sems.at[7 * a + k], device_id=to, device_id_type=MESH)

        mine = [pltpu.make_async_copy(ins[a], slab(a, me), local_sems.at[a]) for a in range(n)]
        for cp in mine:
            cp.start()
        first = []
        for a in range(n):
            first.append(copy(a, 0, me, sibling, src=ins[a]))
            first += [copy(a, 1 + j, me, (*chip, c), src=ins[a]) for j, chip in enumerate(chips)]
        for cp in first:
            cp.start()
        passed = []
        for a in range(n):
            for j, chip in enumerate(chips):
                copy(a, 1 + j, (*chip, c), me).wait_recv()
                fwd = copy(a, 4 + j, (*chip, c), sibling)
                fwd.start()
                passed.append(fwd)
        for a in range(n):
            copy(a, 0, sibling, me).wait_recv()
            for j, chip in enumerate(chips):
                copy(a, 4 + j, (*chip, 1 - c), me).wait_recv()
        for cp in first + passed:
            cp.wait_send()
        for cp in mine:
            cp.wait()

    return pl.pallas_call(
        body, name=name, in_specs=[ANY] * n, out_specs=[ANY] * n,
        out_shape=[jax.ShapeDtypeStruct((N_DEV,) + a.shape, a.dtype) for a in arrs],
        scratch_shapes=[pltpu.SemaphoreType.DMA((7 * n,)), pltpu.SemaphoreType.DMA((7 * n,)), pltpu.SemaphoreType.DMA((n,))],
    )(*arrs)


HBM = pl.BlockSpec(memory_space=pltpu.HBM)
SEM = pl.BlockSpec(memory_space=pltpu.SEMAPHORE)
EFFECT = pltpu.SideEffectType.DATAFLOW_SIDE_EFFECTING


def _slot(dev):
    return 4 * dev[0] + 2 * dev[1] + dev[2]


def _exchange_copies(src_refs, land_refs, send_sems, recv_sems, src_view, land_view):
    x, y, c = _place()
    me = (x, y, c)
    peers = [(1 - x if r & 4 else x, 1 - y if r & 2 else y, 1 - c if r & 1 else c) for r in range(1, N_DEV)]
    sends, lands = [], []
    for a, (src, land) in enumerate(zip(src_refs, land_refs)):
        for k, peer in enumerate(peers):
            sems = dict(send_sem=send_sems.at[7 * a + k], recv_sem=recv_sems.at[7 * a + k], device_id=peer,
                        device_id_type=MESH)
            sends.append(pltpu.make_async_remote_copy(src_ref=src_view(src, _slot(peer)),
                                                      dst_ref=land_view(land, _slot(me)), **sems))
            lands.append(pltpu.make_async_remote_copy(src_ref=src_view(src, _slot(me)),
                                                      dst_ref=land_view(land, _slot(peer)), **sems))
    return sends, lands


def _exchange_start(srcs, lands, after, src_view, land_view, name):
    n = len(srcs)

    def body(*refs):
        src_refs, land_refs = refs[:n], refs[n:2 * n]
        send_sems, recv_sems = refs[2 * n + 1], refs[2 * n + 2]
        token = refs[-1]
        sends, _ = _exchange_copies(src_refs, land_refs, send_sems, recv_sems, src_view, land_view)
        for cp in sends:
            cp.start()
        token[...] = jnp.zeros_like(token)

    thru = [pltpu.HBM(a.shape, a.dtype) for a in list(srcs) + list(lands)]
    outs = pl.pallas_call(
        body, name=name,
        out_shape=(pltpu.SemaphoreType.DMA((7 * n,)), pltpu.SemaphoreType.DMA((7 * n,)), *thru,
                   jax.ShapeDtypeStruct((8, LANES), F32)),
        in_specs=[HBM] * (2 * n) + [ANY],
        out_specs=(SEM, SEM, *([HBM] * (2 * n)), pl.BlockSpec(memory_space=pltpu.VMEM)),
        input_output_aliases={i: 2 + i for i in range(2 * n)},
        compiler_params=pltpu.CompilerParams(has_side_effects=EFFECT),
    )(*[pltpu.with_memory_space_constraint(a, pltpu.HBM) for a in list(srcs) + list(lands)], after)
    return outs[0], outs[1], list(outs[2:2 + n]), list(outs[2 + n:2 + 2 * n]), outs[-1]


def _exchange_wait(send_sems, recv_sems, srcs, lands, after, src_view, land_view, name):
    n = len(srcs)

    def body(*refs):
        src_refs, land_refs = refs[:n], refs[n:2 * n]
        sends, landed = _exchange_copies(src_refs, land_refs, refs[2 * n], refs[2 * n + 1], src_view, land_view)
        for cp in sends:
            cp.wait_send()
        for cp in landed:
            cp.wait_recv()

    outs = pl.pallas_call(
        body, name=name, out_shape=[pltpu.HBM(a.shape, a.dtype) for a in list(srcs) + list(lands)],
        in_specs=[HBM] * (2 * n) + [SEM, SEM, ANY], out_specs=[HBM] * (2 * n),
        input_output_aliases={i: i for i in range(2 * n)},
        compiler_params=pltpu.CompilerParams(has_side_effects=EFFECT),
    )(*srcs, *lands, send_sems, recv_sems, after)
    return list(outs[:n]), list(outs[n:])


ADAMW_BLOCK_BYTES = 1 << 19
PACK_ROWS = 256


def _adamw(slabs, w, m, v, name):
    nl, r, c = w.shape
    row_edges = [r] + [t for t in range(8, r, 8) if r % t == 0]
    col_edges = [c] + [t for t in range(LANES, c, LANES) if c % t == 0]
    fits = [(tr * tc, tc, tr) for tr in row_edges for tc in col_edges if tr * tc * 4 <= ADAMW_BLOCK_BYTES]
    _, tc, tr = max(fits) if fits else (0, min(col_edges), min(row_edges))

    def body(s_ref, w_ref, m_ref, v_ref, g_ref, d_ref, nm_ref, nv_ref):
        g = s_ref[0].astype(F32)
        for q in range(1, N_DEV):
            g = g + s_ref[q].astype(F32)
        m_new = ADAM_B1 * m_ref[...] + (1.0 - ADAM_B1) * g
        v_new = ADAM_B2 * v_ref[...] + (1.0 - ADAM_B2) * (g * g)
        m_hat = m_new / (1.0 - ADAM_B1 ** ADAM_STEP)
        v_hat = v_new / (1.0 - ADAM_B2 ** ADAM_STEP)
        g_ref[...] = g
        d_ref[...] = -ADAM_LR * (m_hat / (jnp.sqrt(v_hat) + ADAM_EPS) + ADAM_WD * w_ref[...])
        nm_ref[...] = m_new
        nv_ref[...] = v_new

    blk = pl.BlockSpec((None, tr, tc), lambda l, i, j: (l, i, j))
    return pl.pallas_call(
        body, name=name, grid=(nl, r // tr, c // tc),
        in_specs=[pl.BlockSpec((N_DEV, None, tr, tc), lambda l, i, j: (0, l, i, j)), blk, blk, blk],
        out_specs=[blk] * 4, out_shape=[jax.ShapeDtypeStruct(w.shape, F32)] * 4,
        compiler_params=_params("parallel", "parallel", "parallel"),
    )(slabs, w, m, v)


def _insert_own(lands, owns, own_view, land_view, name):
    n = len(lands)

    def body(*refs):
        own_refs, land_refs, sems = refs[n:2 * n], refs[2 * n:3 * n], refs[-1]
        me = _slot(_place())
        copies = [pltpu.make_async_copy(own_view(own, me), land_view(land, me), sems.at[a])
                  for a, (land, own) in enumerate(zip(land_refs, own_refs))]
        for cp in copies:
            cp.start()
        for cp in copies:
            cp.wait()

    return pl.pallas_call(
        body, name=name, in_specs=[ANY] * (2 * n), out_specs=[ANY] * n,
        out_shape=[jax.ShapeDtypeStruct(a.shape, a.dtype) for a in lands],
        input_output_aliases={i: i for i in range(n)}, scratch_shapes=[pltpu.SemaphoreType.DMA((n,))],
    )(*lands, *owns)


def _layout(d):
    off = {"gate": 0, "q": 3 * d}
    off["bg"] = off["q"] + 3 * D_ATT
    off["u"] = off["bg"] + 3 * D_CONV
    off["f"] = off["u"] + 2 * D_SGU
    width = -(-(off["f"] + LANES) // 512) * 512
    return off, width


def _pad_w_in(wt, d, token):
    off, width = _layout(d)
    nqkv, nrest = 3 * D_ATT, 3 * D_CONV + 2 * D_SGU
    pad = jnp.zeros((width - off["f"] - N_HEADS, wt.shape[1]), wt.dtype) + token[0, 0].astype(wt.dtype)
    return jnp.concatenate([wt[nqkv + N_HEADS + nrest:], wt[:nqkv], wt[nqkv + N_HEADS:nqkv + N_HEADS + nrest],
                            wt[nqkv:nqkv + N_HEADS], pad], axis=0)


def _unpad_w_in(wtp, d):
    off, _ = _layout(d)
    return jnp.concatenate([wtp[off["q"]:off["bg"]], wtp[off["f"]:off["f"] + N_HEADS], wtp[off["bg"]:off["f"]],
                            wtp[:off["q"]]], axis=0)


def _cols_from_slabs(g):
    return jnp.transpose(g, (1, 0, 2)).reshape(g.shape[1], N_DEV * g.shape[2])


def _cols_to_slabs(w):
    r, c = w.shape[0], w.shape[1] // N_DEV
    return jnp.transpose(w.reshape(r, N_DEV, c), (1, 0, 2))


def kernel(x, pre_mix_g, post_mix_g, pre_ffn_g, post_ffn_g, w_in, b_forget, b_gate, conv_mix_w, sgu_ln_g, sgu_ln_b, sgu_w, sgu_b, w_branch_att, w_branch_conv, w_branch_sgu, w_out, w_ffn_up, conv_ffn_w, w_ffn_down, loss_target, m_pre_mix_g, m_post_mix_g, m_pre_ffn_g, m_post_ffn_g, m_w_in, m_b_forget, m_b_gate, m_conv_mix_w, m_sgu_ln_g, m_sgu_ln_b, m_sgu_w, m_sgu_b, m_w_branch_att, m_w_branch_conv, m_w_branch_sgu, m_w_out, m_w_ffn_up, m_conv_ffn_w, m_w_ffn_down, v_pre_mix_g, v_post_mix_g, v_pre_ffn_g, v_post_ffn_g, v_w_in, v_b_forget, v_b_gate, v_conv_mix_w, v_sgu_ln_g, v_sgu_ln_b, v_sgu_w, v_sgu_b, v_w_branch_att, v_w_branch_conv, v_w_branch_sgu, v_w_out, v_w_ffn_up, v_conv_ffn_w, v_w_ffn_down):
    depth = w_in.shape[0]
    s, d = x.shape[1], x.shape[2]
    dff = w_ffn_down.shape[1] * N_DEV
    off, _ = _layout(d)
    qblk, bgblk, ublk, fblk = off["q"] // LANES, off["bg"] // LANES, off["u"] // D_SGU, off["f"] // LANES
    x0 = x.reshape(s, d)
    target = loss_target.reshape(s, d)
    ncm, ncf = conv_mix_w.shape[2], conv_ffn_w.shape[2]

    me = 4 * lax.axis_index("x") + 2 * lax.axis_index("y") + lax.axis_index("c")
    lo = d // N_DEV

    whole = lambda ref, slot: ref
    slab = lambda ref, slot: ref.at[slot]

    w_in_t, w_up_t = jnp.transpose(w_in, (0, 2, 1)), jnp.transpose(w_ffn_up, (0, 2, 1))

    def gather_start(l, part, after):
        if part == "mix":
            small = jnp.concatenate([b_gate[l], conv_mix_w[l], conv_ffn_w[l]], axis=1)
            shards = [w_in_t[l].astype(BF16), w_branch_att[l].astype(BF16), w_branch_conv[l].astype(BF16),
                      w_branch_sgu[l].astype(BF16), w_out[l].astype(BF16), small]
        else:
            shards = [w_up_t[l].astype(BF16), w_ffn_down[l].astype(BF16)]
        lands = [lax.empty((N_DEV,) + a.shape, a.dtype) for a in shards]
        return _exchange_start(shards, lands, after, whole, slab, name=f"gather_start_{part}_{l}")

    def gather_finish(l, part, started, after):
        send_sems, recv_sems, shards, lands, _ = started[part]
        shards, lands = _exchange_wait(send_sems, recv_sems, shards, lands, after, whole, slab,
                                       name=f"gather_wait_{part}_{l}")
        token = jnp.zeros((8, LANES), F32)
        if l + 1 < depth:
            started[part] = gather_start(l + 1, part, lands[0])
            token = started[part][4]
        return _insert_own(lands, shards, whole, slab, name=f"gather_own_{part}"), token

    def bfull(l):
        return jnp.repeat(jnp.transpose(sgu_b[l]), HEAD_DIM, axis=1)

    def bf_pad(l):
        return jnp.pad(b_forget[l], (0, LANES - N_HEADS)).reshape(1, LANES)

    saved = []
    weights = []
    xin = x0
    xn = _prenorm(x0, pre_mix_g[0:1], name="prenorm_first")
    loss_acc = dy = None
    started = {"mix": gather_start(0, "mix", x0)}
    started["ffn"] = gather_start(0, "ffn", started["mix"][4])
    for l in range(depth):
        (g_in, g_a, g_c, g_s, g_o, g_small), token = gather_finish(l, "mix", started, xin)
        g_small = _cols_from_slabs(g_small).reshape(3, N_DEV, -1)
        w = dict(w_in=_pad_w_in(g_in.reshape(N_DEV * g_in.shape[1], d), d, token), wa=_cols_from_slabs(g_a),
                 wc=_cols_from_slabs(g_c), ws=_cols_from_slabs(g_s), w_out=g_o.reshape(d, d),
                 b_gate=g_small[:, :, :lo].reshape(3, d), cmw=g_small[:, :, lo:lo + ncm].reshape(3, D_CONV),
                 cfw=g_small[:, :, lo + ncm:].reshape(3, 2 * dff))
        weights.append(w)
        h = _mm(xn, w["w_in"], "nt", F32, name="proj_in")
        c, ct = _forget_prep(h, bf_pad(l), fblk, name="forget_prep")
        att, lse = _attention_fwd(h, c, ct, qblk, name="attention_fwd")
        yc = _sconv_fwd(h, w["cmw"], bgblk, name="sconv_fwd")
        ys = _sgu_fwd(h, sgu_ln_g[l:l + 1], sgu_ln_b[l:l + 1], sgu_w[l], bfull(l), ublk, name="sgu_fwd")
        merged = _merge_fwd(h, w["b_gate"], att, yc, ys, w["wa"], w["wc"], w["ws"], name="merge_fwd")
        o = _mm(merged, w["w_out"], "nn", F32, name="proj_out")
        (g_up, g_dn), token = gather_finish(l, "ffn", started, o)
        w["w_up"], w["w_dn"] = g_up.reshape(2 * dff, d), g_dn.reshape(dff, d)
        x1, xn2 = _postnorm(xin, o, post_mix_g[l:l + 1], pre_ffn_g[l:l + 1] + token[0, 0], name="postnorm_mix")
        hh = _mm(xn2, w["w_up"], "nt", F32, name="ffn_up")
        z = _ffn_act_fwd(hh, w["cfw"], name="ffn_act_fwd")
        f = _mm(z, w["w_dn"], "nn", F32, name="ffn_down")
        saved.append(dict(xin=xin, xn=xn, h=h, c=c, ct=ct, lse=lse, att=att, yc=yc, ys=ys, merged=merged, o=o, x1=x1,
                          xn2=xn2, hh=hh, z=z, f=f))
        if l + 1 < depth:
            xin, xn = _postnorm(x1, f, post_ffn_g[l:l + 1], pre_mix_g[l + 1:l + 2], name="postnorm_ffn")
        else:
            dy, loss_acc = _postnorm_loss(x1, f, post_ffn_g[l:l + 1], target, name="postnorm_loss")
    loss = lax.psum(loss_acc[0, 0] * (0.5 / d), ("x", "y", "c"))

    rep = {k: [None] * depth for k in ("pre_mix_g", "post_mix_g", "pre_ffn_g", "post_ffn_g", "b_forget", "sgu_ln_g",
                                       "sgu_ln_b", "sgu_w", "sgu_b")}
    nsmall = lo + ncm + ncf
    lands = {"mix": [lax.empty((N_DEV, depth) + shp, dt) for shp, dt in (
                 (w_in_t.shape[1:], BF16), (w_branch_att.shape[1:], BF16), (w_branch_conv.shape[1:], BF16),
                 (w_branch_sgu.shape[1:], BF16), (w_out.shape[1:], BF16), ((3, nsmall), F32))],
             "ffn": [lax.empty((N_DEV, depth) + shp, BF16) for shp in (w_up_t.shape[1:], w_ffn_down.shape[1:])]}
    scatters = {"mix": [None] * depth, "ffn": [None] * depth}

    def scatter_start(l, part, sends, after):
        layer_slab = lambda ref, slot: ref.at[slot, l]
        send_sems, recv_sems, sends, lands[part], token = _exchange_start(
            sends, lands[part], after, slab, layer_slab, name=f"scatter_start_{part}_{l}")
        scatters[part][l] = (send_sems, recv_sems, sends, layer_slab)
        return token

    def scatter_finish(part, after):
        for l in range(depth):
            send_sems, recv_sems, sends, layer_slab = scatters[part][l]
            sends, lands[part] = _exchange_wait(send_sems, recv_sems, sends, lands[part], after, slab, layer_slab,
                                                name=f"scatter_wait_{part}_{l}")
            lands[part] = _insert_own(lands[part], sends, slab, layer_slab, name=f"scatter_own_{part}")
        return lands[part]

    token = jnp.zeros((8, LANES), F32)
    dx = dy
    for l in reversed(range(depth)):
        w, a = weights[l], saved[l]
        df, rep["post_ffn_g"][l] = _postnorm_bwd(a["f"], post_ffn_g[l:l + 1] + token[0, 0], dx, name="postnorm_bwd")
        dz = _mm(df, w["w_dn"], "nt", F32, name="ffn_down_dx")
        g_dn = _mm(a["z"], df, "tn", BF16, name="ffn_down_dw")
        dha, dhb, dcwa, dcwb = _ffn_act_bwd(a["hh"], w["cfw"], dz, name="ffn_act_bwd")
        dhh = jnp.concatenate([dha, dhb], axis=1)
        dcfw = jnp.concatenate([dcwa[0:3], dcwb[0:3]], axis=1)
        dxn2 = _mm(dhh, w["w_up"], "nn", F32, name="ffn_up_dx")
        g_up = _mm(dhh, a["xn2"], "tn", BF16, name="ffn_up_dw")
        token = scatter_start(l, "ffn", [g_up.reshape(N_DEV, 2 * dff // N_DEV, d), g_dn.reshape(N_DEV, dff // N_DEV, d)],
                              dxn2)
        dx1, rep["pre_ffn_g"][l] = _prenorm_bwd(a["x1"], pre_ffn_g[l:l + 1] + token[0, 0], dxn2, dx, name="prenorm_bwd")
        do, rep["post_mix_g"][l] = _postnorm_bwd(a["o"], post_mix_g[l:l + 1], dx1, name="postnorm_bwd")
        dmerged = _mm(do, w["w_out"], "nt", F32, name="proj_out_dx")
        g_o = _mm(a["merged"], do, "tn", BF16, name="proj_out_dw")
        dya, dyc, dys, dgl, dbg = _merge_bwd(a["h"], w["b_gate"], a["att"], a["yc"], a["ys"], w["wa"], w["wc"], w["ws"],
                                             dmerged, name="merge_bwd")
        datt = _mm(dya, w["wa"], "nt", F32, name="branch_att_dx")
        dconv = _mm(dyc, w["wc"], "nt", F32, name="branch_conv_dx")
        dsgu = _mm(dys, w["ws"], "nt", F32, name="branch_sgu_dx")
        g_a = _mm(a["att"], dya, "tn", BF16, name="branch_att_dw")
        g_c = _mm(a["yc"], dyc, "tn", BF16, name="branch_conv_dw")
        g_s = _mm(a["ys"], dys, "tn", BF16, name="branch_sgu_dw")
        dq, dk, dv, dct4, dcq4 = _attention_bwd(a["h"], a["c"], a["ct"], a["lse"], a["att"], datt, qblk,
                                                name="attention_bwd")
        dcq = jnp.pad(jnp.transpose(dcq4[:, :, 0:2], (1, 0, 2)).reshape(s, N_HEADS), ((0, 0), (0, LANES - N_HEADS)))
        dfl, dbf = _forget_prep_bwd(a["h"], bf_pad(l), dct4[:, 0:2, :].reshape(N_HEADS, s), dcq, fblk,
                                    name="forget_prep_bwd")
        rep["b_forget"][l] = dbf[0, :N_HEADS]
        dbgate, dcg, dhc, dcmw = _sconv_bwd(a["h"], w["cmw"], dconv, bgblk, name="sconv_bwd")
        du, dvs, dlg, dlb, dsw, dbfull = _sgu_bwd(a["h"], sgu_ln_g[l:l + 1], sgu_ln_b[l:l + 1], sgu_w[l], bfull(l), dsgu,
                                                  ublk, name="sgu_bwd")
        rep["sgu_ln_g"][l], rep["sgu_ln_b"][l], rep["sgu_w"][l] = dlg, dlb, dsw
        rep["sgu_b"][l] = jnp.transpose(jnp.sum(dbfull.reshape(SGU_CHUNK, N_SGU_GROUPS, HEAD_DIM), axis=2))
        dh = jnp.concatenate([dgl[0], dgl[1], dgl[2], dq.astype(BF16), dk.astype(BF16), dv.astype(BF16), dbgate, dcg, dhc,
                              du, dvs, dfl.astype(BF16), jnp.zeros((s, w["w_in"].shape[0] - off["f"] - LANES), BF16)], axis=1)
        dxn = _mm(dh, w["w_in"], "nn", F32, name="proj_in_dx")
        g_in = _mm(dh, a["xn"], "tn", BF16, name="proj_in_dw")
        dx, rep["pre_mix_g"][l] = _prenorm_bwd(a["xin"], pre_mix_g[l:l + 1], dxn, dx1, name="prenorm_bwd")
        sends = [_unpad_w_in(g_in, d).reshape(N_DEV, -1, d), _cols_to_slabs(g_a), _cols_to_slabs(g_c), _cols_to_slabs(g_s),
                 g_o.reshape(N_DEV, d // N_DEV, d),
                 jnp.concatenate([_cols_to_slabs(dbg), _cols_to_slabs(dcmw[0:3]), _cols_to_slabs(dcfw)], axis=2)]
        token = scatter_start(l, "mix", sends, dx)

    outs = {}
    t3 = lambda arr: jnp.transpose(arr, (0, 2, 1))

    def update(name_, slabs, w_, m_, v_, transposed=False):
        if transposed:
            w_, m_, v_ = t3(w_), t3(m_), t3(v_)
        shp = w_.shape
        w3 = w_.reshape((shp[0], -1, shp[-1])) if w_.ndim >= 3 else w_.reshape((1,) + shp)
        res = _adamw(slabs.reshape((N_DEV,) + w3.shape), w3, m_.reshape(w3.shape), v_.reshape(w3.shape),
                     name="adamw_" + name_)
        outs[name_] = tuple(t3(t.reshape(shp)) if transposed else t.reshape(shp) for t in res)
        return res[0]

    got_up, got_dn = scatter_finish("ffn", dx)
    update("w_ffn_up", got_up, w_ffn_up, m_w_ffn_up, v_w_ffn_up, transposed=True)
    update("w_ffn_down", got_dn, w_ffn_down, m_w_ffn_down, v_w_ffn_down)

    rep_names = ("pre_mix_g", "post_mix_g", "pre_ffn_g", "post_ffn_g", "b_forget", "sgu_ln_g", "sgu_ln_b", "sgu_w", "sgu_b")
    rep_w = dict(pre_mix_g=(pre_mix_g, m_pre_mix_g, v_pre_mix_g), post_mix_g=(post_mix_g, m_post_mix_g, v_post_mix_g),
                 pre_ffn_g=(pre_ffn_g, m_pre_ffn_g, v_pre_ffn_g), post_ffn_g=(post_ffn_g, m_post_ffn_g, v_post_ffn_g),
                 b_forget=(b_forget, m_b_forget, v_b_forget), sgu_ln_g=(sgu_ln_g, m_sgu_ln_g, v_sgu_ln_g),
                 sgu_ln_b=(sgu_ln_b, m_sgu_ln_b, v_sgu_ln_b), sgu_w=(sgu_w, m_sgu_w, v_sgu_w), sgu_b=(sgu_b, m_sgu_b, v_sgu_b))

    def pack(parts):
        flat = jnp.concatenate([p.reshape(-1) for p in parts])
        return jnp.pad(flat, (0, -flat.shape[0] % (PACK_ROWS * LANES))).reshape(-1, LANES)

    part = pack([jnp.stack([g.reshape(rep_w[k][0].shape[1:]) for g in rep[k]]) for k in rep_names])
    (gathered,) = _all_gather([part], name="gather_small_grads")
    packed = [pack([rep_w[k][i] for k in rep_names]) for i in range(3)]
    res = _adamw(gathered.reshape(N_DEV, 1, -1, LANES), *[p.reshape(1, -1, LANES) for p in packed], name="adamw_replicated")
    pos = 0
    for k in rep_names:
        shp = rep_w[k][0].shape
        size = math.prod(shp)
        outs[k] = tuple(t.reshape(-1)[pos:pos + size].reshape(shp) for t in res)
        pos += size

    got_in, got_a, got_c, got_s, got_o, small = scatter_finish("mix", res[0])
    update("w_in", got_in, w_in, m_w_in, v_w_in, transposed=True)
    update("w_branch_att", got_a, w_branch_att, m_w_branch_att, v_w_branch_att)
    update("w_branch_conv", got_c, w_branch_conv, m_w_branch_conv, v_w_branch_conv)
    update("w_branch_sgu", got_s, w_branch_sgu, m_w_branch_sgu, v_w_branch_sgu)
    update("w_out", got_o, w_out, m_w_out, v_w_out)
    update("b_gate", small[..., :lo], b_gate, m_b_gate, v_b_gate)
    update("conv_mix_w", small[..., lo:lo + ncm], conv_mix_w, m_conv_mix_w, v_conv_mix_w)
    update("conv_ffn_w", small[..., lo + ncm:], conv_ffn_w, m_conv_ffn_w, v_conv_ffn_w)

    order = ("pre_mix_g", "post_mix_g", "pre_ffn_g", "post_ffn_g", "w_in", "b_forget", "b_gate", "conv_mix_w", "sgu_ln_g",
             "sgu_ln_b", "sgu_w", "sgu_b", "w_branch_att", "w_branch_conv", "w_branch_sgu", "w_out", "w_ffn_up",
             "conv_ffn_w", "w_ffn_down")
    grad_x = dx.reshape(x.shape)
    return (loss, grad_x, *[outs[k][0] for k in order], *[outs[k][1] for k in order], *[outs[k][2] for k in order],
            *[outs[k][3] for k in order])
```

```python
import functools
import math

import jax
import jax.numpy as jnp
from jax import lax
from jax.experimental import pallas as pl
from jax.experimental.pallas import tpu as pltpu

F32 = jnp.float32
BF16 = jnp.bfloat16

N_DEV = 8
HEAD_DIM = 64
N_HEADS = 8
D_ATT = 512
D_CONV = 256
D_SGU = 256
N_SGU_GROUPS = 4
SGU_CHUNK = 128
RMS_EPS = 1e-6
LN_EPS = 1e-5
ADAM_LR = 0.001
ADAM_B1 = 0.9
ADAM_B2 = 0.999
ADAM_EPS = 1e-08
ADAM_WD = 0.01
ADAM_STEP = 10
LANES = 128
VMEM_LIMIT = 56 * 1024 * 1024
NEG = -1e30
MESH = pl.DeviceIdType.MESH


def _params(*sem):
    return pltpu.CompilerParams(dimension_semantics=sem if sem else None, vmem_limit_bytes=VMEM_LIMIT)


def _tile(n, cap):
    if n <= cap:
        return n
    t = cap - cap % LANES
    while n % t:
        t -= LANES
    return t


def _gelu(x):
    return 0.5 * x * (1.0 + jnp.tanh(math.sqrt(2.0 / math.pi) * (x + 0.044715 * (x * x * x))))


def _rms(x, g):
    r = lax.rsqrt(jnp.mean(x * x, axis=-1, keepdims=True) + RMS_EPS)
    return x * r * g


def _layer_norm(x, g, b):
    mu = jnp.mean(x, axis=-1, keepdims=True)
    xc = x - mu
    var = jnp.mean(xc * xc, axis=-1, keepdims=True)
    return xc * lax.rsqrt(var + LN_EPS) * g + b


def _shift_down(x, k, rows):
    return jnp.where(rows >= k, pltpu.roll(x, k, 0), 0.0)


def _shift_up(x, k, rows):
    s = x.shape[0]
    return jnp.where(rows < s - k, pltpu.roll(x, s - k, 0), 0.0)


def _conv3(x, w_ref, rows):
    return w_ref[2:3, :] * x + w_ref[1:2, :] * _shift_down(x, 1, rows) + w_ref[0:1, :] * _shift_down(x, 2, rows)


def _conv3_t(dy, w_ref, rows):
    return w_ref[2:3, :] * dy + w_ref[1:2, :] * _shift_up(dy, 1, rows) + w_ref[0:1, :] * _shift_up(dy, 2, rows)


def _conv3_dw(dy, x, rows):
    d2 = jnp.sum(dy * x, axis=0, keepdims=True)
    d1 = jnp.sum(dy * _shift_down(x, 1, rows), axis=0, keepdims=True)
    d0 = jnp.sum(dy * _shift_down(x, 2, rows), axis=0, keepdims=True)
    sub = lax.broadcasted_iota(jnp.int32, (8, x.shape[1]), 0)
    return jnp.where(sub == 0, d0, jnp.where(sub == 1, d1, jnp.where(sub == 2, d2, 0.0)))


MM_VMEM_BUDGET = 40 * 1024 * 1024
MM_TILE_CAP = 1408


def _mm_tiles(m, n, k, out_bytes):
    def edges(d):
        return [t for t in range(LANES, min(d, MM_TILE_CAP) + 1, LANES) if d % t == 0] or [d]

    best = None
    for tm in edges(m):
        for tn in edges(n):
            if 2 * (2 * k * (tm + tn) + tm * tn * out_bytes) > MM_VMEM_BUDGET:
                continue
            for a_outer in (True, False):
                reads = k * m + (m // tm) * k * n if a_outer else k * n + (n // tn) * k * m
                traffic = 2 * reads + m * n * out_bytes
                key = (traffic, -tm * tn)
                if best is None or key < best[0]:
                    best = (key, (tm, tn, a_outer))
    return best[1]


def _mm(a, b, form, out_dtype, name):
    if form == "nn":
        (m, k), n = a.shape, b.shape[1]
    elif form == "nt":
        (m, k), n = a.shape, b.shape[0]
    else:
        (k, m), n = a.shape, b.shape[1]
    tm, tn, a_outer = _mm_tiles(m, n, k, jnp.dtype(out_dtype).itemsize)
    dims = {"nn": (((1,), (0,)), ((), ())), "nt": (((1,), (1,)), ((), ())), "tn": (((0,), (0,)), ((), ()))}[form]

    def body(a_ref, b_ref, o_ref):
        o_ref[...] = lax.dot_general(a_ref[...], b_ref[...], dims, preferred_element_type=F32).astype(o_ref.dtype)

    ij = (lambda g0, g1: (g0, g1)) if a_outer else (lambda g0, g1: (g1, g0))
    a_spec = (pl.BlockSpec((k, tm), lambda g0, g1: (0, ij(g0, g1)[0])) if form == "tn"
              else pl.BlockSpec((tm, k), lambda g0, g1: (ij(g0, g1)[0], 0)))
    b_spec = (pl.BlockSpec((tn, k), lambda g0, g1: (ij(g0, g1)[1], 0)) if form == "nt"
              else pl.BlockSpec((k, tn), lambda g0, g1: (0, ij(g0, g1)[1])))
    return pl.pallas_call(
        body, name=name, grid=(m // tm, n // tn) if a_outer else (n // tn, m // tm),
        in_specs=[a_spec, b_spec], out_specs=pl.BlockSpec((tm, tn), lambda g0, g1: ij(g0, g1)),
        out_shape=jax.ShapeDtypeStruct((m, n), out_dtype),
        compiler_params=_params("parallel", "arbitrary"),
    )(a, b)


def _prenorm(x, g, name):
    s, d = x.shape
    tm = _tile(s, 512)

    def body(x_ref, g_ref, o_ref):
        o_ref[...] = _rms(x_ref[...], g_ref[...]).astype(BF16)

    return pl.pallas_call(
        body, name=name, grid=(s // tm,),
        in_specs=[pl.BlockSpec((tm, d), lambda i: (i, 0)), pl.BlockSpec((1, d), lambda i: (0, 0))],
        out_specs=pl.BlockSpec((tm, d), lambda i: (i, 0)),
        out_shape=jax.ShapeDtypeStruct((s, d), BF16), compiler_params=_params("parallel"),
    )(x, g)


def _postnorm(x, o, g_post, g_next, name):
    s, d = x.shape
    tm = _tile(s, 512)

    def body(x_ref, o_ref, gp_ref, gn_ref, x1_ref, xn_ref):
        x1 = x_ref[...] + _rms(o_ref[...], gp_ref[...])
        x1_ref[...] = x1
        xn_ref[...] = _rms(x1, gn_ref[...]).astype(BF16)

    row = pl.BlockSpec((tm, d), lambda i: (i, 0))
    vec = pl.BlockSpec((1, d), lambda i: (0, 0))
    return pl.pallas_call(
        body, name=name, grid=(s // tm,), in_specs=[row, row, vec, vec], out_specs=[row, row],
        out_shape=[jax.ShapeDtypeStruct((s, d), F32), jax.ShapeDtypeStruct((s, d), BF16)],
        compiler_params=_params("parallel"),
    )(x, o, g_post, g_next)


def _postnorm_loss(x, o, g_post, target, name):
    s, d = x.shape
    tm = _tile(s, 512)

    def body(x_ref, o_ref, gp_ref, t_ref, dy_ref, acc_ref):
        e = x_ref[...] + _rms(o_ref[...], gp_ref[...]) - t_ref[...]
        dy_ref[...] = e / d

        @pl.when(pl.program_id(0) == 0)
        def _():
            acc_ref[...] = jnp.zeros_like(acc_ref)

        acc_ref[...] += jnp.sum(jnp.sum(e * e, axis=1, keepdims=True), axis=0, keepdims=True)

    row = pl.BlockSpec((tm, d), lambda i: (i, 0))
    return pl.pallas_call(
        body, name=name, grid=(s // tm,),
        in_specs=[row, row, pl.BlockSpec((1, d), lambda i: (0, 0)), row],
        out_specs=[row, pl.BlockSpec((1, LANES), lambda i: (0, 0))],
        out_shape=[jax.ShapeDtypeStruct((s, d), F32), jax.ShapeDtypeStruct((1, LANES), F32)],
        compiler_params=_params("arbitrary"),
    )(x, o, g_post, target)


def _postnorm_bwd(o, g, dx, name):
    s, d = o.shape
    tm = _tile(s, 512)

    def body(o_ref, g_ref, dx_ref, do_ref, dg_ref):
        _, vjp = jax.vjp(_rms, o_ref[...], g_ref[...])
        d_o, dg = vjp(dx_ref[...])
        do_ref[...] = d_o.astype(BF16)

        @pl.when(pl.program_id(0) == 0)
        def _():
            dg_ref[...] = jnp.zeros_like(dg_ref)

        dg_ref[...] += dg

    row = pl.BlockSpec((tm, d), lambda i: (i, 0))
    vec = pl.BlockSpec((1, d), lambda i: (0, 0))
    return pl.pallas_call(
        body, name=name, grid=(s // tm,), in_specs=[row, vec, row], out_specs=[row, vec],
        out_shape=[jax.ShapeDtypeStruct((s, d), BF16), jax.ShapeDtypeStruct((1, d), F32)],
        compiler_params=_params("arbitrary"),
    )(o, g, dx)


def _prenorm_bwd(x, g, dxn, dres, name):
    s, d = x.shape
    tm = _tile(s, 512)

    def body(x_ref, g_ref, dxn_ref, dres_ref, dx_ref, dg_ref):
        _, vjp = jax.vjp(_rms, x_ref[...], g_ref[...])
        dx, dg = vjp(dxn_ref[...])
        dx_ref[...] = dres_ref[...] + dx

        @pl.when(pl.program_id(0) == 0)
        def _():
            dg_ref[...] = jnp.zeros_like(dg_ref)

        dg_ref[...] += dg

    row = pl.BlockSpec((tm, d), lambda i: (i, 0))
    vec = pl.BlockSpec((1, d), lambda i: (0, 0))
    return pl.pallas_call(
        body, name=name, grid=(s // tm,), in_specs=[row, vec, row, row], out_specs=[row, vec],
        out_shape=[jax.ShapeDtypeStruct((s, d), F32), jax.ShapeDtypeStruct((1, d), F32)],
        compiler_params=_params("arbitrary"),
    )(x, g, dxn, dres)


def _log_sigmoid(z):
    return jnp.minimum(z, 0.0) - jnp.log(1.0 + jnp.exp(-jnp.abs(z)))


def _forget_prep(h, bf_pad, fblk, name):
    s = h.shape[0]

    def body(f_ref, b_ref, c_ref, ct_ref):
        c = _log_sigmoid(f_ref[...] + b_ref[...])
        rows = lax.broadcasted_iota(jnp.int32, c.shape, 0)
        k = 1
        while k < s:
            c = c + _shift_down(c, k, rows)
            k *= 2
        c_ref[...] = c
        ct_ref[...] = jnp.transpose(c)[0:8, :]

    return pl.pallas_call(
        body, name=name, grid=(1,),
        in_specs=[pl.BlockSpec((s, LANES), lambda i: (0, fblk)), pl.BlockSpec((1, LANES), lambda i: (0, 0))],
        out_specs=[pl.BlockSpec((s, LANES), lambda i: (0, 0)), pl.BlockSpec((8, s), lambda i: (0, 0))],
        out_shape=[jax.ShapeDtypeStruct((s, LANES), F32), jax.ShapeDtypeStruct((8, s), F32)],
        compiler_params=_params("arbitrary"),
    )(h, bf_pad)


def _forget_prep_bwd(h, bf_pad, dct, dcq, fblk, name):
    s = h.shape[0]
    pairs = N_HEADS // 2

    def body(f_ref, b_ref, dct_ref, dcq_ref, df_ref, db_ref):
        dct = dct_ref[0]
        dcq = dcq_ref[0]
        for p in range(1, pairs):
            dct = dct + pltpu.roll(dct_ref[p], 2 * p, 0)
            dcq = dcq + pltpu.roll(dcq_ref[p], 2 * p, 1)
        dc = dcq + jnp.transpose(jnp.concatenate([dct, jnp.zeros((LANES - 8, s), F32)], axis=0))
        rows = lax.broadcasted_iota(jnp.int32, dc.shape, 0)
        k = 1
        while k < s:
            dc = dc + _shift_up(dc, k, rows)
            k *= 2
        z = f_ref[...] + b_ref[...]
        lane = lax.broadcasted_iota(jnp.int32, dc.shape, 1)
        df = jnp.where(lane < N_HEADS, dc * jax.nn.sigmoid(-z), 0.0)
        df_ref[...] = df
        db_ref[...] = jnp.sum(df, axis=0, keepdims=True)

    return pl.pallas_call(
        body, name=name, grid=(1,),
        in_specs=[pl.BlockSpec((s, LANES), lambda i: (0, fblk)), pl.BlockSpec((1, LANES), lambda i: (0, 0)),
                  pl.BlockSpec((pairs, 8, s), lambda i: (0, 0, 0)), pl.BlockSpec((pairs, s, LANES), lambda i: (0, 0, 0))],
        out_specs=[pl.BlockSpec((s, LANES), lambda i: (0, 0)), pl.BlockSpec((1, LANES), lambda i: (0, 0))],
        out_shape=[jax.ShapeDtypeStruct((s, LANES), F32), jax.ShapeDtypeStruct((1, LANES), F32)],
        compiler_params=_params("arbitrary"),
    )(h, bf_pad, dct, dcq)


def _pick_lane(blk, idx):
    lane = lax.broadcasted_iota(jnp.int32, blk.shape, 1)
    return jnp.sum(jnp.where(lane == idx, blk, 0.0), axis=1, keepdims=True)


def _pick_row(blk, idx):
    sub = lax.broadcasted_iota(jnp.int32, blk.shape, 0)
    return jnp.sum(jnp.where(sub == idx, blk, 0.0), axis=0, keepdims=True)


def _attention_fwd(h, c, ct, qblk, name):
    s = h.shape[0]
    t = _tile(s, 512)
    nq = s // t
    scale = HEAD_DIM ** -0.5
    nt_dims = (((1,), (1,)), ((), ()))

    def body(q_ref, k_ref, v_ref, c_ref, ct_ref, o_ref, lse_ref):
        p = pl.program_id(0)
        i = pl.program_id(1)
        lane = lax.broadcasted_iota(jnp.int32, (1, LANES), 1)
        first = lane < HEAD_DIM
        q = q_ref[...] * scale
        qa = jnp.where(first, q, 0.0).astype(BF16)
        qb = jnp.where(first, 0.0, q).astype(BF16)
        cblk = c_ref[...]
        cta = _pick_lane(cblk, 2 * p)
        ctb = _pick_lane(cblk, 2 * p + 1)

        def step(j, carry, diagonal):
            ma, la, mb, lb, acc = carry
            off = pl.multiple_of(j * t, t)
            k = k_ref[pl.ds(off, t), :].astype(BF16)
            v = v_ref[pl.ds(off, t), :].astype(BF16)
            crow = ct_ref[:, pl.ds(off, t)]

            def one(qh, cth, hd, m_old, l_old):
                sc = lax.dot_general(qh, k, nt_dims, preferred_element_type=F32) - _pick_row(crow, hd)
                if diagonal:
                    keep = lax.broadcasted_iota(jnp.int32, (t, t), 0) >= lax.broadcasted_iota(jnp.int32, (t, t), 1)
                    sc = jnp.where(keep, sc, NEG)
                m_new = jnp.maximum(m_old, jnp.max(sc, axis=1, keepdims=True) + cth)
                pr = jnp.exp(sc - (m_new - cth))
                alpha = jnp.exp(m_old - m_new)
                l_new = alpha * l_old + jnp.sum(pr, axis=1, keepdims=True)
                pv = jnp.dot(pr.astype(BF16), v, preferred_element_type=F32)
                return m_new, l_new, alpha, pv

            ma2, la2, aa, pva = one(qa, cta, 2 * p, ma, la)
            mb2, lb2, ab, pvb = one(qb, ctb, 2 * p + 1, mb, lb)
            acc = jnp.where(first, aa * acc + pva, ab * acc + pvb)
            return ma2, la2, mb2, lb2, acc

        init = (jnp.full((t, 1), NEG, F32), jnp.zeros((t, 1), F32), jnp.full((t, 1), NEG, F32),
                jnp.zeros((t, 1), F32), jnp.zeros((t, LANES), F32))
        carry = lax.fori_loop(0, i, lambda j, carry: step(j, carry, False), init)
        ma, la, mb, lb, acc = step(i, carry, True)
        o_ref[...] = (acc / jnp.where(first, la, lb)).astype(BF16)
        lse_ref[0] = jnp.broadcast_to(ma + jnp.log(la), (t, LANES))
        lse_ref[1] = jnp.broadcast_to(mb + jnp.log(lb), (t, LANES))

    return pl.pallas_call(
        body, name=name, grid=(N_HEADS // 2, nq),
        in_specs=[pl.BlockSpec((t, LANES), lambda p, i: (i, qblk + p)),
                  pl.BlockSpec((s, LANES), lambda p, i: (0, qblk + 4 + p)),
                  pl.BlockSpec((s, LANES), lambda p, i: (0, qblk + 8 + p)),
                  pl.BlockSpec((t, LANES), lambda p, i: (i, 0)),
                  pl.BlockSpec((8, s), lambda p, i: (0, 0))],
        out_specs=[pl.BlockSpec((t, LANES), lambda p, i: (i, p)),
                   pl.BlockSpec((2, t, LANES), lambda p, i: (p, i, 0))],
        out_shape=[jax.ShapeDtypeStruct((s, D_ATT), BF16), jax.ShapeDtypeStruct((N_HEADS, s, LANES), F32)],
        compiler_params=_params("parallel", "arbitrary"),
    )(h, h, h, c, ct)


def _attention_bwd(h, c, ct, lse, att, datt, qblk, name):
    s = h.shape[0]
    t = _tile(s, 512)
    nq = s // t
    scale = HEAD_DIM ** -0.5
    nt_dims = (((1,), (1,)), ((), ()))
    tn_dims = (((0,), (0,)), ((), ()))

    def body(q_ref, k_ref, v_ref, c_ref, ct_ref, lse_ref, o_ref, do_ref, dq_ref, dk_ref, dv_ref, dct_ref, dcq_ref):
        p = pl.program_id(0)
        j = pl.program_id(1)
        lane = lax.broadcasted_iota(jnp.int32, (1, LANES), 1)
        first = lane < HEAD_DIM
        kf = k_ref[...]
        vf = v_ref[...]
        k = kf.astype(BF16)
        ka = jnp.where(first, kf, 0.0).astype(BF16)
        kb = jnp.where(first, 0.0, kf).astype(BF16)
        va = jnp.where(first, vf, 0.0).astype(BF16)
        vb = jnp.where(first, 0.0, vf).astype(BF16)
        crow = ct_ref[...]
        csa = _pick_row(crow, 2 * p)
        csb = _pick_row(crow, 2 * p + 1)

        @pl.when(j == 0)
        def _():
            dq_ref[...] = jnp.zeros_like(dq_ref)
            dcq_ref[...] = jnp.zeros_like(dcq_ref)

        def step(i, carry, diagonal):
            dka, dkb, dva, dvb, dca, dcb = carry
            off = pl.multiple_of(i * t, t)
            rows = pl.ds(off, t)
            q = (q_ref[rows, :] * scale).astype(BF16)
            dof = do_ref[rows, :]
            do = dof.astype(BF16)
            prod = dof * o_ref[rows, :].astype(F32)
            cblk = c_ref[rows, :]

            def one(kh, vh, hd, csh, lse_h):
                sc = lax.dot_general(q, kh, nt_dims, preferred_element_type=F32) - csh
                if diagonal:
                    keep = lax.broadcasted_iota(jnp.int32, (t, t), 0) >= lax.broadcasted_iota(jnp.int32, (t, t), 1)
                    sc = jnp.where(keep, sc, NEG)
                pr = jnp.exp(sc - (jnp.max(lse_h, axis=1, keepdims=True) - _pick_lane(cblk, hd)))
                dp = lax.dot_general(do, vh, nt_dims, preferred_element_type=F32)
                return pr, dp

            pra, dpa = one(ka, va, 2 * p, csa, lse_ref[0, rows, :])
            prb, dpb = one(kb, vb, 2 * p + 1, csb, lse_ref[1, rows, :])
            dela = jnp.sum(jnp.where(first, prod, 0.0), axis=1, keepdims=True)
            delb = jnp.sum(jnp.where(first, 0.0, prod), axis=1, keepdims=True)
            dsa = pra * (dpa - dela)
            dsb = prb * (dpb - delb)
            dsa16 = dsa.astype(BF16)
            dsb16 = dsb.astype(BF16)
            dva = dva + lax.dot_general(pra.astype(BF16), do, tn_dims, preferred_element_type=F32)
            dvb = dvb + lax.dot_general(prb.astype(BF16), do, tn_dims, preferred_element_type=F32)
            dka = dka + lax.dot_general(dsa16, q, tn_dims, preferred_element_type=F32)
            dkb = dkb + lax.dot_general(dsb16, q, tn_dims, preferred_element_type=F32)
            dqa = jnp.dot(dsa16, k, preferred_element_type=F32)
            dqb = jnp.dot(dsb16, k, preferred_element_type=F32)
            dq_ref[rows, :] += scale * jnp.where(first, dqa, dqb)
            dca = dca - jnp.sum(dsa, axis=0, keepdims=True)
            dcb = dcb - jnp.sum(dsb, axis=0, keepdims=True)
            dcq_ref[rows, :] += jnp.where(lane == 0, jnp.sum(dsa, axis=1, keepdims=True),
                                          jnp.where(lane == 1, jnp.sum(dsb, axis=1, keepdims=True), 0.0))
            return dka, dkb, dva, dvb, dca, dcb

        z = jnp.zeros((t, LANES), F32)
        zr = jnp.zeros((1, t), F32)
        carry = step(j, (z, z, z, z, zr, zr), True)
        dka, dkb, dva, dvb, dca, dcb = lax.fori_loop(j + 1, nq, lambda i, carry: step(i, carry, False), carry)
        dk_ref[...] = jnp.where(first, dka, dkb)
        dv_ref[...] = jnp.where(first, dva, dvb)
        sub = lax.broadcasted_iota(jnp.int32, (8, t), 0)
        dct_ref[...] = jnp.where(sub == 0, dca, jnp.where(sub == 1, dcb, 0.0))

    full = lambda blk: pl.BlockSpec((s, LANES), blk)
    return pl.pallas_call(
        body, name=name, grid=(N_HEADS // 2, nq),
        in_specs=[full(lambda p, j: (0, qblk + p)),
                  pl.BlockSpec((t, LANES), lambda p, j: (j, qblk + 4 + p)),
                  pl.BlockSpec((t, LANES), lambda p, j: (j, qblk + 8 + p)),
                  full(lambda p, j: (0, 0)),
                  pl.BlockSpec((8, t), lambda p, j: (0, j)),
                  pl.BlockSpec((2, s, LANES), lambda p, j: (p, 0, 0)),
                  full(lambda p, j: (0, p)),
                  full(lambda p, j: (0, p))],
        out_specs=[full(lambda p, j: (0, p)),
                   pl.BlockSpec((t, LANES), lambda p, j: (j, p)),
                   pl.BlockSpec((t, LANES), lambda p, j: (j, p)),
                   pl.BlockSpec((None, 8, t), lambda p, j: (p, 0, j)),
                   pl.BlockSpec((None, s, LANES), lambda p, j: (p, 0, 0))],
        out_shape=[jax.ShapeDtypeStruct((s, D_ATT), F32), jax.ShapeDtypeStruct((s, D_ATT), F32),
                   jax.ShapeDtypeStruct((s, D_ATT), F32), jax.ShapeDtypeStruct((N_HEADS // 2, 8, s), F32),
                   jax.ShapeDtypeStruct((N_HEADS // 2, s, LANES), F32)],
        compiler_params=_params("arbitrary", "arbitrary"),
    )(h, h, h, c, ct, lse, att, datt)


def _sconv_fwd(h, w, bgblk, name):
    s = h.shape[0]
    nblk = D_CONV // LANES

    def body(bg_ref, cg_ref, hc_ref, w_ref, y_ref):
        rows = lax.broadcasted_iota(jnp.int32, (s, LANES), 0)
        y_ref[...] = (bg_ref[...] * _conv3(cg_ref[...] * hc_ref[...], w_ref, rows)).astype(BF16)

    col = lambda base: pl.BlockSpec((s, LANES), lambda j: (0, base + j))
    return pl.pallas_call(
        body, name=name, grid=(nblk,),
        in_specs=[col(bgblk), col(bgblk + nblk), col(bgblk + 2 * nblk), pl.BlockSpec((3, LANES), lambda j: (0, j))],
        out_specs=pl.BlockSpec((s, LANES), lambda j: (0, j)),
        out_shape=jax.ShapeDtypeStruct((s, D_CONV), BF16), compiler_params=_params("parallel"),
    )(h, h, h, w)


def _sconv_bwd(h, w, dy, bgblk, name):
    s = h.shape[0]
    nblk = D_CONV // LANES

    def body(bg_ref, cg_ref, hc_ref, w_ref, dy_ref, dbg_ref, dcg_ref, dhc_ref, dw_ref):
        rows = lax.broadcasted_iota(jnp.int32, (s, LANES), 0)
        cg, hc, dy, w = cg_ref[...], hc_ref[...], dy_ref[...], w_ref
        xin = cg * hc
        dbg_ref[...] = (dy * _conv3(xin, w, rows)).astype(BF16)
        dconv = dy * bg_ref[...]
        dxin = _conv3_t(dconv, w, rows)
        dcg_ref[...] = (dxin * hc).astype(BF16)
        dhc_ref[...] = (dxin * cg).astype(BF16)
        dw_ref[...] = _conv3_dw(dconv, xin, rows)

    col = lambda base: pl.BlockSpec((s, LANES), lambda j: (0, base + j))
    return pl.pallas_call(
        body, name=name, grid=(nblk,),
        in_specs=[col(bgblk), col(bgblk + nblk), col(bgblk + 2 * nblk), pl.BlockSpec((3, LANES), lambda j: (0, j)), col(0)],
        out_specs=[col(0), col(0), col(0), pl.BlockSpec((8, LANES), lambda j: (0, j))],
        out_shape=[jax.ShapeDtypeStruct((s, D_CONV), BF16)] * 3 + [jax.ShapeDtypeStruct((8, D_CONV), F32)],
        compiler_params=_params("parallel"),
    )(h, h, h, w, dy)


def _sgu_group_masks():
    lane = lax.broadcasted_iota(jnp.int32, (1, D_SGU), 1)
    return [(lane // HEAD_DIM) == g for g in range(N_SGU_GROUPS)]


def _sgu_tril():
    r = lax.broadcasted_iota(jnp.int32, (SGU_CHUNK, SGU_CHUNK), 0)
    c = lax.broadcasted_iota(jnp.int32, (SGU_CHUNK, SGU_CHUNK), 1)
    return r >= c


def _sgu_fwd(h, ln_g, ln_b, w_s, b_full, ublk, name):
    s = h.shape[0]
    tr = _tile(s, 512)
    nch = tr // SGU_CHUNK

    def body(u_ref, v_ref, g_ref, b_ref, w_ref, bf_ref, y_ref):
        masks = _sgu_group_masks()
        tril = _sgu_tril()
        wm = [jnp.where(tril, w_ref[g], 0.0).astype(BF16) for g in range(N_SGU_GROUPS)]
        vn = _layer_norm(_gelu(v_ref[...]), g_ref[...], b_ref[...])
        for ch in range(nch):
            rows = pl.ds(ch * SGU_CHUNK, SGU_CHUNK)
            vc = vn[ch * SGU_CHUNK:(ch + 1) * SGU_CHUNK, :]
            mixed = bf_ref[...]
            for g in range(N_SGU_GROUPS):
                mixed = mixed + jnp.dot(wm[g], jnp.where(masks[g], vc, 0.0).astype(BF16), preferred_element_type=F32)
            y_ref[rows, :] = (_gelu(u_ref[rows, :]) * mixed).astype(BF16)

    row = lambda blk: pl.BlockSpec((tr, D_SGU), lambda i: (i, blk))
    vec = pl.BlockSpec((1, D_SGU), lambda i: (0, 0))
    return pl.pallas_call(
        body, name=name, grid=(s // tr,),
        in_specs=[row(ublk), row(ublk + 1), vec, vec,
                  pl.BlockSpec((N_SGU_GROUPS, SGU_CHUNK, SGU_CHUNK), lambda i: (0, 0, 0)),
                  pl.BlockSpec((SGU_CHUNK, D_SGU), lambda i: (0, 0))],
        out_specs=row(0), out_shape=jax.ShapeDtypeStruct((s, D_SGU), BF16), compiler_params=_params("parallel"),
    )(h, h, ln_g, ln_b, w_s, b_full)


def _sgu_bwd(h, ln_g, ln_b, w_s, b_full, dy, ublk, name):
    s = h.shape[0]
    tr = _tile(s, 512)
    nch = tr // SGU_CHUNK
    nt_dims = (((1,), (1,)), ((), ()))

    def norm(v, g, b):
        return _layer_norm(_gelu(v), g, b)

    def body(u_ref, v_ref, g_ref, b_ref, w_ref, bf_ref, dy_ref, du_ref, dv_ref, dg_ref, db_ref, dw_ref, dbf_ref):
        masks = _sgu_group_masks()
        tril = _sgu_tril()
        wf = [jnp.where(tril, w_ref[g], 0.0) for g in range(N_SGU_GROUPS)]
        wm = [w.astype(BF16) for w in wf]
        wmt = [jnp.transpose(w).astype(BF16) for w in wf]
        vn, vjp = jax.vjp(norm, v_ref[...], g_ref[...], b_ref[...])

        @pl.when(pl.program_id(0) == 0)
        def _():
            dg_ref[...] = jnp.zeros_like(dg_ref)
            db_ref[...] = jnp.zeros_like(db_ref)
            dw_ref[...] = jnp.zeros_like(dw_ref)
            dbf_ref[...] = jnp.zeros_like(dbf_ref)

        dvn_parts = []
        for ch in range(nch):
            rows = pl.ds(ch * SGU_CHUNK, SGU_CHUNK)
            vc = vn[ch * SGU_CHUNK:(ch + 1) * SGU_CHUNK, :]
            vc16 = vc.astype(BF16)
            mixed = bf_ref[...]
            for g in range(N_SGU_GROUPS):
                mixed = mixed + jnp.dot(wm[g], jnp.where(masks[g], vc, 0.0).astype(BF16), preferred_element_type=F32)
            dy = dy_ref[rows, :]
            ug, gvjp = jax.vjp(_gelu, u_ref[rows, :])
            du_ref[rows, :] = gvjp(dy * mixed)[0].astype(BF16)
            dmixed = dy * ug
            dbf_ref[...] += dmixed
            dvc = jnp.zeros((SGU_CHUNK, D_SGU), F32)
            for g in range(N_SGU_GROUPS):
                dm16 = jnp.where(masks[g], dmixed, 0.0).astype(BF16)
                dw_ref[g] += jnp.where(tril, lax.dot_general(dm16, vc16, nt_dims, preferred_element_type=F32), 0.0)
                dvc = dvc + jnp.dot(wmt[g], dm16, preferred_element_type=F32)
            dvn_parts.append(dvc)
        dv, dg, db = vjp(jnp.concatenate(dvn_parts, axis=0))
        dv_ref[...] = dv.astype(BF16)
        dg_ref[...] += dg
        db_ref[...] += db

    row = lambda blk: pl.BlockSpec((tr, D_SGU), lambda i: (i, blk))
    vec = pl.BlockSpec((1, D_SGU), lambda i: (0, 0))
    wsp = pl.BlockSpec((N_SGU_GROUPS, SGU_CHUNK, SGU_CHUNK), lambda i: (0, 0, 0))
    bsp = pl.BlockSpec((SGU_CHUNK, D_SGU), lambda i: (0, 0))
    return pl.pallas_call(
        body, name=name, grid=(s // tr,),
        in_specs=[row(ublk), row(ublk + 1), vec, vec, wsp, bsp, row(0)],
        out_specs=[row(0), row(0), vec, vec, wsp, bsp],
        out_shape=[jax.ShapeDtypeStruct((s, D_SGU), BF16), jax.ShapeDtypeStruct((s, D_SGU), BF16),
                   jax.ShapeDtypeStruct((1, D_SGU), F32), jax.ShapeDtypeStruct((1, D_SGU), F32),
                   jax.ShapeDtypeStruct((N_SGU_GROUPS, SGU_CHUNK, SGU_CHUNK), F32),
                   jax.ShapeDtypeStruct((SGU_CHUNK, D_SGU), F32)],
        compiler_params=_params("arbitrary"),
    )(h, h, ln_g, ln_b, w_s, b_full, dy)


def _merge_fwd(h, b_gate, att, yc, ys, wa, wc, ws, name):
    s, d = att.shape[0], wa.shape[1]
    tm, tn = _tile(s, 512), _tile(d, 512)
    nj = d // tn

    def body(g0_ref, g1_ref, g2_ref, bg_ref, a_ref, c_ref, s_ref, wa_ref, wc_ref, ws_ref, o_ref):
        acc = jax.nn.sigmoid(g0_ref[...] + bg_ref[0:1, :]) * jnp.dot(a_ref[...], wa_ref[...], preferred_element_type=F32)
        acc += jax.nn.sigmoid(g1_ref[...] + bg_ref[1:2, :]) * jnp.dot(c_ref[...], wc_ref[...], preferred_element_type=F32)
        acc += jax.nn.sigmoid(g2_ref[...] + bg_ref[2:3, :]) * jnp.dot(s_ref[...], ws_ref[...], preferred_element_type=F32)
        o_ref[...] = acc.astype(BF16)

    gate = lambda b: pl.BlockSpec((tm, tn), lambda j, i: (i, b * nj + j))
    act = lambda k: pl.BlockSpec((tm, k), lambda j, i: (i, 0))
    wgt = lambda k: pl.BlockSpec((k, tn), lambda j, i: (0, j))
    return pl.pallas_call(
        body, name=name, grid=(nj, s // tm),
        in_specs=[gate(0), gate(1), gate(2), pl.BlockSpec((3, tn), lambda j, i: (0, j)),
                  act(D_ATT), act(D_CONV), act(D_SGU), wgt(D_ATT), wgt(D_CONV), wgt(D_SGU)],
        out_specs=pl.BlockSpec((tm, tn), lambda j, i: (i, j)),
        out_shape=jax.ShapeDtypeStruct((s, d), BF16), compiler_params=_params("parallel", "arbitrary"),
    )(h, h, h, b_gate, att, yc, ys, wa, wc, ws)


def _merge_bwd(h, b_gate, att, yc, ys, wa, wc, ws, dm, name):
    s, d = att.shape[0], wa.shape[1]
    tm, tn = _tile(s, 512), _tile(d, 512)
    nj = d // tn

    def body(g0_ref, g1_ref, g2_ref, bg_ref, a_ref, c_ref, s_ref, wa_ref, wc_ref, ws_ref, dm_ref,
             dya_ref, dyc_ref, dys_ref, dgl_ref, dbg_ref):
        dm = dm_ref[...]
        sums = []
        for b, (g_ref, x_ref, w_ref, dy_ref) in enumerate(((g0_ref, a_ref, wa_ref, dya_ref), (g1_ref, c_ref, wc_ref, dyc_ref),
                                                          (g2_ref, s_ref, ws_ref, dys_ref))):
            gate = jax.nn.sigmoid(g_ref[...] + bg_ref[b:b + 1, :])
            y = jnp.dot(x_ref[...], w_ref[...], preferred_element_type=F32)
            dy_ref[...] = (dm * gate).astype(BF16)
            dgl = dm * y * gate * (1.0 - gate)
            dgl_ref[b] = dgl.astype(BF16)
            sums.append(jnp.sum(dgl, axis=0, keepdims=True))
        sub = lax.broadcasted_iota(jnp.int32, (3, tn), 0)
        part = jnp.where(sub == 0, sums[0], jnp.where(sub == 1, sums[1], sums[2]))

        @pl.when(pl.program_id(1) == 0)
        def _():
            dbg_ref[...] = jnp.zeros_like(dbg_ref)

        dbg_ref[...] += part

    gate = lambda b: pl.BlockSpec((tm, tn), lambda j, i: (i, b * nj + j))
    act = lambda k: pl.BlockSpec((tm, k), lambda j, i: (i, 0))
    wgt = lambda k: pl.BlockSpec((k, tn), lambda j, i: (0, j))
    tile = pl.BlockSpec((tm, tn), lambda j, i: (i, j))
    return pl.pallas_call(
        body, name=name, grid=(nj, s // tm),
        in_specs=[gate(0), gate(1), gate(2), pl.BlockSpec((3, tn), lambda j, i: (0, j)),
                  act(D_ATT), act(D_CONV), act(D_SGU), wgt(D_ATT), wgt(D_CONV), wgt(D_SGU), tile],
        out_specs=[tile, tile, tile, pl.BlockSpec((3, tm, tn), lambda j, i: (0, i, j)),
                   pl.BlockSpec((3, tn), lambda j, i: (0, j))],
        out_shape=[jax.ShapeDtypeStruct((s, d), BF16)] * 3 + [jax.ShapeDtypeStruct((3, s, d), BF16),
                                                              jax.ShapeDtypeStruct((3, d), F32)],
        compiler_params=_params("parallel", "arbitrary"),
    )(h, h, h, b_gate, att, yc, ys, wa, wc, ws, dm)


def _ffn_act_fwd(hh, cw, name):
    s, dff = hh.shape[0], hh.shape[1] // 2
    nblk = dff // LANES

    def body(a_ref, b_ref, wa_ref, wb_ref, z_ref):
        rows = lax.broadcasted_iota(jnp.int32, (s, LANES), 0)
        z_ref[...] = (_gelu(_conv3(a_ref[...], wa_ref, rows)) * _conv3(b_ref[...], wb_ref, rows)).astype(BF16)

    col = lambda base: pl.BlockSpec((s, LANES), lambda j: (0, base + j))
    wsp = lambda base: pl.BlockSpec((3, LANES), lambda j: (0, base + j))
    return pl.pallas_call(
        body, name=name, grid=(nblk,), in_specs=[col(0), col(nblk), wsp(0), wsp(nblk)], out_specs=col(0),
        out_shape=jax.ShapeDtypeStruct((s, dff), BF16), compiler_params=_params("parallel"),
    )(hh, hh, cw, cw)


def _ffn_act_bwd(hh, cw, dz, name):
    s, dff = hh.shape[0], hh.shape[1] // 2
    nblk = dff // LANES

    def body(a_ref, b_ref, wa_ref, wb_ref, dz_ref, da_ref, db_ref, dwa_ref, dwb_ref):
        rows = lax.broadcasted_iota(jnp.int32, (s, LANES), 0)
        a, b, wa, wb, dz = a_ref[...], b_ref[...], wa_ref, wb_ref, dz_ref[...]
        ga, gvjp = jax.vjp(_gelu, _conv3(a, wa, rows))
        dca = gvjp(dz * _conv3(b, wb, rows))[0]
        dcb = dz * ga
        da_ref[...] = _conv3_t(dca, wa, rows).astype(BF16)
        db_ref[...] = _conv3_t(dcb, wb, rows).astype(BF16)
        dwa_ref[...] = _conv3_dw(dca, a, rows)
        dwb_ref[...] = _conv3_dw(dcb, b, rows)

    col = lambda base: pl.BlockSpec((s, LANES), lambda j: (0, base + j))
    wsp = lambda base: pl.BlockSpec((3, LANES), lambda j: (0, base + j))
    w8 = lambda base: pl.BlockSpec((8, LANES), lambda j: (0, base + j))
    return pl.pallas_call(
        body, name=name, grid=(nblk,), in_specs=[col(0), col(nblk), wsp(0), wsp(nblk), col(0)],
        out_specs=[col(0), col(0), w8(0), w8(0)],
        out_shape=[jax.ShapeDtypeStruct((s, dff), BF16)] * 2 + [jax.ShapeDtypeStruct((8, dff), F32)] * 2,
        compiler_params=_params("parallel"),
    )(hh, hh, cw, cw, dz)


ANY = pl.BlockSpec(memory_space=pl.ANY)


def _place():
    return lax.axis_index("x"), lax.axis_index("y"), lax.axis_index("c")


def _all_gather(arrs, name):
    n = len(arrs)

    def body(*refs):
        ins, outs = refs[:n], refs[n:2 * n]
        send_sems, recv_sems, local_sems = refs[2 * n:]
        x, y, c = _place()
        me, sibling = (x, y, c), (x, y, 1 - c)
        chips = [(1 - x, y), (x, 1 - y), (1 - x, 1 - y)]

        def slab(a, dev):
            return outs[a].at[4 * dev[0] + 2 * dev[1] + dev[2]]

        def copy(a, k, block, to, src=None):
            return pltpu.make_async_remote_copy(
                src_ref=slab(a, block) if src is None else src, dst_ref=slab(a, block),
                send_sem=send_sems.at[7 * a + k], recv_sem=recv_sems.at[7 * a + k], device_id=to, device_id_type=MESH)

        mine = [pltpu.make_async_copy(ins[a], slab(a, me), local_sems.at[a]) for a in range(n)]
        for cp in mine:
            cp.start()
        first = []
        for a in range(n):
            first.append(copy(a, 0, me, sibling, src=ins[a]))
            first += [copy(a, 1 + j, me, (*chip, c), src=ins[a]) for j, chip in enumerate(chips)]
        for cp in first:
            cp.start()
        passed = []
        for a in range(n):
            for j, chip in enumerate(chips):
                copy(a, 1 + j, (*chip, c), me).wait_recv()
                fwd = copy(a, 4 + j, (*chip, c), sibling)
                fwd.start()
                passed.append(fwd)
        for a in range(n):
            copy(a, 0, sibling, me).wait_recv()
            for j, chip in enumerate(chips):
                copy(a, 4 + j, (*chip, 1 - c), me).wait_recv()
        for cp in first + passed:
            cp.wait_send()
        for cp in mine:
            cp.wait()

    return pl.pallas_call(
        body, name=name, in_specs=[ANY] * n, out_specs=[ANY] * n,
        out_shape=[jax.ShapeDtypeStruct((N_DEV,) + a.shape, a.dtype) for a in arrs],
        scratch_shapes=[pltpu.SemaphoreType.DMA((7 * n,)), pltpu.SemaphoreType.DMA((7 * n,)), pltpu.SemaphoreType.DMA((n,))],
    )(*arrs)


HBM = pl.BlockSpec(memory_space=pltpu.HBM)
SEM = pl.BlockSpec(memory_space=pltpu.SEMAPHORE)
EFFECT = pltpu.SideEffectType.DATAFLOW_SIDE_EFFECTING


def _slot(dev):
    return 4 * dev[0] + 2 * dev[1] + dev[2]


def _exchange_copies(src_refs, land_refs, send_sems, recv_sems, src_view, land_view):
    x, y, c = _place()
    me = (x, y, c)
    peers = [(1 - x if r & 4 else x, 1 - y if r & 2 else y, 1 - c if r & 1 else c) for r in range(1, N_DEV)]
    sends, lands = [], []
    for a, (src, land) in enumerate(zip(src_refs, land_refs)):
        for k, peer in enumerate(peers):
            sems = dict(send_sem=send_sems.at[7 * a + k], recv_sem=recv_sems.at[7 * a + k], device_id=peer,
                        device_id_type=MESH)
            sends.append(pltpu.make_async_remote_copy(src_ref=src_view(src, _slot(peer)),
                                                      dst_ref=land_view(land, _slot(me)), **sems))
            lands.append(pltpu.make_async_remote_copy(src_ref=src_view(src, _slot(me)),
                                                      dst_ref=land_view(land, _slot(peer)), **sems))
    return sends, lands


def _own_copies(src_refs, land_refs, local_sems, src_view, land_view):
    me = _slot(_place())
    return [pltpu.make_async_copy(src_view(src, me), land_view(land, me), local_sems.at[a])
            for a, (src, land) in enumerate(zip(src_refs, land_refs))]


def _exchange_start(srcs, lands, after, src_view, land_view, name):
    n = len(srcs)

    def body(*refs):
        src_refs, land_refs = refs[:n], refs[n:2 * n]
        send_sems, recv_sems, local_sems = refs[2 * n + 1:2 * n + 4]
        token = refs[-1]
        sends, _ = _exchange_copies(src_refs, land_refs, send_sems, recv_sems, src_view, land_view)
        for cp in sends + _own_copies(src_refs, land_refs, local_sems, src_view, land_view):
            cp.start()
        token[...] = jnp.zeros_like(token)

    thru = [pltpu.HBM(a.shape, a.dtype) for a in list(srcs) + list(lands)]
    outs = pl.pallas_call(
        body, name=name,
        out_shape=(pltpu.SemaphoreType.DMA((7 * n,)), pltpu.SemaphoreType.DMA((7 * n,)), pltpu.SemaphoreType.DMA((n,)),
                   *thru, jax.ShapeDtypeStruct((8, LANES), F32)),
        in_specs=[HBM] * (2 * n) + [ANY],
        out_specs=(SEM, SEM, SEM, *([HBM] * (2 * n)), pl.BlockSpec(memory_space=pltpu.VMEM)),
        input_output_aliases={i: 3 + i for i in range(2 * n)},
        compiler_params=pltpu.CompilerParams(has_side_effects=EFFECT),
    )(*[pltpu.with_memory_space_constraint(a, pltpu.HBM) for a in list(srcs) + list(lands)], after)
    return outs[:3], list(outs[3:3 + n]), list(outs[3 + n:3 + 2 * n]), outs[-1]


def _exchange_wait(sems, srcs, lands, after, src_view, land_view, name):
    n = len(srcs)

    def body(*refs):
        src_refs, land_refs = refs[:n], refs[n:2 * n]
        send_sems, recv_sems, local_sems = refs[2 * n:2 * n + 3]
        sends, landed = _exchange_copies(src_refs, land_refs, send_sems, recv_sems, src_view, land_view)
        for cp in sends:
            cp.wait_send()
        for cp in landed:
            cp.wait_recv()
        for cp in _own_copies(src_refs, land_refs, local_sems, src_view, land_view):
            cp.wait()

    outs = pl.pallas_call(
        body, name=name, out_shape=[pltpu.HBM(a.shape, a.dtype) for a in list(srcs) + list(lands)],
        in_specs=[HBM] * (2 * n) + [SEM, SEM, SEM, ANY], out_specs=[HBM] * (2 * n),
        input_output_aliases={i: i for i in range(2 * n)},
        compiler_params=pltpu.CompilerParams(has_side_effects=EFFECT),
    )(*srcs, *lands, *sems, after)
    return list(outs[:n]), list(outs[n:])


ADAMW_BLOCK_BYTES = 1 << 19
PACK_ROWS = 256


def _adamw(slabs, w, m, v, name):
    nl, r, c = w.shape
    row_edges = [r] + [t for t in range(8, r, 8) if r % t == 0]
    col_edges = [c] + [t for t in range(LANES, c, LANES) if c % t == 0]
    fits = [(tr * tc, tc, tr) for tr in row_edges for tc in col_edges if tr * tc * 4 <= ADAMW_BLOCK_BYTES]
    _, tc, tr = max(fits) if fits else (0, min(col_edges), min(row_edges))

    def body(s_ref, w_ref, m_ref, v_ref, g_ref, d_ref, nm_ref, nv_ref):
        g = s_ref[0].astype(F32)
        for q in range(1, N_DEV):
            g = g + s_ref[q].astype(F32)
        m_new = ADAM_B1 * m_ref[...] + (1.0 - ADAM_B1) * g
        v_new = ADAM_B2 * v_ref[...] + (1.0 - ADAM_B2) * (g * g)
        m_hat = m_new / (1.0 - ADAM_B1 ** ADAM_STEP)
        v_hat = v_new / (1.0 - ADAM_B2 ** ADAM_STEP)
        g_ref[...] = g
        d_ref[...] = -ADAM_LR * (m_hat / (jnp.sqrt(v_hat) + ADAM_EPS) + ADAM_WD * w_ref[...])
        nm_ref[...] = m_new
        nv_ref[...] = v_new

    blk = pl.BlockSpec((None, tr, tc), lambda l, i, j: (l, i, j))
    return pl.pallas_call(
        body, name=name, grid=(nl, r // tr, c // tc),
        in_specs=[pl.BlockSpec((N_DEV, None, tr, tc), lambda l, i, j: (0, l, i, j)), blk, blk, blk],
        out_specs=[blk] * 4, out_shape=[jax.ShapeDtypeStruct(w.shape, F32)] * 4,
        compiler_params=_params("parallel", "parallel", "parallel"),
    )(slabs, w, m, v)


def _layout(d):
    off = {"gate": 0, "q": 3 * d}
    off["bg"] = off["q"] + 3 * D_ATT
    off["u"] = off["bg"] + 3 * D_CONV
    off["f"] = off["u"] + 2 * D_SGU
    width = -(-(off["f"] + LANES) // 512) * 512
    return off, width


def _pad_w_in(wt, d, token):
    off, width = _layout(d)
    nqkv, nrest = 3 * D_ATT, 3 * D_CONV + 2 * D_SGU
    pad = jnp.zeros((width - off["f"] - N_HEADS, wt.shape[1]), wt.dtype) + token[0, 0].astype(wt.dtype)
    return jnp.concatenate([wt[nqkv + N_HEADS + nrest:], wt[:nqkv], wt[nqkv + N_HEADS:nqkv + N_HEADS + nrest],
                            wt[nqkv:nqkv + N_HEADS], pad], axis=0)


def _unpad_w_in(wtp, d):
    off, _ = _layout(d)
    return jnp.concatenate([wtp[off["q"]:off["bg"]], wtp[off["f"]:off["f"] + N_HEADS], wtp[off["bg"]:off["f"]],
                            wtp[:off["q"]]], axis=0)


def _cols_from_slabs(g):
    return jnp.transpose(g, (1, 0, 2)).reshape(g.shape[1], N_DEV * g.shape[2])


def _cols_to_slabs(w):
    r, c = w.shape[0], w.shape[1] // N_DEV
    return jnp.transpose(w.reshape(r, N_DEV, c), (1, 0, 2))


def kernel(x, pre_mix_g, post_mix_g, pre_ffn_g, post_ffn_g, w_in, b_forget, b_gate, conv_mix_w, sgu_ln_g, sgu_ln_b, sgu_w, sgu_b, w_branch_att, w_branch_conv, w_branch_sgu, w_out, w_ffn_up, conv_ffn_w, w_ffn_down, loss_target, m_pre_mix_g, m_post_mix_g, m_pre_ffn_g, m_post_ffn_g, m_w_in, m_b_forget, m_b_gate, m_conv_mix_w, m_sgu_ln_g, m_sgu_ln_b, m_sgu_w, m_sgu_b, m_w_branch_att, m_w_branch_conv, m_w_branch_sgu, m_w_out, m_w_ffn_up, m_conv_ffn_w, m_w_ffn_down, v_pre_mix_g, v_post_mix_g, v_pre_ffn_g, v_post_ffn_g, v_w_in, v_b_forget, v_b_gate, v_conv_mix_w, v_sgu_ln_g, v_sgu_ln_b, v_sgu_w, v_sgu_b, v_w_branch_att, v_w_branch_conv, v_w_branch_sgu, v_w_out, v_w_ffn_up, v_conv_ffn_w, v_w_ffn_down):
    depth = w_in.shape[0]
    s, d = x.shape[1], x.shape[2]
    dff = w_ffn_down.shape[1] * N_DEV
    off, _ = _layout(d)
    qblk, bgblk, ublk, fblk = off["q"] // LANES, off["bg"] // LANES, off["u"] // D_SGU, off["f"] // LANES
    x0 = x.reshape(s, d)
    target = loss_target.reshape(s, d)
    ncm, ncf = conv_mix_w.shape[2], conv_ffn_w.shape[2]

    lo = d // N_DEV

    whole = lambda ref, slot: ref
    slab = lambda ref, slot: ref.at[slot]

    w_in_t, w_up_t = jnp.transpose(w_in, (0, 2, 1)), jnp.transpose(w_ffn_up, (0, 2, 1))

    def gather_start(l, part, after):
        if part == "mix":
            small = jnp.concatenate([b_gate[l], conv_mix_w[l], conv_ffn_w[l]], axis=1)
            shards = [w_in_t[l].astype(BF16), w_branch_att[l].astype(BF16), w_branch_conv[l].astype(BF16),
                      w_branch_sgu[l].astype(BF16), w_out[l].astype(BF16), small]
        else:
            shards = [w_up_t[l].astype(BF16), w_ffn_down[l].astype(BF16)]
        lands = [lax.empty((N_DEV,) + a.shape, a.dtype) for a in shards]
        return _exchange_start(shards, lands, after, whole, slab, name=f"gather_start_{part}_{l}")

    def gather_finish(l, part, started, after):
        sems, shards, lands, _ = started[part]
        shards, lands = _exchange_wait(sems, shards, lands, after, whole, slab, name=f"gather_wait_{part}_{l}")
        token = jnp.zeros((8, LANES), F32)
        if l + 1 < depth:
            started[part] = gather_start(l + 1, part, lands[0])
            token = started[part][3]
        return lands, token

    def bfull(l):
        return jnp.repeat(jnp.transpose(sgu_b[l]), HEAD_DIM, axis=1)

    def bf_pad(l):
        return jnp.pad(b_forget[l], (0, LANES - N_HEADS)).reshape(1, LANES)

    saved = []
    weights = []
    xin = x0
    xn = _prenorm(x0, pre_mix_g[0:1], name="prenorm_first")
    loss_acc = dy = None
    started = {"mix": gather_start(0, "mix", x0)}
    started["ffn"] = gather_start(0, "ffn", started["mix"][3])
    for l in range(depth):
        (g_in, g_a, g_c, g_s, g_o, g_small), token = gather_finish(l, "mix", started, xin if l else started["ffn"][3])
        g_small = _cols_from_slabs(g_small).reshape(3, N_DEV, -1)
        w = dict(w_in=_pad_w_in(g_in.reshape(N_DEV * g_in.shape[1], d), d, token), wa=_cols_from_slabs(g_a),
                 wc=_cols_from_slabs(g_c), ws=_cols_from_slabs(g_s), w_out=g_o.reshape(d, d),
                 b_gate=g_small[:, :, :lo].reshape(3, d), cmw=g_small[:, :, lo:lo + ncm].reshape(3, D_CONV),
                 cfw=g_small[:, :, lo + ncm:].reshape(3, 2 * dff))
        weights.append(w)
        h = _mm(xn, w["w_in"], "nt", F32, name="proj_in")
        c, ct = _forget_prep(h, bf_pad(l), fblk, name="forget_prep")
        att, lse = _attention_fwd(h, c, ct, qblk, name="attention_fwd")
        yc = _sconv_fwd(h, w["cmw"], bgblk, name="sconv_fwd")
        ys = _sgu_fwd(h, sgu_ln_g[l:l + 1], sgu_ln_b[l:l + 1], sgu_w[l], bfull(l), ublk, name="sgu_fwd")
        merged = _merge_fwd(h, w["b_gate"], att, yc, ys, w["wa"], w["wc"], w["ws"], name="merge_fwd")
        o = _mm(merged, w["w_out"], "nn", F32, name="proj_out")
        (g_up, g_dn), token = gather_finish(l, "ffn", started, o)
        w["w_up"], w["w_dn"] = g_up.reshape(2 * dff, d), g_dn.reshape(dff, d)
        x1, xn2 = _postnorm(xin, o, post_mix_g[l:l + 1], pre_ffn_g[l:l + 1] + token[0, 0], name="postnorm_mix")
        hh = _mm(xn2, w["w_up"], "nt", F32, name="ffn_up")
        z = _ffn_act_fwd(hh, w["cfw"], name="ffn_act_fwd")
        f = _mm(z, w["w_dn"], "nn", F32, name="ffn_down")
        saved.append(dict(xin=xin, xn=xn, h=h, c=c, ct=ct, lse=lse, att=att, yc=yc, ys=ys, merged=merged, o=o, x1=x1,
                          xn2=xn2, hh=hh, z=z, f=f))
        if l + 1 < depth:
            xin, xn = _postnorm(x1, f, post_ffn_g[l:l + 1], pre_mix_g[l + 1:l + 2], name="postnorm_ffn")
        else:
            dy, loss_acc = _postnorm_loss(x1, f, post_ffn_g[l:l + 1], target, name="postnorm_loss")
    loss = lax.psum(loss_acc[0, 0] * (0.5 / d), ("x", "y", "c"))

    rep = {k: [None] * depth for k in ("pre_mix_g", "post_mix_g", "pre_ffn_g", "post_ffn_g", "b_forget", "sgu_ln_g",
                                       "sgu_ln_b", "sgu_w", "sgu_b")}
    nsmall = lo + ncm + ncf
    lands = {"mix": [lax.empty((N_DEV, depth) + shp, dt) for shp, dt in (
                 (w_in_t.shape[1:], BF16), (w_branch_att.shape[1:], BF16), (w_branch_conv.shape[1:], BF16),
                 (w_branch_sgu.shape[1:], BF16), (w_out.shape[1:], BF16), ((3, nsmall), F32))],
             "ffn": [lax.empty((N_DEV, depth) + shp, BF16) for shp in (w_up_t.shape[1:], w_ffn_down.shape[1:])]}
    scatters = {"mix": [None] * depth, "ffn": [None] * depth}

    def scatter_start(l, part, sends, after):
        layer_slab = lambda ref, slot: ref.at[slot, l]
        sems, sends, lands[part], token = _exchange_start(sends, lands[part], after, slab, layer_slab,
                                                          name=f"scatter_start_{part}_{l}")
        scatters[part][l] = (sems, sends, layer_slab)
        return token

    def scatter_finish(part, after):
        for l in range(depth):
            sems, sends, layer_slab = scatters[part][l]
            _, lands[part] = _exchange_wait(sems, sends, lands[part], after, slab, layer_slab,
                                            name=f"scatter_wait_{part}_{l}")
        return lands[part]

    token = jnp.zeros((8, LANES), F32)
    dx = dy
    for l in reversed(range(depth)):
        w, a = weights[l], saved[l]
        df, rep["post_ffn_g"][l] = _postnorm_bwd(a["f"], post_ffn_g[l:l + 1] + token[0, 0], dx, name="postnorm_bwd")
        dz = _mm(df, w["w_dn"], "nt", F32, name="ffn_down_dx")
        g_dn = _mm(a["z"], df, "tn", BF16, name="ffn_down_dw")
        dha, dhb, dcwa, dcwb = _ffn_act_bwd(a["hh"], w["cfw"], dz, name="ffn_act_bwd")
        dhh = jnp.concatenate([dha, dhb], axis=1)
        dcfw = jnp.concatenate([dcwa[0:3], dcwb[0:3]], axis=1)
        g_up = _mm(dhh, a["xn2"], "tn", BF16, name="ffn_up_dw")
        token = scatter_start(l, "ffn", [g_up.reshape(N_DEV, 2 * dff // N_DEV, d), g_dn.reshape(N_DEV, dff // N_DEV, d)],
                              dz)
        dxn2 = _mm(dhh, w["w_up"], "nn", F32, name="ffn_up_dx")
        dx1, rep["pre_ffn_g"][l] = _prenorm_bwd(a["x1"], pre_ffn_g[l:l + 1] + token[0, 0], dxn2, dx, name="prenorm_bwd")
        do, rep["post_mix_g"][l] = _postnorm_bwd(a["o"], post_mix_g[l:l + 1], dx1, name="postnorm_bwd")
        dmerged = _mm(do, w["w_out"], "nt", F32, name="proj_out_dx")
        g_o = _mm(a["merged"], do, "tn", BF16, name="proj_out_dw")
        dya, dyc, dys, dgl, dbg = _merge_bwd(a["h"], w["b_gate"], a["att"], a["yc"], a["ys"], w["wa"], w["wc"], w["ws"],
                                             dmerged, name="merge_bwd")
        datt = _mm(dya, w["wa"], "nt", F32, name="branch_att_dx")
        dconv = _mm(dyc, w["wc"], "nt", F32, name="branch_conv_dx")
        dsgu = _mm(dys, w["ws"], "nt", F32, name="branch_sgu_dx")
        g_a = _mm(a["att"], dya, "tn", BF16, name="branch_att_dw")
        g_c = _mm(a["yc"], dyc, "tn", BF16, name="branch_conv_dw")
        g_s = _mm(a["ys"], dys, "tn", BF16, name="branch_sgu_dw")
        dq, dk, dv, dct4, dcq4 = _attention_bwd(a["h"], a["c"], a["ct"], a["lse"], a["att"], datt, qblk,
                                                name="attention_bwd")
        dfl, dbf = _forget_prep_bwd(a["h"], bf_pad(l), dct4, dcq4, fblk, name="forget_prep_bwd")
        rep["b_forget"][l] = dbf[0, :N_HEADS]
        dbgate, dcg, dhc, dcmw = _sconv_bwd(a["h"], w["cmw"], dconv, bgblk, name="sconv_bwd")
        du, dvs, dlg, dlb, dsw, dbfull = _sgu_bwd(a["h"], sgu_ln_g[l:l + 1], sgu_ln_b[l:l + 1], sgu_w[l], bfull(l), dsgu,
                                                  ublk, name="sgu_bwd")
        rep["sgu_ln_g"][l], rep["sgu_ln_b"][l], rep["sgu_w"][l] = dlg, dlb, dsw
        rep["sgu_b"][l] = jnp.transpose(jnp.sum(dbfull.reshape(SGU_CHUNK, N_SGU_GROUPS, HEAD_DIM), axis=2))
        dh = jnp.concatenate([dgl[0], dgl[1], dgl[2], dq.astype(BF16), dk.astype(BF16), dv.astype(BF16), dbgate, dcg, dhc,
                              du, dvs, dfl.astype(BF16), jnp.zeros((s, w["w_in"].shape[0] - off["f"] - LANES), BF16)], axis=1)
        g_in = _mm(dh, a["xn"], "tn", BF16, name="proj_in_dw")
        sends = [_unpad_w_in(g_in, d).reshape(N_DEV, -1, d), _cols_to_slabs(g_a), _cols_to_slabs(g_c), _cols_to_slabs(g_s),
                 g_o.reshape(N_DEV, d // N_DEV, d),
                 jnp.concatenate([_cols_to_slabs(dbg), _cols_to_slabs(dcmw[0:3]), _cols_to_slabs(dcfw)], axis=2)]
        token = scatter_start(l, "mix", sends, dx1)
        dxn = _mm(dh, w["w_in"], "nn", F32, name="proj_in_dx")
        dx, rep["pre_mix_g"][l] = _prenorm_bwd(a["xin"], pre_mix_g[l:l + 1] + token[0, 0], dxn, dx1, name="prenorm_bwd")

    outs = {}
    t3 = lambda arr: jnp.transpose(arr, (0, 2, 1))

    def update(name_, slabs, w_, m_, v_, transposed=False):
        if transposed:
            w_, m_, v_ = t3(w_), t3(m_), t3(v_)
        shp = w_.shape
        w3 = w_.reshape((shp[0], -1, shp[-1])) if w_.ndim >= 3 else w_.reshape((1,) + shp)
        res = _adamw(slabs.reshape((N_DEV,) + w3.shape), w3, m_.reshape(w3.shape), v_.reshape(w3.shape),
                     name="adamw_" + name_)
        outs[name_] = tuple(t3(t.reshape(shp)) if transposed else t.reshape(shp) for t in res)
        return res[0]

    got_up, got_dn = scatter_finish("ffn", dx)
    update("w_ffn_up", got_up, w_ffn_up, m_w_ffn_up, v_w_ffn_up, transposed=True)
    update("w_ffn_down", got_dn, w_ffn_down, m_w_ffn_down, v_w_ffn_down)

    rep_names = ("pre_mix_g", "post_mix_g", "pre_ffn_g", "post_ffn_g", "b_forget", "sgu_ln_g", "sgu_ln_b", "sgu_w", "sgu_b")
    rep_w = dict(pre_mix_g=(pre_mix_g, m_pre_mix_g, v_pre_mix_g), post_mix_g=(post_mix_g, m_post_mix_g, v_post_mix_g),
                 pre_ffn_g=(pre_ffn_g, m_pre_ffn_g, v_pre_ffn_g), post_ffn_g=(post_ffn_g, m_post_ffn_g, v_post_ffn_g),
                 b_forget=(b_forget, m_b_forget, v_b_forget), sgu_ln_g=(sgu_ln_g, m_sgu_ln_g, v_sgu_ln_g),
                 sgu_ln_b=(sgu_ln_b, m_sgu_ln_b, v_sgu_ln_b), sgu_w=(sgu_w, m_sgu_w, v_sgu_w), sgu_b=(sgu_b, m_sgu_b, v_sgu_b))

    def pack(parts):
        flat = jnp.concatenate([p.reshape(-1) for p in parts])
        return jnp.pad(flat, (0, -flat.shape[0] % (PACK_ROWS * LANES))).reshape(-1, LANES)

    part = pack([jnp.stack([g.reshape(rep_w[k][0].shape[1:]) for g in rep[k]]) for k in rep_names])
    (gathered,) = _all_gather([part], name="gather_small_grads")
    packed = [pack([rep_w[k][i] for k in rep_names]) for i in range(3)]
    res = _adamw(gathered.reshape(N_DEV, 1, -1, LANES), *[p.reshape(1, -1, LANES) for p in packed], name="adamw_replicated")
    pos = 0
    for k in rep_names:
        shp = rep_w[k][0].shape
        size = math.prod(shp)
        outs[k] = tuple(t.reshape(-1)[pos:pos + size].reshape(shp) for t in res)
        pos += size

    got_in, got_a, got_c, got_s, got_o, small = scatter_finish("mix", res[0])
    update("w_in", got_in, w_in, m_w_in, v_w_in, transposed=True)
    update("w_branch_att", got_a, w_branch_att, m_w_branch_att, v_w_branch_att)
    update("w_branch_conv", got_c, w_branch_conv, m_w_branch_conv, v_w_branch_conv)
    update("w_branch_sgu", got_s, w_branch_sgu, m_w_branch_sgu, v_w_branch_sgu)
    update("w_out", got_o, w_out, m_w_out, v_w_out)
    update("b_gate", small[..., :lo], b_gate, m_b_gate, v_b_gate)
    update("conv_mix_w", small[..., lo:lo + ncm], conv_mix_w, m_conv_mix_w, v_conv_mix_w)
    update("conv_ffn_w", small[..., lo + ncm:], conv_ffn_w, m_conv_ffn_w, v_conv_ffn_w)

    order = ("pre_mix_g", "post_mix_g", "pre_ffn_g", "post_ffn_g", "w_in", "b_forget", "b_gate", "conv_mix_w", "sgu_ln_g",
             "sgu_ln_b", "sgu_w", "sgu_b", "w_branch_att", "w_branch_conv", "w_branch_sgu", "w_out", "w_ffn_up",
             "conv_ffn_w", "w_ffn_down")
    grad_x = dx.reshape(x.shape)
    return (loss, grad_x, *[outs[k][0] for k in order], *[outs[k][1] for k in order], *[outs[k][2] for k in order],
            *[outs[k][3] for k in order])
```

```python
import functools
import math

import jax
import jax.numpy as jnp
from jax import lax
from jax.experimental import pallas as pl
from jax.experimental.pallas import tpu as pltpu

F32 = jnp.float32
BF16 = jnp.bfloat16

N_DEV = 8
HEAD_DIM = 64
N_HEADS = 8
D_ATT = 512
D_CONV = 256
D_SGU = 256
N_SGU_GROUPS = 4
SGU_CHUNK = 128
RMS_EPS = 1e-6
LN_EPS = 1e-5
ADAM_LR = 0.001
ADAM_B1 = 0.9
ADAM_B2 = 0.999
ADAM_EPS = 1e-08
ADAM_WD = 0.01
ADAM_STEP = 10
LANES = 128
VMEM_LIMIT = 56 * 1024 * 1024
ATT_TILE = 512
NEG = -1e30
MESH = pl.DeviceIdType.MESH


def _params(*sem):
    return pltpu.CompilerParams(dimension_semantics=sem if sem else None, vmem_limit_bytes=VMEM_LIMIT)


def _tile(n, cap):
    if n <= cap:
        return n
    t = cap - cap % LANES
    while n % t:
        t -= LANES
    return t


def _gelu(x):
    return 0.5 * x * (1.0 + jnp.tanh(math.sqrt(2.0 / math.pi) * (x + 0.044715 * (x * x * x))))


def _gelu_and_slope(x):
    k0, k1 = math.sqrt(2.0 / math.pi), 0.044715
    x2 = x * x
    t = jnp.tanh(x * (k0 + (k0 * k1) * x2))
    half = 0.5 * (1.0 + t)
    return x * half, half + (0.5 * x) * (1.0 - t * t) * (k0 + (3.0 * k0 * k1) * x2)


def _rms(x, g):
    r = lax.rsqrt(jnp.mean(x * x, axis=-1, keepdims=True) + RMS_EPS)
    return x * r * g


def _layer_norm(x, g, b):
    mu = jnp.mean(x, axis=-1, keepdims=True)
    xc = x - mu
    var = jnp.mean(xc * xc, axis=-1, keepdims=True)
    return xc * lax.rsqrt(var + LN_EPS) * g + b


def _shift_down(x, k, rows):
    return jnp.where(rows >= k, pltpu.roll(x, k, 0), 0.0)


def _shift_up(x, k, rows):
    s = x.shape[0]
    return jnp.where(rows < s - k, pltpu.roll(x, s - k, 0), 0.0)


def _conv3(x, w_ref, rows):
    return w_ref[2:3, :] * x + w_ref[1:2, :] * _shift_down(x, 1, rows) + w_ref[0:1, :] * _shift_down(x, 2, rows)


def _conv3_t(dy, w_ref, rows):
    return w_ref[2:3, :] * dy + w_ref[1:2, :] * _shift_up(dy, 1, rows) + w_ref[0:1, :] * _shift_up(dy, 2, rows)


def _conv3_dw(dy, x, rows):
    d2 = jnp.sum(dy * x, axis=0, keepdims=True)
    d1 = jnp.sum(dy * _shift_down(x, 1, rows), axis=0, keepdims=True)
    d0 = jnp.sum(dy * _shift_down(x, 2, rows), axis=0, keepdims=True)
    sub = lax.broadcasted_iota(jnp.int32, (8, x.shape[1]), 0)
    return jnp.where(sub == 0, d0, jnp.where(sub == 1, d1, jnp.where(sub == 2, d2, 0.0)))


MM_VMEM_BUDGET = 40 * 1024 * 1024
MM_TILE_CAP = 1408


def _mm_tiles(m, n, k, out_bytes):
    def edges(d):
        return [t for t in range(LANES, min(d, MM_TILE_CAP) + 1, LANES) if d % t == 0] or [d]

    best = None
    for tm in edges(m):
        for tn in edges(n):
            if 2 * (2 * k * (tm + tn) + tm * tn * out_bytes) > MM_VMEM_BUDGET:
                continue
            for a_outer in (True, False):
                reads = k * m + (m // tm) * k * n if a_outer else k * n + (n // tn) * k * m
                traffic = 2 * reads + m * n * out_bytes
                key = (traffic, -tm * tn)
                if best is None or key < best[0]:
                    best = (key, (tm, tn, a_outer))
    return best[1]


def _mm(a, b, form, out_dtype, name):
    if form == "nn":
        (m, k), n = a.shape, b.shape[1]
    elif form == "nt":
        (m, k), n = a.shape, b.shape[0]
    else:
        (k, m), n = a.shape, b.shape[1]
    tm, tn, a_outer = _mm_tiles(m, n, k, jnp.dtype(out_dtype).itemsize)
    dims = {"nn": (((1,), (0,)), ((), ())), "nt": (((1,), (1,)), ((), ())), "tn": (((0,), (0,)), ((), ()))}[form]

    def body(a_ref, b_ref, o_ref):
        o_ref[...] = lax.dot_general(a_ref[...], b_ref[...], dims, preferred_element_type=F32).astype(o_ref.dtype)

    ij = (lambda g0, g1: (g0, g1)) if a_outer else (lambda g0, g1: (g1, g0))
    a_spec = (pl.BlockSpec((k, tm), lambda g0, g1: (0, ij(g0, g1)[0])) if form == "tn"
              else pl.BlockSpec((tm, k), lambda g0, g1: (ij(g0, g1)[0], 0)))
    b_spec = (pl.BlockSpec((tn, k), lambda g0, g1: (ij(g0, g1)[1], 0)) if form == "nt"
              else pl.BlockSpec((k, tn), lambda g0, g1: (0, ij(g0, g1)[1])))
    return pl.pallas_call(
        body, name=name, grid=(m // tm, n // tn) if a_outer else (n // tn, m // tm),
        in_specs=[a_spec, b_spec], out_specs=pl.BlockSpec((tm, tn), lambda g0, g1: ij(g0, g1)),
        out_shape=jax.ShapeDtypeStruct((m, n), out_dtype),
        compiler_params=_params("parallel", "arbitrary"),
    )(a, b)


def _prenorm(x, g, name):
    s, d = x.shape
    tm = _tile(s, 512)

    def body(x_ref, g_ref, o_ref):
        o_ref[...] = _rms(x_ref[...], g_ref[...]).astype(BF16)

    return pl.pallas_call(
        body, name=name, grid=(s // tm,),
        in_specs=[pl.BlockSpec((tm, d), lambda i: (i, 0)), pl.BlockSpec((1, d), lambda i: (0, 0))],
        out_specs=pl.BlockSpec((tm, d), lambda i: (i, 0)),
        out_shape=jax.ShapeDtypeStruct((s, d), BF16), compiler_params=_params("parallel"),
    )(x, g)


def _postnorm(x, o, g_post, g_next, name):
    s, d = x.shape
    tm = _tile(s, 512)

    def body(x_ref, o_ref, gp_ref, gn_ref, x1_ref, xn_ref):
        x1 = x_ref[...] + _rms(o_ref[...], gp_ref[...])
        x1_ref[...] = x1
        xn_ref[...] = _rms(x1, gn_ref[...]).astype(BF16)

    row = pl.BlockSpec((tm, d), lambda i: (i, 0))
    vec = pl.BlockSpec((1, d), lambda i: (0, 0))
    return pl.pallas_call(
        body, name=name, grid=(s // tm,), in_specs=[row, row, vec, vec], out_specs=[row, row],
        out_shape=[jax.ShapeDtypeStruct((s, d), F32), jax.ShapeDtypeStruct((s, d), BF16)],
        compiler_params=_params("parallel"),
    )(x, o, g_post, g_next)


def _postnorm_loss(x, o, g_post, target, name):
    s, d = x.shape
    tm = _tile(s, 512)

    def body(x_ref, o_ref, gp_ref, t_ref, dy_ref, acc_ref):
        e = x_ref[...] + _rms(o_ref[...], gp_ref[...]) - t_ref[...]
        dy_ref[...] = e / d

        @pl.when(pl.program_id(0) == 0)
        def _():
            acc_ref[...] = jnp.zeros_like(acc_ref)

        acc_ref[...] += jnp.sum(jnp.sum(e * e, axis=1, keepdims=True), axis=0, keepdims=True)

    row = pl.BlockSpec((tm, d), lambda i: (i, 0))
    return pl.pallas_call(
        body, name=name, grid=(s // tm,),
        in_specs=[row, row, pl.BlockSpec((1, d), lambda i: (0, 0)), row],
        out_specs=[row, pl.BlockSpec((1, LANES), lambda i: (0, 0))],
        out_shape=[jax.ShapeDtypeStruct((s, d), F32), jax.ShapeDtypeStruct((1, LANES), F32)],
        compiler_params=_params("arbitrary"),
    )(x, o, g_post, target)


def _postnorm_bwd(o, g, dx, name):
    s, d = o.shape
    tm = _tile(s, 512)

    def body(o_ref, g_ref, dx_ref, do_ref, dg_ref):
        _, vjp = jax.vjp(_rms, o_ref[...], g_ref[...])
        d_o, dg = vjp(dx_ref[...])
        do_ref[...] = d_o.astype(BF16)

        @pl.when(pl.program_id(0) == 0)
        def _():
            dg_ref[...] = jnp.zeros_like(dg_ref)

        dg_ref[...] += dg

    row = pl.BlockSpec((tm, d), lambda i: (i, 0))
    vec = pl.BlockSpec((1, d), lambda i: (0, 0))
    return pl.pallas_call(
        body, name=name, grid=(s // tm,), in_specs=[row, vec, row], out_specs=[row, vec],
        out_shape=[jax.ShapeDtypeStruct((s, d), BF16), jax.ShapeDtypeStruct((1, d), F32)],
        compiler_params=_params("arbitrary"),
    )(o, g, dx)


def _prenorm_bwd(x, g, dxn, dres, name):
    s, d = x.shape
    tm = _tile(s, 512)

    def body(x_ref, g_ref, dxn_ref, dres_ref, dx_ref, dg_ref):
        _, vjp = jax.vjp(_rms, x_ref[...], g_ref[...])
        dx, dg = vjp(dxn_ref[...])
        dx_ref[...] = dres_ref[...] + dx

        @pl.when(pl.program_id(0) == 0)
        def _():
            dg_ref[...] = jnp.zeros_like(dg_ref)

        dg_ref[...] += dg

    row = pl.BlockSpec((tm, d), lambda i: (i, 0))
    vec = pl.BlockSpec((1, d), lambda i: (0, 0))
    return pl.pallas_call(
        body, name=name, grid=(s // tm,), in_specs=[row, vec, row, row], out_specs=[row, vec],
        out_shape=[jax.ShapeDtypeStruct((s, d), F32), jax.ShapeDtypeStruct((1, d), F32)],
        compiler_params=_params("arbitrary"),
    )(x, g, dxn, dres)


def _log_sigmoid(z):
    return jnp.minimum(z, 0.0) - jnp.log(1.0 + jnp.exp(-jnp.abs(z)))


def _forget_prep(h, bf_pad, fblk, name):
    s = h.shape[0]

    def body(f_ref, b_ref, c_ref, ct_ref):
        c = _log_sigmoid(f_ref[...] + b_ref[...])
        rows = lax.broadcasted_iota(jnp.int32, c.shape, 0)
        k = 1
        while k < s:
            c = c + _shift_down(c, k, rows)
            k *= 2
        c_ref[...] = c
        ct_ref[...] = jnp.transpose(c)[0:8, :]

    return pl.pallas_call(
        body, name=name, grid=(1,),
        in_specs=[pl.BlockSpec((s, LANES), lambda i: (0, fblk)), pl.BlockSpec((1, LANES), lambda i: (0, 0))],
        out_specs=[pl.BlockSpec((s, LANES), lambda i: (0, 0)), pl.BlockSpec((8, s), lambda i: (0, 0))],
        out_shape=[jax.ShapeDtypeStruct((s, LANES), F32), jax.ShapeDtypeStruct((8, s), F32)],
        compiler_params=_params("arbitrary"),
    )(h, bf_pad)


def _forget_prep_bwd(h, bf_pad, dct, dcq, fblk, name):
    s = h.shape[0]
    pairs = N_HEADS // 2

    def body(f_ref, b_ref, dct_ref, dcq_ref, df_ref, db_ref):
        dct = dct_ref[0]
        dcq = dcq_ref[0]
        for p in range(1, pairs):
            dct = dct + pltpu.roll(dct_ref[p], 2 * p, 0)
            dcq = dcq + pltpu.roll(dcq_ref[p], 2 * p, 1)
        dc = dcq + jnp.transpose(jnp.concatenate([dct, jnp.zeros((LANES - 8, s), F32)], axis=0))
        rows = lax.broadcasted_iota(jnp.int32, dc.shape, 0)
        k = 1
        while k < s:
            dc = dc + _shift_up(dc, k, rows)
            k *= 2
        z = f_ref[...] + b_ref[...]
        lane = lax.broadcasted_iota(jnp.int32, dc.shape, 1)
        df = jnp.where(lane < N_HEADS, dc * jax.nn.sigmoid(-z), 0.0)
        df_ref[...] = df
        db_ref[...] = jnp.sum(df, axis=0, keepdims=True)

    return pl.pallas_call(
        body, name=name, grid=(1,),
        in_specs=[pl.BlockSpec((s, LANES), lambda i: (0, fblk)), pl.BlockSpec((1, LANES), lambda i: (0, 0)),
                  pl.BlockSpec((pairs, 8, s), lambda i: (0, 0, 0)), pl.BlockSpec((pairs, s, LANES), lambda i: (0, 0, 0))],
        out_specs=[pl.BlockSpec((s, LANES), lambda i: (0, 0)), pl.BlockSpec((1, LANES), lambda i: (0, 0))],
        out_shape=[jax.ShapeDtypeStruct((s, LANES), F32), jax.ShapeDtypeStruct((1, LANES), F32)],
        compiler_params=_params("arbitrary"),
    )(h, bf_pad, dct, dcq)


def _pick_lane(blk, idx):
    lane = lax.broadcasted_iota(jnp.int32, blk.shape, 1)
    return jnp.sum(jnp.where(lane == idx, blk, 0.0), axis=1, keepdims=True)


def _pick_row(blk, idx):
    sub = lax.broadcasted_iota(jnp.int32, blk.shape, 0)
    return jnp.sum(jnp.where(sub == idx, blk, 0.0), axis=0, keepdims=True)


def _attention_fwd(h, c, ct, qblk, name):
    s = h.shape[0]
    t = _tile(s, ATT_TILE)
    nq = s // t
    scale = HEAD_DIM ** -0.5
    nt_dims = (((1,), (1,)), ((), ()))

    def body(q_ref, k_ref, v_ref, c_ref, ct_ref, o_ref, lse_ref):
        p = pl.program_id(0)
        i = pl.program_id(1)
        lane = lax.broadcasted_iota(jnp.int32, (1, LANES), 1)
        first = lane < HEAD_DIM
        q = q_ref[...] * scale
        qa = jnp.where(first, q, 0.0).astype(BF16)
        qb = jnp.where(first, 0.0, q).astype(BF16)
        cblk = c_ref[...]
        cta = _pick_lane(cblk, 2 * p)
        ctb = _pick_lane(cblk, 2 * p + 1)

        def step(j, carry, diagonal):
            ma, la, mb, lb, acc = carry
            off = pl.multiple_of(j * t, t)
            k = k_ref[pl.ds(off, t), :].astype(BF16)
            v = v_ref[pl.ds(off, t), :].astype(BF16)
            crow = ct_ref[:, pl.ds(off, t)]

            def one(qh, cth, hd, m_old, l_old):
                sc = lax.dot_general(qh, k, nt_dims, preferred_element_type=F32) - _pick_row(crow, hd)
                if diagonal:
                    keep = lax.broadcasted_iota(jnp.int32, (t, t), 0) >= lax.broadcasted_iota(jnp.int32, (t, t), 1)
                    sc = jnp.where(keep, sc, NEG)
                m_new = jnp.maximum(m_old, jnp.max(sc, axis=1, keepdims=True) + cth)
                pr = jnp.exp(sc - (m_new - cth))
                alpha = jnp.exp(m_old - m_new)
                l_new = alpha * l_old + jnp.sum(pr, axis=1, keepdims=True)
                pv = jnp.dot(pr.astype(BF16), v, preferred_element_type=F32)
                return m_new, l_new, alpha, pv

            ma2, la2, aa, pva = one(qa, cta, 2 * p, ma, la)
            mb2, lb2, ab, pvb = one(qb, ctb, 2 * p + 1, mb, lb)
            acc = jnp.where(first, aa * acc + pva, ab * acc + pvb)
            return ma2, la2, mb2, lb2, acc

        init = (jnp.full((t, 1), NEG, F32), jnp.zeros((t, 1), F32), jnp.full((t, 1), NEG, F32),
                jnp.zeros((t, 1), F32), jnp.zeros((t, LANES), F32))
        carry = lax.fori_loop(0, i, lambda j, carry: step(j, carry, False), init)
        ma, la, mb, lb, acc = step(i, carry, True)
        o_ref[...] = (acc / jnp.where(first, la, lb)).astype(BF16)
        lse_ref[0] = jnp.broadcast_to(ma + jnp.log(la), (t, LANES))
        lse_ref[1] = jnp.broadcast_to(mb + jnp.log(lb), (t, LANES))

    return pl.pallas_call(
        body, name=name, grid=(N_HEADS // 2, nq),
        in_specs=[pl.BlockSpec((t, LANES), lambda p, i: (i, qblk + p)),
                  pl.BlockSpec((s, LANES), lambda p, i: (0, qblk + 4 + p)),
                  pl.BlockSpec((s, LANES), lambda p, i: (0, qblk + 8 + p)),
                  pl.BlockSpec((t, LANES), lambda p, i: (i, 0)),
                  pl.BlockSpec((8, s), lambda p, i: (0, 0))],
        out_specs=[pl.BlockSpec((t, LANES), lambda p, i: (i, p)),
                   pl.BlockSpec((2, t, LANES), lambda p, i: (p, i, 0))],
        out_shape=[jax.ShapeDtypeStruct((s, D_ATT), BF16), jax.ShapeDtypeStruct((N_HEADS, s, LANES), F32)],
        compiler_params=_params("parallel", "arbitrary"),
    )(h, h, h, c, ct)


def _attention_bwd(h, c, ct, lse, att, datt, qblk, name):
    s = h.shape[0]
    t = _tile(s, ATT_TILE)
    nq = s // t
    scale = HEAD_DIM ** -0.5
    nt_dims = (((1,), (1,)), ((), ()))
    tn_dims = (((0,), (0,)), ((), ()))

    def body(q_ref, k_ref, v_ref, c_ref, ct_ref, lse_ref, o_ref, do_ref, dq_ref, dk_ref, dv_ref, dct_ref, dcq_ref):
        p = pl.program_id(0)
        j = pl.program_id(1)
        lane = lax.broadcasted_iota(jnp.int32, (1, LANES), 1)
        first = lane < HEAD_DIM
        kf = k_ref[...]
        vf = v_ref[...]
        k = kf.astype(BF16)
        ka = jnp.where(first, kf, 0.0).astype(BF16)
        kb = jnp.where(first, 0.0, kf).astype(BF16)
        va = jnp.where(first, vf, 0.0).astype(BF16)
        vb = jnp.where(first, 0.0, vf).astype(BF16)
        crow = ct_ref[...]
        csa = _pick_row(crow, 2 * p)
        csb = _pick_row(crow, 2 * p + 1)

        @pl.when(j == 0)
        def _():
            dq_ref[...] = jnp.zeros_like(dq_ref)
            dcq_ref[...] = jnp.zeros_like(dcq_ref)

        def step(i, carry, diagonal):
            dka, dkb, dva, dvb, dca, dcb = carry
            off = pl.multiple_of(i * t, t)
            rows = pl.ds(off, t)
            q = (q_ref[rows, :] * scale).astype(BF16)
            dof = do_ref[rows, :]
            do = dof.astype(BF16)
            prod = dof * o_ref[rows, :].astype(F32)
            cblk = c_ref[rows, :]

            def one(kh, vh, hd, csh, lse_h):
                sc = lax.dot_general(q, kh, nt_dims, preferred_element_type=F32) - csh
                if diagonal:
                    keep = lax.broadcasted_iota(jnp.int32, (t, t), 0) >= lax.broadcasted_iota(jnp.int32, (t, t), 1)
                    sc = jnp.where(keep, sc, NEG)
                pr = jnp.exp(sc - (jnp.max(lse_h, axis=1, keepdims=True) - _pick_lane(cblk, hd)))
                dp = lax.dot_general(do, vh, nt_dims, preferred_element_type=F32)
                return pr, dp

            pra, dpa = one(ka, va, 2 * p, csa, lse_ref[0, rows, :])
            prb, dpb = one(kb, vb, 2 * p + 1, csb, lse_ref[1, rows, :])
            dela = jnp.sum(jnp.where(first, prod, 0.0), axis=1, keepdims=True)
            delb = jnp.sum(jnp.where(first, 0.0, prod), axis=1, keepdims=True)
            dsa = pra * (dpa - dela)
            dsb = prb * (dpb - delb)
            dsa16 = dsa.astype(BF16)
            dsb16 = dsb.astype(BF16)
            dva = dva + lax.dot_general(pra.astype(BF16), do, tn_dims, preferred_element_type=F32)
            dvb = dvb + lax.dot_general(prb.astype(BF16), do, tn_dims, preferred_element_type=F32)
            dka = dka + lax.dot_general(dsa16, q, tn_dims, preferred_element_type=F32)
            dkb = dkb + lax.dot_general(dsb16, q, tn_dims, preferred_element_type=F32)
            dqa = jnp.dot(dsa16, k, preferred_element_type=F32)
            dqb = jnp.dot(dsb16, k, preferred_element_type=F32)
            dq_ref[rows, :] += scale * jnp.where(first, dqa, dqb)
            dca = dca - jnp.sum(dsa, axis=0, keepdims=True)
            dcb = dcb - jnp.sum(dsb, axis=0, keepdims=True)
            dcq_ref[rows, :] += jnp.where(lane == 0, jnp.sum(dsa, axis=1, keepdims=True),
                                          jnp.where(lane == 1, jnp.sum(dsb, axis=1, keepdims=True), 0.0))
            return dka, dkb, dva, dvb, dca, dcb

        z = jnp.zeros((t, LANES), F32)
        zr = jnp.zeros((1, t), F32)
        carry = step(j, (z, z, z, z, zr, zr), True)
        dka, dkb, dva, dvb, dca, dcb = lax.fori_loop(j + 1, nq, lambda i, carry: step(i, carry, False), carry)
        dk_ref[...] = jnp.where(first, dka, dkb)
        dv_ref[...] = jnp.where(first, dva, dvb)
        sub = lax.broadcasted_iota(jnp.int32, (8, t), 0)
        dct_ref[...] = jnp.where(sub == 0, dca, jnp.where(sub == 1, dcb, 0.0))

    full = lambda blk: pl.BlockSpec((s, LANES), blk)
    return pl.pallas_call(
        body, name=name, grid=(N_HEADS // 2, nq),
        in_specs=[full(lambda p, j: (0, qblk + p)),
                  pl.BlockSpec((t, LANES), lambda p, j: (j, qblk + 4 + p)),
                  pl.BlockSpec((t, LANES), lambda p, j: (j, qblk + 8 + p)),
                  full(lambda p, j: (0, 0)),
                  pl.BlockSpec((8, t), lambda p, j: (0, j)),
                  pl.BlockSpec((2, s, LANES), lambda p, j: (p, 0, 0)),
                  full(lambda p, j: (0, p)),
                  full(lambda p, j: (0, p))],
        out_specs=[full(lambda p, j: (0, p)),
                   pl.BlockSpec((t, LANES), lambda p, j: (j, p)),
                   pl.BlockSpec((t, LANES), lambda p, j: (j, p)),
                   pl.BlockSpec((None, 8, t), lambda p, j: (p, 0, j)),
                   pl.BlockSpec((None, s, LANES), lambda p, j: (p, 0, 0))],
        out_shape=[jax.ShapeDtypeStruct((s, D_ATT), F32), jax.ShapeDtypeStruct((s, D_ATT), F32),
                   jax.ShapeDtypeStruct((s, D_ATT), F32), jax.ShapeDtypeStruct((N_HEADS // 2, 8, s), F32),
                   jax.ShapeDtypeStruct((N_HEADS // 2, s, LANES), F32)],
        compiler_params=_params("arbitrary", "arbitrary"),
    )(h, h, h, c, ct, lse, att, datt)


def _sconv_fwd(h, w, bgblk, name):
    s = h.shape[0]
    nblk = D_CONV // LANES

    def body(bg_ref, cg_ref, hc_ref, w_ref, y_ref):
        rows = lax.broadcasted_iota(jnp.int32, (s, LANES), 0)
        y_ref[...] = (bg_ref[...] * _conv3(cg_ref[...] * hc_ref[...], w_ref, rows)).astype(BF16)

    col = lambda base: pl.BlockSpec((s, LANES), lambda j: (0, base + j))
    return pl.pallas_call(
        body, name=name, grid=(nblk,),
        in_specs=[col(bgblk), col(bgblk + nblk), col(bgblk + 2 * nblk), pl.BlockSpec((3, LANES), lambda j: (0, j))],
        out_specs=pl.BlockSpec((s, LANES), lambda j: (0, j)),
        out_shape=jax.ShapeDtypeStruct((s, D_CONV), BF16), compiler_params=_params("parallel"),
    )(h, h, h, w)


def _sconv_bwd(h, w, dy, bgblk, name):
    s = h.shape[0]
    nblk = D_CONV // LANES

    def body(bg_ref, cg_ref, hc_ref, w_ref, dy_ref, dbg_ref, dcg_ref, dhc_ref, dw_ref):
        rows = lax.broadcasted_iota(jnp.int32, (s, LANES), 0)
        cg, hc, dy, w = cg_ref[...], hc_ref[...], dy_ref[...], w_ref
        xin = cg * hc
        dbg_ref[...] = (dy * _conv3(xin, w, rows)).astype(BF16)
        dconv = dy * bg_ref[...]
        dxin = _conv3_t(dconv, w, rows)
        dcg_ref[...] = (dxin * hc).astype(BF16)
        dhc_ref[...] = (dxin * cg).astype(BF16)
        dw_ref[...] = _conv3_dw(dconv, xin, rows)

    col = lambda base: pl.BlockSpec((s, LANES), lambda j: (0, base + j))
    return pl.pallas_call(
        body, name=name, grid=(nblk,),
        in_specs=[col(bgblk), col(bgblk + nblk), col(bgblk + 2 * nblk), pl.BlockSpec((3, LANES), lambda j: (0, j)), col(0)],
        out_specs=[col(0), col(0), col(0), pl.BlockSpec((8, LANES), lambda j: (0, j))],
        out_shape=[jax.ShapeDtypeStruct((s, D_CONV), BF16)] * 3 + [jax.ShapeDtypeStruct((8, D_CONV), F32)],
        compiler_params=_params("parallel"),
    )(h, h, h, w, dy)


def _sgu_group_masks():
    lane = lax.broadcasted_iota(jnp.int32, (1, D_SGU), 1)
    return [(lane // HEAD_DIM) == g for g in range(N_SGU_GROUPS)]


def _sgu_tril():
    r = lax.broadcasted_iota(jnp.int32, (SGU_CHUNK, SGU_CHUNK), 0)
    c = lax.broadcasted_iota(jnp.int32, (SGU_CHUNK, SGU_CHUNK), 1)
    return r >= c


def _sgu_fwd(h, ln_g, ln_b, w_s, b_full, ublk, name):
    s = h.shape[0]
    tr = _tile(s, 512)
    nch = tr // SGU_CHUNK

    def body(u_ref, v_ref, g_ref, b_ref, w_ref, bf_ref, y_ref):
        masks = _sgu_group_masks()
        tril = _sgu_tril()
        wm = [jnp.where(tril, w_ref[g], 0.0).astype(BF16) for g in range(N_SGU_GROUPS)]
        vn = _layer_norm(_gelu(v_ref[...]), g_ref[...], b_ref[...])
        for ch in range(nch):
            rows = pl.ds(ch * SGU_CHUNK, SGU_CHUNK)
            vc = vn[ch * SGU_CHUNK:(ch + 1) * SGU_CHUNK, :]
            mixed = bf_ref[...]
            for g in range(N_SGU_GROUPS):
                mixed = mixed + jnp.dot(wm[g], jnp.where(masks[g], vc, 0.0).astype(BF16), preferred_element_type=F32)
            y_ref[rows, :] = (_gelu(u_ref[rows, :]) * mixed).astype(BF16)

    row = lambda blk: pl.BlockSpec((tr, D_SGU), lambda i: (i, blk))
    vec = pl.BlockSpec((1, D_SGU), lambda i: (0, 0))
    return pl.pallas_call(
        body, name=name, grid=(s // tr,),
        in_specs=[row(ublk), row(ublk + 1), vec, vec,
                  pl.BlockSpec((N_SGU_GROUPS, SGU_CHUNK, SGU_CHUNK), lambda i: (0, 0, 0)),
                  pl.BlockSpec((SGU_CHUNK, D_SGU), lambda i: (0, 0))],
        out_specs=row(0), out_shape=jax.ShapeDtypeStruct((s, D_SGU), BF16), compiler_params=_params("parallel"),
    )(h, h, ln_g, ln_b, w_s, b_full)


def _sgu_bwd(h, ln_g, ln_b, w_s, b_full, dy, ublk, name):
    s = h.shape[0]
    tr = _tile(s, 512)
    nch = tr // SGU_CHUNK
    nt_dims = (((1,), (1,)), ((), ()))

    def norm(v, g, b):
        return _layer_norm(_gelu(v), g, b)

    def body(u_ref, v_ref, g_ref, b_ref, w_ref, bf_ref, dy_ref, du_ref, dv_ref, dg_ref, db_ref, dw_ref, dbf_ref):
        masks = _sgu_group_masks()
        tril = _sgu_tril()
        wf = [jnp.where(tril, w_ref[g], 0.0) for g in range(N_SGU_GROUPS)]
        wm = [w.astype(BF16) for w in wf]
        wmt = [jnp.transpose(w).astype(BF16) for w in wf]
        vn, vjp = jax.vjp(norm, v_ref[...], g_ref[...], b_ref[...])

        @pl.when(pl.program_id(0) == 0)
        def _():
            dg_ref[...] = jnp.zeros_like(dg_ref)
            db_ref[...] = jnp.zeros_like(db_ref)
            dw_ref[...] = jnp.zeros_like(dw_ref)
            dbf_ref[...] = jnp.zeros_like(dbf_ref)

        dvn_parts = []
        for ch in range(nch):
            rows = pl.ds(ch * SGU_CHUNK, SGU_CHUNK)
            vc = vn[ch * SGU_CHUNK:(ch + 1) * SGU_CHUNK, :]
            vc16 = vc.astype(BF16)
            mixed = bf_ref[...]
            for g in range(N_SGU_GROUPS):
                mixed = mixed + jnp.dot(wm[g], jnp.where(masks[g], vc, 0.0).astype(BF16), preferred_element_type=F32)
            dy = dy_ref[rows, :]
            ug, slope = _gelu_and_slope(u_ref[rows, :])
            du_ref[rows, :] = (dy * mixed * slope).astype(BF16)
            dmixed = dy * ug
            dbf_ref[...] += dmixed
            dvc = jnp.zeros((SGU_CHUNK, D_SGU), F32)
            for g in range(N_SGU_GROUPS):
                dm16 = jnp.where(masks[g], dmixed, 0.0).astype(BF16)
                dw_ref[g] += jnp.where(tril, lax.dot_general(dm16, vc16, nt_dims, preferred_element_type=F32), 0.0)
                dvc = dvc + jnp.dot(wmt[g], dm16, preferred_element_type=F32)
            dvn_parts.append(dvc)
        dv, dg, db = vjp(jnp.concatenate(dvn_parts, axis=0))
        dv_ref[...] = dv.astype(BF16)
        dg_ref[...] += dg
        db_ref[...] += db

    row = lambda blk: pl.BlockSpec((tr, D_SGU), lambda i: (i, blk))
    vec = pl.BlockSpec((1, D_SGU), lambda i: (0, 0))
    wsp = pl.BlockSpec((N_SGU_GROUPS, SGU_CHUNK, SGU_CHUNK), lambda i: (0, 0, 0))
    bsp = pl.BlockSpec((SGU_CHUNK, D_SGU), lambda i: (0, 0))
    return pl.pallas_call(
        body, name=name, grid=(s // tr,),
        in_specs=[row(ublk), row(ublk + 1), vec, vec, wsp, bsp, row(0)],
        out_specs=[row(0), row(0), vec, vec, wsp, bsp],
        out_shape=[jax.ShapeDtypeStruct((s, D_SGU), BF16), jax.ShapeDtypeStruct((s, D_SGU), BF16),
                   jax.ShapeDtypeStruct((1, D_SGU), F32), jax.ShapeDtypeStruct((1, D_SGU), F32),
                   jax.ShapeDtypeStruct((N_SGU_GROUPS, SGU_CHUNK, SGU_CHUNK), F32),
                   jax.ShapeDtypeStruct((SGU_CHUNK, D_SGU), F32)],
        compiler_params=_params("arbitrary"),
    )(h, h, ln_g, ln_b, w_s, b_full, dy)


def _merge_fwd(h, b_gate, att, yc, ys, wa, wc, ws, name):
    s, d = att.shape[0], wa.shape[1]
    tm, tn = _tile(s, 512), _tile(d, 512)
    nj = d // tn

    def body(g0_ref, g1_ref, g2_ref, bg_ref, a_ref, c_ref, s_ref, wa_ref, wc_ref, ws_ref, o_ref):
        acc = jax.nn.sigmoid(g0_ref[...] + bg_ref[0:1, :]) * jnp.dot(a_ref[...], wa_ref[...], preferred_element_type=F32)
        acc += jax.nn.sigmoid(g1_ref[...] + bg_ref[1:2, :]) * jnp.dot(c_ref[...], wc_ref[...], preferred_element_type=F32)
        acc += jax.nn.sigmoid(g2_ref[...] + bg_ref[2:3, :]) * jnp.dot(s_ref[...], ws_ref[...], preferred_element_type=F32)
        o_ref[...] = acc.astype(BF16)

    gate = lambda b: pl.BlockSpec((tm, tn), lambda j, i: (i, b * nj + j))
    act = lambda k: pl.BlockSpec((tm, k), lambda j, i: (i, 0))
    wgt = lambda k: pl.BlockSpec((k, tn), lambda j, i: (0, j))
    return pl.pallas_call(
        body, name=name, grid=(nj, s // tm),
        in_specs=[gate(0), gate(1), gate(2), pl.BlockSpec((3, tn), lambda j, i: (0, j)),
                  act(D_ATT), act(D_CONV), act(D_SGU), wgt(D_ATT), wgt(D_CONV), wgt(D_SGU)],
        out_specs=pl.BlockSpec((tm, tn), lambda j, i: (i, j)),
        out_shape=jax.ShapeDtypeStruct((s, d), BF16), compiler_params=_params("parallel", "arbitrary"),
    )(h, h, h, b_gate, att, yc, ys, wa, wc, ws)


def _merge_bwd(h, b_gate, att, yc, ys, wa, wc, ws, dm, name):
    s, d = att.shape[0], wa.shape[1]
    tm, tn = _tile(s, 512), _tile(d, 512)
    nj = d // tn

    def body(g0_ref, g1_ref, g2_ref, bg_ref, a_ref, c_ref, s_ref, wa_ref, wc_ref, ws_ref, dm_ref,
             dya_ref, dyc_ref, dys_ref, dgl_ref, dbg_ref):
        dm = dm_ref[...]
        sums = []
        for b, (g_ref, x_ref, w_ref, dy_ref) in enumerate(((g0_ref, a_ref, wa_ref, dya_ref), (g1_ref, c_ref, wc_ref, dyc_ref),
                                                          (g2_ref, s_ref, ws_ref, dys_ref))):
            gate = jax.nn.sigmoid(g_ref[...] + bg_ref[b:b + 1, :])
            y = jnp.dot(x_ref[...], w_ref[...], preferred_element_type=F32)
            dy_ref[...] = (dm * gate).astype(BF16)
            dgl = dm * y * gate * (1.0 - gate)
            dgl_ref[b] = dgl.astype(BF16)
            sums.append(jnp.sum(dgl, axis=0, keepdims=True))
        sub = lax.broadcasted_iota(jnp.int32, (3, tn), 0)
        part = jnp.where(sub == 0, sums[0], jnp.where(sub == 1, sums[1], sums[2]))

        @pl.when(pl.program_id(1) == 0)
        def _():
            dbg_ref[...] = jnp.zeros_like(dbg_ref)

        dbg_ref[...] += part

    gate = lambda b: pl.BlockSpec((tm, tn), lambda j, i: (i, b * nj + j))
    act = lambda k: pl.BlockSpec((tm, k), lambda j, i: (i, 0))
    wgt = lambda k: pl.BlockSpec((k, tn), lambda j, i: (0, j))
    tile = pl.BlockSpec((tm, tn), lambda j, i: (i, j))
    return pl.pallas_call(
        body, name=name, grid=(nj, s // tm),
        in_specs=[gate(0), gate(1), gate(2), pl.BlockSpec((3, tn), lambda j, i: (0, j)),
                  act(D_ATT), act(D_CONV), act(D_SGU), wgt(D_ATT), wgt(D_CONV), wgt(D_SGU), tile],
        out_specs=[tile, tile, tile, pl.BlockSpec((3, tm, tn), lambda j, i: (0, i, j)),
                   pl.BlockSpec((3, tn), lambda j, i: (0, j))],
        out_shape=[jax.ShapeDtypeStruct((s, d), BF16)] * 3 + [jax.ShapeDtypeStruct((3, s, d), BF16),
                                                              jax.ShapeDtypeStruct((3, d), F32)],
        compiler_params=_params("parallel", "arbitrary"),
    )(h, h, h, b_gate, att, yc, ys, wa, wc, ws, dm)


def _ffn_act_fwd(hh, cw, name):
    s, dff = hh.shape[0], hh.shape[1] // 2
    nblk = dff // LANES

    def body(a_ref, b_ref, wa_ref, wb_ref, z_ref):
        rows = lax.broadcasted_iota(jnp.int32, (s, LANES), 0)
        z_ref[...] = (_gelu(_conv3(a_ref[...], wa_ref, rows)) * _conv3(b_ref[...], wb_ref, rows)).astype(BF16)

    col = lambda base: pl.BlockSpec((s, LANES), lambda j: (0, base + j))
    wsp = lambda base: pl.BlockSpec((3, LANES), lambda j: (0, base + j))
    return pl.pallas_call(
        body, name=name, grid=(nblk,), in_specs=[col(0), col(nblk), wsp(0), wsp(nblk)], out_specs=col(0),
        out_shape=jax.ShapeDtypeStruct((s, dff), BF16), compiler_params=_params("parallel"),
    )(hh, hh, cw, cw)


def _ffn_act_bwd(hh, cw, dz, name):
    s, dff = hh.shape[0], hh.shape[1] // 2
    nblk = dff // LANES

    def body(a_ref, b_ref, wa_ref, wb_ref, dz_ref, da_ref, db_ref, dwa_ref, dwb_ref):
        rows = lax.broadcasted_iota(jnp.int32, (s, LANES), 0)
        a, b, wa, wb, dz = a_ref[...], b_ref[...], wa_ref, wb_ref, dz_ref[...]
        ga, slope = _gelu_and_slope(_conv3(a, wa, rows))
        dca = dz * _conv3(b, wb, rows) * slope
        dcb = dz * ga
        da_ref[...] = _conv3_t(dca, wa, rows).astype(BF16)
        db_ref[...] = _conv3_t(dcb, wb, rows).astype(BF16)
        dwa_ref[...] = _conv3_dw(dca, a, rows)
        dwb_ref[...] = _conv3_dw(dcb, b, rows)

    col = lambda base: pl.BlockSpec((s, LANES), lambda j: (0, base + j))
    wsp = lambda base: pl.BlockSpec((3, LANES), lambda j: (0, base + j))
    w8 = lambda base: pl.BlockSpec((8, LANES), lambda j: (0, base + j))
    return pl.pallas_call(
        body, name=name, grid=(nblk,), in_specs=[col(0), col(nblk), wsp(0), wsp(nblk), col(0)],
        out_specs=[col(0), col(0), w8(0), w8(0)],
        out_shape=[jax.ShapeDtypeStruct((s, dff), BF16)] * 2 + [jax.ShapeDtypeStruct((8, dff), F32)] * 2,
        compiler_params=_params("parallel"),
    )(hh, hh, cw, cw, dz)


ANY = pl.BlockSpec(memory_space=pl.ANY)


def _place():
    return lax.axis_index("x"), lax.axis_index("y"), lax.axis_index("c")


def _all_gather(arrs, name):
    n = len(arrs)

    def body(*refs):
        ins, outs = refs[:n], refs[n:2 * n]
        send_sems, recv_sems, local_sems = refs[2 * n:]
        x, y, c = _place()
        me, sibling = (x, y, c), (x, y, 1 - c)
        chips = [(1 - x, y), (x, 1 - y), (1 - x, 1 - y)]

        def slab(a, dev):
            return outs[a].at[4 * dev[0] + 2 * dev[1] + dev[2]]

        def copy(a, k, block, to, src=None):
            return pltpu.make_async_remote_copy(
                src_ref=slab(a, block) if src is None else src, dst_ref=slab(a, block),
                send_sem=send_sems.at[7 * a + k], recv_sem=recv_sems.at[7 * a + k], device_id=to, device_id_type=MESH)

        mine = [pltpu.make_async_copy(ins[a], slab(a, me), local_sems.at[a]) for a in range(n)]
        for cp in mine:
            cp.start()
        first = []
        for a in range(n):
            first.append(copy(a, 0, me, sibling, src=ins[a]))
            first += [copy(a, 1 + j, me, (*chip, c), src=ins[a]) for j, chip in enumerate(chips)]
        for cp in first:
            cp.start()
        passed = []
        for a in range(n):
            for j, chip in enumerate(chips):
                copy(a, 1 + j, (*chip, c), me).wait_recv()
                fwd = copy(a, 4 + j, (*chip, c), sibling)
                fwd.start()
                passed.append(fwd)
        for a in range(n):
            copy(a, 0, sibling, me).wait_recv()
            for j, chip in enumerate(chips):
                copy(a, 4 + j, (*chip, 1 - c), me).wait_recv()
        for cp in first + passed:
            cp.wait_send()
        for cp in mine:
            cp.wait()

    return pl.pallas_call(
        body, name=name, in_specs=[ANY] * n, out_specs=[ANY] * n,
        out_shape=[jax.ShapeDtypeStruct((N_DEV,) + a.shape, a.dtype) for a in arrs],
        scratch_shapes=[pltpu.SemaphoreType.DMA((7 * n,)), pltpu.SemaphoreType.DMA((7 * n,)), pltpu.SemaphoreType.DMA((n,))],
    )(*arrs)


HBM = pl.BlockSpec(memory_space=pltpu.HBM)
SEM = pl.BlockSpec(memory_space=pltpu.SEMAPHORE)
EFFECT = pltpu.SideEffectType.DATAFLOW_SIDE_EFFECTING


def _slot(dev):
    return 4 * dev[0] + 2 * dev[1] + dev[2]


def _exchange_copies(src_refs, land_refs, send_sems, recv_sems, src_view, land_view):
    x, y, c = _place()
    me = (x, y, c)
    peers = [(1 - x if r & 4 else x, 1 - y if r & 2 else y, 1 - c if r & 1 else c) for r in range(1, N_DEV)]
    sends, lands = [], []
    for a, (src, land) in enumerate(zip(src_refs, land_refs)):
        for k, peer in enumerate(peers):
            sems = dict(send_sem=send_sems.at[7 * a + k], recv_sem=recv_sems.at[7 * a + k], device_id=peer,
                        device_id_type=MESH)
            sends.append(pltpu.make_async_remote_copy(src_ref=src_view(src, _slot(peer)),
                                                      dst_ref=land_view(land, _slot(me)), **sems))
            lands.append(pltpu.make_async_remote_copy(src_ref=src_view(src, _slot(me)),
                                                      dst_ref=land_view(land, _slot(peer)), **sems))
    return sends, lands


def _own_copies(src_refs, land_refs, local_sems, src_view, land_view):
    me = _slot(_place())
    return [pltpu.make_async_copy(src_view(src, me), land_view(land, me), local_sems.at[a])
            for a, (src, land) in enumerate(zip(src_refs, land_refs))]


def _exchange_start(srcs, lands, after, src_view, land_view, name):
    n = len(srcs)

    def body(*refs):
        src_refs, land_refs = refs[:n], refs[n:2 * n]
        send_sems, recv_sems, local_sems = refs[2 * n + 1:2 * n + 4]
        token = refs[-1]
        sends, _ = _exchange_copies(src_refs, land_refs, send_sems, recv_sems, src_view, land_view)
        for cp in sends + _own_copies(src_refs, land_refs, local_sems, src_view, land_view):
            cp.start()
        token[...] = jnp.zeros_like(token)

    thru = [pltpu.HBM(a.shape, a.dtype) for a in list(srcs) + list(lands)]
    outs = pl.pallas_call(
        body, name=name,
        out_shape=(pltpu.SemaphoreType.DMA((7 * n,)), pltpu.SemaphoreType.DMA((7 * n,)), pltpu.SemaphoreType.DMA((n,)),
                   *thru, jax.ShapeDtypeStruct((8, LANES), F32)),
        in_specs=[HBM] * (2 * n) + [ANY],
        out_specs=(SEM, SEM, SEM, *([HBM] * (2 * n)), pl.BlockSpec(memory_space=pltpu.VMEM)),
        input_output_aliases={i: 3 + i for i in range(2 * n)},
        compiler_params=pltpu.CompilerParams(has_side_effects=EFFECT),
    )(*[pltpu.with_memory_space_constraint(a, pltpu.HBM) for a in list(srcs) + list(lands)], after)
    return outs[:3], list(outs[3:3 + n]), list(outs[3 + n:3 + 2 * n]), outs[-1]


def _exchange_wait(sems, srcs, lands, after, src_view, land_view, name):
    n = len(srcs)

    def body(*refs):
        src_refs, land_refs = refs[:n], refs[n:2 * n]
        send_sems, recv_sems, local_sems = refs[2 * n:2 * n + 3]
        sends, landed = _exchange_copies(src_refs, land_refs, send_sems, recv_sems, src_view, land_view)
        for cp in sends:
            cp.wait_send()
        for cp in landed:
            cp.wait_recv()
        for cp in _own_copies(src_refs, land_refs, local_sems, src_view, land_view):
            cp.wait()

    outs = pl.pallas_call(
        body, name=name, out_shape=[pltpu.HBM(a.shape, a.dtype) for a in list(srcs) + list(lands)],
        in_specs=[HBM] * (2 * n) + [SEM, SEM, SEM, ANY], out_specs=[HBM] * (2 * n),
        input_output_aliases={i: i for i in range(2 * n)},
        compiler_params=pltpu.CompilerParams(has_side_effects=EFFECT),
    )(*srcs, *lands, *sems, after)
    return list(outs[:n]), list(outs[n:])


ADAMW_BLOCK_BYTES = 1 << 19
PACK_ROWS = 256


def _adamw(slabs, w, m, v, name):
    nl, r, c = w.shape
    row_edges = [r] + [t for t in range(8, r, 8) if r % t == 0]
    col_edges = [c] + [t for t in range(LANES, c, LANES) if c % t == 0]
    fits = [(tr * tc, tc, tr) for tr in row_edges for tc in col_edges if tr * tc * 4 <= ADAMW_BLOCK_BYTES]
    _, tc, tr = max(fits) if fits else (0, min(col_edges), min(row_edges))

    def body(s_ref, w_ref, m_ref, v_ref, g_ref, d_ref, nm_ref, nv_ref):
        g = s_ref[0].astype(F32)
        for q in range(1, N_DEV):
            g = g + s_ref[q].astype(F32)
        m_new = ADAM_B1 * m_ref[...] + (1.0 - ADAM_B1) * g
        v_new = ADAM_B2 * v_ref[...] + (1.0 - ADAM_B2) * (g * g)
        m_hat = m_new / (1.0 - ADAM_B1 ** ADAM_STEP)
        v_hat = v_new / (1.0 - ADAM_B2 ** ADAM_STEP)
        g_ref[...] = g
        d_ref[...] = -ADAM_LR * (m_hat / (jnp.sqrt(v_hat) + ADAM_EPS) + ADAM_WD * w_ref[...])
        nm_ref[...] = m_new
        nv_ref[...] = v_new

    blk = pl.BlockSpec((None, tr, tc), lambda l, i, j: (l, i, j))
    return pl.pallas_call(
        body, name=name, grid=(nl, r // tr, c // tc),
        in_specs=[pl.BlockSpec((N_DEV, None, tr, tc), lambda l, i, j: (0, l, i, j)), blk, blk, blk],
        out_specs=[blk] * 4, out_shape=[jax.ShapeDtypeStruct(w.shape, F32)] * 4,
        compiler_params=_params("parallel", "parallel", "parallel"),
    )(slabs, w, m, v)


def _layout(d):
    off = {"gate": 0, "q": 3 * d}
    off["bg"] = off["q"] + 3 * D_ATT
    off["u"] = off["bg"] + 3 * D_CONV
    off["f"] = off["u"] + 2 * D_SGU
    width = -(-(off["f"] + LANES) // 512) * 512
    return off, width


def _pad_w_in(wt, d, token):
    off, width = _layout(d)
    nqkv, nrest = 3 * D_ATT, 3 * D_CONV + 2 * D_SGU
    pad = jnp.zeros((width - off["f"] - N_HEADS, wt.shape[1]), wt.dtype) + token[0, 0].astype(wt.dtype)
    return jnp.concatenate([wt[nqkv + N_HEADS + nrest:], wt[:nqkv], wt[nqkv + N_HEADS:nqkv + N_HEADS + nrest],
                            wt[nqkv:nqkv + N_HEADS], pad], axis=0)


def _unpad_w_in(wtp, d):
    off, _ = _layout(d)
    return jnp.concatenate([wtp[off["q"]:off["bg"]], wtp[off["f"]:off["f"] + N_HEADS], wtp[off["bg"]:off["f"]],
                            wtp[:off["q"]]], axis=0)


def _cols_from_slabs(g):
    return jnp.transpose(g, (1, 0, 2)).reshape(g.shape[1], N_DEV * g.shape[2])


def _cols_to_slabs(w):
    r, c = w.shape[0], w.shape[1] // N_DEV
    return jnp.transpose(w.reshape(r, N_DEV, c), (1, 0, 2))


def kernel(x, pre_mix_g, post_mix_g, pre_ffn_g, post_ffn_g, w_in, b_forget, b_gate, conv_mix_w, sgu_ln_g, sgu_ln_b, sgu_w, sgu_b, w_branch_att, w_branch_conv, w_branch_sgu, w_out, w_ffn_up, conv_ffn_w, w_ffn_down, loss_target, m_pre_mix_g, m_post_mix_g, m_pre_ffn_g, m_post_ffn_g, m_w_in, m_b_forget, m_b_gate, m_conv_mix_w, m_sgu_ln_g, m_sgu_ln_b, m_sgu_w, m_sgu_b, m_w_branch_att, m_w_branch_conv, m_w_branch_sgu, m_w_out, m_w_ffn_up, m_conv_ffn_w, m_w_ffn_down, v_pre_mix_g, v_post_mix_g, v_pre_ffn_g, v_post_ffn_g, v_w_in, v_b_forget, v_b_gate, v_conv_mix_w, v_sgu_ln_g, v_sgu_ln_b, v_sgu_w, v_sgu_b, v_w_branch_att, v_w_branch_conv, v_w_branch_sgu, v_w_out, v_w_ffn_up, v_conv_ffn_w, v_w_ffn_down):
    depth = w_in.shape[0]
    s, d = x.shape[1], x.shape[2]
    dff = w_ffn_down.shape[1] * N_DEV
    off, _ = _layout(d)
    qblk, bgblk, ublk, fblk = off["q"] // LANES, off["bg"] // LANES, off["u"] // D_SGU, off["f"] // LANES
    x0 = x.reshape(s, d)
    target = loss_target.reshape(s, d)
    ncm, ncf = conv_mix_w.shape[2], conv_ffn_w.shape[2]

    lo = d // N_DEV

    whole = lambda ref, slot: ref
    slab = lambda ref, slot: ref.at[slot]

    w_in_t, w_up_t = jnp.transpose(w_in, (0, 2, 1)), jnp.transpose(w_ffn_up, (0, 2, 1))

    def shards_of(l, part):
        if part == "mix":
            small = jnp.concatenate([b_gate[l], conv_mix_w[l], conv_ffn_w[l]], axis=1)
            return [w_in_t[l].astype(BF16), w_branch_att[l].astype(BF16), w_branch_conv[l].astype(BF16),
                    w_branch_sgu[l].astype(BF16), w_out[l].astype(BF16), small]
        return [w_up_t[l].astype(BF16), w_ffn_down[l].astype(BF16)]

    def gather_start(l, part, after):
        shards = shards_of(l, part)
        lands = [lax.empty((N_DEV,) + a.shape, a.dtype) for a in shards]
        return _exchange_start(shards, lands, after, whole, slab, name=f"gather_start_{part}_{l}")

    def gather_finish(l, part, started, after):
        sems, shards, lands, _ = started[part]
        shards, lands = _exchange_wait(sems, shards, lands, after, whole, slab, name=f"gather_wait_{part}_{l}")
        token = jnp.zeros((8, LANES), F32)
        if l + 1 < depth:
            started[part] = gather_start(l + 1, part, lands[0])
            token = started[part][3]
        return lands, token

    def bfull(l):
        return jnp.repeat(jnp.transpose(sgu_b[l]), HEAD_DIM, axis=1)

    def bf_pad(l):
        return jnp.pad(b_forget[l], (0, LANES - N_HEADS)).reshape(1, LANES)

    saved = []
    weights = []
    xin = x0
    xn = _prenorm(x0, pre_mix_g[0:1], name="prenorm_first")
    loss_acc = dy = None
    first = _all_gather(shards_of(0, "mix"), name="gather_first")
    started = {"ffn": gather_start(0, "ffn", first[0])}
    if depth > 1:
        started["mix"] = gather_start(1, "mix", started["ffn"][3])
    for l in range(depth):
        if l == 0:
            (g_in, g_a, g_c, g_s, g_o, g_small), token = first, started["mix" if depth > 1 else "ffn"][3]
        else:
            (g_in, g_a, g_c, g_s, g_o, g_small), token = gather_finish(l, "mix", started, xin)
        g_small = _cols_from_slabs(g_small).reshape(3, N_DEV, -1)
        w = dict(w_in=_pad_w_in(g_in.reshape(N_DEV * g_in.shape[1], d), d, token), wa=_cols_from_slabs(g_a),
                 wc=_cols_from_slabs(g_c), ws=_cols_from_slabs(g_s), w_out=g_o.reshape(d, d),
                 b_gate=g_small[:, :, :lo].reshape(3, d), cmw=g_small[:, :, lo:lo + ncm].reshape(3, D_CONV),
                 cfw=g_small[:, :, lo + ncm:].reshape(3, 2 * dff))
        weights.append(w)
        h = _mm(xn, w["w_in"], "nt", F32, name="proj_in")
        c, ct = _forget_prep(h, bf_pad(l), fblk, name="forget_prep")
        att, lse = _attention_fwd(h, c, ct, qblk, name="attention_fwd")
        yc = _sconv_fwd(h, w["cmw"], bgblk, name="sconv_fwd")
        ys = _sgu_fwd(h, sgu_ln_g[l:l + 1], sgu_ln_b[l:l + 1], sgu_w[l], bfull(l), ublk, name="sgu_fwd")
        merged = _merge_fwd(h, w["b_gate"], att, yc, ys, w["wa"], w["wc"], w["ws"], name="merge_fwd")
        o = _mm(merged, w["w_out"], "nn", F32, name="proj_out")
        (g_up, g_dn), token = gather_finish(l, "ffn", started, o)
        w["w_up"], w["w_dn"] = g_up.reshape(2 * dff, d), g_dn.reshape(dff, d)
        x1, xn2 = _postnorm(xin, o, post_mix_g[l:l + 1], pre_ffn_g[l:l + 1] + token[0, 0], name="postnorm_mix")
        hh = _mm(xn2, w["w_up"], "nt", F32, name="ffn_up")
        z = _ffn_act_fwd(hh, w["cfw"], name="ffn_act_fwd")
        f = _mm(z, w["w_dn"], "nn", F32, name="ffn_down")
        saved.append(dict(xin=xin, xn=xn, h=h, c=c, ct=ct, lse=lse, att=att, yc=yc, ys=ys, merged=merged, o=o, x1=x1,
                          xn2=xn2, hh=hh, z=z, f=f))
        if l + 1 < depth:
            xin, xn = _postnorm(x1, f, post_ffn_g[l:l + 1], pre_mix_g[l + 1:l + 2], name="postnorm_ffn")
        else:
            dy, loss_acc = _postnorm_loss(x1, f, post_ffn_g[l:l + 1], target, name="postnorm_loss")
    loss = lax.psum(loss_acc[0, 0] * (0.5 / d), ("x", "y", "c"))

    rep = {k: [None] * depth for k in ("pre_mix_g", "post_mix_g", "pre_ffn_g", "post_ffn_g", "b_forget", "sgu_ln_g",
                                       "sgu_ln_b", "sgu_w", "sgu_b")}
    nsmall = lo + ncm + ncf
    lands = {"mix": [lax.empty((N_DEV, depth) + shp, dt) for shp, dt in (
                 (w_in_t.shape[1:], BF16), (w_branch_att.shape[1:], BF16), (w_branch_conv.shape[1:], BF16),
                 (w_branch_sgu.shape[1:], BF16), (w_out.shape[1:], BF16), ((3, nsmall), F32))],
             "ffn": [lax.empty((N_DEV, depth) + shp, BF16) for shp in (w_up_t.shape[1:], w_ffn_down.shape[1:])]}
    scatters = {"mix": [None] * depth, "ffn": [None] * depth}

    def scatter_start(l, part, sends, after):
        layer_slab = lambda ref, slot: ref.at[slot, l]
        sems, sends, lands[part], token = _exchange_start(sends, lands[part], after, slab, layer_slab,
                                                          name=f"scatter_start_{part}_{l}")
        scatters[part][l] = (sems, sends, layer_slab)
        return token

    def scatter_finish(part, after):
        for l in range(depth):
            sems, sends, layer_slab = scatters[part][l]
            _, lands[part] = _exchange_wait(sems, sends, lands[part], after, slab, layer_slab,
                                            name=f"scatter_wait_{part}_{l}")
        return lands[part]

    token = jnp.zeros((8, LANES), F32)
    dx = dy
    for l in reversed(range(depth)):
        w, a = weights[l], saved[l]
        df, rep["post_ffn_g"][l] = _postnorm_bwd(a["f"], post_ffn_g[l:l + 1] + token[0, 0], dx, name="postnorm_bwd")
        dz = _mm(df, w["w_dn"], "nt", F32, name="ffn_down_dx")
        g_dn = _mm(a["z"], df, "tn", BF16, name="ffn_down_dw")
        dha, dhb, dcwa, dcwb = _ffn_act_bwd(a["hh"], w["cfw"], dz, name="ffn_act_bwd")
        dhh = jnp.concatenate([dha, dhb], axis=1)
        dcfw = jnp.concatenate([dcwa[0:3], dcwb[0:3]], axis=1)
        g_up = _mm(dhh, a["xn2"], "tn", BF16, name="ffn_up_dw")
        token = scatter_start(l, "ffn", [g_up.reshape(N_DEV, 2 * dff // N_DEV, d), g_dn.reshape(N_DEV, dff // N_DEV, d)],
                              dz)
        dxn2 = _mm(dhh, w["w_up"], "nn", F32, name="ffn_up_dx")
        dx1, rep["pre_ffn_g"][l] = _prenorm_bwd(a["x1"], pre_ffn_g[l:l + 1] + token[0, 0], dxn2, dx, name="prenorm_bwd")
        do, rep["post_mix_g"][l] = _postnorm_bwd(a["o"], post_mix_g[l:l + 1], dx1, name="postnorm_bwd")
        dmerged = _mm(do, w["w_out"], "nt", F32, name="proj_out_dx")
        g_o = _mm(a["merged"], do, "tn", BF16, name="proj_out_dw")
        dya, dyc, dys, dgl, dbg = _merge_bwd(a["h"], w["b_gate"], a["att"], a["yc"], a["ys"], w["wa"], w["wc"], w["ws"],
                                             dmerged, name="merge_bwd")
        datt = _mm(dya, w["wa"], "nt", F32, name="branch_att_dx")
        dconv = _mm(dyc, w["wc"], "nt", F32, name="branch_conv_dx")
        dsgu = _mm(dys, w["ws"], "nt", F32, name="branch_sgu_dx")
        g_a = _mm(a["att"], dya, "tn", BF16, name="branch_att_dw")
        g_c = _mm(a["yc"], dyc, "tn", BF16, name="branch_conv_dw")
        g_s = _mm(a["ys"], dys, "tn", BF16, name="branch_sgu_dw")
        dq, dk, dv, dct4, dcq4 = _attention_bwd(a["h"], a["c"], a["ct"], a["lse"], a["att"], datt, qblk,
                                                name="attention_bwd")
        dfl, dbf = _forget_prep_bwd(a["h"], bf_pad(l), dct4, dcq4, fblk, name="forget_prep_bwd")
        rep["b_forget"][l] = dbf[0, :N_HEADS]
        dbgate, dcg, dhc, dcmw = _sconv_bwd(a["h"], w["cmw"], dconv, bgblk, name="sconv_bwd")
        du, dvs, dlg, dlb, dsw, dbfull = _sgu_bwd(a["h"], sgu_ln_g[l:l + 1], sgu_ln_b[l:l + 1], sgu_w[l], bfull(l), dsgu,
                                                  ublk, name="sgu_bwd")
        rep["sgu_ln_g"][l], rep["sgu_ln_b"][l], rep["sgu_w"][l] = dlg, dlb, dsw
        rep["sgu_b"][l] = jnp.transpose(jnp.sum(dbfull.reshape(SGU_CHUNK, N_SGU_GROUPS, HEAD_DIM), axis=2))
        dh = jnp.concatenate([dgl[0], dgl[1], dgl[2], dq.astype(BF16), dk.astype(BF16), dv.astype(BF16), dbgate, dcg, dhc,
                              du, dvs, dfl.astype(BF16), jnp.zeros((s, w["w_in"].shape[0] - off["f"] - LANES), BF16)], axis=1)
        g_in = _mm(dh, a["xn"], "tn", BF16, name="proj_in_dw")
        sends = [_unpad_w_in(g_in, d).reshape(N_DEV, -1, d), _cols_to_slabs(g_a), _cols_to_slabs(g_c), _cols_to_slabs(g_s),
                 g_o.reshape(N_DEV, d // N_DEV, d),
                 jnp.concatenate([_cols_to_slabs(dbg), _cols_to_slabs(dcmw[0:3]), _cols_to_slabs(dcfw)], axis=2)]
        token = scatter_start(l, "mix", sends, dx1)
        dxn = _mm(dh, w["w_in"], "nn", F32, name="proj_in_dx")
        dx, rep["pre_mix_g"][l] = _prenorm_bwd(a["xin"], pre_mix_g[l:l + 1] + token[0, 0], dxn, dx1, name="prenorm_bwd")

    outs = {}
    t3 = lambda arr: jnp.transpose(arr, (0, 2, 1))

    def update(name_, slabs, w_, m_, v_, transposed=False):
        if transposed:
            w_, m_, v_ = t3(w_), t3(m_), t3(v_)
        shp = w_.shape
        w3 = w_.reshape((shp[0], -1, shp[-1])) if w_.ndim >= 3 else w_.reshape((1,) + shp)
        res = _adamw(slabs.reshape((N_DEV,) + w3.shape), w3, m_.reshape(w3.shape), v_.reshape(w3.shape),
                     name="adamw_" + name_)
        outs[name_] = tuple(t3(t.reshape(shp)) if transposed else t.reshape(shp) for t in res)
        return res[0]

    rep_names = ("pre_mix_g", "post_mix_g", "pre_ffn_g", "post_ffn_g", "b_forget", "sgu_ln_g", "sgu_ln_b", "sgu_w", "sgu_b")
    rep_w = dict(pre_mix_g=(pre_mix_g, m_pre_mix_g, v_pre_mix_g), post_mix_g=(post_mix_g, m_post_mix_g, v_post_mix_g),
                 pre_ffn_g=(pre_ffn_g, m_pre_ffn_g, v_pre_ffn_g), post_ffn_g=(post_ffn_g, m_post_ffn_g, v_post_ffn_g),
                 b_forget=(b_forget, m_b_forget, v_b_forget), sgu_ln_g=(sgu_ln_g, m_sgu_ln_g, v_sgu_ln_g),
                 sgu_ln_b=(sgu_ln_b, m_sgu_ln_b, v_sgu_ln_b), sgu_w=(sgu_w, m_sgu_w, v_sgu_w), sgu_b=(sgu_b, m_sgu_b, v_sgu_b))

    def pack(parts):
        rows = [jnp.pad(p.reshape(-1), (0, -p.size % LANES)).reshape(-1, LANES) for p in parts]
        rows = jnp.concatenate(rows, axis=0)
        return jnp.pad(rows, ((0, -rows.shape[0] % PACK_ROWS), (0, 0)))

    part = pack([jnp.stack([g.reshape(rep_w[k][0].shape[1:]) for g in rep[k]]) for k in rep_names])
    small_sems, small_src, small_land, _ = _exchange_start([part], [lax.empty((N_DEV,) + part.shape, F32)], dx, whole, slab,
                                                            name="gather_small_start")

    got_up, got_dn = scatter_finish("ffn", dx)
    update("w_ffn_up", got_up, w_ffn_up, m_w_ffn_up, v_w_ffn_up, transposed=True)
    done = update("w_ffn_down", got_dn, w_ffn_down, m_w_ffn_down, v_w_ffn_down)

    _, (gathered,) = _exchange_wait(small_sems, small_src, small_land, done, whole, slab, name="gather_small_wait")
    packed = [pack([rep_w[k][i] for k in rep_names]) for i in range(3)]
    res = _adamw(gathered.reshape(N_DEV, 1, -1, LANES), *[p.reshape(1, -1, LANES) for p in packed], name="adamw_replicated")
    row = 0
    for k in rep_names:
        shp = rep_w[k][0].shape
        size = math.prod(shp)
        nrows = -(-size // LANES)
        outs[k] = tuple(t[0, row:row + nrows].reshape(-1)[:size].reshape(shp) for t in res)
        row += nrows

    got_in, got_a, got_c, got_s, got_o, small = scatter_finish("mix", res[0])
    update("w_in", got_in, w_in, m_w_in, v_w_in, transposed=True)
    update("w_branch_att", got_a, w_branch_att, m_w_branch_att, v_w_branch_att)
    update("w_branch_conv", got_c, w_branch_conv, m_w_branch_conv, v_w_branch_conv)
    update("w_branch_sgu", got_s, w_branch_sgu, m_w_branch_sgu, v_w_branch_sgu)
    update("w_out", got_o, w_out, m_w_out, v_w_out)
    update("b_gate", small[..., :lo], b_gate, m_b_gate, v_b_gate)
    update("conv_mix_w", small[..., lo:lo + ncm], conv_mix_w, m_conv_mix_w, v_conv_mix_w)
    update("conv_ffn_w", small[..., lo + ncm:], conv_ffn_w, m_conv_ffn_w, v_conv_ffn_w)

    order = ("pre_mix_g", "post_mix_g", "pre_ffn_g", "post_ffn_g", "w_in", "b_forget", "b_gate", "conv_mix_w", "sgu_ln_g",
             "sgu_ln_b", "sgu_w", "sgu_b", "w_branch_att", "w_branch_conv", "w_branch_sgu", "w_out", "w_ffn_up",
             "conv_ffn_w", "w_ffn_down")
    grad_x = dx.reshape(x.shape)
    return (loss, grad_x, *[outs[k][0] for k in order], *[outs[k][1] for k in order], *[outs[k][2] for k in order],
            *[outs[k][3] for k in order])
```

```python
import functools
import math

import jax
import jax.numpy as jnp
from jax import lax
from jax.experimental import pallas as pl
from jax.experimental.pallas import tpu as pltpu

F32 = jnp.float32
BF16 = jnp.bfloat16

N_DEV = 8
HEAD_DIM = 64
N_HEADS = 8
D_ATT = 512
D_CONV = 256
D_SGU = 256
N_SGU_GROUPS = 4
SGU_CHUNK = 128
RMS_EPS = 1e-6
LN_EPS = 1e-5
ADAM_LR = 0.001
ADAM_B1 = 0.9
ADAM_B2 = 0.999
ADAM_EPS = 1e-08
ADAM_WD = 0.01
ADAM_STEP = 10
LANES = 128
VMEM_LIMIT = 56 * 1024 * 1024
ATT_TILE = 512
NEG = -1e30
MESH = pl.DeviceIdType.MESH


def _params(*sem):
    return pltpu.CompilerParams(dimension_semantics=sem if sem else None, vmem_limit_bytes=VMEM_LIMIT)


def _tile(n, cap):
    if n <= cap:
        return n
    t = cap - cap % LANES
    while n % t:
        t -= LANES
    return t


def _gelu(x):
    return 0.5 * x * (1.0 + jnp.tanh(math.sqrt(2.0 / math.pi) * (x + 0.044715 * (x * x * x))))


def _gelu_and_slope(x):
    k0, k1 = math.sqrt(2.0 / math.pi), 0.044715
    x2 = x * x
    t = jnp.tanh(x * (k0 + (k0 * k1) * x2))
    half = 0.5 * (1.0 + t)
    return x * half, half + (0.5 * x) * (1.0 - t * t) * (k0 + (3.0 * k0 * k1) * x2)


def _rms(x, g):
    r = lax.rsqrt(jnp.mean(x * x, axis=-1, keepdims=True) + RMS_EPS)
    return x * r * g


def _layer_norm(x, g, b):
    mu = jnp.mean(x, axis=-1, keepdims=True)
    xc = x - mu
    var = jnp.mean(xc * xc, axis=-1, keepdims=True)
    return xc * lax.rsqrt(var + LN_EPS) * g + b


def _shift_down(x, k, rows):
    return jnp.where(rows >= k, pltpu.roll(x, k, 0), 0.0)


def _shift_up(x, k, rows):
    s = x.shape[0]
    return jnp.where(rows < s - k, pltpu.roll(x, s - k, 0), 0.0)


def _conv3(x, w_ref, rows):
    return w_ref[2:3, :] * x + w_ref[1:2, :] * _shift_down(x, 1, rows) + w_ref[0:1, :] * _shift_down(x, 2, rows)


def _conv3_t(dy, w_ref, rows):
    return w_ref[2:3, :] * dy + w_ref[1:2, :] * _shift_up(dy, 1, rows) + w_ref[0:1, :] * _shift_up(dy, 2, rows)


def _conv3_dw(dy, x, rows):
    d2 = jnp.sum(dy * x, axis=0, keepdims=True)
    d1 = jnp.sum(dy * _shift_down(x, 1, rows), axis=0, keepdims=True)
    d0 = jnp.sum(dy * _shift_down(x, 2, rows), axis=0, keepdims=True)
    sub = lax.broadcasted_iota(jnp.int32, (8, x.shape[1]), 0)
    return jnp.where(sub == 0, d0, jnp.where(sub == 1, d1, jnp.where(sub == 2, d2, 0.0)))


MM_VMEM_BUDGET = 40 * 1024 * 1024
MM_TILE_CAP = 1408


def _mm_tiles(m, n, k, out_bytes):
    def edges(d):
        return [t for t in range(LANES, min(d, MM_TILE_CAP) + 1, LANES) if d % t == 0] or [d]

    best = None
    for tm in edges(m):
        for tn in edges(n):
            if 2 * (2 * k * (tm + tn) + tm * tn * out_bytes) > MM_VMEM_BUDGET:
                continue
            for a_outer in (True, False):
                reads = k * m + (m // tm) * k * n if a_outer else k * n + (n // tn) * k * m
                traffic = 2 * reads + m * n * out_bytes
                key = (traffic, -tm * tn)
                if best is None or key < best[0]:
                    best = (key, (tm, tn, a_outer))
    return best[1]


def _mm(a, b, form, out_dtype, name, after=None):
    if form == "nn":
        (m, k), n = a.shape, b.shape[1]
    elif form == "nt":
        (m, k), n = a.shape, b.shape[0]
    else:
        (k, m), n = a.shape, b.shape[1]
    tm, tn, a_outer = _mm_tiles(m, n, k, jnp.dtype(out_dtype).itemsize)
    dims = {"nn": (((1,), (0,)), ((), ())), "nt": (((1,), (1,)), ((), ())), "tn": (((0,), (0,)), ((), ()))}[form]

    def body(a_ref, b_ref, *rest):
        o_ref = rest[-1]
        o_ref[...] = lax.dot_general(a_ref[...], b_ref[...], dims, preferred_element_type=F32).astype(o_ref.dtype)

    ij = (lambda g0, g1: (g0, g1)) if a_outer else (lambda g0, g1: (g1, g0))
    a_spec = (pl.BlockSpec((k, tm), lambda g0, g1: (0, ij(g0, g1)[0])) if form == "tn"
              else pl.BlockSpec((tm, k), lambda g0, g1: (ij(g0, g1)[0], 0)))
    b_spec = (pl.BlockSpec((tn, k), lambda g0, g1: (ij(g0, g1)[1], 0)) if form == "nt"
              else pl.BlockSpec((k, tn), lambda g0, g1: (0, ij(g0, g1)[1])))
    extra = [] if after is None else [pl.BlockSpec((8, LANES), lambda g0, g1: (0, 0))]
    return pl.pallas_call(
        body, name=name, grid=(m // tm, n // tn) if a_outer else (n // tn, m // tm),
        in_specs=[a_spec, b_spec] + extra, out_specs=pl.BlockSpec((tm, tn), lambda g0, g1: ij(g0, g1)),
        out_shape=jax.ShapeDtypeStruct((m, n), out_dtype),
        compiler_params=_params("parallel", "arbitrary"),
    )(a, b, *([] if after is None else [after]))


def _prenorm(x, g, name):
    s, d = x.shape
    tm = _tile(s, 512)

    def body(x_ref, g_ref, o_ref):
        o_ref[...] = _rms(x_ref[...], g_ref[...]).astype(BF16)

    return pl.pallas_call(
        body, name=name, grid=(s // tm,),
        in_specs=[pl.BlockSpec((tm, d), lambda i: (i, 0)), pl.BlockSpec((1, d), lambda i: (0, 0))],
        out_specs=pl.BlockSpec((tm, d), lambda i: (i, 0)),
        out_shape=jax.ShapeDtypeStruct((s, d), BF16), compiler_params=_params("parallel"),
    )(x, g)


def _postnorm(x, o, g_post, g_next, name):
    s, d = x.shape
    tm = _tile(s, 512)

    def body(x_ref, o_ref, gp_ref, gn_ref, x1_ref, xn_ref):
        x1 = x_ref[...] + _rms(o_ref[...], gp_ref[...])
        x1_ref[...] = x1
        xn_ref[...] = _rms(x1, gn_ref[...]).astype(BF16)

    row = pl.BlockSpec((tm, d), lambda i: (i, 0))
    vec = pl.BlockSpec((1, d), lambda i: (0, 0))
    return pl.pallas_call(
        body, name=name, grid=(s // tm,), in_specs=[row, row, vec, vec], out_specs=[row, row],
        out_shape=[jax.ShapeDtypeStruct((s, d), F32), jax.ShapeDtypeStruct((s, d), BF16)],
        compiler_params=_params("parallel"),
    )(x, o, g_post, g_next)


def _postnorm_loss(x, o, g_post, target, name):
    s, d = x.shape
    tm = _tile(s, 512)

    def body(x_ref, o_ref, gp_ref, t_ref, dy_ref, acc_ref):
        e = x_ref[...] + _rms(o_ref[...], gp_ref[...]) - t_ref[...]
        dy_ref[...] = e / d

        @pl.when(pl.program_id(0) == 0)
        def _():
            acc_ref[...] = jnp.zeros_like(acc_ref)

        acc_ref[...] += jnp.sum(jnp.sum(e * e, axis=1, keepdims=True), axis=0, keepdims=True)

    row = pl.BlockSpec((tm, d), lambda i: (i, 0))
    return pl.pallas_call(
        body, name=name, grid=(s // tm,),
        in_specs=[row, row, pl.BlockSpec((1, d), lambda i: (0, 0)), row],
        out_specs=[row, pl.BlockSpec((1, LANES), lambda i: (0, 0))],
        out_shape=[jax.ShapeDtypeStruct((s, d), F32), jax.ShapeDtypeStruct((1, LANES), F32)],
        compiler_params=_params("arbitrary"),
    )(x, o, g_post, target)


def _postnorm_bwd(o, g, dx, name):
    s, d = o.shape
    tm = _tile(s, 512)

    def body(o_ref, g_ref, dx_ref, do_ref, dg_ref):
        _, vjp = jax.vjp(_rms, o_ref[...], g_ref[...])
        d_o, dg = vjp(dx_ref[...])
        do_ref[...] = d_o.astype(BF16)

        @pl.when(pl.program_id(0) == 0)
        def _():
            dg_ref[...] = jnp.zeros_like(dg_ref)

        dg_ref[...] += dg

    row = pl.BlockSpec((tm, d), lambda i: (i, 0))
    vec = pl.BlockSpec((1, d), lambda i: (0, 0))
    return pl.pallas_call(
        body, name=name, grid=(s // tm,), in_specs=[row, vec, row], out_specs=[row, vec],
        out_shape=[jax.ShapeDtypeStruct((s, d), BF16), jax.ShapeDtypeStruct((1, d), F32)],
        compiler_params=_params("arbitrary"),
    )(o, g, dx)


def _prenorm_bwd(x, g, dxn, dres, name):
    s, d = x.shape
    tm = _tile(s, 512)

    def body(x_ref, g_ref, dxn_ref, dres_ref, dx_ref, dg_ref):
        _, vjp = jax.vjp(_rms, x_ref[...], g_ref[...])
        dx, dg = vjp(dxn_ref[...])
        dx_ref[...] = dres_ref[...] + dx

        @pl.when(pl.program_id(0) == 0)
        def _():
            dg_ref[...] = jnp.zeros_like(dg_ref)

        dg_ref[...] += dg

    row = pl.BlockSpec((tm, d), lambda i: (i, 0))
    vec = pl.BlockSpec((1, d), lambda i: (0, 0))
    return pl.pallas_call(
        body, name=name, grid=(s // tm,), in_specs=[row, vec, row, row], out_specs=[row, vec],
        out_shape=[jax.ShapeDtypeStruct((s, d), F32), jax.ShapeDtypeStruct((1, d), F32)],
        compiler_params=_params("arbitrary"),
    )(x, g, dxn, dres)


def _log_sigmoid(z):
    return jnp.minimum(z, 0.0) - jnp.log(1.0 + jnp.exp(-jnp.abs(z)))


def _forget_prep(h, bf_pad, fblk, name):
    s = h.shape[0]

    def body(f_ref, b_ref, c_ref, ct_ref):
        c = _log_sigmoid(f_ref[...] + b_ref[...])
        rows = lax.broadcasted_iota(jnp.int32, c.shape, 0)
        k = 1
        while k < s:
            c = c + _shift_down(c, k, rows)
            k *= 2
        c_ref[...] = c
        ct_ref[...] = jnp.transpose(c)[0:8, :]

    return pl.pallas_call(
        body, name=name, grid=(1,),
        in_specs=[pl.BlockSpec((s, LANES), lambda i: (0, fblk)), pl.BlockSpec((1, LANES), lambda i: (0, 0))],
        out_specs=[pl.BlockSpec((s, LANES), lambda i: (0, 0)), pl.BlockSpec((8, s), lambda i: (0, 0))],
        out_shape=[jax.ShapeDtypeStruct((s, LANES), F32), jax.ShapeDtypeStruct((8, s), F32)],
        compiler_params=_params("arbitrary"),
    )(h, bf_pad)


def _forget_prep_bwd(h, bf_pad, dct, dcq, fblk, name):
    s = h.shape[0]
    pairs = N_HEADS // 2

    def body(f_ref, b_ref, dct_ref, dcq_ref, df_ref, db_ref):
        dct = dct_ref[0]
        dcq = dcq_ref[0]
        for p in range(1, pairs):
            dct = dct + pltpu.roll(dct_ref[p], 2 * p, 0)
            dcq = dcq + pltpu.roll(dcq_ref[p], 2 * p, 1)
        dc = dcq + jnp.transpose(jnp.concatenate([dct, jnp.zeros((LANES - 8, s), F32)], axis=0))
        rows = lax.broadcasted_iota(jnp.int32, dc.shape, 0)
        k = 1
        while k < s:
            dc = dc + _shift_up(dc, k, rows)
            k *= 2
        z = f_ref[...] + b_ref[...]
        lane = lax.broadcasted_iota(jnp.int32, dc.shape, 1)
        df = jnp.where(lane < N_HEADS, dc * jax.nn.sigmoid(-z), 0.0)
        df_ref[...] = df
        db_ref[...] = jnp.sum(df, axis=0, keepdims=True)

    return pl.pallas_call(
        body, name=name, grid=(1,),
        in_specs=[pl.BlockSpec((s, LANES), lambda i: (0, fblk)), pl.BlockSpec((1, LANES), lambda i: (0, 0)),
                  pl.BlockSpec((pairs, 8, s), lambda i: (0, 0, 0)), pl.BlockSpec((pairs, s, LANES), lambda i: (0, 0, 0))],
        out_specs=[pl.BlockSpec((s, LANES), lambda i: (0, 0)), pl.BlockSpec((1, LANES), lambda i: (0, 0))],
        out_shape=[jax.ShapeDtypeStruct((s, LANES), F32), jax.ShapeDtypeStruct((1, LANES), F32)],
        compiler_params=_params("arbitrary"),
    )(h, bf_pad, dct, dcq)


def _pick_lane(blk, idx):
    lane = lax.broadcasted_iota(jnp.int32, blk.shape, 1)
    return jnp.sum(jnp.where(lane == idx, blk, 0.0), axis=1, keepdims=True)


def _pick_row(blk, idx):
    sub = lax.broadcasted_iota(jnp.int32, blk.shape, 0)
    return jnp.sum(jnp.where(sub == idx, blk, 0.0), axis=0, keepdims=True)


def _attention_fwd(h, c, ct, qblk, name):
    s = h.shape[0]
    t = _tile(s, ATT_TILE)
    nq = s // t
    scale = HEAD_DIM ** -0.5
    nt_dims = (((1,), (1,)), ((), ()))

    def body(q_ref, k_ref, v_ref, c_ref, ct_ref, o_ref, lse_ref):
        p = pl.program_id(0)
        i = pl.program_id(1)
        lane = lax.broadcasted_iota(jnp.int32, (1, LANES), 1)
        first = lane < HEAD_DIM
        q = q_ref[...] * scale
        qa = jnp.where(first, q, 0.0).astype(BF16)
        qb = jnp.where(first, 0.0, q).astype(BF16)
        cblk = c_ref[...]
        cta = _pick_lane(cblk, 2 * p)
        ctb = _pick_lane(cblk, 2 * p + 1)

        def step(j, carry, diagonal):
            ma, la, mb, lb, acc = carry
            off = pl.multiple_of(j * t, t)
            k = k_ref[pl.ds(off, t), :].astype(BF16)
            v = v_ref[pl.ds(off, t), :].astype(BF16)
            crow = ct_ref[:, pl.ds(off, t)]

            def one(qh, cth, hd, m_old, l_old):
                sc = lax.dot_general(qh, k, nt_dims, preferred_element_type=F32) - _pick_row(crow, hd)
                if diagonal:
                    keep = lax.broadcasted_iota(jnp.int32, (t, t), 0) >= lax.broadcasted_iota(jnp.int32, (t, t), 1)
                    sc = jnp.where(keep, sc, NEG)
                m_new = jnp.maximum(m_old, jnp.max(sc, axis=1, keepdims=True) + cth)
                pr = jnp.exp(sc - (m_new - cth))
                alpha = jnp.exp(m_old - m_new)
                l_new = alpha * l_old + jnp.sum(pr, axis=1, keepdims=True)
                pv = jnp.dot(pr.astype(BF16), v, preferred_element_type=F32)
                return m_new, l_new, alpha, pv

            ma2, la2, aa, pva = one(qa, cta, 2 * p, ma, la)
            mb2, lb2, ab, pvb = one(qb, ctb, 2 * p + 1, mb, lb)
            acc = jnp.where(first, aa * acc + pva, ab * acc + pvb)
            return ma2, la2, mb2, lb2, acc

        init = (jnp.full((t, 1), NEG, F32), jnp.zeros((t, 1), F32), jnp.full((t, 1), NEG, F32),
                jnp.zeros((t, 1), F32), jnp.zeros((t, LANES), F32))
        carry = lax.fori_loop(0, i, lambda j, carry: step(j, carry, False), init)
        ma, la, mb, lb, acc = step(i, carry, True)
        o_ref[...] = (acc / jnp.where(first, la, lb)).astype(BF16)
        lse_ref[0] = jnp.broadcast_to(ma + jnp.log(la), (t, LANES))
        lse_ref[1] = jnp.broadcast_to(mb + jnp.log(lb), (t, LANES))

    return pl.pallas_call(
        body, name=name, grid=(N_HEADS // 2, nq),
        in_specs=[pl.BlockSpec((t, LANES), lambda p, i: (i, qblk + p)),
                  pl.BlockSpec((s, LANES), lambda p, i: (0, qblk + 4 + p)),
                  pl.BlockSpec((s, LANES), lambda p, i: (0, qblk + 8 + p)),
                  pl.BlockSpec((t, LANES), lambda p, i: (i, 0)),
                  pl.BlockSpec((8, s), lambda p, i: (0, 0))],
        out_specs=[pl.BlockSpec((t, LANES), lambda p, i: (i, p)),
                   pl.BlockSpec((2, t, LANES), lambda p, i: (p, i, 0))],
        out_shape=[jax.ShapeDtypeStruct((s, D_ATT), BF16), jax.ShapeDtypeStruct((N_HEADS, s, LANES), F32)],
        compiler_params=_params("parallel", "arbitrary"),
    )(h, h, h, c, ct)


def _attention_bwd(h, c, ct, lse, att, datt, qblk, name):
    s = h.shape[0]
    t = _tile(s, ATT_TILE)
    nq = s // t
    scale = HEAD_DIM ** -0.5
    nt_dims = (((1,), (1,)), ((), ()))
    tn_dims = (((0,), (0,)), ((), ()))

    def body(q_ref, k_ref, v_ref, c_ref, ct_ref, lse_ref, o_ref, do_ref, dq_ref, dk_ref, dv_ref, dct_ref, dcq_ref):
        p = pl.program_id(0)
        j = pl.program_id(1)
        lane = lax.broadcasted_iota(jnp.int32, (1, LANES), 1)
        first = lane < HEAD_DIM
        kf = k_ref[...]
        vf = v_ref[...]
        k = kf.astype(BF16)
        ka = jnp.where(first, kf, 0.0).astype(BF16)
        kb = jnp.where(first, 0.0, kf).astype(BF16)
        va = jnp.where(first, vf, 0.0).astype(BF16)
        vb = jnp.where(first, 0.0, vf).astype(BF16)
        crow = ct_ref[...]
        csa = _pick_row(crow, 2 * p)
        csb = _pick_row(crow, 2 * p + 1)

        @pl.when(j == 0)
        def _():
            dq_ref[...] = jnp.zeros_like(dq_ref)
            dcq_ref[...] = jnp.zeros_like(dcq_ref)

        def step(i, carry, diagonal):
            dka, dkb, dva, dvb, dca, dcb = carry
            off = pl.multiple_of(i * t, t)
            rows = pl.ds(off, t)
            q = (q_ref[rows, :] * scale).astype(BF16)
            dof = do_ref[rows, :]
            do = dof.astype(BF16)
            prod = dof * o_ref[rows, :].astype(F32)
            cblk = c_ref[rows, :]

            def one(kh, vh, hd, csh, lse_h):
                sc = lax.dot_general(q, kh, nt_dims, preferred_element_type=F32) - csh
                if diagonal:
                    keep = lax.broadcasted_iota(jnp.int32, (t, t), 0) >= lax.broadcasted_iota(jnp.int32, (t, t), 1)
                    sc = jnp.where(keep, sc, NEG)
                pr = jnp.exp(sc - (jnp.max(lse_h, axis=1, keepdims=True) - _pick_lane(cblk, hd)))
                dp = lax.dot_general(do, vh, nt_dims, preferred_element_type=F32)
                return pr, dp

            pra, dpa = one(ka, va, 2 * p, csa, lse_ref[0, rows, :])
            prb, dpb = one(kb, vb, 2 * p + 1, csb, lse_ref[1, rows, :])
            dela = jnp.sum(jnp.where(first, prod, 0.0), axis=1, keepdims=True)
            delb = jnp.sum(jnp.where(first, 0.0, prod), axis=1, keepdims=True)
            dsa = pra * (dpa - dela)
            dsb = prb * (dpb - delb)
            dsa16 = dsa.astype(BF16)
            dsb16 = dsb.astype(BF16)
            dva = dva + lax.dot_general(pra.astype(BF16), do, tn_dims, preferred_element_type=F32)
            dvb = dvb + lax.dot_general(prb.astype(BF16), do, tn_dims, preferred_element_type=F32)
            dka = dka + lax.dot_general(dsa16, q, tn_dims, preferred_element_type=F32)
            dkb = dkb + lax.dot_general(dsb16, q, tn_dims, preferred_element_type=F32)
            dqa = jnp.dot(dsa16, k, preferred_element_type=F32)
            dqb = jnp.dot(dsb16, k, preferred_element_type=F32)
            dq_ref[rows, :] += scale * jnp.where(first, dqa, dqb)
            dca = dca - jnp.sum(dsa, axis=0, keepdims=True)
            dcb = dcb - jnp.sum(dsb, axis=0, keepdims=True)
            dcq_ref[rows, :] += jnp.where(lane == 0, jnp.sum(dsa, axis=1, keepdims=True),
                                          jnp.where(lane == 1, jnp.sum(dsb, axis=1, keepdims=True), 0.0))
            return dka, dkb, dva, dvb, dca, dcb

        z = jnp.zeros((t, LANES), F32)
        zr = jnp.zeros((1, t), F32)
        carry = step(j, (z, z, z, z, zr, zr), True)
        dka, dkb, dva, dvb, dca, dcb = lax.fori_loop(j + 1, nq, lambda i, carry: step(i, carry, False), carry)
        dk_ref[...] = jnp.where(first, dka, dkb)
        dv_ref[...] = jnp.where(first, dva, dvb)
        sub = lax.broadcasted_iota(jnp.int32, (8, t), 0)
        dct_ref[...] = jnp.where(sub == 0, dca, jnp.where(sub == 1, dcb, 0.0))

    full = lambda blk: pl.BlockSpec((s, LANES), blk)
    return pl.pallas_call(
        body, name=name, grid=(N_HEADS // 2, nq),
        in_specs=[full(lambda p, j: (0, qblk + p)),
                  pl.BlockSpec((t, LANES), lambda p, j: (j, qblk + 4 + p)),
                  pl.BlockSpec((t, LANES), lambda p, j: (j, qblk + 8 + p)),
                  full(lambda p, j: (0, 0)),
                  pl.BlockSpec((8, t), lambda p, j: (0, j)),
                  pl.BlockSpec((2, s, LANES), lambda p, j: (p, 0, 0)),
                  full(lambda p, j: (0, p)),
                  full(lambda p, j: (0, p))],
        out_specs=[full(lambda p, j: (0, p)),
                   pl.BlockSpec((t, LANES), lambda p, j: (j, p)),
                   pl.BlockSpec((t, LANES), lambda p, j: (j, p)),
                   pl.BlockSpec((None, 8, t), lambda p, j: (p, 0, j)),
                   pl.BlockSpec((None, s, LANES), lambda p, j: (p, 0, 0))],
        out_shape=[jax.ShapeDtypeStruct((s, D_ATT), F32), jax.ShapeDtypeStruct((s, D_ATT), F32),
                   jax.ShapeDtypeStruct((s, D_ATT), F32), jax.ShapeDtypeStruct((N_HEADS // 2, 8, s), F32),
                   jax.ShapeDtypeStruct((N_HEADS // 2, s, LANES), F32)],
        compiler_params=_params("arbitrary", "arbitrary"),
    )(h, h, h, c, ct, lse, att, datt)


def _sconv_fwd(h, w, bgblk, name):
    s = h.shape[0]
    nblk = D_CONV // LANES

    def body(bg_ref, cg_ref, hc_ref, w_ref, y_ref):
        rows = lax.broadcasted_iota(jnp.int32, (s, LANES), 0)
        y_ref[...] = (bg_ref[...] * _conv3(cg_ref[...] * hc_ref[...], w_ref, rows)).astype(BF16)

    col = lambda base: pl.BlockSpec((s, LANES), lambda j: (0, base + j))
    return pl.pallas_call(
        body, name=name, grid=(nblk,),
        in_specs=[col(bgblk), col(bgblk + nblk), col(bgblk + 2 * nblk), pl.BlockSpec((3, LANES), lambda j: (0, j))],
        out_specs=pl.BlockSpec((s, LANES), lambda j: (0, j)),
        out_shape=jax.ShapeDtypeStruct((s, D_CONV), BF16), compiler_params=_params("parallel"),
    )(h, h, h, w)


def _sconv_bwd(h, w, dy, bgblk, name):
    s = h.shape[0]
    nblk = D_CONV // LANES

    def body(bg_ref, cg_ref, hc_ref, w_ref, dy_ref, dbg_ref, dcg_ref, dhc_ref, dw_ref):
        rows = lax.broadcasted_iota(jnp.int32, (s, LANES), 0)
        cg, hc, dy, w = cg_ref[...], hc_ref[...], dy_ref[...], w_ref
        xin = cg * hc
        dbg_ref[...] = (dy * _conv3(xin, w, rows)).astype(BF16)
        dconv = dy * bg_ref[...]
        dxin = _conv3_t(dconv, w, rows)
        dcg_ref[...] = (dxin * hc).astype(BF16)
        dhc_ref[...] = (dxin * cg).astype(BF16)
        dw_ref[...] = _conv3_dw(dconv, xin, rows)

    col = lambda base: pl.BlockSpec((s, LANES), lambda j: (0, base + j))
    return pl.pallas_call(
        body, name=name, grid=(nblk,),
        in_specs=[col(bgblk), col(bgblk + nblk), col(bgblk + 2 * nblk), pl.BlockSpec((3, LANES), lambda j: (0, j)), col(0)],
        out_specs=[col(0), col(0), col(0), pl.BlockSpec((8, LANES), lambda j: (0, j))],
        out_shape=[jax.ShapeDtypeStruct((s, D_CONV), BF16)] * 3 + [jax.ShapeDtypeStruct((8, D_CONV), F32)],
        compiler_params=_params("parallel"),
    )(h, h, h, w, dy)


def _sgu_group_masks():
    lane = lax.broadcasted_iota(jnp.int32, (1, D_SGU), 1)
    return [(lane // HEAD_DIM) == g for g in range(N_SGU_GROUPS)]


def _sgu_tril():
    r = lax.broadcasted_iota(jnp.int32, (SGU_CHUNK, SGU_CHUNK), 0)
    c = lax.broadcasted_iota(jnp.int32, (SGU_CHUNK, SGU_CHUNK), 1)
    return r >= c


def _sgu_fwd(h, ln_g, ln_b, w_s, b_full, ublk, name):
    s = h.shape[0]
    tr = _tile(s, 512)
    nch = tr // SGU_CHUNK

    def body(u_ref, v_ref, g_ref, b_ref, w_ref, bf_ref, y_ref):
        masks = _sgu_group_masks()
        tril = _sgu_tril()
        wm = [jnp.where(tril, w_ref[g], 0.0).astype(BF16) for g in range(N_SGU_GROUPS)]
        vn = _layer_norm(_gelu(v_ref[...]), g_ref[...], b_ref[...])
        for ch in range(nch):
            rows = pl.ds(ch * SGU_CHUNK, SGU_CHUNK)
            vc = vn[ch * SGU_CHUNK:(ch + 1) * SGU_CHUNK, :]
            mixed = bf_ref[...]
            for g in range(N_SGU_GROUPS):
                mixed = mixed + jnp.dot(wm[g], jnp.where(masks[g], vc, 0.0).astype(BF16), preferred_element_type=F32)
            y_ref[rows, :] = (_gelu(u_ref[rows, :]) * mixed).astype(BF16)

    row = lambda blk: pl.BlockSpec((tr, D_SGU), lambda i: (i, blk))
    vec = pl.BlockSpec((1, D_SGU), lambda i: (0, 0))
    return pl.pallas_call(
        body, name=name, grid=(s // tr,),
        in_specs=[row(ublk), row(ublk + 1), vec, vec,
                  pl.BlockSpec((N_SGU_GROUPS, SGU_CHUNK, SGU_CHUNK), lambda i: (0, 0, 0)),
                  pl.BlockSpec((SGU_CHUNK, D_SGU), lambda i: (0, 0))],
        out_specs=row(0), out_shape=jax.ShapeDtypeStruct((s, D_SGU), BF16), compiler_params=_params("parallel"),
    )(h, h, ln_g, ln_b, w_s, b_full)


def _sgu_bwd(h, ln_g, ln_b, w_s, b_full, dy, ublk, name):
    s = h.shape[0]
    tr = _tile(s, 512)
    nch = tr // SGU_CHUNK
    nt_dims = (((1,), (1,)), ((), ()))

    def norm(v, g, b):
        return _layer_norm(_gelu(v), g, b)

    def body(u_ref, v_ref, g_ref, b_ref, w_ref, bf_ref, dy_ref, du_ref, dv_ref, dg_ref, db_ref, dw_ref, dbf_ref):
        masks = _sgu_group_masks()
        tril = _sgu_tril()
        wf = [jnp.where(tril, w_ref[g], 0.0) for g in range(N_SGU_GROUPS)]
        wm = [w.astype(BF16) for w in wf]
        wmt = [jnp.transpose(w).astype(BF16) for w in wf]
        vn, vjp = jax.vjp(norm, v_ref[...], g_ref[...], b_ref[...])

        @pl.when(pl.program_id(0) == 0)
        def _():
            dg_ref[...] = jnp.zeros_like(dg_ref)
            db_ref[...] = jnp.zeros_like(db_ref)
            dw_ref[...] = jnp.zeros_like(dw_ref)
            dbf_ref[...] = jnp.zeros_like(dbf_ref)

        dvn_parts = []
        for ch in range(nch):
            rows = pl.ds(ch * SGU_CHUNK, SGU_CHUNK)
            vc = vn[ch * SGU_CHUNK:(ch + 1) * SGU_CHUNK, :]
            vc16 = vc.astype(BF16)
            mixed = bf_ref[...]
            for g in range(N_SGU_GROUPS):
                mixed = mixed + jnp.dot(wm[g], jnp.where(masks[g], vc, 0.0).astype(BF16), preferred_element_type=F32)
            dy = dy_ref[rows, :]
            ug, slope = _gelu_and_slope(u_ref[rows, :])
            du_ref[rows, :] = (dy * mixed * slope).astype(BF16)
            dmixed = dy * ug
            dbf_ref[...] += dmixed
            dvc = jnp.zeros((SGU_CHUNK, D_SGU), F32)
            for g in range(N_SGU_GROUPS):
                dm16 = jnp.where(masks[g], dmixed, 0.0).astype(BF16)
                dw_ref[g] += jnp.where(tril, lax.dot_general(dm16, vc16, nt_dims, preferred_element_type=F32), 0.0)
                dvc = dvc + jnp.dot(wmt[g], dm16, preferred_element_type=F32)
            dvn_parts.append(dvc)
        dv, dg, db = vjp(jnp.concatenate(dvn_parts, axis=0))
        dv_ref[...] = dv.astype(BF16)
        dg_ref[...] += dg
        db_ref[...] += db

    row = lambda blk: pl.BlockSpec((tr, D_SGU), lambda i: (i, blk))
    vec = pl.BlockSpec((1, D_SGU), lambda i: (0, 0))
    wsp = pl.BlockSpec((N_SGU_GROUPS, SGU_CHUNK, SGU_CHUNK), lambda i: (0, 0, 0))
    bsp = pl.BlockSpec((SGU_CHUNK, D_SGU), lambda i: (0, 0))
    return pl.pallas_call(
        body, name=name, grid=(s // tr,),
        in_specs=[row(ublk), row(ublk + 1), vec, vec, wsp, bsp, row(0)],
        out_specs=[row(0), row(0), vec, vec, wsp, bsp],
        out_shape=[jax.ShapeDtypeStruct((s, D_SGU), BF16), jax.ShapeDtypeStruct((s, D_SGU), BF16),
                   jax.ShapeDtypeStruct((1, D_SGU), F32), jax.ShapeDtypeStruct((1, D_SGU), F32),
                   jax.ShapeDtypeStruct((N_SGU_GROUPS, SGU_CHUNK, SGU_CHUNK), F32),
                   jax.ShapeDtypeStruct((SGU_CHUNK, D_SGU), F32)],
        compiler_params=_params("arbitrary"),
    )(h, h, ln_g, ln_b, w_s, b_full, dy)


def _merge_fwd(h, b_gate, att, yc, ys, wa, wc, ws, name):
    s, d = att.shape[0], wa.shape[1]
    tm, tn = _tile(s, 512), _tile(d, 512)
    nj = d // tn

    def body(g0_ref, g1_ref, g2_ref, bg_ref, a_ref, c_ref, s_ref, wa_ref, wc_ref, ws_ref, o_ref):
        acc = jax.nn.sigmoid(g0_ref[...] + bg_ref[0:1, :]) * jnp.dot(a_ref[...], wa_ref[...], preferred_element_type=F32)
        acc += jax.nn.sigmoid(g1_ref[...] + bg_ref[1:2, :]) * jnp.dot(c_ref[...], wc_ref[...], preferred_element_type=F32)
        acc += jax.nn.sigmoid(g2_ref[...] + bg_ref[2:3, :]) * jnp.dot(s_ref[...], ws_ref[...], preferred_element_type=F32)
        o_ref[...] = acc.astype(BF16)

    gate = lambda b: pl.BlockSpec((tm, tn), lambda j, i: (i, b * nj + j))
    act = lambda k: pl.BlockSpec((tm, k), lambda j, i: (i, 0))
    wgt = lambda k: pl.BlockSpec((k, tn), lambda j, i: (0, j))
    return pl.pallas_call(
        body, name=name, grid=(nj, s // tm),
        in_specs=[gate(0), gate(1), gate(2), pl.BlockSpec((3, tn), lambda j, i: (0, j)),
                  act(D_ATT), act(D_CONV), act(D_SGU), wgt(D_ATT), wgt(D_CONV), wgt(D_SGU)],
        out_specs=pl.BlockSpec((tm, tn), lambda j, i: (i, j)),
        out_shape=jax.ShapeDtypeStruct((s, d), BF16), compiler_params=_params("parallel", "arbitrary"),
    )(h, h, h, b_gate, att, yc, ys, wa, wc, ws)


def _merge_bwd(h, b_gate, att, yc, ys, wa, wc, ws, dm, name):
    s, d = att.shape[0], wa.shape[1]
    tm, tn = _tile(s, 512), _tile(d, 512)
    nj = d // tn

    def body(g0_ref, g1_ref, g2_ref, bg_ref, a_ref, c_ref, s_ref, wa_ref, wc_ref, ws_ref, dm_ref,
             dya_ref, dyc_ref, dys_ref, dgl_ref, dbg_ref):
        dm = dm_ref[...]
        sums = []
        for b, (g_ref, x_ref, w_ref, dy_ref) in enumerate(((g0_ref, a_ref, wa_ref, dya_ref), (g1_ref, c_ref, wc_ref, dyc_ref),
                                                          (g2_ref, s_ref, ws_ref, dys_ref))):
            gate = jax.nn.sigmoid(g_ref[...] + bg_ref[b:b + 1, :])
            y = jnp.dot(x_ref[...], w_ref[...], preferred_element_type=F32)
            dy_ref[...] = (dm * gate).astype(BF16)
            dgl = dm * y * gate * (1.0 - gate)
            dgl_ref[b] = dgl.astype(BF16)
            sums.append(jnp.sum(dgl, axis=0, keepdims=True))
        sub = lax.broadcasted_iota(jnp.int32, (3, tn), 0)
        part = jnp.where(sub == 0, sums[0], jnp.where(sub == 1, sums[1], sums[2]))

        @pl.when(pl.program_id(1) == 0)
        def _():
            dbg_ref[...] = jnp.zeros_like(dbg_ref)

        dbg_ref[...] += part

    gate = lambda b: pl.BlockSpec((tm, tn), lambda j, i: (i, b * nj + j))
    act = lambda k: pl.BlockSpec((tm, k), lambda j, i: (i, 0))
    wgt = lambda k: pl.BlockSpec((k, tn), lambda j, i: (0, j))
    tile = pl.BlockSpec((tm, tn), lambda j, i: (i, j))
    return pl.pallas_call(
        body, name=name, grid=(nj, s // tm),
        in_specs=[gate(0), gate(1), gate(2), pl.BlockSpec((3, tn), lambda j, i: (0, j)),
                  act(D_ATT), act(D_CONV), act(D_SGU), wgt(D_ATT), wgt(D_CONV), wgt(D_SGU), tile],
        out_specs=[tile, tile, tile, pl.BlockSpec((3, tm, tn), lambda j, i: (0, i, j)),
                   pl.BlockSpec((3, tn), lambda j, i: (0, j))],
        out_shape=[jax.ShapeDtypeStruct((s, d), BF16)] * 3 + [jax.ShapeDtypeStruct((3, s, d), BF16),
                                                              jax.ShapeDtypeStruct((3, d), F32)],
        compiler_params=_params("parallel", "arbitrary"),
    )(h, h, h, b_gate, att, yc, ys, wa, wc, ws, dm)


def _ffn_act_fwd(hh, cw, name):
    s, dff = hh.shape[0], hh.shape[1] // 2
    nblk = dff // LANES

    def body(a_ref, b_ref, wa_ref, wb_ref, z_ref):
        rows = lax.broadcasted_iota(jnp.int32, (s, LANES), 0)
        z_ref[...] = (_gelu(_conv3(a_ref[...], wa_ref, rows)) * _conv3(b_ref[...], wb_ref, rows)).astype(BF16)

    col = lambda base: pl.BlockSpec((s, LANES), lambda j: (0, base + j))
    wsp = lambda base: pl.BlockSpec((3, LANES), lambda j: (0, base + j))
    return pl.pallas_call(
        body, name=name, grid=(nblk,), in_specs=[col(0), col(nblk), wsp(0), wsp(nblk)], out_specs=col(0),
        out_shape=jax.ShapeDtypeStruct((s, dff), BF16), compiler_params=_params("parallel"),
    )(hh, hh, cw, cw)


def _ffn_act_bwd(hh, cw, dz, name):
    s, dff = hh.shape[0], hh.shape[1] // 2
    nblk = dff // LANES

    def body(a_ref, b_ref, wa_ref, wb_ref, dz_ref, da_ref, db_ref, dwa_ref, dwb_ref):
        rows = lax.broadcasted_iota(jnp.int32, (s, LANES), 0)
        a, b, wa, wb, dz = a_ref[...], b_ref[...], wa_ref, wb_ref, dz_ref[...]
        ga, slope = _gelu_and_slope(_conv3(a, wa, rows))
        dca = dz * _conv3(b, wb, rows) * slope
        dcb = dz * ga
        da_ref[...] = _conv3_t(dca, wa, rows).astype(BF16)
        db_ref[...] = _conv3_t(dcb, wb, rows).astype(BF16)
        dwa_ref[...] = _conv3_dw(dca, a, rows)
        dwb_ref[...] = _conv3_dw(dcb, b, rows)

    col = lambda base: pl.BlockSpec((s, LANES), lambda j: (0, base + j))
    wsp = lambda base: pl.BlockSpec((3, LANES), lambda j: (0, base + j))
    w8 = lambda base: pl.BlockSpec((8, LANES), lambda j: (0, base + j))
    return pl.pallas_call(
        body, name=name, grid=(nblk,), in_specs=[col(0), col(nblk), wsp(0), wsp(nblk), col(0)],
        out_specs=[col(0), col(0), w8(0), w8(0)],
        out_shape=[jax.ShapeDtypeStruct((s, dff), BF16)] * 2 + [jax.ShapeDtypeStruct((8, dff), F32)] * 2,
        compiler_params=_params("parallel"),
    )(hh, hh, cw, cw, dz)


ANY = pl.BlockSpec(memory_space=pl.ANY)


def _place():
    return lax.axis_index("x"), lax.axis_index("y"), lax.axis_index("c")


def _all_gather(arrs, name):
    n = len(arrs)

    def body(*refs):
        ins, outs = refs[:n], refs[n:2 * n]
        send_sems, recv_sems, local_sems = refs[2 * n:]
        x, y, c = _place()
        me, sibling = (x, y, c), (x, y, 1 - c)
        chips = [(1 - x, y), (x, 1 - y), (1 - x, 1 - y)]

        def slab(a, dev):
            return outs[a].at[4 * dev[0] + 2 * dev[1] + dev[2]]

        def copy(a, k, block, to, src=None):
            return pltpu.make_async_remote_copy(
                src_ref=slab(a, block) if src is None else src, dst_ref=slab(a, block),
                send_sem=send_sems.at[7 * a + k], recv_sem=recv_sems.at[7 * a + k], device_id=to, device_id_type=MESH)

        mine = [pltpu.make_async_copy(ins[a], slab(a, me), local_sems.at[a]) for a in range(n)]
        for cp in mine:
            cp.start()
        first = []
        for a in range(n):
            first.append(copy(a, 0, me, sibling, src=ins[a]))
            first += [copy(a, 1 + j, me, (*chip, c), src=ins[a]) for j, chip in enumerate(chips)]
        for cp in first:
            cp.start()
        passed = []
        for a in range(n):
            for j, chip in enumerate(chips):
                copy(a, 1 + j, (*chip, c), me).wait_recv()
                fwd = copy(a, 4 + j, (*chip, c), sibling)
                fwd.start()
                passed.append(fwd)
        for a in range(n):
            copy(a, 0, sibling, me).wait_recv()
            for j, chip in enumerate(chips):
                copy(a, 4 + j, (*chip, 1 - c), me).wait_recv()
        for cp in first + passed:
            cp.wait_send()
        for cp in mine:
            cp.wait()

    return pl.pallas_call(
        body, name=name, in_specs=[ANY] * n, out_specs=[ANY] * n,
        out_shape=[jax.ShapeDtypeStruct((N_DEV,) + a.shape, a.dtype) for a in arrs],
        scratch_shapes=[pltpu.SemaphoreType.DMA((7 * n,)), pltpu.SemaphoreType.DMA((7 * n,)), pltpu.SemaphoreType.DMA((n,))],
    )(*arrs)


HBM = pl.BlockSpec(memory_space=pltpu.HBM)
SEM = pl.BlockSpec(memory_space=pltpu.SEMAPHORE)
EFFECT = pltpu.SideEffectType.DATAFLOW_SIDE_EFFECTING


def _slot(dev):
    return 4 * dev[0] + 2 * dev[1] + dev[2]


def _exchange_copies(src_refs, land_refs, send_sems, recv_sems, src_view, land_view):
    x, y, c = _place()
    me = (x, y, c)
    peers = [(1 - x if r & 4 else x, 1 - y if r & 2 else y, 1 - c if r & 1 else c) for r in range(1, N_DEV)]
    sends, lands = [], []
    for a, (src, land) in enumerate(zip(src_refs, land_refs)):
        for k, peer in enumerate(peers):
            sems = dict(send_sem=send_sems.at[7 * a + k], recv_sem=recv_sems.at[7 * a + k], device_id=peer,
                        device_id_type=MESH)
            sends.append(pltpu.make_async_remote_copy(src_ref=src_view(src, _slot(peer)),
                                                      dst_ref=land_view(land, _slot(me)), **sems))
            lands.append(pltpu.make_async_remote_copy(src_ref=src_view(src, _slot(me)),
                                                      dst_ref=land_view(land, _slot(peer)), **sems))
    return sends, lands


def _own_copies(src_refs, land_refs, local_sems, src_view, land_view):
    me = _slot(_place())
    return [pltpu.make_async_copy(src_view(src, me), land_view(land, me), local_sems.at[a])
            for a, (src, land) in enumerate(zip(src_refs, land_refs))]


def _exchange_start(srcs, lands, after, src_view, land_view, name):
    n = len(srcs)

    def body(*refs):
        src_refs, land_refs = refs[:n], refs[n:2 * n]
        send_sems, recv_sems, local_sems = refs[2 * n + 1:2 * n + 4]
        token = refs[-1]
        sends, _ = _exchange_copies(src_refs, land_refs, send_sems, recv_sems, src_view, land_view)
        for cp in sends + _own_copies(src_refs, land_refs, local_sems, src_view, land_view):
            cp.start()
        token[...] = jnp.zeros_like(token)

    thru = [pltpu.HBM(a.shape, a.dtype) for a in list(srcs) + list(lands)]
    outs = pl.pallas_call(
        body, name=name,
        out_shape=(pltpu.SemaphoreType.DMA((7 * n,)), pltpu.SemaphoreType.DMA((7 * n,)), pltpu.SemaphoreType.DMA((n,)),
                   *thru, jax.ShapeDtypeStruct((8, LANES), F32)),
        in_specs=[HBM] * (2 * n) + [ANY],
        out_specs=(SEM, SEM, SEM, *([HBM] * (2 * n)), pl.BlockSpec(memory_space=pltpu.VMEM)),
        input_output_aliases={i: 3 + i for i in range(2 * n)},
        compiler_params=pltpu.CompilerParams(has_side_effects=EFFECT),
    )(*[pltpu.with_memory_space_constraint(a, pltpu.HBM) for a in list(srcs) + list(lands)], after)
    return outs[:3], list(outs[3:3 + n]), list(outs[3 + n:3 + 2 * n]), outs[-1]


def _exchange_wait(sems, srcs, lands, after, src_view, land_view, name):
    n = len(srcs)

    def body(*refs):
        src_refs, land_refs = refs[:n], refs[n:2 * n]
        send_sems, recv_sems, local_sems = refs[2 * n:2 * n + 3]
        sends, landed = _exchange_copies(src_refs, land_refs, send_sems, recv_sems, src_view, land_view)
        for cp in sends:
            cp.wait_send()
        for cp in landed:
            cp.wait_recv()
        for cp in _own_copies(src_refs, land_refs, local_sems, src_view, land_view):
            cp.wait()

    after = after if isinstance(after, (list, tuple)) else [after]
    outs = pl.pallas_call(
        body, name=name, out_shape=[pltpu.HBM(a.shape, a.dtype) for a in list(srcs) + list(lands)],
        in_specs=[HBM] * (2 * n) + [SEM, SEM, SEM] + [ANY] * len(after), out_specs=[HBM] * (2 * n),
        input_output_aliases={i: i for i in range(2 * n)},
        compiler_params=pltpu.CompilerParams(has_side_effects=EFFECT),
    )(*srcs, *lands, *sems, *after)
    return list(outs[:n]), list(outs[n:])


ADAMW_BLOCK_BYTES = 1 << 19
PACK_ROWS = 256


def _adamw(slabs, w, m, v, name):
    nl, r, c = w.shape
    row_edges = [r] + [t for t in range(8, r, 8) if r % t == 0]
    col_edges = [c] + [t for t in range(LANES, c, LANES) if c % t == 0]
    fits = [(tr * tc, tc, tr) for tr in row_edges for tc in col_edges if tr * tc * 4 <= ADAMW_BLOCK_BYTES]
    _, tc, tr = max(fits) if fits else (0, min(col_edges), min(row_edges))

    def body(s_ref, w_ref, m_ref, v_ref, g_ref, d_ref, nm_ref, nv_ref):
        g = s_ref[0].astype(F32)
        for q in range(1, N_DEV):
            g = g + s_ref[q].astype(F32)
        m_new = ADAM_B1 * m_ref[...] + (1.0 - ADAM_B1) * g
        v_new = ADAM_B2 * v_ref[...] + (1.0 - ADAM_B2) * (g * g)
        m_hat = m_new / (1.0 - ADAM_B1 ** ADAM_STEP)
        v_hat = v_new / (1.0 - ADAM_B2 ** ADAM_STEP)
        g_ref[...] = g
        d_ref[...] = -ADAM_LR * (m_hat / (jnp.sqrt(v_hat) + ADAM_EPS) + ADAM_WD * w_ref[...])
        nm_ref[...] = m_new
        nv_ref[...] = v_new

    blk = pl.BlockSpec((None, tr, tc), lambda l, i, j: (l, i, j))
    return pl.pallas_call(
        body, name=name, grid=(nl, r // tr, c // tc),
        in_specs=[pl.BlockSpec((N_DEV, None, tr, tc), lambda l, i, j: (0, l, i, j)), blk, blk, blk],
        out_specs=[blk] * 4, out_shape=[jax.ShapeDtypeStruct(w.shape, F32)] * 4,
        compiler_params=_params("parallel", "parallel", "parallel"),
    )(slabs, w, m, v)


def _layout(d):
    off = {"gate": 0, "q": 3 * d}
    off["bg"] = off["q"] + 3 * D_ATT
    off["u"] = off["bg"] + 3 * D_CONV
    off["f"] = off["u"] + 2 * D_SGU
    width = -(-(off["f"] + LANES) // 512) * 512
    return off, width


def _pad_w_in(wt, d, token):
    off, width = _layout(d)
    nqkv, nrest = 3 * D_ATT, 3 * D_CONV + 2 * D_SGU
    pad = jnp.zeros((width - off["f"] - N_HEADS, wt.shape[1]), wt.dtype) + token[0, 0].astype(wt.dtype)
    return jnp.concatenate([wt[nqkv + N_HEADS + nrest:], wt[:nqkv], wt[nqkv + N_HEADS:nqkv + N_HEADS + nrest],
                            wt[nqkv:nqkv + N_HEADS], pad], axis=0)


def _unpad_w_in(wtp, d):
    off, _ = _layout(d)
    return jnp.concatenate([wtp[off["q"]:off["bg"]], wtp[off["f"]:off["f"] + N_HEADS], wtp[off["bg"]:off["f"]],
                            wtp[:off["q"]]], axis=0)


def _cols_from_slabs(g):
    return jnp.transpose(g, (1, 0, 2)).reshape(g.shape[1], N_DEV * g.shape[2])


def _cols_to_slabs(w):
    r, c = w.shape[0], w.shape[1] // N_DEV
    return jnp.transpose(w.reshape(r, N_DEV, c), (1, 0, 2))


def kernel(x, pre_mix_g, post_mix_g, pre_ffn_g, post_ffn_g, w_in, b_forget, b_gate, conv_mix_w, sgu_ln_g, sgu_ln_b, sgu_w, sgu_b, w_branch_att, w_branch_conv, w_branch_sgu, w_out, w_ffn_up, conv_ffn_w, w_ffn_down, loss_target, m_pre_mix_g, m_post_mix_g, m_pre_ffn_g, m_post_ffn_g, m_w_in, m_b_forget, m_b_gate, m_conv_mix_w, m_sgu_ln_g, m_sgu_ln_b, m_sgu_w, m_sgu_b, m_w_branch_att, m_w_branch_conv, m_w_branch_sgu, m_w_out, m_w_ffn_up, m_conv_ffn_w, m_w_ffn_down, v_pre_mix_g, v_post_mix_g, v_pre_ffn_g, v_post_ffn_g, v_w_in, v_b_forget, v_b_gate, v_conv_mix_w, v_sgu_ln_g, v_sgu_ln_b, v_sgu_w, v_sgu_b, v_w_branch_att, v_w_branch_conv, v_w_branch_sgu, v_w_out, v_w_ffn_up, v_conv_ffn_w, v_w_ffn_down):
    depth = w_in.shape[0]
    s, d = x.shape[1], x.shape[2]
    dff = w_ffn_down.shape[1] * N_DEV
    off, _ = _layout(d)
    qblk, bgblk, ublk, fblk = off["q"] // LANES, off["bg"] // LANES, off["u"] // D_SGU, off["f"] // LANES
    x0 = x.reshape(s, d)
    target = loss_target.reshape(s, d)
    ncm, ncf = conv_mix_w.shape[2], conv_ffn_w.shape[2]

    lo = d // N_DEV

    whole = lambda ref, slot: ref
    slab = lambda ref, slot: ref.at[slot]

    w_in_t, w_up_t = jnp.transpose(w_in, (0, 2, 1)), jnp.transpose(w_ffn_up, (0, 2, 1))

    def shards_of(l, part):
        if part == "mix":
            small = jnp.concatenate([b_gate[l], conv_mix_w[l], conv_ffn_w[l]], axis=1)
            return [w_in_t[l].astype(BF16), w_branch_att[l].astype(BF16), w_branch_conv[l].astype(BF16),
                    w_branch_sgu[l].astype(BF16), w_out[l].astype(BF16), small]
        return [w_up_t[l].astype(BF16), w_ffn_down[l].astype(BF16)]

    def gather_start(l, part, after):
        shards = shards_of(l, part)
        lands = [lax.empty((N_DEV,) + a.shape, a.dtype) for a in shards]
        return _exchange_start(shards, lands, after, whole, slab, name=f"gather_start_{part}_{l}")

    def gather_finish(l, part, started, after):
        sems, shards, lands, _ = started[part]
        shards, lands = _exchange_wait(sems, shards, lands, after, whole, slab, name=f"gather_wait_{part}_{l}")
        token = jnp.zeros((8, LANES), F32)
        if l + 1 < depth:
            started[part] = gather_start(l + 1, part, lands[0])
            token = started[part][3]
        return lands, token

    def bfull(l):
        return jnp.repeat(jnp.transpose(sgu_b[l]), HEAD_DIM, axis=1)

    def bf_pad(l):
        return jnp.pad(b_forget[l], (0, LANES - N_HEADS)).reshape(1, LANES)

    saved = []
    weights = []
    xin = x0
    xn = _prenorm(x0, pre_mix_g[0:1], name="prenorm_first")
    loss_acc = dy = None
    first = _all_gather(shards_of(0, "mix"), name="gather_first")
    started = {"ffn": gather_start(0, "ffn", first[0])}
    if depth > 1:
        started["mix"] = gather_start(1, "mix", started["ffn"][3])
    for l in range(depth):
        if l == 0:
            (g_in, g_a, g_c, g_s, g_o, g_small), token = first, started["mix" if depth > 1 else "ffn"][3]
        else:
            (g_in, g_a, g_c, g_s, g_o, g_small), token = gather_finish(l, "mix", started, xin)
        g_small = _cols_from_slabs(g_small).reshape(3, N_DEV, -1)
        w = dict(w_in=_pad_w_in(g_in.reshape(N_DEV * g_in.shape[1], d), d, token), wa=_cols_from_slabs(g_a),
                 wc=_cols_from_slabs(g_c), ws=_cols_from_slabs(g_s), w_out=g_o.reshape(d, d),
                 b_gate=g_small[:, :, :lo].reshape(3, d), cmw=g_small[:, :, lo:lo + ncm].reshape(3, D_CONV),
                 cfw=g_small[:, :, lo + ncm:].reshape(3, 2 * dff))
        weights.append(w)
        h = _mm(xn, w["w_in"], "nt", F32, name="proj_in")
        c, ct = _forget_prep(h, bf_pad(l), fblk, name="forget_prep")
        att, lse = _attention_fwd(h, c, ct, qblk, name="attention_fwd")
        yc = _sconv_fwd(h, w["cmw"], bgblk, name="sconv_fwd")
        ys = _sgu_fwd(h, sgu_ln_g[l:l + 1], sgu_ln_b[l:l + 1], sgu_w[l], bfull(l), ublk, name="sgu_fwd")
        merged = _merge_fwd(h, w["b_gate"], att, yc, ys, w["wa"], w["wc"], w["ws"], name="merge_fwd")
        o = _mm(merged, w["w_out"], "nn", F32, name="proj_out")
        (g_up, g_dn), token = gather_finish(l, "ffn", started, o)
        w["w_up"], w["w_dn"] = g_up.reshape(2 * dff, d), g_dn.reshape(dff, d)
        x1, xn2 = _postnorm(xin, o, post_mix_g[l:l + 1], pre_ffn_g[l:l + 1] + token[0, 0], name="postnorm_mix")
        hh = _mm(xn2, w["w_up"], "nt", F32, name="ffn_up")
        z = _ffn_act_fwd(hh, w["cfw"], name="ffn_act_fwd")
        f = _mm(z, w["w_dn"], "nn", F32, name="ffn_down")
        saved.append(dict(xin=xin, xn=xn, h=h, c=c, ct=ct, lse=lse, att=att, yc=yc, ys=ys, merged=merged, o=o, x1=x1,
                          xn2=xn2, hh=hh, z=z, f=f))
        if l + 1 < depth:
            xin, xn = _postnorm(x1, f, post_ffn_g[l:l + 1], pre_mix_g[l + 1:l + 2], name="postnorm_ffn")
        else:
            dy, loss_acc = _postnorm_loss(x1, f, post_ffn_g[l:l + 1], target, name="postnorm_loss")
    loss = lax.psum(loss_acc[0, 0] * (0.5 / d), ("x", "y", "c"))

    rep = {k: [None] * depth for k in ("pre_mix_g", "post_mix_g", "pre_ffn_g", "post_ffn_g", "b_forget", "sgu_ln_g",
                                       "sgu_ln_b", "sgu_w", "sgu_b")}
    nsmall = lo + ncm + ncf
    lands = {"win": [lax.empty((N_DEV, depth) + w_in_t.shape[1:], BF16)],
             "mid": [lax.empty((N_DEV, depth) + shp, dt) for shp, dt in (
                 (w_branch_att.shape[1:], BF16), (w_branch_conv.shape[1:], BF16), (w_branch_sgu.shape[1:], BF16),
                 (w_out.shape[1:], BF16), ((3, nsmall), F32))],
             "ffn": [lax.empty((N_DEV, depth) + shp, BF16) for shp in (w_up_t.shape[1:], w_ffn_down.shape[1:])]}
    scatters = {part: [None] * depth for part in lands}

    def scatter_start(l, part, sends, after):
        layer_slab = lambda ref, slot: ref.at[slot, l]
        sems, sends, lands[part], token = _exchange_start(sends, lands[part], after, slab, layer_slab,
                                                          name=f"scatter_start_{part}_{l}")
        scatters[part][l] = (sems, sends, layer_slab)
        return token

    def scatter_finish(part, after):
        for l in range(depth):
            sems, sends, layer_slab = scatters[part][l]
            _, lands[part] = _exchange_wait(sems, sends, lands[part], after, slab, layer_slab,
                                            name=f"scatter_wait_{part}_{l}")
        return lands[part]

    token = jnp.zeros((8, LANES), F32)
    dx = dy
    for l in reversed(range(depth)):
        w, a = weights[l], saved[l]
        df, rep["post_ffn_g"][l] = _postnorm_bwd(a["f"], post_ffn_g[l:l + 1] + token[0, 0], dx, name="postnorm_bwd")
        dz = _mm(df, w["w_dn"], "nt", F32, name="ffn_down_dx")
        g_dn = _mm(a["z"], df, "tn", BF16, name="ffn_down_dw")
        dha, dhb, dcwa, dcwb = _ffn_act_bwd(a["hh"], w["cfw"], dz, name="ffn_act_bwd")
        dhh = jnp.concatenate([dha, dhb], axis=1)
        dcfw = jnp.concatenate([dcwa[0:3], dcwb[0:3]], axis=1)
        g_up = _mm(dhh, a["xn2"], "tn", BF16, name="ffn_up_dw")
        token = scatter_start(l, "ffn", [g_up.reshape(N_DEV, 2 * dff // N_DEV, d), g_dn.reshape(N_DEV, dff // N_DEV, d)],
                              dz)
        dxn2 = _mm(dhh, w["w_up"], "nn", F32, name="ffn_up_dx", after=token)
        dx1, rep["pre_ffn_g"][l] = _prenorm_bwd(a["x1"], pre_ffn_g[l:l + 1], dxn2, dx, name="prenorm_bwd")
        do, rep["post_mix_g"][l] = _postnorm_bwd(a["o"], post_mix_g[l:l + 1], dx1, name="postnorm_bwd")
        dmerged = _mm(do, w["w_out"], "nt", F32, name="proj_out_dx")
        g_o = _mm(a["merged"], do, "tn", BF16, name="proj_out_dw")
        dya, dyc, dys, dgl, dbg = _merge_bwd(a["h"], w["b_gate"], a["att"], a["yc"], a["ys"], w["wa"], w["wc"], w["ws"],
                                             dmerged, name="merge_bwd")
        dconv = _mm(dyc, w["wc"], "nt", F32, name="branch_conv_dx")
        dsgu = _mm(dys, w["ws"], "nt", F32, name="branch_sgu_dx")
        g_a = _mm(a["att"], dya, "tn", BF16, name="branch_att_dw")
        g_c = _mm(a["yc"], dyc, "tn", BF16, name="branch_conv_dw")
        g_s = _mm(a["ys"], dys, "tn", BF16, name="branch_sgu_dw")
        dbgate, dcg, dhc, dcmw = _sconv_bwd(a["h"], w["cmw"], dconv, bgblk, name="sconv_bwd")
        sends = [_cols_to_slabs(g_a), _cols_to_slabs(g_c), _cols_to_slabs(g_s), g_o.reshape(N_DEV, d // N_DEV, d),
                 jnp.concatenate([_cols_to_slabs(dbg), _cols_to_slabs(dcmw[0:3]), _cols_to_slabs(dcfw)], axis=2)]
        token = scatter_start(l, "mid", sends, dx1)
        datt = _mm(dya, w["wa"], "nt", F32, name="branch_att_dx", after=token)
        dq, dk, dv, dct4, dcq4 = _attention_bwd(a["h"], a["c"], a["ct"], a["lse"], a["att"], datt, qblk,
                                                name="attention_bwd")
        dfl, dbf = _forget_prep_bwd(a["h"], bf_pad(l), dct4, dcq4, fblk, name="forget_prep_bwd")
        rep["b_forget"][l] = dbf[0, :N_HEADS]
        du, dvs, dlg, dlb, dsw, dbfull = _sgu_bwd(a["h"], sgu_ln_g[l:l + 1], sgu_ln_b[l:l + 1], sgu_w[l], bfull(l), dsgu,
                                                  ublk, name="sgu_bwd")
        rep["sgu_ln_g"][l], rep["sgu_ln_b"][l], rep["sgu_w"][l] = dlg, dlb, dsw
        rep["sgu_b"][l] = jnp.transpose(jnp.sum(dbfull.reshape(SGU_CHUNK, N_SGU_GROUPS, HEAD_DIM), axis=2))
        dh = jnp.concatenate([dgl[0], dgl[1], dgl[2], dq.astype(BF16), dk.astype(BF16), dv.astype(BF16), dbgate, dcg, dhc,
                              du, dvs, dfl.astype(BF16), jnp.zeros((s, w["w_in"].shape[0] - off["f"] - LANES), BF16)], axis=1)
        g_in = _mm(dh, a["xn"], "tn", BF16, name="proj_in_dw")
        token = scatter_start(l, "win", [_unpad_w_in(g_in, d).reshape(N_DEV, -1, d)], dx1)
        dxn = _mm(dh, w["w_in"], "nn", F32, name="proj_in_dx", after=token)
        dx, rep["pre_mix_g"][l] = _prenorm_bwd(a["xin"], pre_mix_g[l:l + 1], dxn, dx1, name="prenorm_bwd")

    outs = {}
    t3 = lambda arr: jnp.transpose(arr, (0, 2, 1))

    def update(name_, slabs, w_, m_, v_, transposed=False):
        if transposed:
            w_, m_, v_ = t3(w_), t3(m_), t3(v_)
        shp = w_.shape
        w3 = w_.reshape((shp[0], -1, shp[-1])) if w_.ndim >= 3 else w_.reshape((1,) + shp)
        res = _adamw(slabs.reshape((N_DEV,) + w3.shape), w3, m_.reshape(w3.shape), v_.reshape(w3.shape),
                     name="adamw_" + name_)
        outs[name_] = tuple(t3(t.reshape(shp)) if transposed else t.reshape(shp) for t in res)
        return res[0]

    rep_names = ("pre_mix_g", "post_mix_g", "pre_ffn_g", "post_ffn_g", "b_forget", "sgu_ln_g", "sgu_ln_b", "sgu_w", "sgu_b")
    rep_w = dict(pre_mix_g=(pre_mix_g, m_pre_mix_g, v_pre_mix_g), post_mix_g=(post_mix_g, m_post_mix_g, v_post_mix_g),
                 pre_ffn_g=(pre_ffn_g, m_pre_ffn_g, v_pre_ffn_g), post_ffn_g=(post_ffn_g, m_post_ffn_g, v_post_ffn_g),
                 b_forget=(b_forget, m_b_forget, v_b_forget), sgu_ln_g=(sgu_ln_g, m_sgu_ln_g, v_sgu_ln_g),
                 sgu_ln_b=(sgu_ln_b, m_sgu_ln_b, v_sgu_ln_b), sgu_w=(sgu_w, m_sgu_w, v_sgu_w), sgu_b=(sgu_b, m_sgu_b, v_sgu_b))

    def pack(parts):
        rows = [jnp.pad(p.reshape(-1), (0, -p.size % LANES)).reshape(-1, LANES) for p in parts]
        rows = jnp.concatenate(rows, axis=0)
        return jnp.pad(rows, ((0, -rows.shape[0] % PACK_ROWS), (0, 0)))

    part = pack([jnp.stack([g.reshape(rep_w[k][0].shape[1:]) for g in rep[k]]) for k in rep_names])
    small_sems, small_src, small_land, _ = _exchange_start([part], [lax.empty((N_DEV,) + part.shape, F32)], dx, whole, slab,
                                                            name="gather_small_start")

    got_up, got_dn = scatter_finish("ffn", dx)
    done = [update("w_ffn_up", got_up, w_ffn_up, m_w_ffn_up, v_w_ffn_up, transposed=True),
            update("w_ffn_down", got_dn, w_ffn_down, m_w_ffn_down, v_w_ffn_down)]

    _, (gathered,) = _exchange_wait(small_sems, small_src, small_land, done, whole, slab, name="gather_small_wait")
    packed = [pack([rep_w[k][i] for k in rep_names]) for i in range(3)]
    res = _adamw(gathered.reshape(N_DEV, 1, -1, LANES), *[p.reshape(1, -1, LANES) for p in packed], name="adamw_replicated")
    row = 0
    for k in rep_names:
        shp = rep_w[k][0].shape
        size = math.prod(shp)
        nrows = -(-size // LANES)
        outs[k] = tuple(t[0, row:row + nrows].reshape(-1)[:size].reshape(shp) for t in res)
        row += nrows

    got_a, got_c, got_s, got_o, small = scatter_finish("mid", res[0])
    done = [update("w_branch_att", got_a, w_branch_att, m_w_branch_att, v_w_branch_att),
            update("w_branch_conv", got_c, w_branch_conv, m_w_branch_conv, v_w_branch_conv),
            update("w_branch_sgu", got_s, w_branch_sgu, m_w_branch_sgu, v_w_branch_sgu),
            update("w_out", got_o, w_out, m_w_out, v_w_out),
            update("b_gate", small[..., :lo], b_gate, m_b_gate, v_b_gate),
            update("conv_mix_w", small[..., lo:lo + ncm], conv_mix_w, m_conv_mix_w, v_conv_mix_w),
            update("conv_ffn_w", small[..., lo + ncm:], conv_ffn_w, m_conv_ffn_w, v_conv_ffn_w)]
    (got_in,) = scatter_finish("win", done)
    update("w_in", got_in, w_in, m_w_in, v_w_in, transposed=True)

    order = ("pre_mix_g", "post_mix_g", "pre_ffn_g", "post_ffn_g", "w_in", "b_forget", "b_gate", "conv_mix_w", "sgu_ln_g",
             "sgu_ln_b", "sgu_w", "sgu_b", "w_branch_att", "w_branch_conv", "w_branch_sgu", "w_out", "w_ffn_up",
             "conv_ffn_w", "w_ffn_down")
    grad_x = dx.reshape(x.shape)
    return (loss, grad_x, *[outs[k][0] for k in order], *[outs[k][1] for k in order], *[outs[k][2] for k in order],
            *[outs[k][3] for k in order])
```

```python
import functools
import math

import jax
import jax.numpy as jnp
from jax import lax
from jax.experimental import pallas as pl
from jax.experimental.pallas import tpu as pltpu

F32 = jnp.float32
BF16 = jnp.bfloat16

N_DEV = 8
HEAD_DIM = 64
N_HEADS = 8
D_ATT = 512
D_CONV = 256
D_SGU = 256
N_SGU_GROUPS = 4
SGU_CHUNK = 128
RMS_EPS = 1e-6
LN_EPS = 1e-5
ADAM_LR = 0.001
ADAM_B1 = 0.9
ADAM_B2 = 0.999
ADAM_EPS = 1e-08
ADAM_WD = 0.01
ADAM_STEP = 10
LANES = 128
VMEM_LIMIT = 56 * 1024 * 1024
ATT_TILE = 512
LOG2E = math.log2(math.e)
NEG = -1e30
MESH = pl.DeviceIdType.MESH


def _params(*sem):
    return pltpu.CompilerParams(dimension_semantics=sem if sem else None, vmem_limit_bytes=VMEM_LIMIT)


def _tile(n, cap):
    if n <= cap:
        return n
    t = cap - cap % LANES
    while n % t:
        t -= LANES
    return t


def _gelu(x):
    return 0.5 * x * (1.0 + jnp.tanh(math.sqrt(2.0 / math.pi) * (x + 0.044715 * (x * x * x))))


def _gelu_and_slope(x):
    k0, k1 = math.sqrt(2.0 / math.pi), 0.044715
    x2 = x * x
    t = jnp.tanh(x * (k0 + (k0 * k1) * x2))
    half = 0.5 * (1.0 + t)
    return x * half, half + (0.5 * x) * (1.0 - t * t) * (k0 + (3.0 * k0 * k1) * x2)


def _rms(x, g):
    r = lax.rsqrt(jnp.mean(x * x, axis=-1, keepdims=True) + RMS_EPS)
    return x * r * g


def _layer_norm(x, g, b):
    mu = jnp.mean(x, axis=-1, keepdims=True)
    xc = x - mu
    var = jnp.mean(xc * xc, axis=-1, keepdims=True)
    return xc * lax.rsqrt(var + LN_EPS) * g + b


def _shift_down(x, k, rows):
    return jnp.where(rows >= k, pltpu.roll(x, k, 0), 0.0)


def _shift_up(x, k, rows):
    s = x.shape[0]
    return jnp.where(rows < s - k, pltpu.roll(x, s - k, 0), 0.0)


def _conv3(x, w_ref, rows):
    return w_ref[2:3, :] * x + w_ref[1:2, :] * _shift_down(x, 1, rows) + w_ref[0:1, :] * _shift_down(x, 2, rows)


def _conv3_bwd(dy, x, w_ref, rows):
    up1, up2 = _shift_up(dy, 1, rows), _shift_up(dy, 2, rows)
    dx = w_ref[2:3, :] * dy + w_ref[1:2, :] * up1 + w_ref[0:1, :] * up2
    d2 = jnp.sum(dy * x, axis=0, keepdims=True)
    d1 = jnp.sum(up1 * x, axis=0, keepdims=True)
    d0 = jnp.sum(up2 * x, axis=0, keepdims=True)
    sub = lax.broadcasted_iota(jnp.int32, (8, x.shape[1]), 0)
    return dx, jnp.where(sub == 0, d0, jnp.where(sub == 1, d1, jnp.where(sub == 2, d2, 0.0)))


MM_VMEM_BUDGET = 40 * 1024 * 1024
MM_TILE_CAP = 1408


def _mm_tiles(m, n, k, out_bytes):
    def edges(d):
        return [t for t in range(LANES, min(d, MM_TILE_CAP) + 1, LANES) if d % t == 0] or [d]

    best = None
    for tm in edges(m):
        for tn in edges(n):
            if 2 * (2 * k * (tm + tn) + tm * tn * out_bytes) > MM_VMEM_BUDGET:
                continue
            for a_outer in (True, False):
                reads = k * m + (m // tm) * k * n if a_outer else k * n + (n // tn) * k * m
                traffic = 2 * reads + m * n * out_bytes
                key = (traffic, -tm * tn)
                if best is None or key < best[0]:
                    best = (key, (tm, tn, a_outer))
    return best[1]


def _mm(a, b, form, out_dtype, name, after=None):
    if form == "nn":
        (m, k), n = a.shape, b.shape[1]
    elif form == "nt":
        (m, k), n = a.shape, b.shape[0]
    else:
        (k, m), n = a.shape, b.shape[1]
    tm, tn, a_outer = _mm_tiles(m, n, k, jnp.dtype(out_dtype).itemsize)
    dims = {"nn": (((1,), (0,)), ((), ())), "nt": (((1,), (1,)), ((), ())), "tn": (((0,), (0,)), ((), ()))}[form]

    def body(a_ref, b_ref, *rest):
        o_ref = rest[-1]
        o_ref[...] = lax.dot_general(a_ref[...], b_ref[...], dims, preferred_element_type=F32).astype(o_ref.dtype)

    ij = (lambda g0, g1: (g0, g1)) if a_outer else (lambda g0, g1: (g1, g0))
    a_spec = (pl.BlockSpec((k, tm), lambda g0, g1: (0, ij(g0, g1)[0])) if form == "tn"
              else pl.BlockSpec((tm, k), lambda g0, g1: (ij(g0, g1)[0], 0)))
    b_spec = (pl.BlockSpec((tn, k), lambda g0, g1: (ij(g0, g1)[1], 0)) if form == "nt"
              else pl.BlockSpec((k, tn), lambda g0, g1: (0, ij(g0, g1)[1])))
    extra = [] if after is None else [pl.BlockSpec((8, LANES), lambda g0, g1: (0, 0))]
    return pl.pallas_call(
        body, name=name, grid=(m // tm, n // tn) if a_outer else (n // tn, m // tm),
        in_specs=[a_spec, b_spec] + extra, out_specs=pl.BlockSpec((tm, tn), lambda g0, g1: ij(g0, g1)),
        out_shape=jax.ShapeDtypeStruct((m, n), out_dtype),
        compiler_params=_params("parallel", "arbitrary"),
    )(a, b, *([] if after is None else [after]))


def _prenorm(x, g, name):
    s, d = x.shape
    tm = _tile(s, 512)

    def body(x_ref, g_ref, o_ref):
        o_ref[...] = _rms(x_ref[...], g_ref[...]).astype(BF16)

    return pl.pallas_call(
        body, name=name, grid=(s // tm,),
        in_specs=[pl.BlockSpec((tm, d), lambda i: (i, 0)), pl.BlockSpec((1, d), lambda i: (0, 0))],
        out_specs=pl.BlockSpec((tm, d), lambda i: (i, 0)),
        out_shape=jax.ShapeDtypeStruct((s, d), BF16), compiler_params=_params("parallel"),
    )(x, g)


def _postnorm(x, o, g_post, g_next, name):
    s, d = x.shape
    tm = _tile(s, 512)

    def body(x_ref, o_ref, gp_ref, gn_ref, x1_ref, xn_ref):
        x1 = x_ref[...] + _rms(o_ref[...], gp_ref[...])
        x1_ref[...] = x1
        xn_ref[...] = _rms(x1, gn_ref[...]).astype(BF16)

    row = pl.BlockSpec((tm, d), lambda i: (i, 0))
    vec = pl.BlockSpec((1, d), lambda i: (0, 0))
    return pl.pallas_call(
        body, name=name, grid=(s // tm,), in_specs=[row, row, vec, vec], out_specs=[row, row],
        out_shape=[jax.ShapeDtypeStruct((s, d), F32), jax.ShapeDtypeStruct((s, d), BF16)],
        compiler_params=_params("parallel"),
    )(x, o, g_post, g_next)


def _postnorm_loss(x, o, g_post, target, name):
    s, d = x.shape
    tm = _tile(s, 512)

    def body(x_ref, o_ref, gp_ref, t_ref, dy_ref, acc_ref):
        e = x_ref[...] + _rms(o_ref[...], gp_ref[...]) - t_ref[...]
        dy_ref[...] = e / d

        @pl.when(pl.program_id(0) == 0)
        def _():
            acc_ref[...] = jnp.zeros_like(acc_ref)

        acc_ref[...] += jnp.sum(jnp.sum(e * e, axis=1, keepdims=True), axis=0, keepdims=True)

    row = pl.BlockSpec((tm, d), lambda i: (i, 0))
    return pl.pallas_call(
        body, name=name, grid=(s // tm,),
        in_specs=[row, row, pl.BlockSpec((1, d), lambda i: (0, 0)), row],
        out_specs=[row, pl.BlockSpec((1, LANES), lambda i: (0, 0))],
        out_shape=[jax.ShapeDtypeStruct((s, d), F32), jax.ShapeDtypeStruct((1, LANES), F32)],
        compiler_params=_params("arbitrary"),
    )(x, o, g_post, target)


def _postnorm_bwd(o, g, dx, name):
    s, d = o.shape
    tm = _tile(s, 512)

    def body(o_ref, g_ref, dx_ref, do_ref, dg_ref):
        _, vjp = jax.vjp(_rms, o_ref[...], g_ref[...])
        d_o, dg = vjp(dx_ref[...])
        do_ref[...] = d_o.astype(BF16)

        @pl.when(pl.program_id(0) == 0)
        def _():
            dg_ref[...] = jnp.zeros_like(dg_ref)

        dg_ref[...] += dg

    row = pl.BlockSpec((tm, d), lambda i: (i, 0))
    vec = pl.BlockSpec((1, d), lambda i: (0, 0))
    return pl.pallas_call(
        body, name=name, grid=(s // tm,), in_specs=[row, vec, row], out_specs=[row, vec],
        out_shape=[jax.ShapeDtypeStruct((s, d), BF16), jax.ShapeDtypeStruct((1, d), F32)],
        compiler_params=_params("arbitrary"),
    )(o, g, dx)


def _prenorm_bwd(x, g, dxn, dres, name):
    s, d = x.shape
    tm = _tile(s, 512)

    def body(x_ref, g_ref, dxn_ref, dres_ref, dx_ref, dg_ref):
        _, vjp = jax.vjp(_rms, x_ref[...], g_ref[...])
        dx, dg = vjp(dxn_ref[...])
        dx_ref[...] = dres_ref[...] + dx

        @pl.when(pl.program_id(0) == 0)
        def _():
            dg_ref[...] = jnp.zeros_like(dg_ref)

        dg_ref[...] += dg

    row = pl.BlockSpec((tm, d), lambda i: (i, 0))
    vec = pl.BlockSpec((1, d), lambda i: (0, 0))
    return pl.pallas_call(
        body, name=name, grid=(s // tm,), in_specs=[row, vec, row, row], out_specs=[row, vec],
        out_shape=[jax.ShapeDtypeStruct((s, d), F32), jax.ShapeDtypeStruct((1, d), F32)],
        compiler_params=_params("arbitrary"),
    )(x, g, dxn, dres)


def _log_sigmoid(z):
    return jnp.minimum(z, 0.0) - jnp.log(1.0 + jnp.exp(-jnp.abs(z)))


def _forget_prep(h, bf_pad, fblk, name):
    s = h.shape[0]

    def body(f_ref, b_ref, c_ref, ct_ref):
        c = _log_sigmoid(f_ref[...] + b_ref[...])
        rows = lax.broadcasted_iota(jnp.int32, c.shape, 0)
        k = 1
        while k < s:
            c = c + _shift_down(c, k, rows)
            k *= 2
        c_ref[...] = c
        ct_ref[...] = jnp.transpose(c)[0:8, :]

    return pl.pallas_call(
        body, name=name, grid=(1,),
        in_specs=[pl.BlockSpec((s, LANES), lambda i: (0, fblk)), pl.BlockSpec((1, LANES), lambda i: (0, 0))],
        out_specs=[pl.BlockSpec((s, LANES), lambda i: (0, 0)), pl.BlockSpec((8, s), lambda i: (0, 0))],
        out_shape=[jax.ShapeDtypeStruct((s, LANES), F32), jax.ShapeDtypeStruct((8, s), F32)],
        compiler_params=_params("arbitrary"),
    )(h, bf_pad)


def _forget_prep_bwd(h, bf_pad, dct, dcq, fblk, name):
    s = h.shape[0]
    pairs = N_HEADS // 2

    def body(f_ref, b_ref, dct_ref, dcq_ref, df_ref, db_ref):
        dct = dct_ref[0]
        dcq = dcq_ref[0]
        for p in range(1, pairs):
            dct = dct + pltpu.roll(dct_ref[p], 2 * p, 0)
            dcq = dcq + pltpu.roll(dcq_ref[p], 2 * p, 1)
        dc = dcq + jnp.transpose(jnp.concatenate([dct, jnp.zeros((LANES - 8, s), F32)], axis=0))
        rows = lax.broadcasted_iota(jnp.int32, dc.shape, 0)
        k = 1
        while k < s:
            dc = dc + _shift_up(dc, k, rows)
            k *= 2
        z = f_ref[...] + b_ref[...]
        lane = lax.broadcasted_iota(jnp.int32, dc.shape, 1)
        df = jnp.where(lane < N_HEADS, dc * jax.nn.sigmoid(-z), 0.0)
        df_ref[...] = df
        db_ref[...] = jnp.sum(df, axis=0, keepdims=True)

    return pl.pallas_call(
        body, name=name, grid=(1,),
        in_specs=[pl.BlockSpec((s, LANES), lambda i: (0, fblk)), pl.BlockSpec((1, LANES), lambda i: (0, 0)),
                  pl.BlockSpec((pairs, 8, s), lambda i: (0, 0, 0)), pl.BlockSpec((pairs, s, LANES), lambda i: (0, 0, 0))],
        out_specs=[pl.BlockSpec((s, LANES), lambda i: (0, 0)), pl.BlockSpec((1, LANES), lambda i: (0, 0))],
        out_shape=[jax.ShapeDtypeStruct((s, LANES), F32), jax.ShapeDtypeStruct((1, LANES), F32)],
        compiler_params=_params("arbitrary"),
    )(h, bf_pad, dct, dcq)


def _pick_lane(blk, idx):
    lane = lax.broadcasted_iota(jnp.int32, blk.shape, 1)
    return jnp.sum(jnp.where(lane == idx, blk, 0.0), axis=1, keepdims=True)


def _pick_row(blk, idx):
    sub = lax.broadcasted_iota(jnp.int32, blk.shape, 0)
    return jnp.sum(jnp.where(sub == idx, blk, 0.0), axis=0, keepdims=True)


def _attention_fwd(h, c, ct, qblk, name):
    s = h.shape[0]
    t = _tile(s, ATT_TILE)
    nq = s // t
    scale = HEAD_DIM ** -0.5
    nt_dims = (((1,), (1,)), ((), ()))

    def body(q_ref, k_ref, v_ref, c_ref, ct_ref, o_ref, lse_ref):
        p = pl.program_id(0)
        i = pl.program_id(1)
        lane = lax.broadcasted_iota(jnp.int32, (1, LANES), 1)
        first = lane < HEAD_DIM
        q = q_ref[...] * (scale * LOG2E)
        qa = jnp.where(first, q, 0.0).astype(BF16)
        qb = jnp.where(first, 0.0, q).astype(BF16)
        cblk = c_ref[...]
        cta = _pick_lane(cblk, 2 * p) * LOG2E
        ctb = _pick_lane(cblk, 2 * p + 1) * LOG2E

        def step(j, carry, diagonal):
            ma, la, mb, lb, acc = carry
            off = pl.multiple_of(j * t, t)
            k = k_ref[pl.ds(off, t), :].astype(BF16)
            v = v_ref[pl.ds(off, t), :].astype(BF16)
            crow = ct_ref[:, pl.ds(off, t)] * LOG2E

            def one(qh, cth, hd, m_old, l_old):
                sc = lax.dot_general(qh, k, nt_dims, preferred_element_type=F32) - _pick_row(crow, hd)
                if diagonal:
                    keep = lax.broadcasted_iota(jnp.int32, (t, t), 0) >= lax.broadcasted_iota(jnp.int32, (t, t), 1)
                    sc = jnp.where(keep, sc, NEG)
                m_new = jnp.maximum(m_old, jnp.max(sc, axis=1, keepdims=True) + cth)
                pr = jnp.exp2(sc - (m_new - cth))
                alpha = jnp.exp2(m_old - m_new)
                l_new = alpha * l_old + jnp.sum(pr, axis=1, keepdims=True)
                pv = jnp.dot(pr.astype(BF16), v, preferred_element_type=F32)
                return m_new, l_new, alpha, pv

            ma2, la2, aa, pva = one(qa, cta, 2 * p, ma, la)
            mb2, lb2, ab, pvb = one(qb, ctb, 2 * p + 1, mb, lb)
            acc = jnp.where(first, aa * acc + pva, ab * acc + pvb)
            return ma2, la2, mb2, lb2, acc

        init = (jnp.full((t, 1), NEG, F32), jnp.zeros((t, 1), F32), jnp.full((t, 1), NEG, F32),
                jnp.zeros((t, 1), F32), jnp.zeros((t, LANES), F32))
        carry = lax.fori_loop(0, i, lambda j, carry: step(j, carry, False), init)
        ma, la, mb, lb, acc = step(i, carry, True)
        o_ref[...] = (acc / jnp.where(first, la, lb)).astype(BF16)
        lse_ref[0] = jnp.broadcast_to(ma + jnp.log2(la), (t, LANES))
        lse_ref[1] = jnp.broadcast_to(mb + jnp.log2(lb), (t, LANES))

    return pl.pallas_call(
        body, name=name, grid=(N_HEADS // 2, nq),
        in_specs=[pl.BlockSpec((t, LANES), lambda p, i: (i, qblk + p)),
                  pl.BlockSpec((s, LANES), lambda p, i: (0, qblk + 4 + p)),
                  pl.BlockSpec((s, LANES), lambda p, i: (0, qblk + 8 + p)),
                  pl.BlockSpec((t, LANES), lambda p, i: (i, 0)),
                  pl.BlockSpec((8, s), lambda p, i: (0, 0))],
        out_specs=[pl.BlockSpec((t, LANES), lambda p, i: (i, p)),
                   pl.BlockSpec((2, t, LANES), lambda p, i: (p, i, 0))],
        out_shape=[jax.ShapeDtypeStruct((s, D_ATT), BF16), jax.ShapeDtypeStruct((N_HEADS, s, LANES), F32)],
        compiler_params=_params("parallel", "arbitrary"),
    )(h, h, h, c, ct)


def _attention_bwd(h, c, ct, lse, att, datt, qblk, name):
    s = h.shape[0]
    t = _tile(s, ATT_TILE)
    nq = s // t
    scale = HEAD_DIM ** -0.5
    nt_dims = (((1,), (1,)), ((), ()))
    tn_dims = (((0,), (0,)), ((), ()))

    def body(q_ref, k_ref, v_ref, c_ref, ct_ref, lse_ref, o_ref, do_ref, dq_ref, dk_ref, dv_ref, dct_ref, dcq_ref):
        p = pl.program_id(0)
        j = pl.program_id(1)
        lane = lax.broadcasted_iota(jnp.int32, (1, LANES), 1)
        first = lane < HEAD_DIM
        kf = k_ref[...]
        vf = v_ref[...]
        k = kf.astype(BF16)
        ka = jnp.where(first, kf, 0.0).astype(BF16)
        kb = jnp.where(first, 0.0, kf).astype(BF16)
        va = jnp.where(first, vf, 0.0).astype(BF16)
        vb = jnp.where(first, 0.0, vf).astype(BF16)
        crow = ct_ref[...] * LOG2E
        csa = _pick_row(crow, 2 * p)
        csb = _pick_row(crow, 2 * p + 1)

        @pl.when(j == 0)
        def _():
            dq_ref[...] = jnp.zeros_like(dq_ref)
            dcq_ref[...] = jnp.zeros_like(dcq_ref)

        def step(i, carry, diagonal):
            dka, dkb, dva, dvb, dca, dcb = carry
            off = pl.multiple_of(i * t, t)
            rows = pl.ds(off, t)
            q = (q_ref[rows, :] * (scale * LOG2E)).astype(BF16)
            dof = do_ref[rows, :]
            do = dof.astype(BF16)
            prod = dof * o_ref[rows, :].astype(F32)
            cblk = c_ref[rows, :] * LOG2E

            def one(kh, vh, hd, csh, lse_h):
                sc = lax.dot_general(q, kh, nt_dims, preferred_element_type=F32) - csh
                if diagonal:
                    keep = lax.broadcasted_iota(jnp.int32, (t, t), 0) >= lax.broadcasted_iota(jnp.int32, (t, t), 1)
                    sc = jnp.where(keep, sc, NEG)
                pr = jnp.exp2(sc - (jnp.max(lse_h, axis=1, keepdims=True) - _pick_lane(cblk, hd)))
                dp = lax.dot_general(do, vh, nt_dims, preferred_element_type=F32)
                return pr, dp

            pra, dpa = one(ka, va, 2 * p, csa, lse_ref[0, rows, :])
            prb, dpb = one(kb, vb, 2 * p + 1, csb, lse_ref[1, rows, :])
            dela = jnp.sum(jnp.where(first, prod, 0.0), axis=1, keepdims=True)
            delb = jnp.sum(jnp.where(first, 0.0, prod), axis=1, keepdims=True)
            dsa = pra * (dpa - dela)
            dsb = prb * (dpb - delb)
            dsa16 = dsa.astype(BF16)
            dsb16 = dsb.astype(BF16)
            dva = dva + lax.dot_general(pra.astype(BF16), do, tn_dims, preferred_element_type=F32)
            dvb = dvb + lax.dot_general(prb.astype(BF16), do, tn_dims, preferred_element_type=F32)
            dka = dka + lax.dot_general(dsa16, q, tn_dims, preferred_element_type=F32)
            dkb = dkb + lax.dot_general(dsb16, q, tn_dims, preferred_element_type=F32)
            dqa = jnp.dot(dsa16, k, preferred_element_type=F32)
            dqb = jnp.dot(dsb16, k, preferred_element_type=F32)
            dq_ref[rows, :] += scale * jnp.where(first, dqa, dqb)
            dca = dca - jnp.sum(dsa, axis=0, keepdims=True)
            dcb = dcb - jnp.sum(dsb, axis=0, keepdims=True)
            dcq_ref[rows, :] += jnp.where(lane == 0, jnp.sum(dsa, axis=1, keepdims=True),
                                          jnp.where(lane == 1, jnp.sum(dsb, axis=1, keepdims=True), 0.0))
            return dka, dkb, dva, dvb, dca, dcb

        z = jnp.zeros((t, LANES), F32)
        zr = jnp.zeros((1, t), F32)
        carry = step(j, (z, z, z, z, zr, zr), True)
        dka, dkb, dva, dvb, dca, dcb = lax.fori_loop(j + 1, nq, lambda i, carry: step(i, carry, False), carry)
        dk_ref[...] = jnp.where(first, dka, dkb) * (1.0 / LOG2E)
        dv_ref[...] = jnp.where(first, dva, dvb)
        sub = lax.broadcasted_iota(jnp.int32, (8, t), 0)
        dct_ref[...] = jnp.where(sub == 0, dca, jnp.where(sub == 1, dcb, 0.0))

    full = lambda blk: pl.BlockSpec((s, LANES), blk)
    return pl.pallas_call(
        body, name=name, grid=(N_HEADS // 2, nq),
        in_specs=[full(lambda p, j: (0, qblk + p)),
                  pl.BlockSpec((t, LANES), lambda p, j: (j, qblk + 4 + p)),
                  pl.BlockSpec((t, LANES), lambda p, j: (j, qblk + 8 + p)),
                  full(lambda p, j: (0, 0)),
                  pl.BlockSpec((8, t), lambda p, j: (0, j)),
                  pl.BlockSpec((2, s, LANES), lambda p, j: (p, 0, 0)),
                  full(lambda p, j: (0, p)),
                  full(lambda p, j: (0, p))],
        out_specs=[full(lambda p, j: (0, p)),
                   pl.BlockSpec((t, LANES), lambda p, j: (j, p)),
                   pl.BlockSpec((t, LANES), lambda p, j: (j, p)),
                   pl.BlockSpec((None, 8, t), lambda p, j: (p, 0, j)),
                   pl.BlockSpec((None, s, LANES), lambda p, j: (p, 0, 0))],
        out_shape=[jax.ShapeDtypeStruct((s, D_ATT), F32), jax.ShapeDtypeStruct((s, D_ATT), F32),
                   jax.ShapeDtypeStruct((s, D_ATT), F32), jax.ShapeDtypeStruct((N_HEADS // 2, 8, s), F32),
                   jax.ShapeDtypeStruct((N_HEADS // 2, s, LANES), F32)],
        compiler_params=_params("arbitrary", "arbitrary"),
    )(h, h, h, c, ct, lse, att, datt)


def _sconv_fwd(h, w, bgblk, name):
    s = h.shape[0]
    nblk = D_CONV // LANES

    def body(bg_ref, cg_ref, hc_ref, w_ref, y_ref):
        rows = lax.broadcasted_iota(jnp.int32, (s, LANES), 0)
        y_ref[...] = (bg_ref[...] * _conv3(cg_ref[...] * hc_ref[...], w_ref, rows)).astype(BF16)

    col = lambda base: pl.BlockSpec((s, LANES), lambda j: (0, base + j))
    return pl.pallas_call(
        body, name=name, grid=(nblk,),
        in_specs=[col(bgblk), col(bgblk + nblk), col(bgblk + 2 * nblk), pl.BlockSpec((3, LANES), lambda j: (0, j))],
        out_specs=pl.BlockSpec((s, LANES), lambda j: (0, j)),
        out_shape=jax.ShapeDtypeStruct((s, D_CONV), BF16), compiler_params=_params("parallel"),
    )(h, h, h, w)


def _sconv_bwd(h, w, dy, bgblk, name):
    s = h.shape[0]
    nblk = D_CONV // LANES

    def body(bg_ref, cg_ref, hc_ref, w_ref, dy_ref, dbg_ref, dcg_ref, dhc_ref, dw_ref):
        rows = lax.broadcasted_iota(jnp.int32, (s, LANES), 0)
        cg, hc, dy, w = cg_ref[...], hc_ref[...], dy_ref[...], w_ref
        xin = cg * hc
        dbg_ref[...] = (dy * _conv3(xin, w, rows)).astype(BF16)
        dxin, dw_ref[...] = _conv3_bwd(dy * bg_ref[...], xin, w, rows)
        dcg_ref[...] = (dxin * hc).astype(BF16)
        dhc_ref[...] = (dxin * cg).astype(BF16)

    col = lambda base: pl.BlockSpec((s, LANES), lambda j: (0, base + j))
    return pl.pallas_call(
        body, name=name, grid=(nblk,),
        in_specs=[col(bgblk), col(bgblk + nblk), col(bgblk + 2 * nblk), pl.BlockSpec((3, LANES), lambda j: (0, j)), col(0)],
        out_specs=[col(0), col(0), col(0), pl.BlockSpec((8, LANES), lambda j: (0, j))],
        out_shape=[jax.ShapeDtypeStruct((s, D_CONV), BF16)] * 3 + [jax.ShapeDtypeStruct((8, D_CONV), F32)],
        compiler_params=_params("parallel"),
    )(h, h, h, w, dy)


def _sgu_group_masks():
    lane = lax.broadcasted_iota(jnp.int32, (1, D_SGU), 1)
    return [(lane // HEAD_DIM) == g for g in range(N_SGU_GROUPS)]


def _sgu_tril():
    r = lax.broadcasted_iota(jnp.int32, (SGU_CHUNK, SGU_CHUNK), 0)
    c = lax.broadcasted_iota(jnp.int32, (SGU_CHUNK, SGU_CHUNK), 1)
    return r >= c


def _sgu_fwd(h, ln_g, ln_b, w_s, b_full, ublk, name):
    s = h.shape[0]
    tr = _tile(s, 512)
    nch = tr // SGU_CHUNK

    def body(u_ref, v_ref, g_ref, b_ref, w_ref, bf_ref, y_ref):
        masks = _sgu_group_masks()
        tril = _sgu_tril()
        wm = [jnp.where(tril, w_ref[g], 0.0).astype(BF16) for g in range(N_SGU_GROUPS)]
        vn = _layer_norm(_gelu(v_ref[...]), g_ref[...], b_ref[...])
        for ch in range(nch):
            rows = pl.ds(ch * SGU_CHUNK, SGU_CHUNK)
            vc = vn[ch * SGU_CHUNK:(ch + 1) * SGU_CHUNK, :]
            mixed = bf_ref[...]
            for g in range(N_SGU_GROUPS):
                mixed = mixed + jnp.dot(wm[g], jnp.where(masks[g], vc, 0.0).astype(BF16), preferred_element_type=F32)
            y_ref[rows, :] = (_gelu(u_ref[rows, :]) * mixed).astype(BF16)

    row = lambda blk: pl.BlockSpec((tr, D_SGU), lambda i: (i, blk))
    vec = pl.BlockSpec((1, D_SGU), lambda i: (0, 0))
    return pl.pallas_call(
        body, name=name, grid=(s // tr,),
        in_specs=[row(ublk), row(ublk + 1), vec, vec,
                  pl.BlockSpec((N_SGU_GROUPS, SGU_CHUNK, SGU_CHUNK), lambda i: (0, 0, 0)),
                  pl.BlockSpec((SGU_CHUNK, D_SGU), lambda i: (0, 0))],
        out_specs=row(0), out_shape=jax.ShapeDtypeStruct((s, D_SGU), BF16), compiler_params=_params("parallel"),
    )(h, h, ln_g, ln_b, w_s, b_full)


def _sgu_bwd(h, ln_g, ln_b, w_s, b_full, dy, ublk, name):
    s = h.shape[0]
    tr = _tile(s, 512)
    nch = tr // SGU_CHUNK
    nt_dims = (((1,), (1,)), ((), ()))

    def norm(v, g, b):
        return _layer_norm(_gelu(v), g, b)

    def body(u_ref, v_ref, g_ref, b_ref, w_ref, bf_ref, dy_ref, du_ref, dv_ref, dg_ref, db_ref, dw_ref, dbf_ref):
        masks = _sgu_group_masks()
        tril = _sgu_tril()
        wf = [jnp.where(tril, w_ref[g], 0.0) for g in range(N_SGU_GROUPS)]
        wm = [w.astype(BF16) for w in wf]
        wmt = [jnp.transpose(w).astype(BF16) for w in wf]
        vn, vjp = jax.vjp(norm, v_ref[...], g_ref[...], b_ref[...])

        @pl.when(pl.program_id(0) == 0)
        def _():
            dg_ref[...] = jnp.zeros_like(dg_ref)
            db_ref[...] = jnp.zeros_like(db_ref)
            dw_ref[...] = jnp.zeros_like(dw_ref)
            dbf_ref[...] = jnp.zeros_like(dbf_ref)

        dvn_parts = []
        for ch in range(nch):
            rows = pl.ds(ch * SGU_CHUNK, SGU_CHUNK)
            vc = vn[ch * SGU_CHUNK:(ch + 1) * SGU_CHUNK, :]
            vc16 = vc.astype(BF16)
            mixed = bf_ref[...]
            for g in range(N_SGU_GROUPS):
                mixed = mixed + jnp.dot(wm[g], jnp.where(masks[g], vc, 0.0).astype(BF16), preferred_element_type=F32)
            dy = dy_ref[rows, :]
            ug, slope = _gelu_and_slope(u_ref[rows, :])
            du_ref[rows, :] = (dy * mixed * slope).astype(BF16)
            dmixed = dy * ug
            dbf_ref[...] += dmixed
            dvc = jnp.zeros((SGU_CHUNK, D_SGU), F32)
            for g in range(N_SGU_GROUPS):
                dm16 = jnp.where(masks[g], dmixed, 0.0).astype(BF16)
                dw_ref[g] += jnp.where(tril, lax.dot_general(dm16, vc16, nt_dims, preferred_element_type=F32), 0.0)
                dvc = dvc + jnp.dot(wmt[g], dm16, preferred_element_type=F32)
            dvn_parts.append(dvc)
        dv, dg, db = vjp(jnp.concatenate(dvn_parts, axis=0))
        dv_ref[...] = dv.astype(BF16)
        dg_ref[...] += dg
        db_ref[...] += db

    row = lambda blk: pl.BlockSpec((tr, D_SGU), lambda i: (i, blk))
    vec = pl.BlockSpec((1, D_SGU), lambda i: (0, 0))
    wsp = pl.BlockSpec((N_SGU_GROUPS, SGU_CHUNK, SGU_CHUNK), lambda i: (0, 0, 0))
    bsp = pl.BlockSpec((SGU_CHUNK, D_SGU), lambda i: (0, 0))
    return pl.pallas_call(
        body, name=name, grid=(s // tr,),
        in_specs=[row(ublk), row(ublk + 1), vec, vec, wsp, bsp, row(0)],
        out_specs=[row(0), row(0), vec, vec, wsp, bsp],
        out_shape=[jax.ShapeDtypeStruct((s, D_SGU), BF16), jax.ShapeDtypeStruct((s, D_SGU), BF16),
                   jax.ShapeDtypeStruct((1, D_SGU), F32), jax.ShapeDtypeStruct((1, D_SGU), F32),
                   jax.ShapeDtypeStruct((N_SGU_GROUPS, SGU_CHUNK, SGU_CHUNK), F32),
                   jax.ShapeDtypeStruct((SGU_CHUNK, D_SGU), F32)],
        compiler_params=_params("arbitrary"),
    )(h, h, ln_g, ln_b, w_s, b_full, dy)


def _merge_fwd(h, b_gate, att, yc, ys, wa, wc, ws, name):
    s, d = att.shape[0], wa.shape[1]
    tm, tn = _tile(s, 512), _tile(d, 512)
    nj = d // tn

    def body(g0_ref, g1_ref, g2_ref, bg_ref, a_ref, c_ref, s_ref, wa_ref, wc_ref, ws_ref, o_ref):
        acc = jax.nn.sigmoid(g0_ref[...] + bg_ref[0:1, :]) * jnp.dot(a_ref[...], wa_ref[...], preferred_element_type=F32)
        acc += jax.nn.sigmoid(g1_ref[...] + bg_ref[1:2, :]) * jnp.dot(c_ref[...], wc_ref[...], preferred_element_type=F32)
        acc += jax.nn.sigmoid(g2_ref[...] + bg_ref[2:3, :]) * jnp.dot(s_ref[...], ws_ref[...], preferred_element_type=F32)
        o_ref[...] = acc.astype(BF16)

    gate = lambda b: pl.BlockSpec((tm, tn), lambda j, i: (i, b * nj + j))
    act = lambda k: pl.BlockSpec((tm, k), lambda j, i: (i, 0))
    wgt = lambda k: pl.BlockSpec((k, tn), lambda j, i: (0, j))
    return pl.pallas_call(
        body, name=name, grid=(nj, s // tm),
        in_specs=[gate(0), gate(1), gate(2), pl.BlockSpec((3, tn), lambda j, i: (0, j)),
                  act(D_ATT), act(D_CONV), act(D_SGU), wgt(D_ATT), wgt(D_CONV), wgt(D_SGU)],
        out_specs=pl.BlockSpec((tm, tn), lambda j, i: (i, j)),
        out_shape=jax.ShapeDtypeStruct((s, d), BF16), compiler_params=_params("parallel", "arbitrary"),
    )(h, h, h, b_gate, att, yc, ys, wa, wc, ws)


def _merge_bwd(h, b_gate, att, yc, ys, wa, wc, ws, dm, name):
    s, d = att.shape[0], wa.shape[1]
    tm, tn = _tile(s, 512), _tile(d, 512)
    nj = d // tn

    def body(g0_ref, g1_ref, g2_ref, bg_ref, a_ref, c_ref, s_ref, wa_ref, wc_ref, ws_ref, dm_ref,
             dya_ref, dyc_ref, dys_ref, dgl_ref, dbg_ref):
        dm = dm_ref[...]
        sums = []
        for b, (g_ref, x_ref, w_ref, dy_ref) in enumerate(((g0_ref, a_ref, wa_ref, dya_ref), (g1_ref, c_ref, wc_ref, dyc_ref),
                                                          (g2_ref, s_ref, ws_ref, dys_ref))):
            gate = jax.nn.sigmoid(g_ref[...] + bg_ref[b:b + 1, :])
            y = jnp.dot(x_ref[...], w_ref[...], preferred_element_type=F32)
            dy_ref[...] = (dm * gate).astype(BF16)
            dgl = dm * y * gate * (1.0 - gate)
            dgl_ref[b] = dgl.astype(BF16)
            sums.append(jnp.sum(dgl, axis=0, keepdims=True))
        sub = lax.broadcasted_iota(jnp.int32, (3, tn), 0)
        part = jnp.where(sub == 0, sums[0], jnp.where(sub == 1, sums[1], sums[2]))

        @pl.when(pl.program_id(1) == 0)
        def _():
            dbg_ref[...] = jnp.zeros_like(dbg_ref)

        dbg_ref[...] += part

    gate = lambda b: pl.BlockSpec((tm, tn), lambda j, i: (i, b * nj + j))
    act = lambda k: pl.BlockSpec((tm, k), lambda j, i: (i, 0))
    wgt = lambda k: pl.BlockSpec((k, tn), lambda j, i: (0, j))
    tile = pl.BlockSpec((tm, tn), lambda j, i: (i, j))
    return pl.pallas_call(
        body, name=name, grid=(nj, s // tm),
        in_specs=[gate(0), gate(1), gate(2), pl.BlockSpec((3, tn), lambda j, i: (0, j)),
                  act(D_ATT), act(D_CONV), act(D_SGU), wgt(D_ATT), wgt(D_CONV), wgt(D_SGU), tile],
        out_specs=[tile, tile, tile, pl.BlockSpec((3, tm, tn), lambda j, i: (0, i, j)),
                   pl.BlockSpec((3, tn), lambda j, i: (0, j))],
        out_shape=[jax.ShapeDtypeStruct((s, d), BF16)] * 3 + [jax.ShapeDtypeStruct((3, s, d), BF16),
                                                              jax.ShapeDtypeStruct((3, d), F32)],
        compiler_params=_params("parallel", "arbitrary"),
    )(h, h, h, b_gate, att, yc, ys, wa, wc, ws, dm)


def _ffn_act_fwd(hh, cw, name):
    s, dff = hh.shape[0], hh.shape[1] // 2
    nblk = dff // LANES

    def body(a_ref, b_ref, wa_ref, wb_ref, z_ref, ca_ref, cb_ref):
        rows = lax.broadcasted_iota(jnp.int32, (s, LANES), 0)
        ca, cb = _conv3(a_ref[...], wa_ref, rows), _conv3(b_ref[...], wb_ref, rows)
        ca_ref[...] = ca
        cb_ref[...] = cb
        z_ref[...] = (_gelu(ca) * cb).astype(BF16)

    col = lambda base: pl.BlockSpec((s, LANES), lambda j: (0, base + j))
    wsp = lambda base: pl.BlockSpec((3, LANES), lambda j: (0, base + j))
    return pl.pallas_call(
        body, name=name, grid=(nblk,), in_specs=[col(0), col(nblk), wsp(0), wsp(nblk)], out_specs=[col(0)] * 3,
        out_shape=[jax.ShapeDtypeStruct((s, dff), BF16)] + [jax.ShapeDtypeStruct((s, dff), F32)] * 2,
        compiler_params=_params("parallel"),
    )(hh, hh, cw, cw)


def _ffn_act_bwd(hh, ca, cb, cw, dz, name):
    s, dff = hh.shape[0], hh.shape[1] // 2
    nblk = dff // LANES

    def body(a_ref, b_ref, ca_ref, cb_ref, wa_ref, wb_ref, dz_ref, da_ref, db_ref, dwa_ref, dwb_ref):
        rows = lax.broadcasted_iota(jnp.int32, (s, LANES), 0)
        dz = dz_ref[...]
        ga, slope = _gelu_and_slope(ca_ref[...])
        da, dwa_ref[...] = _conv3_bwd(dz * cb_ref[...] * slope, a_ref[...], wa_ref, rows)
        db, dwb_ref[...] = _conv3_bwd(dz * ga, b_ref[...], wb_ref, rows)
        da_ref[...] = da.astype(BF16)
        db_ref[...] = db.astype(BF16)

    col = lambda base: pl.BlockSpec((s, LANES), lambda j: (0, base + j))
    wsp = lambda base: pl.BlockSpec((3, LANES), lambda j: (0, base + j))
    w8 = lambda base: pl.BlockSpec((8, LANES), lambda j: (0, base + j))
    return pl.pallas_call(
        body, name=name, grid=(nblk,), in_specs=[col(0), col(nblk), col(0), col(0), wsp(0), wsp(nblk), col(0)],
        out_specs=[col(0), col(0), w8(0), w8(0)],
        out_shape=[jax.ShapeDtypeStruct((s, dff), BF16)] * 2 + [jax.ShapeDtypeStruct((8, dff), F32)] * 2,
        compiler_params=_params("parallel"),
    )(hh, hh, ca, cb, cw, cw, dz)


ANY = pl.BlockSpec(memory_space=pl.ANY)


def _place():
    return lax.axis_index("x"), lax.axis_index("y"), lax.axis_index("c")


def _all_gather(arrs, name):
    n = len(arrs)

    def body(*refs):
        ins, outs = refs[:n], refs[n:2 * n]
        send_sems, recv_sems, local_sems = refs[2 * n:]
        x, y, c = _place()
        me, sibling = (x, y, c), (x, y, 1 - c)
        chips = [(1 - x, y), (x, 1 - y), (1 - x, 1 - y)]

        def slab(a, dev):
            return outs[a].at[4 * dev[0] + 2 * dev[1] + dev[2]]

        def copy(a, k, block, to, src=None):
            return pltpu.make_async_remote_copy(
                src_ref=slab(a, block) if src is None else src, dst_ref=slab(a, block),
                send_sem=send_sems.at[7 * a + k], recv_sem=recv_sems.at[7 * a + k], device_id=to, device_id_type=MESH)

        mine = [pltpu.make_async_copy(ins[a], slab(a, me), local_sems.at[a]) for a in range(n)]
        for cp in mine:
            cp.start()
        first = []
        for a in range(n):
            first.append(copy(a, 0, me, sibling, src=ins[a]))
            first += [copy(a, 1 + j, me, (*chip, c), src=ins[a]) for j, chip in enumerate(chips)]
        for cp in first:
            cp.start()
        passed = []
        for a in range(n):
            for j, chip in enumerate(chips):
                copy(a, 1 + j, (*chip, c), me).wait_recv()
                fwd = copy(a, 4 + j, (*chip, c), sibling)
                fwd.start()
                passed.append(fwd)
        for a in range(n):
            copy(a, 0, sibling, me).wait_recv()
            for j, chip in enumerate(chips):
                copy(a, 4 + j, (*chip, 1 - c), me).wait_recv()
        for cp in first + passed:
            cp.wait_send()
        for cp in mine:
            cp.wait()

    return pl.pallas_call(
        body, name=name, in_specs=[ANY] * n, out_specs=[ANY] * n,
        out_shape=[jax.ShapeDtypeStruct((N_DEV,) + a.shape, a.dtype) for a in arrs],
        scratch_shapes=[pltpu.SemaphoreType.DMA((7 * n,)), pltpu.SemaphoreType.DMA((7 * n,)), pltpu.SemaphoreType.DMA((n,))],
    )(*arrs)


HBM = pl.BlockSpec(memory_space=pltpu.HBM)
SEM = pl.BlockSpec(memory_space=pltpu.SEMAPHORE)
EFFECT = pltpu.SideEffectType.DATAFLOW_SIDE_EFFECTING


def _slot(dev):
    return 4 * dev[0] + 2 * dev[1] + dev[2]


def _exchange_copies(src_refs, land_refs, send_sems, recv_sems, src_view, land_view):
    x, y, c = _place()
    me = (x, y, c)
    peers = [(1 - x if r & 4 else x, 1 - y if r & 2 else y, 1 - c if r & 1 else c) for r in range(1, N_DEV)]
    sends, lands = [], []
    for a, (src, land) in enumerate(zip(src_refs, land_refs)):
        for k, peer in enumerate(peers):
            sems = dict(send_sem=send_sems.at[7 * a + k], recv_sem=recv_sems.at[7 * a + k], device_id=peer,
                        device_id_type=MESH)
            sends.append(pltpu.make_async_remote_copy(src_ref=src_view(src, _slot(peer)),
                                                      dst_ref=land_view(land, _slot(me)), **sems))
            lands.append(pltpu.make_async_remote_copy(src_ref=src_view(src, _slot(me)),
                                                      dst_ref=land_view(land, _slot(peer)), **sems))
    return sends, lands


def _own_copies(src_refs, land_refs, local_sems, src_view, land_view):
    me = _slot(_place())
    return [pltpu.make_async_copy(src_view(src, me), land_view(land, me), local_sems.at[a])
            for a, (src, land) in enumerate(zip(src_refs, land_refs))]


def _exchange_start(srcs, lands, after, src_view, land_view, name):
    n = len(srcs)

    def body(*refs):
        src_refs, land_refs = refs[:n], refs[n:2 * n]
        send_sems, recv_sems, local_sems = refs[2 * n + 1:2 * n + 4]
        token = refs[-1]
        sends, _ = _exchange_copies(src_refs, land_refs, send_sems, recv_sems, src_view, land_view)
        for cp in sends + _own_copies(src_refs, land_refs, local_sems, src_view, land_view):
            cp.start()
        token[...] = jnp.zeros_like(token)

    thru = [pltpu.HBM(a.shape, a.dtype) for a in list(srcs) + list(lands)]
    outs = pl.pallas_call(
        body, name=name,
        out_shape=(pltpu.SemaphoreType.DMA((7 * n,)), pltpu.SemaphoreType.DMA((7 * n,)), pltpu.SemaphoreType.DMA((n,)),
                   *thru, jax.ShapeDtypeStruct((8, LANES), F32)),
        in_specs=[HBM] * (2 * n) + [ANY],
        out_specs=(SEM, SEM, SEM, *([HBM] * (2 * n)), pl.BlockSpec(memory_space=pltpu.VMEM)),
        input_output_aliases={i: 3 + i for i in range(2 * n)},
        compiler_params=pltpu.CompilerParams(has_side_effects=EFFECT),
    )(*[pltpu.with_memory_space_constraint(a, pltpu.HBM) for a in list(srcs) + list(lands)], after)
    return outs[:3], list(outs[3:3 + n]), list(outs[3 + n:3 + 2 * n]), outs[-1]


def _exchange_wait(sems, srcs, lands, after, src_view, land_view, name):
    n = len(srcs)

    def body(*refs):
        src_refs, land_refs = refs[:n], refs[n:2 * n]
        send_sems, recv_sems, local_sems = refs[2 * n:2 * n + 3]
        sends, landed = _exchange_copies(src_refs, land_refs, send_sems, recv_sems, src_view, land_view)
        for cp in sends:
            cp.wait_send()
        for cp in landed:
            cp.wait_recv()
        for cp in _own_copies(src_refs, land_refs, local_sems, src_view, land_view):
            cp.wait()

    after = after if isinstance(after, (list, tuple)) else [after]
    outs = pl.pallas_call(
        body, name=name, out_shape=[pltpu.HBM(a.shape, a.dtype) for a in list(srcs) + list(lands)],
        in_specs=[HBM] * (2 * n) + [SEM, SEM, SEM] + [ANY] * len(after), out_specs=[HBM] * (2 * n),
        input_output_aliases={i: i for i in range(2 * n)},
        compiler_params=pltpu.CompilerParams(has_side_effects=EFFECT),
    )(*srcs, *lands, *sems, *after)
    return list(outs[:n]), list(outs[n:])


ADAMW_BLOCK_BYTES = 1 << 19
PACK_ROWS = 256


def _adamw(slabs, w, m, v, name):
    nl, r, c = w.shape
    row_edges = [r] + [t for t in range(8, r, 8) if r % t == 0]
    col_edges = [c] + [t for t in range(LANES, c, LANES) if c % t == 0]
    fits = [(tr * tc, tc, tr) for tr in row_edges for tc in col_edges if tr * tc * 4 <= ADAMW_BLOCK_BYTES]
    _, tc, tr = max(fits) if fits else (0, min(col_edges), min(row_edges))

    def body(s_ref, w_ref, m_ref, v_ref, g_ref, d_ref, nm_ref, nv_ref):
        g = s_ref[0].astype(F32)
        for q in range(1, N_DEV):
            g = g + s_ref[q].astype(F32)
        m_new = ADAM_B1 * m_ref[...] + (1.0 - ADAM_B1) * g
        v_new = ADAM_B2 * v_ref[...] + (1.0 - ADAM_B2) * (g * g)
        m_hat = m_new / (1.0 - ADAM_B1 ** ADAM_STEP)
        v_hat = v_new / (1.0 - ADAM_B2 ** ADAM_STEP)
        g_ref[...] = g
        d_ref[...] = -ADAM_LR * (m_hat / (jnp.sqrt(v_hat) + ADAM_EPS) + ADAM_WD * w_ref[...])
        nm_ref[...] = m_new
        nv_ref[...] = v_new

    blk = pl.BlockSpec((None, tr, tc), lambda l, i, j: (l, i, j))
    return pl.pallas_call(
        body, name=name, grid=(nl, r // tr, c // tc),
        in_specs=[pl.BlockSpec((N_DEV, None, tr, tc), lambda l, i, j: (0, l, i, j)), blk, blk, blk],
        out_specs=[blk] * 4, out_shape=[jax.ShapeDtypeStruct(w.shape, F32)] * 4,
        compiler_params=_params("parallel", "parallel", "parallel"),
    )(slabs, w, m, v)


def _layout(d):
    off = {"gate": 0, "q": 3 * d}
    off["bg"] = off["q"] + 3 * D_ATT
    off["u"] = off["bg"] + 3 * D_CONV
    off["f"] = off["u"] + 2 * D_SGU
    width = -(-(off["f"] + LANES) // 512) * 512
    return off, width


def _pad_w_in(wt, d, token):
    off, width = _layout(d)
    nqkv, nrest = 3 * D_ATT, 3 * D_CONV + 2 * D_SGU
    pad = jnp.zeros((width - off["f"] - N_HEADS, wt.shape[1]), wt.dtype) + token[0, 0].astype(wt.dtype)
    return jnp.concatenate([wt[nqkv + N_HEADS + nrest:], wt[:nqkv], wt[nqkv + N_HEADS:nqkv + N_HEADS + nrest],
                            wt[nqkv:nqkv + N_HEADS], pad], axis=0)


def _unpad_w_in(wtp, d):
    off, _ = _layout(d)
    return jnp.concatenate([wtp[off["q"]:off["bg"]], wtp[off["f"]:off["f"] + N_HEADS], wtp[off["bg"]:off["f"]],
                            wtp[:off["q"]]], axis=0)


def _cols_from_slabs(g):
    return jnp.transpose(g, (1, 0, 2)).reshape(g.shape[1], N_DEV * g.shape[2])


def _cols_to_slabs(w):
    r, c = w.shape[0], w.shape[1] // N_DEV
    return jnp.transpose(w.reshape(r, N_DEV, c), (1, 0, 2))


def kernel(x, pre_mix_g, post_mix_g, pre_ffn_g, post_ffn_g, w_in, b_forget, b_gate, conv_mix_w, sgu_ln_g, sgu_ln_b, sgu_w, sgu_b, w_branch_att, w_branch_conv, w_branch_sgu, w_out, w_ffn_up, conv_ffn_w, w_ffn_down, loss_target, m_pre_mix_g, m_post_mix_g, m_pre_ffn_g, m_post_ffn_g, m_w_in, m_b_forget, m_b_gate, m_conv_mix_w, m_sgu_ln_g, m_sgu_ln_b, m_sgu_w, m_sgu_b, m_w_branch_att, m_w_branch_conv, m_w_branch_sgu, m_w_out, m_w_ffn_up, m_conv_ffn_w, m_w_ffn_down, v_pre_mix_g, v_post_mix_g, v_pre_ffn_g, v_post_ffn_g, v_w_in, v_b_forget, v_b_gate, v_conv_mix_w, v_sgu_ln_g, v_sgu_ln_b, v_sgu_w, v_sgu_b, v_w_branch_att, v_w_branch_conv, v_w_branch_sgu, v_w_out, v_w_ffn_up, v_conv_ffn_w, v_w_ffn_down):
    depth = w_in.shape[0]
    s, d = x.shape[1], x.shape[2]
    dff = w_ffn_down.shape[1] * N_DEV
    off, _ = _layout(d)
    qblk, bgblk, ublk, fblk = off["q"] // LANES, off["bg"] // LANES, off["u"] // D_SGU, off["f"] // LANES
    x0 = x.reshape(s, d)
    target = loss_target.reshape(s, d)
    ncm, ncf = conv_mix_w.shape[2], conv_ffn_w.shape[2]

    lo = d // N_DEV

    whole = lambda ref, slot: ref
    slab = lambda ref, slot: ref.at[slot]

    w_in_t, w_up_t = jnp.transpose(w_in, (0, 2, 1)), jnp.transpose(w_ffn_up, (0, 2, 1))

    def shards_of(l, part):
        if part == "mix":
            small = jnp.concatenate([b_gate[l], conv_mix_w[l], conv_ffn_w[l]], axis=1)
            return [w_in_t[l].astype(BF16), w_branch_att[l].astype(BF16), w_branch_conv[l].astype(BF16),
                    w_branch_sgu[l].astype(BF16), w_out[l].astype(BF16), small]
        return [w_up_t[l].astype(BF16), w_ffn_down[l].astype(BF16)]

    def gather_start(l, part, after):
        shards = shards_of(l, part)
        lands = [lax.empty((N_DEV,) + a.shape, a.dtype) for a in shards]
        return _exchange_start(shards, lands, after, whole, slab, name=f"gather_start_{part}_{l}")

    def gather_finish(l, part, started, after):
        sems, shards, lands, _ = started[part]
        shards, lands = _exchange_wait(sems, shards, lands, after, whole, slab, name=f"gather_wait_{part}_{l}")
        token = jnp.zeros((8, LANES), F32)
        if l + 1 < depth:
            started[part] = gather_start(l + 1, part, lands[0])
            token = started[part][3]
        return lands, token

    def bfull(l):
        return jnp.repeat(jnp.transpose(sgu_b[l]), HEAD_DIM, axis=1)

    def bf_pad(l):
        return jnp.pad(b_forget[l], (0, LANES - N_HEADS)).reshape(1, LANES)

    saved = []
    weights = []
    xin = x0
    xn = _prenorm(x0, pre_mix_g[0:1], name="prenorm_first")
    loss_acc = dy = None
    first = _all_gather(shards_of(0, "mix"), name="gather_first")
    started = {"ffn": gather_start(0, "ffn", first[0])}
    if depth > 1:
        started["mix"] = gather_start(1, "mix", started["ffn"][3])
    for l in range(depth):
        if l == 0:
            (g_in, g_a, g_c, g_s, g_o, g_small), token = first, started["mix" if depth > 1 else "ffn"][3]
        else:
            (g_in, g_a, g_c, g_s, g_o, g_small), token = gather_finish(l, "mix", started, xin)
        g_small = _cols_from_slabs(g_small).reshape(3, N_DEV, -1)
        w = dict(w_in=_pad_w_in(g_in.reshape(N_DEV * g_in.shape[1], d), d, token), wa=_cols_from_slabs(g_a),
                 wc=_cols_from_slabs(g_c), ws=_cols_from_slabs(g_s), w_out=g_o.reshape(d, d),
                 b_gate=g_small[:, :, :lo].reshape(3, d), cmw=g_small[:, :, lo:lo + ncm].reshape(3, D_CONV),
                 cfw=g_small[:, :, lo + ncm:].reshape(3, 2 * dff))
        weights.append(w)
        h = _mm(xn, w["w_in"], "nt", F32, name="proj_in")
        c, ct = _forget_prep(h, bf_pad(l), fblk, name="forget_prep")
        att, lse = _attention_fwd(h, c, ct, qblk, name="attention_fwd")
        yc = _sconv_fwd(h, w["cmw"], bgblk, name="sconv_fwd")
        ys = _sgu_fwd(h, sgu_ln_g[l:l + 1], sgu_ln_b[l:l + 1], sgu_w[l], bfull(l), ublk, name="sgu_fwd")
        merged = _merge_fwd(h, w["b_gate"], att, yc, ys, w["wa"], w["wc"], w["ws"], name="merge_fwd")
        o = _mm(merged, w["w_out"], "nn", F32, name="proj_out")
        (g_up, g_dn), token = gather_finish(l, "ffn", started, o)
        w["w_up"], w["w_dn"] = g_up.reshape(2 * dff, d), g_dn.reshape(dff, d)
        x1, xn2 = _postnorm(xin, o, post_mix_g[l:l + 1], pre_ffn_g[l:l + 1] + token[0, 0], name="postnorm_mix")
        hh = _mm(xn2, w["w_up"], "nt", F32, name="ffn_up")
        z, ca, cb = _ffn_act_fwd(hh, w["cfw"], name="ffn_act_fwd")
        f = _mm(z, w["w_dn"], "nn", F32, name="ffn_down")
        saved.append(dict(xin=xin, xn=xn, h=h, c=c, ct=ct, lse=lse, att=att, yc=yc, ys=ys, merged=merged, o=o, x1=x1,
                          xn2=xn2, hh=hh, ca=ca, cb=cb, z=z, f=f))
        if l + 1 < depth:
            xin, xn = _postnorm(x1, f, post_ffn_g[l:l + 1], pre_mix_g[l + 1:l + 2], name="postnorm_ffn")
        else:
            dy, loss_acc = _postnorm_loss(x1, f, post_ffn_g[l:l + 1], target, name="postnorm_loss")
    loss = lax.psum(loss_acc[0, 0] * (0.5 / d), ("x", "y", "c"))

    rep = {k: [None] * depth for k in ("pre_mix_g", "post_mix_g", "pre_ffn_g", "post_ffn_g", "b_forget", "sgu_ln_g",
                                       "sgu_ln_b", "sgu_w", "sgu_b")}
    nsmall = lo + ncm + ncf
    lands = {"win": [lax.empty((N_DEV, depth) + w_in_t.shape[1:], BF16)],
             "mid": [lax.empty((N_DEV, depth) + shp, dt) for shp, dt in (
                 (w_branch_att.shape[1:], BF16), (w_branch_conv.shape[1:], BF16), (w_branch_sgu.shape[1:], BF16),
                 (w_out.shape[1:], BF16), ((3, nsmall), F32))],
             "ffn": [lax.empty((N_DEV, depth) + shp, BF16) for shp in (w_up_t.shape[1:], w_ffn_down.shape[1:])]}
    scatters = {part: [None] * depth for part in lands}

    def scatter_start(l, part, sends, after):
        layer_slab = lambda ref, slot: ref.at[slot, l]
        sems, sends, lands[part], token = _exchange_start(sends, lands[part], after, slab, layer_slab,
                                                          name=f"scatter_start_{part}_{l}")
        scatters[part][l] = (sems, sends, layer_slab)
        return token

    def scatter_finish(part, after):
        for l in range(depth):
            sems, sends, layer_slab = scatters[part][l]
            _, lands[part] = _exchange_wait(sems, sends, lands[part], after, slab, layer_slab,
                                            name=f"scatter_wait_{part}_{l}")
        return lands[part]

    token = jnp.zeros((8, LANES), F32)
    dx = dy
    for l in reversed(range(depth)):
        w, a = weights[l], saved[l]
        df, rep["post_ffn_g"][l] = _postnorm_bwd(a["f"], post_ffn_g[l:l + 1] + token[0, 0], dx, name="postnorm_bwd")
        dz = _mm(df, w["w_dn"], "nt", F32, name="ffn_down_dx")
        g_dn = _mm(a["z"], df, "tn", BF16, name="ffn_down_dw")
        dha, dhb, dcwa, dcwb = _ffn_act_bwd(a["hh"], a["ca"], a["cb"], w["cfw"], dz, name="ffn_act_bwd")
        dhh = jnp.concatenate([dha, dhb], axis=1)
        dcfw = jnp.concatenate([dcwa[0:3], dcwb[0:3]], axis=1)
        g_up = _mm(dhh, a["xn2"], "tn", BF16, name="ffn_up_dw")
        token = scatter_start(l, "ffn", [g_up.reshape(N_DEV, 2 * dff // N_DEV, d), g_dn.reshape(N_DEV, dff // N_DEV, d)],
                              dz)
        dxn2 = _mm(dhh, w["w_up"], "nn", F32, name="ffn_up_dx", after=token)
        dx1, rep["pre_ffn_g"][l] = _prenorm_bwd(a["x1"], pre_ffn_g[l:l + 1], dxn2, dx, name="prenorm_bwd")
        do, rep["post_mix_g"][l] = _postnorm_bwd(a["o"], post_mix_g[l:l + 1], dx1, name="postnorm_bwd")
        dmerged = _mm(do, w["w_out"], "nt", F32, name="proj_out_dx")
        g_o = _mm(a["merged"], do, "tn", BF16, name="proj_out_dw")
        dya, dyc, dys, dgl, dbg = _merge_bwd(a["h"], w["b_gate"], a["att"], a["yc"], a["ys"], w["wa"], w["wc"], w["ws"],
                                             dmerged, name="merge_bwd")
        dconv = _mm(dyc, w["wc"], "nt", F32, name="branch_conv_dx")
        dsgu = _mm(dys, w["ws"], "nt", F32, name="branch_sgu_dx")
        g_a = _mm(a["att"], dya, "tn", BF16, name="branch_att_dw")
        g_c = _mm(a["yc"], dyc, "tn", BF16, name="branch_conv_dw")
        g_s = _mm(a["ys"], dys, "tn", BF16, name="branch_sgu_dw")
        dbgate, dcg, dhc, dcmw = _sconv_bwd(a["h"], w["cmw"], dconv, bgblk, name="sconv_bwd")
        sends = [_cols_to_slabs(g_a), _cols_to_slabs(g_c), _cols_to_slabs(g_s), g_o.reshape(N_DEV, d // N_DEV, d),
                 jnp.concatenate([_cols_to_slabs(dbg), _cols_to_slabs(dcmw[0:3]), _cols_to_slabs(dcfw)], axis=2)]
        token = scatter_start(l, "mid", sends, dx1)
        datt = _mm(dya, w["wa"], "nt", F32, name="branch_att_dx", after=token)
        dq, dk, dv, dct4, dcq4 = _attention_bwd(a["h"], a["c"], a["ct"], a["lse"], a["att"], datt, qblk,
                                                name="attention_bwd")
        dfl, dbf = _forget_prep_bwd(a["h"], bf_pad(l), dct4, dcq4, fblk, name="forget_prep_bwd")
        rep["b_forget"][l] = dbf[0, :N_HEADS]
        du, dvs, dlg, dlb, dsw, dbfull = _sgu_bwd(a["h"], sgu_ln_g[l:l + 1], sgu_ln_b[l:l + 1], sgu_w[l], bfull(l), dsgu,
                                                  ublk, name="sgu_bwd")
        rep["sgu_ln_g"][l], rep["sgu_ln_b"][l], rep["sgu_w"][l] = dlg, dlb, dsw
        rep["sgu_b"][l] = jnp.transpose(jnp.sum(dbfull.reshape(SGU_CHUNK, N_SGU_GROUPS, HEAD_DIM), axis=2))
        dh = jnp.concatenate([dgl[0], dgl[1], dgl[2], dq.astype(BF16), dk.astype(BF16), dv.astype(BF16), dbgate, dcg, dhc,
                              du, dvs, dfl.astype(BF16), jnp.zeros((s, w["w_in"].shape[0] - off["f"] - LANES), BF16)], axis=1)
        g_in = _mm(dh, a["xn"], "tn", BF16, name="proj_in_dw")
        token = scatter_start(l, "win", [_unpad_w_in(g_in, d).reshape(N_DEV, -1, d)], dx1)
        dxn = _mm(dh, w["w_in"], "nn", F32, name="proj_in_dx", after=token)
        dx, rep["pre_mix_g"][l] = _prenorm_bwd(a["xin"], pre_mix_g[l:l + 1], dxn, dx1, name="prenorm_bwd")

    outs = {}
    t3 = lambda arr: jnp.transpose(arr, (0, 2, 1))

    def update(name_, slabs, w_, m_, v_, transposed=False):
        if transposed:
            w_, m_, v_ = t3(w_), t3(m_), t3(v_)
        shp = w_.shape
        w3 = w_.reshape((shp[0], -1, shp[-1])) if w_.ndim >= 3 else w_.reshape((1,) + shp)
        res = _adamw(slabs.reshape((N_DEV,) + w3.shape), w3, m_.reshape(w3.shape), v_.reshape(w3.shape),
                     name="adamw_" + name_)
        outs[name_] = tuple(t3(t.reshape(shp)) if transposed else t.reshape(shp) for t in res)
        return res[0]

    rep_names = ("pre_mix_g", "post_mix_g", "pre_ffn_g", "post_ffn_g", "b_forget", "sgu_ln_g", "sgu_ln_b", "sgu_w", "sgu_b")
    rep_w = dict(pre_mix_g=(pre_mix_g, m_pre_mix_g, v_pre_mix_g), post_mix_g=(post_mix_g, m_post_mix_g, v_post_mix_g),
                 pre_ffn_g=(pre_ffn_g, m_pre_ffn_g, v_pre_ffn_g), post_ffn_g=(post_ffn_g, m_post_ffn_g, v_post_ffn_g),
                 b_forget=(b_forget, m_b_forget, v_b_forget), sgu_ln_g=(sgu_ln_g, m_sgu_ln_g, v_sgu_ln_g),
                 sgu_ln_b=(sgu_ln_b, m_sgu_ln_b, v_sgu_ln_b), sgu_w=(sgu_w, m_sgu_w, v_sgu_w), sgu_b=(sgu_b, m_sgu_b, v_sgu_b))

    def pack(parts):
        rows = [jnp.pad(p.reshape(-1), (0, -p.size % LANES)).reshape(-1, LANES) for p in parts]
        rows = jnp.concatenate(rows, axis=0)
        return jnp.pad(rows, ((0, -rows.shape[0] % PACK_ROWS), (0, 0)))

    part = pack([jnp.stack([g.reshape(rep_w[k][0].shape[1:]) for g in rep[k]]) for k in rep_names])
    small_sems, small_src, small_land, _ = _exchange_start([part], [lax.empty((N_DEV,) + part.shape, F32)], dx, whole, slab,
                                                            name="gather_small_start")

    got_up, got_dn = scatter_finish("ffn", dx)
    done = [update("w_ffn_up", got_up, w_ffn_up, m_w_ffn_up, v_w_ffn_up, transposed=True),
            update("w_ffn_down", got_dn, w_ffn_down, m_w_ffn_down, v_w_ffn_down)]

    _, (gathered,) = _exchange_wait(small_sems, small_src, small_land, done, whole, slab, name="gather_small_wait")
    packed = [pack([rep_w[k][i] for k in rep_names]) for i in range(3)]
    res = _adamw(gathered.reshape(N_DEV, 1, -1, LANES), *[p.reshape(1, -1, LANES) for p in packed], name="adamw_replicated")
    row = 0
    for k in rep_names:
        shp = rep_w[k][0].shape
        size = math.prod(shp)
        nrows = -(-size // LANES)
        outs[k] = tuple(t[0, row:row + nrows].reshape(-1)[:size].reshape(shp) for t in res)
        row += nrows

    got_a, got_c, got_s, got_o, small = scatter_finish("mid", res[0])
    done = [update("w_branch_att", got_a, w_branch_att, m_w_branch_att, v_w_branch_att),
            update("w_branch_conv", got_c, w_branch_conv, m_w_branch_conv, v_w_branch_conv),
            update("w_branch_sgu", got_s, w_branch_sgu, m_w_branch_sgu, v_w_branch_sgu),
            update("w_out", got_o, w_out, m_w_out, v_w_out),
            update("b_gate", small[..., :lo], b_gate, m_b_gate, v_b_gate),
            update("conv_mix_w", small[..., lo:lo + ncm], conv_mix_w, m_conv_mix_w, v_conv_mix_w),
            update("conv_ffn_w", small[..., lo + ncm:], conv_ffn_w, m_conv_ffn_w, v_conv_ffn_w)]
    (got_in,) = scatter_finish("win", done)
    update("w_in", got_in, w_in, m_w_in, v_w_in, transposed=True)

    order = ("pre_mix_g", "post_mix_g", "pre_ffn_g", "post_ffn_g", "w_in", "b_forget", "b_gate", "conv_mix_w", "sgu_ln_g",
             "sgu_ln_b", "sgu_w", "sgu_b", "w_branch_att", "w_branch_conv", "w_branch_sgu", "w_out", "w_ffn_up",
             "conv_ffn_w", "w_ffn_down")
    grad_x = dx.reshape(x.shape)
    return (loss, grad_x, *[outs[k][0] for k in order], *[outs[k][1] for k in order], *[outs[k][2] for k in order],
            *[outs[k][3] for k in order])
```

```python
import functools
import math

import jax
import jax.numpy as jnp
from jax import lax
from jax.experimental import pallas as pl
from jax.experimental.pallas import tpu as pltpu

F32 = jnp.float32
BF16 = jnp.bfloat16

N_DEV = 8
HEAD_DIM = 64
N_HEADS = 8
D_ATT = 512
D_CONV = 256
D_SGU = 256
N_SGU_GROUPS = 4
SGU_CHUNK = 128
RMS_EPS = 1e-6
LN_EPS = 1e-5
ADAM_LR = 0.001
ADAM_B1 = 0.9
ADAM_B2 = 0.999
ADAM_EPS = 1e-08
ADAM_WD = 0.01
ADAM_STEP = 10
LANES = 128
VMEM_LIMIT = 56 * 1024 * 1024
ATT_TILE = 512
LOG2E = math.log2(math.e)
NEG = -1e30
MESH = pl.DeviceIdType.MESH


def _params(*sem):
    return pltpu.CompilerParams(dimension_semantics=sem if sem else None, vmem_limit_bytes=VMEM_LIMIT)


def _tile(n, cap):
    if n <= cap:
        return n
    t = cap - cap % LANES
    while n % t:
        t -= LANES
    return t


def _gelu(x):
    return 0.5 * x * (1.0 + jnp.tanh(math.sqrt(2.0 / math.pi) * (x + 0.044715 * (x * x * x))))


def _gelu_and_slope(x):
    k0, k1 = math.sqrt(2.0 / math.pi), 0.044715
    x2 = x * x
    t = jnp.tanh(x * (k0 + (k0 * k1) * x2))
    half = 0.5 * (1.0 + t)
    return x * half, half + (0.5 * x) * (1.0 - t * t) * (k0 + (3.0 * k0 * k1) * x2)


def _rms(x, g):
    r = lax.rsqrt(jnp.mean(x * x, axis=-1, keepdims=True) + RMS_EPS)
    return x * r * g


def _layer_norm(x, g, b):
    mu = jnp.mean(x, axis=-1, keepdims=True)
    xc = x - mu
    var = jnp.mean(xc * xc, axis=-1, keepdims=True)
    return xc * lax.rsqrt(var + LN_EPS) * g + b


def _shift_down(x, k, rows):
    return jnp.where(rows >= k, pltpu.roll(x, k, 0), 0.0)


def _shift_up(x, k, rows):
    s = x.shape[0]
    return jnp.where(rows < s - k, pltpu.roll(x, s - k, 0), 0.0)


def _conv3(x, w_ref, rows):
    return w_ref[2:3, :] * x + w_ref[1:2, :] * _shift_down(x, 1, rows) + w_ref[0:1, :] * _shift_down(x, 2, rows)


def _conv3_bwd(dy, x, w_ref, rows):
    up1, up2 = _shift_up(dy, 1, rows), _shift_up(dy, 2, rows)
    dx = w_ref[2:3, :] * dy + w_ref[1:2, :] * up1 + w_ref[0:1, :] * up2
    d2 = jnp.sum(dy * x, axis=0, keepdims=True)
    d1 = jnp.sum(up1 * x, axis=0, keepdims=True)
    d0 = jnp.sum(up2 * x, axis=0, keepdims=True)
    sub = lax.broadcasted_iota(jnp.int32, (8, x.shape[1]), 0)
    return dx, jnp.where(sub == 0, d0, jnp.where(sub == 1, d1, jnp.where(sub == 2, d2, 0.0)))


MM_VMEM_BUDGET = 40 * 1024 * 1024
MM_TILE_CAP = 1408


def _mm_tiles(m, n, k, out_bytes):
    def edges(d):
        return [t for t in range(LANES, min(d, MM_TILE_CAP) + 1, LANES) if d % t == 0] or [d]

    best = None
    for tm in edges(m):
        for tn in edges(n):
            if 2 * (2 * k * (tm + tn) + tm * tn * out_bytes) > MM_VMEM_BUDGET:
                continue
            for a_outer in (True, False):
                reads = k * m + (m // tm) * k * n if a_outer else k * n + (n // tn) * k * m
                traffic = 2 * reads + m * n * out_bytes
                key = (traffic, -tm * tn)
                if best is None or key < best[0]:
                    best = (key, (tm, tn, a_outer))
    return best[1]


def _mm(a, b, form, out_dtype, name, after=None):
    if form == "nn":
        (m, k), n = a.shape, b.shape[1]
    elif form == "nt":
        (m, k), n = a.shape, b.shape[0]
    else:
        (k, m), n = a.shape, b.shape[1]
    tm, tn, a_outer = _mm_tiles(m, n, k, jnp.dtype(out_dtype).itemsize)
    dims = {"nn": (((1,), (0,)), ((), ())), "nt": (((1,), (1,)), ((), ())), "tn": (((0,), (0,)), ((), ()))}[form]

    def body(a_ref, b_ref, *rest):
        o_ref = rest[-1]
        o_ref[...] = lax.dot_general(a_ref[...], b_ref[...], dims, preferred_element_type=F32).astype(o_ref.dtype)

    ij = (lambda g0, g1: (g0, g1)) if a_outer else (lambda g0, g1: (g1, g0))
    a_spec = (pl.BlockSpec((k, tm), lambda g0, g1: (0, ij(g0, g1)[0])) if form == "tn"
              else pl.BlockSpec((tm, k), lambda g0, g1: (ij(g0, g1)[0], 0)))
    b_spec = (pl.BlockSpec((tn, k), lambda g0, g1: (ij(g0, g1)[1], 0)) if form == "nt"
              else pl.BlockSpec((k, tn), lambda g0, g1: (0, ij(g0, g1)[1])))
    extra = [] if after is None else [pl.BlockSpec((8, LANES), lambda g0, g1: (0, 0))]
    return pl.pallas_call(
        body, name=name, grid=(m // tm, n // tn) if a_outer else (n // tn, m // tm),
        in_specs=[a_spec, b_spec] + extra, out_specs=pl.BlockSpec((tm, tn), lambda g0, g1: ij(g0, g1)),
        out_shape=jax.ShapeDtypeStruct((m, n), out_dtype),
        compiler_params=_params("parallel", "arbitrary"),
    )(a, b, *([] if after is None else [after]))


def _mm_prenorm_bwd(a, b, x, g, dres, after, name):
    s, k = a.shape
    d = b.shape[1]
    tm = _tile(s, 512)

    def body(a_ref, b_ref, x_ref, g_ref, dres_ref, after_ref, dx_ref, dg_ref):
        dxn = jnp.dot(a_ref[...], b_ref[...], preferred_element_type=F32)
        _, vjp = jax.vjp(_rms, x_ref[...], g_ref[...])
        dx, dg = vjp(dxn)
        dx_ref[...] = dres_ref[...] + dx

        @pl.when(pl.program_id(0) == 0)
        def _():
            dg_ref[...] = jnp.zeros_like(dg_ref)

        dg_ref[...] += dg

    row = pl.BlockSpec((tm, d), lambda i: (i, 0))
    vec = pl.BlockSpec((1, d), lambda i: (0, 0))
    return pl.pallas_call(
        body, name=name, grid=(s // tm,),
        in_specs=[pl.BlockSpec((tm, k), lambda i: (i, 0)),
                  pl.BlockSpec((k, d), lambda i: (0, 0), pipeline_mode=pl.Buffered(1)),
                  row, vec, row, pl.BlockSpec((8, LANES), lambda i: (0, 0))],
        out_specs=[row, vec], out_shape=[jax.ShapeDtypeStruct((s, d), F32), jax.ShapeDtypeStruct((1, d), F32)],
        compiler_params=_params("arbitrary"),
    )(a, b, x, g, dres, after)


def _prenorm(x, g, name):
    s, d = x.shape
    tm = _tile(s, 512)

    def body(x_ref, g_ref, o_ref):
        o_ref[...] = _rms(x_ref[...], g_ref[...]).astype(BF16)

    return pl.pallas_call(
        body, name=name, grid=(s // tm,),
        in_specs=[pl.BlockSpec((tm, d), lambda i: (i, 0)), pl.BlockSpec((1, d), lambda i: (0, 0))],
        out_specs=pl.BlockSpec((tm, d), lambda i: (i, 0)),
        out_shape=jax.ShapeDtypeStruct((s, d), BF16), compiler_params=_params("parallel"),
    )(x, g)


def _postnorm(x, o, g_post, g_next, name):
    s, d = x.shape
    tm = _tile(s, 512)

    def body(x_ref, o_ref, gp_ref, gn_ref, x1_ref, xn_ref):
        x1 = x_ref[...] + _rms(o_ref[...], gp_ref[...])
        x1_ref[...] = x1
        xn_ref[...] = _rms(x1, gn_ref[...]).astype(BF16)

    row = pl.BlockSpec((tm, d), lambda i: (i, 0))
    vec = pl.BlockSpec((1, d), lambda i: (0, 0))
    return pl.pallas_call(
        body, name=name, grid=(s // tm,), in_specs=[row, row, vec, vec], out_specs=[row, row],
        out_shape=[jax.ShapeDtypeStruct((s, d), F32), jax.ShapeDtypeStruct((s, d), BF16)],
        compiler_params=_params("parallel"),
    )(x, o, g_post, g_next)


def _postnorm_loss(x, o, g_post, target, name):
    s, d = x.shape
    tm = _tile(s, 512)

    def body(x_ref, o_ref, gp_ref, t_ref, dy_ref, acc_ref):
        e = x_ref[...] + _rms(o_ref[...], gp_ref[...]) - t_ref[...]
        dy_ref[...] = e / d

        @pl.when(pl.program_id(0) == 0)
        def _():
            acc_ref[...] = jnp.zeros_like(acc_ref)

        acc_ref[...] += jnp.sum(jnp.sum(e * e, axis=1, keepdims=True), axis=0, keepdims=True)

    row = pl.BlockSpec((tm, d), lambda i: (i, 0))
    return pl.pallas_call(
        body, name=name, grid=(s // tm,),
        in_specs=[row, row, pl.BlockSpec((1, d), lambda i: (0, 0)), row],
        out_specs=[row, pl.BlockSpec((1, LANES), lambda i: (0, 0))],
        out_shape=[jax.ShapeDtypeStruct((s, d), F32), jax.ShapeDtypeStruct((1, LANES), F32)],
        compiler_params=_params("arbitrary"),
    )(x, o, g_post, target)


def _postnorm_bwd(o, g, dx, name):
    s, d = o.shape
    tm = _tile(s, 512)

    def body(o_ref, g_ref, dx_ref, do_ref, dg_ref):
        _, vjp = jax.vjp(_rms, o_ref[...], g_ref[...])
        d_o, dg = vjp(dx_ref[...])
        do_ref[...] = d_o.astype(BF16)

        @pl.when(pl.program_id(0) == 0)
        def _():
            dg_ref[...] = jnp.zeros_like(dg_ref)

        dg_ref[...] += dg

    row = pl.BlockSpec((tm, d), lambda i: (i, 0))
    vec = pl.BlockSpec((1, d), lambda i: (0, 0))
    return pl.pallas_call(
        body, name=name, grid=(s // tm,), in_specs=[row, vec, row], out_specs=[row, vec],
        out_shape=[jax.ShapeDtypeStruct((s, d), BF16), jax.ShapeDtypeStruct((1, d), F32)],
        compiler_params=_params("arbitrary"),
    )(o, g, dx)


def _prenorm_bwd(x, g, dxn, dres, name):
    s, d = x.shape
    tm = _tile(s, 512)

    def body(x_ref, g_ref, dxn_ref, dres_ref, dx_ref, dg_ref):
        _, vjp = jax.vjp(_rms, x_ref[...], g_ref[...])
        dx, dg = vjp(dxn_ref[...])
        dx_ref[...] = dres_ref[...] + dx

        @pl.when(pl.program_id(0) == 0)
        def _():
            dg_ref[...] = jnp.zeros_like(dg_ref)

        dg_ref[...] += dg

    row = pl.BlockSpec((tm, d), lambda i: (i, 0))
    vec = pl.BlockSpec((1, d), lambda i: (0, 0))
    return pl.pallas_call(
        body, name=name, grid=(s // tm,), in_specs=[row, vec, row, row], out_specs=[row, vec],
        out_shape=[jax.ShapeDtypeStruct((s, d), F32), jax.ShapeDtypeStruct((1, d), F32)],
        compiler_params=_params("arbitrary"),
    )(x, g, dxn, dres)


def _log_sigmoid(z):
    return jnp.minimum(z, 0.0) - jnp.log(1.0 + jnp.exp(-jnp.abs(z)))


def _forget_prep(h, bf_pad, fblk, name):
    s = h.shape[0]

    def body(f_ref, b_ref, c_ref, ct_ref):
        c = _log_sigmoid(f_ref[...] + b_ref[...])
        rows = lax.broadcasted_iota(jnp.int32, c.shape, 0)
        k = 1
        while k < s:
            c = c + _shift_down(c, k, rows)
            k *= 2
        c_ref[...] = c
        ct_ref[...] = jnp.transpose(c)[0:8, :]

    return pl.pallas_call(
        body, name=name, grid=(1,),
        in_specs=[pl.BlockSpec((s, LANES), lambda i: (0, fblk)), pl.BlockSpec((1, LANES), lambda i: (0, 0))],
        out_specs=[pl.BlockSpec((s, LANES), lambda i: (0, 0)), pl.BlockSpec((8, s), lambda i: (0, 0))],
        out_shape=[jax.ShapeDtypeStruct((s, LANES), F32), jax.ShapeDtypeStruct((8, s), F32)],
        compiler_params=_params("arbitrary"),
    )(h, bf_pad)


def _forget_prep_bwd(h, bf_pad, dct, dcq, fblk, name):
    s = h.shape[0]
    pairs = N_HEADS // 2

    def body(f_ref, b_ref, dct_ref, dcq_ref, df_ref, db_ref):
        dct = dct_ref[0]
        dcq = dcq_ref[0]
        for p in range(1, pairs):
            dct = dct + pltpu.roll(dct_ref[p], 2 * p, 0)
            dcq = dcq + pltpu.roll(dcq_ref[p], 2 * p, 1)
        dc = dcq + jnp.transpose(jnp.concatenate([dct, jnp.zeros((LANES - 8, s), F32)], axis=0))
        rows = lax.broadcasted_iota(jnp.int32, dc.shape, 0)
        k = 1
        while k < s:
            dc = dc + _shift_up(dc, k, rows)
            k *= 2
        z = f_ref[...] + b_ref[...]
        lane = lax.broadcasted_iota(jnp.int32, dc.shape, 1)
        df = jnp.where(lane < N_HEADS, dc * jax.nn.sigmoid(-z), 0.0)
        df_ref[...] = df
        db_ref[...] = jnp.sum(df, axis=0, keepdims=True)

    return pl.pallas_call(
        body, name=name, grid=(1,),
        in_specs=[pl.BlockSpec((s, LANES), lambda i: (0, fblk)), pl.BlockSpec((1, LANES), lambda i: (0, 0)),
                  pl.BlockSpec((pairs, 8, s), lambda i: (0, 0, 0)), pl.BlockSpec((pairs, s, LANES), lambda i: (0, 0, 0))],
        out_specs=[pl.BlockSpec((s, LANES), lambda i: (0, 0)), pl.BlockSpec((1, LANES), lambda i: (0, 0))],
        out_shape=[jax.ShapeDtypeStruct((s, LANES), F32), jax.ShapeDtypeStruct((1, LANES), F32)],
        compiler_params=_params("arbitrary"),
    )(h, bf_pad, dct, dcq)


def _pick_lane(blk, idx):
    lane = lax.broadcasted_iota(jnp.int32, blk.shape, 1)
    return jnp.sum(jnp.where(lane == idx, blk, 0.0), axis=1, keepdims=True)


def _pick_row(blk, idx):
    sub = lax.broadcasted_iota(jnp.int32, blk.shape, 0)
    return jnp.sum(jnp.where(sub == idx, blk, 0.0), axis=0, keepdims=True)


def _attention_fwd(h, c, ct, qblk, name):
    s = h.shape[0]
    t = _tile(s, ATT_TILE)
    nq = s // t
    scale = HEAD_DIM ** -0.5
    nt_dims = (((1,), (1,)), ((), ()))

    def body(q_ref, k_ref, v_ref, c_ref, ct_ref, o_ref, lse_ref):
        p = pl.program_id(0)
        i = pl.program_id(1)
        lane = lax.broadcasted_iota(jnp.int32, (1, LANES), 1)
        first = lane < HEAD_DIM
        q = q_ref[...] * (scale * LOG2E)
        qa = jnp.where(first, q, 0.0).astype(BF16)
        qb = jnp.where(first, 0.0, q).astype(BF16)
        cblk = c_ref[...]
        cta = _pick_lane(cblk, 2 * p) * LOG2E
        ctb = _pick_lane(cblk, 2 * p + 1) * LOG2E

        def step(j, carry, diagonal):
            ma, la, mb, lb, acc = carry
            off = pl.multiple_of(j * t, t)
            k = k_ref[pl.ds(off, t), :].astype(BF16)
            v = v_ref[pl.ds(off, t), :].astype(BF16)
            crow = ct_ref[:, pl.ds(off, t)] * LOG2E

            def one(qh, cth, hd, m_old, l_old):
                sc = lax.dot_general(qh, k, nt_dims, preferred_element_type=F32) - _pick_row(crow, hd)
                if diagonal:
                    keep = lax.broadcasted_iota(jnp.int32, (t, t), 0) >= lax.broadcasted_iota(jnp.int32, (t, t), 1)
                    sc = jnp.where(keep, sc, NEG)
                m_new = jnp.maximum(m_old, jnp.max(sc, axis=1, keepdims=True) + cth)
                pr = jnp.exp2(sc - (m_new - cth))
                alpha = jnp.exp2(m_old - m_new)
                l_new = alpha * l_old + jnp.sum(pr, axis=1, keepdims=True)
                pv = jnp.dot(pr.astype(BF16), v, preferred_element_type=F32)
                return m_new, l_new, alpha, pv

            ma2, la2, aa, pva = one(qa, cta, 2 * p, ma, la)
            mb2, lb2, ab, pvb = one(qb, ctb, 2 * p + 1, mb, lb)
            acc = jnp.where(first, aa * acc + pva, ab * acc + pvb)
            return ma2, la2, mb2, lb2, acc

        init = (jnp.full((t, 1), NEG, F32), jnp.zeros((t, 1), F32), jnp.full((t, 1), NEG, F32),
                jnp.zeros((t, 1), F32), jnp.zeros((t, LANES), F32))
        carry = lax.fori_loop(0, i, lambda j, carry: step(j, carry, False), init)
        ma, la, mb, lb, acc = step(i, carry, True)
        o_ref[...] = (acc / jnp.where(first, la, lb)).astype(BF16)
        lse_ref[0] = jnp.broadcast_to(ma + jnp.log2(la), (t, LANES))
        lse_ref[1] = jnp.broadcast_to(mb + jnp.log2(lb), (t, LANES))

    return pl.pallas_call(
        body, name=name, grid=(N_HEADS // 2, nq),
        in_specs=[pl.BlockSpec((t, LANES), lambda p, i: (i, qblk + p)),
                  pl.BlockSpec((s, LANES), lambda p, i: (0, qblk + 4 + p)),
                  pl.BlockSpec((s, LANES), lambda p, i: (0, qblk + 8 + p)),
                  pl.BlockSpec((t, LANES), lambda p, i: (i, 0)),
                  pl.BlockSpec((8, s), lambda p, i: (0, 0))],
        out_specs=[pl.BlockSpec((t, LANES), lambda p, i: (i, p)),
                   pl.BlockSpec((2, t, LANES), lambda p, i: (p, i, 0))],
        out_shape=[jax.ShapeDtypeStruct((s, D_ATT), BF16), jax.ShapeDtypeStruct((N_HEADS, s, LANES), F32)],
        compiler_params=_params("parallel", "arbitrary"),
    )(h, h, h, c, ct)


def _attention_bwd(h, c, ct, lse, att, datt, qblk, name):
    s = h.shape[0]
    t = _tile(s, ATT_TILE)
    nq = s // t
    scale = HEAD_DIM ** -0.5
    nt_dims = (((1,), (1,)), ((), ()))
    tn_dims = (((0,), (0,)), ((), ()))

    def body(q_ref, k_ref, v_ref, c_ref, ct_ref, lse_ref, o_ref, do_ref, dq_ref, dk_ref, dv_ref, dct_ref, dcq_ref):
        p = pl.program_id(0)
        j = pl.program_id(1)
        lane = lax.broadcasted_iota(jnp.int32, (1, LANES), 1)
        first = lane < HEAD_DIM
        kf = k_ref[...]
        vf = v_ref[...]
        k = kf.astype(BF16)
        ka = jnp.where(first, kf, 0.0).astype(BF16)
        kb = jnp.where(first, 0.0, kf).astype(BF16)
        va = jnp.where(first, vf, 0.0).astype(BF16)
        vb = jnp.where(first, 0.0, vf).astype(BF16)
        crow = ct_ref[...] * LOG2E
        csa = _pick_row(crow, 2 * p)
        csb = _pick_row(crow, 2 * p + 1)

        @pl.when(j == 0)
        def _():
            dq_ref[...] = jnp.zeros_like(dq_ref)
            dcq_ref[...] = jnp.zeros_like(dcq_ref)

        def step(i, carry, diagonal):
            dka, dkb, dva, dvb, dca, dcb = carry
            off = pl.multiple_of(i * t, t)
            rows = pl.ds(off, t)
            q = (q_ref[rows, :] * (scale * LOG2E)).astype(BF16)
            dof = do_ref[rows, :]
            do = dof.astype(BF16)
            prod = dof * o_ref[rows, :].astype(F32)
            cblk = c_ref[rows, :] * LOG2E

            def one(kh, vh, hd, csh, lse_h):
                sc = lax.dot_general(q, kh, nt_dims, preferred_element_type=F32) - csh
                if diagonal:
                    keep = lax.broadcasted_iota(jnp.int32, (t, t), 0) >= lax.broadcasted_iota(jnp.int32, (t, t), 1)
                    sc = jnp.where(keep, sc, NEG)
                pr = jnp.exp2(sc - (jnp.max(lse_h, axis=1, keepdims=True) - _pick_lane(cblk, hd)))
                dp = lax.dot_general(do, vh, nt_dims, preferred_element_type=F32)
                return pr, dp

            pra, dpa = one(ka, va, 2 * p, csa, lse_ref[0, rows, :])
            prb, dpb = one(kb, vb, 2 * p + 1, csb, lse_ref[1, rows, :])
            dela = jnp.sum(jnp.where(first, prod, 0.0), axis=1, keepdims=True)
            delb = jnp.sum(jnp.where(first, 0.0, prod), axis=1, keepdims=True)
            dsa = pra * (dpa - dela)
            dsb = prb * (dpb - delb)
            dsa16 = dsa.astype(BF16)
            dsb16 = dsb.astype(BF16)
            dva = dva + lax.dot_general(pra.astype(BF16), do, tn_dims, preferred_element_type=F32)
            dvb = dvb + lax.dot_general(prb.astype(BF16), do, tn_dims, preferred_element_type=F32)
            dka = dka + lax.dot_general(dsa16, q, tn_dims, preferred_element_type=F32)
            dkb = dkb + lax.dot_general(dsb16, q, tn_dims, preferred_element_type=F32)
            dqa = jnp.dot(dsa16, k, preferred_element_type=F32)
            dqb = jnp.dot(dsb16, k, preferred_element_type=F32)
            dq_ref[rows, :] += scale * jnp.where(first, dqa, dqb)
            dca = dca - jnp.sum(dsa, axis=0, keepdims=True)
            dcb = dcb - jnp.sum(dsb, axis=0, keepdims=True)
            dcq_ref[rows, :] += jnp.where(lane == 0, jnp.sum(dsa, axis=1, keepdims=True),
                                          jnp.where(lane == 1, jnp.sum(dsb, axis=1, keepdims=True), 0.0))
            return dka, dkb, dva, dvb, dca, dcb

        z = jnp.zeros((t, LANES), F32)
        zr = jnp.zeros((1, t), F32)
        carry = step(j, (z, z, z, z, zr, zr), True)
        dka, dkb, dva, dvb, dca, dcb = lax.fori_loop(j + 1, nq, lambda i, carry: step(i, carry, False), carry)
        dk_ref[...] = jnp.where(first, dka, dkb) * (1.0 / LOG2E)
        dv_ref[...] = jnp.where(first, dva, dvb)
        sub = lax.broadcasted_iota(jnp.int32, (8, t), 0)
        dct_ref[...] = jnp.where(sub == 0, dca, jnp.where(sub == 1, dcb, 0.0))

    full = lambda blk: pl.BlockSpec((s, LANES), blk)
    return pl.pallas_call(
        body, name=name, grid=(N_HEADS // 2, nq),
        in_specs=[full(lambda p, j: (0, qblk + p)),
                  pl.BlockSpec((t, LANES), lambda p, j: (j, qblk + 4 + p)),
                  pl.BlockSpec((t, LANES), lambda p, j: (j, qblk + 8 + p)),
                  full(lambda p, j: (0, 0)),
                  pl.BlockSpec((8, t), lambda p, j: (0, j)),
                  pl.BlockSpec((2, s, LANES), lambda p, j: (p, 0, 0)),
                  full(lambda p, j: (0, p)),
                  full(lambda p, j: (0, p))],
        out_specs=[full(lambda p, j: (0, p)),
                   pl.BlockSpec((t, LANES), lambda p, j: (j, p)),
                   pl.BlockSpec((t, LANES), lambda p, j: (j, p)),
                   pl.BlockSpec((None, 8, t), lambda p, j: (p, 0, j)),
                   pl.BlockSpec((None, s, LANES), lambda p, j: (p, 0, 0))],
        out_shape=[jax.ShapeDtypeStruct((s, D_ATT), F32), jax.ShapeDtypeStruct((s, D_ATT), F32),
                   jax.ShapeDtypeStruct((s, D_ATT), F32), jax.ShapeDtypeStruct((N_HEADS // 2, 8, s), F32),
                   jax.ShapeDtypeStruct((N_HEADS // 2, s, LANES), F32)],
        compiler_params=_params("arbitrary", "arbitrary"),
    )(h, h, h, c, ct, lse, att, datt)


def _sconv_fwd(h, w, bgblk, name):
    s = h.shape[0]
    nblk = D_CONV // LANES

    def body(bg_ref, cg_ref, hc_ref, w_ref, y_ref):
        rows = lax.broadcasted_iota(jnp.int32, (s, LANES), 0)
        y_ref[...] = (bg_ref[...] * _conv3(cg_ref[...] * hc_ref[...], w_ref, rows)).astype(BF16)

    col = lambda base: pl.BlockSpec((s, LANES), lambda j: (0, base + j))
    return pl.pallas_call(
        body, name=name, grid=(nblk,),
        in_specs=[col(bgblk), col(bgblk + nblk), col(bgblk + 2 * nblk), pl.BlockSpec((3, LANES), lambda j: (0, j))],
        out_specs=pl.BlockSpec((s, LANES), lambda j: (0, j)),
        out_shape=jax.ShapeDtypeStruct((s, D_CONV), BF16), compiler_params=_params("parallel"),
    )(h, h, h, w)


def _sconv_bwd(h, w, dy, bgblk, name):
    s = h.shape[0]
    nblk = D_CONV // LANES

    def body(bg_ref, cg_ref, hc_ref, w_ref, dy_ref, dbg_ref, dcg_ref, dhc_ref, dw_ref):
        rows = lax.broadcasted_iota(jnp.int32, (s, LANES), 0)
        cg, hc, dy, w = cg_ref[...], hc_ref[...], dy_ref[...], w_ref
        xin = cg * hc
        dbg_ref[...] = (dy * _conv3(xin, w, rows)).astype(BF16)
        dxin, dw_ref[...] = _conv3_bwd(dy * bg_ref[...], xin, w, rows)
        dcg_ref[...] = (dxin * hc).astype(BF16)
        dhc_ref[...] = (dxin * cg).astype(BF16)

    col = lambda base: pl.BlockSpec((s, LANES), lambda j: (0, base + j))
    return pl.pallas_call(
        body, name=name, grid=(nblk,),
        in_specs=[col(bgblk), col(bgblk + nblk), col(bgblk + 2 * nblk), pl.BlockSpec((3, LANES), lambda j: (0, j)), col(0)],
        out_specs=[col(0), col(0), col(0), pl.BlockSpec((8, LANES), lambda j: (0, j))],
        out_shape=[jax.ShapeDtypeStruct((s, D_CONV), BF16)] * 3 + [jax.ShapeDtypeStruct((8, D_CONV), F32)],
        compiler_params=_params("parallel"),
    )(h, h, h, w, dy)


def _sgu_group_masks():
    lane = lax.broadcasted_iota(jnp.int32, (1, D_SGU), 1)
    return [(lane // HEAD_DIM) == g for g in range(N_SGU_GROUPS)]


def _sgu_tril():
    r = lax.broadcasted_iota(jnp.int32, (SGU_CHUNK, SGU_CHUNK), 0)
    c = lax.broadcasted_iota(jnp.int32, (SGU_CHUNK, SGU_CHUNK), 1)
    return r >= c


def _sgu_fwd(h, ln_g, ln_b, w_s, b_full, ublk, name):
    s = h.shape[0]
    tr = _tile(s, 512)
    nch = tr // SGU_CHUNK

    def body(u_ref, v_ref, g_ref, b_ref, w_ref, bf_ref, y_ref):
        masks = _sgu_group_masks()
        tril = _sgu_tril()
        wm = [jnp.where(tril, w_ref[g], 0.0).astype(BF16) for g in range(N_SGU_GROUPS)]
        vn = _layer_norm(_gelu(v_ref[...]), g_ref[...], b_ref[...])
        for ch in range(nch):
            rows = pl.ds(ch * SGU_CHUNK, SGU_CHUNK)
            vc = vn[ch * SGU_CHUNK:(ch + 1) * SGU_CHUNK, :]
            mixed = bf_ref[...]
            for g in range(N_SGU_GROUPS):
                mixed = mixed + jnp.dot(wm[g], jnp.where(masks[g], vc, 0.0).astype(BF16), preferred_element_type=F32)
            y_ref[rows, :] = (_gelu(u_ref[rows, :]) * mixed).astype(BF16)

    row = lambda blk: pl.BlockSpec((tr, D_SGU), lambda i: (i, blk))
    vec = pl.BlockSpec((1, D_SGU), lambda i: (0, 0))
    return pl.pallas_call(
        body, name=name, grid=(s // tr,),
        in_specs=[row(ublk), row(ublk + 1), vec, vec,
                  pl.BlockSpec((N_SGU_GROUPS, SGU_CHUNK, SGU_CHUNK), lambda i: (0, 0, 0)),
                  pl.BlockSpec((SGU_CHUNK, D_SGU), lambda i: (0, 0))],
        out_specs=row(0), out_shape=jax.ShapeDtypeStruct((s, D_SGU), BF16), compiler_params=_params("parallel"),
    )(h, h, ln_g, ln_b, w_s, b_full)


def _sgu_bwd(h, ln_g, ln_b, w_s, b_full, dy, ublk, name):
    s = h.shape[0]
    tr = _tile(s, 512)
    nch = tr // SGU_CHUNK
    nt_dims = (((1,), (1,)), ((), ()))

    def norm(v, g, b):
        return _layer_norm(_gelu(v), g, b)

    def body(u_ref, v_ref, g_ref, b_ref, w_ref, bf_ref, dy_ref, du_ref, dv_ref, dg_ref, db_ref, dw_ref, dbf_ref):
        masks = _sgu_group_masks()
        tril = _sgu_tril()
        wf = [jnp.where(tril, w_ref[g], 0.0) for g in range(N_SGU_GROUPS)]
        wm = [w.astype(BF16) for w in wf]
        wmt = [jnp.transpose(w).astype(BF16) for w in wf]
        vn, vjp = jax.vjp(norm, v_ref[...], g_ref[...], b_ref[...])

        @pl.when(pl.program_id(0) == 0)
        def _():
            dg_ref[...] = jnp.zeros_like(dg_ref)
            db_ref[...] = jnp.zeros_like(db_ref)
            dw_ref[...] = jnp.zeros_like(dw_ref)
            dbf_ref[...] = jnp.zeros_like(dbf_ref)

        dvn_parts = []
        for ch in range(nch):
            rows = pl.ds(ch * SGU_CHUNK, SGU_CHUNK)
            vc = vn[ch * SGU_CHUNK:(ch + 1) * SGU_CHUNK, :]
            vc16 = vc.astype(BF16)
            mixed = bf_ref[...]
            for g in range(N_SGU_GROUPS):
                mixed = mixed + jnp.dot(wm[g], jnp.where(masks[g], vc, 0.0).astype(BF16), preferred_element_type=F32)
            dy = dy_ref[rows, :]
            ug, slope = _gelu_and_slope(u_ref[rows, :])
            du_ref[rows, :] = (dy * mixed * slope).astype(BF16)
            dmixed = dy * ug
            dbf_ref[...] += dmixed
            dvc = jnp.zeros((SGU_CHUNK, D_SGU), F32)
            for g in range(N_SGU_GROUPS):
                dm16 = jnp.where(masks[g], dmixed, 0.0).astype(BF16)
                dw_ref[g] += jnp.where(tril, lax.dot_general(dm16, vc16, nt_dims, preferred_element_type=F32), 0.0)
                dvc = dvc + jnp.dot(wmt[g], dm16, preferred_element_type=F32)
            dvn_parts.append(dvc)
        dv, dg, db = vjp(jnp.concatenate(dvn_parts, axis=0))
        dv_ref[...] = dv.astype(BF16)
        dg_ref[...] += dg
        db_ref[...] += db

    row = lambda blk: pl.BlockSpec((tr, D_SGU), lambda i: (i, blk))
    vec = pl.BlockSpec((1, D_SGU), lambda i: (0, 0))
    wsp = pl.BlockSpec((N_SGU_GROUPS, SGU_CHUNK, SGU_CHUNK), lambda i: (0, 0, 0))
    bsp = pl.BlockSpec((SGU_CHUNK, D_SGU), lambda i: (0, 0))
    return pl.pallas_call(
        body, name=name, grid=(s // tr,),
        in_specs=[row(ublk), row(ublk + 1), vec, vec, wsp, bsp, row(0)],
        out_specs=[row(0), row(0), vec, vec, wsp, bsp],
        out_shape=[jax.ShapeDtypeStruct((s, D_SGU), BF16), jax.ShapeDtypeStruct((s, D_SGU), BF16),
                   jax.ShapeDtypeStruct((1, D_SGU), F32), jax.ShapeDtypeStruct((1, D_SGU), F32),
                   jax.ShapeDtypeStruct((N_SGU_GROUPS, SGU_CHUNK, SGU_CHUNK), F32),
                   jax.ShapeDtypeStruct((SGU_CHUNK, D_SGU), F32)],
        compiler_params=_params("arbitrary"),
    )(h, h, ln_g, ln_b, w_s, b_full, dy)


def _merge_fwd(h, b_gate, att, yc, ys, wa, wc, ws, name):
    s, d = att.shape[0], wa.shape[1]
    tm, tn = _tile(s, 512), _tile(d, 512)
    nj = d // tn

    def body(g0_ref, g1_ref, g2_ref, bg_ref, a_ref, c_ref, s_ref, wa_ref, wc_ref, ws_ref, o_ref):
        acc = jax.nn.sigmoid(g0_ref[...] + bg_ref[0:1, :]) * jnp.dot(a_ref[...], wa_ref[...], preferred_element_type=F32)
        acc += jax.nn.sigmoid(g1_ref[...] + bg_ref[1:2, :]) * jnp.dot(c_ref[...], wc_ref[...], preferred_element_type=F32)
        acc += jax.nn.sigmoid(g2_ref[...] + bg_ref[2:3, :]) * jnp.dot(s_ref[...], ws_ref[...], preferred_element_type=F32)
        o_ref[...] = acc.astype(BF16)

    gate = lambda b: pl.BlockSpec((tm, tn), lambda j, i: (i, b * nj + j))
    act = lambda k: pl.BlockSpec((tm, k), lambda j, i: (i, 0))
    wgt = lambda k: pl.BlockSpec((k, tn), lambda j, i: (0, j))
    return pl.pallas_call(
        body, name=name, grid=(nj, s // tm),
        in_specs=[gate(0), gate(1), gate(2), pl.BlockSpec((3, tn), lambda j, i: (0, j)),
                  act(D_ATT), act(D_CONV), act(D_SGU), wgt(D_ATT), wgt(D_CONV), wgt(D_SGU)],
        out_specs=pl.BlockSpec((tm, tn), lambda j, i: (i, j)),
        out_shape=jax.ShapeDtypeStruct((s, d), BF16), compiler_params=_params("parallel", "arbitrary"),
    )(h, h, h, b_gate, att, yc, ys, wa, wc, ws)


def _merge_bwd(h, b_gate, att, yc, ys, wa, wc, ws, dm, name):
    s, d = att.shape[0], wa.shape[1]
    tm, tn = _tile(s, 512), _tile(d, 512)
    nj = d // tn

    def body(g0_ref, g1_ref, g2_ref, bg_ref, a_ref, c_ref, s_ref, wa_ref, wc_ref, ws_ref, dm_ref,
             dya_ref, dyc_ref, dys_ref, dgl_ref, dbg_ref):
        dm = dm_ref[...]
        sums = []
        for b, (g_ref, x_ref, w_ref, dy_ref) in enumerate(((g0_ref, a_ref, wa_ref, dya_ref), (g1_ref, c_ref, wc_ref, dyc_ref),
                                                          (g2_ref, s_ref, ws_ref, dys_ref))):
            gate = jax.nn.sigmoid(g_ref[...] + bg_ref[b:b + 1, :])
            y = jnp.dot(x_ref[...], w_ref[...], preferred_element_type=F32)
            dy_ref[...] = (dm * gate).astype(BF16)
            dgl = dm * y * gate * (1.0 - gate)
            dgl_ref[b] = dgl.astype(BF16)
            sums.append(jnp.sum(dgl, axis=0, keepdims=True))
        sub = lax.broadcasted_iota(jnp.int32, (3, tn), 0)
        part = jnp.where(sub == 0, sums[0], jnp.where(sub == 1, sums[1], sums[2]))

        @pl.when(pl.program_id(1) == 0)
        def _():
            dbg_ref[...] = jnp.zeros_like(dbg_ref)

        dbg_ref[...] += part

    gate = lambda b: pl.BlockSpec((tm, tn), lambda j, i: (i, b * nj + j))
    act = lambda k: pl.BlockSpec((tm, k), lambda j, i: (i, 0))
    wgt = lambda k: pl.BlockSpec((k, tn), lambda j, i: (0, j))
    tile = pl.BlockSpec((tm, tn), lambda j, i: (i, j))
    return pl.pallas_call(
        body, name=name, grid=(nj, s // tm),
        in_specs=[gate(0), gate(1), gate(2), pl.BlockSpec((3, tn), lambda j, i: (0, j)),
                  act(D_ATT), act(D_CONV), act(D_SGU), wgt(D_ATT), wgt(D_CONV), wgt(D_SGU), tile],
        out_specs=[tile, tile, tile, pl.BlockSpec((3, tm, tn), lambda j, i: (0, i, j)),
                   pl.BlockSpec((3, tn), lambda j, i: (0, j))],
        out_shape=[jax.ShapeDtypeStruct((s, d), BF16)] * 3 + [jax.ShapeDtypeStruct((3, s, d), BF16),
                                                              jax.ShapeDtypeStruct((3, d), F32)],
        compiler_params=_params("parallel", "arbitrary"),
    )(h, h, h, b_gate, att, yc, ys, wa, wc, ws, dm)


def _ffn_act_fwd(hh, cw, name):
    s, dff = hh.shape[0], hh.shape[1] // 2
    nblk = dff // LANES

    def body(a_ref, b_ref, wa_ref, wb_ref, z_ref):
        rows = lax.broadcasted_iota(jnp.int32, (s, LANES), 0)
        z_ref[...] = (_gelu(_conv3(a_ref[...], wa_ref, rows)) * _conv3(b_ref[...], wb_ref, rows)).astype(BF16)

    col = lambda base: pl.BlockSpec((s, LANES), lambda j: (0, base + j))
    wsp = lambda base: pl.BlockSpec((3, LANES), lambda j: (0, base + j))
    return pl.pallas_call(
        body, name=name, grid=(nblk,), in_specs=[col(0), col(nblk), wsp(0), wsp(nblk)], out_specs=col(0),
        out_shape=jax.ShapeDtypeStruct((s, dff), BF16), compiler_params=_params("parallel"),
    )(hh, hh, cw, cw)


def _ffn_act_bwd(hh, cw, dz, name):
    s, dff = hh.shape[0], hh.shape[1] // 2
    nblk = dff // LANES

    def body(a_ref, b_ref, wa_ref, wb_ref, dz_ref, da_ref, db_ref, dwa_ref, dwb_ref):
        rows = lax.broadcasted_iota(jnp.int32, (s, LANES), 0)
        a, b, dz = a_ref[...], b_ref[...], dz_ref[...]
        ga, slope = _gelu_and_slope(_conv3(a, wa_ref, rows))
        da, dwa_ref[...] = _conv3_bwd(dz * _conv3(b, wb_ref, rows) * slope, a, wa_ref, rows)
        db, dwb_ref[...] = _conv3_bwd(dz * ga, b, wb_ref, rows)
        da_ref[...] = da.astype(BF16)
        db_ref[...] = db.astype(BF16)

    col = lambda base: pl.BlockSpec((s, LANES), lambda j: (0, base + j))
    wsp = lambda base: pl.BlockSpec((3, LANES), lambda j: (0, base + j))
    w8 = lambda base: pl.BlockSpec((8, LANES), lambda j: (0, base + j))
    return pl.pallas_call(
        body, name=name, grid=(nblk,), in_specs=[col(0), col(nblk), wsp(0), wsp(nblk), col(0)],
        out_specs=[col(0), col(0), w8(0), w8(0)],
        out_shape=[jax.ShapeDtypeStruct((s, dff), BF16)] * 2 + [jax.ShapeDtypeStruct((8, dff), F32)] * 2,
        compiler_params=_params("parallel"),
    )(hh, hh, cw, cw, dz)


ANY = pl.BlockSpec(memory_space=pl.ANY)


def _place():
    return lax.axis_index("x"), lax.axis_index("y"), lax.axis_index("c")


def _all_gather(arrs, name):
    n = len(arrs)

    def body(*refs):
        ins, outs = refs[:n], refs[n:2 * n]
        send_sems, recv_sems, local_sems = refs[2 * n:]
        x, y, c = _place()
        me, sibling = (x, y, c), (x, y, 1 - c)
        chips = [(1 - x, y), (x, 1 - y), (1 - x, 1 - y)]

        def slab(a, dev):
            return outs[a].at[4 * dev[0] + 2 * dev[1] + dev[2]]

        def copy(a, k, block, to, src=None):
            return pltpu.make_async_remote_copy(
                src_ref=slab(a, block) if src is None else src, dst_ref=slab(a, block),
                send_sem=send_sems.at[7 * a + k], recv_sem=recv_sems.at[7 * a + k], device_id=to, device_id_type=MESH)

        mine = [pltpu.make_async_copy(ins[a], slab(a, me), local_sems.at[a]) for a in range(n)]
        for cp in mine:
            cp.start()
        first = []
        for a in range(n):
            first.append(copy(a, 0, me, sibling, src=ins[a]))
            first += [copy(a, 1 + j, me, (*chip, c), src=ins[a]) for j, chip in enumerate(chips)]
        for cp in first:
            cp.start()
        passed = []
        for a in range(n):
            for j, chip in enumerate(chips):
                copy(a, 1 + j, (*chip, c), me).wait_recv()
                fwd = copy(a, 4 + j, (*chip, c), sibling)
                fwd.start()
                passed.append(fwd)
        for a in range(n):
            copy(a, 0, sibling, me).wait_recv()
            for j, chip in enumerate(chips):
                copy(a, 4 + j, (*chip, 1 - c), me).wait_recv()
        for cp in first + passed:
            cp.wait_send()
        for cp in mine:
            cp.wait()

    return pl.pallas_call(
        body, name=name, in_specs=[ANY] * n, out_specs=[ANY] * n,
        out_shape=[jax.ShapeDtypeStruct((N_DEV,) + a.shape, a.dtype) for a in arrs],
        scratch_shapes=[pltpu.SemaphoreType.DMA((7 * n,)), pltpu.SemaphoreType.DMA((7 * n,)), pltpu.SemaphoreType.DMA((n,))],
    )(*arrs)


HBM = pl.BlockSpec(memory_space=pltpu.HBM)
SEM = pl.BlockSpec(memory_space=pltpu.SEMAPHORE)
EFFECT = pltpu.SideEffectType.DATAFLOW_SIDE_EFFECTING


def _slot(dev):
    return 4 * dev[0] + 2 * dev[1] + dev[2]


def _exchange_copies(src_refs, land_refs, send_sems, recv_sems, src_view, land_view):
    x, y, c = _place()
    me = (x, y, c)
    peers = [(1 - x if r & 4 else x, 1 - y if r & 2 else y, 1 - c if r & 1 else c) for r in range(1, N_DEV)]
    sends, lands = [], []
    for a, (src, land) in enumerate(zip(src_refs, land_refs)):
        for k, peer in enumerate(peers):
            sems = dict(send_sem=send_sems.at[7 * a + k], recv_sem=recv_sems.at[7 * a + k], device_id=peer,
                        device_id_type=MESH)
            sends.append(pltpu.make_async_remote_copy(src_ref=src_view(src, _slot(peer)),
                                                      dst_ref=land_view(land, _slot(me)), **sems))
            lands.append(pltpu.make_async_remote_copy(src_ref=src_view(src, _slot(me)),
                                                      dst_ref=land_view(land, _slot(peer)), **sems))
    return sends, lands


def _own_copies(src_refs, land_refs, local_sems, src_view, land_view):
    me = _slot(_place())
    return [pltpu.make_async_copy(src_view(src, me), land_view(land, me), local_sems.at[a])
            for a, (src, land) in enumerate(zip(src_refs, land_refs))]


def _exchange_start(srcs, lands, after, src_view, land_view, name):
    n = len(srcs)

    def body(*refs):
        src_refs, land_refs = refs[:n], refs[n:2 * n]
        send_sems, recv_sems, local_sems = refs[2 * n + 1:2 * n + 4]
        token = refs[-1]
        sends, _ = _exchange_copies(src_refs, land_refs, send_sems, recv_sems, src_view, land_view)
        for cp in sends + _own_copies(src_refs, land_refs, local_sems, src_view, land_view):
            cp.start()
        token[...] = jnp.zeros_like(token)

    thru = [pltpu.HBM(a.shape, a.dtype) for a in list(srcs) + list(lands)]
    outs = pl.pallas_call(
        body, name=name,
        out_shape=(pltpu.SemaphoreType.DMA((7 * n,)), pltpu.SemaphoreType.DMA((7 * n,)), pltpu.SemaphoreType.DMA((n,)),
                   *thru, jax.ShapeDtypeStruct((8, LANES), F32)),
        in_specs=[HBM] * (2 * n) + [ANY],
        out_specs=(SEM, SEM, SEM, *([HBM] * (2 * n)), pl.BlockSpec(memory_space=pltpu.VMEM)),
        input_output_aliases={i: 3 + i for i in range(2 * n)},
        compiler_params=pltpu.CompilerParams(has_side_effects=EFFECT),
    )(*[pltpu.with_memory_space_constraint(a, pltpu.HBM) for a in list(srcs) + list(lands)], after)
    return outs[:3], list(outs[3:3 + n]), list(outs[3 + n:3 + 2 * n]), outs[-1]


def _exchange_wait(sems, srcs, lands, after, src_view, land_view, name):
    n = len(srcs)

    def body(*refs):
        src_refs, land_refs = refs[:n], refs[n:2 * n]
        send_sems, recv_sems, local_sems = refs[2 * n:2 * n + 3]
        sends, landed = _exchange_copies(src_refs, land_refs, send_sems, recv_sems, src_view, land_view)
        for cp in sends:
            cp.wait_send()
        for cp in landed:
            cp.wait_recv()
        for cp in _own_copies(src_refs, land_refs, local_sems, src_view, land_view):
            cp.wait()

    after = after if isinstance(after, (list, tuple)) else [after]
    outs = pl.pallas_call(
        body, name=name, out_shape=[pltpu.HBM(a.shape, a.dtype) for a in list(srcs) + list(lands)],
        in_specs=[HBM] * (2 * n) + [SEM, SEM, SEM] + [ANY] * len(after), out_specs=[HBM] * (2 * n),
        input_output_aliases={i: i for i in range(2 * n)},
        compiler_params=pltpu.CompilerParams(has_side_effects=EFFECT),
    )(*srcs, *lands, *sems, *after)
    return list(outs[:n]), list(outs[n:])


ADAMW_BLOCK_BYTES = 1 << 19
PACK_ROWS = 256


def _adamw(slabs, w, m, v, name):
    nl, r, c = w.shape
    row_edges = [r] + [t for t in range(8, r, 8) if r % t == 0]
    col_edges = [c] + [t for t in range(LANES, c, LANES) if c % t == 0]
    fits = [(tr * tc, tc, tr) for tr in row_edges for tc in col_edges if tr * tc * 4 <= ADAMW_BLOCK_BYTES]
    _, tc, tr = max(fits) if fits else (0, min(col_edges), min(row_edges))

    def body(s_ref, w_ref, m_ref, v_ref, g_ref, d_ref, nm_ref, nv_ref):
        g = s_ref[0].astype(F32)
        for q in range(1, N_DEV):
            g = g + s_ref[q].astype(F32)
        m_new = ADAM_B1 * m_ref[...] + (1.0 - ADAM_B1) * g
        v_new = ADAM_B2 * v_ref[...] + (1.0 - ADAM_B2) * (g * g)
        m_hat = m_new / (1.0 - ADAM_B1 ** ADAM_STEP)
        v_hat = v_new / (1.0 - ADAM_B2 ** ADAM_STEP)
        g_ref[...] = g
        d_ref[...] = -ADAM_LR * (m_hat / (jnp.sqrt(v_hat) + ADAM_EPS) + ADAM_WD * w_ref[...])
        nm_ref[...] = m_new
        nv_ref[...] = v_new

    blk = pl.BlockSpec((None, tr, tc), lambda l, i, j: (l, i, j))
    return pl.pallas_call(
        body, name=name, grid=(nl, r // tr, c // tc),
        in_specs=[pl.BlockSpec((N_DEV, None, tr, tc), lambda l, i, j: (0, l, i, j)), blk, blk, blk],
        out_specs=[blk] * 4, out_shape=[jax.ShapeDtypeStruct(w.shape, F32)] * 4,
        compiler_params=_params("parallel", "parallel", "parallel"),
    )(slabs, w, m, v)


def _layout(d):
    off = {"gate": 0, "q": 3 * d}
    off["bg"] = off["q"] + 3 * D_ATT
    off["u"] = off["bg"] + 3 * D_CONV
    off["f"] = off["u"] + 2 * D_SGU
    width = -(-(off["f"] + LANES) // 512) * 512
    return off, width


def _pad_w_in(wt, d, token):
    off, width = _layout(d)
    nqkv, nrest = 3 * D_ATT, 3 * D_CONV + 2 * D_SGU
    pad = jnp.zeros((width - off["f"] - N_HEADS, wt.shape[1]), wt.dtype) + token[0, 0].astype(wt.dtype)
    return jnp.concatenate([wt[nqkv + N_HEADS + nrest:], wt[:nqkv], wt[nqkv + N_HEADS:nqkv + N_HEADS + nrest],
                            wt[nqkv:nqkv + N_HEADS], pad], axis=0)


def _unpad_w_in(wtp, d):
    off, _ = _layout(d)
    return jnp.concatenate([wtp[off["q"]:off["bg"]], wtp[off["f"]:off["f"] + N_HEADS], wtp[off["bg"]:off["f"]],
                            wtp[:off["q"]]], axis=0)


def _cols_from_slabs(g):
    return jnp.transpose(g, (1, 0, 2)).reshape(g.shape[1], N_DEV * g.shape[2])


def _cols_to_slabs(w):
    r, c = w.shape[0], w.shape[1] // N_DEV
    return jnp.transpose(w.reshape(r, N_DEV, c), (1, 0, 2))


def kernel(x, pre_mix_g, post_mix_g, pre_ffn_g, post_ffn_g, w_in, b_forget, b_gate, conv_mix_w, sgu_ln_g, sgu_ln_b, sgu_w, sgu_b, w_branch_att, w_branch_conv, w_branch_sgu, w_out, w_ffn_up, conv_ffn_w, w_ffn_down, loss_target, m_pre_mix_g, m_post_mix_g, m_pre_ffn_g, m_post_ffn_g, m_w_in, m_b_forget, m_b_gate, m_conv_mix_w, m_sgu_ln_g, m_sgu_ln_b, m_sgu_w, m_sgu_b, m_w_branch_att, m_w_branch_conv, m_w_branch_sgu, m_w_out, m_w_ffn_up, m_conv_ffn_w, m_w_ffn_down, v_pre_mix_g, v_post_mix_g, v_pre_ffn_g, v_post_ffn_g, v_w_in, v_b_forget, v_b_gate, v_conv_mix_w, v_sgu_ln_g, v_sgu_ln_b, v_sgu_w, v_sgu_b, v_w_branch_att, v_w_branch_conv, v_w_branch_sgu, v_w_out, v_w_ffn_up, v_conv_ffn_w, v_w_ffn_down):
    depth = w_in.shape[0]
    s, d = x.shape[1], x.shape[2]
    dff = w_ffn_down.shape[1] * N_DEV
    off, _ = _layout(d)
    qblk, bgblk, ublk, fblk = off["q"] // LANES, off["bg"] // LANES, off["u"] // D_SGU, off["f"] // LANES
    x0 = x.reshape(s, d)
    target = loss_target.reshape(s, d)
    ncm, ncf = conv_mix_w.shape[2], conv_ffn_w.shape[2]

    lo = d // N_DEV

    whole = lambda ref, slot: ref
    slab = lambda ref, slot: ref.at[slot]

    w_in_t, w_up_t = jnp.transpose(w_in, (0, 2, 1)), jnp.transpose(w_ffn_up, (0, 2, 1))

    def shards_of(l, part):
        if part == "mix":
            small = jnp.concatenate([b_gate[l], conv_mix_w[l], conv_ffn_w[l]], axis=1)
            return [w_in_t[l].astype(BF16), w_branch_att[l].astype(BF16), w_branch_conv[l].astype(BF16),
                    w_branch_sgu[l].astype(BF16), w_out[l].astype(BF16), small]
        return [w_up_t[l].astype(BF16), w_ffn_down[l].astype(BF16)]

    def gather_start(l, part, after):
        shards = shards_of(l, part)
        lands = [lax.empty((N_DEV,) + a.shape, a.dtype) for a in shards]
        return _exchange_start(shards, lands, after, whole, slab, name=f"gather_start_{part}_{l}")

    def gather_finish(l, part, started, after):
        sems, shards, lands, _ = started[part]
        shards, lands = _exchange_wait(sems, shards, lands, after, whole, slab, name=f"gather_wait_{part}_{l}")
        token = jnp.zeros((8, LANES), F32)
        if l + 1 < depth:
            started[part] = gather_start(l + 1, part, lands[0])
            token = started[part][3]
        return lands, token

    def bfull(l):
        return jnp.repeat(jnp.transpose(sgu_b[l]), HEAD_DIM, axis=1)

    def bf_pad(l):
        return jnp.pad(b_forget[l], (0, LANES - N_HEADS)).reshape(1, LANES)

    saved = []
    weights = []
    xin = x0
    xn = _prenorm(x0, pre_mix_g[0:1], name="prenorm_first")
    loss_acc = dy = None
    first = _all_gather(shards_of(0, "mix"), name="gather_first")
    started = {"ffn": gather_start(0, "ffn", first[0])}
    if depth > 1:
        started["mix"] = gather_start(1, "mix", started["ffn"][3])
    for l in range(depth):
        if l == 0:
            (g_in, g_a, g_c, g_s, g_o, g_small), token = first, started["mix" if depth > 1 else "ffn"][3]
        else:
            (g_in, g_a, g_c, g_s, g_o, g_small), token = gather_finish(l, "mix", started, xin)
        g_small = _cols_from_slabs(g_small).reshape(3, N_DEV, -1)
        w = dict(w_in=_pad_w_in(g_in.reshape(N_DEV * g_in.shape[1], d), d, token), wa=_cols_from_slabs(g_a),
                 wc=_cols_from_slabs(g_c), ws=_cols_from_slabs(g_s), w_out=g_o.reshape(d, d),
                 b_gate=g_small[:, :, :lo].reshape(3, d), cmw=g_small[:, :, lo:lo + ncm].reshape(3, D_CONV),
                 cfw=g_small[:, :, lo + ncm:].reshape(3, 2 * dff))
        weights.append(w)
        h = _mm(xn, w["w_in"], "nt", F32, name="proj_in")
        c, ct = _forget_prep(h, bf_pad(l), fblk, name="forget_prep")
        att, lse = _attention_fwd(h, c, ct, qblk, name="attention_fwd")
        yc = _sconv_fwd(h, w["cmw"], bgblk, name="sconv_fwd")
        ys = _sgu_fwd(h, sgu_ln_g[l:l + 1], sgu_ln_b[l:l + 1], sgu_w[l], bfull(l), ublk, name="sgu_fwd")
        merged = _merge_fwd(h, w["b_gate"], att, yc, ys, w["wa"], w["wc"], w["ws"], name="merge_fwd")
        o = _mm(merged, w["w_out"], "nn", F32, name="proj_out")
        (g_up, g_dn), token = gather_finish(l, "ffn", started, o)
        w["w_up"], w["w_dn"] = g_up.reshape(2 * dff, d), g_dn.reshape(dff, d)
        x1, xn2 = _postnorm(xin, o, post_mix_g[l:l + 1], pre_ffn_g[l:l + 1] + token[0, 0], name="postnorm_mix")
        hh = _mm(xn2, w["w_up"], "nt", F32, name="ffn_up")
        z = _ffn_act_fwd(hh, w["cfw"], name="ffn_act_fwd")
        f = _mm(z, w["w_dn"], "nn", F32, name="ffn_down")
        saved.append(dict(xin=xin, xn=xn, h=h, c=c, ct=ct, lse=lse, att=att, yc=yc, ys=ys, merged=merged, o=o, x1=x1,
                          xn2=xn2, hh=hh, z=z, f=f))
        if l + 1 < depth:
            xin, xn = _postnorm(x1, f, post_ffn_g[l:l + 1], pre_mix_g[l + 1:l + 2], name="postnorm_ffn")
        else:
            dy, loss_acc = _postnorm_loss(x1, f, post_ffn_g[l:l + 1], target, name="postnorm_loss")
    loss = lax.psum(loss_acc[0, 0] * (0.5 / d), ("x", "y", "c"))

    rep = {k: [None] * depth for k in ("pre_mix_g", "post_mix_g", "pre_ffn_g", "post_ffn_g", "b_forget", "sgu_ln_g",
                                       "sgu_ln_b", "sgu_w", "sgu_b")}
    nsmall = lo + ncm + ncf
    lands = {"win": [lax.empty((N_DEV, depth) + w_in_t.shape[1:], BF16)],
             "mid": [lax.empty((N_DEV, depth) + shp, dt) for shp, dt in (
                 (w_branch_att.shape[1:], BF16), (w_branch_conv.shape[1:], BF16), (w_branch_sgu.shape[1:], BF16),
                 (w_out.shape[1:], BF16), ((3, nsmall), F32))],
             "ffn": [lax.empty((N_DEV, depth) + shp, BF16) for shp in (w_up_t.shape[1:], w_ffn_down.shape[1:])]}
    scatters = {part: [None] * depth for part in lands}

    def scatter_start(l, part, sends, after):
        layer_slab = lambda ref, slot: ref.at[slot, l]
        sems, sends, lands[part], token = _exchange_start(sends, lands[part], after, slab, layer_slab,
                                                          name=f"scatter_start_{part}_{l}")
        scatters[part][l] = (sems, sends, layer_slab)
        return token

    def scatter_finish(part, after):
        for l in range(depth):
            sems, sends, layer_slab = scatters[part][l]
            _, lands[part] = _exchange_wait(sems, sends, lands[part], after, slab, layer_slab,
                                            name=f"scatter_wait_{part}_{l}")
        return lands[part]

    token = jnp.zeros((8, LANES), F32)
    dx = dy
    for l in reversed(range(depth)):
        w, a = weights[l], saved[l]
        df, rep["post_ffn_g"][l] = _postnorm_bwd(a["f"], post_ffn_g[l:l + 1] + token[0, 0], dx, name="postnorm_bwd")
        dz = _mm(df, w["w_dn"], "nt", F32, name="ffn_down_dx")
        g_dn = _mm(a["z"], df, "tn", BF16, name="ffn_down_dw")
        dha, dhb, dcwa, dcwb = _ffn_act_bwd(a["hh"], w["cfw"], dz, name="ffn_act_bwd")
        dhh = jnp.concatenate([dha, dhb], axis=1)
        dcfw = jnp.concatenate([dcwa[0:3], dcwb[0:3]], axis=1)
        g_up = _mm(dhh, a["xn2"], "tn", BF16, name="ffn_up_dw")
        token = scatter_start(l, "ffn", [g_up.reshape(N_DEV, 2 * dff // N_DEV, d), g_dn.reshape(N_DEV, dff // N_DEV, d)],
                              dz)
        dx1, rep["pre_ffn_g"][l] = _mm_prenorm_bwd(dhh, w["w_up"], a["x1"], pre_ffn_g[l:l + 1], dx, token, name="ffn_up_dx")
        do, rep["post_mix_g"][l] = _postnorm_bwd(a["o"], post_mix_g[l:l + 1], dx1, name="postnorm_bwd")
        dmerged = _mm(do, w["w_out"], "nt", F32, name="proj_out_dx")
        g_o = _mm(a["merged"], do, "tn", BF16, name="proj_out_dw")
        dya, dyc, dys, dgl, dbg = _merge_bwd(a["h"], w["b_gate"], a["att"], a["yc"], a["ys"], w["wa"], w["wc"], w["ws"],
                                             dmerged, name="merge_bwd")
        dconv = _mm(dyc, w["wc"], "nt", F32, name="branch_conv_dx")
        dsgu = _mm(dys, w["ws"], "nt", F32, name="branch_sgu_dx")
        g_a = _mm(a["att"], dya, "tn", BF16, name="branch_att_dw")
        g_c = _mm(a["yc"], dyc, "tn", BF16, name="branch_conv_dw")
        g_s = _mm(a["ys"], dys, "tn", BF16, name="branch_sgu_dw")
        dbgate, dcg, dhc, dcmw = _sconv_bwd(a["h"], w["cmw"], dconv, bgblk, name="sconv_bwd")
        sends = [_cols_to_slabs(g_a), _cols_to_slabs(g_c), _cols_to_slabs(g_s), g_o.reshape(N_DEV, d // N_DEV, d),
                 jnp.concatenate([_cols_to_slabs(dbg), _cols_to_slabs(dcmw[0:3]), _cols_to_slabs(dcfw)], axis=2)]
        token = scatter_start(l, "mid", sends, dx1)
        datt = _mm(dya, w["wa"], "nt", F32, name="branch_att_dx", after=token)
        dq, dk, dv, dct4, dcq4 = _attention_bwd(a["h"], a["c"], a["ct"], a["lse"], a["att"], datt, qblk,
                                                name="attention_bwd")
        dfl, dbf = _forget_prep_bwd(a["h"], bf_pad(l), dct4, dcq4, fblk, name="forget_prep_bwd")
        rep["b_forget"][l] = dbf[0, :N_HEADS]
        du, dvs, dlg, dlb, dsw, dbfull = _sgu_bwd(a["h"], sgu_ln_g[l:l + 1], sgu_ln_b[l:l + 1], sgu_w[l], bfull(l), dsgu,
                                                  ublk, name="sgu_bwd")
        rep["sgu_ln_g"][l], rep["sgu_ln_b"][l], rep["sgu_w"][l] = dlg, dlb, dsw
        rep["sgu_b"][l] = jnp.transpose(jnp.sum(dbfull.reshape(SGU_CHUNK, N_SGU_GROUPS, HEAD_DIM), axis=2))
        dh = jnp.concatenate([dgl[0], dgl[1], dgl[2], dq.astype(BF16), dk.astype(BF16), dv.astype(BF16), dbgate, dcg, dhc,
                              du, dvs, dfl.astype(BF16), jnp.zeros((s, w["w_in"].shape[0] - off["f"] - LANES), BF16)], axis=1)
        g_in = _mm(dh, a["xn"], "tn", BF16, name="proj_in_dw")
        token = scatter_start(l, "win", [_unpad_w_in(g_in, d).reshape(N_DEV, -1, d)], dx1)
        dx, rep["pre_mix_g"][l] = _mm_prenorm_bwd(dh, w["w_in"], a["xin"], pre_mix_g[l:l + 1], dx1, token, name="proj_in_dx")

    outs = {}
    t3 = lambda arr: jnp.transpose(arr, (0, 2, 1))

    def update(name_, slabs, w_, m_, v_, transposed=False):
        if transposed:
            w_, m_, v_ = t3(w_), t3(m_), t3(v_)
        shp = w_.shape
        w3 = w_.reshape((shp[0], -1, shp[-1])) if w_.ndim >= 3 else w_.reshape((1,) + shp)
        res = _adamw(slabs.reshape((N_DEV,) + w3.shape), w3, m_.reshape(w3.shape), v_.reshape(w3.shape),
                     name="adamw_" + name_)
        outs[name_] = tuple(t3(t.reshape(shp)) if transposed else t.reshape(shp) for t in res)
        return res[0]

    rep_names = ("pre_mix_g", "post_mix_g", "pre_ffn_g", "post_ffn_g", "b_forget", "sgu_ln_g", "sgu_ln_b", "sgu_w", "sgu_b")
    rep_w = dict(pre_mix_g=(pre_mix_g, m_pre_mix_g, v_pre_mix_g), post_mix_g=(post_mix_g, m_post_mix_g, v_post_mix_g),
                 pre_ffn_g=(pre_ffn_g, m_pre_ffn_g, v_pre_ffn_g), post_ffn_g=(post_ffn_g, m_post_ffn_g, v_post_ffn_g),
                 b_forget=(b_forget, m_b_forget, v_b_forget), sgu_ln_g=(sgu_ln_g, m_sgu_ln_g, v_sgu_ln_g),
                 sgu_ln_b=(sgu_ln_b, m_sgu_ln_b, v_sgu_ln_b), sgu_w=(sgu_w, m_sgu_w, v_sgu_w), sgu_b=(sgu_b, m_sgu_b, v_sgu_b))

    def pack(parts):
        rows = [jnp.pad(p.reshape(-1), (0, -p.size % LANES)).reshape(-1, LANES) for p in parts]
        rows = jnp.concatenate(rows, axis=0)
        return jnp.pad(rows, ((0, -rows.shape[0] % PACK_ROWS), (0, 0)))

    part = pack([jnp.stack([g.reshape(rep_w[k][0].shape[1:]) for g in rep[k]]) for k in rep_names])
    small_sems, small_src, small_land, _ = _exchange_start([part], [lax.empty((N_DEV,) + part.shape, F32)], dx, whole, slab,
                                                            name="gather_small_start")

    got_up, got_dn = scatter_finish("ffn", dx)
    done = [update("w_ffn_up", got_up, w_ffn_up, m_w_ffn_up, v_w_ffn_up, transposed=True),
            update("w_ffn_down", got_dn, w_ffn_down, m_w_ffn_down, v_w_ffn_down)]

    _, (gathered,) = _exchange_wait(small_sems, small_src, small_land, done, whole, slab, name="gather_small_wait")
    packed = [pack([rep_w[k][i] for k in rep_names]) for i in range(3)]
    res = _adamw(gathered.reshape(N_DEV, 1, -1, LANES), *[p.reshape(1, -1, LANES) for p in packed], name="adamw_replicated")
    row = 0
    for k in rep_names:
        shp = rep_w[k][0].shape
        size = math.prod(shp)
        nrows = -(-size // LANES)
        outs[k] = tuple(t[0, row:row + nrows].reshape(-1)[:size].reshape(shp) for t in res)
        row += nrows

    got_a, got_c, got_s, got_o, small = scatter_finish("mid", res[0])
    done = [update("w_branch_att", got_a, w_branch_att, m_w_branch_att, v_w_branch_att),
            update("w_branch_conv", got_c, w_branch_conv, m_w_branch_conv, v_w_branch_conv),
            update("w_branch_sgu", got_s, w_branch_sgu, m_w_branch_sgu, v_w_branch_sgu),
            update("w_out", got_o, w_out, m_w_out, v_w_out),
            update("b_gate", small[..., :lo], b_gate, m_b_gate, v_b_gate),
            update("conv_mix_w", small[..., lo:lo + ncm], conv_mix_w, m_conv_mix_w, v_conv_mix_w),
            update("conv_ffn_w", small[..., lo + ncm:], conv_ffn_w, m_conv_ffn_w, v_conv_ffn_w)]
    (got_in,) = scatter_finish("win", done)
    update("w_in", got_in, w_in, m_w_in, v_w_in, transposed=True)

    order = ("pre_mix_g", "post_mix_g", "pre_ffn_g", "post_ffn_g", "w_in", "b_forget", "b_gate", "conv_mix_w", "sgu_ln_g",
             "sgu_ln_b", "sgu_w", "sgu_b", "w_branch_att", "w_branch_conv", "w_branch_sgu", "w_out", "w_ffn_up",
             "conv_ffn_w", "w_ffn_down")
    grad_x = dx.reshape(x.shape)
    return (loss, grad_x, *[outs[k][0] for k in order], *[outs[k][1] for k in order], *[outs[k][2] for k in order],
            *[outs[k][3] for k in order])
```

```python
import functools
import math

import jax
import jax.numpy as jnp
from jax import lax
from jax.experimental import pallas as pl
from jax.experimental.pallas import tpu as pltpu

F32 = jnp.float32
BF16 = jnp.bfloat16

N_DEV = 8
HEAD_DIM = 64
N_HEADS = 8
D_ATT = 512
D_CONV = 256
D_SGU = 256
N_SGU_GROUPS = 4
SGU_CHUNK = 128
RMS_EPS = 1e-6
LN_EPS = 1e-5
ADAM_LR = 0.001
ADAM_B1 = 0.9
ADAM_B2 = 0.999
ADAM_EPS = 1e-08
ADAM_WD = 0.01
ADAM_STEP = 10
LANES = 128
VMEM_LIMIT = 56 * 1024 * 1024
ATT_TILE = 512
LOG2E = math.log2(math.e)
NEG = -1e30
MESH = pl.DeviceIdType.MESH


def _params(*sem):
    return pltpu.CompilerParams(dimension_semantics=sem if sem else None, vmem_limit_bytes=VMEM_LIMIT)


def _tile(n, cap):
    if n <= cap:
        return n
    t = cap - cap % LANES
    while n % t:
        t -= LANES
    return t


def _gelu(x):
    return 0.5 * x * (1.0 + jnp.tanh(math.sqrt(2.0 / math.pi) * (x + 0.044715 * (x * x * x))))


def _gelu_and_slope(x):
    k0, k1 = math.sqrt(2.0 / math.pi), 0.044715
    x2 = x * x
    t = jnp.tanh(x * (k0 + (k0 * k1) * x2))
    half = 0.5 * (1.0 + t)
    return x * half, half + (0.5 * x) * (1.0 - t * t) * (k0 + (3.0 * k0 * k1) * x2)


def _rms(x, g):
    r = lax.rsqrt(jnp.mean(x * x, axis=-1, keepdims=True) + RMS_EPS)
    return x * r * g


def _layer_norm(x, g, b):
    mu = jnp.mean(x, axis=-1, keepdims=True)
    xc = x - mu
    var = jnp.mean(xc * xc, axis=-1, keepdims=True)
    return xc * lax.rsqrt(var + LN_EPS) * g + b


def _shift_down(x, k, rows):
    return jnp.where(rows >= k, pltpu.roll(x, k, 0), 0.0)


def _shift_up(x, k, rows):
    s = x.shape[0]
    return jnp.where(rows < s - k, pltpu.roll(x, s - k, 0), 0.0)


def _conv3(x, w_ref, rows):
    return w_ref[2:3, :] * x + w_ref[1:2, :] * _shift_down(x, 1, rows) + w_ref[0:1, :] * _shift_down(x, 2, rows)


def _conv3_bwd(dy, x, w_ref, rows):
    up1, up2 = _shift_up(dy, 1, rows), _shift_up(dy, 2, rows)
    dx = w_ref[2:3, :] * dy + w_ref[1:2, :] * up1 + w_ref[0:1, :] * up2
    d2 = jnp.sum(dy * x, axis=0, keepdims=True)
    d1 = jnp.sum(up1 * x, axis=0, keepdims=True)
    d0 = jnp.sum(up2 * x, axis=0, keepdims=True)
    sub = lax.broadcasted_iota(jnp.int32, (8, x.shape[1]), 0)
    return dx, jnp.where(sub == 0, d0, jnp.where(sub == 1, d1, jnp.where(sub == 2, d2, 0.0)))


MM_VMEM_BUDGET = 40 * 1024 * 1024
MM_TILE_CAP = 1408


def _mm_tiles(m, n, k, out_bytes):
    def edges(d):
        return [t for t in range(LANES, min(d, MM_TILE_CAP) + 1, LANES) if d % t == 0] or [d]

    best = None
    for tm in edges(m):
        for tn in edges(n):
            if 2 * (2 * k * (tm + tn) + tm * tn * out_bytes) > MM_VMEM_BUDGET:
                continue
            for a_outer in (True, False):
                reads = k * m + (m // tm) * k * n if a_outer else k * n + (n // tn) * k * m
                traffic = 2 * reads + m * n * out_bytes
                key = (traffic, -tm * tn)
                if best is None or key < best[0]:
                    best = (key, (tm, tn, a_outer))
    return best[1]


def _mm(a, b, form, out_dtype, name, after=None):
    if form == "nn":
        (m, k), n = a.shape, b.shape[1]
    elif form == "nt":
        (m, k), n = a.shape, b.shape[0]
    else:
        (k, m), n = a.shape, b.shape[1]
    tm, tn, a_outer = _mm_tiles(m, n, k, jnp.dtype(out_dtype).itemsize)
    dims = {"nn": (((1,), (0,)), ((), ())), "nt": (((1,), (1,)), ((), ())), "tn": (((0,), (0,)), ((), ()))}[form]

    def body(a_ref, b_ref, *rest):
        o_ref = rest[-1]
        o_ref[...] = lax.dot_general(a_ref[...], b_ref[...], dims, preferred_element_type=F32).astype(o_ref.dtype)

    ij = (lambda g0, g1: (g0, g1)) if a_outer else (lambda g0, g1: (g1, g0))
    a_spec = (pl.BlockSpec((k, tm), lambda g0, g1: (0, ij(g0, g1)[0])) if form == "tn"
              else pl.BlockSpec((tm, k), lambda g0, g1: (ij(g0, g1)[0], 0)))
    b_spec = (pl.BlockSpec((tn, k), lambda g0, g1: (ij(g0, g1)[1], 0)) if form == "nt"
              else pl.BlockSpec((k, tn), lambda g0, g1: (0, ij(g0, g1)[1])))
    extra = [] if after is None else [pl.BlockSpec((8, LANES), lambda g0, g1: (0, 0))]
    return pl.pallas_call(
        body, name=name, grid=(m // tm, n // tn) if a_outer else (n // tn, m // tm),
        in_specs=[a_spec, b_spec] + extra, out_specs=pl.BlockSpec((tm, tn), lambda g0, g1: ij(g0, g1)),
        out_shape=jax.ShapeDtypeStruct((m, n), out_dtype),
        compiler_params=_params("parallel", "arbitrary"),
    )(a, b, *([] if after is None else [after]))


def _postnorm_bwd_mm(o, g, dx, b, name):
    s, d = o.shape
    n = b.shape[0]
    tm, tn = _tile(s, 512), _tile(n, MM_TILE_CAP)

    def body(o_ref, g_ref, dx_ref, b_ref, do_ref, dg_ref, out_ref):
        i, j = pl.program_id(0), pl.program_id(1)

        @pl.when(j == 0)
        def _():
            _, vjp = jax.vjp(_rms, o_ref[...], g_ref[...])
            d_o, dg = vjp(dx_ref[...])
            do_ref[...] = d_o.astype(BF16)

            @pl.when(i == 0)
            def _():
                dg_ref[...] = jnp.zeros_like(dg_ref)

            dg_ref[...] += dg

        out_ref[...] = lax.dot_general(do_ref[...], b_ref[...], (((1,), (1,)), ((), ())), preferred_element_type=F32)

    row = pl.BlockSpec((tm, d), lambda i, j: (i, 0))
    vec = pl.BlockSpec((1, d), lambda i, j: (0, 0))
    return pl.pallas_call(
        body, name=name, grid=(s // tm, n // tn),
        in_specs=[row, vec, row, pl.BlockSpec((tn, d), lambda i, j: (j, 0))],
        out_specs=[row, vec, pl.BlockSpec((tm, tn), lambda i, j: (i, j))],
        out_shape=[jax.ShapeDtypeStruct((s, d), BF16), jax.ShapeDtypeStruct((1, d), F32), jax.ShapeDtypeStruct((s, n), F32)],
        compiler_params=_params("arbitrary", "arbitrary"),
    )(o, g, dx, b)


def _mm_prenorm_bwd(a, b, x, g, dres, after, name):
    s, k = a.shape
    d = b.shape[1]
    tm = _tile(s, 512)

    def body(a_ref, b_ref, x_ref, g_ref, dres_ref, after_ref, dx_ref, dg_ref):
        dxn = jnp.dot(a_ref[...], b_ref[...], preferred_element_type=F32)
        _, vjp = jax.vjp(_rms, x_ref[...], g_ref[...])
        dx, dg = vjp(dxn)
        dx_ref[...] = dres_ref[...] + dx

        @pl.when(pl.program_id(0) == 0)
        def _():
            dg_ref[...] = jnp.zeros_like(dg_ref)

        dg_ref[...] += dg

    row = pl.BlockSpec((tm, d), lambda i: (i, 0))
    vec = pl.BlockSpec((1, d), lambda i: (0, 0))
    return pl.pallas_call(
        body, name=name, grid=(s // tm,),
        in_specs=[pl.BlockSpec((tm, k), lambda i: (i, 0)),
                  pl.BlockSpec((k, d), lambda i: (0, 0), pipeline_mode=pl.Buffered(1)),
                  row, vec, row, pl.BlockSpec((8, LANES), lambda i: (0, 0))],
        out_specs=[row, vec], out_shape=[jax.ShapeDtypeStruct((s, d), F32), jax.ShapeDtypeStruct((1, d), F32)],
        compiler_params=_params("arbitrary"),
    )(a, b, x, g, dres, after)


def _prenorm(x, g, name):
    s, d = x.shape
    tm = _tile(s, 512)

    def body(x_ref, g_ref, o_ref):
        o_ref[...] = _rms(x_ref[...], g_ref[...]).astype(BF16)

    return pl.pallas_call(
        body, name=name, grid=(s // tm,),
        in_specs=[pl.BlockSpec((tm, d), lambda i: (i, 0)), pl.BlockSpec((1, d), lambda i: (0, 0))],
        out_specs=pl.BlockSpec((tm, d), lambda i: (i, 0)),
        out_shape=jax.ShapeDtypeStruct((s, d), BF16), compiler_params=_params("parallel"),
    )(x, g)


def _postnorm(x, o, g_post, g_next, name):
    s, d = x.shape
    tm = _tile(s, 512)

    def body(x_ref, o_ref, gp_ref, gn_ref, x1_ref, xn_ref):
        x1 = x_ref[...] + _rms(o_ref[...], gp_ref[...])
        x1_ref[...] = x1
        xn_ref[...] = _rms(x1, gn_ref[...]).astype(BF16)

    row = pl.BlockSpec((tm, d), lambda i: (i, 0))
    vec = pl.BlockSpec((1, d), lambda i: (0, 0))
    return pl.pallas_call(
        body, name=name, grid=(s // tm,), in_specs=[row, row, vec, vec], out_specs=[row, row],
        out_shape=[jax.ShapeDtypeStruct((s, d), F32), jax.ShapeDtypeStruct((s, d), BF16)],
        compiler_params=_params("parallel"),
    )(x, o, g_post, g_next)


def _postnorm_loss(x, o, g_post, target, name):
    s, d = x.shape
    tm = _tile(s, 512)

    def body(x_ref, o_ref, gp_ref, t_ref, dy_ref, acc_ref):
        e = x_ref[...] + _rms(o_ref[...], gp_ref[...]) - t_ref[...]
        dy_ref[...] = e / d

        @pl.when(pl.program_id(0) == 0)
        def _():
            acc_ref[...] = jnp.zeros_like(acc_ref)

        acc_ref[...] += jnp.sum(jnp.sum(e * e, axis=1, keepdims=True), axis=0, keepdims=True)

    row = pl.BlockSpec((tm, d), lambda i: (i, 0))
    return pl.pallas_call(
        body, name=name, grid=(s // tm,),
        in_specs=[row, row, pl.BlockSpec((1, d), lambda i: (0, 0)), row],
        out_specs=[row, pl.BlockSpec((1, LANES), lambda i: (0, 0))],
        out_shape=[jax.ShapeDtypeStruct((s, d), F32), jax.ShapeDtypeStruct((1, LANES), F32)],
        compiler_params=_params("arbitrary"),
    )(x, o, g_post, target)


def _log_sigmoid(z):
    return jnp.minimum(z, 0.0) - jnp.log(1.0 + jnp.exp(-jnp.abs(z)))


def _forget_prep(h, bf_pad, fblk, name):
    s = h.shape[0]

    def body(f_ref, b_ref, c_ref, ct_ref):
        c = _log_sigmoid(f_ref[...] + b_ref[...])
        rows = lax.broadcasted_iota(jnp.int32, c.shape, 0)
        k = 1
        while k < s:
            c = c + _shift_down(c, k, rows)
            k *= 2
        c_ref[...] = c
        ct_ref[...] = jnp.transpose(c)[0:8, :]

    return pl.pallas_call(
        body, name=name, grid=(1,),
        in_specs=[pl.BlockSpec((s, LANES), lambda i: (0, fblk)), pl.BlockSpec((1, LANES), lambda i: (0, 0))],
        out_specs=[pl.BlockSpec((s, LANES), lambda i: (0, 0)), pl.BlockSpec((8, s), lambda i: (0, 0))],
        out_shape=[jax.ShapeDtypeStruct((s, LANES), F32), jax.ShapeDtypeStruct((8, s), F32)],
        compiler_params=_params("arbitrary"),
    )(h, bf_pad)


def _forget_prep_bwd(h, bf_pad, dct, dcq, fblk, name):
    s = h.shape[0]
    pairs = N_HEADS // 2

    def body(f_ref, b_ref, dct_ref, dcq_ref, df_ref, db_ref):
        dct = dct_ref[0]
        dcq = dcq_ref[0]
        for p in range(1, pairs):
            dct = dct + pltpu.roll(dct_ref[p], 2 * p, 0)
            dcq = dcq + pltpu.roll(dcq_ref[p], 2 * p, 1)
        dc = dcq + jnp.transpose(jnp.concatenate([dct, jnp.zeros((LANES - 8, s), F32)], axis=0))
        rows = lax.broadcasted_iota(jnp.int32, dc.shape, 0)
        k = 1
        while k < s:
            dc = dc + _shift_up(dc, k, rows)
            k *= 2
        z = f_ref[...] + b_ref[...]
        lane = lax.broadcasted_iota(jnp.int32, dc.shape, 1)
        df = jnp.where(lane < N_HEADS, dc * jax.nn.sigmoid(-z), 0.0)
        df_ref[...] = df
        db_ref[...] = jnp.sum(df, axis=0, keepdims=True)

    return pl.pallas_call(
        body, name=name, grid=(1,),
        in_specs=[pl.BlockSpec((s, LANES), lambda i: (0, fblk)), pl.BlockSpec((1, LANES), lambda i: (0, 0)),
                  pl.BlockSpec((pairs, 8, s), lambda i: (0, 0, 0)), pl.BlockSpec((pairs, s, LANES), lambda i: (0, 0, 0))],
        out_specs=[pl.BlockSpec((s, LANES), lambda i: (0, 0)), pl.BlockSpec((1, LANES), lambda i: (0, 0))],
        out_shape=[jax.ShapeDtypeStruct((s, LANES), F32), jax.ShapeDtypeStruct((1, LANES), F32)],
        compiler_params=_params("arbitrary"),
    )(h, bf_pad, dct, dcq)


def _pick_lane(blk, idx):
    lane = lax.broadcasted_iota(jnp.int32, blk.shape, 1)
    return jnp.sum(jnp.where(lane == idx, blk, 0.0), axis=1, keepdims=True)


def _pick_row(blk, idx):
    sub = lax.broadcasted_iota(jnp.int32, blk.shape, 0)
    return jnp.sum(jnp.where(sub == idx, blk, 0.0), axis=0, keepdims=True)


def _attention_fwd(h, c, ct, qblk, name):
    s = h.shape[0]
    t = _tile(s, ATT_TILE)
    nq = s // t
    scale = HEAD_DIM ** -0.5
    nt_dims = (((1,), (1,)), ((), ()))

    def body(q_ref, k_ref, v_ref, c_ref, ct_ref, o_ref, lse_ref):
        p = pl.program_id(0)
        i = pl.program_id(1)
        lane = lax.broadcasted_iota(jnp.int32, (1, LANES), 1)
        first = lane < HEAD_DIM
        q = q_ref[...] * (scale * LOG2E)
        qa = jnp.where(first, q, 0.0).astype(BF16)
        qb = jnp.where(first, 0.0, q).astype(BF16)
        cblk = c_ref[...]
        cta = _pick_lane(cblk, 2 * p) * LOG2E
        ctb = _pick_lane(cblk, 2 * p + 1) * LOG2E

        def step(j, carry, diagonal):
            ma, la, mb, lb, acc = carry
            off = pl.multiple_of(j * t, t)
            k = k_ref[pl.ds(off, t), :].astype(BF16)
            v = v_ref[pl.ds(off, t), :].astype(BF16)
            crow = ct_ref[:, pl.ds(off, t)] * LOG2E

            def one(qh, cth, hd, m_old, l_old):
                sc = lax.dot_general(qh, k, nt_dims, preferred_element_type=F32) - _pick_row(crow, hd)
                if diagonal:
                    keep = lax.broadcasted_iota(jnp.int32, (t, t), 0) >= lax.broadcasted_iota(jnp.int32, (t, t), 1)
                    sc = jnp.where(keep, sc, NEG)
                m_new = jnp.maximum(m_old, jnp.max(sc, axis=1, keepdims=True) + cth)
                pr = jnp.exp2(sc - (m_new - cth))
                alpha = jnp.exp2(m_old - m_new)
                l_new = alpha * l_old + jnp.sum(pr, axis=1, keepdims=True)
                pv = jnp.dot(pr.astype(BF16), v, preferred_element_type=F32)
                return m_new, l_new, alpha, pv

            ma2, la2, aa, pva = one(qa, cta, 2 * p, ma, la)
            mb2, lb2, ab, pvb = one(qb, ctb, 2 * p + 1, mb, lb)
            acc = jnp.where(first, aa * acc + pva, ab * acc + pvb)
            return ma2, la2, mb2, lb2, acc

        init = (jnp.full((t, 1), NEG, F32), jnp.zeros((t, 1), F32), jnp.full((t, 1), NEG, F32),
                jnp.zeros((t, 1), F32), jnp.zeros((t, LANES), F32))
        carry = lax.fori_loop(0, i, lambda j, carry: step(j, carry, False), init)
        ma, la, mb, lb, acc = step(i, carry, True)
        o_ref[...] = (acc / jnp.where(first, la, lb)).astype(BF16)
        lse_ref[0] = jnp.broadcast_to(ma + jnp.log2(la), (t, LANES))
        lse_ref[1] = jnp.broadcast_to(mb + jnp.log2(lb), (t, LANES))

    return pl.pallas_call(
        body, name=name, grid=(N_HEADS // 2, nq),
        in_specs=[pl.BlockSpec((t, LANES), lambda p, i: (i, qblk + p)),
                  pl.BlockSpec((s, LANES), lambda p, i: (0, qblk + 4 + p)),
                  pl.BlockSpec((s, LANES), lambda p, i: (0, qblk + 8 + p)),
                  pl.BlockSpec((t, LANES), lambda p, i: (i, 0)),
                  pl.BlockSpec((8, s), lambda p, i: (0, 0))],
        out_specs=[pl.BlockSpec((t, LANES), lambda p, i: (i, p)),
                   pl.BlockSpec((2, t, LANES), lambda p, i: (p, i, 0))],
        out_shape=[jax.ShapeDtypeStruct((s, D_ATT), BF16), jax.ShapeDtypeStruct((N_HEADS, s, LANES), F32)],
        compiler_params=_params("parallel", "arbitrary"),
    )(h, h, h, c, ct)


def _attention_bwd(h, c, ct, lse, att, datt, qblk, name):
    s = h.shape[0]
    t = _tile(s, ATT_TILE)
    nq = s // t
    scale = HEAD_DIM ** -0.5
    nt_dims = (((1,), (1,)), ((), ()))
    tn_dims = (((0,), (0,)), ((), ()))

    def body(q_ref, k_ref, v_ref, c_ref, ct_ref, lse_ref, o_ref, do_ref, dq_ref, dk_ref, dv_ref, dct_ref, dcq_ref):
        p = pl.program_id(0)
        j = pl.program_id(1)
        lane = lax.broadcasted_iota(jnp.int32, (1, LANES), 1)
        first = lane < HEAD_DIM
        kf = k_ref[...]
        vf = v_ref[...]
        k = kf.astype(BF16)
        ka = jnp.where(first, kf, 0.0).astype(BF16)
        kb = jnp.where(first, 0.0, kf).astype(BF16)
        va = jnp.where(first, vf, 0.0).astype(BF16)
        vb = jnp.where(first, 0.0, vf).astype(BF16)
        crow = ct_ref[...] * LOG2E
        csa = _pick_row(crow, 2 * p)
        csb = _pick_row(crow, 2 * p + 1)

        @pl.when(j == 0)
        def _():
            dq_ref[...] = jnp.zeros_like(dq_ref)
            dcq_ref[...] = jnp.zeros_like(dcq_ref)

        def step(i, carry, diagonal):
            dka, dkb, dva, dvb, dca, dcb = carry
            off = pl.multiple_of(i * t, t)
            rows = pl.ds(off, t)
            q = (q_ref[rows, :] * (scale * LOG2E)).astype(BF16)
            dof = do_ref[rows, :]
            do = dof.astype(BF16)
            prod = dof * o_ref[rows, :].astype(F32)
            cblk = c_ref[rows, :] * LOG2E

            def one(kh, vh, hd, csh, lse_h):
                sc = lax.dot_general(q, kh, nt_dims, preferred_element_type=F32) - csh
                if diagonal:
                    keep = lax.broadcasted_iota(jnp.int32, (t, t), 0) >= lax.broadcasted_iota(jnp.int32, (t, t), 1)
                    sc = jnp.where(keep, sc, NEG)
                pr = jnp.exp2(sc - (jnp.max(lse_h, axis=1, keepdims=True) - _pick_lane(cblk, hd)))
                dp = lax.dot_general(do, vh, nt_dims, preferred_element_type=F32)
                return pr, dp

            pra, dpa = one(ka, va, 2 * p, csa, lse_ref[0, rows, :])
            prb, dpb = one(kb, vb, 2 * p + 1, csb, lse_ref[1, rows, :])
            dela = jnp.sum(jnp.where(first, prod, 0.0), axis=1, keepdims=True)
            delb = jnp.sum(jnp.where(first, 0.0, prod), axis=1, keepdims=True)
            dsa = pra * (dpa - dela)
            dsb = prb * (dpb - delb)
            dsa16 = dsa.astype(BF16)
            dsb16 = dsb.astype(BF16)
            dva = dva + lax.dot_general(pra.astype(BF16), do, tn_dims, preferred_element_type=F32)
            dvb = dvb + lax.dot_general(prb.astype(BF16), do, tn_dims, preferred_element_type=F32)
            dka = dka + lax.dot_general(dsa16, q, tn_dims, preferred_element_type=F32)
            dkb = dkb + lax.dot_general(dsb16, q, tn_dims, preferred_element_type=F32)
            dqa = jnp.dot(dsa16, k, preferred_element_type=F32)
            dqb = jnp.dot(dsb16, k, preferred_element_type=F32)
            dq_ref[rows, :] += scale * jnp.where(first, dqa, dqb)
            dca = dca - jnp.sum(dsa, axis=0, keepdims=True)
            dcb = dcb - jnp.sum(dsb, axis=0, keepdims=True)
            dcq_ref[rows, :] += jnp.where(lane == 0, jnp.sum(dsa, axis=1, keepdims=True),
                                          jnp.where(lane == 1, jnp.sum(dsb, axis=1, keepdims=True), 0.0))
            return dka, dkb, dva, dvb, dca, dcb

        z = jnp.zeros((t, LANES), F32)
        zr = jnp.zeros((1, t), F32)
        carry = step(j, (z, z, z, z, zr, zr), True)
        dka, dkb, dva, dvb, dca, dcb = lax.fori_loop(j + 1, nq, lambda i, carry: step(i, carry, False), carry)
        dk_ref[...] = jnp.where(first, dka, dkb) * (1.0 / LOG2E)
        dv_ref[...] = jnp.where(first, dva, dvb)
        sub = lax.broadcasted_iota(jnp.int32, (8, t), 0)
        dct_ref[...] = jnp.where(sub == 0, dca, jnp.where(sub == 1, dcb, 0.0))

    full = lambda blk: pl.BlockSpec((s, LANES), blk)
    return pl.pallas_call(
        body, name=name, grid=(N_HEADS // 2, nq),
        in_specs=[full(lambda p, j: (0, qblk + p)),
                  pl.BlockSpec((t, LANES), lambda p, j: (j, qblk + 4 + p)),
                  pl.BlockSpec((t, LANES), lambda p, j: (j, qblk + 8 + p)),
                  full(lambda p, j: (0, 0)),
                  pl.BlockSpec((8, t), lambda p, j: (0, j)),
                  pl.BlockSpec((2, s, LANES), lambda p, j: (p, 0, 0)),
                  full(lambda p, j: (0, p)),
                  full(lambda p, j: (0, p))],
        out_specs=[full(lambda p, j: (0, p)),
                   pl.BlockSpec((t, LANES), lambda p, j: (j, p)),
                   pl.BlockSpec((t, LANES), lambda p, j: (j, p)),
                   pl.BlockSpec((None, 8, t), lambda p, j: (p, 0, j)),
                   pl.BlockSpec((None, s, LANES), lambda p, j: (p, 0, 0))],
        out_shape=[jax.ShapeDtypeStruct((s, D_ATT), F32), jax.ShapeDtypeStruct((s, D_ATT), F32),
                   jax.ShapeDtypeStruct((s, D_ATT), F32), jax.ShapeDtypeStruct((N_HEADS // 2, 8, s), F32),
                   jax.ShapeDtypeStruct((N_HEADS // 2, s, LANES), F32)],
        compiler_params=_params("arbitrary", "arbitrary"),
    )(h, h, h, c, ct, lse, att, datt)


def _sconv_fwd(h, w, bgblk, name):
    s = h.shape[0]
    nblk = D_CONV // LANES

    def body(bg_ref, cg_ref, hc_ref, w_ref, y_ref):
        rows = lax.broadcasted_iota(jnp.int32, (s, LANES), 0)
        y_ref[...] = (bg_ref[...] * _conv3(cg_ref[...] * hc_ref[...], w_ref, rows)).astype(BF16)

    col = lambda base: pl.BlockSpec((s, LANES), lambda j: (0, base + j))
    return pl.pallas_call(
        body, name=name, grid=(nblk,),
        in_specs=[col(bgblk), col(bgblk + nblk), col(bgblk + 2 * nblk), pl.BlockSpec((3, LANES), lambda j: (0, j))],
        out_specs=pl.BlockSpec((s, LANES), lambda j: (0, j)),
        out_shape=jax.ShapeDtypeStruct((s, D_CONV), BF16), compiler_params=_params("parallel"),
    )(h, h, h, w)


def _sconv_bwd(h, w, dy, bgblk, name):
    s = h.shape[0]
    nblk = D_CONV // LANES

    def body(bg_ref, cg_ref, hc_ref, w_ref, dy_ref, dbg_ref, dcg_ref, dhc_ref, dw_ref):
        rows = lax.broadcasted_iota(jnp.int32, (s, LANES), 0)
        cg, hc, dy, w = cg_ref[...], hc_ref[...], dy_ref[...], w_ref
        xin = cg * hc
        dbg_ref[...] = (dy * _conv3(xin, w, rows)).astype(BF16)
        dxin, dw_ref[...] = _conv3_bwd(dy * bg_ref[...], xin, w, rows)
        dcg_ref[...] = (dxin * hc).astype(BF16)
        dhc_ref[...] = (dxin * cg).astype(BF16)

    col = lambda base: pl.BlockSpec((s, LANES), lambda j: (0, base + j))
    return pl.pallas_call(
        body, name=name, grid=(nblk,),
        in_specs=[col(bgblk), col(bgblk + nblk), col(bgblk + 2 * nblk), pl.BlockSpec((3, LANES), lambda j: (0, j)), col(0)],
        out_specs=[col(0), col(0), col(0), pl.BlockSpec((8, LANES), lambda j: (0, j))],
        out_shape=[jax.ShapeDtypeStruct((s, D_CONV), BF16)] * 3 + [jax.ShapeDtypeStruct((8, D_CONV), F32)],
        compiler_params=_params("parallel"),
    )(h, h, h, w, dy)


def _sgu_group_masks():
    lane = lax.broadcasted_iota(jnp.int32, (1, D_SGU), 1)
    return [(lane // HEAD_DIM) == g for g in range(N_SGU_GROUPS)]


def _sgu_tril():
    r = lax.broadcasted_iota(jnp.int32, (SGU_CHUNK, SGU_CHUNK), 0)
    c = lax.broadcasted_iota(jnp.int32, (SGU_CHUNK, SGU_CHUNK), 1)
    return r >= c


def _sgu_fwd(h, ln_g, ln_b, w_s, b_full, ublk, name):
    s = h.shape[0]
    tr = _tile(s, 512)
    nch = tr // SGU_CHUNK

    def body(u_ref, v_ref, g_ref, b_ref, w_ref, bf_ref, y_ref):
        masks = _sgu_group_masks()
        tril = _sgu_tril()
        wm = [jnp.where(tril, w_ref[g], 0.0).astype(BF16) for g in range(N_SGU_GROUPS)]
        vn = _layer_norm(_gelu(v_ref[...]), g_ref[...], b_ref[...])
        for ch in range(nch):
            rows = pl.ds(ch * SGU_CHUNK, SGU_CHUNK)
            vc = vn[ch * SGU_CHUNK:(ch + 1) * SGU_CHUNK, :]
            mixed = bf_ref[...]
            for g in range(N_SGU_GROUPS):
                mixed = mixed + jnp.dot(wm[g], jnp.where(masks[g], vc, 0.0).astype(BF16), preferred_element_type=F32)
            y_ref[rows, :] = (_gelu(u_ref[rows, :]) * mixed).astype(BF16)

    row = lambda blk: pl.BlockSpec((tr, D_SGU), lambda i: (i, blk))
    vec = pl.BlockSpec((1, D_SGU), lambda i: (0, 0))
    return pl.pallas_call(
        body, name=name, grid=(s // tr,),
        in_specs=[row(ublk), row(ublk + 1), vec, vec,
                  pl.BlockSpec((N_SGU_GROUPS, SGU_CHUNK, SGU_CHUNK), lambda i: (0, 0, 0)),
                  pl.BlockSpec((SGU_CHUNK, D_SGU), lambda i: (0, 0))],
        out_specs=row(0), out_shape=jax.ShapeDtypeStruct((s, D_SGU), BF16), compiler_params=_params("parallel"),
    )(h, h, ln_g, ln_b, w_s, b_full)


def _sgu_bwd(h, ln_g, ln_b, w_s, b_full, dy, ublk, name):
    s = h.shape[0]
    tr = _tile(s, 512)
    nch = tr // SGU_CHUNK
    nt_dims = (((1,), (1,)), ((), ()))

    def norm(v, g, b):
        return _layer_norm(_gelu(v), g, b)

    def body(u_ref, v_ref, g_ref, b_ref, w_ref, bf_ref, dy_ref, du_ref, dv_ref, dg_ref, db_ref, dw_ref, dbf_ref):
        masks = _sgu_group_masks()
        tril = _sgu_tril()
        wf = [jnp.where(tril, w_ref[g], 0.0) for g in range(N_SGU_GROUPS)]
        wm = [w.astype(BF16) for w in wf]
        wmt = [jnp.transpose(w).astype(BF16) for w in wf]
        vn, vjp = jax.vjp(norm, v_ref[...], g_ref[...], b_ref[...])

        @pl.when(pl.program_id(0) == 0)
        def _():
            dg_ref[...] = jnp.zeros_like(dg_ref)
            db_ref[...] = jnp.zeros_like(db_ref)
            dw_ref[...] = jnp.zeros_like(dw_ref)
            dbf_ref[...] = jnp.zeros_like(dbf_ref)

        dvn_parts = []
        for ch in range(nch):
            rows = pl.ds(ch * SGU_CHUNK, SGU_CHUNK)
            vc = vn[ch * SGU_CHUNK:(ch + 1) * SGU_CHUNK, :]
            vc16 = vc.astype(BF16)
            mixed = bf_ref[...]
            for g in range(N_SGU_GROUPS):
                mixed = mixed + jnp.dot(wm[g], jnp.where(masks[g], vc, 0.0).astype(BF16), preferred_element_type=F32)
            dy = dy_ref[rows, :]
            ug, slope = _gelu_and_slope(u_ref[rows, :])
            du_ref[rows, :] = (dy * mixed * slope).astype(BF16)
            dmixed = dy * ug
            dbf_ref[...] += dmixed
            dvc = jnp.zeros((SGU_CHUNK, D_SGU), F32)
            for g in range(N_SGU_GROUPS):
                dm16 = jnp.where(masks[g], dmixed, 0.0).astype(BF16)
                dw_ref[g] += jnp.where(tril, lax.dot_general(dm16, vc16, nt_dims, preferred_element_type=F32), 0.0)
                dvc = dvc + jnp.dot(wmt[g], dm16, preferred_element_type=F32)
            dvn_parts.append(dvc)
        dv, dg, db = vjp(jnp.concatenate(dvn_parts, axis=0))
        dv_ref[...] = dv.astype(BF16)
        dg_ref[...] += dg
        db_ref[...] += db

    row = lambda blk: pl.BlockSpec((tr, D_SGU), lambda i: (i, blk))
    vec = pl.BlockSpec((1, D_SGU), lambda i: (0, 0))
    wsp = pl.BlockSpec((N_SGU_GROUPS, SGU_CHUNK, SGU_CHUNK), lambda i: (0, 0, 0))
    bsp = pl.BlockSpec((SGU_CHUNK, D_SGU), lambda i: (0, 0))
    return pl.pallas_call(
        body, name=name, grid=(s // tr,),
        in_specs=[row(ublk), row(ublk + 1), vec, vec, wsp, bsp, row(0)],
        out_specs=[row(0), row(0), vec, vec, wsp, bsp],
        out_shape=[jax.ShapeDtypeStruct((s, D_SGU), BF16), jax.ShapeDtypeStruct((s, D_SGU), BF16),
                   jax.ShapeDtypeStruct((1, D_SGU), F32), jax.ShapeDtypeStruct((1, D_SGU), F32),
                   jax.ShapeDtypeStruct((N_SGU_GROUPS, SGU_CHUNK, SGU_CHUNK), F32),
                   jax.ShapeDtypeStruct((SGU_CHUNK, D_SGU), F32)],
        compiler_params=_params("arbitrary"),
    )(h, h, ln_g, ln_b, w_s, b_full, dy)


def _merge_fwd(h, b_gate, att, yc, ys, wa, wc, ws, name):
    s, d = att.shape[0], wa.shape[1]
    tm, tn = _tile(s, 512), _tile(d, 512)
    nj = d // tn

    def body(g0_ref, g1_ref, g2_ref, bg_ref, a_ref, c_ref, s_ref, wa_ref, wc_ref, ws_ref, o_ref):
        acc = jax.nn.sigmoid(g0_ref[...] + bg_ref[0:1, :]) * jnp.dot(a_ref[...], wa_ref[...], preferred_element_type=F32)
        acc += jax.nn.sigmoid(g1_ref[...] + bg_ref[1:2, :]) * jnp.dot(c_ref[...], wc_ref[...], preferred_element_type=F32)
        acc += jax.nn.sigmoid(g2_ref[...] + bg_ref[2:3, :]) * jnp.dot(s_ref[...], ws_ref[...], preferred_element_type=F32)
        o_ref[...] = acc.astype(BF16)

    gate = lambda b: pl.BlockSpec((tm, tn), lambda j, i: (i, b * nj + j))
    act = lambda k: pl.BlockSpec((tm, k), lambda j, i: (i, 0))
    wgt = lambda k: pl.BlockSpec((k, tn), lambda j, i: (0, j))
    return pl.pallas_call(
        body, name=name, grid=(nj, s // tm),
        in_specs=[gate(0), gate(1), gate(2), pl.BlockSpec((3, tn), lambda j, i: (0, j)),
                  act(D_ATT), act(D_CONV), act(D_SGU), wgt(D_ATT), wgt(D_CONV), wgt(D_SGU)],
        out_specs=pl.BlockSpec((tm, tn), lambda j, i: (i, j)),
        out_shape=jax.ShapeDtypeStruct((s, d), BF16), compiler_params=_params("parallel", "arbitrary"),
    )(h, h, h, b_gate, att, yc, ys, wa, wc, ws)


def _merge_bwd(h, b_gate, att, yc, ys, wa, wc, ws, dm, name):
    s, d = att.shape[0], wa.shape[1]
    tm, tn = _tile(s, 512), _tile(d, 512)
    nj = d // tn

    def body(g0_ref, g1_ref, g2_ref, bg_ref, a_ref, c_ref, s_ref, wa_ref, wc_ref, ws_ref, dm_ref,
             dya_ref, dyc_ref, dys_ref, dgl_ref, dbg_ref):
        dm = dm_ref[...]
        sums = []
        for b, (g_ref, x_ref, w_ref, dy_ref) in enumerate(((g0_ref, a_ref, wa_ref, dya_ref), (g1_ref, c_ref, wc_ref, dyc_ref),
                                                          (g2_ref, s_ref, ws_ref, dys_ref))):
            gate = jax.nn.sigmoid(g_ref[...] + bg_ref[b:b + 1, :])
            y = jnp.dot(x_ref[...], w_ref[...], preferred_element_type=F32)
            dy_ref[...] = (dm * gate).astype(BF16)
            dgl = dm * y * gate * (1.0 - gate)
            dgl_ref[b] = dgl.astype(BF16)
            sums.append(jnp.sum(dgl, axis=0, keepdims=True))
        sub = lax.broadcasted_iota(jnp.int32, (3, tn), 0)
        part = jnp.where(sub == 0, sums[0], jnp.where(sub == 1, sums[1], sums[2]))

        @pl.when(pl.program_id(1) == 0)
        def _():
            dbg_ref[...] = jnp.zeros_like(dbg_ref)

        dbg_ref[...] += part

    gate = lambda b: pl.BlockSpec((tm, tn), lambda j, i: (i, b * nj + j))
    act = lambda k: pl.BlockSpec((tm, k), lambda j, i: (i, 0))
    wgt = lambda k: pl.BlockSpec((k, tn), lambda j, i: (0, j))
    tile = pl.BlockSpec((tm, tn), lambda j, i: (i, j))
    return pl.pallas_call(
        body, name=name, grid=(nj, s // tm),
        in_specs=[gate(0), gate(1), gate(2), pl.BlockSpec((3, tn), lambda j, i: (0, j)),
                  act(D_ATT), act(D_CONV), act(D_SGU), wgt(D_ATT), wgt(D_CONV), wgt(D_SGU), tile],
        out_specs=[tile, tile, tile, pl.BlockSpec((3, tm, tn), lambda j, i: (0, i, j)),
                   pl.BlockSpec((3, tn), lambda j, i: (0, j))],
        out_shape=[jax.ShapeDtypeStruct((s, d), BF16)] * 3 + [jax.ShapeDtypeStruct((3, s, d), BF16),
                                                              jax.ShapeDtypeStruct((3, d), F32)],
        compiler_params=_params("parallel", "arbitrary"),
    )(h, h, h, b_gate, att, yc, ys, wa, wc, ws, dm)


def _ffn_act_fwd(hh, cw, name):
    s, dff = hh.shape[0], hh.shape[1] // 2
    nblk = dff // LANES

    def body(a_ref, b_ref, wa_ref, wb_ref, z_ref):
        rows = lax.broadcasted_iota(jnp.int32, (s, LANES), 0)
        z_ref[...] = (_gelu(_conv3(a_ref[...], wa_ref, rows)) * _conv3(b_ref[...], wb_ref, rows)).astype(BF16)

    col = lambda base: pl.BlockSpec((s, LANES), lambda j: (0, base + j))
    wsp = lambda base: pl.BlockSpec((3, LANES), lambda j: (0, base + j))
    return pl.pallas_call(
        body, name=name, grid=(nblk,), in_specs=[col(0), col(nblk), wsp(0), wsp(nblk)], out_specs=col(0),
        out_shape=jax.ShapeDtypeStruct((s, dff), BF16), compiler_params=_params("parallel"),
    )(hh, hh, cw, cw)


def _ffn_act_bwd(hh, cw, dz, name):
    s, dff = hh.shape[0], hh.shape[1] // 2
    nblk = dff // LANES

    def body(a_ref, b_ref, wa_ref, wb_ref, dz_ref, da_ref, db_ref, dwa_ref, dwb_ref):
        rows = lax.broadcasted_iota(jnp.int32, (s, LANES), 0)
        a, b, dz = a_ref[...], b_ref[...], dz_ref[...]
        ga, slope = _gelu_and_slope(_conv3(a, wa_ref, rows))
        da, dwa_ref[...] = _conv3_bwd(dz * _conv3(b, wb_ref, rows) * slope, a, wa_ref, rows)
        db, dwb_ref[...] = _conv3_bwd(dz * ga, b, wb_ref, rows)
        da_ref[...] = da.astype(BF16)
        db_ref[...] = db.astype(BF16)

    col = lambda base: pl.BlockSpec((s, LANES), lambda j: (0, base + j))
    wsp = lambda base: pl.BlockSpec((3, LANES), lambda j: (0, base + j))
    w8 = lambda base: pl.BlockSpec((8, LANES), lambda j: (0, base + j))
    return pl.pallas_call(
        body, name=name, grid=(nblk,), in_specs=[col(0), col(nblk), wsp(0), wsp(nblk), col(0)],
        out_specs=[col(0), col(0), w8(0), w8(0)],
        out_shape=[jax.ShapeDtypeStruct((s, dff), BF16)] * 2 + [jax.ShapeDtypeStruct((8, dff), F32)] * 2,
        compiler_params=_params("parallel"),
    )(hh, hh, cw, cw, dz)


ANY = pl.BlockSpec(memory_space=pl.ANY)


def _place():
    return lax.axis_index("x"), lax.axis_index("y"), lax.axis_index("c")


def _all_gather(arrs, name):
    n = len(arrs)

    def body(*refs):
        ins, outs = refs[:n], refs[n:2 * n]
        send_sems, recv_sems, local_sems = refs[2 * n:]
        x, y, c = _place()
        me, sibling = (x, y, c), (x, y, 1 - c)
        chips = [(1 - x, y), (x, 1 - y), (1 - x, 1 - y)]

        def slab(a, dev):
            return outs[a].at[4 * dev[0] + 2 * dev[1] + dev[2]]

        def copy(a, k, block, to, src=None):
            return pltpu.make_async_remote_copy(
                src_ref=slab(a, block) if src is None else src, dst_ref=slab(a, block),
                send_sem=send_sems.at[7 * a + k], recv_sem=recv_sems.at[7 * a + k], device_id=to, device_id_type=MESH)

        mine = [pltpu.make_async_copy(ins[a], slab(a, me), local_sems.at[a]) for a in range(n)]
        for cp in mine:
            cp.start()
        first = []
        for a in range(n):
            first.append(copy(a, 0, me, sibling, src=ins[a]))
            first += [copy(a, 1 + j, me, (*chip, c), src=ins[a]) for j, chip in enumerate(chips)]
        for cp in first:
            cp.start()
        passed = []
        for a in range(n):
            for j, chip in enumerate(chips):
                copy(a, 1 + j, (*chip, c), me).wait_recv()
                fwd = copy(a, 4 + j, (*chip, c), sibling)
                fwd.start()
                passed.append(fwd)
        for a in range(n):
            copy(a, 0, sibling, me).wait_recv()
            for j, chip in enumerate(chips):
                copy(a, 4 + j, (*chip, 1 - c), me).wait_recv()
        for cp in first + passed:
            cp.wait_send()
        for cp in mine:
            cp.wait()

    return pl.pallas_call(
        body, name=name, in_specs=[ANY] * n, out_specs=[ANY] * n,
        out_shape=[jax.ShapeDtypeStruct((N_DEV,) + a.shape, a.dtype) for a in arrs],
        scratch_shapes=[pltpu.SemaphoreType.DMA((7 * n,)), pltpu.SemaphoreType.DMA((7 * n,)), pltpu.SemaphoreType.DMA((n,))],
    )(*arrs)


HBM = pl.BlockSpec(memory_space=pltpu.HBM)
SEM = pl.BlockSpec(memory_space=pltpu.SEMAPHORE)
EFFECT = pltpu.SideEffectType.DATAFLOW_SIDE_EFFECTING


def _slot(dev):
    return 4 * dev[0] + 2 * dev[1] + dev[2]


def _exchange_copies(src_refs, land_refs, send_sems, recv_sems, src_view, land_view):
    x, y, c = _place()
    me = (x, y, c)
    peers = [(1 - x if r & 4 else x, 1 - y if r & 2 else y, 1 - c if r & 1 else c) for r in range(1, N_DEV)]
    sends, lands = [], []
    for a, (src, land) in enumerate(zip(src_refs, land_refs)):
        for k, peer in enumerate(peers):
            sems = dict(send_sem=send_sems.at[7 * a + k], recv_sem=recv_sems.at[7 * a + k], device_id=peer,
                        device_id_type=MESH)
            sends.append(pltpu.make_async_remote_copy(src_ref=src_view(src, _slot(peer)),
                                                      dst_ref=land_view(land, _slot(me)), **sems))
            lands.append(pltpu.make_async_remote_copy(src_ref=src_view(src, _slot(me)),
                                                      dst_ref=land_view(land, _slot(peer)), **sems))
    return sends, lands


def _own_copies(src_refs, land_refs, local_sems, src_view, land_view):
    me = _slot(_place())
    return [pltpu.make_async_copy(src_view(src, me), land_view(land, me), local_sems.at[a])
            for a, (src, land) in enumerate(zip(src_refs, land_refs))]


def _exchange_start(srcs, lands, after, src_view, land_view, name):
    n = len(srcs)

    def body(*refs):
        src_refs, land_refs = refs[:n], refs[n:2 * n]
        send_sems, recv_sems, local_sems = refs[2 * n + 1:2 * n + 4]
        token = refs[-1]
        sends, _ = _exchange_copies(src_refs, land_refs, send_sems, recv_sems, src_view, land_view)
        for cp in sends + _own_copies(src_refs, land_refs, local_sems, src_view, land_view):
            cp.start()
        token[...] = jnp.zeros_like(token)

    thru = [pltpu.HBM(a.shape, a.dtype) for a in list(srcs) + list(lands)]
    outs = pl.pallas_call(
        body, name=name,
        out_shape=(pltpu.SemaphoreType.DMA((7 * n,)), pltpu.SemaphoreType.DMA((7 * n,)), pltpu.SemaphoreType.DMA((n,)),
                   *thru, jax.ShapeDtypeStruct((8, LANES), F32)),
        in_specs=[HBM] * (2 * n) + [ANY],
        out_specs=(SEM, SEM, SEM, *([HBM] * (2 * n)), pl.BlockSpec(memory_space=pltpu.VMEM)),
        input_output_aliases={i: 3 + i for i in range(2 * n)},
        compiler_params=pltpu.CompilerParams(has_side_effects=EFFECT),
    )(*[pltpu.with_memory_space_constraint(a, pltpu.HBM) for a in list(srcs) + list(lands)], after)
    return outs[:3], list(outs[3:3 + n]), list(outs[3 + n:3 + 2 * n]), outs[-1]


def _exchange_wait(sems, srcs, lands, after, src_view, land_view, name):
    n = len(srcs)

    def body(*refs):
        src_refs, land_refs = refs[:n], refs[n:2 * n]
        send_sems, recv_sems, local_sems = refs[2 * n:2 * n + 3]
        sends, landed = _exchange_copies(src_refs, land_refs, send_sems, recv_sems, src_view, land_view)
        for cp in sends:
            cp.wait_send()
        for cp in landed:
            cp.wait_recv()
        for cp in _own_copies(src_refs, land_refs, local_sems, src_view, land_view):
            cp.wait()

    after = after if isinstance(after, (list, tuple)) else [after]
    outs = pl.pallas_call(
        body, name=name, out_shape=[pltpu.HBM(a.shape, a.dtype) for a in list(srcs) + list(lands)],
        in_specs=[HBM] * (2 * n) + [SEM, SEM, SEM] + [ANY] * len(after), out_specs=[HBM] * (2 * n),
        input_output_aliases={i: i for i in range(2 * n)},
        compiler_params=pltpu.CompilerParams(has_side_effects=EFFECT),
    )(*srcs, *lands, *sems, *after)
    return list(outs[:n]), list(outs[n:])


ADAMW_BLOCK_BYTES = 1 << 19
PACK_ROWS = 256


def _adamw(slabs, w, m, v, name):
    nl, r, c = w.shape
    row_edges = [r] + [t for t in range(8, r, 8) if r % t == 0]
    col_edges = [c] + [t for t in range(LANES, c, LANES) if c % t == 0]
    fits = [(tr * tc, tc, tr) for tr in row_edges for tc in col_edges if tr * tc * 4 <= ADAMW_BLOCK_BYTES]
    _, tc, tr = max(fits) if fits else (0, min(col_edges), min(row_edges))

    def body(s_ref, w_ref, m_ref, v_ref, g_ref, d_ref, nm_ref, nv_ref):
        g = s_ref[0].astype(F32)
        for q in range(1, N_DEV):
            g = g + s_ref[q].astype(F32)
        m_new = ADAM_B1 * m_ref[...] + (1.0 - ADAM_B1) * g
        v_new = ADAM_B2 * v_ref[...] + (1.0 - ADAM_B2) * (g * g)
        m_hat = m_new / (1.0 - ADAM_B1 ** ADAM_STEP)
        v_hat = v_new / (1.0 - ADAM_B2 ** ADAM_STEP)
        g_ref[...] = g
        d_ref[...] = -ADAM_LR * (m_hat / (jnp.sqrt(v_hat) + ADAM_EPS) + ADAM_WD * w_ref[...])
        nm_ref[...] = m_new
        nv_ref[...] = v_new

    blk = pl.BlockSpec((None, tr, tc), lambda l, i, j: (l, i, j))
    return pl.pallas_call(
        body, name=name, grid=(nl, r // tr, c // tc),
        in_specs=[pl.BlockSpec((N_DEV, None, tr, tc), lambda l, i, j: (0, l, i, j)), blk, blk, blk],
        out_specs=[blk] * 4, out_shape=[jax.ShapeDtypeStruct(w.shape, F32)] * 4,
        compiler_params=_params("parallel", "parallel", "parallel"),
    )(slabs, w, m, v)


def _layout(d):
    off = {"gate": 0, "q": 3 * d}
    off["bg"] = off["q"] + 3 * D_ATT
    off["u"] = off["bg"] + 3 * D_CONV
    off["f"] = off["u"] + 2 * D_SGU
    width = -(-(off["f"] + LANES) // 512) * 512
    return off, width


def _pad_w_in(wt, d, token):
    off, width = _layout(d)
    nqkv, nrest = 3 * D_ATT, 3 * D_CONV + 2 * D_SGU
    pad = jnp.zeros((width - off["f"] - N_HEADS, wt.shape[1]), wt.dtype) + token[0, 0].astype(wt.dtype)
    return jnp.concatenate([wt[nqkv + N_HEADS + nrest:], wt[:nqkv], wt[nqkv + N_HEADS:nqkv + N_HEADS + nrest],
                            wt[nqkv:nqkv + N_HEADS], pad], axis=0)


def _unpad_w_in(wtp, d):
    off, _ = _layout(d)
    return jnp.concatenate([wtp[off["q"]:off["bg"]], wtp[off["f"]:off["f"] + N_HEADS], wtp[off["bg"]:off["f"]],
                            wtp[:off["q"]]], axis=0)


def _cols_from_slabs(g):
    return jnp.transpose(g, (1, 0, 2)).reshape(g.shape[1], N_DEV * g.shape[2])


def _cols_to_slabs(w):
    r, c = w.shape[0], w.shape[1] // N_DEV
    return jnp.transpose(w.reshape(r, N_DEV, c), (1, 0, 2))


def kernel(x, pre_mix_g, post_mix_g, pre_ffn_g, post_ffn_g, w_in, b_forget, b_gate, conv_mix_w, sgu_ln_g, sgu_ln_b, sgu_w, sgu_b, w_branch_att, w_branch_conv, w_branch_sgu, w_out, w_ffn_up, conv_ffn_w, w_ffn_down, loss_target, m_pre_mix_g, m_post_mix_g, m_pre_ffn_g, m_post_ffn_g, m_w_in, m_b_forget, m_b_gate, m_conv_mix_w, m_sgu_ln_g, m_sgu_ln_b, m_sgu_w, m_sgu_b, m_w_branch_att, m_w_branch_conv, m_w_branch_sgu, m_w_out, m_w_ffn_up, m_conv_ffn_w, m_w_ffn_down, v_pre_mix_g, v_post_mix_g, v_pre_ffn_g, v_post_ffn_g, v_w_in, v_b_forget, v_b_gate, v_conv_mix_w, v_sgu_ln_g, v_sgu_ln_b, v_sgu_w, v_sgu_b, v_w_branch_att, v_w_branch_conv, v_w_branch_sgu, v_w_out, v_w_ffn_up, v_conv_ffn_w, v_w_ffn_down):
    depth = w_in.shape[0]
    s, d = x.shape[1], x.shape[2]
    dff = w_ffn_down.shape[1] * N_DEV
    off, _ = _layout(d)
    qblk, bgblk, ublk, fblk = off["q"] // LANES, off["bg"] // LANES, off["u"] // D_SGU, off["f"] // LANES
    x0 = x.reshape(s, d)
    target = loss_target.reshape(s, d)
    ncm, ncf = conv_mix_w.shape[2], conv_ffn_w.shape[2]

    lo = d // N_DEV

    whole = lambda ref, slot: ref
    slab = lambda ref, slot: ref.at[slot]

    w_in_t, w_up_t = jnp.transpose(w_in, (0, 2, 1)), jnp.transpose(w_ffn_up, (0, 2, 1))

    def shards_of(l, part):
        if part == "mix":
            small = jnp.concatenate([b_gate[l], conv_mix_w[l], conv_ffn_w[l]], axis=1)
            return [w_in_t[l].astype(BF16), w_branch_att[l].astype(BF16), w_branch_conv[l].astype(BF16),
                    w_branch_sgu[l].astype(BF16), w_out[l].astype(BF16), small]
        return [w_up_t[l].astype(BF16), w_ffn_down[l].astype(BF16)]

    def gather_start(l, part, after):
        shards = shards_of(l, part)
        lands = [lax.empty((N_DEV,) + a.shape, a.dtype) for a in shards]
        return _exchange_start(shards, lands, after, whole, slab, name=f"gather_start_{part}_{l}")

    def gather_finish(l, part, started, after):
        sems, shards, lands, _ = started[part]
        shards, lands = _exchange_wait(sems, shards, lands, after, whole, slab, name=f"gather_wait_{part}_{l}")
        token = jnp.zeros((8, LANES), F32)
        if l + 1 < depth:
            started[part] = gather_start(l + 1, part, lands[0])
            token = started[part][3]
        return lands, token

    def bfull(l):
        return jnp.repeat(jnp.transpose(sgu_b[l]), HEAD_DIM, axis=1)

    def bf_pad(l):
        return jnp.pad(b_forget[l], (0, LANES - N_HEADS)).reshape(1, LANES)

    saved = []
    weights = []
    xin = x0
    xn = _prenorm(x0, pre_mix_g[0:1], name="prenorm_first")
    loss_acc = dy = None
    first = _all_gather(shards_of(0, "mix"), name="gather_first")
    started = {"ffn": gather_start(0, "ffn", first[0])}
    if depth > 1:
        started["mix"] = gather_start(1, "mix", started["ffn"][3])
    for l in range(depth):
        if l == 0:
            (g_in, g_a, g_c, g_s, g_o, g_small), token = first, started["mix" if depth > 1 else "ffn"][3]
        else:
            (g_in, g_a, g_c, g_s, g_o, g_small), token = gather_finish(l, "mix", started, xin)
        g_small = _cols_from_slabs(g_small).reshape(3, N_DEV, -1)
        w = dict(w_in=_pad_w_in(g_in.reshape(N_DEV * g_in.shape[1], d), d, token), wa=_cols_from_slabs(g_a),
                 wc=_cols_from_slabs(g_c), ws=_cols_from_slabs(g_s), w_out=g_o.reshape(d, d),
                 b_gate=g_small[:, :, :lo].reshape(3, d), cmw=g_small[:, :, lo:lo + ncm].reshape(3, D_CONV),
                 cfw=g_small[:, :, lo + ncm:].reshape(3, 2 * dff))
        weights.append(w)
        h = _mm(xn, w["w_in"], "nt", F32, name="proj_in")
        c, ct = _forget_prep(h, bf_pad(l), fblk, name="forget_prep")
        att, lse = _attention_fwd(h, c, ct, qblk, name="attention_fwd")
        yc = _sconv_fwd(h, w["cmw"], bgblk, name="sconv_fwd")
        ys = _sgu_fwd(h, sgu_ln_g[l:l + 1], sgu_ln_b[l:l + 1], sgu_w[l], bfull(l), ublk, name="sgu_fwd")
        merged = _merge_fwd(h, w["b_gate"], att, yc, ys, w["wa"], w["wc"], w["ws"], name="merge_fwd")
        o = _mm(merged, w["w_out"], "nn", F32, name="proj_out")
        (g_up, g_dn), token = gather_finish(l, "ffn", started, o)
        w["w_up"], w["w_dn"] = g_up.reshape(2 * dff, d), g_dn.reshape(dff, d)
        x1, xn2 = _postnorm(xin, o, post_mix_g[l:l + 1], pre_ffn_g[l:l + 1] + token[0, 0], name="postnorm_mix")
        hh = _mm(xn2, w["w_up"], "nt", F32, name="ffn_up")
        z = _ffn_act_fwd(hh, w["cfw"], name="ffn_act_fwd")
        f = _mm(z, w["w_dn"], "nn", F32, name="ffn_down")
        saved.append(dict(xin=xin, xn=xn, h=h, c=c, ct=ct, lse=lse, att=att, yc=yc, ys=ys, merged=merged, o=o, x1=x1,
                          xn2=xn2, hh=hh, z=z, f=f))
        if l + 1 < depth:
            xin, xn = _postnorm(x1, f, post_ffn_g[l:l + 1], pre_mix_g[l + 1:l + 2], name="postnorm_ffn")
        else:
            dy, loss_acc = _postnorm_loss(x1, f, post_ffn_g[l:l + 1], target, name="postnorm_loss")
    loss = lax.psum(loss_acc[0, 0] * (0.5 / d), ("x", "y", "c"))

    rep = {k: [None] * depth for k in ("pre_mix_g", "post_mix_g", "pre_ffn_g", "post_ffn_g", "b_forget", "sgu_ln_g",
                                       "sgu_ln_b", "sgu_w", "sgu_b")}
    nsmall = lo + ncm + ncf
    lands = {"win": [lax.empty((N_DEV, depth) + w_in_t.shape[1:], BF16)],
             "mid": [lax.empty((N_DEV, depth) + shp, dt) for shp, dt in (
                 (w_branch_att.shape[1:], BF16), (w_branch_conv.shape[1:], BF16), (w_branch_sgu.shape[1:], BF16),
                 (w_out.shape[1:], BF16), ((3, nsmall), F32))],
             "ffn": [lax.empty((N_DEV, depth) + shp, BF16) for shp in (w_up_t.shape[1:], w_ffn_down.shape[1:])]}
    scatters = {part: [None] * depth for part in lands}

    def scatter_start(l, part, sends, after):
        layer_slab = lambda ref, slot: ref.at[slot, l]
        sems, sends, lands[part], token = _exchange_start(sends, lands[part], after, slab, layer_slab,
                                                          name=f"scatter_start_{part}_{l}")
        scatters[part][l] = (sems, sends, layer_slab)
        return token

    def scatter_finish(part, after):
        for l in range(depth):
            sems, sends, layer_slab = scatters[part][l]
            _, lands[part] = _exchange_wait(sems, sends, lands[part], after, slab, layer_slab,
                                            name=f"scatter_wait_{part}_{l}")
        return lands[part]

    token = jnp.zeros((8, LANES), F32)
    dx = dy
    for l in reversed(range(depth)):
        w, a = weights[l], saved[l]
        df, rep["post_ffn_g"][l], dz = _postnorm_bwd_mm(a["f"], post_ffn_g[l:l + 1] + token[0, 0], dx, w["w_dn"],
                                                        name="ffn_down_dx")
        g_dn = _mm(a["z"], df, "tn", BF16, name="ffn_down_dw")
        dha, dhb, dcwa, dcwb = _ffn_act_bwd(a["hh"], w["cfw"], dz, name="ffn_act_bwd")
        dhh = jnp.concatenate([dha, dhb], axis=1)
        dcfw = jnp.concatenate([dcwa[0:3], dcwb[0:3]], axis=1)
        g_up = _mm(dhh, a["xn2"], "tn", BF16, name="ffn_up_dw")
        token = scatter_start(l, "ffn", [g_up.reshape(N_DEV, 2 * dff // N_DEV, d), g_dn.reshape(N_DEV, dff // N_DEV, d)],
                              dz)
        dx1, rep["pre_ffn_g"][l] = _mm_prenorm_bwd(dhh, w["w_up"], a["x1"], pre_ffn_g[l:l + 1], dx, token, name="ffn_up_dx")
        do, rep["post_mix_g"][l], dmerged = _postnorm_bwd_mm(a["o"], post_mix_g[l:l + 1], dx1, w["w_out"], name="proj_out_dx")
        g_o = _mm(a["merged"], do, "tn", BF16, name="proj_out_dw")
        dya, dyc, dys, dgl, dbg = _merge_bwd(a["h"], w["b_gate"], a["att"], a["yc"], a["ys"], w["wa"], w["wc"], w["ws"],
                                             dmerged, name="merge_bwd")
        dconv = _mm(dyc, w["wc"], "nt", F32, name="branch_conv_dx")
        dsgu = _mm(dys, w["ws"], "nt", F32, name="branch_sgu_dx")
        g_a = _mm(a["att"], dya, "tn", BF16, name="branch_att_dw")
        g_c = _mm(a["yc"], dyc, "tn", BF16, name="branch_conv_dw")
        g_s = _mm(a["ys"], dys, "tn", BF16, name="branch_sgu_dw")
        dbgate, dcg, dhc, dcmw = _sconv_bwd(a["h"], w["cmw"], dconv, bgblk, name="sconv_bwd")
        sends = [_cols_to_slabs(g_a), _cols_to_slabs(g_c), _cols_to_slabs(g_s), g_o.reshape(N_DEV, d // N_DEV, d),
                 jnp.concatenate([_cols_to_slabs(dbg), _cols_to_slabs(dcmw[0:3]), _cols_to_slabs(dcfw)], axis=2)]
        token = scatter_start(l, "mid", sends, dx1)
        datt = _mm(dya, w["wa"], "nt", F32, name="branch_att_dx", after=token)
        dq, dk, dv, dct4, dcq4 = _attention_bwd(a["h"], a["c"], a["ct"], a["lse"], a["att"], datt, qblk,
                                                name="attention_bwd")
        dfl, dbf = _forget_prep_bwd(a["h"], bf_pad(l), dct4, dcq4, fblk, name="forget_prep_bwd")
        rep["b_forget"][l] = dbf[0, :N_HEADS]
        du, dvs, dlg, dlb, dsw, dbfull = _sgu_bwd(a["h"], sgu_ln_g[l:l + 1], sgu_ln_b[l:l + 1], sgu_w[l], bfull(l), dsgu,
                                                  ublk, name="sgu_bwd")
        rep["sgu_ln_g"][l], rep["sgu_ln_b"][l], rep["sgu_w"][l] = dlg, dlb, dsw
        rep["sgu_b"][l] = jnp.transpose(jnp.sum(dbfull.reshape(SGU_CHUNK, N_SGU_GROUPS, HEAD_DIM), axis=2))
        dh = jnp.concatenate([dgl[0], dgl[1], dgl[2], dq.astype(BF16), dk.astype(BF16), dv.astype(BF16), dbgate, dcg, dhc,
                              du, dvs, dfl.astype(BF16), jnp.zeros((s, w["w_in"].shape[0] - off["f"] - LANES), BF16)], axis=1)
        g_in = _mm(dh, a["xn"], "tn", BF16, name="proj_in_dw")
        token = scatter_start(l, "win", [_unpad_w_in(g_in, d).reshape(N_DEV, -1, d)], dx1)
        dx, rep["pre_mix_g"][l] = _mm_prenorm_bwd(dh, w["w_in"], a["xin"], pre_mix_g[l:l + 1], dx1, token, name="proj_in_dx")

    outs = {}
    t3 = lambda arr: jnp.transpose(arr, (0, 2, 1))

    def update(name_, slabs, w_, m_, v_, transposed=False):
        if transposed:
            w_, m_, v_ = t3(w_), t3(m_), t3(v_)
        shp = w_.shape
        w3 = w_.reshape((shp[0], -1, shp[-1])) if w_.ndim >= 3 else w_.reshape((1,) + shp)
        res = _adamw(slabs.reshape((N_DEV,) + w3.shape), w3, m_.reshape(w3.shape), v_.reshape(w3.shape),
                     name="adamw_" + name_)
        outs[name_] = tuple(t3(t.reshape(shp)) if transposed else t.reshape(shp) for t in res)
        return res[0]

    rep_names = ("pre_mix_g", "post_mix_g", "pre_ffn_g", "post_ffn_g", "b_forget", "sgu_ln_g", "sgu_ln_b", "sgu_w", "sgu_b")
    rep_w = dict(pre_mix_g=(pre_mix_g, m_pre_mix_g, v_pre_mix_g), post_mix_g=(post_mix_g, m_post_mix_g, v_post_mix_g),
                 pre_ffn_g=(pre_ffn_g, m_pre_ffn_g, v_pre_ffn_g), post_ffn_g=(post_ffn_g, m_post_ffn_g, v_post_ffn_g),
                 b_forget=(b_forget, m_b_forget, v_b_forget), sgu_ln_g=(sgu_ln_g, m_sgu_ln_g, v_sgu_ln_g),
                 sgu_ln_b=(sgu_ln_b, m_sgu_ln_b, v_sgu_ln_b), sgu_w=(sgu_w, m_sgu_w, v_sgu_w), sgu_b=(sgu_b, m_sgu_b, v_sgu_b))

    def pack(parts):
        rows = [jnp.pad(p.reshape(-1), (0, -p.size % LANES)).reshape(-1, LANES) for p in parts]
        rows = jnp.concatenate(rows, axis=0)
        return jnp.pad(rows, ((0, -rows.shape[0] % PACK_ROWS), (0, 0)))

    part = pack([jnp.stack([g.reshape(rep_w[k][0].shape[1:]) for g in rep[k]]) for k in rep_names])
    small_sems, small_src, small_land, _ = _exchange_start([part], [lax.empty((N_DEV,) + part.shape, F32)], dx, whole, slab,
                                                            name="gather_small_start")

    got_up, got_dn = scatter_finish("ffn", dx)
    done = [update("w_ffn_up", got_up, w_ffn_up, m_w_ffn_up, v_w_ffn_up, transposed=True),
            update("w_ffn_down", got_dn, w_ffn_down, m_w_ffn_down, v_w_ffn_down)]

    _, (gathered,) = _exchange_wait(small_sems, small_src, small_land, done, whole, slab, name="gather_small_wait")
    packed = [pack([rep_w[k][i] for k in rep_names]) for i in range(3)]
    res = _adamw(gathered.reshape(N_DEV, 1, -1, LANES), *[p.reshape(1, -1, LANES) for p in packed], name="adamw_replicated")
    row = 0
    for k in rep_names:
        shp = rep_w[k][0].shape
        size = math.prod(shp)
        nrows = -(-size // LANES)
        outs[k] = tuple(t[0, row:row + nrows].reshape(-1)[:size].reshape(shp) for t in res)
        row += nrows

    got_a, got_c, got_s, got_o, small = scatter_finish("mid", res[0])
    done = [update("w_branch_att", got_a, w_branch_att, m_w_branch_att, v_w_branch_att),
            update("w_branch_conv", got_c, w_branch_conv, m_w_branch_conv, v_w_branch_conv),
            update("w_branch_sgu", got_s, w_branch_sgu, m_w_branch_sgu, v_w_branch_sgu),
            update("w_out", got_o, w_out, m_w_out, v_w_out),
            update("b_gate", small[..., :lo], b_gate, m_b_gate, v_b_gate),
            update("conv_mix_w", small[..., lo:lo + ncm], conv_mix_w, m_conv_mix_w, v_conv_mix_w),
            update("conv_ffn_w", small[..., lo + ncm:], conv_ffn_w, m_conv_ffn_w, v_conv_ffn_w)]
    (got_in,) = scatter_finish("win", done)
    update("w_in", got_in, w_in, m_w_in, v_w_in, transposed=True)

    order = ("pre_mix_g", "post_mix_g", "pre_ffn_g", "post_ffn_g", "w_in", "b_forget", "b_gate", "conv_mix_w", "sgu_ln_g",
             "sgu_ln_b", "sgu_w", "sgu_b", "w_branch_att", "w_branch_conv", "w_branch_sgu", "w_out", "w_ffn_up",
             "conv_ffn_w", "w_ffn_down")
    grad_x = dx.reshape(x.shape)
    return (loss, grad_x, *[outs[k][0] for k in order], *[outs[k][1] for k in order], *[outs[k][2] for k in order],
            *[outs[k][3] for k in order])
```

```python
import functools
import math

import jax
import jax.numpy as jnp
from jax import lax
from jax.experimental import pallas as pl
from jax.experimental.pallas import tpu as pltpu

F32 = jnp.float32
BF16 = jnp.bfloat16

N_DEV = 8
HEAD_DIM = 64
N_HEADS = 8
D_ATT = 512
D_CONV = 256
D_SGU = 256
N_SGU_GROUPS = 4
SGU_CHUNK = 128
RMS_EPS = 1e-6
LN_EPS = 1e-5
ADAM_LR = 0.001
ADAM_B1 = 0.9
ADAM_B2 = 0.999
ADAM_EPS = 1e-08
ADAM_WD = 0.01
ADAM_STEP = 10
LANES = 128
VMEM_LIMIT = 56 * 1024 * 1024
ATT_TILE = 512
LOG2E = math.log2(math.e)
NEG = -1e30
MESH = pl.DeviceIdType.MESH


def _params(*sem):
    return pltpu.CompilerParams(dimension_semantics=sem if sem else None, vmem_limit_bytes=VMEM_LIMIT)


def _tile(n, cap):
    if n <= cap:
        return n
    t = cap - cap % LANES
    while n % t:
        t -= LANES
    return t


def _gelu(x):
    return 0.5 * x * (1.0 + jnp.tanh(math.sqrt(2.0 / math.pi) * (x + 0.044715 * (x * x * x))))


def _gelu_and_slope(x):
    k0, k1 = math.sqrt(2.0 / math.pi), 0.044715
    x2 = x * x
    t = jnp.tanh(x * (k0 + (k0 * k1) * x2))
    half = 0.5 * (1.0 + t)
    return x * half, half + (0.5 * x) * (1.0 - t * t) * (k0 + (3.0 * k0 * k1) * x2)


def _rms(x, g):
    r = lax.rsqrt(jnp.mean(x * x, axis=-1, keepdims=True) + RMS_EPS)
    return x * r * g


def _layer_norm(x, g, b):
    mu = jnp.mean(x, axis=-1, keepdims=True)
    xc = x - mu
    var = jnp.mean(xc * xc, axis=-1, keepdims=True)
    return xc * lax.rsqrt(var + LN_EPS) * g + b


def _shift_down(x, k, rows):
    return jnp.where(rows >= k, pltpu.roll(x, k, 0), 0.0)


def _shift_up(x, k, rows):
    s = x.shape[0]
    return jnp.where(rows < s - k, pltpu.roll(x, s - k, 0), 0.0)


def _conv3(x, w_ref, rows):
    return w_ref[2:3, :] * x + w_ref[1:2, :] * _shift_down(x, 1, rows) + w_ref[0:1, :] * _shift_down(x, 2, rows)


def _conv3_bwd(dy, x, w_ref, rows):
    up1, up2 = _shift_up(dy, 1, rows), _shift_up(dy, 2, rows)
    dx = w_ref[2:3, :] * dy + w_ref[1:2, :] * up1 + w_ref[0:1, :] * up2
    d2 = jnp.sum(dy * x, axis=0, keepdims=True)
    d1 = jnp.sum(up1 * x, axis=0, keepdims=True)
    d0 = jnp.sum(up2 * x, axis=0, keepdims=True)
    sub = lax.broadcasted_iota(jnp.int32, (8, x.shape[1]), 0)
    return dx, jnp.where(sub == 0, d0, jnp.where(sub == 1, d1, jnp.where(sub == 2, d2, 0.0)))


MM_VMEM_BUDGET = 40 * 1024 * 1024
MM_TILE_CAP = 1408


def _mm_tiles(m, n, k, out_bytes):
    def edges(d):
        return [t for t in range(LANES, min(d, MM_TILE_CAP) + 1, LANES) if d % t == 0] or [d]

    best = None
    for tm in edges(m):
        for tn in edges(n):
            if 2 * (2 * k * (tm + tn) + tm * tn * out_bytes) > MM_VMEM_BUDGET:
                continue
            for a_outer in (True, False):
                reads = k * m + (m // tm) * k * n if a_outer else k * n + (n // tn) * k * m
                traffic = 2 * reads + m * n * out_bytes
                key = (traffic, -tm * tn)
                if best is None or key < best[0]:
                    best = (key, (tm, tn, a_outer))
    return best[1]


def _mm(a, b, form, out_dtype, name, after=None):
    if form == "nn":
        (m, k), n = a.shape, b.shape[1]
    elif form == "nt":
        (m, k), n = a.shape, b.shape[0]
    else:
        (k, m), n = a.shape, b.shape[1]
    tm, tn, a_outer = _mm_tiles(m, n, k, jnp.dtype(out_dtype).itemsize)
    dims = {"nn": (((1,), (0,)), ((), ())), "nt": (((1,), (1,)), ((), ())), "tn": (((0,), (0,)), ((), ()))}[form]

    def body(a_ref, b_ref, *rest):
        o_ref = rest[-1]
        o_ref[...] = lax.dot_general(a_ref[...], b_ref[...], dims, preferred_element_type=F32).astype(o_ref.dtype)

    ij = (lambda g0, g1: (g0, g1)) if a_outer else (lambda g0, g1: (g1, g0))
    a_spec = (pl.BlockSpec((k, tm), lambda g0, g1: (0, ij(g0, g1)[0])) if form == "tn"
              else pl.BlockSpec((tm, k), lambda g0, g1: (ij(g0, g1)[0], 0)))
    b_spec = (pl.BlockSpec((tn, k), lambda g0, g1: (ij(g0, g1)[1], 0)) if form == "nt"
              else pl.BlockSpec((k, tn), lambda g0, g1: (0, ij(g0, g1)[1])))
    extra = [] if after is None else [pl.BlockSpec((8, LANES), lambda g0, g1: (0, 0))]
    return pl.pallas_call(
        body, name=name, grid=(m // tm, n // tn) if a_outer else (n // tn, m // tm),
        in_specs=[a_spec, b_spec] + extra, out_specs=pl.BlockSpec((tm, tn), lambda g0, g1: ij(g0, g1)),
        out_shape=jax.ShapeDtypeStruct((m, n), out_dtype),
        compiler_params=_params("parallel", "arbitrary"),
    )(a, b, *([] if after is None else [after]))


def _postnorm_bwd_mm(o, g, dx, b, name):
    s, d = o.shape
    n = b.shape[0]
    tm, tn = _tile(s, 512), _tile(n, MM_TILE_CAP)

    def body(o_ref, g_ref, dx_ref, b_ref, do_ref, dg_ref, out_ref):
        i, j = pl.program_id(0), pl.program_id(1)

        @pl.when(j == 0)
        def _():
            _, vjp = jax.vjp(_rms, o_ref[...], g_ref[...])
            d_o, dg = vjp(dx_ref[...])
            do_ref[...] = d_o.astype(BF16)

            @pl.when(i == 0)
            def _():
                dg_ref[...] = jnp.zeros_like(dg_ref)

            dg_ref[...] += dg

        out_ref[...] = lax.dot_general(do_ref[...], b_ref[...], (((1,), (1,)), ((), ())), preferred_element_type=F32)

    row = pl.BlockSpec((tm, d), lambda i, j: (i, 0))
    vec = pl.BlockSpec((1, d), lambda i, j: (0, 0))
    return pl.pallas_call(
        body, name=name, grid=(s // tm, n // tn),
        in_specs=[row, vec, row, pl.BlockSpec((tn, d), lambda i, j: (j, 0))],
        out_specs=[row, vec, pl.BlockSpec((tm, tn), lambda i, j: (i, j))],
        out_shape=[jax.ShapeDtypeStruct((s, d), BF16), jax.ShapeDtypeStruct((1, d), F32), jax.ShapeDtypeStruct((s, n), F32)],
        compiler_params=_params("arbitrary", "arbitrary"),
    )(o, g, dx, b)


def _mm_prenorm_bwd(a, b, x, g, dres, after, name):
    s, k = a.shape
    d = b.shape[1]
    tm = _tile(s, 512)

    def body(a_ref, b_ref, x_ref, g_ref, dres_ref, after_ref, dx_ref, dg_ref):
        dxn = jnp.dot(a_ref[...], b_ref[...], preferred_element_type=F32)
        _, vjp = jax.vjp(_rms, x_ref[...], g_ref[...])
        dx, dg = vjp(dxn)
        dx_ref[...] = dres_ref[...] + dx

        @pl.when(pl.program_id(0) == 0)
        def _():
            dg_ref[...] = jnp.zeros_like(dg_ref)

        dg_ref[...] += dg

    row = pl.BlockSpec((tm, d), lambda i: (i, 0))
    vec = pl.BlockSpec((1, d), lambda i: (0, 0))
    return pl.pallas_call(
        body, name=name, grid=(s // tm,),
        in_specs=[pl.BlockSpec((tm, k), lambda i: (i, 0)),
                  pl.BlockSpec((k, d), lambda i: (0, 0), pipeline_mode=pl.Buffered(1)),
                  row, vec, row, pl.BlockSpec((8, LANES), lambda i: (0, 0))],
        out_specs=[row, vec], out_shape=[jax.ShapeDtypeStruct((s, d), F32), jax.ShapeDtypeStruct((1, d), F32)],
        compiler_params=_params("arbitrary"),
    )(a, b, x, g, dres, after)


def _prenorm(x, g, name):
    s, d = x.shape
    tm = _tile(s, 512)

    def body(x_ref, g_ref, o_ref):
        o_ref[...] = _rms(x_ref[...], g_ref[...]).astype(BF16)

    return pl.pallas_call(
        body, name=name, grid=(s // tm,),
        in_specs=[pl.BlockSpec((tm, d), lambda i: (i, 0)), pl.BlockSpec((1, d), lambda i: (0, 0))],
        out_specs=pl.BlockSpec((tm, d), lambda i: (i, 0)),
        out_shape=jax.ShapeDtypeStruct((s, d), BF16), compiler_params=_params("parallel"),
    )(x, g)


def _mm_postnorm(a, b, x, g_post, g_next, name):
    s, k = a.shape
    d = b.shape[1]
    tm = _tile(s, 512)

    def body(a_ref, b_ref, x_ref, gp_ref, gn_ref, o_ref, x1_ref, xn_ref):
        o = jnp.dot(a_ref[...], b_ref[...], preferred_element_type=F32)
        o_ref[...] = o
        x1 = x_ref[...] + _rms(o, gp_ref[...])
        x1_ref[...] = x1
        xn_ref[...] = _rms(x1, gn_ref[...]).astype(BF16)

    row = pl.BlockSpec((tm, d), lambda i: (i, 0))
    vec = pl.BlockSpec((1, d), lambda i: (0, 0))
    return pl.pallas_call(
        body, name=name, grid=(s // tm,),
        in_specs=[pl.BlockSpec((tm, k), lambda i: (i, 0)),
                  pl.BlockSpec((k, d), lambda i: (0, 0), pipeline_mode=pl.Buffered(1)), row, vec, vec],
        out_specs=[row, row, row],
        out_shape=[jax.ShapeDtypeStruct((s, d), F32), jax.ShapeDtypeStruct((s, d), F32), jax.ShapeDtypeStruct((s, d), BF16)],
        compiler_params=_params("parallel"),
    )(a, b, x, g_post, g_next)


def _mm_postnorm_loss(a, b, x, g_post, target, name):
    s, k = a.shape
    d = b.shape[1]
    tm = _tile(s, 512)

    def body(a_ref, b_ref, x_ref, gp_ref, t_ref, o_ref, dy_ref, acc_ref):
        o = jnp.dot(a_ref[...], b_ref[...], preferred_element_type=F32)
        o_ref[...] = o
        e = x_ref[...] + _rms(o, gp_ref[...]) - t_ref[...]
        dy_ref[...] = e / d

        @pl.when(pl.program_id(0) == 0)
        def _():
            acc_ref[...] = jnp.zeros_like(acc_ref)

        acc_ref[...] += jnp.sum(jnp.sum(e * e, axis=1, keepdims=True), axis=0, keepdims=True)

    row = pl.BlockSpec((tm, d), lambda i: (i, 0))
    return pl.pallas_call(
        body, name=name, grid=(s // tm,),
        in_specs=[pl.BlockSpec((tm, k), lambda i: (i, 0)),
                  pl.BlockSpec((k, d), lambda i: (0, 0), pipeline_mode=pl.Buffered(1)), row,
                  pl.BlockSpec((1, d), lambda i: (0, 0)), row],
        out_specs=[row, row, pl.BlockSpec((1, LANES), lambda i: (0, 0))],
        out_shape=[jax.ShapeDtypeStruct((s, d), F32), jax.ShapeDtypeStruct((s, d), F32), jax.ShapeDtypeStruct((1, LANES), F32)],
        compiler_params=_params("arbitrary"),
    )(a, b, x, g_post, target)


def _postnorm_bwd(o, g, dx, name):
    s, d = o.shape
    tm = _tile(s, 512)

    def body(o_ref, g_ref, dx_ref, do_ref, dg_ref):
        _, vjp = jax.vjp(_rms, o_ref[...], g_ref[...])
        d_o, dg = vjp(dx_ref[...])
        do_ref[...] = d_o.astype(BF16)

        @pl.when(pl.program_id(0) == 0)
        def _():
            dg_ref[...] = jnp.zeros_like(dg_ref)

        dg_ref[...] += dg

    row = pl.BlockSpec((tm, d), lambda i: (i, 0))
    vec = pl.BlockSpec((1, d), lambda i: (0, 0))
    return pl.pallas_call(
        body, name=name, grid=(s // tm,), in_specs=[row, vec, row], out_specs=[row, vec],
        out_shape=[jax.ShapeDtypeStruct((s, d), BF16), jax.ShapeDtypeStruct((1, d), F32)],
        compiler_params=_params("arbitrary"),
    )(o, g, dx)


def _log_sigmoid(z):
    return jnp.minimum(z, 0.0) - jnp.log(1.0 + jnp.exp(-jnp.abs(z)))


def _forget_prep(h, bf_pad, fblk, name):
    s = h.shape[0]

    def body(f_ref, b_ref, c_ref, ct_ref):
        c = _log_sigmoid(f_ref[...] + b_ref[...])
        rows = lax.broadcasted_iota(jnp.int32, c.shape, 0)
        k = 1
        while k < s:
            c = c + _shift_down(c, k, rows)
            k *= 2
        c_ref[...] = c
        ct_ref[...] = jnp.transpose(c)[0:8, :]

    return pl.pallas_call(
        body, name=name, grid=(1,),
        in_specs=[pl.BlockSpec((s, LANES), lambda i: (0, fblk)), pl.BlockSpec((1, LANES), lambda i: (0, 0))],
        out_specs=[pl.BlockSpec((s, LANES), lambda i: (0, 0)), pl.BlockSpec((8, s), lambda i: (0, 0))],
        out_shape=[jax.ShapeDtypeStruct((s, LANES), F32), jax.ShapeDtypeStruct((8, s), F32)],
        compiler_params=_params("arbitrary"),
    )(h, bf_pad)


def _forget_prep_bwd(h, bf_pad, dct, dcq, fblk, name):
    s = h.shape[0]
    pairs = N_HEADS // 2

    def body(f_ref, b_ref, dct_ref, dcq_ref, df_ref, db_ref):
        dct = dct_ref[0]
        dcq = dcq_ref[0]
        for p in range(1, pairs):
            dct = dct + pltpu.roll(dct_ref[p], 2 * p, 0)
            dcq = dcq + pltpu.roll(dcq_ref[p], 2 * p, 1)
        dc = dcq + jnp.transpose(jnp.concatenate([dct, jnp.zeros((LANES - 8, s), F32)], axis=0))
        rows = lax.broadcasted_iota(jnp.int32, dc.shape, 0)
        k = 1
        while k < s:
            dc = dc + _shift_up(dc, k, rows)
            k *= 2
        z = f_ref[...] + b_ref[...]
        lane = lax.broadcasted_iota(jnp.int32, dc.shape, 1)
        df = jnp.where(lane < N_HEADS, dc * jax.nn.sigmoid(-z), 0.0)
        df_ref[...] = df
        db_ref[...] = jnp.sum(df, axis=0, keepdims=True)

    return pl.pallas_call(
        body, name=name, grid=(1,),
        in_specs=[pl.BlockSpec((s, LANES), lambda i: (0, fblk)), pl.BlockSpec((1, LANES), lambda i: (0, 0)),
                  pl.BlockSpec((pairs, 8, s), lambda i: (0, 0, 0)), pl.BlockSpec((pairs, s, LANES), lambda i: (0, 0, 0))],
        out_specs=[pl.BlockSpec((s, LANES), lambda i: (0, 0)), pl.BlockSpec((1, LANES), lambda i: (0, 0))],
        out_shape=[jax.ShapeDtypeStruct((s, LANES), F32), jax.ShapeDtypeStruct((1, LANES), F32)],
        compiler_params=_params("arbitrary"),
    )(h, bf_pad, dct, dcq)


def _pick_lane(blk, idx):
    lane = lax.broadcasted_iota(jnp.int32, blk.shape, 1)
    return jnp.sum(jnp.where(lane == idx, blk, 0.0), axis=1, keepdims=True)


def _pick_row(blk, idx):
    sub = lax.broadcasted_iota(jnp.int32, blk.shape, 0)
    return jnp.sum(jnp.where(sub == idx, blk, 0.0), axis=0, keepdims=True)


def _attention_fwd(h, c, ct, qblk, name):
    s = h.shape[0]
    t = _tile(s, ATT_TILE)
    nq = s // t
    scale = HEAD_DIM ** -0.5
    nt_dims = (((1,), (1,)), ((), ()))

    def body(q_ref, k_ref, v_ref, c_ref, ct_ref, o_ref, lse_ref):
        p = pl.program_id(0)
        i = pl.program_id(1)
        lane = lax.broadcasted_iota(jnp.int32, (1, LANES), 1)
        first = lane < HEAD_DIM
        q = q_ref[...] * (scale * LOG2E)
        qa = jnp.where(first, q, 0.0).astype(BF16)
        qb = jnp.where(first, 0.0, q).astype(BF16)
        cblk = c_ref[...]
        cta = _pick_lane(cblk, 2 * p) * LOG2E
        ctb = _pick_lane(cblk, 2 * p + 1) * LOG2E

        def step(j, carry, diagonal):
            ma, la, mb, lb, acc = carry
            off = pl.multiple_of(j * t, t)
            k = k_ref[pl.ds(off, t), :].astype(BF16)
            v = v_ref[pl.ds(off, t), :].astype(BF16)
            crow = ct_ref[:, pl.ds(off, t)] * LOG2E

            def one(qh, cth, hd, m_old, l_old):
                sc = lax.dot_general(qh, k, nt_dims, preferred_element_type=F32) - _pick_row(crow, hd)
                if diagonal:
                    keep = lax.broadcasted_iota(jnp.int32, (t, t), 0) >= lax.broadcasted_iota(jnp.int32, (t, t), 1)
                    sc = jnp.where(keep, sc, NEG)
                m_new = jnp.maximum(m_old, jnp.max(sc, axis=1, keepdims=True) + cth)
                pr = jnp.exp2(sc - (m_new - cth))
                alpha = jnp.exp2(m_old - m_new)
                l_new = alpha * l_old + jnp.sum(pr, axis=1, keepdims=True)
                pv = jnp.dot(pr.astype(BF16), v, preferred_element_type=F32)
                return m_new, l_new, alpha, pv

            ma2, la2, aa, pva = one(qa, cta, 2 * p, ma, la)
            mb2, lb2, ab, pvb = one(qb, ctb, 2 * p + 1, mb, lb)
            acc = jnp.where(first, aa * acc + pva, ab * acc + pvb)
            return ma2, la2, mb2, lb2, acc

        init = (jnp.full((t, 1), NEG, F32), jnp.zeros((t, 1), F32), jnp.full((t, 1), NEG, F32),
                jnp.zeros((t, 1), F32), jnp.zeros((t, LANES), F32))
        carry = lax.fori_loop(0, i, lambda j, carry: step(j, carry, False), init)
        ma, la, mb, lb, acc = step(i, carry, True)
        o_ref[...] = (acc / jnp.where(first, la, lb)).astype(BF16)
        lse_ref[0] = jnp.broadcast_to(ma + jnp.log2(la), (t, LANES))
        lse_ref[1] = jnp.broadcast_to(mb + jnp.log2(lb), (t, LANES))

    return pl.pallas_call(
        body, name=name, grid=(N_HEADS // 2, nq),
        in_specs=[pl.BlockSpec((t, LANES), lambda p, i: (i, qblk + p)),
                  pl.BlockSpec((s, LANES), lambda p, i: (0, qblk + 4 + p)),
                  pl.BlockSpec((s, LANES), lambda p, i: (0, qblk + 8 + p)),
                  pl.BlockSpec((t, LANES), lambda p, i: (i, 0)),
                  pl.BlockSpec((8, s), lambda p, i: (0, 0))],
        out_specs=[pl.BlockSpec((t, LANES), lambda p, i: (i, p)),
                   pl.BlockSpec((2, t, LANES), lambda p, i: (p, i, 0))],
        out_shape=[jax.ShapeDtypeStruct((s, D_ATT), BF16), jax.ShapeDtypeStruct((N_HEADS, s, LANES), F32)],
        compiler_params=_params("parallel", "arbitrary"),
    )(h, h, h, c, ct)


def _attention_bwd(h, c, ct, lse, att, datt, qblk, name):
    s = h.shape[0]
    t = _tile(s, ATT_TILE)
    nq = s // t
    scale = HEAD_DIM ** -0.5
    nt_dims = (((1,), (1,)), ((), ()))
    tn_dims = (((0,), (0,)), ((), ()))

    def body(q_ref, k_ref, v_ref, c_ref, ct_ref, lse_ref, o_ref, do_ref, dq_ref, dk_ref, dv_ref, dct_ref, dcq_ref):
        p = pl.program_id(0)
        j = pl.program_id(1)
        lane = lax.broadcasted_iota(jnp.int32, (1, LANES), 1)
        first = lane < HEAD_DIM
        kf = k_ref[...]
        vf = v_ref[...]
        k = kf.astype(BF16)
        ka = jnp.where(first, kf, 0.0).astype(BF16)
        kb = jnp.where(first, 0.0, kf).astype(BF16)
        va = jnp.where(first, vf, 0.0).astype(BF16)
        vb = jnp.where(first, 0.0, vf).astype(BF16)
        crow = ct_ref[...] * LOG2E
        csa = _pick_row(crow, 2 * p)
        csb = _pick_row(crow, 2 * p + 1)

        @pl.when(j == 0)
        def _():
            dq_ref[...] = jnp.zeros_like(dq_ref)
            dcq_ref[...] = jnp.zeros_like(dcq_ref)

        def step(i, carry, diagonal):
            dka, dkb, dva, dvb, dca, dcb = carry
            off = pl.multiple_of(i * t, t)
            rows = pl.ds(off, t)
            q = (q_ref[rows, :] * (scale * LOG2E)).astype(BF16)
            dof = do_ref[rows, :]
            do = dof.astype(BF16)
            prod = dof * o_ref[rows, :].astype(F32)
            cblk = c_ref[rows, :] * LOG2E

            def one(kh, vh, hd, csh, lse_h):
                sc = lax.dot_general(q, kh, nt_dims, preferred_element_type=F32) - csh
                if diagonal:
                    keep = lax.broadcasted_iota(jnp.int32, (t, t), 0) >= lax.broadcasted_iota(jnp.int32, (t, t), 1)
                    sc = jnp.where(keep, sc, NEG)
                pr = jnp.exp2(sc - (jnp.max(lse_h, axis=1, keepdims=True) - _pick_lane(cblk, hd)))
                dp = lax.dot_general(do, vh, nt_dims, preferred_element_type=F32)
                return pr, dp

            pra, dpa = one(ka, va, 2 * p, csa, lse_ref[0, rows, :])
            prb, dpb = one(kb, vb, 2 * p + 1, csb, lse_ref[1, rows, :])
            dela = jnp.sum(jnp.where(first, prod, 0.0), axis=1, keepdims=True)
            delb = jnp.sum(jnp.where(first, 0.0, prod), axis=1, keepdims=True)
            dsa = pra * (dpa - dela)
            dsb = prb * (dpb - delb)
            dsa16 = dsa.astype(BF16)
            dsb16 = dsb.astype(BF16)
            dva = dva + lax.dot_general(pra.astype(BF16), do, tn_dims, preferred_element_type=F32)
            dvb = dvb + lax.dot_general(prb.astype(BF16), do, tn_dims, preferred_element_type=F32)
            dka = dka + lax.dot_general(dsa16, q, tn_dims, preferred_element_type=F32)
            dkb = dkb + lax.dot_general(dsb16, q, tn_dims, preferred_element_type=F32)
            dqa = jnp.dot(dsa16, k, preferred_element_type=F32)
            dqb = jnp.dot(dsb16, k, preferred_element_type=F32)
            dq_ref[rows, :] += scale * jnp.where(first, dqa, dqb)
            dca = dca - jnp.sum(dsa, axis=0, keepdims=True)
            dcb = dcb - jnp.sum(dsb, axis=0, keepdims=True)
            dcq_ref[rows, :] += jnp.where(lane == 0, jnp.sum(dsa, axis=1, keepdims=True),
                                          jnp.where(lane == 1, jnp.sum(dsb, axis=1, keepdims=True), 0.0))
            return dka, dkb, dva, dvb, dca, dcb

        z = jnp.zeros((t, LANES), F32)
        zr = jnp.zeros((1, t), F32)
        carry = step(j, (z, z, z, z, zr, zr), True)
        dka, dkb, dva, dvb, dca, dcb = lax.fori_loop(j + 1, nq, lambda i, carry: step(i, carry, False), carry)
        dk_ref[...] = jnp.where(first, dka, dkb) * (1.0 / LOG2E)
        dv_ref[...] = jnp.where(first, dva, dvb)
        sub = lax.broadcasted_iota(jnp.int32, (8, t), 0)
        dct_ref[...] = jnp.where(sub == 0, dca, jnp.where(sub == 1, dcb, 0.0))

    full = lambda blk: pl.BlockSpec((s, LANES), blk)
    return pl.pallas_call(
        body, name=name, grid=(N_HEADS // 2, nq),
        in_specs=[full(lambda p, j: (0, qblk + p)),
                  pl.BlockSpec((t, LANES), lambda p, j: (j, qblk + 4 + p)),
                  pl.BlockSpec((t, LANES), lambda p, j: (j, qblk + 8 + p)),
                  full(lambda p, j: (0, 0)),
                  pl.BlockSpec((8, t), lambda p, j: (0, j)),
                  pl.BlockSpec((2, s, LANES), lambda p, j: (p, 0, 0)),
                  full(lambda p, j: (0, p)),
                  full(lambda p, j: (0, p))],
        out_specs=[full(lambda p, j: (0, p)),
                   pl.BlockSpec((t, LANES), lambda p, j: (j, p)),
                   pl.BlockSpec((t, LANES), lambda p, j: (j, p)),
                   pl.BlockSpec((None, 8, t), lambda p, j: (p, 0, j)),
                   pl.BlockSpec((None, s, LANES), lambda p, j: (p, 0, 0))],
        out_shape=[jax.ShapeDtypeStruct((s, D_ATT), F32), jax.ShapeDtypeStruct((s, D_ATT), F32),
                   jax.ShapeDtypeStruct((s, D_ATT), F32), jax.ShapeDtypeStruct((N_HEADS // 2, 8, s), F32),
                   jax.ShapeDtypeStruct((N_HEADS // 2, s, LANES), F32)],
        compiler_params=_params("arbitrary", "arbitrary"),
    )(h, h, h, c, ct, lse, att, datt)


def _sconv_fwd(h, w, bgblk, name):
    s = h.shape[0]
    nblk = D_CONV // LANES

    def body(bg_ref, cg_ref, hc_ref, w_ref, y_ref):
        rows = lax.broadcasted_iota(jnp.int32, (s, LANES), 0)
        y_ref[...] = (bg_ref[...] * _conv3(cg_ref[...] * hc_ref[...], w_ref, rows)).astype(BF16)

    col = lambda base: pl.BlockSpec((s, LANES), lambda j: (0, base + j))
    return pl.pallas_call(
        body, name=name, grid=(nblk,),
        in_specs=[col(bgblk), col(bgblk + nblk), col(bgblk + 2 * nblk), pl.BlockSpec((3, LANES), lambda j: (0, j))],
        out_specs=pl.BlockSpec((s, LANES), lambda j: (0, j)),
        out_shape=jax.ShapeDtypeStruct((s, D_CONV), BF16), compiler_params=_params("parallel"),
    )(h, h, h, w)


def _sconv_bwd(h, w, dy, bgblk, name):
    s = h.shape[0]
    nblk = D_CONV // LANES

    def body(bg_ref, cg_ref, hc_ref, w_ref, dy_ref, dbg_ref, dcg_ref, dhc_ref, dw_ref):
        rows = lax.broadcasted_iota(jnp.int32, (s, LANES), 0)
        cg, hc, dy, w = cg_ref[...], hc_ref[...], dy_ref[...], w_ref
        xin = cg * hc
        dbg_ref[...] = (dy * _conv3(xin, w, rows)).astype(BF16)
        dxin, dw_ref[...] = _conv3_bwd(dy * bg_ref[...], xin, w, rows)
        dcg_ref[...] = (dxin * hc).astype(BF16)
        dhc_ref[...] = (dxin * cg).astype(BF16)

    col = lambda base: pl.BlockSpec((s, LANES), lambda j: (0, base + j))
    return pl.pallas_call(
        body, name=name, grid=(nblk,),
        in_specs=[col(bgblk), col(bgblk + nblk), col(bgblk + 2 * nblk), pl.BlockSpec((3, LANES), lambda j: (0, j)), col(0)],
        out_specs=[col(0), col(0), col(0), pl.BlockSpec((8, LANES), lambda j: (0, j))],
        out_shape=[jax.ShapeDtypeStruct((s, D_CONV), BF16)] * 3 + [jax.ShapeDtypeStruct((8, D_CONV), F32)],
        compiler_params=_params("parallel"),
    )(h, h, h, w, dy)


def _sgu_group_masks():
    lane = lax.broadcasted_iota(jnp.int32, (1, D_SGU), 1)
    return [(lane // HEAD_DIM) == g for g in range(N_SGU_GROUPS)]


def _sgu_tril():
    r = lax.broadcasted_iota(jnp.int32, (SGU_CHUNK, SGU_CHUNK), 0)
    c = lax.broadcasted_iota(jnp.int32, (SGU_CHUNK, SGU_CHUNK), 1)
    return r >= c


def _sgu_fwd(h, ln_g, ln_b, w_s, b_full, ublk, name):
    s = h.shape[0]
    tr = _tile(s, 512)
    nch = tr // SGU_CHUNK

    def body(u_ref, v_ref, g_ref, b_ref, w_ref, bf_ref, y_ref):
        masks = _sgu_group_masks()
        tril = _sgu_tril()
        wm = [jnp.where(tril, w_ref[g], 0.0).astype(BF16) for g in range(N_SGU_GROUPS)]
        vn = _layer_norm(_gelu(v_ref[...]), g_ref[...], b_ref[...])
        for ch in range(nch):
            rows = pl.ds(ch * SGU_CHUNK, SGU_CHUNK)
            vc = vn[ch * SGU_CHUNK:(ch + 1) * SGU_CHUNK, :]
            mixed = bf_ref[...]
            for g in range(N_SGU_GROUPS):
                mixed = mixed + jnp.dot(wm[g], jnp.where(masks[g], vc, 0.0).astype(BF16), preferred_element_type=F32)
            y_ref[rows, :] = (_gelu(u_ref[rows, :]) * mixed).astype(BF16)

    row = lambda blk: pl.BlockSpec((tr, D_SGU), lambda i: (i, blk))
    vec = pl.BlockSpec((1, D_SGU), lambda i: (0, 0))
    return pl.pallas_call(
        body, name=name, grid=(s // tr,),
        in_specs=[row(ublk), row(ublk + 1), vec, vec,
                  pl.BlockSpec((N_SGU_GROUPS, SGU_CHUNK, SGU_CHUNK), lambda i: (0, 0, 0)),
                  pl.BlockSpec((SGU_CHUNK, D_SGU), lambda i: (0, 0))],
        out_specs=row(0), out_shape=jax.ShapeDtypeStruct((s, D_SGU), BF16), compiler_params=_params("parallel"),
    )(h, h, ln_g, ln_b, w_s, b_full)


def _sgu_bwd(h, ln_g, ln_b, w_s, b_full, dy, ublk, name):
    s = h.shape[0]
    tr = _tile(s, 512)
    nch = tr // SGU_CHUNK
    nt_dims = (((1,), (1,)), ((), ()))

    def norm(v, g, b):
        return _layer_norm(_gelu(v), g, b)

    def body(u_ref, v_ref, g_ref, b_ref, w_ref, bf_ref, dy_ref, du_ref, dv_ref, dg_ref, db_ref, dw_ref, dbf_ref):
        masks = _sgu_group_masks()
        tril = _sgu_tril()
        wf = [jnp.where(tril, w_ref[g], 0.0) for g in range(N_SGU_GROUPS)]
        wm = [w.astype(BF16) for w in wf]
        wmt = [jnp.transpose(w).astype(BF16) for w in wf]
        vn, vjp = jax.vjp(norm, v_ref[...], g_ref[...], b_ref[...])

        @pl.when(pl.program_id(0) == 0)
        def _():
            dg_ref[...] = jnp.zeros_like(dg_ref)
            db_ref[...] = jnp.zeros_like(db_ref)
            dw_ref[...] = jnp.zeros_like(dw_ref)
            dbf_ref[...] = jnp.zeros_like(dbf_ref)

        dvn_parts = []
        for ch in range(nch):
            rows = pl.ds(ch * SGU_CHUNK, SGU_CHUNK)
            vc = vn[ch * SGU_CHUNK:(ch + 1) * SGU_CHUNK, :]
            vc16 = vc.astype(BF16)
            mixed = bf_ref[...]
            for g in range(N_SGU_GROUPS):
                mixed = mixed + jnp.dot(wm[g], jnp.where(masks[g], vc, 0.0).astype(BF16), preferred_element_type=F32)
            dy = dy_ref[rows, :]
            ug, slope = _gelu_and_slope(u_ref[rows, :])
            du_ref[rows, :] = (dy * mixed * slope).astype(BF16)
            dmixed = dy * ug
            dbf_ref[...] += dmixed
            dvc = jnp.zeros((SGU_CHUNK, D_SGU), F32)
            for g in range(N_SGU_GROUPS):
                dm16 = jnp.where(masks[g], dmixed, 0.0).astype(BF16)
                dw_ref[g] += jnp.where(tril, lax.dot_general(dm16, vc16, nt_dims, preferred_element_type=F32), 0.0)
                dvc = dvc + jnp.dot(wmt[g], dm16, preferred_element_type=F32)
            dvn_parts.append(dvc)
        dv, dg, db = vjp(jnp.concatenate(dvn_parts, axis=0))
        dv_ref[...] = dv.astype(BF16)
        dg_ref[...] += dg
        db_ref[...] += db

    row = lambda blk: pl.BlockSpec((tr, D_SGU), lambda i: (i, blk))
    vec = pl.BlockSpec((1, D_SGU), lambda i: (0, 0))
    wsp = pl.BlockSpec((N_SGU_GROUPS, SGU_CHUNK, SGU_CHUNK), lambda i: (0, 0, 0))
    bsp = pl.BlockSpec((SGU_CHUNK, D_SGU), lambda i: (0, 0))
    return pl.pallas_call(
        body, name=name, grid=(s // tr,),
        in_specs=[row(ublk), row(ublk + 1), vec, vec, wsp, bsp, row(0)],
        out_specs=[row(0), row(0), vec, vec, wsp, bsp],
        out_shape=[jax.ShapeDtypeStruct((s, D_SGU), BF16), jax.ShapeDtypeStruct((s, D_SGU), BF16),
                   jax.ShapeDtypeStruct((1, D_SGU), F32), jax.ShapeDtypeStruct((1, D_SGU), F32),
                   jax.ShapeDtypeStruct((N_SGU_GROUPS, SGU_CHUNK, SGU_CHUNK), F32),
                   jax.ShapeDtypeStruct((SGU_CHUNK, D_SGU), F32)],
        compiler_params=_params("arbitrary"),
    )(h, h, ln_g, ln_b, w_s, b_full, dy)


def _merge_fwd(h, b_gate, att, yc, ys, wa, wc, ws, name):
    s, d = att.shape[0], wa.shape[1]
    tm, tn = _tile(s, 512), _tile(d, 512)
    nj = d // tn

    def body(g0_ref, g1_ref, g2_ref, bg_ref, a_ref, c_ref, s_ref, wa_ref, wc_ref, ws_ref, o_ref):
        acc = jax.nn.sigmoid(g0_ref[...] + bg_ref[0:1, :]) * jnp.dot(a_ref[...], wa_ref[...], preferred_element_type=F32)
        acc += jax.nn.sigmoid(g1_ref[...] + bg_ref[1:2, :]) * jnp.dot(c_ref[...], wc_ref[...], preferred_element_type=F32)
        acc += jax.nn.sigmoid(g2_ref[...] + bg_ref[2:3, :]) * jnp.dot(s_ref[...], ws_ref[...], preferred_element_type=F32)
        o_ref[...] = acc.astype(BF16)

    gate = lambda b: pl.BlockSpec((tm, tn), lambda j, i: (i, b * nj + j))
    act = lambda k: pl.BlockSpec((tm, k), lambda j, i: (i, 0))
    wgt = lambda k: pl.BlockSpec((k, tn), lambda j, i: (0, j))
    return pl.pallas_call(
        body, name=name, grid=(nj, s // tm),
        in_specs=[gate(0), gate(1), gate(2), pl.BlockSpec((3, tn), lambda j, i: (0, j)),
                  act(D_ATT), act(D_CONV), act(D_SGU), wgt(D_ATT), wgt(D_CONV), wgt(D_SGU)],
        out_specs=pl.BlockSpec((tm, tn), lambda j, i: (i, j)),
        out_shape=jax.ShapeDtypeStruct((s, d), BF16), compiler_params=_params("parallel", "arbitrary"),
    )(h, h, h, b_gate, att, yc, ys, wa, wc, ws)


def _merge_bwd(h, b_gate, att, yc, ys, wa, wc, ws, dm, name):
    s, d = att.shape[0], wa.shape[1]
    tm, tn = _tile(s, 512), _tile(d, 512)
    nj = d // tn

    def body(g0_ref, g1_ref, g2_ref, bg_ref, a_ref, c_ref, s_ref, wa_ref, wc_ref, ws_ref, dm_ref,
             dya_ref, dyc_ref, dys_ref, dgl_ref, dbg_ref):
        dm = dm_ref[...]
        sums = []
        for b, (g_ref, x_ref, w_ref, dy_ref) in enumerate(((g0_ref, a_ref, wa_ref, dya_ref), (g1_ref, c_ref, wc_ref, dyc_ref),
                                                          (g2_ref, s_ref, ws_ref, dys_ref))):
            gate = jax.nn.sigmoid(g_ref[...] + bg_ref[b:b + 1, :])
            y = jnp.dot(x_ref[...], w_ref[...], preferred_element_type=F32)
            dy_ref[...] = (dm * gate).astype(BF16)
            dgl = dm * y * gate * (1.0 - gate)
            dgl_ref[b] = dgl.astype(BF16)
            sums.append(jnp.sum(dgl, axis=0, keepdims=True))
        sub = lax.broadcasted_iota(jnp.int32, (3, tn), 0)
        part = jnp.where(sub == 0, sums[0], jnp.where(sub == 1, sums[1], sums[2]))

        @pl.when(pl.program_id(1) == 0)
        def _():
            dbg_ref[...] = jnp.zeros_like(dbg_ref)

        dbg_ref[...] += part

    gate = lambda b: pl.BlockSpec((tm, tn), lambda j, i: (i, b * nj + j))
    act = lambda k: pl.BlockSpec((tm, k), lambda j, i: (i, 0))
    wgt = lambda k: pl.BlockSpec((k, tn), lambda j, i: (0, j))
    tile = pl.BlockSpec((tm, tn), lambda j, i: (i, j))
    return pl.pallas_call(
        body, name=name, grid=(nj, s // tm),
        in_specs=[gate(0), gate(1), gate(2), pl.BlockSpec((3, tn), lambda j, i: (0, j)),
                  act(D_ATT), act(D_CONV), act(D_SGU), wgt(D_ATT), wgt(D_CONV), wgt(D_SGU), tile],
        out_specs=[tile, tile, tile, pl.BlockSpec((3, tm, tn), lambda j, i: (0, i, j)),
                   pl.BlockSpec((3, tn), lambda j, i: (0, j))],
        out_shape=[jax.ShapeDtypeStruct((s, d), BF16)] * 3 + [jax.ShapeDtypeStruct((3, s, d), BF16),
                                                              jax.ShapeDtypeStruct((3, d), F32)],
        compiler_params=_params("parallel", "arbitrary"),
    )(h, h, h, b_gate, att, yc, ys, wa, wc, ws, dm)


def _ffn_act_fwd(hh, cw, name):
    s, dff = hh.shape[0], hh.shape[1] // 2
    nblk = dff // LANES

    def body(a_ref, b_ref, wa_ref, wb_ref, z_ref):
        rows = lax.broadcasted_iota(jnp.int32, (s, LANES), 0)
        z_ref[...] = (_gelu(_conv3(a_ref[...], wa_ref, rows)) * _conv3(b_ref[...], wb_ref, rows)).astype(BF16)

    col = lambda base: pl.BlockSpec((s, LANES), lambda j: (0, base + j))
    wsp = lambda base: pl.BlockSpec((3, LANES), lambda j: (0, base + j))
    return pl.pallas_call(
        body, name=name, grid=(nblk,), in_specs=[col(0), col(nblk), wsp(0), wsp(nblk)], out_specs=col(0),
        out_shape=jax.ShapeDtypeStruct((s, dff), BF16), compiler_params=_params("parallel"),
    )(hh, hh, cw, cw)


def _ffn_act_bwd(hh, cw, dz, name):
    s, dff = hh.shape[0], hh.shape[1] // 2
    nblk = dff // LANES

    def body(a_ref, b_ref, wa_ref, wb_ref, dz_ref, da_ref, db_ref, dwa_ref, dwb_ref):
        rows = lax.broadcasted_iota(jnp.int32, (s, LANES), 0)
        a, b, dz = a_ref[...], b_ref[...], dz_ref[...]
        ga, slope = _gelu_and_slope(_conv3(a, wa_ref, rows))
        da, dwa_ref[...] = _conv3_bwd(dz * _conv3(b, wb_ref, rows) * slope, a, wa_ref, rows)
        db, dwb_ref[...] = _conv3_bwd(dz * ga, b, wb_ref, rows)
        da_ref[...] = da.astype(BF16)
        db_ref[...] = db.astype(BF16)

    col = lambda base: pl.BlockSpec((s, LANES), lambda j: (0, base + j))
    wsp = lambda base: pl.BlockSpec((3, LANES), lambda j: (0, base + j))
    w8 = lambda base: pl.BlockSpec((8, LANES), lambda j: (0, base + j))
    return pl.pallas_call(
        body, name=name, grid=(nblk,), in_specs=[col(0), col(nblk), wsp(0), wsp(nblk), col(0)],
        out_specs=[col(0), col(0), w8(0), w8(0)],
        out_shape=[jax.ShapeDtypeStruct((s, dff), BF16)] * 2 + [jax.ShapeDtypeStruct((8, dff), F32)] * 2,
        compiler_params=_params("parallel"),
    )(hh, hh, cw, cw, dz)


ANY = pl.BlockSpec(memory_space=pl.ANY)


def _place():
    return lax.axis_index("x"), lax.axis_index("y"), lax.axis_index("c")


def _all_gather(arrs, name):
    n = len(arrs)

    def body(*refs):
        ins, outs = refs[:n], refs[n:2 * n]
        send_sems, recv_sems, local_sems = refs[2 * n:]
        x, y, c = _place()
        me, sibling = (x, y, c), (x, y, 1 - c)
        chips = [(1 - x, y), (x, 1 - y), (1 - x, 1 - y)]

        def slab(a, dev):
            return outs[a].at[4 * dev[0] + 2 * dev[1] + dev[2]]

        def copy(a, k, block, to, src=None):
            return pltpu.make_async_remote_copy(
                src_ref=slab(a, block) if src is None else src, dst_ref=slab(a, block),
                send_sem=send_sems.at[7 * a + k], recv_sem=recv_sems.at[7 * a + k], device_id=to, device_id_type=MESH)

        mine = [pltpu.make_async_copy(ins[a], slab(a, me), local_sems.at[a]) for a in range(n)]
        for cp in mine:
            cp.start()
        first = []
        for a in range(n):
            first.append(copy(a, 0, me, sibling, src=ins[a]))
            first += [copy(a, 1 + j, me, (*chip, c), src=ins[a]) for j, chip in enumerate(chips)]
        for cp in first:
            cp.start()
        passed = []
        for a in range(n):
            for j, chip in enumerate(chips):
                copy(a, 1 + j, (*chip, c), me).wait_recv()
                fwd = copy(a, 4 + j, (*chip, c), sibling)
                fwd.start()
                passed.append(fwd)
        for a in range(n):
            copy(a, 0, sibling, me).wait_recv()
            for j, chip in enumerate(chips):
                copy(a, 4 + j, (*chip, 1 - c), me).wait_recv()
        for cp in first + passed:
            cp.wait_send()
        for cp in mine:
            cp.wait()

    return pl.pallas_call(
        body, name=name, in_specs=[ANY] * n, out_specs=[ANY] * n,
        out_shape=[jax.ShapeDtypeStruct((N_DEV,) + a.shape, a.dtype) for a in arrs],
        scratch_shapes=[pltpu.SemaphoreType.DMA((7 * n,)), pltpu.SemaphoreType.DMA((7 * n,)), pltpu.SemaphoreType.DMA((n,))],
    )(*arrs)


HBM = pl.BlockSpec(memory_space=pltpu.HBM)
SEM = pl.BlockSpec(memory_space=pltpu.SEMAPHORE)
EFFECT = pltpu.SideEffectType.DATAFLOW_SIDE_EFFECTING


def _slot(dev):
    return 4 * dev[0] + 2 * dev[1] + dev[2]


def _exchange_copies(src_refs, land_refs, send_sems, recv_sems, src_view, land_view):
    x, y, c = _place()
    me = (x, y, c)
    peers = [(1 - x if r & 4 else x, 1 - y if r & 2 else y, 1 - c if r & 1 else c) for r in range(1, N_DEV)]
    sends, lands = [], []
    for a, (src, land) in enumerate(zip(src_refs, land_refs)):
        for k, peer in enumerate(peers):
            sems = dict(send_sem=send_sems.at[7 * a + k], recv_sem=recv_sems.at[7 * a + k], device_id=peer,
                        device_id_type=MESH)
            sends.append(pltpu.make_async_remote_copy(src_ref=src_view(src, _slot(peer)),
                                                      dst_ref=land_view(land, _slot(me)), **sems))
            lands.append(pltpu.make_async_remote_copy(src_ref=src_view(src, _slot(me)),
                                                      dst_ref=land_view(land, _slot(peer)), **sems))
    return sends, lands


def _own_copies(src_refs, land_refs, local_sems, src_view, land_view):
    me = _slot(_place())
    return [pltpu.make_async_copy(src_view(src, me), land_view(land, me), local_sems.at[a])
            for a, (src, land) in enumerate(zip(src_refs, land_refs))]


def _exchange_start(srcs, lands, after, src_view, land_view, name):
    n = len(srcs)

    def body(*refs):
        src_refs, land_refs = refs[:n], refs[n:2 * n]
        send_sems, recv_sems, local_sems = refs[2 * n + 1:2 * n + 4]
        token = refs[-1]
        sends, _ = _exchange_copies(src_refs, land_refs, send_sems, recv_sems, src_view, land_view)
        for cp in sends + _own_copies(src_refs, land_refs, local_sems, src_view, land_view):
            cp.start()
        token[...] = jnp.zeros_like(token)

    thru = [pltpu.HBM(a.shape, a.dtype) for a in list(srcs) + list(lands)]
    outs = pl.pallas_call(
        body, name=name,
        out_shape=(pltpu.SemaphoreType.DMA((7 * n,)), pltpu.SemaphoreType.DMA((7 * n,)), pltpu.SemaphoreType.DMA((n,)),
                   *thru, jax.ShapeDtypeStruct((8, LANES), F32)),
        in_specs=[HBM] * (2 * n) + [ANY],
        out_specs=(SEM, SEM, SEM, *([HBM] * (2 * n)), pl.BlockSpec(memory_space=pltpu.VMEM)),
        input_output_aliases={i: 3 + i for i in range(2 * n)},
        compiler_params=pltpu.CompilerParams(has_side_effects=EFFECT),
    )(*[pltpu.with_memory_space_constraint(a, pltpu.HBM) for a in list(srcs) + list(lands)], after)
    return outs[:3], list(outs[3:3 + n]), list(outs[3 + n:3 + 2 * n]), outs[-1]


def _exchange_wait(sems, srcs, lands, after, src_view, land_view, name):
    n = len(srcs)

    def body(*refs):
        src_refs, land_refs = refs[:n], refs[n:2 * n]
        send_sems, recv_sems, local_sems = refs[2 * n:2 * n + 3]
        sends, landed = _exchange_copies(src_refs, land_refs, send_sems, recv_sems, src_view, land_view)
        for cp in sends:
            cp.wait_send()
        for cp in landed:
            cp.wait_recv()
        for cp in _own_copies(src_refs, land_refs, local_sems, src_view, land_view):
            cp.wait()

    after = after if isinstance(after, (list, tuple)) else [after]
    outs = pl.pallas_call(
        body, name=name, out_shape=[pltpu.HBM(a.shape, a.dtype) for a in list(srcs) + list(lands)],
        in_specs=[HBM] * (2 * n) + [SEM, SEM, SEM] + [ANY] * len(after), out_specs=[HBM] * (2 * n),
        input_output_aliases={i: i for i in range(2 * n)},
        compiler_params=pltpu.CompilerParams(has_side_effects=EFFECT),
    )(*srcs, *lands, *sems, *after)
    return list(outs[:n]), list(outs[n:])


ADAMW_BLOCK_BYTES = 1 << 19
PACK_ROWS = 256


def _adamw(slabs, w, m, v, name):
    nl, r, c = w.shape
    row_edges = [r] + [t for t in range(8, r, 8) if r % t == 0]
    col_edges = [c] + [t for t in range(LANES, c, LANES) if c % t == 0]
    fits = [(tr * tc, tc, tr) for tr in row_edges for tc in col_edges if tr * tc * 4 <= ADAMW_BLOCK_BYTES]
    _, tc, tr = max(fits) if fits else (0, min(col_edges), min(row_edges))

    def body(s_ref, w_ref, m_ref, v_ref, g_ref, d_ref, nm_ref, nv_ref):
        g = s_ref[0].astype(F32)
        for q in range(1, N_DEV):
            g = g + s_ref[q].astype(F32)
        m_new = ADAM_B1 * m_ref[...] + (1.0 - ADAM_B1) * g
        v_new = ADAM_B2 * v_ref[...] + (1.0 - ADAM_B2) * (g * g)
        m_hat = m_new / (1.0 - ADAM_B1 ** ADAM_STEP)
        v_hat = v_new / (1.0 - ADAM_B2 ** ADAM_STEP)
        g_ref[...] = g
        d_ref[...] = -ADAM_LR * (m_hat / (jnp.sqrt(v_hat) + ADAM_EPS) + ADAM_WD * w_ref[...])
        nm_ref[...] = m_new
        nv_ref[...] = v_new

    blk = pl.BlockSpec((None, tr, tc), lambda l, i, j: (l, i, j))
    return pl.pallas_call(
        body, name=name, grid=(nl, r // tr, c // tc),
        in_specs=[pl.BlockSpec((N_DEV, None, tr, tc), lambda l, i, j: (0, l, i, j)), blk, blk, blk],
        out_specs=[blk] * 4, out_shape=[jax.ShapeDtypeStruct(w.shape, F32)] * 4,
        compiler_params=_params("parallel", "parallel", "parallel"),
    )(slabs, w, m, v)


def _layout(d):
    off = {"gate": 0, "q": 3 * d}
    off["bg"] = off["q"] + 3 * D_ATT
    off["u"] = off["bg"] + 3 * D_CONV
    off["f"] = off["u"] + 2 * D_SGU
    width = -(-(off["f"] + LANES) // 512) * 512
    return off, width


def _pad_w_in(wt, d, token):
    off, width = _layout(d)
    nqkv, nrest = 3 * D_ATT, 3 * D_CONV + 2 * D_SGU
    pad = jnp.zeros((width - off["f"] - N_HEADS, wt.shape[1]), wt.dtype) + token[0, 0].astype(wt.dtype)
    return jnp.concatenate([wt[nqkv + N_HEADS + nrest:], wt[:nqkv], wt[nqkv + N_HEADS:nqkv + N_HEADS + nrest],
                            wt[nqkv:nqkv + N_HEADS], pad], axis=0)


def _unpad_w_in(wtp, d):
    off, _ = _layout(d)
    return jnp.concatenate([wtp[off["q"]:off["bg"]], wtp[off["f"]:off["f"] + N_HEADS], wtp[off["bg"]:off["f"]],
                            wtp[:off["q"]]], axis=0)


def _cols_from_slabs(g):
    return jnp.transpose(g, (1, 0, 2)).reshape(g.shape[1], N_DEV * g.shape[2])


def _cols_to_slabs(w):
    r, c = w.shape[0], w.shape[1] // N_DEV
    return jnp.transpose(w.reshape(r, N_DEV, c), (1, 0, 2))


def kernel(x, pre_mix_g, post_mix_g, pre_ffn_g, post_ffn_g, w_in, b_forget, b_gate, conv_mix_w, sgu_ln_g, sgu_ln_b, sgu_w, sgu_b, w_branch_att, w_branch_conv, w_branch_sgu, w_out, w_ffn_up, conv_ffn_w, w_ffn_down, loss_target, m_pre_mix_g, m_post_mix_g, m_pre_ffn_g, m_post_ffn_g, m_w_in, m_b_forget, m_b_gate, m_conv_mix_w, m_sgu_ln_g, m_sgu_ln_b, m_sgu_w, m_sgu_b, m_w_branch_att, m_w_branch_conv, m_w_branch_sgu, m_w_out, m_w_ffn_up, m_conv_ffn_w, m_w_ffn_down, v_pre_mix_g, v_post_mix_g, v_pre_ffn_g, v_post_ffn_g, v_w_in, v_b_forget, v_b_gate, v_conv_mix_w, v_sgu_ln_g, v_sgu_ln_b, v_sgu_w, v_sgu_b, v_w_branch_att, v_w_branch_conv, v_w_branch_sgu, v_w_out, v_w_ffn_up, v_conv_ffn_w, v_w_ffn_down):
    depth = w_in.shape[0]
    s, d = x.shape[1], x.shape[2]
    dff = w_ffn_down.shape[1] * N_DEV
    off, _ = _layout(d)
    qblk, bgblk, ublk, fblk = off["q"] // LANES, off["bg"] // LANES, off["u"] // D_SGU, off["f"] // LANES
    x0 = x.reshape(s, d)
    target = loss_target.reshape(s, d)
    ncm, ncf = conv_mix_w.shape[2], conv_ffn_w.shape[2]

    lo = d // N_DEV

    whole = lambda ref, slot: ref
    slab = lambda ref, slot: ref.at[slot]

    w_in_t, w_up_t = jnp.transpose(w_in, (0, 2, 1)), jnp.transpose(w_ffn_up, (0, 2, 1))

    def shards_of(l, part):
        if part == "mix":
            small = jnp.concatenate([b_gate[l], conv_mix_w[l], conv_ffn_w[l]], axis=1)
            return [w_in_t[l].astype(BF16), w_branch_att[l].astype(BF16), w_branch_conv[l].astype(BF16),
                    w_branch_sgu[l].astype(BF16), w_out[l].astype(BF16), small]
        return [w_up_t[l].astype(BF16), w_ffn_down[l].astype(BF16)]

    def gather_start(l, part, after):
        shards = shards_of(l, part)
        lands = [lax.empty((N_DEV,) + a.shape, a.dtype) for a in shards]
        return _exchange_start(shards, lands, after, whole, slab, name=f"gather_start_{part}_{l}")

    def gather_finish(l, part, started, after):
        sems, shards, lands, _ = started[part]
        shards, lands = _exchange_wait(sems, shards, lands, after, whole, slab, name=f"gather_wait_{part}_{l}")
        token = jnp.zeros((8, LANES), F32)
        if l + 1 < depth:
            started[part] = gather_start(l + 1, part, lands[0])
            token = started[part][3]
        return lands, token

    def bfull(l):
        return jnp.repeat(jnp.transpose(sgu_b[l]), HEAD_DIM, axis=1)

    def bf_pad(l):
        return jnp.pad(b_forget[l], (0, LANES - N_HEADS)).reshape(1, LANES)

    saved = []
    weights = []
    xin = x0
    xn = _prenorm(x0, pre_mix_g[0:1], name="prenorm_first")
    loss_acc = dy = None
    first = _all_gather(shards_of(0, "mix"), name="gather_first")
    started = {"ffn": gather_start(0, "ffn", first[0])}
    if depth > 1:
        started["mix"] = gather_start(1, "mix", started["ffn"][3])
    for l in range(depth):
        if l == 0:
            (g_in, g_a, g_c, g_s, g_o, g_small), token = first, started["mix" if depth > 1 else "ffn"][3]
        else:
            (g_in, g_a, g_c, g_s, g_o, g_small), token = gather_finish(l, "mix", started, xin)
        g_small = _cols_from_slabs(g_small).reshape(3, N_DEV, -1)
        w = dict(w_in=_pad_w_in(g_in.reshape(N_DEV * g_in.shape[1], d), d, token), wa=_cols_from_slabs(g_a),
                 wc=_cols_from_slabs(g_c), ws=_cols_from_slabs(g_s), w_out=g_o.reshape(d, d),
                 b_gate=g_small[:, :, :lo].reshape(3, d), cmw=g_small[:, :, lo:lo + ncm].reshape(3, D_CONV),
                 cfw=g_small[:, :, lo + ncm:].reshape(3, 2 * dff))
        weights.append(w)
        h = _mm(xn, w["w_in"], "nt", F32, name="proj_in")
        c, ct = _forget_prep(h, bf_pad(l), fblk, name="forget_prep")
        att, lse = _attention_fwd(h, c, ct, qblk, name="attention_fwd")
        yc = _sconv_fwd(h, w["cmw"], bgblk, name="sconv_fwd")
        ys = _sgu_fwd(h, sgu_ln_g[l:l + 1], sgu_ln_b[l:l + 1], sgu_w[l], bfull(l), ublk, name="sgu_fwd")
        merged = _merge_fwd(h, w["b_gate"], att, yc, ys, w["wa"], w["wc"], w["ws"], name="merge_fwd")
        (g_up, g_dn), token = gather_finish(l, "ffn", started, merged)
        w["w_up"], w["w_dn"] = g_up.reshape(2 * dff, d), g_dn.reshape(dff, d)
        o, x1, xn2 = _mm_postnorm(merged, w["w_out"], xin, post_mix_g[l:l + 1], pre_ffn_g[l:l + 1] + token[0, 0],
                                  name="proj_out")
        hh = _mm(xn2, w["w_up"], "nt", F32, name="ffn_up")
        z = _ffn_act_fwd(hh, w["cfw"], name="ffn_act_fwd")
        layer = dict(xin=xin, xn=xn, h=h, c=c, ct=ct, lse=lse, att=att, yc=yc, ys=ys, merged=merged, o=o, x1=x1,
                     xn2=xn2, hh=hh, z=z)
        if l + 1 < depth:
            layer["f"], xin, xn = _mm_postnorm(z, w["w_dn"], x1, post_ffn_g[l:l + 1], pre_mix_g[l + 1:l + 2],
                                               name="ffn_down")
        else:
            layer["f"], dy, loss_acc = _mm_postnorm_loss(z, w["w_dn"], x1, post_ffn_g[l:l + 1], target,
                                                         name="ffn_down_loss")
        saved.append(layer)
    loss = lax.psum(loss_acc[0, 0] * (0.5 / d), ("x", "y", "c"))

    rep = {k: [None] * depth for k in ("pre_mix_g", "post_mix_g", "pre_ffn_g", "post_ffn_g", "b_forget", "sgu_ln_g",
                                       "sgu_ln_b", "sgu_w", "sgu_b")}
    nsmall = lo + ncm + ncf
    lands = {"win": [lax.empty((N_DEV, depth) + w_in_t.shape[1:], BF16)],
             "mid": [lax.empty((N_DEV, depth) + shp, dt) for shp, dt in (
                 (w_branch_att.shape[1:], BF16), (w_branch_conv.shape[1:], BF16), (w_branch_sgu.shape[1:], BF16),
                 (w_out.shape[1:], BF16), ((3, nsmall), F32))],
             "ffn": [lax.empty((N_DEV, depth) + shp, BF16) for shp in (w_up_t.shape[1:], w_ffn_down.shape[1:])]}
    scatters = {part: [None] * depth for part in lands}

    def scatter_start(l, part, sends, after):
        layer_slab = lambda ref, slot: ref.at[slot, l]
        sems, sends, lands[part], token = _exchange_start(sends, lands[part], after, slab, layer_slab,
                                                          name=f"scatter_start_{part}_{l}")
        scatters[part][l] = (sems, sends, layer_slab)
        return token

    def scatter_finish(part, after):
        for l in range(depth):
            sems, sends, layer_slab = scatters[part][l]
            _, lands[part] = _exchange_wait(sems, sends, lands[part], after, slab, layer_slab,
                                            name=f"scatter_wait_{part}_{l}")
        return lands[part]

    token = jnp.zeros((8, LANES), F32)
    dx = dy
    for l in reversed(range(depth)):
        w, a = weights[l], saved[l]
        df, rep["post_ffn_g"][l] = _postnorm_bwd(a["f"], post_ffn_g[l:l + 1] + token[0, 0], dx, name="postnorm_bwd")
        dz = _mm(df, w["w_dn"], "nt", F32, name="ffn_down_dx")
        g_dn = _mm(a["z"], df, "tn", BF16, name="ffn_down_dw")
        dha, dhb, dcwa, dcwb = _ffn_act_bwd(a["hh"], w["cfw"], dz, name="ffn_act_bwd")
        dhh = jnp.concatenate([dha, dhb], axis=1)
        dcfw = jnp.concatenate([dcwa[0:3], dcwb[0:3]], axis=1)
        g_up = _mm(dhh, a["xn2"], "tn", BF16, name="ffn_up_dw")
        token = scatter_start(l, "ffn", [g_up.reshape(N_DEV, 2 * dff // N_DEV, d), g_dn.reshape(N_DEV, dff // N_DEV, d)],
                              dz)
        dx1, rep["pre_ffn_g"][l] = _mm_prenorm_bwd(dhh, w["w_up"], a["x1"], pre_ffn_g[l:l + 1], dx, token, name="ffn_up_dx")
        do, rep["post_mix_g"][l], dmerged = _postnorm_bwd_mm(a["o"], post_mix_g[l:l + 1], dx1, w["w_out"], name="proj_out_dx")
        g_o = _mm(a["merged"], do, "tn", BF16, name="proj_out_dw")
        dya, dyc, dys, dgl, dbg = _merge_bwd(a["h"], w["b_gate"], a["att"], a["yc"], a["ys"], w["wa"], w["wc"], w["ws"],
                                             dmerged, name="merge_bwd")
        dconv = _mm(dyc, w["wc"], "nt", F32, name="branch_conv_dx")
        dsgu = _mm(dys, w["ws"], "nt", F32, name="branch_sgu_dx")
        g_a = _mm(a["att"], dya, "tn", BF16, name="branch_att_dw")
        g_c = _mm(a["yc"], dyc, "tn", BF16, name="branch_conv_dw")
        g_s = _mm(a["ys"], dys, "tn", BF16, name="branch_sgu_dw")
        dbgate, dcg, dhc, dcmw = _sconv_bwd(a["h"], w["cmw"], dconv, bgblk, name="sconv_bwd")
        sends = [_cols_to_slabs(g_a), _cols_to_slabs(g_c), _cols_to_slabs(g_s), g_o.reshape(N_DEV, d // N_DEV, d),
                 jnp.concatenate([_cols_to_slabs(dbg), _cols_to_slabs(dcmw[0:3]), _cols_to_slabs(dcfw)], axis=2)]
        token = scatter_start(l, "mid", sends, dx1)
        datt = _mm(dya, w["wa"], "nt", F32, name="branch_att_dx", after=token)
        dq, dk, dv, dct4, dcq4 = _attention_bwd(a["h"], a["c"], a["ct"], a["lse"], a["att"], datt, qblk,
                                                name="attention_bwd")
        dfl, dbf = _forget_prep_bwd(a["h"], bf_pad(l), dct4, dcq4, fblk, name="forget_prep_bwd")
        rep["b_forget"][l] = dbf[0, :N_HEADS]
        du, dvs, dlg, dlb, dsw, dbfull = _sgu_bwd(a["h"], sgu_ln_g[l:l + 1], sgu_ln_b[l:l + 1], sgu_w[l], bfull(l), dsgu,
                                                  ublk, name="sgu_bwd")
        rep["sgu_ln_g"][l], rep["sgu_ln_b"][l], rep["sgu_w"][l] = dlg, dlb, dsw
        rep["sgu_b"][l] = jnp.transpose(jnp.sum(dbfull.reshape(SGU_CHUNK, N_SGU_GROUPS, HEAD_DIM), axis=2))
        dh = jnp.concatenate([dgl[0], dgl[1], dgl[2], dq.astype(BF16), dk.astype(BF16), dv.astype(BF16), dbgate, dcg, dhc,
                              du, dvs, dfl.astype(BF16), jnp.zeros((s, w["w_in"].shape[0] - off["f"] - LANES), BF16)], axis=1)
        g_in = _mm(dh, a["xn"], "tn", BF16, name="proj_in_dw")
        token = scatter_start(l, "win", [_unpad_w_in(g_in, d).reshape(N_DEV, -1, d)], dx1)
        dx, rep["pre_mix_g"][l] = _mm_prenorm_bwd(dh, w["w_in"], a["xin"], pre_mix_g[l:l + 1], dx1, token, name="proj_in_dx")

    outs = {}
    t3 = lambda arr: jnp.transpose(arr, (0, 2, 1))

    def update(name_, slabs, w_, m_, v_, transposed=False):
        if transposed:
            w_, m_, v_ = t3(w_), t3(m_), t3(v_)
        shp = w_.shape
        w3 = w_.reshape((shp[0], -1, shp[-1])) if w_.ndim >= 3 else w_.reshape((1,) + shp)
        res = _adamw(slabs.reshape((N_DEV,) + w3.shape), w3, m_.reshape(w3.shape), v_.reshape(w3.shape),
                     name="adamw_" + name_)
        outs[name_] = tuple(t3(t.reshape(shp)) if transposed else t.reshape(shp) for t in res)
        return res[0]

    rep_names = ("pre_mix_g", "post_mix_g", "pre_ffn_g", "post_ffn_g", "b_forget", "sgu_ln_g", "sgu_ln_b", "sgu_w", "sgu_b")
    rep_w = dict(pre_mix_g=(pre_mix_g, m_pre_mix_g, v_pre_mix_g), post_mix_g=(post_mix_g, m_post_mix_g, v_post_mix_g),
                 pre_ffn_g=(pre_ffn_g, m_pre_ffn_g, v_pre_ffn_g), post_ffn_g=(post_ffn_g, m_post_ffn_g, v_post_ffn_g),
                 b_forget=(b_forget, m_b_forget, v_b_forget), sgu_ln_g=(sgu_ln_g, m_sgu_ln_g, v_sgu_ln_g),
                 sgu_ln_b=(sgu_ln_b, m_sgu_ln_b, v_sgu_ln_b), sgu_w=(sgu_w, m_sgu_w, v_sgu_w), sgu_b=(sgu_b, m_sgu_b, v_sgu_b))

    def pack(parts):
        rows = [jnp.pad(p.reshape(-1), (0, -p.size % LANES)).reshape(-1, LANES) for p in parts]
        rows = jnp.concatenate(rows, axis=0)
        return jnp.pad(rows, ((0, -rows.shape[0] % PACK_ROWS), (0, 0)))

    part = pack([jnp.stack([g.reshape(rep_w[k][0].shape[1:]) for g in rep[k]]) for k in rep_names])
    small_sems, small_src, small_land, _ = _exchange_start([part], [lax.empty((N_DEV,) + part.shape, F32)], dx, whole, slab,
                                                            name="gather_small_start")

    got_up, got_dn = scatter_finish("ffn", dx)
    done = [update("w_ffn_up", got_up, w_ffn_up, m_w_ffn_up, v_w_ffn_up, transposed=True),
            update("w_ffn_down", got_dn, w_ffn_down, m_w_ffn_down, v_w_ffn_down)]

    _, (gathered,) = _exchange_wait(small_sems, small_src, small_land, done, whole, slab, name="gather_small_wait")
    packed = [pack([rep_w[k][i] for k in rep_names]) for i in range(3)]
    res = _adamw(gathered.reshape(N_DEV, 1, -1, LANES), *[p.reshape(1, -1, LANES) for p in packed], name="adamw_replicated")
    row = 0
    for k in rep_names:
        shp = rep_w[k][0].shape
        size = math.prod(shp)
        nrows = -(-size // LANES)
        outs[k] = tuple(t[0, row:row + nrows].reshape(-1)[:size].reshape(shp) for t in res)
        row += nrows

    got_a, got_c, got_s, got_o, small = scatter_finish("mid", res[0])
    done = [update("w_branch_att", got_a, w_branch_att, m_w_branch_att, v_w_branch_att),
            update("w_branch_conv", got_c, w_branch_conv, m_w_branch_conv, v_w_branch_conv),
            update("w_branch_sgu", got_s, w_branch_sgu, m_w_branch_sgu, v_w_branch_sgu),
            update("w_out", got_o, w_out, m_w_out, v_w_out),
            update("b_gate", small[..., :lo], b_gate, m_b_gate, v_b_gate),
            update("conv_mix_w", small[..., lo:lo + ncm], conv_mix_w, m_conv_mix_w, v_conv_mix_w),
            update("conv_ffn_w", small[..., lo + ncm:], conv_ffn_w, m_conv_ffn_w, v_conv_ffn_w)]
    (got_in,) = scatter_finish("win", done)
    update("w_in", got_in, w_in, m_w_in, v_w_in, transposed=True)

    order = ("pre_mix_g", "post_mix_g", "pre_ffn_g", "post_ffn_g", "w_in", "b_forget", "b_gate", "conv_mix_w", "sgu_ln_g",
             "sgu_ln_b", "sgu_w", "sgu_b", "w_branch_att", "w_branch_conv", "w_branch_sgu", "w_out", "w_ffn_up",
             "conv_ffn_w", "w_ffn_down")
    grad_x = dx.reshape(x.shape)
    return (loss, grad_x, *[outs[k][0] for k in order], *[outs[k][1] for k in order], *[outs[k][2] for k in order],
            *[outs[k][3] for k in order])
```

```python
import functools
import math

import jax
import jax.numpy as jnp
from jax import lax
from jax.experimental import pallas as pl
from jax.experimental.pallas import tpu as pltpu

F32 = jnp.float32
BF16 = jnp.bfloat16

N_DEV = 8
HEAD_DIM = 64
N_HEADS = 8
D_ATT = 512
D_CONV = 256
D_SGU = 256
N_SGU_GROUPS = 4
SGU_CHUNK = 128
RMS_EPS = 1e-6
LN_EPS = 1e-5
ADAM_LR = 0.001
ADAM_B1 = 0.9
ADAM_B2 = 0.999
ADAM_EPS = 1e-08
ADAM_WD = 0.01
ADAM_STEP = 10
LANES = 128
VMEM_LIMIT = 56 * 1024 * 1024
ATT_TILE = 512
LOG2E = math.log2(math.e)
NEG = -1e30
MESH = pl.DeviceIdType.MESH


def _params(*sem):
    return pltpu.CompilerParams(dimension_semantics=sem if sem else None, vmem_limit_bytes=VMEM_LIMIT)


def _tile(n, cap):
    if n <= cap:
        return n
    t = cap - cap % LANES
    while n % t:
        t -= LANES
    return t


def _gelu(x):
    return 0.5 * x * (1.0 + jnp.tanh(math.sqrt(2.0 / math.pi) * (x + 0.044715 * (x * x * x))))


def _gelu_and_slope(x):
    k0, k1 = math.sqrt(2.0 / math.pi), 0.044715
    x2 = x * x
    t = jnp.tanh(x * (k0 + (k0 * k1) * x2))
    half = 0.5 * (1.0 + t)
    return x * half, half + (0.5 * x) * (1.0 - t * t) * (k0 + (3.0 * k0 * k1) * x2)


def _rms(x, g):
    r = lax.rsqrt(jnp.mean(x * x, axis=-1, keepdims=True) + RMS_EPS)
    return x * r * g


def _layer_norm(x, g, b):
    mu = jnp.mean(x, axis=-1, keepdims=True)
    xc = x - mu
    var = jnp.mean(xc * xc, axis=-1, keepdims=True)
    return xc * lax.rsqrt(var + LN_EPS) * g + b


def _shift_down(x, k, rows):
    return jnp.where(rows >= k, pltpu.roll(x, k, 0), 0.0)


def _shift_up(x, k, rows):
    s = x.shape[0]
    return jnp.where(rows < s - k, pltpu.roll(x, s - k, 0), 0.0)


def _conv3(x, w_ref, rows):
    return w_ref[2:3, :] * x + w_ref[1:2, :] * _shift_down(x, 1, rows) + w_ref[0:1, :] * _shift_down(x, 2, rows)


def _conv3_bwd(dy, x, w_ref, rows):
    up1, up2 = _shift_up(dy, 1, rows), _shift_up(dy, 2, rows)
    dx = w_ref[2:3, :] * dy + w_ref[1:2, :] * up1 + w_ref[0:1, :] * up2
    d2 = jnp.sum(dy * x, axis=0, keepdims=True)
    d1 = jnp.sum(up1 * x, axis=0, keepdims=True)
    d0 = jnp.sum(up2 * x, axis=0, keepdims=True)
    sub = lax.broadcasted_iota(jnp.int32, (8, x.shape[1]), 0)
    return dx, jnp.where(sub == 0, d0, jnp.where(sub == 1, d1, jnp.where(sub == 2, d2, 0.0)))


MM_VMEM_BUDGET = 40 * 1024 * 1024
MM_TILE_CAP = 1408


def _mm_tiles(m, n, k, out_bytes):
    def edges(d):
        return [t for t in range(LANES, min(d, MM_TILE_CAP) + 1, LANES) if d % t == 0] or [d]

    best = None
    for tm in edges(m):
        for tn in edges(n):
            if 2 * (2 * k * (tm + tn) + tm * tn * out_bytes) > MM_VMEM_BUDGET:
                continue
            for a_outer in (True, False):
                reads = k * m + (m // tm) * k * n if a_outer else k * n + (n // tn) * k * m
                traffic = 2 * reads + m * n * out_bytes
                key = (traffic, -tm * tn)
                if best is None or key < best[0]:
                    best = (key, (tm, tn, a_outer))
    return best[1]


def _mm(a, b, form, out_dtype, name, after=None):
    if form == "nn":
        (m, k), n = a.shape, b.shape[1]
    elif form == "nt":
        (m, k), n = a.shape, b.shape[0]
    else:
        (k, m), n = a.shape, b.shape[1]
    tm, tn, a_outer = _mm_tiles(m, n, k, jnp.dtype(out_dtype).itemsize)
    dims = {"nn": (((1,), (0,)), ((), ())), "nt": (((1,), (1,)), ((), ())), "tn": (((0,), (0,)), ((), ()))}[form]

    def body(a_ref, b_ref, *rest):
        o_ref = rest[-1]
        o_ref[...] = lax.dot_general(a_ref[...], b_ref[...], dims, preferred_element_type=F32).astype(o_ref.dtype)

    ij = (lambda g0, g1: (g0, g1)) if a_outer else (lambda g0, g1: (g1, g0))
    a_spec = (pl.BlockSpec((k, tm), lambda g0, g1: (0, ij(g0, g1)[0])) if form == "tn"
              else pl.BlockSpec((tm, k), lambda g0, g1: (ij(g0, g1)[0], 0)))
    b_spec = (pl.BlockSpec((tn, k), lambda g0, g1: (ij(g0, g1)[1], 0)) if form == "nt"
              else pl.BlockSpec((k, tn), lambda g0, g1: (0, ij(g0, g1)[1])))
    extra = [] if after is None else [pl.BlockSpec((8, LANES), lambda g0, g1: (0, 0))]
    return pl.pallas_call(
        body, name=name, grid=(m // tm, n // tn) if a_outer else (n // tn, m // tm),
        in_specs=[a_spec, b_spec] + extra, out_specs=pl.BlockSpec((tm, tn), lambda g0, g1: ij(g0, g1)),
        out_shape=jax.ShapeDtypeStruct((m, n), out_dtype),
        compiler_params=_params("parallel", "arbitrary"),
    )(a, b, *([] if after is None else [after]))


def _postnorm_bwd_mm(o, g, dx, b, name):
    s, d = o.shape
    n = b.shape[0]
    tm, tn = _tile(s, 512), _tile(n, MM_TILE_CAP)

    def body(o_ref, g_ref, dx_ref, b_ref, do_ref, dg_ref, out_ref):
        i, j = pl.program_id(0), pl.program_id(1)

        @pl.when(j == 0)
        def _():
            _, vjp = jax.vjp(_rms, o_ref[...], g_ref[...])
            d_o, dg = vjp(dx_ref[...])
            do_ref[...] = d_o.astype(BF16)

            @pl.when(i == 0)
            def _():
                dg_ref[...] = jnp.zeros_like(dg_ref)

            dg_ref[...] += dg

        out_ref[...] = lax.dot_general(do_ref[...], b_ref[...], (((1,), (1,)), ((), ())), preferred_element_type=F32)

    row = pl.BlockSpec((tm, d), lambda i, j: (i, 0))
    vec = pl.BlockSpec((1, d), lambda i, j: (0, 0))
    return pl.pallas_call(
        body, name=name, grid=(s // tm, n // tn),
        in_specs=[row, vec, row, pl.BlockSpec((tn, d), lambda i, j: (j, 0))],
        out_specs=[row, vec, pl.BlockSpec((tm, tn), lambda i, j: (i, j))],
        out_shape=[jax.ShapeDtypeStruct((s, d), BF16), jax.ShapeDtypeStruct((1, d), F32), jax.ShapeDtypeStruct((s, n), F32)],
        compiler_params=_params("arbitrary", "arbitrary"),
    )(o, g, dx, b)


def _mm_prenorm_bwd(a, b, x, g, dres, after, name):
    s, k = a.shape
    d = b.shape[1]
    tm = _tile(s, 512)

    def body(a_ref, b_ref, x_ref, g_ref, dres_ref, after_ref, dx_ref, dg_ref):
        dxn = jnp.dot(a_ref[...], b_ref[...], preferred_element_type=F32)
        _, vjp = jax.vjp(_rms, x_ref[...], g_ref[...])
        dx, dg = vjp(dxn)
        dx_ref[...] = dres_ref[...] + dx

        @pl.when(pl.program_id(0) == 0)
        def _():
            dg_ref[...] = jnp.zeros_like(dg_ref)

        dg_ref[...] += dg

    row = pl.BlockSpec((tm, d), lambda i: (i, 0))
    vec = pl.BlockSpec((1, d), lambda i: (0, 0))
    return pl.pallas_call(
        body, name=name, grid=(s // tm,),
        in_specs=[pl.BlockSpec((tm, k), lambda i: (i, 0)),
                  pl.BlockSpec((k, d), lambda i: (0, 0), pipeline_mode=pl.Buffered(1)),
                  row, vec, row, pl.BlockSpec((8, LANES), lambda i: (0, 0))],
        out_specs=[row, vec], out_shape=[jax.ShapeDtypeStruct((s, d), F32), jax.ShapeDtypeStruct((1, d), F32)],
        compiler_params=_params("arbitrary"),
    )(a, b, x, g, dres, after)


def _prenorm(x, g, name):
    s, d = x.shape
    tm = _tile(s, 512)

    def body(x_ref, g_ref, o_ref):
        o_ref[...] = _rms(x_ref[...], g_ref[...]).astype(BF16)

    return pl.pallas_call(
        body, name=name, grid=(s // tm,),
        in_specs=[pl.BlockSpec((tm, d), lambda i: (i, 0)), pl.BlockSpec((1, d), lambda i: (0, 0))],
        out_specs=pl.BlockSpec((tm, d), lambda i: (i, 0)),
        out_shape=jax.ShapeDtypeStruct((s, d), BF16), compiler_params=_params("parallel"),
    )(x, g)


def _mm_postnorm(a, b, x, g_post, g_next, name):
    s, k = a.shape
    d = b.shape[1]
    tm = _tile(s, 512)

    def body(a_ref, b_ref, x_ref, gp_ref, gn_ref, o_ref, x1_ref, xn_ref):
        o = jnp.dot(a_ref[...], b_ref[...], preferred_element_type=F32)
        o_ref[...] = o
        x1 = x_ref[...] + _rms(o, gp_ref[...])
        x1_ref[...] = x1
        xn_ref[...] = _rms(x1, gn_ref[...]).astype(BF16)

    row = pl.BlockSpec((tm, d), lambda i: (i, 0))
    vec = pl.BlockSpec((1, d), lambda i: (0, 0))
    return pl.pallas_call(
        body, name=name, grid=(s // tm,),
        in_specs=[pl.BlockSpec((tm, k), lambda i: (i, 0)),
                  pl.BlockSpec((k, d), lambda i: (0, 0), pipeline_mode=pl.Buffered(1)), row, vec, vec],
        out_specs=[row, row, row],
        out_shape=[jax.ShapeDtypeStruct((s, d), F32), jax.ShapeDtypeStruct((s, d), F32), jax.ShapeDtypeStruct((s, d), BF16)],
        compiler_params=_params("parallel"),
    )(a, b, x, g_post, g_next)


def _mm_postnorm_loss(a, b, x, g_post, target, name):
    s, k = a.shape
    d = b.shape[1]
    tm = _tile(s, 512)

    def body(a_ref, b_ref, x_ref, gp_ref, t_ref, o_ref, dy_ref, acc_ref):
        o = jnp.dot(a_ref[...], b_ref[...], preferred_element_type=F32)
        o_ref[...] = o
        e = x_ref[...] + _rms(o, gp_ref[...]) - t_ref[...]
        dy_ref[...] = e / d

        @pl.when(pl.program_id(0) == 0)
        def _():
            acc_ref[...] = jnp.zeros_like(acc_ref)

        acc_ref[...] += jnp.sum(jnp.sum(e * e, axis=1, keepdims=True), axis=0, keepdims=True)

    row = pl.BlockSpec((tm, d), lambda i: (i, 0))
    return pl.pallas_call(
        body, name=name, grid=(s // tm,),
        in_specs=[pl.BlockSpec((tm, k), lambda i: (i, 0)),
                  pl.BlockSpec((k, d), lambda i: (0, 0), pipeline_mode=pl.Buffered(1)), row,
                  pl.BlockSpec((1, d), lambda i: (0, 0)), row],
        out_specs=[row, row, pl.BlockSpec((1, LANES), lambda i: (0, 0))],
        out_shape=[jax.ShapeDtypeStruct((s, d), F32), jax.ShapeDtypeStruct((s, d), F32), jax.ShapeDtypeStruct((1, LANES), F32)],
        compiler_params=_params("arbitrary"),
    )(a, b, x, g_post, target)


def _postnorm_bwd(o, g, dx, name):
    s, d = o.shape
    tm = _tile(s, 512)

    def body(o_ref, g_ref, dx_ref, do_ref, dg_ref):
        _, vjp = jax.vjp(_rms, o_ref[...], g_ref[...])
        d_o, dg = vjp(dx_ref[...])
        do_ref[...] = d_o.astype(BF16)

        @pl.when(pl.program_id(0) == 0)
        def _():
            dg_ref[...] = jnp.zeros_like(dg_ref)

        dg_ref[...] += dg

    row = pl.BlockSpec((tm, d), lambda i: (i, 0))
    vec = pl.BlockSpec((1, d), lambda i: (0, 0))
    return pl.pallas_call(
        body, name=name, grid=(s // tm,), in_specs=[row, vec, row], out_specs=[row, vec],
        out_shape=[jax.ShapeDtypeStruct((s, d), BF16), jax.ShapeDtypeStruct((1, d), F32)],
        compiler_params=_params("arbitrary"),
    )(o, g, dx)


def _log_sigmoid(z):
    return jnp.minimum(z, 0.0) - jnp.log(1.0 + jnp.exp(-jnp.abs(z)))


def _forget_prep(h, bf_pad, fblk, name):
    s = h.shape[0]

    def body(f_ref, b_ref, c_ref, ct_ref):
        c = _log_sigmoid(f_ref[...] + b_ref[...])
        rows = lax.broadcasted_iota(jnp.int32, c.shape, 0)
        k = 1
        while k < s:
            c = c + _shift_down(c, k, rows)
            k *= 2
        c_ref[...] = c
        ct_ref[...] = jnp.transpose(c)[0:8, :]

    return pl.pallas_call(
        body, name=name, grid=(1,),
        in_specs=[pl.BlockSpec((s, LANES), lambda i: (0, fblk)), pl.BlockSpec((1, LANES), lambda i: (0, 0))],
        out_specs=[pl.BlockSpec((s, LANES), lambda i: (0, 0)), pl.BlockSpec((8, s), lambda i: (0, 0))],
        out_shape=[jax.ShapeDtypeStruct((s, LANES), F32), jax.ShapeDtypeStruct((8, s), F32)],
        compiler_params=_params("arbitrary"),
    )(h, bf_pad)


def _forget_prep_bwd(h, bf_pad, dct, dcq, fblk, name):
    s = h.shape[0]
    pairs = N_HEADS // 2

    def body(f_ref, b_ref, dct_ref, dcq_ref, df_ref, db_ref):
        dct = dct_ref[0]
        dcq = dcq_ref[0]
        for p in range(1, pairs):
            dct = dct + pltpu.roll(dct_ref[p], 2 * p, 0)
            dcq = dcq + pltpu.roll(dcq_ref[p], 2 * p, 1)
        dc = dcq + jnp.transpose(jnp.concatenate([dct, jnp.zeros((LANES - 8, s), F32)], axis=0))
        rows = lax.broadcasted_iota(jnp.int32, dc.shape, 0)
        k = 1
        while k < s:
            dc = dc + _shift_up(dc, k, rows)
            k *= 2
        z = f_ref[...] + b_ref[...]
        lane = lax.broadcasted_iota(jnp.int32, dc.shape, 1)
        df = jnp.where(lane < N_HEADS, dc * jax.nn.sigmoid(-z), 0.0)
        df_ref[...] = df
        db_ref[...] = jnp.sum(df, axis=0, keepdims=True)

    return pl.pallas_call(
        body, name=name, grid=(1,),
        in_specs=[pl.BlockSpec((s, LANES), lambda i: (0, fblk)), pl.BlockSpec((1, LANES), lambda i: (0, 0)),
                  pl.BlockSpec((pairs, 8, s), lambda i: (0, 0, 0)), pl.BlockSpec((pairs, s, LANES), lambda i: (0, 0, 0))],
        out_specs=[pl.BlockSpec((s, LANES), lambda i: (0, 0)), pl.BlockSpec((1, LANES), lambda i: (0, 0))],
        out_shape=[jax.ShapeDtypeStruct((s, LANES), F32), jax.ShapeDtypeStruct((1, LANES), F32)],
        compiler_params=_params("arbitrary"),
    )(h, bf_pad, dct, dcq)


def _pick_lane(blk, idx):
    lane = lax.broadcasted_iota(jnp.int32, blk.shape, 1)
    return jnp.sum(jnp.where(lane == idx, blk, 0.0), axis=1, keepdims=True)


def _pick_row(blk, idx):
    sub = lax.broadcasted_iota(jnp.int32, blk.shape, 0)
    return jnp.sum(jnp.where(sub == idx, blk, 0.0), axis=0, keepdims=True)


def _attention_fwd(h, c, ct, qblk, name):
    s = h.shape[0]
    t = _tile(s, ATT_TILE)
    nq = s // t
    scale = HEAD_DIM ** -0.5
    nt_dims = (((1,), (1,)), ((), ()))

    def body(q_ref, k_ref, v_ref, c_ref, ct_ref, o_ref, lse_ref):
        p = pl.program_id(0)
        i = pl.program_id(1)
        lane = lax.broadcasted_iota(jnp.int32, (1, LANES), 1)
        first = lane < HEAD_DIM
        q = q_ref[...] * (scale * LOG2E)
        qa = jnp.where(first, q, 0.0).astype(BF16)
        qb = jnp.where(first, 0.0, q).astype(BF16)
        cblk = c_ref[...]
        cta = _pick_lane(cblk, 2 * p) * LOG2E
        ctb = _pick_lane(cblk, 2 * p + 1) * LOG2E

        def step(j, carry, diagonal):
            ma, la, mb, lb, acc = carry
            off = pl.multiple_of(j * t, t)
            k = k_ref[pl.ds(off, t), :].astype(BF16)
            v = v_ref[pl.ds(off, t), :].astype(BF16)
            crow = ct_ref[:, pl.ds(off, t)] * LOG2E

            def one(qh, cth, hd, m_old, l_old):
                sc = lax.dot_general(qh, k, nt_dims, preferred_element_type=F32) - _pick_row(crow, hd)
                if diagonal:
                    keep = lax.broadcasted_iota(jnp.int32, (t, t), 0) >= lax.broadcasted_iota(jnp.int32, (t, t), 1)
                    sc = jnp.where(keep, sc, NEG)
                m_new = jnp.maximum(m_old, jnp.max(sc, axis=1, keepdims=True) + cth)
                pr = jnp.exp2(sc - (m_new - cth))
                alpha = jnp.exp2(m_old - m_new)
                l_new = alpha * l_old + jnp.sum(pr, axis=1, keepdims=True)
                pv = jnp.dot(pr.astype(BF16), v, preferred_element_type=F32)
                return m_new, l_new, alpha, pv

            ma2, la2, aa, pva = one(qa, cta, 2 * p, ma, la)
            mb2, lb2, ab, pvb = one(qb, ctb, 2 * p + 1, mb, lb)
            acc = jnp.where(first, aa * acc + pva, ab * acc + pvb)
            return ma2, la2, mb2, lb2, acc

        init = (jnp.full((t, 1), NEG, F32), jnp.zeros((t, 1), F32), jnp.full((t, 1), NEG, F32),
                jnp.zeros((t, 1), F32), jnp.zeros((t, LANES), F32))
        carry = lax.fori_loop(0, i, lambda j, carry: step(j, carry, False), init)
        ma, la, mb, lb, acc = step(i, carry, True)
        o_ref[...] = (acc / jnp.where(first, la, lb)).astype(BF16)
        lse_ref[0] = jnp.broadcast_to(ma + jnp.log2(la), (t, LANES))
        lse_ref[1] = jnp.broadcast_to(mb + jnp.log2(lb), (t, LANES))

    return pl.pallas_call(
        body, name=name, grid=(N_HEADS // 2, nq),
        in_specs=[pl.BlockSpec((t, LANES), lambda p, i: (i, qblk + p)),
                  pl.BlockSpec((s, LANES), lambda p, i: (0, qblk + 4 + p)),
                  pl.BlockSpec((s, LANES), lambda p, i: (0, qblk + 8 + p)),
                  pl.BlockSpec((t, LANES), lambda p, i: (i, 0)),
                  pl.BlockSpec((8, s), lambda p, i: (0, 0))],
        out_specs=[pl.BlockSpec((t, LANES), lambda p, i: (i, p)),
                   pl.BlockSpec((2, t, LANES), lambda p, i: (p, i, 0))],
        out_shape=[jax.ShapeDtypeStruct((s, D_ATT), BF16), jax.ShapeDtypeStruct((N_HEADS, s, LANES), F32)],
        compiler_params=_params("parallel", "arbitrary"),
    )(h, h, h, c, ct)


def _attention_bwd(h, c, ct, lse, att, datt, after, qblk, name):
    s = h.shape[0]
    t = _tile(s, ATT_TILE)
    nq = s // t
    scale = HEAD_DIM ** -0.5
    nt_dims = (((1,), (1,)), ((), ()))
    tn_dims = (((0,), (0,)), ((), ()))

    def body(q_ref, k_ref, v_ref, c_ref, ct_ref, lse_ref, o_ref, do_ref, after_ref, dq_ref, dk_ref, dv_ref, dct_ref, dcq_ref):
        p = pl.program_id(0)
        j = pl.program_id(1)
        lane = lax.broadcasted_iota(jnp.int32, (1, LANES), 1)
        first = lane < HEAD_DIM
        kf = k_ref[...]
        vf = v_ref[...]
        k = kf.astype(BF16)
        ka = jnp.where(first, kf, 0.0).astype(BF16)
        kb = jnp.where(first, 0.0, kf).astype(BF16)
        va = jnp.where(first, vf, 0.0).astype(BF16)
        vb = jnp.where(first, 0.0, vf).astype(BF16)
        crow = ct_ref[...] * LOG2E
        csa = _pick_row(crow, 2 * p)
        csb = _pick_row(crow, 2 * p + 1)

        @pl.when(j == 0)
        def _():
            dq_ref[...] = jnp.zeros_like(dq_ref)
            dcq_ref[...] = jnp.zeros_like(dcq_ref)

        def step(i, carry, diagonal):
            dka, dkb, dva, dvb, dca, dcb = carry
            off = pl.multiple_of(i * t, t)
            rows = pl.ds(off, t)
            q = (q_ref[rows, :] * (scale * LOG2E)).astype(BF16)
            dof = do_ref[rows, :]
            do = dof.astype(BF16)
            prod = dof * o_ref[rows, :].astype(F32)
            cblk = c_ref[rows, :] * LOG2E

            def one(kh, vh, hd, csh, lse_h):
                sc = lax.dot_general(q, kh, nt_dims, preferred_element_type=F32) - csh
                if diagonal:
                    keep = lax.broadcasted_iota(jnp.int32, (t, t), 0) >= lax.broadcasted_iota(jnp.int32, (t, t), 1)
                    sc = jnp.where(keep, sc, NEG)
                pr = jnp.exp2(sc - (jnp.max(lse_h, axis=1, keepdims=True) - _pick_lane(cblk, hd)))
                dp = lax.dot_general(do, vh, nt_dims, preferred_element_type=F32)
                return pr, dp

            pra, dpa = one(ka, va, 2 * p, csa, lse_ref[0, rows, :])
            prb, dpb = one(kb, vb, 2 * p + 1, csb, lse_ref[1, rows, :])
            dela = jnp.sum(jnp.where(first, prod, 0.0), axis=1, keepdims=True)
            delb = jnp.sum(jnp.where(first, 0.0, prod), axis=1, keepdims=True)
            dsa = pra * (dpa - dela)
            dsb = prb * (dpb - delb)
            dsa16 = dsa.astype(BF16)
            dsb16 = dsb.astype(BF16)
            dva = dva + lax.dot_general(pra.astype(BF16), do, tn_dims, preferred_element_type=F32)
            dvb = dvb + lax.dot_general(prb.astype(BF16), do, tn_dims, preferred_element_type=F32)
            dka = dka + lax.dot_general(dsa16, q, tn_dims, preferred_element_type=F32)
            dkb = dkb + lax.dot_general(dsb16, q, tn_dims, preferred_element_type=F32)
            dqa = jnp.dot(dsa16, k, preferred_element_type=F32)
            dqb = jnp.dot(dsb16, k, preferred_element_type=F32)
            dq_ref[rows, :] += scale * jnp.where(first, dqa, dqb)
            dca = dca - jnp.sum(dsa, axis=0, keepdims=True)
            dcb = dcb - jnp.sum(dsb, axis=0, keepdims=True)
            dcq_ref[rows, :] += jnp.where(lane == 0, jnp.sum(dsa, axis=1, keepdims=True),
                                          jnp.where(lane == 1, jnp.sum(dsb, axis=1, keepdims=True), 0.0))
            return dka, dkb, dva, dvb, dca, dcb

        z = jnp.zeros((t, LANES), F32)
        zr = jnp.zeros((1, t), F32)
        carry = step(j, (z, z, z, z, zr, zr), True)
        dka, dkb, dva, dvb, dca, dcb = lax.fori_loop(j + 1, nq, lambda i, carry: step(i, carry, False), carry)
        dk_ref[...] = jnp.where(first, dka, dkb) * (1.0 / LOG2E)
        dv_ref[...] = jnp.where(first, dva, dvb)
        sub = lax.broadcasted_iota(jnp.int32, (8, t), 0)
        dct_ref[...] = jnp.where(sub == 0, dca, jnp.where(sub == 1, dcb, 0.0))

    full = lambda blk: pl.BlockSpec((s, LANES), blk)
    return pl.pallas_call(
        body, name=name, grid=(N_HEADS // 2, nq),
        in_specs=[full(lambda p, j: (0, qblk + p)),
                  pl.BlockSpec((t, LANES), lambda p, j: (j, qblk + 4 + p)),
                  pl.BlockSpec((t, LANES), lambda p, j: (j, qblk + 8 + p)),
                  full(lambda p, j: (0, 0)),
                  pl.BlockSpec((8, t), lambda p, j: (0, j)),
                  pl.BlockSpec((2, s, LANES), lambda p, j: (p, 0, 0)),
                  full(lambda p, j: (0, p)),
                  full(lambda p, j: (0, p)),
                  pl.BlockSpec((8, LANES), lambda p, j: (0, 0))],
        out_specs=[full(lambda p, j: (0, p)),
                   pl.BlockSpec((t, LANES), lambda p, j: (j, p)),
                   pl.BlockSpec((t, LANES), lambda p, j: (j, p)),
                   pl.BlockSpec((None, 8, t), lambda p, j: (p, 0, j)),
                   pl.BlockSpec((None, s, LANES), lambda p, j: (p, 0, 0))],
        out_shape=[jax.ShapeDtypeStruct((s, D_ATT), F32), jax.ShapeDtypeStruct((s, D_ATT), F32),
                   jax.ShapeDtypeStruct((s, D_ATT), F32), jax.ShapeDtypeStruct((N_HEADS // 2, 8, s), F32),
                   jax.ShapeDtypeStruct((N_HEADS // 2, s, LANES), F32)],
        compiler_params=_params("arbitrary", "arbitrary"),
    )(h, h, h, c, ct, lse, att, datt, after)


def _sconv_fwd(h, w, bgblk, name):
    s = h.shape[0]
    nblk = D_CONV // LANES

    def body(bg_ref, cg_ref, hc_ref, w_ref, y_ref):
        rows = lax.broadcasted_iota(jnp.int32, (s, LANES), 0)
        y_ref[...] = (bg_ref[...] * _conv3(cg_ref[...] * hc_ref[...], w_ref, rows)).astype(BF16)

    col = lambda base: pl.BlockSpec((s, LANES), lambda j: (0, base + j))
    return pl.pallas_call(
        body, name=name, grid=(nblk,),
        in_specs=[col(bgblk), col(bgblk + nblk), col(bgblk + 2 * nblk), pl.BlockSpec((3, LANES), lambda j: (0, j))],
        out_specs=pl.BlockSpec((s, LANES), lambda j: (0, j)),
        out_shape=jax.ShapeDtypeStruct((s, D_CONV), BF16), compiler_params=_params("parallel"),
    )(h, h, h, w)


def _sconv_bwd(h, w, dy, bgblk, name):
    s = h.shape[0]
    nblk = D_CONV // LANES

    def body(bg_ref, cg_ref, hc_ref, w_ref, dy_ref, dbg_ref, dcg_ref, dhc_ref, dw_ref):
        rows = lax.broadcasted_iota(jnp.int32, (s, LANES), 0)
        cg, hc, dy, w = cg_ref[...], hc_ref[...], dy_ref[...], w_ref
        xin = cg * hc
        dbg_ref[...] = (dy * _conv3(xin, w, rows)).astype(BF16)
        dxin, dw_ref[...] = _conv3_bwd(dy * bg_ref[...], xin, w, rows)
        dcg_ref[...] = (dxin * hc).astype(BF16)
        dhc_ref[...] = (dxin * cg).astype(BF16)

    col = lambda base: pl.BlockSpec((s, LANES), lambda j: (0, base + j))
    return pl.pallas_call(
        body, name=name, grid=(nblk,),
        in_specs=[col(bgblk), col(bgblk + nblk), col(bgblk + 2 * nblk), pl.BlockSpec((3, LANES), lambda j: (0, j)), col(0)],
        out_specs=[col(0), col(0), col(0), pl.BlockSpec((8, LANES), lambda j: (0, j))],
        out_shape=[jax.ShapeDtypeStruct((s, D_CONV), BF16)] * 3 + [jax.ShapeDtypeStruct((8, D_CONV), F32)],
        compiler_params=_params("parallel"),
    )(h, h, h, w, dy)


def _sgu_group_masks():
    lane = lax.broadcasted_iota(jnp.int32, (1, D_SGU), 1)
    return [(lane // HEAD_DIM) == g for g in range(N_SGU_GROUPS)]


def _sgu_tril():
    r = lax.broadcasted_iota(jnp.int32, (SGU_CHUNK, SGU_CHUNK), 0)
    c = lax.broadcasted_iota(jnp.int32, (SGU_CHUNK, SGU_CHUNK), 1)
    return r >= c


def _sgu_fwd(h, ln_g, ln_b, w_s, b_full, ublk, name):
    s = h.shape[0]
    tr = _tile(s, 512)
    nch = tr // SGU_CHUNK

    def body(u_ref, v_ref, g_ref, b_ref, w_ref, bf_ref, y_ref):
        masks = _sgu_group_masks()
        tril = _sgu_tril()
        wm = [jnp.where(tril, w_ref[g], 0.0).astype(BF16) for g in range(N_SGU_GROUPS)]
        vn = _layer_norm(_gelu(v_ref[...]), g_ref[...], b_ref[...])
        for ch in range(nch):
            rows = pl.ds(ch * SGU_CHUNK, SGU_CHUNK)
            vc = vn[ch * SGU_CHUNK:(ch + 1) * SGU_CHUNK, :]
            mixed = bf_ref[...]
            for g in range(N_SGU_GROUPS):
                mixed = mixed + jnp.dot(wm[g], jnp.where(masks[g], vc, 0.0).astype(BF16), preferred_element_type=F32)
            y_ref[rows, :] = (_gelu(u_ref[rows, :]) * mixed).astype(BF16)

    row = lambda blk: pl.BlockSpec((tr, D_SGU), lambda i: (i, blk))
    vec = pl.BlockSpec((1, D_SGU), lambda i: (0, 0))
    return pl.pallas_call(
        body, name=name, grid=(s // tr,),
        in_specs=[row(ublk), row(ublk + 1), vec, vec,
                  pl.BlockSpec((N_SGU_GROUPS, SGU_CHUNK, SGU_CHUNK), lambda i: (0, 0, 0)),
                  pl.BlockSpec((SGU_CHUNK, D_SGU), lambda i: (0, 0))],
        out_specs=row(0), out_shape=jax.ShapeDtypeStruct((s, D_SGU), BF16), compiler_params=_params("parallel"),
    )(h, h, ln_g, ln_b, w_s, b_full)


def _sgu_bwd(h, ln_g, ln_b, w_s, b_full, dy, ublk, name):
    s = h.shape[0]
    tr = _tile(s, 512)
    nch = tr // SGU_CHUNK
    nt_dims = (((1,), (1,)), ((), ()))

    def norm(v, g, b):
        return _layer_norm(_gelu(v), g, b)

    def body(u_ref, v_ref, g_ref, b_ref, w_ref, bf_ref, dy_ref, du_ref, dv_ref, dg_ref, db_ref, dw_ref, dbf_ref):
        masks = _sgu_group_masks()
        tril = _sgu_tril()
        wf = [jnp.where(tril, w_ref[g], 0.0) for g in range(N_SGU_GROUPS)]
        wm = [w.astype(BF16) for w in wf]
        wmt = [jnp.transpose(w).astype(BF16) for w in wf]
        vn, vjp = jax.vjp(norm, v_ref[...], g_ref[...], b_ref[...])

        @pl.when(pl.program_id(0) == 0)
        def _():
            dg_ref[...] = jnp.zeros_like(dg_ref)
            db_ref[...] = jnp.zeros_like(db_ref)
            dw_ref[...] = jnp.zeros_like(dw_ref)
            dbf_ref[...] = jnp.zeros_like(dbf_ref)

        dvn_parts = []
        for ch in range(nch):
            rows = pl.ds(ch * SGU_CHUNK, SGU_CHUNK)
            vc = vn[ch * SGU_CHUNK:(ch + 1) * SGU_CHUNK, :]
            vc16 = vc.astype(BF16)
            mixed = bf_ref[...]
            for g in range(N_SGU_GROUPS):
                mixed = mixed + jnp.dot(wm[g], jnp.where(masks[g], vc, 0.0).astype(BF16), preferred_element_type=F32)
            dy = dy_ref[rows, :]
            ug, slope = _gelu_and_slope(u_ref[rows, :])
            du_ref[rows, :] = (dy * mixed * slope).astype(BF16)
            dmixed = dy * ug
            dbf_ref[...] += dmixed
            dvc = jnp.zeros((SGU_CHUNK, D_SGU), F32)
            for g in range(N_SGU_GROUPS):
                dm16 = jnp.where(masks[g], dmixed, 0.0).astype(BF16)
                dw_ref[g] += jnp.where(tril, lax.dot_general(dm16, vc16, nt_dims, preferred_element_type=F32), 0.0)
                dvc = dvc + jnp.dot(wmt[g], dm16, preferred_element_type=F32)
            dvn_parts.append(dvc)
        dv, dg, db = vjp(jnp.concatenate(dvn_parts, axis=0))
        dv_ref[...] = dv.astype(BF16)
        dg_ref[...] += dg
        db_ref[...] += db

    row = lambda blk: pl.BlockSpec((tr, D_SGU), lambda i: (i, blk))
    vec = pl.BlockSpec((1, D_SGU), lambda i: (0, 0))
    wsp = pl.BlockSpec((N_SGU_GROUPS, SGU_CHUNK, SGU_CHUNK), lambda i: (0, 0, 0))
    bsp = pl.BlockSpec((SGU_CHUNK, D_SGU), lambda i: (0, 0))
    return pl.pallas_call(
        body, name=name, grid=(s // tr,),
        in_specs=[row(ublk), row(ublk + 1), vec, vec, wsp, bsp, row(0)],
        out_specs=[row(0), row(0), vec, vec, wsp, bsp],
        out_shape=[jax.ShapeDtypeStruct((s, D_SGU), BF16), jax.ShapeDtypeStruct((s, D_SGU), BF16),
                   jax.ShapeDtypeStruct((1, D_SGU), F32), jax.ShapeDtypeStruct((1, D_SGU), F32),
                   jax.ShapeDtypeStruct((N_SGU_GROUPS, SGU_CHUNK, SGU_CHUNK), F32),
                   jax.ShapeDtypeStruct((SGU_CHUNK, D_SGU), F32)],
        compiler_params=_params("arbitrary"),
    )(h, h, ln_g, ln_b, w_s, b_full, dy)


def _merge_fwd(h, b_gate, att, yc, ys, wa, wc, ws, name):
    s, d = att.shape[0], wa.shape[1]
    tm, tn = _tile(s, 512), _tile(d, 512)
    nj = d // tn

    def body(g0_ref, g1_ref, g2_ref, bg_ref, a_ref, c_ref, s_ref, wa_ref, wc_ref, ws_ref, o_ref):
        acc = jax.nn.sigmoid(g0_ref[...] + bg_ref[0:1, :]) * jnp.dot(a_ref[...], wa_ref[...], preferred_element_type=F32)
        acc += jax.nn.sigmoid(g1_ref[...] + bg_ref[1:2, :]) * jnp.dot(c_ref[...], wc_ref[...], preferred_element_type=F32)
        acc += jax.nn.sigmoid(g2_ref[...] + bg_ref[2:3, :]) * jnp.dot(s_ref[...], ws_ref[...], preferred_element_type=F32)
        o_ref[...] = acc.astype(BF16)

    gate = lambda b: pl.BlockSpec((tm, tn), lambda j, i: (i, b * nj + j))
    act = lambda k: pl.BlockSpec((tm, k), lambda j, i: (i, 0))
    wgt = lambda k: pl.BlockSpec((k, tn), lambda j, i: (0, j))
    return pl.pallas_call(
        body, name=name, grid=(nj, s // tm),
        in_specs=[gate(0), gate(1), gate(2), pl.BlockSpec((3, tn), lambda j, i: (0, j)),
                  act(D_ATT), act(D_CONV), act(D_SGU), wgt(D_ATT), wgt(D_CONV), wgt(D_SGU)],
        out_specs=pl.BlockSpec((tm, tn), lambda j, i: (i, j)),
        out_shape=jax.ShapeDtypeStruct((s, d), BF16), compiler_params=_params("parallel", "arbitrary"),
    )(h, h, h, b_gate, att, yc, ys, wa, wc, ws)


def _merge_bwd(h, b_gate, att, yc, ys, wa, wc, ws, dm, name):
    s, d = att.shape[0], wa.shape[1]
    tm = _tile(s, 512)
    nt_dims, tn_dims = (((1,), (1,)), ((), ())), (((0,), (0,)), ((), ()))
    widths = (D_ATT, D_CONV, D_SGU)

    def body(g0_ref, g1_ref, g2_ref, bg_ref, a_ref, c_ref, s_ref, wa_ref, wc_ref, ws_ref, dm_ref,
             da_ref, dc_ref, ds_ref, ga_ref, gc_ref, gs_ref, dgl_ref, dbg_ref, acc_a, acc_c, acc_s):
        i = pl.program_id(0)

        @pl.when(i == 0)
        def _():
            dbg_ref[...] = jnp.zeros_like(dbg_ref)
            for acc in (acc_a, acc_c, acc_s):
                acc[...] = jnp.zeros_like(acc)

        dm = dm_ref[...]
        sums = []
        for b, (g_ref, x_ref, w_ref, dx_ref, acc) in enumerate((
                (g0_ref, a_ref, wa_ref, da_ref, acc_a), (g1_ref, c_ref, wc_ref, dc_ref, acc_c),
                (g2_ref, s_ref, ws_ref, ds_ref, acc_s))):
            gate = jax.nn.sigmoid(g_ref[...] + bg_ref[b:b + 1, :])
            x, w = x_ref[...], w_ref[...]
            y = jnp.dot(x, w, preferred_element_type=F32)
            dy = (dm * gate).astype(BF16)
            dgl = dm * y * gate * (1.0 - gate)
            dgl_ref[b] = dgl.astype(BF16)
            sums.append(jnp.sum(dgl, axis=0, keepdims=True))
            dx_ref[...] = lax.dot_general(dy, w, nt_dims, preferred_element_type=F32)
            acc[...] += lax.dot_general(x, dy, tn_dims, preferred_element_type=F32)
        sub = lax.broadcasted_iota(jnp.int32, (3, d), 0)
        dbg_ref[...] += jnp.where(sub == 0, sums[0], jnp.where(sub == 1, sums[1], sums[2]))

        @pl.when(i == pl.num_programs(0) - 1)
        def _():
            for g_out, acc in ((ga_ref, acc_a), (gc_ref, acc_c), (gs_ref, acc_s)):
                g_out[...] = acc[...].astype(BF16)

    gate = lambda b: pl.BlockSpec((tm, d), lambda i: (i, b))
    act = lambda k: pl.BlockSpec((tm, k), lambda i: (i, 0))
    wgt = lambda k: pl.BlockSpec((k, d), lambda i: (0, 0), pipeline_mode=pl.Buffered(1))
    res = pl.pallas_call(
        body, name=name, grid=(s // tm,),
        in_specs=[gate(0), gate(1), gate(2), pl.BlockSpec((3, d), lambda i: (0, 0)),
                  act(D_ATT), act(D_CONV), act(D_SGU), wgt(D_ATT), wgt(D_CONV), wgt(D_SGU), act(d)],
        out_specs=[act(k) for k in widths] + [pl.BlockSpec((k, d), lambda i: (0, 0)) for k in widths]
        + [pl.BlockSpec((3, tm, d), lambda i: (0, i, 0)), pl.BlockSpec((3, d), lambda i: (0, 0))],
        out_shape=[jax.ShapeDtypeStruct((s, k), F32) for k in widths] + [jax.ShapeDtypeStruct((k, d), BF16) for k in widths]
        + [jax.ShapeDtypeStruct((3, s, d), BF16), jax.ShapeDtypeStruct((3, d), F32)],
        scratch_shapes=[pltpu.VMEM((k, d), F32) for k in widths],
        compiler_params=_params("arbitrary"),
    )(h, h, h, b_gate, att, yc, ys, wa, wc, ws, dm)
    return res[0:3], res[3:6], res[6], res[7]


def _ffn_act_fwd(hh, cw, name):
    s, dff = hh.shape[0], hh.shape[1] // 2
    nblk = dff // LANES

    def body(a_ref, b_ref, wa_ref, wb_ref, z_ref):
        rows = lax.broadcasted_iota(jnp.int32, (s, LANES), 0)
        z_ref[...] = (_gelu(_conv3(a_ref[...], wa_ref, rows)) * _conv3(b_ref[...], wb_ref, rows)).astype(BF16)

    col = lambda base: pl.BlockSpec((s, LANES), lambda j: (0, base + j))
    wsp = lambda base: pl.BlockSpec((3, LANES), lambda j: (0, base + j))
    return pl.pallas_call(
        body, name=name, grid=(nblk,), in_specs=[col(0), col(nblk), wsp(0), wsp(nblk)], out_specs=col(0),
        out_shape=jax.ShapeDtypeStruct((s, dff), BF16), compiler_params=_params("parallel"),
    )(hh, hh, cw, cw)


def _ffn_act_bwd(hh, cw, dz, name):
    s, dff = hh.shape[0], hh.shape[1] // 2
    nblk = dff // LANES

    def body(a_ref, b_ref, wa_ref, wb_ref, dz_ref, da_ref, db_ref, dwa_ref, dwb_ref):
        rows = lax.broadcasted_iota(jnp.int32, (s, LANES), 0)
        a, b, dz = a_ref[...], b_ref[...], dz_ref[...]
        ga, slope = _gelu_and_slope(_conv3(a, wa_ref, rows))
        da, dwa_ref[...] = _conv3_bwd(dz * _conv3(b, wb_ref, rows) * slope, a, wa_ref, rows)
        db, dwb_ref[...] = _conv3_bwd(dz * ga, b, wb_ref, rows)
        da_ref[...] = da.astype(BF16)
        db_ref[...] = db.astype(BF16)

    col = lambda base: pl.BlockSpec((s, LANES), lambda j: (0, base + j))
    wsp = lambda base: pl.BlockSpec((3, LANES), lambda j: (0, base + j))
    w8 = lambda base: pl.BlockSpec((8, LANES), lambda j: (0, base + j))
    return pl.pallas_call(
        body, name=name, grid=(nblk,), in_specs=[col(0), col(nblk), wsp(0), wsp(nblk), col(0)],
        out_specs=[col(0), col(0), w8(0), w8(0)],
        out_shape=[jax.ShapeDtypeStruct((s, dff), BF16)] * 2 + [jax.ShapeDtypeStruct((8, dff), F32)] * 2,
        compiler_params=_params("parallel"),
    )(hh, hh, cw, cw, dz)


ANY = pl.BlockSpec(memory_space=pl.ANY)


def _place():
    return lax.axis_index("x"), lax.axis_index("y"), lax.axis_index("c")


def _all_gather(arrs, name):
    n = len(arrs)

    def body(*refs):
        ins, outs = refs[:n], refs[n:2 * n]
        send_sems, recv_sems, local_sems = refs[2 * n:]
        x, y, c = _place()
        me, sibling = (x, y, c), (x, y, 1 - c)
        chips = [(1 - x, y), (x, 1 - y), (1 - x, 1 - y)]

        def slab(a, dev):
            return outs[a].at[4 * dev[0] + 2 * dev[1] + dev[2]]

        def copy(a, k, block, to, src=None):
            return pltpu.make_async_remote_copy(
                src_ref=slab(a, block) if src is None else src, dst_ref=slab(a, block),
                send_sem=send_sems.at[7 * a + k], recv_sem=recv_sems.at[7 * a + k], device_id=to, device_id_type=MESH)

        mine = [pltpu.make_async_copy(ins[a], slab(a, me), local_sems.at[a]) for a in range(n)]
        for cp in mine:
            cp.start()
        first = []
        for a in range(n):
            first.append(copy(a, 0, me, sibling, src=ins[a]))
            first += [copy(a, 1 + j, me, (*chip, c), src=ins[a]) for j, chip in enumerate(chips)]
        for cp in first:
            cp.start()
        passed = []
        for a in range(n):
            for j, chip in enumerate(chips):
                copy(a, 1 + j, (*chip, c), me).wait_recv()
                fwd = copy(a, 4 + j, (*chip, c), sibling)
                fwd.start()
                passed.append(fwd)
        for a in range(n):
            copy(a, 0, sibling, me).wait_recv()
            for j, chip in enumerate(chips):
                copy(a, 4 + j, (*chip, 1 - c), me).wait_recv()
        for cp in first + passed:
            cp.wait_send()
        for cp in mine:
            cp.wait()

    return pl.pallas_call(
        body, name=name, in_specs=[ANY] * n, out_specs=[ANY] * n,
        out_shape=[jax.ShapeDtypeStruct((N_DEV,) + a.shape, a.dtype) for a in arrs],
        scratch_shapes=[pltpu.SemaphoreType.DMA((7 * n,)), pltpu.SemaphoreType.DMA((7 * n,)), pltpu.SemaphoreType.DMA((n,))],
    )(*arrs)


HBM = pl.BlockSpec(memory_space=pltpu.HBM)
SEM = pl.BlockSpec(memory_space=pltpu.SEMAPHORE)
EFFECT = pltpu.SideEffectType.DATAFLOW_SIDE_EFFECTING


def _slot(dev):
    return 4 * dev[0] + 2 * dev[1] + dev[2]


def _exchange_copies(src_refs, land_refs, send_sems, recv_sems, src_view, land_view):
    x, y, c = _place()
    me = (x, y, c)
    peers = [(1 - x if r & 4 else x, 1 - y if r & 2 else y, 1 - c if r & 1 else c) for r in range(1, N_DEV)]
    sends, lands = [], []
    for a, (src, land) in enumerate(zip(src_refs, land_refs)):
        for k, peer in enumerate(peers):
            sems = dict(send_sem=send_sems.at[7 * a + k], recv_sem=recv_sems.at[7 * a + k], device_id=peer,
                        device_id_type=MESH)
            sends.append(pltpu.make_async_remote_copy(src_ref=src_view(src, _slot(peer)),
                                                      dst_ref=land_view(land, _slot(me)), **sems))
            lands.append(pltpu.make_async_remote_copy(src_ref=src_view(src, _slot(me)),
                                                      dst_ref=land_view(land, _slot(peer)), **sems))
    return sends, lands


def _own_copies(src_refs, land_refs, local_sems, src_view, land_view):
    me = _slot(_place())
    return [pltpu.make_async_copy(src_view(src, me), land_view(land, me), local_sems.at[a])
            for a, (src, land) in enumerate(zip(src_refs, land_refs))]


def _exchange_start(srcs, lands, after, src_view, land_view, name):
    n = len(srcs)

    def body(*refs):
        src_refs, land_refs = refs[:n], refs[n:2 * n]
        send_sems, recv_sems, local_sems = refs[2 * n + 1:2 * n + 4]
        token = refs[-1]
        sends, _ = _exchange_copies(src_refs, land_refs, send_sems, recv_sems, src_view, land_view)
        for cp in sends + _own_copies(src_refs, land_refs, local_sems, src_view, land_view):
            cp.start()
        token[...] = jnp.zeros_like(token)

    thru = [pltpu.HBM(a.shape, a.dtype) for a in list(srcs) + list(lands)]
    outs = pl.pallas_call(
        body, name=name,
        out_shape=(pltpu.SemaphoreType.DMA((7 * n,)), pltpu.SemaphoreType.DMA((7 * n,)), pltpu.SemaphoreType.DMA((n,)),
                   *thru, jax.ShapeDtypeStruct((8, LANES), F32)),
        in_specs=[HBM] * (2 * n) + [ANY],
        out_specs=(SEM, SEM, SEM, *([HBM] * (2 * n)), pl.BlockSpec(memory_space=pltpu.VMEM)),
        input_output_aliases={i: 3 + i for i in range(2 * n)},
        compiler_params=pltpu.CompilerParams(has_side_effects=EFFECT),
    )(*[pltpu.with_memory_space_constraint(a, pltpu.HBM) for a in list(srcs) + list(lands)], after)
    return outs[:3], list(outs[3:3 + n]), list(outs[3 + n:3 + 2 * n]), outs[-1]


def _exchange_wait(sems, srcs, lands, after, src_view, land_view, name):
    n = len(srcs)

    def body(*refs):
        src_refs, land_refs = refs[:n], refs[n:2 * n]
        send_sems, recv_sems, local_sems = refs[2 * n:2 * n + 3]
        sends, landed = _exchange_copies(src_refs, land_refs, send_sems, recv_sems, src_view, land_view)
        for cp in sends:
            cp.wait_send()
        for cp in landed:
            cp.wait_recv()
        for cp in _own_copies(src_refs, land_refs, local_sems, src_view, land_view):
            cp.wait()

    after = after if isinstance(after, (list, tuple)) else [after]
    outs = pl.pallas_call(
        body, name=name, out_shape=[pltpu.HBM(a.shape, a.dtype) for a in list(srcs) + list(lands)],
        in_specs=[HBM] * (2 * n) + [SEM, SEM, SEM] + [ANY] * len(after), out_specs=[HBM] * (2 * n),
        input_output_aliases={i: i for i in range(2 * n)},
        compiler_params=pltpu.CompilerParams(has_side_effects=EFFECT),
    )(*srcs, *lands, *sems, *after)
    return list(outs[:n]), list(outs[n:])


ADAMW_BLOCK_BYTES = 1 << 19
PACK_ROWS = 256


def _adamw(slabs, w, m, v, name):
    nl, r, c = w.shape
    row_edges = [r] + [t for t in range(8, r, 8) if r % t == 0]
    col_edges = [c] + [t for t in range(LANES, c, LANES) if c % t == 0]
    fits = [(tr * tc, tc, tr) for tr in row_edges for tc in col_edges if tr * tc * 4 <= ADAMW_BLOCK_BYTES]
    _, tc, tr = max(fits) if fits else (0, min(col_edges), min(row_edges))

    def body(s_ref, w_ref, m_ref, v_ref, g_ref, d_ref, nm_ref, nv_ref):
        g = s_ref[0].astype(F32)
        for q in range(1, N_DEV):
            g = g + s_ref[q].astype(F32)
        m_new = ADAM_B1 * m_ref[...] + (1.0 - ADAM_B1) * g
        v_new = ADAM_B2 * v_ref[...] + (1.0 - ADAM_B2) * (g * g)
        m_hat = m_new / (1.0 - ADAM_B1 ** ADAM_STEP)
        v_hat = v_new / (1.0 - ADAM_B2 ** ADAM_STEP)
        g_ref[...] = g
        d_ref[...] = -ADAM_LR * (m_hat / (jnp.sqrt(v_hat) + ADAM_EPS) + ADAM_WD * w_ref[...])
        nm_ref[...] = m_new
        nv_ref[...] = v_new

    blk = pl.BlockSpec((None, tr, tc), lambda l, i, j: (l, i, j))
    return pl.pallas_call(
        body, name=name, grid=(nl, r // tr, c // tc),
        in_specs=[pl.BlockSpec((N_DEV, None, tr, tc), lambda l, i, j: (0, l, i, j)), blk, blk, blk],
        out_specs=[blk] * 4, out_shape=[jax.ShapeDtypeStruct(w.shape, F32)] * 4,
        compiler_params=_params("parallel", "parallel", "parallel"),
    )(slabs, w, m, v)


def _layout(d):
    off = {"gate": 0, "q": 3 * d}
    off["bg"] = off["q"] + 3 * D_ATT
    off["u"] = off["bg"] + 3 * D_CONV
    off["f"] = off["u"] + 2 * D_SGU
    width = -(-(off["f"] + LANES) // 512) * 512
    return off, width


def _pad_w_in(wt, d, token):
    off, width = _layout(d)
    nqkv, nrest = 3 * D_ATT, 3 * D_CONV + 2 * D_SGU
    pad = jnp.zeros((width - off["f"] - N_HEADS, wt.shape[1]), wt.dtype) + token[0, 0].astype(wt.dtype)
    return jnp.concatenate([wt[nqkv + N_HEADS + nrest:], wt[:nqkv], wt[nqkv + N_HEADS:nqkv + N_HEADS + nrest],
                            wt[nqkv:nqkv + N_HEADS], pad], axis=0)


def _unpad_w_in(wtp, d):
    off, _ = _layout(d)
    return jnp.concatenate([wtp[off["q"]:off["bg"]], wtp[off["f"]:off["f"] + N_HEADS], wtp[off["bg"]:off["f"]],
                            wtp[:off["q"]]], axis=0)


def _cols_from_slabs(g):
    return jnp.transpose(g, (1, 0, 2)).reshape(g.shape[1], N_DEV * g.shape[2])


def _cols_to_slabs(w):
    r, c = w.shape[0], w.shape[1] // N_DEV
    return jnp.transpose(w.reshape(r, N_DEV, c), (1, 0, 2))


def kernel(x, pre_mix_g, post_mix_g, pre_ffn_g, post_ffn_g, w_in, b_forget, b_gate, conv_mix_w, sgu_ln_g, sgu_ln_b, sgu_w, sgu_b, w_branch_att, w_branch_conv, w_branch_sgu, w_out, w_ffn_up, conv_ffn_w, w_ffn_down, loss_target, m_pre_mix_g, m_post_mix_g, m_pre_ffn_g, m_post_ffn_g, m_w_in, m_b_forget, m_b_gate, m_conv_mix_w, m_sgu_ln_g, m_sgu_ln_b, m_sgu_w, m_sgu_b, m_w_branch_att, m_w_branch_conv, m_w_branch_sgu, m_w_out, m_w_ffn_up, m_conv_ffn_w, m_w_ffn_down, v_pre_mix_g, v_post_mix_g, v_pre_ffn_g, v_post_ffn_g, v_w_in, v_b_forget, v_b_gate, v_conv_mix_w, v_sgu_ln_g, v_sgu_ln_b, v_sgu_w, v_sgu_b, v_w_branch_att, v_w_branch_conv, v_w_branch_sgu, v_w_out, v_w_ffn_up, v_conv_ffn_w, v_w_ffn_down):
    depth = w_in.shape[0]
    s, d = x.shape[1], x.shape[2]
    dff = w_ffn_down.shape[1] * N_DEV
    off, _ = _layout(d)
    qblk, bgblk, ublk, fblk = off["q"] // LANES, off["bg"] // LANES, off["u"] // D_SGU, off["f"] // LANES
    x0 = x.reshape(s, d)
    target = loss_target.reshape(s, d)
    ncm, ncf = conv_mix_w.shape[2], conv_ffn_w.shape[2]

    lo = d // N_DEV

    whole = lambda ref, slot: ref
    slab = lambda ref, slot: ref.at[slot]

    w_in_t, w_up_t = jnp.transpose(w_in, (0, 2, 1)), jnp.transpose(w_ffn_up, (0, 2, 1))

    def shards_of(l, part):
        if part == "mix":
            small = jnp.concatenate([b_gate[l], conv_mix_w[l], conv_ffn_w[l]], axis=1)
            return [w_in_t[l].astype(BF16), w_branch_att[l].astype(BF16), w_branch_conv[l].astype(BF16),
                    w_branch_sgu[l].astype(BF16), w_out[l].astype(BF16), small]
        return [w_up_t[l].astype(BF16), w_ffn_down[l].astype(BF16)]

    def gather_start(l, part, after):
        shards = shards_of(l, part)
        lands = [lax.empty((N_DEV,) + a.shape, a.dtype) for a in shards]
        return _exchange_start(shards, lands, after, whole, slab, name=f"gather_start_{part}_{l}")

    def gather_finish(l, part, started, after):
        sems, shards, lands, _ = started[part]
        shards, lands = _exchange_wait(sems, shards, lands, after, whole, slab, name=f"gather_wait_{part}_{l}")
        token = jnp.zeros((8, LANES), F32)
        if l + 1 < depth:
            started[part] = gather_start(l + 1, part, lands[0])
            token = started[part][3]
        return lands, token

    def bfull(l):
        return jnp.repeat(jnp.transpose(sgu_b[l]), HEAD_DIM, axis=1)

    def bf_pad(l):
        return jnp.pad(b_forget[l], (0, LANES - N_HEADS)).reshape(1, LANES)

    saved = []
    weights = []
    xin = x0
    xn = _prenorm(x0, pre_mix_g[0:1], name="prenorm_first")
    loss_acc = dy = None
    first = _all_gather(shards_of(0, "mix"), name="gather_first")
    started = {"ffn": gather_start(0, "ffn", first[0])}
    if depth > 1:
        started["mix"] = gather_start(1, "mix", started["ffn"][3])
    for l in range(depth):
        if l == 0:
            (g_in, g_a, g_c, g_s, g_o, g_small), token = first, started["mix" if depth > 1 else "ffn"][3]
        else:
            (g_in, g_a, g_c, g_s, g_o, g_small), token = gather_finish(l, "mix", started, xin)
        g_small = _cols_from_slabs(g_small).reshape(3, N_DEV, -1)
        w = dict(w_in=_pad_w_in(g_in.reshape(N_DEV * g_in.shape[1], d), d, token), wa=_cols_from_slabs(g_a),
                 wc=_cols_from_slabs(g_c), ws=_cols_from_slabs(g_s), w_out=g_o.reshape(d, d),
                 b_gate=g_small[:, :, :lo].reshape(3, d), cmw=g_small[:, :, lo:lo + ncm].reshape(3, D_CONV),
                 cfw=g_small[:, :, lo + ncm:].reshape(3, 2 * dff))
        weights.append(w)
        h = _mm(xn, w["w_in"], "nt", F32, name="proj_in")
        c, ct = _forget_prep(h, bf_pad(l), fblk, name="forget_prep")
        att, lse = _attention_fwd(h, c, ct, qblk, name="attention_fwd")
        yc = _sconv_fwd(h, w["cmw"], bgblk, name="sconv_fwd")
        ys = _sgu_fwd(h, sgu_ln_g[l:l + 1], sgu_ln_b[l:l + 1], sgu_w[l], bfull(l), ublk, name="sgu_fwd")
        merged = _merge_fwd(h, w["b_gate"], att, yc, ys, w["wa"], w["wc"], w["ws"], name="merge_fwd")
        (g_up, g_dn), token = gather_finish(l, "ffn", started, merged)
        w["w_up"], w["w_dn"] = g_up.reshape(2 * dff, d), g_dn.reshape(dff, d)
        o, x1, xn2 = _mm_postnorm(merged, w["w_out"], xin, post_mix_g[l:l + 1], pre_ffn_g[l:l + 1] + token[0, 0],
                                  name="proj_out")
        hh = _mm(xn2, w["w_up"], "nt", F32, name="ffn_up")
        z = _ffn_act_fwd(hh, w["cfw"], name="ffn_act_fwd")
        layer = dict(xin=xin, xn=xn, h=h, c=c, ct=ct, lse=lse, att=att, yc=yc, ys=ys, merged=merged, o=o, x1=x1,
                     xn2=xn2, hh=hh, z=z)
        if l + 1 < depth:
            layer["f"], xin, xn = _mm_postnorm(z, w["w_dn"], x1, post_ffn_g[l:l + 1], pre_mix_g[l + 1:l + 2],
                                               name="ffn_down")
        else:
            layer["f"], dy, loss_acc = _mm_postnorm_loss(z, w["w_dn"], x1, post_ffn_g[l:l + 1], target,
                                                         name="ffn_down_loss")
        saved.append(layer)
    loss = lax.psum(loss_acc[0, 0] * (0.5 / d), ("x", "y", "c"))

    rep = {k: [None] * depth for k in ("pre_mix_g", "post_mix_g", "pre_ffn_g", "post_ffn_g", "b_forget", "sgu_ln_g",
                                       "sgu_ln_b", "sgu_w", "sgu_b")}
    nsmall = lo + ncm + ncf
    lands = {"win": [lax.empty((N_DEV, depth) + w_in_t.shape[1:], BF16)],
             "mid": [lax.empty((N_DEV, depth) + shp, dt) for shp, dt in (
                 (w_branch_att.shape[1:], BF16), (w_branch_conv.shape[1:], BF16), (w_branch_sgu.shape[1:], BF16),
                 (w_out.shape[1:], BF16), ((3, nsmall), F32))],
             "ffn": [lax.empty((N_DEV, depth) + shp, BF16) for shp in (w_up_t.shape[1:], w_ffn_down.shape[1:])]}
    scatters = {part: [None] * depth for part in lands}

    def scatter_start(l, part, sends, after):
        layer_slab = lambda ref, slot: ref.at[slot, l]
        sems, sends, lands[part], token = _exchange_start(sends, lands[part], after, slab, layer_slab,
                                                          name=f"scatter_start_{part}_{l}")
        scatters[part][l] = (sems, sends, layer_slab)
        return token

    def scatter_finish(part, after):
        for l in range(depth):
            sems, sends, layer_slab = scatters[part][l]
            _, lands[part] = _exchange_wait(sems, sends, lands[part], after, slab, layer_slab,
                                            name=f"scatter_wait_{part}_{l}")
        return lands[part]

    token = jnp.zeros((8, LANES), F32)
    dx = dy
    for l in reversed(range(depth)):
        w, a = weights[l], saved[l]
        df, rep["post_ffn_g"][l] = _postnorm_bwd(a["f"], post_ffn_g[l:l + 1] + token[0, 0], dx, name="postnorm_bwd")
        dz = _mm(df, w["w_dn"], "nt", F32, name="ffn_down_dx")
        g_dn = _mm(a["z"], df, "tn", BF16, name="ffn_down_dw")
        dha, dhb, dcwa, dcwb = _ffn_act_bwd(a["hh"], w["cfw"], dz, name="ffn_act_bwd")
        dhh = jnp.concatenate([dha, dhb], axis=1)
        dcfw = jnp.concatenate([dcwa[0:3], dcwb[0:3]], axis=1)
        g_up = _mm(dhh, a["xn2"], "tn", BF16, name="ffn_up_dw")
        token = scatter_start(l, "ffn", [g_up.reshape(N_DEV, 2 * dff // N_DEV, d), g_dn.reshape(N_DEV, dff // N_DEV, d)],
                              dz)
        dx1, rep["pre_ffn_g"][l] = _mm_prenorm_bwd(dhh, w["w_up"], a["x1"], pre_ffn_g[l:l + 1], dx, token, name="ffn_up_dx")
        do, rep["post_mix_g"][l], dmerged = _postnorm_bwd_mm(a["o"], post_mix_g[l:l + 1], dx1, w["w_out"], name="proj_out_dx")
        g_o = _mm(a["merged"], do, "tn", BF16, name="proj_out_dw")
        (datt, dconv, dsgu), (g_a, g_c, g_s), dgl, dbg = _merge_bwd(
            a["h"], w["b_gate"], a["att"], a["yc"], a["ys"], w["wa"], w["wc"], w["ws"], dmerged, name="merge_bwd")
        dbgate, dcg, dhc, dcmw = _sconv_bwd(a["h"], w["cmw"], dconv, bgblk, name="sconv_bwd")
        sends = [_cols_to_slabs(g_a), _cols_to_slabs(g_c), _cols_to_slabs(g_s), g_o.reshape(N_DEV, d // N_DEV, d),
                 jnp.concatenate([_cols_to_slabs(dbg), _cols_to_slabs(dcmw[0:3]), _cols_to_slabs(dcfw)], axis=2)]
        token = scatter_start(l, "mid", sends, dx1)
        dq, dk, dv, dct4, dcq4 = _attention_bwd(a["h"], a["c"], a["ct"], a["lse"], a["att"], datt, token, qblk,
                                                name="attention_bwd")
        dfl, dbf = _forget_prep_bwd(a["h"], bf_pad(l), dct4, dcq4, fblk, name="forget_prep_bwd")
        rep["b_forget"][l] = dbf[0, :N_HEADS]
        du, dvs, dlg, dlb, dsw, dbfull = _sgu_bwd(a["h"], sgu_ln_g[l:l + 1], sgu_ln_b[l:l + 1], sgu_w[l], bfull(l), dsgu,
                                                  ublk, name="sgu_bwd")
        rep["sgu_ln_g"][l], rep["sgu_ln_b"][l], rep["sgu_w"][l] = dlg, dlb, dsw
        rep["sgu_b"][l] = jnp.transpose(jnp.sum(dbfull.reshape(SGU_CHUNK, N_SGU_GROUPS, HEAD_DIM), axis=2))
        dh = jnp.concatenate([dgl[0], dgl[1], dgl[2], dq.astype(BF16), dk.astype(BF16), dv.astype(BF16), dbgate, dcg, dhc,
                              du, dvs, dfl.astype(BF16), jnp.zeros((s, w["w_in"].shape[0] - off["f"] - LANES), BF16)], axis=1)
        g_in = _mm(dh, a["xn"], "tn", BF16, name="proj_in_dw")
        token = scatter_start(l, "win", [_unpad_w_in(g_in, d).reshape(N_DEV, -1, d)], dx1)
        dx, rep["pre_mix_g"][l] = _mm_prenorm_bwd(dh, w["w_in"], a["xin"], pre_mix_g[l:l + 1], dx1, token, name="proj_in_dx")

    outs = {}
    t3 = lambda arr: jnp.transpose(arr, (0, 2, 1))

    def update(name_, slabs, w_, m_, v_, transposed=False):
        if transposed:
            w_, m_, v_ = t3(w_), t3(m_), t3(v_)
        shp = w_.shape
        w3 = w_.reshape((shp[0], -1, shp[-1])) if w_.ndim >= 3 else w_.reshape((1,) + shp)
        res = _adamw(slabs.reshape((N_DEV,) + w3.shape), w3, m_.reshape(w3.shape), v_.reshape(w3.shape),
                     name="adamw_" + name_)
        outs[name_] = tuple(t3(t.reshape(shp)) if transposed else t.reshape(shp) for t in res)
        return res[0]

    rep_names = ("pre_mix_g", "post_mix_g", "pre_ffn_g", "post_ffn_g", "b_forget", "sgu_ln_g", "sgu_ln_b", "sgu_w", "sgu_b")
    rep_w = dict(pre_mix_g=(pre_mix_g, m_pre_mix_g, v_pre_mix_g), post_mix_g=(post_mix_g, m_post_mix_g, v_post_mix_g),
                 pre_ffn_g=(pre_ffn_g, m_pre_ffn_g, v_pre_ffn_g), post_ffn_g=(post_ffn_g, m_post_ffn_g, v_post_ffn_g),
                 b_forget=(b_forget, m_b_forget, v_b_forget), sgu_ln_g=(sgu_ln_g, m_sgu_ln_g, v_sgu_ln_g),
                 sgu_ln_b=(sgu_ln_b, m_sgu_ln_b, v_sgu_ln_b), sgu_w=(sgu_w, m_sgu_w, v_sgu_w), sgu_b=(sgu_b, m_sgu_b, v_sgu_b))

    def pack(parts):
        rows = [jnp.pad(p.reshape(-1), (0, -p.size % LANES)).reshape(-1, LANES) for p in parts]
        rows = jnp.concatenate(rows, axis=0)
        return jnp.pad(rows, ((0, -rows.shape[0] % PACK_ROWS), (0, 0)))

    part = pack([jnp.stack([g.reshape(rep_w[k][0].shape[1:]) for g in rep[k]]) for k in rep_names])
    small_sems, small_src, small_land, _ = _exchange_start([part], [lax.empty((N_DEV,) + part.shape, F32)], dx, whole, slab,
                                                            name="gather_small_start")

    got_up, got_dn = scatter_finish("ffn", dx)
    done = [update("w_ffn_up", got_up, w_ffn_up, m_w_ffn_up, v_w_ffn_up, transposed=True),
            update("w_ffn_down", got_dn, w_ffn_down, m_w_ffn_down, v_w_ffn_down)]

    _, (gathered,) = _exchange_wait(small_sems, small_src, small_land, done, whole, slab, name="gather_small_wait")
    packed = [pack([rep_w[k][i] for k in rep_names]) for i in range(3)]
    res = _adamw(gathered.reshape(N_DEV, 1, -1, LANES), *[p.reshape(1, -1, LANES) for p in packed], name="adamw_replicated")
    row = 0
    for k in rep_names:
        shp = rep_w[k][0].shape
        size = math.prod(shp)
        nrows = -(-size // LANES)
        outs[k] = tuple(t[0, row:row + nrows].reshape(-1)[:size].reshape(shp) for t in res)
        row += nrows

    got_a, got_c, got_s, got_o, small = scatter_finish("mid", res[0])
    done = [update("w_branch_att", got_a, w_branch_att, m_w_branch_att, v_w_branch_att),
            update("w_branch_conv", got_c, w_branch_conv, m_w_branch_conv, v_w_branch_conv),
            update("w_branch_sgu", got_s, w_branch_sgu, m_w_branch_sgu, v_w_branch_sgu),
            update("w_out", got_o, w_out, m_w_out, v_w_out),
            update("b_gate", small[..., :lo], b_gate, m_b_gate, v_b_gate),
            update("conv_mix_w", small[..., lo:lo + ncm], conv_mix_w, m_conv_mix_w, v_conv_mix_w),
            update("conv_ffn_w", small[..., lo + ncm:], conv_ffn_w, m_conv_ffn_w, v_conv_ffn_w)]
    (got_in,) = scatter_finish("win", done)
    update("w_in", got_in, w_in, m_w_in, v_w_in, transposed=True)

    order = ("pre_mix_g", "post_mix_g", "pre_ffn_g", "post_ffn_g", "w_in", "b_forget", "b_gate", "conv_mix_w", "sgu_ln_g",
             "sgu_ln_b", "sgu_w", "sgu_b", "w_branch_att", "w_branch_conv", "w_branch_sgu", "w_out", "w_ffn_up",
             "conv_ffn_w", "w_ffn_down")
    grad_x = dx.reshape(x.shape)
    return (loss, grad_x, *[outs[k][0] for k in order], *[outs[k][1] for k in order], *[outs[k][2] for k in order],
            *[outs[k][3] for k in order])
```

```python
import math

import jax
import jax.numpy as jnp
from jax import lax
from jax.experimental import pallas as pl
from jax.experimental.pallas import tpu as pltpu

F32 = jnp.float32
BF16 = jnp.bfloat16

N_DEV = 8
HEAD_DIM = 64
N_HEADS = 8
D_ATT = 512
D_CONV = 256
D_SGU = 256
N_SGU_GROUPS = 4
SGU_CHUNK = 128
RMS_EPS = 1e-6
LN_EPS = 1e-5
ADAM_LR = 0.001
ADAM_B1 = 0.9
ADAM_B2 = 0.999
ADAM_EPS = 1e-08
ADAM_WD = 0.01
ADAM_STEP = 10
LANES = 128
VMEM_LIMIT = 56 * 1024 * 1024
ATT_TILE = 512
LOG2E = math.log2(math.e)
NEG = -1e30
MESH = pl.DeviceIdType.MESH


def _params(*sem):
    return pltpu.CompilerParams(dimension_semantics=sem if sem else None, vmem_limit_bytes=VMEM_LIMIT)


def _tile(n, cap):
    if n <= cap:
        return n
    t = cap - cap % LANES
    while n % t:
        t -= LANES
    return t


def _gelu(x):
    return 0.5 * x * (1.0 + jnp.tanh(math.sqrt(2.0 / math.pi) * (x + 0.044715 * (x * x * x))))


def _gelu_and_slope(x):
    k0, k1 = math.sqrt(2.0 / math.pi), 0.044715
    x2 = x * x
    t = jnp.tanh(x * (k0 + (k0 * k1) * x2))
    half = 0.5 * (1.0 + t)
    return x * half, half + (0.5 * x) * (1.0 - t * t) * (k0 + (3.0 * k0 * k1) * x2)


def _rms(x, g):
    r = lax.rsqrt(jnp.mean(x * x, axis=-1, keepdims=True) + RMS_EPS)
    return x * r * g


def _layer_norm(x, g, b):
    mu = jnp.mean(x, axis=-1, keepdims=True)
    xc = x - mu
    var = jnp.mean(xc * xc, axis=-1, keepdims=True)
    return xc * lax.rsqrt(var + LN_EPS) * g + b


def _shift_down(x, k, rows):
    return jnp.where(rows >= k, pltpu.roll(x, k, 0), 0.0)


def _shift_up(x, k, rows):
    s = x.shape[0]
    return jnp.where(rows < s - k, pltpu.roll(x, s - k, 0), 0.0)


def _conv3(x, w_ref, rows):
    return w_ref[2:3, :] * x + w_ref[1:2, :] * _shift_down(x, 1, rows) + w_ref[0:1, :] * _shift_down(x, 2, rows)


def _conv3_bwd(dy, x, w_ref, rows):
    up1, up2 = _shift_up(dy, 1, rows), _shift_up(dy, 2, rows)
    dx = w_ref[2:3, :] * dy + w_ref[1:2, :] * up1 + w_ref[0:1, :] * up2
    d2 = jnp.sum(dy * x, axis=0, keepdims=True)
    d1 = jnp.sum(up1 * x, axis=0, keepdims=True)
    d0 = jnp.sum(up2 * x, axis=0, keepdims=True)
    sub = lax.broadcasted_iota(jnp.int32, (8, x.shape[1]), 0)
    return dx, jnp.where(sub == 0, d0, jnp.where(sub == 1, d1, jnp.where(sub == 2, d2, 0.0)))


MM_VMEM_BUDGET = 40 * 1024 * 1024
MM_TILE_CAP = 1408


def _mm_tiles(m, n, k, out_bytes):
    def edges(d):
        return [t for t in range(LANES, min(d, MM_TILE_CAP) + 1, LANES) if d % t == 0] or [d]

    best = None
    for tm in edges(m):
        for tn in edges(n):
            if 2 * (2 * k * (tm + tn) + tm * tn * out_bytes) > MM_VMEM_BUDGET:
                continue
            for a_outer in (True, False):
                reads = k * m + (m // tm) * k * n if a_outer else k * n + (n // tn) * k * m
                traffic = 2 * reads + m * n * out_bytes
                key = (traffic, -tm * tn)
                if best is None or key < best[0]:
                    best = (key, (tm, tn, a_outer))
    return best[1]


def _mm(a, b, form, out_dtype, name, after=None):
    if form == "nn":
        (m, k), n = a.shape, b.shape[1]
    elif form == "nt":
        (m, k), n = a.shape, b.shape[0]
    else:
        (k, m), n = a.shape, b.shape[1]
    tm, tn, a_outer = _mm_tiles(m, n, k, jnp.dtype(out_dtype).itemsize)
    dims = {"nn": (((1,), (0,)), ((), ())), "nt": (((1,), (1,)), ((), ())), "tn": (((0,), (0,)), ((), ()))}[form]

    def body(a_ref, b_ref, *rest):
        o_ref = rest[-1]
        o_ref[...] = lax.dot_general(a_ref[...], b_ref[...], dims, preferred_element_type=F32).astype(o_ref.dtype)

    ij = (lambda g0, g1: (g0, g1)) if a_outer else (lambda g0, g1: (g1, g0))
    a_spec = (pl.BlockSpec((k, tm), lambda g0, g1: (0, ij(g0, g1)[0])) if form == "tn"
              else pl.BlockSpec((tm, k), lambda g0, g1: (ij(g0, g1)[0], 0)))
    b_spec = (pl.BlockSpec((tn, k), lambda g0, g1: (ij(g0, g1)[1], 0)) if form == "nt"
              else pl.BlockSpec((k, tn), lambda g0, g1: (0, ij(g0, g1)[1])))
    extra = [] if after is None else [pl.BlockSpec((8, LANES), lambda g0, g1: (0, 0))]
    return pl.pallas_call(
        body, name=name, grid=(m // tm, n // tn) if a_outer else (n // tn, m // tm),
        in_specs=[a_spec, b_spec] + extra, out_specs=pl.BlockSpec((tm, tn), lambda g0, g1: ij(g0, g1)),
        out_shape=jax.ShapeDtypeStruct((m, n), out_dtype),
        compiler_params=_params("parallel", "arbitrary"),
    )(a, b, *([] if after is None else [after]))


def _postnorm_bwd_mm(o, g, dx, b, name):
    s, d = o.shape
    n = b.shape[0]
    tm, tn = _tile(s, 512), _tile(n, MM_TILE_CAP)

    def body(o_ref, g_ref, dx_ref, b_ref, do_ref, dg_ref, out_ref):
        i, j = pl.program_id(0), pl.program_id(1)

        @pl.when(j == 0)
        def _():
            _, vjp = jax.vjp(_rms, o_ref[...], g_ref[...])
            d_o, dg = vjp(dx_ref[...])
            do_ref[...] = d_o.astype(BF16)

            @pl.when(i == 0)
            def _():
                dg_ref[...] = jnp.zeros_like(dg_ref)

            dg_ref[...] += dg

        out_ref[...] = lax.dot_general(do_ref[...], b_ref[...], (((1,), (1,)), ((), ())), preferred_element_type=F32)

    row = pl.BlockSpec((tm, d), lambda i, j: (i, 0))
    vec = pl.BlockSpec((1, d), lambda i, j: (0, 0))
    return pl.pallas_call(
        body, name=name, grid=(s // tm, n // tn),
        in_specs=[row, vec, row, pl.BlockSpec((tn, d), lambda i, j: (j, 0))],
        out_specs=[row, vec, pl.BlockSpec((tm, tn), lambda i, j: (i, j))],
        out_shape=[jax.ShapeDtypeStruct((s, d), BF16), jax.ShapeDtypeStruct((1, d), F32), jax.ShapeDtypeStruct((s, n), F32)],
        compiler_params=_params("arbitrary", "arbitrary"),
    )(o, g, dx, b)


def _mm_prenorm_bwd(a, b, x, g, dres, after, name):
    s, k = a.shape
    d = b.shape[1]
    tm = _tile(s, 512)

    def body(a_ref, b_ref, x_ref, g_ref, dres_ref, after_ref, dx_ref, dg_ref):
        dxn = jnp.dot(a_ref[...], b_ref[...], preferred_element_type=F32)
        _, vjp = jax.vjp(_rms, x_ref[...], g_ref[...])
        dx, dg = vjp(dxn)
        dx_ref[...] = dres_ref[...] + dx

        @pl.when(pl.program_id(0) == 0)
        def _():
            dg_ref[...] = jnp.zeros_like(dg_ref)

        dg_ref[...] += dg

    row = pl.BlockSpec((tm, d), lambda i: (i, 0))
    vec = pl.BlockSpec((1, d), lambda i: (0, 0))
    return pl.pallas_call(
        body, name=name, grid=(s // tm,),
        in_specs=[pl.BlockSpec((tm, k), lambda i: (i, 0)),
                  pl.BlockSpec((k, d), lambda i: (0, 0), pipeline_mode=pl.Buffered(1)),
                  row, vec, row, pl.BlockSpec((8, LANES), lambda i: (0, 0))],
        out_specs=[row, vec], out_shape=[jax.ShapeDtypeStruct((s, d), F32), jax.ShapeDtypeStruct((1, d), F32)],
        compiler_params=_params("arbitrary"),
    )(a, b, x, g, dres, after)


def _prenorm(x, g, name):
    s, d = x.shape
    tm = _tile(s, 512)

    def body(x_ref, g_ref, o_ref):
        o_ref[...] = _rms(x_ref[...], g_ref[...]).astype(BF16)

    return pl.pallas_call(
        body, name=name, grid=(s // tm,),
        in_specs=[pl.BlockSpec((tm, d), lambda i: (i, 0)), pl.BlockSpec((1, d), lambda i: (0, 0))],
        out_specs=pl.BlockSpec((tm, d), lambda i: (i, 0)),
        out_shape=jax.ShapeDtypeStruct((s, d), BF16), compiler_params=_params("parallel"),
    )(x, g)


def _mm_postnorm(a, b, x, g_post, g_next, name):
    s, k = a.shape
    d = b.shape[1]
    tm = _tile(s, 512)

    def body(a_ref, b_ref, x_ref, gp_ref, gn_ref, o_ref, x1_ref, xn_ref):
        o = jnp.dot(a_ref[...], b_ref[...], preferred_element_type=F32)
        o_ref[...] = o
        x1 = x_ref[...] + _rms(o, gp_ref[...])
        x1_ref[...] = x1
        xn_ref[...] = _rms(x1, gn_ref[...]).astype(BF16)

    row = pl.BlockSpec((tm, d), lambda i: (i, 0))
    vec = pl.BlockSpec((1, d), lambda i: (0, 0))
    return pl.pallas_call(
        body, name=name, grid=(s // tm,),
        in_specs=[pl.BlockSpec((tm, k), lambda i: (i, 0)),
                  pl.BlockSpec((k, d), lambda i: (0, 0), pipeline_mode=pl.Buffered(1)), row, vec, vec],
        out_specs=[row, row, row],
        out_shape=[jax.ShapeDtypeStruct((s, d), F32), jax.ShapeDtypeStruct((s, d), F32), jax.ShapeDtypeStruct((s, d), BF16)],
        compiler_params=_params("parallel"),
    )(a, b, x, g_post, g_next)


def _mm_postnorm_loss(a, b, x, g_post, target, name):
    s, k = a.shape
    d = b.shape[1]
    tm = _tile(s, 512)

    def body(a_ref, b_ref, x_ref, gp_ref, t_ref, o_ref, dy_ref, acc_ref):
        o = jnp.dot(a_ref[...], b_ref[...], preferred_element_type=F32)
        o_ref[...] = o
        e = x_ref[...] + _rms(o, gp_ref[...]) - t_ref[...]
        dy_ref[...] = e / d

        @pl.when(pl.program_id(0) == 0)
        def _():
            acc_ref[...] = jnp.zeros_like(acc_ref)

        acc_ref[...] += jnp.sum(jnp.sum(e * e, axis=1, keepdims=True), axis=0, keepdims=True)

    row = pl.BlockSpec((tm, d), lambda i: (i, 0))
    return pl.pallas_call(
        body, name=name, grid=(s // tm,),
        in_specs=[pl.BlockSpec((tm, k), lambda i: (i, 0)),
                  pl.BlockSpec((k, d), lambda i: (0, 0), pipeline_mode=pl.Buffered(1)), row,
                  pl.BlockSpec((1, d), lambda i: (0, 0)), row],
        out_specs=[row, row, pl.BlockSpec((1, LANES), lambda i: (0, 0))],
        out_shape=[jax.ShapeDtypeStruct((s, d), F32), jax.ShapeDtypeStruct((s, d), F32), jax.ShapeDtypeStruct((1, LANES), F32)],
        compiler_params=_params("arbitrary"),
    )(a, b, x, g_post, target)


def _postnorm_bwd(o, g, dx, name):
    s, d = o.shape
    tm = _tile(s, 512)

    def body(o_ref, g_ref, dx_ref, do_ref, dg_ref):
        _, vjp = jax.vjp(_rms, o_ref[...], g_ref[...])
        d_o, dg = vjp(dx_ref[...])
        do_ref[...] = d_o.astype(BF16)

        @pl.when(pl.program_id(0) == 0)
        def _():
            dg_ref[...] = jnp.zeros_like(dg_ref)

        dg_ref[...] += dg

    row = pl.BlockSpec((tm, d), lambda i: (i, 0))
    vec = pl.BlockSpec((1, d), lambda i: (0, 0))
    return pl.pallas_call(
        body, name=name, grid=(s // tm,), in_specs=[row, vec, row], out_specs=[row, vec],
        out_shape=[jax.ShapeDtypeStruct((s, d), BF16), jax.ShapeDtypeStruct((1, d), F32)],
        compiler_params=_params("arbitrary"),
    )(o, g, dx)


def _log_sigmoid(z):
    return jnp.minimum(z, 0.0) - jnp.log(1.0 + jnp.exp(-jnp.abs(z)))


def _forget_prep(h, bf_pad, fblk, name):
    s = h.shape[0]

    def body(f_ref, b_ref, c_ref, ct_ref):
        c = _log_sigmoid(f_ref[...] + b_ref[...])
        rows = lax.broadcasted_iota(jnp.int32, c.shape, 0)
        k = 1
        while k < s:
            c = c + _shift_down(c, k, rows)
            k *= 2
        c_ref[...] = c
        ct_ref[...] = jnp.transpose(c)[0:8, :]

    return pl.pallas_call(
        body, name=name, grid=(1,),
        in_specs=[pl.BlockSpec((s, LANES), lambda i: (0, fblk)), pl.BlockSpec((1, LANES), lambda i: (0, 0))],
        out_specs=[pl.BlockSpec((s, LANES), lambda i: (0, 0)), pl.BlockSpec((8, s), lambda i: (0, 0))],
        out_shape=[jax.ShapeDtypeStruct((s, LANES), F32), jax.ShapeDtypeStruct((8, s), F32)],
        compiler_params=_params("arbitrary"),
    )(h, bf_pad)


def _forget_prep_bwd(h, bf_pad, dct, dcq, fblk, name):
    s = h.shape[0]
    pairs = N_HEADS // 2

    def body(f_ref, b_ref, dct_ref, dcq_ref, df_ref, db_ref):
        dct = dct_ref[0]
        dcq = dcq_ref[0]
        for p in range(1, pairs):
            dct = dct + pltpu.roll(dct_ref[p], 2 * p, 0)
            dcq = dcq + pltpu.roll(dcq_ref[p], 2 * p, 1)
        dc = dcq + jnp.transpose(jnp.concatenate([dct, jnp.zeros((LANES - 8, s), F32)], axis=0))
        rows = lax.broadcasted_iota(jnp.int32, dc.shape, 0)
        k = 1
        while k < s:
            dc = dc + _shift_up(dc, k, rows)
            k *= 2
        z = f_ref[...] + b_ref[...]
        lane = lax.broadcasted_iota(jnp.int32, dc.shape, 1)
        df = jnp.where(lane < N_HEADS, dc * jax.nn.sigmoid(-z), 0.0)
        df_ref[...] = df
        db_ref[...] = jnp.sum(df, axis=0, keepdims=True)

    return pl.pallas_call(
        body, name=name, grid=(1,),
        in_specs=[pl.BlockSpec((s, LANES), lambda i: (0, fblk)), pl.BlockSpec((1, LANES), lambda i: (0, 0)),
                  pl.BlockSpec((pairs, 8, s), lambda i: (0, 0, 0)), pl.BlockSpec((pairs, s, LANES), lambda i: (0, 0, 0))],
        out_specs=[pl.BlockSpec((s, LANES), lambda i: (0, 0)), pl.BlockSpec((1, LANES), lambda i: (0, 0))],
        out_shape=[jax.ShapeDtypeStruct((s, LANES), F32), jax.ShapeDtypeStruct((1, LANES), F32)],
        compiler_params=_params("arbitrary"),
    )(h, bf_pad, dct, dcq)


def _pick_lane(blk, idx):
    lane = lax.broadcasted_iota(jnp.int32, blk.shape, 1)
    return jnp.sum(jnp.where(lane == idx, blk, 0.0), axis=1, keepdims=True)


def _pick_row(blk, idx):
    sub = lax.broadcasted_iota(jnp.int32, blk.shape, 0)
    return jnp.sum(jnp.where(sub == idx, blk, 0.0), axis=0, keepdims=True)


def _attention_fwd(h, c, ct, qblk, name):
    s = h.shape[0]
    t = _tile(s, ATT_TILE)
    nq = s // t
    scale = HEAD_DIM ** -0.5
    nt_dims = (((1,), (1,)), ((), ()))

    def body(q_ref, k_ref, v_ref, c_ref, ct_ref, o_ref, lse_ref):
        p = pl.program_id(0)
        i = pl.program_id(1)
        lane = lax.broadcasted_iota(jnp.int32, (1, LANES), 1)
        first = lane < HEAD_DIM
        q = q_ref[...] * (scale * LOG2E)
        qa = jnp.where(first, q, 0.0).astype(BF16)
        qb = jnp.where(first, 0.0, q).astype(BF16)
        cblk = c_ref[...]
        cta = _pick_lane(cblk, 2 * p) * LOG2E
        ctb = _pick_lane(cblk, 2 * p + 1) * LOG2E

        def step(j, carry, diagonal):
            ma, la, mb, lb, acc = carry
            off = pl.multiple_of(j * t, t)
            k = k_ref[pl.ds(off, t), :].astype(BF16)
            v = v_ref[pl.ds(off, t), :].astype(BF16)
            crow = ct_ref[:, pl.ds(off, t)] * LOG2E

            def one(qh, cth, hd, m_old, l_old):
                sc = lax.dot_general(qh, k, nt_dims, preferred_element_type=F32) - _pick_row(crow, hd)
                if diagonal:
                    keep = lax.broadcasted_iota(jnp.int32, (t, t), 0) >= lax.broadcasted_iota(jnp.int32, (t, t), 1)
                    sc = jnp.where(keep, sc, NEG)
                m_new = jnp.maximum(m_old, jnp.max(sc, axis=1, keepdims=True) + cth)
                pr = jnp.exp2(sc - (m_new - cth))
                alpha = jnp.exp2(m_old - m_new)
                l_new = alpha * l_old + jnp.sum(pr, axis=1, keepdims=True)
                pv = jnp.dot(pr.astype(BF16), v, preferred_element_type=F32)
                return m_new, l_new, alpha, pv

            ma2, la2, aa, pva = one(qa, cta, 2 * p, ma, la)
            mb2, lb2, ab, pvb = one(qb, ctb, 2 * p + 1, mb, lb)
            acc = jnp.where(first, aa * acc + pva, ab * acc + pvb)
            return ma2, la2, mb2, lb2, acc

        init = (jnp.full((t, 1), NEG, F32), jnp.zeros((t, 1), F32), jnp.full((t, 1), NEG, F32),
                jnp.zeros((t, 1), F32), jnp.zeros((t, LANES), F32))
        carry = lax.fori_loop(0, i, lambda j, carry: step(j, carry, False), init)
        ma, la, mb, lb, acc = step(i, carry, True)
        o_ref[...] = (acc / jnp.where(first, la, lb)).astype(BF16)
        lse_ref[0] = jnp.broadcast_to(ma + jnp.log2(la), (t, LANES))
        lse_ref[1] = jnp.broadcast_to(mb + jnp.log2(lb), (t, LANES))

    return pl.pallas_call(
        body, name=name, grid=(N_HEADS // 2, nq),
        in_specs=[pl.BlockSpec((t, LANES), lambda p, i: (i, qblk + p)),
                  pl.BlockSpec((s, LANES), lambda p, i: (0, qblk + 4 + p)),
                  pl.BlockSpec((s, LANES), lambda p, i: (0, qblk + 8 + p)),
                  pl.BlockSpec((t, LANES), lambda p, i: (i, 0)),
                  pl.BlockSpec((8, s), lambda p, i: (0, 0))],
        out_specs=[pl.BlockSpec((t, LANES), lambda p, i: (i, p)),
                   pl.BlockSpec((2, t, LANES), lambda p, i: (p, i, 0))],
        out_shape=[jax.ShapeDtypeStruct((s, D_ATT), BF16), jax.ShapeDtypeStruct((N_HEADS, s, LANES), F32)],
        compiler_params=_params("parallel", "arbitrary"),
    )(h, h, h, c, ct)


def _attention_bwd(h, c, ct, lse, att, datt, after, qblk, name):
    s = h.shape[0]
    t = _tile(s, ATT_TILE)
    nq = s // t
    scale = HEAD_DIM ** -0.5
    nt_dims = (((1,), (1,)), ((), ()))
    tn_dims = (((0,), (0,)), ((), ()))

    def body(q_ref, k_ref, v_ref, c_ref, ct_ref, lse_ref, o_ref, do_ref, after_ref, dq_ref, dk_ref, dv_ref, dct_ref, dcq_ref):
        p = pl.program_id(0)
        j = pl.program_id(1)
        lane = lax.broadcasted_iota(jnp.int32, (1, LANES), 1)
        first = lane < HEAD_DIM
        kf = k_ref[...]
        vf = v_ref[...]
        k = kf.astype(BF16)
        ka = jnp.where(first, kf, 0.0).astype(BF16)
        kb = jnp.where(first, 0.0, kf).astype(BF16)
        va = jnp.where(first, vf, 0.0).astype(BF16)
        vb = jnp.where(first, 0.0, vf).astype(BF16)
        crow = ct_ref[...] * LOG2E
        csa = _pick_row(crow, 2 * p)
        csb = _pick_row(crow, 2 * p + 1)

        @pl.when(j == 0)
        def _():
            dq_ref[...] = jnp.zeros_like(dq_ref)
            dcq_ref[...] = jnp.zeros_like(dcq_ref)

        def step(i, carry, diagonal):
            dka, dkb, dva, dvb, dca, dcb = carry
            off = pl.multiple_of(i * t, t)
            rows = pl.ds(off, t)
            q = (q_ref[rows, :] * (scale * LOG2E)).astype(BF16)
            dof = do_ref[rows, :]
            do = dof.astype(BF16)
            prod = dof * o_ref[rows, :].astype(F32)
            cblk = c_ref[rows, :] * LOG2E

            def one(kh, vh, hd, csh, lse_h):
                sc = lax.dot_general(q, kh, nt_dims, preferred_element_type=F32) - csh
                if diagonal:
                    keep = lax.broadcasted_iota(jnp.int32, (t, t), 0) >= lax.broadcasted_iota(jnp.int32, (t, t), 1)
                    sc = jnp.where(keep, sc, NEG)
                pr = jnp.exp2(sc - (jnp.max(lse_h, axis=1, keepdims=True) - _pick_lane(cblk, hd)))
                dp = lax.dot_general(do, vh, nt_dims, preferred_element_type=F32)
                return pr, dp

            pra, dpa = one(ka, va, 2 * p, csa, lse_ref[0, rows, :])
            prb, dpb = one(kb, vb, 2 * p + 1, csb, lse_ref[1, rows, :])
            dela = jnp.sum(jnp.where(first, prod, 0.0), axis=1, keepdims=True)
            delb = jnp.sum(jnp.where(first, 0.0, prod), axis=1, keepdims=True)
            dsa = pra * (dpa - dela)
            dsb = prb * (dpb - delb)
            dsa16 = dsa.astype(BF16)
            dsb16 = dsb.astype(BF16)
            dva = dva + lax.dot_general(pra.astype(BF16), do, tn_dims, preferred_element_type=F32)
            dvb = dvb + lax.dot_general(prb.astype(BF16), do, tn_dims, preferred_element_type=F32)
            dka = dka + lax.dot_general(dsa16, q, tn_dims, preferred_element_type=F32)
            dkb = dkb + lax.dot_general(dsb16, q, tn_dims, preferred_element_type=F32)
            dqa = jnp.dot(dsa16, k, preferred_element_type=F32)
            dqb = jnp.dot(dsb16, k, preferred_element_type=F32)
            dq_ref[rows, :] += scale * jnp.where(first, dqa, dqb)
            dca = dca - jnp.sum(dsa, axis=0, keepdims=True)
            dcb = dcb - jnp.sum(dsb, axis=0, keepdims=True)
            dcq_ref[rows, :] += jnp.where(lane == 0, jnp.sum(dsa, axis=1, keepdims=True),
                                          jnp.where(lane == 1, jnp.sum(dsb, axis=1, keepdims=True), 0.0))
            return dka, dkb, dva, dvb, dca, dcb

        z = jnp.zeros((t, LANES), F32)
        zr = jnp.zeros((1, t), F32)
        carry = step(j, (z, z, z, z, zr, zr), True)
        dka, dkb, dva, dvb, dca, dcb = lax.fori_loop(j + 1, nq, lambda i, carry: step(i, carry, False), carry)
        dk_ref[...] = jnp.where(first, dka, dkb) * (1.0 / LOG2E)
        dv_ref[...] = jnp.where(first, dva, dvb)
        sub = lax.broadcasted_iota(jnp.int32, (8, t), 0)
        dct_ref[...] = jnp.where(sub == 0, dca, jnp.where(sub == 1, dcb, 0.0))

    full = lambda blk: pl.BlockSpec((s, LANES), blk)
    return pl.pallas_call(
        body, name=name, grid=(N_HEADS // 2, nq),
        in_specs=[full(lambda p, j: (0, qblk + p)),
                  pl.BlockSpec((t, LANES), lambda p, j: (j, qblk + 4 + p)),
                  pl.BlockSpec((t, LANES), lambda p, j: (j, qblk + 8 + p)),
                  full(lambda p, j: (0, 0)),
                  pl.BlockSpec((8, t), lambda p, j: (0, j)),
                  pl.BlockSpec((2, s, LANES), lambda p, j: (p, 0, 0)),
                  full(lambda p, j: (0, p)),
                  full(lambda p, j: (0, p)),
                  pl.BlockSpec((8, LANES), lambda p, j: (0, 0))],
        out_specs=[full(lambda p, j: (0, p)),
                   pl.BlockSpec((t, LANES), lambda p, j: (j, p)),
                   pl.BlockSpec((t, LANES), lambda p, j: (j, p)),
                   pl.BlockSpec((None, 8, t), lambda p, j: (p, 0, j)),
                   pl.BlockSpec((None, s, LANES), lambda p, j: (p, 0, 0))],
        out_shape=[jax.ShapeDtypeStruct((s, D_ATT), F32), jax.ShapeDtypeStruct((s, D_ATT), F32),
                   jax.ShapeDtypeStruct((s, D_ATT), F32), jax.ShapeDtypeStruct((N_HEADS // 2, 8, s), F32),
                   jax.ShapeDtypeStruct((N_HEADS // 2, s, LANES), F32)],
        compiler_params=_params("arbitrary", "arbitrary"),
    )(h, h, h, c, ct, lse, att, datt, after)


def _sconv_fwd(h, w, bgblk, name):
    s = h.shape[0]
    nblk = D_CONV // LANES

    def body(bg_ref, cg_ref, hc_ref, w_ref, y_ref):
        rows = lax.broadcasted_iota(jnp.int32, (s, LANES), 0)
        y_ref[...] = (bg_ref[...] * _conv3(cg_ref[...] * hc_ref[...], w_ref, rows)).astype(BF16)

    col = lambda base: pl.BlockSpec((s, LANES), lambda j: (0, base + j))
    return pl.pallas_call(
        body, name=name, grid=(nblk,),
        in_specs=[col(bgblk), col(bgblk + nblk), col(bgblk + 2 * nblk), pl.BlockSpec((3, LANES), lambda j: (0, j))],
        out_specs=pl.BlockSpec((s, LANES), lambda j: (0, j)),
        out_shape=jax.ShapeDtypeStruct((s, D_CONV), BF16), compiler_params=_params("parallel"),
    )(h, h, h, w)


def _sconv_bwd(h, w, dy, bgblk, name):
    s = h.shape[0]
    nblk = D_CONV // LANES

    def body(bg_ref, cg_ref, hc_ref, w_ref, dy_ref, dbg_ref, dcg_ref, dhc_ref, dw_ref):
        rows = lax.broadcasted_iota(jnp.int32, (s, LANES), 0)
        cg, hc, dy, w = cg_ref[...], hc_ref[...], dy_ref[...], w_ref
        xin = cg * hc
        dbg_ref[...] = (dy * _conv3(xin, w, rows)).astype(BF16)
        dxin, dw_ref[...] = _conv3_bwd(dy * bg_ref[...], xin, w, rows)
        dcg_ref[...] = (dxin * hc).astype(BF16)
        dhc_ref[...] = (dxin * cg).astype(BF16)

    col = lambda base: pl.BlockSpec((s, LANES), lambda j: (0, base + j))
    return pl.pallas_call(
        body, name=name, grid=(nblk,),
        in_specs=[col(bgblk), col(bgblk + nblk), col(bgblk + 2 * nblk), pl.BlockSpec((3, LANES), lambda j: (0, j)), col(0)],
        out_specs=[col(0), col(0), col(0), pl.BlockSpec((8, LANES), lambda j: (0, j))],
        out_shape=[jax.ShapeDtypeStruct((s, D_CONV), BF16)] * 3 + [jax.ShapeDtypeStruct((8, D_CONV), F32)],
        compiler_params=_params("parallel"),
    )(h, h, h, w, dy)


def _sgu_group_masks():
    lane = lax.broadcasted_iota(jnp.int32, (1, D_SGU), 1)
    return [(lane // HEAD_DIM) == g for g in range(N_SGU_GROUPS)]


def _sgu_tril():
    r = lax.broadcasted_iota(jnp.int32, (SGU_CHUNK, SGU_CHUNK), 0)
    c = lax.broadcasted_iota(jnp.int32, (SGU_CHUNK, SGU_CHUNK), 1)
    return r >= c


def _sgu_fwd(h, ln_g, ln_b, w_s, b_full, ublk, name):
    s = h.shape[0]
    tr = _tile(s, 512)
    nch = tr // SGU_CHUNK

    def body(u_ref, v_ref, g_ref, b_ref, w_ref, bf_ref, y_ref):
        masks = _sgu_group_masks()
        tril = _sgu_tril()
        wm = [jnp.where(tril, w_ref[g], 0.0).astype(BF16) for g in range(N_SGU_GROUPS)]
        vn = _layer_norm(_gelu(v_ref[...]), g_ref[...], b_ref[...])
        for ch in range(nch):
            rows = pl.ds(ch * SGU_CHUNK, SGU_CHUNK)
            vc = vn[ch * SGU_CHUNK:(ch + 1) * SGU_CHUNK, :]
            mixed = bf_ref[...]
            for g in range(N_SGU_GROUPS):
                mixed = mixed + jnp.dot(wm[g], jnp.where(masks[g], vc, 0.0).astype(BF16), preferred_element_type=F32)
            y_ref[rows, :] = (_gelu(u_ref[rows, :]) * mixed).astype(BF16)

    row = lambda blk: pl.BlockSpec((tr, D_SGU), lambda i: (i, blk))
    vec = pl.BlockSpec((1, D_SGU), lambda i: (0, 0))
    return pl.pallas_call(
        body, name=name, grid=(s // tr,),
        in_specs=[row(ublk), row(ublk + 1), vec, vec,
                  pl.BlockSpec((N_SGU_GROUPS, SGU_CHUNK, SGU_CHUNK), lambda i: (0, 0, 0)),
                  pl.BlockSpec((SGU_CHUNK, D_SGU), lambda i: (0, 0))],
        out_specs=row(0), out_shape=jax.ShapeDtypeStruct((s, D_SGU), BF16), compiler_params=_params("parallel"),
    )(h, h, ln_g, ln_b, w_s, b_full)


def _sgu_bwd(h, ln_g, ln_b, w_s, b_full, dy, ublk, name):
    s = h.shape[0]
    tr = _tile(s, 512)
    nch = tr // SGU_CHUNK
    nt_dims = (((1,), (1,)), ((), ()))

    def norm(v, g, b):
        return _layer_norm(_gelu(v), g, b)

    def body(u_ref, v_ref, g_ref, b_ref, w_ref, bf_ref, dy_ref, du_ref, dv_ref, dg_ref, db_ref, dw_ref, dbf_ref):
        masks = _sgu_group_masks()
        tril = _sgu_tril()
        wf = [jnp.where(tril, w_ref[g], 0.0) for g in range(N_SGU_GROUPS)]
        wm = [w.astype(BF16) for w in wf]
        wmt = [jnp.transpose(w).astype(BF16) for w in wf]
        vn, vjp = jax.vjp(norm, v_ref[...], g_ref[...], b_ref[...])

        @pl.when(pl.program_id(0) == 0)
        def _():
            dg_ref[...] = jnp.zeros_like(dg_ref)
            db_ref[...] = jnp.zeros_like(db_ref)
            dw_ref[...] = jnp.zeros_like(dw_ref)
            dbf_ref[...] = jnp.zeros_like(dbf_ref)

        dvn_parts = []
        for ch in range(nch):
            rows = pl.ds(ch * SGU_CHUNK, SGU_CHUNK)
            vc = vn[ch * SGU_CHUNK:(ch + 1) * SGU_CHUNK, :]
            vc16 = vc.astype(BF16)
            mixed = bf_ref[...]
            for g in range(N_SGU_GROUPS):
                mixed = mixed + jnp.dot(wm[g], jnp.where(masks[g], vc, 0.0).astype(BF16), preferred_element_type=F32)
            dy = dy_ref[rows, :]
            ug, slope = _gelu_and_slope(u_ref[rows, :])
            du_ref[rows, :] = (dy * mixed * slope).astype(BF16)
            dmixed = dy * ug
            dbf_ref[...] += dmixed
            dvc = jnp.zeros((SGU_CHUNK, D_SGU), F32)
            for g in range(N_SGU_GROUPS):
                dm16 = jnp.where(masks[g], dmixed, 0.0).astype(BF16)
                dw_ref[g] += jnp.where(tril, lax.dot_general(dm16, vc16, nt_dims, preferred_element_type=F32), 0.0)
                dvc = dvc + jnp.dot(wmt[g], dm16, preferred_element_type=F32)
            dvn_parts.append(dvc)
        dv, dg, db = vjp(jnp.concatenate(dvn_parts, axis=0))
        dv_ref[...] = dv.astype(BF16)
        dg_ref[...] += dg
        db_ref[...] += db

    row = lambda blk: pl.BlockSpec((tr, D_SGU), lambda i: (i, blk))
    vec = pl.BlockSpec((1, D_SGU), lambda i: (0, 0))
    wsp = pl.BlockSpec((N_SGU_GROUPS, SGU_CHUNK, SGU_CHUNK), lambda i: (0, 0, 0))
    bsp = pl.BlockSpec((SGU_CHUNK, D_SGU), lambda i: (0, 0))
    return pl.pallas_call(
        body, name=name, grid=(s // tr,),
        in_specs=[row(ublk), row(ublk + 1), vec, vec, wsp, bsp, row(0)],
        out_specs=[row(0), row(0), vec, vec, wsp, bsp],
        out_shape=[jax.ShapeDtypeStruct((s, D_SGU), BF16), jax.ShapeDtypeStruct((s, D_SGU), BF16),
                   jax.ShapeDtypeStruct((1, D_SGU), F32), jax.ShapeDtypeStruct((1, D_SGU), F32),
                   jax.ShapeDtypeStruct((N_SGU_GROUPS, SGU_CHUNK, SGU_CHUNK), F32),
                   jax.ShapeDtypeStruct((SGU_CHUNK, D_SGU), F32)],
        compiler_params=_params("arbitrary"),
    )(h, h, ln_g, ln_b, w_s, b_full, dy)


def _merge_fwd(h, b_gate, att, yc, ys, wa, wc, ws, w_out, x, g_post, g_next, name):
    s, d = att.shape[0], wa.shape[1]
    tm = _tile(s, 512)

    def body(g0_ref, g1_ref, g2_ref, bg_ref, a_ref, c_ref, s_ref, wa_ref, wc_ref, ws_ref, wo_ref, x_ref, gp_ref, gn_ref,
             m_ref, o_ref, x1_ref, xn_ref):
        acc = jax.nn.sigmoid(g0_ref[...] + bg_ref[0:1, :]) * jnp.dot(a_ref[...], wa_ref[...], preferred_element_type=F32)
        acc += jax.nn.sigmoid(g1_ref[...] + bg_ref[1:2, :]) * jnp.dot(c_ref[...], wc_ref[...], preferred_element_type=F32)
        acc += jax.nn.sigmoid(g2_ref[...] + bg_ref[2:3, :]) * jnp.dot(s_ref[...], ws_ref[...], preferred_element_type=F32)
        merged = acc.astype(BF16)
        m_ref[...] = merged
        o = jnp.dot(merged, wo_ref[...], preferred_element_type=F32)
        o_ref[...] = o
        x1 = x_ref[...] + _rms(o, gp_ref[...])
        x1_ref[...] = x1
        xn_ref[...] = _rms(x1, gn_ref[...]).astype(BF16)

    gate = lambda b: pl.BlockSpec((tm, d), lambda i: (i, b))
    act = lambda k: pl.BlockSpec((tm, k), lambda i: (i, 0))
    wgt = lambda k: pl.BlockSpec((k, d), lambda i: (0, 0), pipeline_mode=pl.Buffered(1))
    vec = pl.BlockSpec((1, d), lambda i: (0, 0))
    return pl.pallas_call(
        body, name=name, grid=(s // tm,),
        in_specs=[gate(0), gate(1), gate(2), pl.BlockSpec((3, d), lambda i: (0, 0)),
                  act(D_ATT), act(D_CONV), act(D_SGU), wgt(D_ATT), wgt(D_CONV), wgt(D_SGU), wgt(d), act(d), vec, vec],
        out_specs=[act(d)] * 4,
        out_shape=[jax.ShapeDtypeStruct((s, d), BF16), jax.ShapeDtypeStruct((s, d), F32), jax.ShapeDtypeStruct((s, d), F32),
                   jax.ShapeDtypeStruct((s, d), BF16)],
        compiler_params=_params("parallel"),
    )(h, h, h, b_gate, att, yc, ys, wa, wc, ws, w_out, x, g_post, g_next)


def _merge_bwd(h, b_gate, att, yc, ys, wa, wc, ws, dm, name):
    s, d = att.shape[0], wa.shape[1]
    tm = _tile(s, 512)
    nt_dims, tn_dims = (((1,), (1,)), ((), ())), (((0,), (0,)), ((), ()))
    widths = (D_ATT, D_CONV, D_SGU)

    def body(g0_ref, g1_ref, g2_ref, bg_ref, a_ref, c_ref, s_ref, wa_ref, wc_ref, ws_ref, dm_ref,
             da_ref, dc_ref, ds_ref, ga_ref, gc_ref, gs_ref, dgl_ref, dbg_ref, acc_a, acc_c, acc_s):
        i = pl.program_id(0)

        @pl.when(i == 0)
        def _():
            dbg_ref[...] = jnp.zeros_like(dbg_ref)
            for acc in (acc_a, acc_c, acc_s):
                acc[...] = jnp.zeros_like(acc)

        dm = dm_ref[...]
        sums = []
        for b, (g_ref, x_ref, w_ref, dx_ref, acc) in enumerate((
                (g0_ref, a_ref, wa_ref, da_ref, acc_a), (g1_ref, c_ref, wc_ref, dc_ref, acc_c),
                (g2_ref, s_ref, ws_ref, ds_ref, acc_s))):
            gate = jax.nn.sigmoid(g_ref[...] + bg_ref[b:b + 1, :])
            x, w = x_ref[...], w_ref[...]
            y = jnp.dot(x, w, preferred_element_type=F32)
            dy = (dm * gate).astype(BF16)
            dgl = dm * y * gate * (1.0 - gate)
            dgl_ref[b] = dgl.astype(BF16)
            sums.append(jnp.sum(dgl, axis=0, keepdims=True))
            dx_ref[...] = lax.dot_general(dy, w, nt_dims, preferred_element_type=F32)
            acc[...] += lax.dot_general(x, dy, tn_dims, preferred_element_type=F32)
        sub = lax.broadcasted_iota(jnp.int32, (3, d), 0)
        dbg_ref[...] += jnp.where(sub == 0, sums[0], jnp.where(sub == 1, sums[1], sums[2]))

        @pl.when(i == pl.num_programs(0) - 1)
        def _():
            for g_out, acc in ((ga_ref, acc_a), (gc_ref, acc_c), (gs_ref, acc_s)):
                g_out[...] = acc[...].astype(BF16)

    gate = lambda b: pl.BlockSpec((tm, d), lambda i: (i, b))
    act = lambda k: pl.BlockSpec((tm, k), lambda i: (i, 0))
    wgt = lambda k: pl.BlockSpec((k, d), lambda i: (0, 0), pipeline_mode=pl.Buffered(1))
    res = pl.pallas_call(
        body, name=name, grid=(s // tm,),
        in_specs=[gate(0), gate(1), gate(2), pl.BlockSpec((3, d), lambda i: (0, 0)),
                  act(D_ATT), act(D_CONV), act(D_SGU), wgt(D_ATT), wgt(D_CONV), wgt(D_SGU), act(d)],
        out_specs=[act(k) for k in widths] + [pl.BlockSpec((k, d), lambda i: (0, 0)) for k in widths]
        + [pl.BlockSpec((3, tm, d), lambda i: (0, i, 0)), pl.BlockSpec((3, d), lambda i: (0, 0))],
        out_shape=[jax.ShapeDtypeStruct((s, k), F32) for k in widths] + [jax.ShapeDtypeStruct((k, d), BF16) for k in widths]
        + [jax.ShapeDtypeStruct((3, s, d), BF16), jax.ShapeDtypeStruct((3, d), F32)],
        scratch_shapes=[pltpu.VMEM((k, d), F32) for k in widths],
        compiler_params=_params("arbitrary"),
    )(h, h, h, b_gate, att, yc, ys, wa, wc, ws, dm)
    return res[0:3], res[3:6], res[6], res[7]


def _ffn_act_fwd(hh, cw, name):
    s, dff = hh.shape[0], hh.shape[1] // 2
    nblk = dff // LANES

    def body(a_ref, b_ref, wa_ref, wb_ref, z_ref):
        rows = lax.broadcasted_iota(jnp.int32, (s, LANES), 0)
        z_ref[...] = (_gelu(_conv3(a_ref[...], wa_ref, rows)) * _conv3(b_ref[...], wb_ref, rows)).astype(BF16)

    col = lambda base: pl.BlockSpec((s, LANES), lambda j: (0, base + j))
    wsp = lambda base: pl.BlockSpec((3, LANES), lambda j: (0, base + j))
    return pl.pallas_call(
        body, name=name, grid=(nblk,), in_specs=[col(0), col(nblk), wsp(0), wsp(nblk)], out_specs=col(0),
        out_shape=jax.ShapeDtypeStruct((s, dff), BF16), compiler_params=_params("parallel"),
    )(hh, hh, cw, cw)


def _ffn_act_bwd(hh, cw, dz, name):
    s, dff = hh.shape[0], hh.shape[1] // 2
    nblk = dff // LANES

    def body(a_ref, b_ref, wa_ref, wb_ref, dz_ref, da_ref, db_ref, dwa_ref, dwb_ref):
        rows = lax.broadcasted_iota(jnp.int32, (s, LANES), 0)
        a, b, dz = a_ref[...], b_ref[...], dz_ref[...]
        ga, slope = _gelu_and_slope(_conv3(a, wa_ref, rows))
        da, dwa_ref[...] = _conv3_bwd(dz * _conv3(b, wb_ref, rows) * slope, a, wa_ref, rows)
        db, dwb_ref[...] = _conv3_bwd(dz * ga, b, wb_ref, rows)
        da_ref[...] = da.astype(BF16)
        db_ref[...] = db.astype(BF16)

    col = lambda base: pl.BlockSpec((s, LANES), lambda j: (0, base + j))
    wsp = lambda base: pl.BlockSpec((3, LANES), lambda j: (0, base + j))
    w8 = lambda base: pl.BlockSpec((8, LANES), lambda j: (0, base + j))
    return pl.pallas_call(
        body, name=name, grid=(nblk,), in_specs=[col(0), col(nblk), wsp(0), wsp(nblk), col(0)],
        out_specs=[col(0), col(0), w8(0), w8(0)],
        out_shape=[jax.ShapeDtypeStruct((s, dff), BF16)] * 2 + [jax.ShapeDtypeStruct((8, dff), F32)] * 2,
        compiler_params=_params("parallel"),
    )(hh, hh, cw, cw, dz)


ANY = pl.BlockSpec(memory_space=pl.ANY)


def _place():
    return lax.axis_index("x"), lax.axis_index("y"), lax.axis_index("c")


def _all_gather(arrs, name):
    n = len(arrs)

    def body(*refs):
        ins, outs = refs[:n], refs[n:2 * n]
        send_sems, recv_sems, local_sems = refs[2 * n:]
        x, y, c = _place()
        me, sibling = (x, y, c), (x, y, 1 - c)
        chips = [(1 - x, y), (x, 1 - y), (1 - x, 1 - y)]

        def slab(a, dev):
            return outs[a].at[4 * dev[0] + 2 * dev[1] + dev[2]]

        def copy(a, k, block, to, src=None):
            return pltpu.make_async_remote_copy(
                src_ref=slab(a, block) if src is None else src, dst_ref=slab(a, block),
                send_sem=send_sems.at[7 * a + k], recv_sem=recv_sems.at[7 * a + k], device_id=to, device_id_type=MESH)

        mine = [pltpu.make_async_copy(ins[a], slab(a, me), local_sems.at[a]) for a in range(n)]
        for cp in mine:
            cp.start()
        first = []
        for a in range(n):
            first.append(copy(a, 0, me, sibling, src=ins[a]))
            first += [copy(a, 1 + j, me, (*chip, c), src=ins[a]) for j, chip in enumerate(chips)]
        for cp in first:
            cp.start()
        passed = []
        for a in range(n):
            for j, chip in enumerate(chips):
                copy(a, 1 + j, (*chip, c), me).wait_recv()
                fwd = copy(a, 4 + j, (*chip, c), sibling)
                fwd.start()
                passed.append(fwd)
        for a in range(n):
            copy(a, 0, sibling, me).wait_recv()
            for j, chip in enumerate(chips):
                copy(a, 4 + j, (*chip, 1 - c), me).wait_recv()
        for cp in first + passed:
            cp.wait_send()
        for cp in mine:
            cp.wait()

    return pl.pallas_call(
        body, name=name, in_specs=[ANY] * n, out_specs=[ANY] * n,
        out_shape=[jax.ShapeDtypeStruct((N_DEV,) + a.shape, a.dtype) for a in arrs],
        scratch_shapes=[pltpu.SemaphoreType.DMA((7 * n,)), pltpu.SemaphoreType.DMA((7 * n,)), pltpu.SemaphoreType.DMA((n,))],
    )(*arrs)


HBM = pl.BlockSpec(memory_space=pltpu.HBM)
SEM = pl.BlockSpec(memory_space=pltpu.SEMAPHORE)
EFFECT = pltpu.SideEffectType.DATAFLOW_SIDE_EFFECTING


def _slot(dev):
    return 4 * dev[0] + 2 * dev[1] + dev[2]


def _exchange_copies(src_refs, land_refs, send_sems, recv_sems, src_view, land_view):
    x, y, c = _place()
    me = (x, y, c)
    peers = [(1 - x if r & 4 else x, 1 - y if r & 2 else y, 1 - c if r & 1 else c) for r in range(1, N_DEV)]
    sends, lands = [], []
    for a, (src, land) in enumerate(zip(src_refs, land_refs)):
        for k, peer in enumerate(peers):
            sems = dict(send_sem=send_sems.at[7 * a + k], recv_sem=recv_sems.at[7 * a + k], device_id=peer,
                        device_id_type=MESH)
            sends.append(pltpu.make_async_remote_copy(src_ref=src_view(src, _slot(peer)),
                                                      dst_ref=land_view(land, _slot(me)), **sems))
            lands.append(pltpu.make_async_remote_copy(src_ref=src_view(src, _slot(me)),
                                                      dst_ref=land_view(land, _slot(peer)), **sems))
    return sends, lands


def _own_copies(src_refs, land_refs, local_sems, src_view, land_view):
    me = _slot(_place())
    return [pltpu.make_async_copy(src_view(src, me), land_view(land, me), local_sems.at[a])
            for a, (src, land) in enumerate(zip(src_refs, land_refs))]


def _exchange_start(srcs, lands, after, src_view, land_view, name):
    n = len(srcs)

    def body(*refs):
        src_refs, land_refs = refs[:n], refs[n:2 * n]
        send_sems, recv_sems, local_sems = refs[2 * n + 1:2 * n + 4]
        token = refs[-1]
        sends, _ = _exchange_copies(src_refs, land_refs, send_sems, recv_sems, src_view, land_view)
        for cp in sends + _own_copies(src_refs, land_refs, local_sems, src_view, land_view):
            cp.start()
        token[...] = jnp.zeros_like(token)

    thru = [pltpu.HBM(a.shape, a.dtype) for a in list(srcs) + list(lands)]
    outs = pl.pallas_call(
        body, name=name,
        out_shape=(pltpu.SemaphoreType.DMA((7 * n,)), pltpu.SemaphoreType.DMA((7 * n,)), pltpu.SemaphoreType.DMA((n,)),
                   *thru, jax.ShapeDtypeStruct((8, LANES), F32)),
        in_specs=[HBM] * (2 * n) + [ANY],
        out_specs=(SEM, SEM, SEM, *([HBM] * (2 * n)), pl.BlockSpec(memory_space=pltpu.VMEM)),
        input_output_aliases={i: 3 + i for i in range(2 * n)},
        compiler_params=pltpu.CompilerParams(has_side_effects=EFFECT),
    )(*[pltpu.with_memory_space_constraint(a, pltpu.HBM) for a in list(srcs) + list(lands)], after)
    return outs[:3], list(outs[3:3 + n]), list(outs[3 + n:3 + 2 * n]), outs[-1]


def _exchange_wait(sems, srcs, lands, after, src_view, land_view, name):
    n = len(srcs)

    def body(*refs):
        src_refs, land_refs = refs[:n], refs[n:2 * n]
        send_sems, recv_sems, local_sems = refs[2 * n:2 * n + 3]
        sends, landed = _exchange_copies(src_refs, land_refs, send_sems, recv_sems, src_view, land_view)
        for cp in sends:
            cp.wait_send()
        for cp in landed:
            cp.wait_recv()
        for cp in _own_copies(src_refs, land_refs, local_sems, src_view, land_view):
            cp.wait()

    after = after if isinstance(after, (list, tuple)) else [after]
    outs = pl.pallas_call(
        body, name=name, out_shape=[pltpu.HBM(a.shape, a.dtype) for a in list(srcs) + list(lands)],
        in_specs=[HBM] * (2 * n) + [SEM, SEM, SEM] + [ANY] * len(after), out_specs=[HBM] * (2 * n),
        input_output_aliases={i: i for i in range(2 * n)},
        compiler_params=pltpu.CompilerParams(has_side_effects=EFFECT),
    )(*srcs, *lands, *sems, *after)
    return list(outs[:n]), list(outs[n:])


ADAMW_BLOCK_BYTES = 1 << 19
PACK_ROWS = 256


def _adamw(slabs, w, m, v, name):
    nl, r, c = w.shape
    row_edges = [r] + [t for t in range(8, r, 8) if r % t == 0]
    col_edges = [c] + [t for t in range(LANES, c, LANES) if c % t == 0]
    fits = [(tr * tc, tc, tr) for tr in row_edges for tc in col_edges if tr * tc * 4 <= ADAMW_BLOCK_BYTES]
    _, tc, tr = max(fits) if fits else (0, min(col_edges), min(row_edges))

    def body(s_ref, w_ref, m_ref, v_ref, g_ref, d_ref, nm_ref, nv_ref):
        g = s_ref[0].astype(F32)
        for q in range(1, N_DEV):
            g = g + s_ref[q].astype(F32)
        m_new = ADAM_B1 * m_ref[...] + (1.0 - ADAM_B1) * g
        v_new = ADAM_B2 * v_ref[...] + (1.0 - ADAM_B2) * (g * g)
        m_hat = m_new / (1.0 - ADAM_B1 ** ADAM_STEP)
        v_hat = v_new / (1.0 - ADAM_B2 ** ADAM_STEP)
        g_ref[...] = g
        d_ref[...] = -ADAM_LR * (m_hat / (jnp.sqrt(v_hat) + ADAM_EPS) + ADAM_WD * w_ref[...])
        nm_ref[...] = m_new
        nv_ref[...] = v_new

    blk = pl.BlockSpec((None, tr, tc), lambda l, i, j: (l, i, j))
    return pl.pallas_call(
        body, name=name, grid=(nl, r // tr, c // tc),
        in_specs=[pl.BlockSpec((N_DEV, None, tr, tc), lambda l, i, j: (0, l, i, j)), blk, blk, blk],
        out_specs=[blk] * 4, out_shape=[jax.ShapeDtypeStruct(w.shape, F32)] * 4,
        compiler_params=_params("parallel", "parallel", "parallel"),
    )(slabs, w, m, v)


def _layout(d):
    off = {"gate": 0, "q": 3 * d}
    off["bg"] = off["q"] + 3 * D_ATT
    off["u"] = off["bg"] + 3 * D_CONV
    off["f"] = off["u"] + 2 * D_SGU
    width = -(-(off["f"] + LANES) // 512) * 512
    return off, width


def _pad_w_in(wt, d, token):
    off, width = _layout(d)
    nqkv, nrest = 3 * D_ATT, 3 * D_CONV + 2 * D_SGU
    pad = jnp.zeros((width - off["f"] - N_HEADS, wt.shape[1]), wt.dtype) + token[0, 0].astype(wt.dtype)
    return jnp.concatenate([wt[nqkv + N_HEADS + nrest:], wt[:nqkv], wt[nqkv + N_HEADS:nqkv + N_HEADS + nrest],
                            wt[nqkv:nqkv + N_HEADS], pad], axis=0)


def _unpad_w_in(wtp, d):
    off, _ = _layout(d)
    return jnp.concatenate([wtp[off["q"]:off["bg"]], wtp[off["f"]:off["f"] + N_HEADS], wtp[off["bg"]:off["f"]],
                            wtp[:off["q"]]], axis=0)


def _cols_from_slabs(g):
    return jnp.transpose(g, (1, 0, 2)).reshape(g.shape[1], N_DEV * g.shape[2])


def _cols_to_slabs(w):
    r, c = w.shape[0], w.shape[1] // N_DEV
    return jnp.transpose(w.reshape(r, N_DEV, c), (1, 0, 2))


def kernel(x, pre_mix_g, post_mix_g, pre_ffn_g, post_ffn_g, w_in, b_forget, b_gate, conv_mix_w, sgu_ln_g, sgu_ln_b, sgu_w, sgu_b, w_branch_att, w_branch_conv, w_branch_sgu, w_out, w_ffn_up, conv_ffn_w, w_ffn_down, loss_target, m_pre_mix_g, m_post_mix_g, m_pre_ffn_g, m_post_ffn_g, m_w_in, m_b_forget, m_b_gate, m_conv_mix_w, m_sgu_ln_g, m_sgu_ln_b, m_sgu_w, m_sgu_b, m_w_branch_att, m_w_branch_conv, m_w_branch_sgu, m_w_out, m_w_ffn_up, m_conv_ffn_w, m_w_ffn_down, v_pre_mix_g, v_post_mix_g, v_pre_ffn_g, v_post_ffn_g, v_w_in, v_b_forget, v_b_gate, v_conv_mix_w, v_sgu_ln_g, v_sgu_ln_b, v_sgu_w, v_sgu_b, v_w_branch_att, v_w_branch_conv, v_w_branch_sgu, v_w_out, v_w_ffn_up, v_conv_ffn_w, v_w_ffn_down):
    depth = w_in.shape[0]
    s, d = x.shape[1], x.shape[2]
    dff = w_ffn_down.shape[1] * N_DEV
    off, _ = _layout(d)
    qblk, bgblk, ublk, fblk = off["q"] // LANES, off["bg"] // LANES, off["u"] // D_SGU, off["f"] // LANES
    x0 = x.reshape(s, d)
    target = loss_target.reshape(s, d)
    ncm, ncf = conv_mix_w.shape[2], conv_ffn_w.shape[2]

    lo = d // N_DEV

    whole = lambda ref, slot: ref
    slab = lambda ref, slot: ref.at[slot]

    w_in_t, w_up_t = jnp.transpose(w_in, (0, 2, 1)), jnp.transpose(w_ffn_up, (0, 2, 1))

    def shards_of(l, part):
        if part == "mix":
            small = jnp.concatenate([b_gate[l], conv_mix_w[l], conv_ffn_w[l]], axis=1)
            return [w_in_t[l].astype(BF16), w_branch_att[l].astype(BF16), w_branch_conv[l].astype(BF16),
                    w_branch_sgu[l].astype(BF16), w_out[l].astype(BF16), small]
        return [w_up_t[l].astype(BF16), w_ffn_down[l].astype(BF16)]

    def gather_start(l, part, after):
        shards = shards_of(l, part)
        lands = [lax.empty((N_DEV,) + a.shape, a.dtype) for a in shards]
        return _exchange_start(shards, lands, after, whole, slab, name=f"gather_start_{part}_{l}")

    def gather_finish(l, part, started, after):
        sems, shards, lands, _ = started[part]
        shards, lands = _exchange_wait(sems, shards, lands, after, whole, slab, name=f"gather_wait_{part}_{l}")
        token = jnp.zeros((8, LANES), F32)
        if l + 1 < depth:
            started[part] = gather_start(l + 1, part, lands[0])
            token = started[part][3]
        return lands, token

    def bfull(l):
        return jnp.repeat(jnp.transpose(sgu_b[l]), HEAD_DIM, axis=1)

    def bf_pad(l):
        return jnp.pad(b_forget[l], (0, LANES - N_HEADS)).reshape(1, LANES)

    saved = []
    weights = []
    xin = x0
    xn = _prenorm(x0, pre_mix_g[0:1], name="prenorm_first")
    loss_acc = dy = None
    first = _all_gather(shards_of(0, "mix"), name="gather_first")
    started = {"ffn": gather_start(0, "ffn", first[0])}
    if depth > 1:
        started["mix"] = gather_start(1, "mix", started["ffn"][3])
    for l in range(depth):
        if l == 0:
            (g_in, g_a, g_c, g_s, g_o, g_small), token = first, started["mix" if depth > 1 else "ffn"][3]
        else:
            (g_in, g_a, g_c, g_s, g_o, g_small), token = gather_finish(l, "mix", started, xin)
        g_small = _cols_from_slabs(g_small).reshape(3, N_DEV, -1)
        w = dict(w_in=_pad_w_in(g_in.reshape(N_DEV * g_in.shape[1], d), d, token), wa=_cols_from_slabs(g_a),
                 wc=_cols_from_slabs(g_c), ws=_cols_from_slabs(g_s), w_out=g_o.reshape(d, d),
                 b_gate=g_small[:, :, :lo].reshape(3, d), cmw=g_small[:, :, lo:lo + ncm].reshape(3, D_CONV),
                 cfw=g_small[:, :, lo + ncm:].reshape(3, 2 * dff))
        weights.append(w)
        h = _mm(xn, w["w_in"], "nt", F32, name="proj_in")
        c, ct = _forget_prep(h, bf_pad(l), fblk, name="forget_prep")
        att, lse = _attention_fwd(h, c, ct, qblk, name="attention_fwd")
        yc = _sconv_fwd(h, w["cmw"], bgblk, name="sconv_fwd")
        ys = _sgu_fwd(h, sgu_ln_g[l:l + 1], sgu_ln_b[l:l + 1], sgu_w[l], bfull(l), ublk, name="sgu_fwd")
        (g_up, g_dn), token = gather_finish(l, "ffn", started, ys)
        w["w_up"], w["w_dn"] = g_up.reshape(2 * dff, d), g_dn.reshape(dff, d)
        merged, o, x1, xn2 = _merge_fwd(h, w["b_gate"], att, yc, ys, w["wa"], w["wc"], w["ws"], w["w_out"], xin,
                                        post_mix_g[l:l + 1], pre_ffn_g[l:l + 1] + token[0, 0], name="merge_proj_out")
        hh = _mm(xn2, w["w_up"], "nt", F32, name="ffn_up")
        z = _ffn_act_fwd(hh, w["cfw"], name="ffn_act_fwd")
        layer = dict(xin=xin, xn=xn, h=h, c=c, ct=ct, lse=lse, att=att, yc=yc, ys=ys, merged=merged, o=o, x1=x1,
                     xn2=xn2, hh=hh, z=z)
        if l + 1 < depth:
            layer["f"], xin, xn = _mm_postnorm(z, w["w_dn"], x1, post_ffn_g[l:l + 1], pre_mix_g[l + 1:l + 2],
                                               name="ffn_down")
        else:
            layer["f"], dy, loss_acc = _mm_postnorm_loss(z, w["w_dn"], x1, post_ffn_g[l:l + 1], target,
                                                         name="ffn_down_loss")
        saved.append(layer)
    loss = lax.psum(loss_acc[0, 0] * (0.5 / d), ("x", "y", "c"))

    rep = {k: [None] * depth for k in ("pre_mix_g", "post_mix_g", "pre_ffn_g", "post_ffn_g", "b_forget", "sgu_ln_g",
                                       "sgu_ln_b", "sgu_w", "sgu_b")}
    nsmall = lo + ncm + ncf
    lands = {"win": [lax.empty((N_DEV, depth) + w_in_t.shape[1:], BF16)],
             "mid": [lax.empty((N_DEV, depth) + shp, dt) for shp, dt in (
                 (w_branch_att.shape[1:], BF16), (w_branch_conv.shape[1:], BF16), (w_branch_sgu.shape[1:], BF16),
                 (w_out.shape[1:], BF16), ((3, nsmall), F32))],
             "ffn": [lax.empty((N_DEV, depth) + shp, BF16) for shp in (w_up_t.shape[1:], w_ffn_down.shape[1:])]}
    scatters = {part: [None] * depth for part in lands}

    def scatter_start(l, part, sends, after):
        layer_slab = lambda ref, slot: ref.at[slot, l]
        sems, sends, lands[part], token = _exchange_start(sends, lands[part], after, slab, layer_slab,
                                                          name=f"scatter_start_{part}_{l}")
        scatters[part][l] = (sems, sends, layer_slab)
        return token

    def scatter_finish(part, after):
        for l in range(depth):
            sems, sends, layer_slab = scatters[part][l]
            _, lands[part] = _exchange_wait(sems, sends, lands[part], after, slab, layer_slab,
                                            name=f"scatter_wait_{part}_{l}")
        return lands[part]

    token = jnp.zeros((8, LANES), F32)
    dx = dy
    for l in reversed(range(depth)):
        w, a = weights[l], saved[l]
        df, rep["post_ffn_g"][l] = _postnorm_bwd(a["f"], post_ffn_g[l:l + 1] + token[0, 0], dx, name="postnorm_bwd")
        dz = _mm(df, w["w_dn"], "nt", F32, name="ffn_down_dx")
        g_dn = _mm(a["z"], df, "tn", BF16, name="ffn_down_dw")
        dha, dhb, dcwa, dcwb = _ffn_act_bwd(a["hh"], w["cfw"], dz, name="ffn_act_bwd")
        dhh = jnp.concatenate([dha, dhb], axis=1)
        dcfw = jnp.concatenate([dcwa[0:3], dcwb[0:3]], axis=1)
        g_up = _mm(dhh, a["xn2"], "tn", BF16, name="ffn_up_dw")
        token = scatter_start(l, "ffn", [g_up.reshape(N_DEV, 2 * dff // N_DEV, d), g_dn.reshape(N_DEV, dff // N_DEV, d)],
                              dz)
        dx1, rep["pre_ffn_g"][l] = _mm_prenorm_bwd(dhh, w["w_up"], a["x1"], pre_ffn_g[l:l + 1], dx, token, name="ffn_up_dx")
        do, rep["post_mix_g"][l], dmerged = _postnorm_bwd_mm(a["o"], post_mix_g[l:l + 1], dx1, w["w_out"], name="proj_out_dx")
        g_o = _mm(a["merged"], do, "tn", BF16, name="proj_out_dw")
        (datt, dconv, dsgu), (g_a, g_c, g_s), dgl, dbg = _merge_bwd(
            a["h"], w["b_gate"], a["att"], a["yc"], a["ys"], w["wa"], w["wc"], w["ws"], dmerged, name="merge_bwd")
        dbgate, dcg, dhc, dcmw = _sconv_bwd(a["h"], w["cmw"], dconv, bgblk, name="sconv_bwd")
        sends = [_cols_to_slabs(g_a), _cols_to_slabs(g_c), _cols_to_slabs(g_s), g_o.reshape(N_DEV, d // N_DEV, d),
                 jnp.concatenate([_cols_to_slabs(dbg), _cols_to_slabs(dcmw[0:3]), _cols_to_slabs(dcfw)], axis=2)]
        token = scatter_start(l, "mid", sends, dx1)
        dq, dk, dv, dct4, dcq4 = _attention_bwd(a["h"], a["c"], a["ct"], a["lse"], a["att"], datt, token, qblk,
                                                name="attention_bwd")
        dfl, dbf = _forget_prep_bwd(a["h"], bf_pad(l), dct4, dcq4, fblk, name="forget_prep_bwd")
        rep["b_forget"][l] = dbf[0, :N_HEADS]
        du, dvs, dlg, dlb, dsw, dbfull = _sgu_bwd(a["h"], sgu_ln_g[l:l + 1], sgu_ln_b[l:l + 1], sgu_w[l], bfull(l), dsgu,
                                                  ublk, name="sgu_bwd")
        rep["sgu_ln_g"][l], rep["sgu_ln_b"][l], rep["sgu_w"][l] = dlg, dlb, dsw
        rep["sgu_b"][l] = jnp.transpose(jnp.sum(dbfull.reshape(SGU_CHUNK, N_SGU_GROUPS, HEAD_DIM), axis=2))
        dh = jnp.concatenate([dgl[0], dgl[1], dgl[2], dq.astype(BF16), dk.astype(BF16), dv.astype(BF16), dbgate, dcg, dhc,
                              du, dvs, dfl.astype(BF16), jnp.zeros((s, w["w_in"].shape[0] - off["f"] - LANES), BF16)], axis=1)
        g_in = _mm(dh, a["xn"], "tn", BF16, name="proj_in_dw")
        token = scatter_start(l, "win", [_unpad_w_in(g_in, d).reshape(N_DEV, -1, d)], dx1)
        dx, rep["pre_mix_g"][l] = _mm_prenorm_bwd(dh, w["w_in"], a["xin"], pre_mix_g[l:l + 1], dx1, token, name="proj_in_dx")

    outs = {}
    t3 = lambda arr: jnp.transpose(arr, (0, 2, 1))

    def update(name_, slabs, w_, m_, v_, transposed=False):
        if transposed:
            w_, m_, v_ = t3(w_), t3(m_), t3(v_)
        shp = w_.shape
        w3 = w_.reshape((shp[0], -1, shp[-1])) if w_.ndim >= 3 else w_.reshape((1,) + shp)
        res = _adamw(slabs.reshape((N_DEV,) + w3.shape), w3, m_.reshape(w3.shape), v_.reshape(w3.shape),
                     name="adamw_" + name_)
        outs[name_] = tuple(t3(t.reshape(shp)) if transposed else t.reshape(shp) for t in res)
        return res[0]

    rep_names = ("pre_mix_g", "post_mix_g", "pre_ffn_g", "post_ffn_g", "b_forget", "sgu_ln_g", "sgu_ln_b", "sgu_w", "sgu_b")
    rep_w = dict(pre_mix_g=(pre_mix_g, m_pre_mix_g, v_pre_mix_g), post_mix_g=(post_mix_g, m_post_mix_g, v_post_mix_g),
                 pre_ffn_g=(pre_ffn_g, m_pre_ffn_g, v_pre_ffn_g), post_ffn_g=(post_ffn_g, m_post_ffn_g, v_post_ffn_g),
                 b_forget=(b_forget, m_b_forget, v_b_forget), sgu_ln_g=(sgu_ln_g, m_sgu_ln_g, v_sgu_ln_g),
                 sgu_ln_b=(sgu_ln_b, m_sgu_ln_b, v_sgu_ln_b), sgu_w=(sgu_w, m_sgu_w, v_sgu_w), sgu_b=(sgu_b, m_sgu_b, v_sgu_b))

    def pack(parts):
        rows = [jnp.pad(p.reshape(-1), (0, -p.size % LANES)).reshape(-1, LANES) for p in parts]
        rows = jnp.concatenate(rows, axis=0)
        return jnp.pad(rows, ((0, -rows.shape[0] % PACK_ROWS), (0, 0)))

    part = pack([jnp.stack([g.reshape(rep_w[k][0].shape[1:]) for g in rep[k]]) for k in rep_names])
    small_sems, small_src, small_land, _ = _exchange_start([part], [lax.empty((N_DEV,) + part.shape, F32)], dx, whole, slab,
                                                            name="gather_small_start")

    got_up, got_dn = scatter_finish("ffn", dx)
    done = [update("w_ffn_up", got_up, w_ffn_up, m_w_ffn_up, v_w_ffn_up, transposed=True),
            update("w_ffn_down", got_dn, w_ffn_down, m_w_ffn_down, v_w_ffn_down)]

    _, (gathered,) = _exchange_wait(small_sems, small_src, small_land, done, whole, slab, name="gather_small_wait")
    packed = [pack([rep_w[k][i] for k in rep_names]) for i in range(3)]
    res = _adamw(gathered.reshape(N_DEV, 1, -1, LANES), *[p.reshape(1, -1, LANES) for p in packed], name="adamw_replicated")
    row = 0
    for k in rep_names:
        shp = rep_w[k][0].shape
        size = math.prod(shp)
        nrows = -(-size // LANES)
        outs[k] = tuple(t[0, row:row + nrows].reshape(-1)[:size].reshape(shp) for t in res)
        row += nrows

    got_a, got_c, got_s, got_o, small = scatter_finish("mid", res[0])
    done = [update("w_branch_att", got_a, w_branch_att, m_w_branch_att, v_w_branch_att),
            update("w_branch_conv", got_c, w_branch_conv, m_w_branch_conv, v_w_branch_conv),
            update("w_branch_sgu", got_s, w_branch_sgu, m_w_branch_sgu, v_w_branch_sgu),
            update("w_out", got_o, w_out, m_w_out, v_w_out),
            update("b_gate", small[..., :lo], b_gate, m_b_gate, v_b_gate),
            update("conv_mix_w", small[..., lo:lo + ncm], conv_mix_w, m_conv_mix_w, v_conv_mix_w),
            update("conv_ffn_w", small[..., lo + ncm:], conv_ffn_w, m_conv_ffn_w, v_conv_ffn_w)]
    (got_in,) = scatter_finish("win", done)
    update("w_in", got_in, w_in, m_w_in, v_w_in, transposed=True)

    order = ("pre_mix_g", "post_mix_g", "pre_ffn_g", "post_ffn_g", "w_in", "b_forget", "b_gate", "conv_mix_w", "sgu_ln_g",
             "sgu_ln_b", "sgu_w", "sgu_b", "w_branch_att", "w_branch_conv", "w_branch_sgu", "w_out", "w_ffn_up",
             "conv_ffn_w", "w_ffn_down")
    grad_x = dx.reshape(x.shape)
    return (loss, grad_x, *[outs[k][0] for k in order], *[outs[k][1] for k in order], *[outs[k][2] for k in order],
            *[outs[k][3] for k in order])
```

```python
import math

import jax
import jax.numpy as jnp
from jax import lax
from jax.experimental import pallas as pl
from jax.experimental.pallas import tpu as pltpu

F32 = jnp.float32
BF16 = jnp.bfloat16

N_DEV = 8
HEAD_DIM = 64
N_HEADS = 8
D_ATT = 512
D_CONV = 256
D_SGU = 256
N_SGU_GROUPS = 4
SGU_CHUNK = 128
RMS_EPS = 1e-6
LN_EPS = 1e-5
ADAM_LR = 0.001
ADAM_B1 = 0.9
ADAM_B2 = 0.999
ADAM_EPS = 1e-08
ADAM_WD = 0.01
ADAM_STEP = 10
LANES = 128
VMEM_LIMIT = 56 * 1024 * 1024
ATT_TILE = 512
LOG2E = math.log2(math.e)
NEG = -1e30
MESH = pl.DeviceIdType.MESH


def _params(*sem):
    return pltpu.CompilerParams(dimension_semantics=sem if sem else None, vmem_limit_bytes=VMEM_LIMIT)


def _tile(n, cap):
    if n <= cap:
        return n
    t = cap - cap % LANES
    while n % t:
        t -= LANES
    return t


def _gelu(x):
    return 0.5 * x * (1.0 + jnp.tanh(math.sqrt(2.0 / math.pi) * (x + 0.044715 * (x * x * x))))


def _gelu_and_slope(x):
    k0, k1 = math.sqrt(2.0 / math.pi), 0.044715
    x2 = x * x
    t = jnp.tanh(x * (k0 + (k0 * k1) * x2))
    half = 0.5 * (1.0 + t)
    return x * half, half + (0.5 * x) * (1.0 - t * t) * (k0 + (3.0 * k0 * k1) * x2)


def _rms(x, g):
    r = lax.rsqrt(jnp.mean(x * x, axis=-1, keepdims=True) + RMS_EPS)
    return x * r * g


def _layer_norm(x, g, b):
    mu = jnp.mean(x, axis=-1, keepdims=True)
    xc = x - mu
    var = jnp.mean(xc * xc, axis=-1, keepdims=True)
    return xc * lax.rsqrt(var + LN_EPS) * g + b


def _shift_down(x, k, rows):
    return jnp.where(rows >= k, pltpu.roll(x, k, 0), 0.0)


def _shift_up(x, k, rows):
    s = x.shape[0]
    return jnp.where(rows < s - k, pltpu.roll(x, s - k, 0), 0.0)


def _conv3(x, w_ref, rows):
    return w_ref[2:3, :] * x + w_ref[1:2, :] * _shift_down(x, 1, rows) + w_ref[0:1, :] * _shift_down(x, 2, rows)


def _conv3_bwd(dy, x, w_ref, rows):
    up1, up2 = _shift_up(dy, 1, rows), _shift_up(dy, 2, rows)
    dx = w_ref[2:3, :] * dy + w_ref[1:2, :] * up1 + w_ref[0:1, :] * up2
    d2 = jnp.sum(dy * x, axis=0, keepdims=True)
    d1 = jnp.sum(up1 * x, axis=0, keepdims=True)
    d0 = jnp.sum(up2 * x, axis=0, keepdims=True)
    sub = lax.broadcasted_iota(jnp.int32, (8, x.shape[1]), 0)
    return dx, jnp.where(sub == 0, d0, jnp.where(sub == 1, d1, jnp.where(sub == 2, d2, 0.0)))


MM_VMEM_BUDGET = 40 * 1024 * 1024
MM_TILE_CAP = 1408


def _mm_tiles(m, n, k, out_bytes):
    def edges(d):
        return [t for t in range(LANES, min(d, MM_TILE_CAP) + 1, LANES) if d % t == 0] or [d]

    best = None
    for tm in edges(m):
        for tn in edges(n):
            if 2 * (2 * k * (tm + tn) + tm * tn * out_bytes) > MM_VMEM_BUDGET:
                continue
            for a_outer in (True, False):
                reads = k * m + (m // tm) * k * n if a_outer else k * n + (n // tn) * k * m
                traffic = 2 * reads + m * n * out_bytes
                key = (traffic, -tm * tn)
                if best is None or key < best[0]:
                    best = (key, (tm, tn, a_outer))
    return best[1]


def _mm(a, b, form, out_dtype, name, after=None):
    if form == "nn":
        (m, k), n = a.shape, b.shape[1]
    elif form == "nt":
        (m, k), n = a.shape, b.shape[0]
    else:
        (k, m), n = a.shape, b.shape[1]
    tm, tn, a_outer = _mm_tiles(m, n, k, jnp.dtype(out_dtype).itemsize)
    dims = {"nn": (((1,), (0,)), ((), ())), "nt": (((1,), (1,)), ((), ())), "tn": (((0,), (0,)), ((), ()))}[form]

    def body(a_ref, b_ref, *rest):
        o_ref = rest[-1]
        o_ref[...] = lax.dot_general(a_ref[...], b_ref[...], dims, preferred_element_type=F32).astype(o_ref.dtype)

    ij = (lambda g0, g1: (g0, g1)) if a_outer else (lambda g0, g1: (g1, g0))
    a_spec = (pl.BlockSpec((k, tm), lambda g0, g1: (0, ij(g0, g1)[0])) if form == "tn"
              else pl.BlockSpec((tm, k), lambda g0, g1: (ij(g0, g1)[0], 0)))
    b_spec = (pl.BlockSpec((tn, k), lambda g0, g1: (ij(g0, g1)[1], 0)) if form == "nt"
              else pl.BlockSpec((k, tn), lambda g0, g1: (0, ij(g0, g1)[1])))
    extra = [] if after is None else [pl.BlockSpec((8, LANES), lambda g0, g1: (0, 0))]
    return pl.pallas_call(
        body, name=name, grid=(m // tm, n // tn) if a_outer else (n // tn, m // tm),
        in_specs=[a_spec, b_spec] + extra, out_specs=pl.BlockSpec((tm, tn), lambda g0, g1: ij(g0, g1)),
        out_shape=jax.ShapeDtypeStruct((m, n), out_dtype),
        compiler_params=_params("parallel", "arbitrary"),
    )(a, b, *([] if after is None else [after]))


def _postnorm_bwd_mm(o, g, dx, b, name):
    s, d = o.shape
    n = b.shape[0]
    tm, tn = _tile(s, 512), _tile(n, MM_TILE_CAP)

    def body(o_ref, g_ref, dx_ref, b_ref, do_ref, dg_ref, out_ref):
        i, j = pl.program_id(0), pl.program_id(1)

        @pl.when(j == 0)
        def _():
            _, vjp = jax.vjp(_rms, o_ref[...], g_ref[...])
            d_o, dg = vjp(dx_ref[...])
            do_ref[...] = d_o.astype(BF16)

            @pl.when(i == 0)
            def _():
                dg_ref[...] = jnp.zeros_like(dg_ref)

            dg_ref[...] += dg

        out_ref[...] = lax.dot_general(do_ref[...], b_ref[...], (((1,), (1,)), ((), ())), preferred_element_type=F32)

    row = pl.BlockSpec((tm, d), lambda i, j: (i, 0))
    vec = pl.BlockSpec((1, d), lambda i, j: (0, 0))
    return pl.pallas_call(
        body, name=name, grid=(s // tm, n // tn),
        in_specs=[row, vec, row, pl.BlockSpec((tn, d), lambda i, j: (j, 0))],
        out_specs=[row, vec, pl.BlockSpec((tm, tn), lambda i, j: (i, j))],
        out_shape=[jax.ShapeDtypeStruct((s, d), BF16), jax.ShapeDtypeStruct((1, d), F32), jax.ShapeDtypeStruct((s, n), F32)],
        compiler_params=_params("arbitrary", "arbitrary"),
    )(o, g, dx, b)


def _mm_prenorm_bwd(a, b, x, g, dres, after, name):
    s, k = a.shape
    d = b.shape[1]
    tm = _tile(s, 512)

    def body(a_ref, b_ref, x_ref, g_ref, dres_ref, after_ref, dx_ref, dg_ref):
        dxn = jnp.dot(a_ref[...], b_ref[...], preferred_element_type=F32)
        _, vjp = jax.vjp(_rms, x_ref[...], g_ref[...])
        dx, dg = vjp(dxn)
        dx_ref[...] = dres_ref[...] + dx

        @pl.when(pl.program_id(0) == 0)
        def _():
            dg_ref[...] = jnp.zeros_like(dg_ref)

        dg_ref[...] += dg

    row = pl.BlockSpec((tm, d), lambda i: (i, 0))
    vec = pl.BlockSpec((1, d), lambda i: (0, 0))
    return pl.pallas_call(
        body, name=name, grid=(s // tm,),
        in_specs=[pl.BlockSpec((tm, k), lambda i: (i, 0)),
                  pl.BlockSpec((k, d), lambda i: (0, 0), pipeline_mode=pl.Buffered(1)),
                  row, vec, row, pl.BlockSpec((8, LANES), lambda i: (0, 0))],
        out_specs=[row, vec], out_shape=[jax.ShapeDtypeStruct((s, d), F32), jax.ShapeDtypeStruct((1, d), F32)],
        compiler_params=_params("arbitrary"),
    )(a, b, x, g, dres, after)


def _prenorm(x, g, name):
    s, d = x.shape
    tm = _tile(s, 512)

    def body(x_ref, g_ref, o_ref):
        o_ref[...] = _rms(x_ref[...], g_ref[...]).astype(BF16)

    return pl.pallas_call(
        body, name=name, grid=(s // tm,),
        in_specs=[pl.BlockSpec((tm, d), lambda i: (i, 0)), pl.BlockSpec((1, d), lambda i: (0, 0))],
        out_specs=pl.BlockSpec((tm, d), lambda i: (i, 0)),
        out_shape=jax.ShapeDtypeStruct((s, d), BF16), compiler_params=_params("parallel"),
    )(x, g)


def _mm_postnorm(a, b, x, g_post, g_next, name):
    s, k = a.shape
    d = b.shape[1]
    tm = _tile(s, 512)

    def body(a_ref, b_ref, x_ref, gp_ref, gn_ref, o_ref, x1_ref, xn_ref):
        o = jnp.dot(a_ref[...], b_ref[...], preferred_element_type=F32)
        o_ref[...] = o
        x1 = x_ref[...] + _rms(o, gp_ref[...])
        x1_ref[...] = x1
        xn_ref[...] = _rms(x1, gn_ref[...]).astype(BF16)

    row = pl.BlockSpec((tm, d), lambda i: (i, 0))
    vec = pl.BlockSpec((1, d), lambda i: (0, 0))
    return pl.pallas_call(
        body, name=name, grid=(s // tm,),
        in_specs=[pl.BlockSpec((tm, k), lambda i: (i, 0)),
                  pl.BlockSpec((k, d), lambda i: (0, 0), pipeline_mode=pl.Buffered(1)), row, vec, vec],
        out_specs=[row, row, row],
        out_shape=[jax.ShapeDtypeStruct((s, d), F32), jax.ShapeDtypeStruct((s, d), F32), jax.ShapeDtypeStruct((s, d), BF16)],
        compiler_params=_params("parallel"),
    )(a, b, x, g_post, g_next)


def _mm_postnorm_loss(a, b, x, g_post, target, name):
    s, k = a.shape
    d = b.shape[1]
    tm = _tile(s, 512)

    def body(a_ref, b_ref, x_ref, gp_ref, t_ref, o_ref, dy_ref, acc_ref):
        o = jnp.dot(a_ref[...], b_ref[...], preferred_element_type=F32)
        o_ref[...] = o
        e = x_ref[...] + _rms(o, gp_ref[...]) - t_ref[...]
        dy_ref[...] = e / d

        @pl.when(pl.program_id(0) == 0)
        def _():
            acc_ref[...] = jnp.zeros_like(acc_ref)

        acc_ref[...] += jnp.sum(jnp.sum(e * e, axis=1, keepdims=True), axis=0, keepdims=True)

    row = pl.BlockSpec((tm, d), lambda i: (i, 0))
    return pl.pallas_call(
        body, name=name, grid=(s // tm,),
        in_specs=[pl.BlockSpec((tm, k), lambda i: (i, 0)),
                  pl.BlockSpec((k, d), lambda i: (0, 0), pipeline_mode=pl.Buffered(1)), row,
                  pl.BlockSpec((1, d), lambda i: (0, 0)), row],
        out_specs=[row, row, pl.BlockSpec((1, LANES), lambda i: (0, 0))],
        out_shape=[jax.ShapeDtypeStruct((s, d), F32), jax.ShapeDtypeStruct((s, d), F32), jax.ShapeDtypeStruct((1, LANES), F32)],
        compiler_params=_params("arbitrary"),
    )(a, b, x, g_post, target)


def _postnorm_bwd(o, g, dx, name):
    s, d = o.shape
    tm = _tile(s, 512)

    def body(o_ref, g_ref, dx_ref, do_ref, dg_ref):
        _, vjp = jax.vjp(_rms, o_ref[...], g_ref[...])
        d_o, dg = vjp(dx_ref[...])
        do_ref[...] = d_o.astype(BF16)

        @pl.when(pl.program_id(0) == 0)
        def _():
            dg_ref[...] = jnp.zeros_like(dg_ref)

        dg_ref[...] += dg

    row = pl.BlockSpec((tm, d), lambda i: (i, 0))
    vec = pl.BlockSpec((1, d), lambda i: (0, 0))
    return pl.pallas_call(
        body, name=name, grid=(s // tm,), in_specs=[row, vec, row], out_specs=[row, vec],
        out_shape=[jax.ShapeDtypeStruct((s, d), BF16), jax.ShapeDtypeStruct((1, d), F32)],
        compiler_params=_params("arbitrary"),
    )(o, g, dx)


def _log_sigmoid(z):
    return jnp.minimum(z, 0.0) - jnp.log(1.0 + jnp.exp(-jnp.abs(z)))


def _forget_prep(h, bf_pad, fblk, name):
    s = h.shape[0]

    def body(f_ref, b_ref, c_ref, ct_ref):
        c = _log_sigmoid(f_ref[...] + b_ref[...])
        rows = lax.broadcasted_iota(jnp.int32, c.shape, 0)
        k = 1
        while k < s:
            c = c + _shift_down(c, k, rows)
            k *= 2
        c_ref[...] = c
        ct_ref[...] = jnp.transpose(c)[0:8, :]

    return pl.pallas_call(
        body, name=name, grid=(1,),
        in_specs=[pl.BlockSpec((s, LANES), lambda i: (0, fblk)), pl.BlockSpec((1, LANES), lambda i: (0, 0))],
        out_specs=[pl.BlockSpec((s, LANES), lambda i: (0, 0)), pl.BlockSpec((8, s), lambda i: (0, 0))],
        out_shape=[jax.ShapeDtypeStruct((s, LANES), F32), jax.ShapeDtypeStruct((8, s), F32)],
        compiler_params=_params("arbitrary"),
    )(h, bf_pad)


def _forget_prep_bwd(h, bf_pad, dct, dcq, fblk, name):
    s = h.shape[0]
    pairs = N_HEADS // 2

    def body(f_ref, b_ref, dct_ref, dcq_ref, df_ref, db_ref):
        dct = dct_ref[0]
        dcq = dcq_ref[0]
        for p in range(1, pairs):
            dct = dct + pltpu.roll(dct_ref[p], 2 * p, 0)
            dcq = dcq + pltpu.roll(dcq_ref[p], 2 * p, 1)
        dc = dcq + jnp.transpose(jnp.concatenate([dct, jnp.zeros((LANES - 8, s), F32)], axis=0))
        rows = lax.broadcasted_iota(jnp.int32, dc.shape, 0)
        k = 1
        while k < s:
            dc = dc + _shift_up(dc, k, rows)
            k *= 2
        z = f_ref[...] + b_ref[...]
        lane = lax.broadcasted_iota(jnp.int32, dc.shape, 1)
        df = jnp.where(lane < N_HEADS, dc * jax.nn.sigmoid(-z), 0.0)
        df_ref[...] = df
        db_ref[...] = jnp.sum(df, axis=0, keepdims=True)

    return pl.pallas_call(
        body, name=name, grid=(1,),
        in_specs=[pl.BlockSpec((s, LANES), lambda i: (0, fblk)), pl.BlockSpec((1, LANES), lambda i: (0, 0)),
                  pl.BlockSpec((pairs, 8, s), lambda i: (0, 0, 0)), pl.BlockSpec((pairs, s, LANES), lambda i: (0, 0, 0))],
        out_specs=[pl.BlockSpec((s, LANES), lambda i: (0, 0)), pl.BlockSpec((1, LANES), lambda i: (0, 0))],
        out_shape=[jax.ShapeDtypeStruct((s, LANES), F32), jax.ShapeDtypeStruct((1, LANES), F32)],
        compiler_params=_params("arbitrary"),
    )(h, bf_pad, dct, dcq)


def _pick_lane(blk, idx):
    lane = lax.broadcasted_iota(jnp.int32, blk.shape, 1)
    return jnp.sum(jnp.where(lane == idx, blk, 0.0), axis=1, keepdims=True)


def _pick_row(blk, idx):
    sub = lax.broadcasted_iota(jnp.int32, blk.shape, 0)
    return jnp.sum(jnp.where(sub == idx, blk, 0.0), axis=0, keepdims=True)


def _attention_fwd(h, c, ct, qblk, name):
    s = h.shape[0]
    t = _tile(s, ATT_TILE)
    nq = s // t
    scale = HEAD_DIM ** -0.5
    nt_dims = (((1,), (1,)), ((), ()))

    def body(q_ref, k_ref, v_ref, c_ref, ct_ref, o_ref, lse_ref):
        p = pl.program_id(0)
        i = pl.program_id(1)
        lane = lax.broadcasted_iota(jnp.int32, (1, LANES), 1)
        first = lane < HEAD_DIM
        q = q_ref[...] * (scale * LOG2E)
        qa = jnp.where(first, q, 0.0).astype(BF16)
        qb = jnp.where(first, 0.0, q).astype(BF16)
        cblk = c_ref[...]
        cta = _pick_lane(cblk, 2 * p) * LOG2E
        ctb = _pick_lane(cblk, 2 * p + 1) * LOG2E

        def step(j, carry, diagonal):
            ma, la, mb, lb, acc = carry
            off = pl.multiple_of(j * t, t)
            k = k_ref[pl.ds(off, t), :].astype(BF16)
            v = v_ref[pl.ds(off, t), :].astype(BF16)
            crow = ct_ref[:, pl.ds(off, t)] * LOG2E

            def one(qh, cth, hd, m_old, l_old):
                sc = lax.dot_general(qh, k, nt_dims, preferred_element_type=F32) - _pick_row(crow, hd)
                if diagonal:
                    keep = lax.broadcasted_iota(jnp.int32, (t, t), 0) >= lax.broadcasted_iota(jnp.int32, (t, t), 1)
                    sc = jnp.where(keep, sc, NEG)
                m_new = jnp.maximum(m_old, jnp.max(sc, axis=1, keepdims=True) + cth)
                pr = jnp.exp2(sc - (m_new - cth))
                alpha = jnp.exp2(m_old - m_new)
                l_new = alpha * l_old + jnp.sum(pr, axis=1, keepdims=True)
                pv = jnp.dot(pr.astype(BF16), v, preferred_element_type=F32)
                return m_new, l_new, alpha, pv

            ma2, la2, aa, pva = one(qa, cta, 2 * p, ma, la)
            mb2, lb2, ab, pvb = one(qb, ctb, 2 * p + 1, mb, lb)
            acc = jnp.where(first, aa * acc + pva, ab * acc + pvb)
            return ma2, la2, mb2, lb2, acc

        init = (jnp.full((t, 1), NEG, F32), jnp.zeros((t, 1), F32), jnp.full((t, 1), NEG, F32),
                jnp.zeros((t, 1), F32), jnp.zeros((t, LANES), F32))
        carry = lax.fori_loop(0, i, lambda j, carry: step(j, carry, False), init)
        ma, la, mb, lb, acc = step(i, carry, True)
        o_ref[...] = (acc / jnp.where(first, la, lb)).astype(BF16)
        lse_ref[0] = jnp.broadcast_to(ma + jnp.log2(la), (t, LANES))
        lse_ref[1] = jnp.broadcast_to(mb + jnp.log2(lb), (t, LANES))

    return pl.pallas_call(
        body, name=name, grid=(N_HEADS // 2, nq),
        in_specs=[pl.BlockSpec((t, LANES), lambda p, i: (i, qblk + p)),
                  pl.BlockSpec((s, LANES), lambda p, i: (0, qblk + 4 + p)),
                  pl.BlockSpec((s, LANES), lambda p, i: (0, qblk + 8 + p)),
                  pl.BlockSpec((t, LANES), lambda p, i: (i, 0)),
                  pl.BlockSpec((8, s), lambda p, i: (0, 0))],
        out_specs=[pl.BlockSpec((t, LANES), lambda p, i: (i, p)),
                   pl.BlockSpec((2, t, LANES), lambda p, i: (p, i, 0))],
        out_shape=[jax.ShapeDtypeStruct((s, D_ATT), BF16), jax.ShapeDtypeStruct((N_HEADS, s, LANES), F32)],
        compiler_params=_params("parallel", "arbitrary"),
    )(h, h, h, c, ct)


def _attention_bwd(h, c, ct, lse, att, datt, after, qblk, name):
    s = h.shape[0]
    t = _tile(s, ATT_TILE)
    nq = s // t
    scale = HEAD_DIM ** -0.5
    nt_dims = (((1,), (1,)), ((), ()))
    tn_dims = (((0,), (0,)), ((), ()))

    def body(q_ref, k_ref, v_ref, c_ref, ct_ref, lse_ref, o_ref, do_ref, after_ref, dq_ref, dk_ref, dv_ref, dct_ref, dcq_ref):
        p = pl.program_id(0)
        j = pl.program_id(1)
        lane = lax.broadcasted_iota(jnp.int32, (1, LANES), 1)
        first = lane < HEAD_DIM
        kf = k_ref[...]
        vf = v_ref[...]
        k = kf.astype(BF16)
        ka = jnp.where(first, kf, 0.0).astype(BF16)
        kb = jnp.where(first, 0.0, kf).astype(BF16)
        va = jnp.where(first, vf, 0.0).astype(BF16)
        vb = jnp.where(first, 0.0, vf).astype(BF16)
        crow = ct_ref[...] * LOG2E
        csa = _pick_row(crow, 2 * p)
        csb = _pick_row(crow, 2 * p + 1)

        @pl.when(j == 0)
        def _():
            dq_ref[...] = jnp.zeros_like(dq_ref)
            dcq_ref[...] = jnp.zeros_like(dcq_ref)

        def step(i, carry, diagonal):
            dka, dkb, dva, dvb, dca, dcb = carry
            off = pl.multiple_of(i * t, t)
            rows = pl.ds(off, t)
            q = (q_ref[rows, :] * (scale * LOG2E)).astype(BF16)
            dof = do_ref[rows, :]
            do = dof.astype(BF16)
            prod = dof * o_ref[rows, :].astype(F32)
            cblk = c_ref[rows, :] * LOG2E

            def one(kh, vh, hd, csh, lse_h):
                sc = lax.dot_general(q, kh, nt_dims, preferred_element_type=F32) - csh
                if diagonal:
                    keep = lax.broadcasted_iota(jnp.int32, (t, t), 0) >= lax.broadcasted_iota(jnp.int32, (t, t), 1)
                    sc = jnp.where(keep, sc, NEG)
                pr = jnp.exp2(sc - (jnp.max(lse_h, axis=1, keepdims=True) - _pick_lane(cblk, hd)))
                dp = lax.dot_general(do, vh, nt_dims, preferred_element_type=F32)
                return pr, dp

            pra, dpa = one(ka, va, 2 * p, csa, lse_ref[0, rows, :])
            prb, dpb = one(kb, vb, 2 * p + 1, csb, lse_ref[1, rows, :])
            dela = jnp.sum(jnp.where(first, prod, 0.0), axis=1, keepdims=True)
            delb = jnp.sum(jnp.where(first, 0.0, prod), axis=1, keepdims=True)
            dsa = pra * (dpa - dela)
            dsb = prb * (dpb - delb)
            dsa16 = dsa.astype(BF16)
            dsb16 = dsb.astype(BF16)
            dva = dva + lax.dot_general(pra.astype(BF16), do, tn_dims, preferred_element_type=F32)
            dvb = dvb + lax.dot_general(prb.astype(BF16), do, tn_dims, preferred_element_type=F32)
            dka = dka + lax.dot_general(dsa16, q, tn_dims, preferred_element_type=F32)
            dkb = dkb + lax.dot_general(dsb16, q, tn_dims, preferred_element_type=F32)
            dqa = jnp.dot(dsa16, k, preferred_element_type=F32)
            dqb = jnp.dot(dsb16, k, preferred_element_type=F32)
            dq_ref[rows, :] += scale * jnp.where(first, dqa, dqb)
            dca = dca - jnp.sum(dsa, axis=0, keepdims=True)
            dcb = dcb - jnp.sum(dsb, axis=0, keepdims=True)
            dcq_ref[rows, :] += jnp.where(lane == 0, jnp.sum(dsa, axis=1, keepdims=True),
                                          jnp.where(lane == 1, jnp.sum(dsb, axis=1, keepdims=True), 0.0))
            return dka, dkb, dva, dvb, dca, dcb

        z = jnp.zeros((t, LANES), F32)
        zr = jnp.zeros((1, t), F32)
        carry = step(j, (z, z, z, z, zr, zr), True)
        dka, dkb, dva, dvb, dca, dcb = lax.fori_loop(j + 1, nq, lambda i, carry: step(i, carry, False), carry)
        dk_ref[...] = jnp.where(first, dka, dkb) * (1.0 / LOG2E)
        dv_ref[...] = jnp.where(first, dva, dvb)
        sub = lax.broadcasted_iota(jnp.int32, (8, t), 0)
        dct_ref[...] = jnp.where(sub == 0, dca, jnp.where(sub == 1, dcb, 0.0))

    full = lambda blk: pl.BlockSpec((s, LANES), blk)
    return pl.pallas_call(
        body, name=name, grid=(N_HEADS // 2, nq),
        in_specs=[full(lambda p, j: (0, qblk + p)),
                  pl.BlockSpec((t, LANES), lambda p, j: (j, qblk + 4 + p)),
                  pl.BlockSpec((t, LANES), lambda p, j: (j, qblk + 8 + p)),
                  full(lambda p, j: (0, 0)),
                  pl.BlockSpec((8, t), lambda p, j: (0, j)),
                  pl.BlockSpec((2, s, LANES), lambda p, j: (p, 0, 0)),
                  full(lambda p, j: (0, p)),
                  full(lambda p, j: (0, p)),
                  pl.BlockSpec((8, LANES), lambda p, j: (0, 0))],
        out_specs=[full(lambda p, j: (0, p)),
                   pl.BlockSpec((t, LANES), lambda p, j: (j, p)),
                   pl.BlockSpec((t, LANES), lambda p, j: (j, p)),
                   pl.BlockSpec((None, 8, t), lambda p, j: (p, 0, j)),
                   pl.BlockSpec((None, s, LANES), lambda p, j: (p, 0, 0))],
        out_shape=[jax.ShapeDtypeStruct((s, D_ATT), F32), jax.ShapeDtypeStruct((s, D_ATT), F32),
                   jax.ShapeDtypeStruct((s, D_ATT), F32), jax.ShapeDtypeStruct((N_HEADS // 2, 8, s), F32),
                   jax.ShapeDtypeStruct((N_HEADS // 2, s, LANES), F32)],
        compiler_params=_params("arbitrary", "arbitrary"),
    )(h, h, h, c, ct, lse, att, datt, after)


def _sconv_fwd(h, w, bgblk, name):
    s = h.shape[0]
    nblk = D_CONV // LANES

    def body(bg_ref, cg_ref, hc_ref, w_ref, y_ref):
        rows = lax.broadcasted_iota(jnp.int32, (s, LANES), 0)
        y_ref[...] = (bg_ref[...] * _conv3(cg_ref[...] * hc_ref[...], w_ref, rows)).astype(BF16)

    col = lambda base: pl.BlockSpec((s, LANES), lambda j: (0, base + j))
    return pl.pallas_call(
        body, name=name, grid=(nblk,),
        in_specs=[col(bgblk), col(bgblk + nblk), col(bgblk + 2 * nblk), pl.BlockSpec((3, LANES), lambda j: (0, j))],
        out_specs=pl.BlockSpec((s, LANES), lambda j: (0, j)),
        out_shape=jax.ShapeDtypeStruct((s, D_CONV), BF16), compiler_params=_params("parallel"),
    )(h, h, h, w)


def _sconv_bwd(h, w, dy, bgblk, name):
    s = h.shape[0]
    nblk = D_CONV // LANES

    def body(bg_ref, cg_ref, hc_ref, w_ref, dy_ref, dbg_ref, dcg_ref, dhc_ref, dw_ref):
        rows = lax.broadcasted_iota(jnp.int32, (s, LANES), 0)
        cg, hc, dy, w = cg_ref[...], hc_ref[...], dy_ref[...], w_ref
        xin = cg * hc
        dbg_ref[...] = (dy * _conv3(xin, w, rows)).astype(BF16)
        dxin, dw_ref[...] = _conv3_bwd(dy * bg_ref[...], xin, w, rows)
        dcg_ref[...] = (dxin * hc).astype(BF16)
        dhc_ref[...] = (dxin * cg).astype(BF16)

    col = lambda base: pl.BlockSpec((s, LANES), lambda j: (0, base + j))
    return pl.pallas_call(
        body, name=name, grid=(nblk,),
        in_specs=[col(bgblk), col(bgblk + nblk), col(bgblk + 2 * nblk), pl.BlockSpec((3, LANES), lambda j: (0, j)), col(0)],
        out_specs=[col(0), col(0), col(0), pl.BlockSpec((8, LANES), lambda j: (0, j))],
        out_shape=[jax.ShapeDtypeStruct((s, D_CONV), BF16)] * 3 + [jax.ShapeDtypeStruct((8, D_CONV), F32)],
        compiler_params=_params("parallel"),
    )(h, h, h, w, dy)


def _sgu_group_masks():
    lane = lax.broadcasted_iota(jnp.int32, (1, D_SGU), 1)
    return [(lane // HEAD_DIM) == g for g in range(N_SGU_GROUPS)]


def _sgu_tril():
    r = lax.broadcasted_iota(jnp.int32, (SGU_CHUNK, SGU_CHUNK), 0)
    c = lax.broadcasted_iota(jnp.int32, (SGU_CHUNK, SGU_CHUNK), 1)
    return r >= c


def _sgu_fwd(h, ln_g, ln_b, w_s, b_full, ublk, name):
    s = h.shape[0]
    tr = _tile(s, 512)
    nch = tr // SGU_CHUNK

    def body(u_ref, v_ref, g_ref, b_ref, w_ref, bf_ref, y_ref):
        masks = _sgu_group_masks()
        tril = _sgu_tril()
        wm = [jnp.where(tril, w_ref[g], 0.0).astype(BF16) for g in range(N_SGU_GROUPS)]
        vn = _layer_norm(_gelu(v_ref[...]), g_ref[...], b_ref[...])
        for ch in range(nch):
            rows = pl.ds(ch * SGU_CHUNK, SGU_CHUNK)
            vc = vn[ch * SGU_CHUNK:(ch + 1) * SGU_CHUNK, :]
            mixed = bf_ref[...]
            for g in range(N_SGU_GROUPS):
                mixed = mixed + jnp.dot(wm[g], jnp.where(masks[g], vc, 0.0).astype(BF16), preferred_element_type=F32)
            y_ref[rows, :] = (_gelu(u_ref[rows, :]) * mixed).astype(BF16)

    row = lambda blk: pl.BlockSpec((tr, D_SGU), lambda i: (i, blk))
    vec = pl.BlockSpec((1, D_SGU), lambda i: (0, 0))
    return pl.pallas_call(
        body, name=name, grid=(s // tr,),
        in_specs=[row(ublk), row(ublk + 1), vec, vec,
                  pl.BlockSpec((N_SGU_GROUPS, SGU_CHUNK, SGU_CHUNK), lambda i: (0, 0, 0)),
                  pl.BlockSpec((SGU_CHUNK, D_SGU), lambda i: (0, 0))],
        out_specs=row(0), out_shape=jax.ShapeDtypeStruct((s, D_SGU), BF16), compiler_params=_params("parallel"),
    )(h, h, ln_g, ln_b, w_s, b_full)


def _sgu_bwd(h, ln_g, ln_b, w_s, b_full, dy, ublk, name):
    s = h.shape[0]
    tr = _tile(s, 512)
    nch = tr // SGU_CHUNK
    nt_dims = (((1,), (1,)), ((), ()))

    def norm(v, g, b):
        return _layer_norm(_gelu(v), g, b)

    def body(u_ref, v_ref, g_ref, b_ref, w_ref, bf_ref, dy_ref, du_ref, dv_ref, dg_ref, db_ref, dw_ref, dbf_ref):
        masks = _sgu_group_masks()
        tril = _sgu_tril()
        wf = [jnp.where(tril, w_ref[g], 0.0) for g in range(N_SGU_GROUPS)]
        wm = [w.astype(BF16) for w in wf]
        wmt = [jnp.transpose(w).astype(BF16) for w in wf]
        vn, vjp = jax.vjp(norm, v_ref[...], g_ref[...], b_ref[...])

        @pl.when(pl.program_id(0) == 0)
        def _():
            dg_ref[...] = jnp.zeros_like(dg_ref)
            db_ref[...] = jnp.zeros_like(db_ref)
            dw_ref[...] = jnp.zeros_like(dw_ref)
            dbf_ref[...] = jnp.zeros_like(dbf_ref)

        dvn_parts = []
        for ch in range(nch):
            rows = pl.ds(ch * SGU_CHUNK, SGU_CHUNK)
            vc = vn[ch * SGU_CHUNK:(ch + 1) * SGU_CHUNK, :]
            vc16 = vc.astype(BF16)
            mixed = bf_ref[...]
            for g in range(N_SGU_GROUPS):
                mixed = mixed + jnp.dot(wm[g], jnp.where(masks[g], vc, 0.0).astype(BF16), preferred_element_type=F32)
            dy = dy_ref[rows, :]
            ug, slope = _gelu_and_slope(u_ref[rows, :])
            du_ref[rows, :] = (dy * mixed * slope).astype(BF16)
            dmixed = dy * ug
            dbf_ref[...] += dmixed
            dvc = jnp.zeros((SGU_CHUNK, D_SGU), F32)
            for g in range(N_SGU_GROUPS):
                dm16 = jnp.where(masks[g], dmixed, 0.0).astype(BF16)
                dw_ref[g] += jnp.where(tril, lax.dot_general(dm16, vc16, nt_dims, preferred_element_type=F32), 0.0)
                dvc = dvc + jnp.dot(wmt[g], dm16, preferred_element_type=F32)
            dvn_parts.append(dvc)
        dv, dg, db = vjp(jnp.concatenate(dvn_parts, axis=0))
        dv_ref[...] = dv.astype(BF16)
        dg_ref[...] += dg
        db_ref[...] += db

    row = lambda blk: pl.BlockSpec((tr, D_SGU), lambda i: (i, blk))
    vec = pl.BlockSpec((1, D_SGU), lambda i: (0, 0))
    wsp = pl.BlockSpec((N_SGU_GROUPS, SGU_CHUNK, SGU_CHUNK), lambda i: (0, 0, 0))
    bsp = pl.BlockSpec((SGU_CHUNK, D_SGU), lambda i: (0, 0))
    return pl.pallas_call(
        body, name=name, grid=(s // tr,),
        in_specs=[row(ublk), row(ublk + 1), vec, vec, wsp, bsp, row(0)],
        out_specs=[row(0), row(0), vec, vec, wsp, bsp],
        out_shape=[jax.ShapeDtypeStruct((s, D_SGU), BF16), jax.ShapeDtypeStruct((s, D_SGU), BF16),
                   jax.ShapeDtypeStruct((1, D_SGU), F32), jax.ShapeDtypeStruct((1, D_SGU), F32),
                   jax.ShapeDtypeStruct((N_SGU_GROUPS, SGU_CHUNK, SGU_CHUNK), F32),
                   jax.ShapeDtypeStruct((SGU_CHUNK, D_SGU), F32)],
        compiler_params=_params("arbitrary"),
    )(h, h, ln_g, ln_b, w_s, b_full, dy)


def _merge_fwd(h, b_gate, att, yc, ys, wa, wc, ws, w_out, x, g_post, g_next, name):
    s, d = att.shape[0], wa.shape[1]
    tm = _tile(s, 512)

    def body(g0_ref, g1_ref, g2_ref, bg_ref, a_ref, c_ref, s_ref, wa_ref, wc_ref, ws_ref, wo_ref, x_ref, gp_ref, gn_ref,
             m_ref, o_ref, x1_ref, xn_ref):
        acc = jax.nn.sigmoid(g0_ref[...] + bg_ref[0:1, :]) * jnp.dot(a_ref[...], wa_ref[...], preferred_element_type=F32)
        acc += jax.nn.sigmoid(g1_ref[...] + bg_ref[1:2, :]) * jnp.dot(c_ref[...], wc_ref[...], preferred_element_type=F32)
        acc += jax.nn.sigmoid(g2_ref[...] + bg_ref[2:3, :]) * jnp.dot(s_ref[...], ws_ref[...], preferred_element_type=F32)
        merged = acc.astype(BF16)
        m_ref[...] = merged
        o = jnp.dot(merged, wo_ref[...], preferred_element_type=F32)
        o_ref[...] = o
        x1 = x_ref[...] + _rms(o, gp_ref[...])
        x1_ref[...] = x1
        xn_ref[...] = _rms(x1, gn_ref[...]).astype(BF16)

    gate = lambda b: pl.BlockSpec((tm, d), lambda i: (i, b))
    act = lambda k: pl.BlockSpec((tm, k), lambda i: (i, 0))
    wgt = lambda k: pl.BlockSpec((k, d), lambda i: (0, 0), pipeline_mode=pl.Buffered(1))
    vec = pl.BlockSpec((1, d), lambda i: (0, 0))
    return pl.pallas_call(
        body, name=name, grid=(s // tm,),
        in_specs=[gate(0), gate(1), gate(2), pl.BlockSpec((3, d), lambda i: (0, 0)),
                  act(D_ATT), act(D_CONV), act(D_SGU), wgt(D_ATT), wgt(D_CONV), wgt(D_SGU), wgt(d), act(d), vec, vec],
        out_specs=[act(d)] * 4,
        out_shape=[jax.ShapeDtypeStruct((s, d), BF16), jax.ShapeDtypeStruct((s, d), F32), jax.ShapeDtypeStruct((s, d), F32),
                   jax.ShapeDtypeStruct((s, d), BF16)],
        compiler_params=_params("parallel"),
    )(h, h, h, b_gate, att, yc, ys, wa, wc, ws, w_out, x, g_post, g_next)


def _merge_bwd(h, b_gate, att, yc, ys, wa, wc, ws, dm, name):
    s, d = att.shape[0], wa.shape[1]
    tm = _tile(s, 512)
    nt_dims, tn_dims = (((1,), (1,)), ((), ())), (((0,), (0,)), ((), ()))
    widths = (D_ATT, D_CONV, D_SGU)

    def body(g0_ref, g1_ref, g2_ref, bg_ref, a_ref, c_ref, s_ref, wa_ref, wc_ref, ws_ref, dm_ref,
             da_ref, dc_ref, ds_ref, ga_ref, gc_ref, gs_ref, dgl_ref, dbg_ref, acc_a, acc_c, acc_s):
        i = pl.program_id(0)

        @pl.when(i == 0)
        def _():
            dbg_ref[...] = jnp.zeros_like(dbg_ref)
            for acc in (acc_a, acc_c, acc_s):
                acc[...] = jnp.zeros_like(acc)

        dm = dm_ref[...]
        sums = []
        for b, (g_ref, x_ref, w_ref, dx_ref, acc) in enumerate((
                (g0_ref, a_ref, wa_ref, da_ref, acc_a), (g1_ref, c_ref, wc_ref, dc_ref, acc_c),
                (g2_ref, s_ref, ws_ref, ds_ref, acc_s))):
            gate = jax.nn.sigmoid(g_ref[...] + bg_ref[b:b + 1, :])
            x, w = x_ref[...], w_ref[...]
            y = jnp.dot(x, w, preferred_element_type=F32)
            dy = (dm * gate).astype(BF16)
            dgl = dm * y * gate * (1.0 - gate)
            dgl_ref[b] = dgl.astype(BF16)
            sums.append(jnp.sum(dgl, axis=0, keepdims=True))
            dx_ref[...] = lax.dot_general(dy, w, nt_dims, preferred_element_type=F32)
            acc[...] += lax.dot_general(x, dy, tn_dims, preferred_element_type=F32)
        sub = lax.broadcasted_iota(jnp.int32, (3, d), 0)
        dbg_ref[...] += jnp.where(sub == 0, sums[0], jnp.where(sub == 1, sums[1], sums[2]))

        @pl.when(i == pl.num_programs(0) - 1)
        def _():
            for g_out, acc in ((ga_ref, acc_a), (gc_ref, acc_c), (gs_ref, acc_s)):
                g_out[...] = acc[...].astype(BF16)

    gate = lambda b: pl.BlockSpec((tm, d), lambda i: (i, b))
    act = lambda k: pl.BlockSpec((tm, k), lambda i: (i, 0))
    wgt = lambda k: pl.BlockSpec((k, d), lambda i: (0, 0), pipeline_mode=pl.Buffered(1))
    res = pl.pallas_call(
        body, name=name, grid=(s // tm,),
        in_specs=[gate(0), gate(1), gate(2), pl.BlockSpec((3, d), lambda i: (0, 0)),
                  act(D_ATT), act(D_CONV), act(D_SGU), wgt(D_ATT), wgt(D_CONV), wgt(D_SGU), act(d)],
        out_specs=[act(k) for k in widths] + [pl.BlockSpec((k, d), lambda i: (0, 0)) for k in widths]
        + [pl.BlockSpec((3, tm, d), lambda i: (0, i, 0)), pl.BlockSpec((3, d), lambda i: (0, 0))],
        out_shape=[jax.ShapeDtypeStruct((s, k), F32) for k in widths] + [jax.ShapeDtypeStruct((k, d), BF16) for k in widths]
        + [jax.ShapeDtypeStruct((3, s, d), BF16), jax.ShapeDtypeStruct((3, d), F32)],
        scratch_shapes=[pltpu.VMEM((k, d), F32) for k in widths],
        compiler_params=_params("arbitrary"),
    )(h, h, h, b_gate, att, yc, ys, wa, wc, ws, dm)
    return res[0:3], res[3:6], res[6], res[7]


def _ffn_act_fwd(hh, cw, name):
    s, dff = hh.shape[0], hh.shape[1] // 2
    nblk = dff // LANES

    def body(a_ref, b_ref, wa_ref, wb_ref, z_ref):
        rows = lax.broadcasted_iota(jnp.int32, (s, LANES), 0)
        z_ref[...] = (_gelu(_conv3(a_ref[...], wa_ref, rows)) * _conv3(b_ref[...], wb_ref, rows)).astype(BF16)

    col = lambda base: pl.BlockSpec((s, LANES), lambda j: (0, base + j))
    wsp = lambda base: pl.BlockSpec((3, LANES), lambda j: (0, base + j))
    return pl.pallas_call(
        body, name=name, grid=(nblk,), in_specs=[col(0), col(nblk), wsp(0), wsp(nblk)], out_specs=col(0),
        out_shape=jax.ShapeDtypeStruct((s, dff), BF16), compiler_params=_params("parallel"),
    )(hh, hh, cw, cw)


def _ffn_act_bwd(hh, cw, dz, name):
    s, dff = hh.shape[0], hh.shape[1] // 2
    nblk = dff // LANES

    def body(a_ref, b_ref, wa_ref, wb_ref, dz_ref, da_ref, db_ref, dwa_ref, dwb_ref):
        rows = lax.broadcasted_iota(jnp.int32, (s, LANES), 0)
        a, b, dz = a_ref[...], b_ref[...], dz_ref[...]
        ga, slope = _gelu_and_slope(_conv3(a, wa_ref, rows))
        da, dwa_ref[...] = _conv3_bwd(dz * _conv3(b, wb_ref, rows) * slope, a, wa_ref, rows)
        db, dwb_ref[...] = _conv3_bwd(dz * ga, b, wb_ref, rows)
        da_ref[...] = da.astype(BF16)
        db_ref[...] = db.astype(BF16)

    col = lambda base: pl.BlockSpec((s, LANES), lambda j: (0, base + j))
    wsp = lambda base: pl.BlockSpec((3, LANES), lambda j: (0, base + j))
    w8 = lambda base: pl.BlockSpec((8, LANES), lambda j: (0, base + j))
    return pl.pallas_call(
        body, name=name, grid=(nblk,), in_specs=[col(0), col(nblk), wsp(0), wsp(nblk), col(0)],
        out_specs=[col(0), col(0), w8(0), w8(0)],
        out_shape=[jax.ShapeDtypeStruct((s, dff), BF16)] * 2 + [jax.ShapeDtypeStruct((8, dff), F32)] * 2,
        compiler_params=_params("parallel"),
    )(hh, hh, cw, cw, dz)


ANY = pl.BlockSpec(memory_space=pl.ANY)


def _place():
    return lax.axis_index("x"), lax.axis_index("y"), lax.axis_index("c")


def _all_gather(arrs, name):
    n = len(arrs)

    def body(*refs):
        ins, outs = refs[:n], refs[n:2 * n]
        send_sems, recv_sems, local_sems = refs[2 * n:]
        x, y, c = _place()
        me, sibling = (x, y, c), (x, y, 1 - c)
        chips = [(1 - x, y), (x, 1 - y), (1 - x, 1 - y)]

        def slab(a, dev):
            return outs[a].at[4 * dev[0] + 2 * dev[1] + dev[2]]

        def copy(a, k, block, to, src=None):
            return pltpu.make_async_remote_copy(
                src_ref=slab(a, block) if src is None else src, dst_ref=slab(a, block),
                send_sem=send_sems.at[7 * a + k], recv_sem=recv_sems.at[7 * a + k], device_id=to, device_id_type=MESH)

        mine = [pltpu.make_async_copy(ins[a], slab(a, me), local_sems.at[a]) for a in range(n)]
        for cp in mine:
            cp.start()
        first = []
        for a in range(n):
            first.append(copy(a, 0, me, sibling, src=ins[a]))
            first += [copy(a, 1 + j, me, (*chip, c), src=ins[a]) for j, chip in enumerate(chips)]
        for cp in first:
            cp.start()
        passed = []
        for a in range(n):
            for j, chip in enumerate(chips):
                copy(a, 1 + j, (*chip, c), me).wait_recv()
                fwd = copy(a, 4 + j, (*chip, c), sibling)
                fwd.start()
                passed.append(fwd)
        for a in range(n):
            copy(a, 0, sibling, me).wait_recv()
            for j, chip in enumerate(chips):
                copy(a, 4 + j, (*chip, 1 - c), me).wait_recv()
        for cp in first + passed:
            cp.wait_send()
        for cp in mine:
            cp.wait()

    return pl.pallas_call(
        body, name=name, in_specs=[ANY] * n, out_specs=[ANY] * n,
        out_shape=[jax.ShapeDtypeStruct((N_DEV,) + a.shape, a.dtype) for a in arrs],
        scratch_shapes=[pltpu.SemaphoreType.DMA((7 * n,)), pltpu.SemaphoreType.DMA((7 * n,)), pltpu.SemaphoreType.DMA((n,))],
    )(*arrs)


HBM = pl.BlockSpec(memory_space=pltpu.HBM)
SEM = pl.BlockSpec(memory_space=pltpu.SEMAPHORE)
EFFECT = pltpu.SideEffectType.DATAFLOW_SIDE_EFFECTING


def _slot(dev):
    return 4 * dev[0] + 2 * dev[1] + dev[2]


def _exchange_copies(src_refs, land_refs, send_sems, recv_sems, src_view, land_view):
    x, y, c = _place()
    me = (x, y, c)
    peers = [(1 - x if r & 4 else x, 1 - y if r & 2 else y, 1 - c if r & 1 else c) for r in range(1, N_DEV)]
    sends, lands = [], []
    for a, (src, land) in enumerate(zip(src_refs, land_refs)):
        for k, peer in enumerate(peers):
            sems = dict(send_sem=send_sems.at[7 * a + k], recv_sem=recv_sems.at[7 * a + k], device_id=peer,
                        device_id_type=MESH)
            sends.append(pltpu.make_async_remote_copy(src_ref=src_view(src, _slot(peer)),
                                                      dst_ref=land_view(land, _slot(me)), **sems))
            lands.append(pltpu.make_async_remote_copy(src_ref=src_view(src, _slot(me)),
                                                      dst_ref=land_view(land, _slot(peer)), **sems))
    return sends, lands


def _own_copies(src_refs, land_refs, local_sems, src_view, land_view):
    me = _slot(_place())
    return [pltpu.make_async_copy(src_view(src, me), land_view(land, me), local_sems.at[a])
            for a, (src, land) in enumerate(zip(src_refs, land_refs))]


def _exchange_start(srcs, lands, after, src_view, land_view, name):
    n = len(srcs)

    def body(*refs):
        src_refs, land_refs = refs[:n], refs[n:2 * n]
        send_sems, recv_sems, local_sems = refs[2 * n + 1:2 * n + 4]
        token = refs[-1]
        sends, _ = _exchange_copies(src_refs, land_refs, send_sems, recv_sems, src_view, land_view)
        for cp in sends + _own_copies(src_refs, land_refs, local_sems, src_view, land_view):
            cp.start()
        token[...] = jnp.zeros_like(token)

    thru = [pltpu.HBM(a.shape, a.dtype) for a in list(srcs) + list(lands)]
    outs = pl.pallas_call(
        body, name=name,
        out_shape=(pltpu.SemaphoreType.DMA((7 * n,)), pltpu.SemaphoreType.DMA((7 * n,)), pltpu.SemaphoreType.DMA((n,)),
                   *thru, jax.ShapeDtypeStruct((8, LANES), F32)),
        in_specs=[HBM] * (2 * n) + [ANY],
        out_specs=(SEM, SEM, SEM, *([HBM] * (2 * n)), pl.BlockSpec(memory_space=pltpu.VMEM)),
        input_output_aliases={i: 3 + i for i in range(2 * n)},
        compiler_params=pltpu.CompilerParams(has_side_effects=EFFECT),
    )(*[pltpu.with_memory_space_constraint(a, pltpu.HBM) for a in list(srcs) + list(lands)], after)
    return outs[:3], list(outs[3:3 + n]), list(outs[3 + n:3 + 2 * n]), outs[-1]


def _exchange_wait(sems, srcs, lands, after, src_view, land_view, name):
    n = len(srcs)

    def body(*refs):
        src_refs, land_refs = refs[:n], refs[n:2 * n]
        send_sems, recv_sems, local_sems = refs[2 * n:2 * n + 3]
        sends, landed = _exchange_copies(src_refs, land_refs, send_sems, recv_sems, src_view, land_view)
        for cp in sends:
            cp.wait_send()
        for cp in landed:
            cp.wait_recv()
        for cp in _own_copies(src_refs, land_refs, local_sems, src_view, land_view):
            cp.wait()

    after = after if isinstance(after, (list, tuple)) else [after]
    outs = pl.pallas_call(
        body, name=name, out_shape=[pltpu.HBM(a.shape, a.dtype) for a in list(srcs) + list(lands)],
        in_specs=[HBM] * (2 * n) + [SEM, SEM, SEM] + [ANY] * len(after), out_specs=[HBM] * (2 * n),
        input_output_aliases={i: i for i in range(2 * n)},
        compiler_params=pltpu.CompilerParams(has_side_effects=EFFECT),
    )(*srcs, *lands, *sems, *after)
    return list(outs[:n]), list(outs[n:])


ADAMW_BLOCK_BYTES = 1 << 19
PACK_ROWS = 256


def _adamw(slabs, w, m, v, name):
    nl, r, c = w.shape
    row_edges = [r] + [t for t in range(8, r, 8) if r % t == 0]
    col_edges = [c] + [t for t in range(LANES, c, LANES) if c % t == 0]
    fits = [(tr * tc, tc, tr) for tr in row_edges for tc in col_edges if tr * tc * 4 <= ADAMW_BLOCK_BYTES]
    _, tc, tr = max(fits) if fits else (0, min(col_edges), min(row_edges))

    def body(s_ref, w_ref, m_ref, v_ref, g_ref, d_ref, nm_ref, nv_ref):
        g = s_ref[0].astype(F32)
        for q in range(1, N_DEV):
            g = g + s_ref[q].astype(F32)
        m_new = ADAM_B1 * m_ref[...] + (1.0 - ADAM_B1) * g
        v_new = ADAM_B2 * v_ref[...] + (1.0 - ADAM_B2) * (g * g)
        m_hat = m_new / (1.0 - ADAM_B1 ** ADAM_STEP)
        v_hat = v_new / (1.0 - ADAM_B2 ** ADAM_STEP)
        g_ref[...] = g
        d_ref[...] = -ADAM_LR * (m_hat / (jnp.sqrt(v_hat) + ADAM_EPS) + ADAM_WD * w_ref[...])
        nm_ref[...] = m_new
        nv_ref[...] = v_new

    blk = pl.BlockSpec((None, tr, tc), lambda l, i, j: (l, i, j))
    return pl.pallas_call(
        body, name=name, grid=(nl, r // tr, c // tc),
        in_specs=[pl.BlockSpec((N_DEV, None, tr, tc), lambda l, i, j: (0, l, i, j)), blk, blk, blk],
        out_specs=[blk] * 4, out_shape=[jax.ShapeDtypeStruct(w.shape, F32)] * 4,
        compiler_params=_params("parallel", "parallel", "parallel"),
    )(slabs, w, m, v)


def _layout(d):
    off = {"gate": 0, "q": 3 * d}
    off["bg"] = off["q"] + 3 * D_ATT
    off["u"] = off["bg"] + 3 * D_CONV
    off["f"] = off["u"] + 2 * D_SGU
    width = -(-(off["f"] + LANES) // 512) * 512
    return off, width


def _pad_w_in(wt, d, token):
    off, width = _layout(d)
    nqkv, nrest = 3 * D_ATT, 3 * D_CONV + 2 * D_SGU
    pad = jnp.zeros((width - off["f"] - N_HEADS, wt.shape[1]), wt.dtype) + token[0, 0].astype(wt.dtype)
    return jnp.concatenate([wt[nqkv + N_HEADS + nrest:], wt[:nqkv], wt[nqkv + N_HEADS:nqkv + N_HEADS + nrest],
                            wt[nqkv:nqkv + N_HEADS], pad], axis=0)


def _unpad_w_in(wtp, d):
    off, _ = _layout(d)
    return jnp.concatenate([wtp[off["q"]:off["bg"]], wtp[off["f"]:off["f"] + N_HEADS], wtp[off["bg"]:off["f"]],
                            wtp[:off["q"]]], axis=0)


def _cols_from_slabs(g):
    return jnp.transpose(g, (1, 0, 2)).reshape(g.shape[1], N_DEV * g.shape[2])


def _cols_to_slabs(w):
    r, c = w.shape[0], w.shape[1] // N_DEV
    return jnp.transpose(w.reshape(r, N_DEV, c), (1, 0, 2))


def kernel(x, pre_mix_g, post_mix_g, pre_ffn_g, post_ffn_g, w_in, b_forget, b_gate, conv_mix_w, sgu_ln_g, sgu_ln_b, sgu_w, sgu_b, w_branch_att, w_branch_conv, w_branch_sgu, w_out, w_ffn_up, conv_ffn_w, w_ffn_down, loss_target, m_pre_mix_g, m_post_mix_g, m_pre_ffn_g, m_post_ffn_g, m_w_in, m_b_forget, m_b_gate, m_conv_mix_w, m_sgu_ln_g, m_sgu_ln_b, m_sgu_w, m_sgu_b, m_w_branch_att, m_w_branch_conv, m_w_branch_sgu, m_w_out, m_w_ffn_up, m_conv_ffn_w, m_w_ffn_down, v_pre_mix_g, v_post_mix_g, v_pre_ffn_g, v_post_ffn_g, v_w_in, v_b_forget, v_b_gate, v_conv_mix_w, v_sgu_ln_g, v_sgu_ln_b, v_sgu_w, v_sgu_b, v_w_branch_att, v_w_branch_conv, v_w_branch_sgu, v_w_out, v_w_ffn_up, v_conv_ffn_w, v_w_ffn_down):
    depth = w_in.shape[0]
    s, d = x.shape[1], x.shape[2]
    dff = w_ffn_down.shape[1] * N_DEV
    off, _ = _layout(d)
    qblk, bgblk, ublk, fblk = off["q"] // LANES, off["bg"] // LANES, off["u"] // D_SGU, off["f"] // LANES
    x0 = x.reshape(s, d)
    target = loss_target.reshape(s, d)
    ncm, ncf = conv_mix_w.shape[2], conv_ffn_w.shape[2]

    lo = d // N_DEV

    whole = lambda ref, slot: ref
    slab = lambda ref, slot: ref.at[slot]

    w_in_t, w_up_t = jnp.transpose(w_in, (0, 2, 1)), jnp.transpose(w_ffn_up, (0, 2, 1))

    def shards_of(l, part):
        if part == "mix":
            small = jnp.concatenate([b_gate[l], conv_mix_w[l], conv_ffn_w[l]], axis=1)
            return [w_in_t[l].astype(BF16), w_branch_att[l].astype(BF16), w_branch_conv[l].astype(BF16),
                    w_branch_sgu[l].astype(BF16), w_out[l].astype(BF16), small]
        return [w_up_t[l].astype(BF16), w_ffn_down[l].astype(BF16)]

    def gather_start(l, part, after):
        shards = shards_of(l, part)
        lands = [lax.empty((N_DEV,) + a.shape, a.dtype) for a in shards]
        return _exchange_start(shards, lands, after, whole, slab, name=f"gather_start_{part}_{l}")

    def gather_finish(l, part, started, after):
        sems, shards, lands, _ = started[part]
        shards, lands = _exchange_wait(sems, shards, lands, after, whole, slab, name=f"gather_wait_{part}_{l}")
        token = jnp.zeros((8, LANES), F32)
        if l + 1 < depth:
            started[part] = gather_start(l + 1, part, lands[0])
            token = started[part][3]
        return lands, token

    def bfull(l):
        return jnp.repeat(jnp.transpose(sgu_b[l]), HEAD_DIM, axis=1)

    def bf_pad(l):
        return jnp.pad(b_forget[l], (0, LANES - N_HEADS)).reshape(1, LANES)

    saved = []
    weights = []
    xin = x0
    xn = _prenorm(x0, pre_mix_g[0:1], name="prenorm_first")
    loss_acc = dy = None
    first = _all_gather(shards_of(0, "mix"), name="gather_first")
    started = {"ffn": gather_start(0, "ffn", first[0])}
    for l in range(depth):
        if l == 0:
            (g_in, g_a, g_c, g_s, g_o, g_small), token = first, started["ffn"][3]
        else:
            (g_in, g_a, g_c, g_s, g_o, g_small), token = gather_finish(l, "mix", started, xin)
        g_small = _cols_from_slabs(g_small).reshape(3, N_DEV, -1)
        w = dict(w_in=_pad_w_in(g_in.reshape(N_DEV * g_in.shape[1], d), d, token), wa=_cols_from_slabs(g_a),
                 wc=_cols_from_slabs(g_c), ws=_cols_from_slabs(g_s), w_out=g_o.reshape(d, d),
                 b_gate=g_small[:, :, :lo].reshape(3, d), cmw=g_small[:, :, lo:lo + ncm].reshape(3, D_CONV),
                 cfw=g_small[:, :, lo + ncm:].reshape(3, 2 * dff))
        weights.append(w)
        h = _mm(xn, w["w_in"], "nt", F32, name="proj_in")
        c, ct = _forget_prep(h, bf_pad(l), fblk, name="forget_prep")
        att, lse = _attention_fwd(h, c, ct, qblk, name="attention_fwd")
        if l == 0 and depth > 1:
            started["mix"] = gather_start(1, "mix", att)
            w["cmw"] = w["cmw"] + started["mix"][3][0, 0]
        yc = _sconv_fwd(h, w["cmw"], bgblk, name="sconv_fwd")
        ys = _sgu_fwd(h, sgu_ln_g[l:l + 1], sgu_ln_b[l:l + 1], sgu_w[l], bfull(l), ublk, name="sgu_fwd")
        (g_up, g_dn), token = gather_finish(l, "ffn", started, ys)
        w["w_up"], w["w_dn"] = g_up.reshape(2 * dff, d), g_dn.reshape(dff, d)
        merged, o, x1, xn2 = _merge_fwd(h, w["b_gate"], att, yc, ys, w["wa"], w["wc"], w["ws"], w["w_out"], xin,
                                        post_mix_g[l:l + 1], pre_ffn_g[l:l + 1] + token[0, 0], name="merge_proj_out")
        hh = _mm(xn2, w["w_up"], "nt", F32, name="ffn_up")
        z = _ffn_act_fwd(hh, w["cfw"], name="ffn_act_fwd")
        layer = dict(xin=xin, xn=xn, h=h, c=c, ct=ct, lse=lse, att=att, yc=yc, ys=ys, merged=merged, o=o, x1=x1,
                     xn2=xn2, hh=hh, z=z)
        if l + 1 < depth:
            layer["f"], xin, xn = _mm_postnorm(z, w["w_dn"], x1, post_ffn_g[l:l + 1], pre_mix_g[l + 1:l + 2],
                                               name="ffn_down")
        else:
            layer["f"], dy, loss_acc = _mm_postnorm_loss(z, w["w_dn"], x1, post_ffn_g[l:l + 1], target,
                                                         name="ffn_down_loss")
        saved.append(layer)
    loss = lax.psum(loss_acc[0, 0] * (0.5 / d), ("x", "y", "c"))

    rep = {k: [None] * depth for k in ("pre_mix_g", "post_mix_g", "pre_ffn_g", "post_ffn_g", "b_forget", "sgu_ln_g",
                                       "sgu_ln_b", "sgu_w", "sgu_b")}
    nsmall = lo + ncm + ncf
    lands = {"win": [lax.empty((N_DEV, depth) + w_in_t.shape[1:], BF16)],
             "mid": [lax.empty((N_DEV, depth) + shp, dt) for shp, dt in (
                 (w_branch_att.shape[1:], BF16), (w_branch_conv.shape[1:], BF16), (w_branch_sgu.shape[1:], BF16),
                 (w_out.shape[1:], BF16), ((3, nsmall), F32))],
             "ffn": [lax.empty((N_DEV, depth) + shp, BF16) for shp in (w_up_t.shape[1:], w_ffn_down.shape[1:])]}
    scatters = {part: [None] * depth for part in lands}

    def scatter_start(l, part, sends, after):
        layer_slab = lambda ref, slot: ref.at[slot, l]
        sems, sends, lands[part], token = _exchange_start(sends, lands[part], after, slab, layer_slab,
                                                          name=f"scatter_start_{part}_{l}")
        scatters[part][l] = (sems, sends, layer_slab)
        return token

    def scatter_finish(part, after):
        for l in range(depth):
            sems, sends, layer_slab = scatters[part][l]
            _, lands[part] = _exchange_wait(sems, sends, lands[part], after, slab, layer_slab,
                                            name=f"scatter_wait_{part}_{l}")
        return lands[part]

    token = jnp.zeros((8, LANES), F32)
    dx = dy
    for l in reversed(range(depth)):
        w, a = weights[l], saved[l]
        df, rep["post_ffn_g"][l] = _postnorm_bwd(a["f"], post_ffn_g[l:l + 1] + token[0, 0], dx, name="postnorm_bwd")
        dz = _mm(df, w["w_dn"], "nt", F32, name="ffn_down_dx")
        g_dn = _mm(a["z"], df, "tn", BF16, name="ffn_down_dw")
        dha, dhb, dcwa, dcwb = _ffn_act_bwd(a["hh"], w["cfw"], dz, name="ffn_act_bwd")
        dhh = jnp.concatenate([dha, dhb], axis=1)
        dcfw = jnp.concatenate([dcwa[0:3], dcwb[0:3]], axis=1)
        g_up = _mm(dhh, a["xn2"], "tn", BF16, name="ffn_up_dw")
        token = scatter_start(l, "ffn", [g_up.reshape(N_DEV, 2 * dff // N_DEV, d), g_dn.reshape(N_DEV, dff // N_DEV, d)],
                              dz)
        dx1, rep["pre_ffn_g"][l] = _mm_prenorm_bwd(dhh, w["w_up"], a["x1"], pre_ffn_g[l:l + 1], dx, token, name="ffn_up_dx")
        do, rep["post_mix_g"][l], dmerged = _postnorm_bwd_mm(a["o"], post_mix_g[l:l + 1], dx1, w["w_out"], name="proj_out_dx")
        g_o = _mm(a["merged"], do, "tn", BF16, name="proj_out_dw")
        (datt, dconv, dsgu), (g_a, g_c, g_s), dgl, dbg = _merge_bwd(
            a["h"], w["b_gate"], a["att"], a["yc"], a["ys"], w["wa"], w["wc"], w["ws"], dmerged, name="merge_bwd")
        dbgate, dcg, dhc, dcmw = _sconv_bwd(a["h"], w["cmw"], dconv, bgblk, name="sconv_bwd")
        sends = [_cols_to_slabs(g_a), _cols_to_slabs(g_c), _cols_to_slabs(g_s), g_o.reshape(N_DEV, d // N_DEV, d),
                 jnp.concatenate([_cols_to_slabs(dbg), _cols_to_slabs(dcmw[0:3]), _cols_to_slabs(dcfw)], axis=2)]
        token = scatter_start(l, "mid", sends, dx1)
        dq, dk, dv, dct4, dcq4 = _attention_bwd(a["h"], a["c"], a["ct"], a["lse"], a["att"], datt, token, qblk,
                                                name="attention_bwd")
        dfl, dbf = _forget_prep_bwd(a["h"], bf_pad(l), dct4, dcq4, fblk, name="forget_prep_bwd")
        rep["b_forget"][l] = dbf[0, :N_HEADS]
        du, dvs, dlg, dlb, dsw, dbfull = _sgu_bwd(a["h"], sgu_ln_g[l:l + 1], sgu_ln_b[l:l + 1], sgu_w[l], bfull(l), dsgu,
                                                  ublk, name="sgu_bwd")
        rep["sgu_ln_g"][l], rep["sgu_ln_b"][l], rep["sgu_w"][l] = dlg, dlb, dsw
        rep["sgu_b"][l] = jnp.transpose(jnp.sum(dbfull.reshape(SGU_CHUNK, N_SGU_GROUPS, HEAD_DIM), axis=2))
        dh = jnp.concatenate([dgl[0], dgl[1], dgl[2], dq.astype(BF16), dk.astype(BF16), dv.astype(BF16), dbgate, dcg, dhc,
                              du, dvs, dfl.astype(BF16), jnp.zeros((s, w["w_in"].shape[0] - off["f"] - LANES), BF16)], axis=1)
        g_in = _mm(dh, a["xn"], "tn", BF16, name="proj_in_dw")
        token = scatter_start(l, "win", [_unpad_w_in(g_in, d).reshape(N_DEV, -1, d)], dx1)
        dx, rep["pre_mix_g"][l] = _mm_prenorm_bwd(dh, w["w_in"], a["xin"], pre_mix_g[l:l + 1], dx1, token, name="proj_in_dx")

    outs = {}
    t3 = lambda arr: jnp.transpose(arr, (0, 2, 1))

    def update(name_, slabs, w_, m_, v_, transposed=False):
        if transposed:
            w_, m_, v_ = t3(w_), t3(m_), t3(v_)
        shp = w_.shape
        w3 = w_.reshape((shp[0], -1, shp[-1])) if w_.ndim >= 3 else w_.reshape((1,) + shp)
        res = _adamw(slabs.reshape((N_DEV,) + w3.shape), w3, m_.reshape(w3.shape), v_.reshape(w3.shape),
                     name="adamw_" + name_)
        outs[name_] = tuple(t3(t.reshape(shp)) if transposed else t.reshape(shp) for t in res)
        return res[0]

    rep_names = ("pre_mix_g", "post_mix_g", "pre_ffn_g", "post_ffn_g", "b_forget", "sgu_ln_g", "sgu_ln_b", "sgu_w", "sgu_b")
    rep_w = dict(pre_mix_g=(pre_mix_g, m_pre_mix_g, v_pre_mix_g), post_mix_g=(post_mix_g, m_post_mix_g, v_post_mix_g),
                 pre_ffn_g=(pre_ffn_g, m_pre_ffn_g, v_pre_ffn_g), post_ffn_g=(post_ffn_g, m_post_ffn_g, v_post_ffn_g),
                 b_forget=(b_forget, m_b_forget, v_b_forget), sgu_ln_g=(sgu_ln_g, m_sgu_ln_g, v_sgu_ln_g),
                 sgu_ln_b=(sgu_ln_b, m_sgu_ln_b, v_sgu_ln_b), sgu_w=(sgu_w, m_sgu_w, v_sgu_w), sgu_b=(sgu_b, m_sgu_b, v_sgu_b))

    def pack(parts):
        rows = [jnp.pad(p.reshape(-1), (0, -p.size % LANES)).reshape(-1, LANES) for p in parts]
        rows = jnp.concatenate(rows, axis=0)
        return jnp.pad(rows, ((0, -rows.shape[0] % PACK_ROWS), (0, 0)))

    part = pack([jnp.stack([g.reshape(rep_w[k][0].shape[1:]) for g in rep[k]]) for k in rep_names])
    small_sems, small_src, small_land, _ = _exchange_start([part], [lax.empty((N_DEV,) + part.shape, F32)], dx, whole, slab,
                                                            name="gather_small_start")

    got_up, got_dn = scatter_finish("ffn", dx)
    done = [update("w_ffn_up", got_up, w_ffn_up, m_w_ffn_up, v_w_ffn_up, transposed=True),
            update("w_ffn_down", got_dn, w_ffn_down, m_w_ffn_down, v_w_ffn_down)]

    _, (gathered,) = _exchange_wait(small_sems, small_src, small_land, done, whole, slab, name="gather_small_wait")
    packed = [pack([rep_w[k][i] for k in rep_names]) for i in range(3)]
    res = _adamw(gathered.reshape(N_DEV, 1, -1, LANES), *[p.reshape(1, -1, LANES) for p in packed], name="adamw_replicated")
    row = 0
    for k in rep_names:
        shp = rep_w[k][0].shape
        size = math.prod(shp)
        nrows = -(-size // LANES)
        outs[k] = tuple(t[0, row:row + nrows].reshape(-1)[:size].reshape(shp) for t in res)
        row += nrows

    got_a, got_c, got_s, got_o, small = scatter_finish("mid", res[0])
    done = [update("w_branch_att", got_a, w_branch_att, m_w_branch_att, v_w_branch_att),
            update("w_branch_conv", got_c, w_branch_conv, m_w_branch_conv, v_w_branch_conv),
            update("w_branch_sgu", got_s, w_branch_sgu, m_w_branch_sgu, v_w_branch_sgu),
            update("w_out", got_o, w_out, m_w_out, v_w_out),
            update("b_gate", small[..., :lo], b_gate, m_b_gate, v_b_gate),
            update("conv_mix_w", small[..., lo:lo + ncm], conv_mix_w, m_conv_mix_w, v_conv_mix_w),
            update("conv_ffn_w", small[..., lo + ncm:], conv_ffn_w, m_conv_ffn_w, v_conv_ffn_w)]
    (got_in,) = scatter_finish("win", done)
    update("w_in", got_in, w_in, m_w_in, v_w_in, transposed=True)

    order = ("pre_mix_g", "post_mix_g", "pre_ffn_g", "post_ffn_g", "w_in", "b_forget", "b_gate", "conv_mix_w", "sgu_ln_g",
             "sgu_ln_b", "sgu_w", "sgu_b", "w_branch_att", "w_branch_conv", "w_branch_sgu", "w_out", "w_ffn_up",
             "conv_ffn_w", "w_ffn_down")
    grad_x = dx.reshape(x.shape)
    return (loss, grad_x, *[outs[k][0] for k in order], *[outs[k][1] for k in order], *[outs[k][2] for k in order],
            *[outs[k][3] for k in order])
```

```python
import math

import jax
import jax.numpy as jnp
from jax import lax
from jax.experimental import pallas as pl
from jax.experimental.pallas import tpu as pltpu

F32 = jnp.float32
BF16 = jnp.bfloat16

N_DEV = 8
HEAD_DIM = 64
N_HEADS = 8
D_ATT = 512
D_CONV = 256
D_SGU = 256
N_SGU_GROUPS = 4
SGU_CHUNK = 128
RMS_EPS = 1e-6
LN_EPS = 1e-5
ADAM_LR = 0.001
ADAM_B1 = 0.9
ADAM_B2 = 0.999
ADAM_EPS = 1e-08
ADAM_WD = 0.01
ADAM_STEP = 10
LANES = 128
VMEM_LIMIT = 56 * 1024 * 1024
ATT_TILE = 512
LOG2E = math.log2(math.e)
NEG = -1e30
MESH = pl.DeviceIdType.MESH


def _params(*sem):
    return pltpu.CompilerParams(dimension_semantics=sem if sem else None, vmem_limit_bytes=VMEM_LIMIT)


def _tile(n, cap):
    if n <= cap:
        return n
    t = cap - cap % LANES
    while n % t:
        t -= LANES
    return t


def _gelu(x):
    return 0.5 * x * (1.0 + jnp.tanh(math.sqrt(2.0 / math.pi) * (x + 0.044715 * (x * x * x))))


def _gelu_and_slope(x):
    k0, k1 = math.sqrt(2.0 / math.pi), 0.044715
    x2 = x * x
    t = jnp.tanh(x * (k0 + (k0 * k1) * x2))
    half = 0.5 * (1.0 + t)
    return x * half, half + (0.5 * x) * (1.0 - t * t) * (k0 + (3.0 * k0 * k1) * x2)


def _rms(x, g):
    r = lax.rsqrt(jnp.mean(x * x, axis=-1, keepdims=True) + RMS_EPS)
    return x * r * g


def _layer_norm(x, g, b):
    mu = jnp.mean(x, axis=-1, keepdims=True)
    xc = x - mu
    var = jnp.mean(xc * xc, axis=-1, keepdims=True)
    return xc * lax.rsqrt(var + LN_EPS) * g + b


def _shift_down(x, k, rows):
    return jnp.where(rows >= k, pltpu.roll(x, k, 0), 0.0)


def _shift_up(x, k, rows):
    s = x.shape[0]
    return jnp.where(rows < s - k, pltpu.roll(x, s - k, 0), 0.0)


def _conv3(x, w_ref, rows):
    return w_ref[2:3, :] * x + w_ref[1:2, :] * _shift_down(x, 1, rows) + w_ref[0:1, :] * _shift_down(x, 2, rows)


def _conv3_bwd(dy, x, w_ref, rows):
    up1, up2 = _shift_up(dy, 1, rows), _shift_up(dy, 2, rows)
    dx = w_ref[2:3, :] * dy + w_ref[1:2, :] * up1 + w_ref[0:1, :] * up2
    d2 = jnp.sum(dy * x, axis=0, keepdims=True)
    d1 = jnp.sum(up1 * x, axis=0, keepdims=True)
    d0 = jnp.sum(up2 * x, axis=0, keepdims=True)
    sub = lax.broadcasted_iota(jnp.int32, (8, x.shape[1]), 0)
    return dx, jnp.where(sub == 0, d0, jnp.where(sub == 1, d1, jnp.where(sub == 2, d2, 0.0)))


MM_VMEM_BUDGET = 40 * 1024 * 1024
MM_TILE_CAP = 1408


def _mm_tiles(m, n, k, out_bytes):
    def edges(d):
        return [t for t in range(LANES, min(d, MM_TILE_CAP) + 1, LANES) if d % t == 0] or [d]

    best = None
    for tm in edges(m):
        for tn in edges(n):
            if 2 * (2 * k * (tm + tn) + tm * tn * out_bytes) > MM_VMEM_BUDGET:
                continue
            for a_outer in (True, False):
                reads = k * m + (m // tm) * k * n if a_outer else k * n + (n // tn) * k * m
                traffic = 2 * reads + m * n * out_bytes
                key = (traffic, -tm * tn)
                if best is None or key < best[0]:
                    best = (key, (tm, tn, a_outer))
    return best[1]


def _mm(a, b, form, out_dtype, name, after=None):
    if form == "nn":
        (m, k), n = a.shape, b.shape[1]
    elif form == "nt":
        (m, k), n = a.shape, b.shape[0]
    else:
        (k, m), n = a.shape, b.shape[1]
    tm, tn, a_outer = _mm_tiles(m, n, k, jnp.dtype(out_dtype).itemsize)
    dims = {"nn": (((1,), (0,)), ((), ())), "nt": (((1,), (1,)), ((), ())), "tn": (((0,), (0,)), ((), ()))}[form]

    def body(a_ref, b_ref, *rest):
        o_ref = rest[-1]
        o_ref[...] = lax.dot_general(a_ref[...], b_ref[...], dims, preferred_element_type=F32).astype(o_ref.dtype)

    ij = (lambda g0, g1: (g0, g1)) if a_outer else (lambda g0, g1: (g1, g0))
    a_spec = (pl.BlockSpec((k, tm), lambda g0, g1: (0, ij(g0, g1)[0])) if form == "tn"
              else pl.BlockSpec((tm, k), lambda g0, g1: (ij(g0, g1)[0], 0)))
    b_spec = (pl.BlockSpec((tn, k), lambda g0, g1: (ij(g0, g1)[1], 0)) if form == "nt"
              else pl.BlockSpec((k, tn), lambda g0, g1: (0, ij(g0, g1)[1])))
    extra = [] if after is None else [pl.BlockSpec((8, LANES), lambda g0, g1: (0, 0))]
    return pl.pallas_call(
        body, name=name, grid=(m // tm, n // tn) if a_outer else (n // tn, m // tm),
        in_specs=[a_spec, b_spec] + extra, out_specs=pl.BlockSpec((tm, tn), lambda g0, g1: ij(g0, g1)),
        out_shape=jax.ShapeDtypeStruct((m, n), out_dtype),
        compiler_params=_params("parallel", "arbitrary"),
    )(a, b, *([] if after is None else [after]))


def _postnorm_bwd_mm(o, g, dx, b, name):
    s, d = o.shape
    n = b.shape[0]
    tm, tn = _tile(s, 512), _tile(n, MM_TILE_CAP)

    def body(o_ref, g_ref, dx_ref, b_ref, do_ref, dg_ref, out_ref):
        i, j = pl.program_id(0), pl.program_id(1)

        @pl.when(j == 0)
        def _():
            _, vjp = jax.vjp(_rms, o_ref[...], g_ref[...])
            d_o, dg = vjp(dx_ref[...])
            do_ref[...] = d_o.astype(BF16)

            @pl.when(i == 0)
            def _():
                dg_ref[...] = jnp.zeros_like(dg_ref)

            dg_ref[...] += dg

        out_ref[...] = lax.dot_general(do_ref[...], b_ref[...], (((1,), (1,)), ((), ())), preferred_element_type=F32)

    row = pl.BlockSpec((tm, d), lambda i, j: (i, 0))
    vec = pl.BlockSpec((1, d), lambda i, j: (0, 0))
    return pl.pallas_call(
        body, name=name, grid=(s // tm, n // tn),
        in_specs=[row, vec, row, pl.BlockSpec((tn, d), lambda i, j: (j, 0))],
        out_specs=[row, vec, pl.BlockSpec((tm, tn), lambda i, j: (i, j))],
        out_shape=[jax.ShapeDtypeStruct((s, d), BF16), jax.ShapeDtypeStruct((1, d), F32), jax.ShapeDtypeStruct((s, n), F32)],
        compiler_params=_params("arbitrary", "arbitrary"),
    )(o, g, dx, b)


def _mm_prenorm_bwd(a, b, x, g, dres, after, name):
    s, k = a.shape
    d = b.shape[1]
    tm = _tile(s, 512)

    def body(a_ref, b_ref, x_ref, g_ref, dres_ref, after_ref, dx_ref, dg_ref):
        dxn = jnp.dot(a_ref[...], b_ref[...], preferred_element_type=F32)
        _, vjp = jax.vjp(_rms, x_ref[...], g_ref[...])
        dx, dg = vjp(dxn)
        dx_ref[...] = dres_ref[...] + dx

        @pl.when(pl.program_id(0) == 0)
        def _():
            dg_ref[...] = jnp.zeros_like(dg_ref)

        dg_ref[...] += dg

    row = pl.BlockSpec((tm, d), lambda i: (i, 0))
    vec = pl.BlockSpec((1, d), lambda i: (0, 0))
    return pl.pallas_call(
        body, name=name, grid=(s // tm,),
        in_specs=[pl.BlockSpec((tm, k), lambda i: (i, 0)),
                  pl.BlockSpec((k, d), lambda i: (0, 0), pipeline_mode=pl.Buffered(1)),
                  row, vec, row, pl.BlockSpec((8, LANES), lambda i: (0, 0))],
        out_specs=[row, vec], out_shape=[jax.ShapeDtypeStruct((s, d), F32), jax.ShapeDtypeStruct((1, d), F32)],
        compiler_params=_params("arbitrary"),
    )(a, b, x, g, dres, after)


def _prenorm(x, g, name):
    s, d = x.shape
    tm = _tile(s, 512)

    def body(x_ref, g_ref, o_ref):
        o_ref[...] = _rms(x_ref[...], g_ref[...]).astype(BF16)

    return pl.pallas_call(
        body, name=name, grid=(s // tm,),
        in_specs=[pl.BlockSpec((tm, d), lambda i: (i, 0)), pl.BlockSpec((1, d), lambda i: (0, 0))],
        out_specs=pl.BlockSpec((tm, d), lambda i: (i, 0)),
        out_shape=jax.ShapeDtypeStruct((s, d), BF16), compiler_params=_params("parallel"),
    )(x, g)


def _mm_postnorm(a, b, x, g_post, g_next, name):
    s, k = a.shape
    d = b.shape[1]
    tm = _tile(s, 512)

    def body(a_ref, b_ref, x_ref, gp_ref, gn_ref, o_ref, x1_ref, xn_ref):
        o = jnp.dot(a_ref[...], b_ref[...], preferred_element_type=F32)
        o_ref[...] = o
        x1 = x_ref[...] + _rms(o, gp_ref[...])
        x1_ref[...] = x1
        xn_ref[...] = _rms(x1, gn_ref[...]).astype(BF16)

    row = pl.BlockSpec((tm, d), lambda i: (i, 0))
    vec = pl.BlockSpec((1, d), lambda i: (0, 0))
    return pl.pallas_call(
        body, name=name, grid=(s // tm,),
        in_specs=[pl.BlockSpec((tm, k), lambda i: (i, 0)),
                  pl.BlockSpec((k, d), lambda i: (0, 0), pipeline_mode=pl.Buffered(1)), row, vec, vec],
        out_specs=[row, row, row],
        out_shape=[jax.ShapeDtypeStruct((s, d), F32), jax.ShapeDtypeStruct((s, d), F32), jax.ShapeDtypeStruct((s, d), BF16)],
        compiler_params=_params("parallel"),
    )(a, b, x, g_post, g_next)


def _mm_postnorm_loss(a, b, x, g_post, target, name):
    s, k = a.shape
    d = b.shape[1]
    tm = _tile(s, 512)

    def body(a_ref, b_ref, x_ref, gp_ref, t_ref, o_ref, dy_ref, acc_ref):
        o = jnp.dot(a_ref[...], b_ref[...], preferred_element_type=F32)
        o_ref[...] = o
        e = x_ref[...] + _rms(o, gp_ref[...]) - t_ref[...]
        dy_ref[...] = e / d

        @pl.when(pl.program_id(0) == 0)
        def _():
            acc_ref[...] = jnp.zeros_like(acc_ref)

        acc_ref[...] += jnp.sum(jnp.sum(e * e, axis=1, keepdims=True), axis=0, keepdims=True)

    row = pl.BlockSpec((tm, d), lambda i: (i, 0))
    return pl.pallas_call(
        body, name=name, grid=(s // tm,),
        in_specs=[pl.BlockSpec((tm, k), lambda i: (i, 0)),
                  pl.BlockSpec((k, d), lambda i: (0, 0), pipeline_mode=pl.Buffered(1)), row,
                  pl.BlockSpec((1, d), lambda i: (0, 0)), row],
        out_specs=[row, row, pl.BlockSpec((1, LANES), lambda i: (0, 0))],
        out_shape=[jax.ShapeDtypeStruct((s, d), F32), jax.ShapeDtypeStruct((s, d), F32), jax.ShapeDtypeStruct((1, LANES), F32)],
        compiler_params=_params("arbitrary"),
    )(a, b, x, g_post, target)


def _postnorm_bwd(o, g, dx, name):
    s, d = o.shape
    tm = _tile(s, 512)

    def body(o_ref, g_ref, dx_ref, do_ref, dg_ref):
        _, vjp = jax.vjp(_rms, o_ref[...], g_ref[...])
        d_o, dg = vjp(dx_ref[...])
        do_ref[...] = d_o.astype(BF16)

        @pl.when(pl.program_id(0) == 0)
        def _():
            dg_ref[...] = jnp.zeros_like(dg_ref)

        dg_ref[...] += dg

    row = pl.BlockSpec((tm, d), lambda i: (i, 0))
    vec = pl.BlockSpec((1, d), lambda i: (0, 0))
    return pl.pallas_call(
        body, name=name, grid=(s // tm,), in_specs=[row, vec, row], out_specs=[row, vec],
        out_shape=[jax.ShapeDtypeStruct((s, d), BF16), jax.ShapeDtypeStruct((1, d), F32)],
        compiler_params=_params("arbitrary"),
    )(o, g, dx)


def _log_sigmoid(z):
    return jnp.minimum(z, 0.0) - jnp.log(1.0 + jnp.exp(-jnp.abs(z)))


def _forget_prep(h, bf_pad, fblk, name):
    s = h.shape[0]

    def body(f_ref, b_ref, c_ref, ct_ref):
        c = _log_sigmoid(f_ref[...] + b_ref[...])
        rows = lax.broadcasted_iota(jnp.int32, c.shape, 0)
        k = 1
        while k < s:
            c = c + _shift_down(c, k, rows)
            k *= 2
        c_ref[...] = c
        ct_ref[...] = jnp.transpose(c)[0:8, :]

    return pl.pallas_call(
        body, name=name, grid=(1,),
        in_specs=[pl.BlockSpec((s, LANES), lambda i: (0, fblk)), pl.BlockSpec((1, LANES), lambda i: (0, 0))],
        out_specs=[pl.BlockSpec((s, LANES), lambda i: (0, 0)), pl.BlockSpec((8, s), lambda i: (0, 0))],
        out_shape=[jax.ShapeDtypeStruct((s, LANES), F32), jax.ShapeDtypeStruct((8, s), F32)],
        compiler_params=_params("arbitrary"),
    )(h, bf_pad)


def _forget_prep_bwd(h, bf_pad, dct, dcq, fblk, name):
    s = h.shape[0]
    pairs = N_HEADS // 2

    def body(f_ref, b_ref, dct_ref, dcq_ref, df_ref, db_ref):
        dct = dct_ref[0]
        dcq = dcq_ref[0]
        for p in range(1, pairs):
            dct = dct + pltpu.roll(dct_ref[p], 2 * p, 0)
            dcq = dcq + pltpu.roll(dcq_ref[p], 2 * p, 1)
        dc = dcq + jnp.transpose(jnp.concatenate([dct, jnp.zeros((LANES - 8, s), F32)], axis=0))
        rows = lax.broadcasted_iota(jnp.int32, dc.shape, 0)
        k = 1
        while k < s:
            dc = dc + _shift_up(dc, k, rows)
            k *= 2
        z = f_ref[...] + b_ref[...]
        lane = lax.broadcasted_iota(jnp.int32, dc.shape, 1)
        df = jnp.where(lane < N_HEADS, dc * jax.nn.sigmoid(-z), 0.0)
        df_ref[...] = df
        db_ref[...] = jnp.sum(df, axis=0, keepdims=True)

    return pl.pallas_call(
        body, name=name, grid=(1,),
        in_specs=[pl.BlockSpec((s, LANES), lambda i: (0, fblk)), pl.BlockSpec((1, LANES), lambda i: (0, 0)),
                  pl.BlockSpec((pairs, 8, s), lambda i: (0, 0, 0)), pl.BlockSpec((pairs, s, LANES), lambda i: (0, 0, 0))],
        out_specs=[pl.BlockSpec((s, LANES), lambda i: (0, 0)), pl.BlockSpec((1, LANES), lambda i: (0, 0))],
        out_shape=[jax.ShapeDtypeStruct((s, LANES), F32), jax.ShapeDtypeStruct((1, LANES), F32)],
        compiler_params=_params("arbitrary"),
    )(h, bf_pad, dct, dcq)


def _pick_lane(blk, idx):
    lane = lax.broadcasted_iota(jnp.int32, blk.shape, 1)
    return jnp.sum(jnp.where(lane == idx, blk, 0.0), axis=1, keepdims=True)


def _pick_row(blk, idx):
    sub = lax.broadcasted_iota(jnp.int32, blk.shape, 0)
    return jnp.sum(jnp.where(sub == idx, blk, 0.0), axis=0, keepdims=True)


def _attention_fwd(h, c, ct, qblk, name):
    s = h.shape[0]
    t = _tile(s, ATT_TILE)
    nq = s // t
    scale = HEAD_DIM ** -0.5
    nt_dims = (((1,), (1,)), ((), ()))

    def body(q_ref, k_ref, v_ref, c_ref, ct_ref, o_ref, lse_ref):
        p = pl.program_id(0)
        i = pl.program_id(1)
        lane = lax.broadcasted_iota(jnp.int32, (1, LANES), 1)
        first = lane < HEAD_DIM
        q = q_ref[...] * (scale * LOG2E)
        qa = jnp.where(first, q, 0.0).astype(BF16)
        qb = jnp.where(first, 0.0, q).astype(BF16)
        cblk = c_ref[...]
        cta = _pick_lane(cblk, 2 * p) * LOG2E
        ctb = _pick_lane(cblk, 2 * p + 1) * LOG2E

        def step(j, carry, diagonal):
            ma, la, mb, lb, acc = carry
            off = pl.multiple_of(j * t, t)
            k = k_ref[pl.ds(off, t), :].astype(BF16)
            v = v_ref[pl.ds(off, t), :].astype(BF16)
            crow = ct_ref[:, pl.ds(off, t)] * LOG2E

            def one(qh, cth, hd, m_old, l_old):
                sc = lax.dot_general(qh, k, nt_dims, preferred_element_type=F32) - _pick_row(crow, hd)
                if diagonal:
                    keep = lax.broadcasted_iota(jnp.int32, (t, t), 0) >= lax.broadcasted_iota(jnp.int32, (t, t), 1)
                    sc = jnp.where(keep, sc, NEG)
                m_new = jnp.maximum(m_old, jnp.max(sc, axis=1, keepdims=True) + cth)
                pr = jnp.exp2(sc - (m_new - cth))
                alpha = jnp.exp2(m_old - m_new)
                l_new = alpha * l_old + jnp.sum(pr, axis=1, keepdims=True)
                pv = jnp.dot(pr.astype(BF16), v, preferred_element_type=F32)
                return m_new, l_new, alpha, pv

            ma2, la2, aa, pva = one(qa, cta, 2 * p, ma, la)
            mb2, lb2, ab, pvb = one(qb, ctb, 2 * p + 1, mb, lb)
            acc = jnp.where(first, aa * acc + pva, ab * acc + pvb)
            return ma2, la2, mb2, lb2, acc

        init = (jnp.full((t, 1), NEG, F32), jnp.zeros((t, 1), F32), jnp.full((t, 1), NEG, F32),
                jnp.zeros((t, 1), F32), jnp.zeros((t, LANES), F32))
        carry = lax.fori_loop(0, i, lambda j, carry: step(j, carry, False), init)
        ma, la, mb, lb, acc = step(i, carry, True)
        o_ref[...] = (acc / jnp.where(first, la, lb)).astype(BF16)
        lse_ref[0] = jnp.broadcast_to(ma + jnp.log2(la), (t, LANES))
        lse_ref[1] = jnp.broadcast_to(mb + jnp.log2(lb), (t, LANES))

    return pl.pallas_call(
        body, name=name, grid=(N_HEADS // 2, nq),
        in_specs=[pl.BlockSpec((t, LANES), lambda p, i: (i, qblk + p)),
                  pl.BlockSpec((s, LANES), lambda p, i: (0, qblk + 4 + p)),
                  pl.BlockSpec((s, LANES), lambda p, i: (0, qblk + 8 + p)),
                  pl.BlockSpec((t, LANES), lambda p, i: (i, 0)),
                  pl.BlockSpec((8, s), lambda p, i: (0, 0))],
        out_specs=[pl.BlockSpec((t, LANES), lambda p, i: (i, p)),
                   pl.BlockSpec((2, t, LANES), lambda p, i: (p, i, 0))],
        out_shape=[jax.ShapeDtypeStruct((s, D_ATT), BF16), jax.ShapeDtypeStruct((N_HEADS, s, LANES), F32)],
        compiler_params=_params("parallel", "arbitrary"),
    )(h, h, h, c, ct)


def _attention_bwd(h, c, ct, lse, att, datt, after, qblk, name):
    s = h.shape[0]
    t = _tile(s, ATT_TILE)
    nq = s // t
    scale = HEAD_DIM ** -0.5
    nt_dims = (((1,), (1,)), ((), ()))
    tn_dims = (((0,), (0,)), ((), ()))

    def body(q_ref, k_ref, v_ref, c_ref, ct_ref, lse_ref, o_ref, do_ref, after_ref, dq_ref, dk_ref, dv_ref, dct_ref, dcq_ref):
        p = pl.program_id(0)
        j = pl.program_id(1)
        lane = lax.broadcasted_iota(jnp.int32, (1, LANES), 1)
        first = lane < HEAD_DIM
        kf = k_ref[...]
        vf = v_ref[...]
        k = kf.astype(BF16)
        ka = jnp.where(first, kf, 0.0).astype(BF16)
        kb = jnp.where(first, 0.0, kf).astype(BF16)
        va = jnp.where(first, vf, 0.0).astype(BF16)
        vb = jnp.where(first, 0.0, vf).astype(BF16)
        crow = ct_ref[...] * LOG2E
        csa = _pick_row(crow, 2 * p)
        csb = _pick_row(crow, 2 * p + 1)

        @pl.when(j == 0)
        def _():
            dq_ref[...] = jnp.zeros_like(dq_ref)
            dcq_ref[...] = jnp.zeros_like(dcq_ref)

        def step(i, carry, diagonal):
            dka, dkb, dva, dvb, dca, dcb = carry
            off = pl.multiple_of(i * t, t)
            rows = pl.ds(off, t)
            q = (q_ref[rows, :] * (scale * LOG2E)).astype(BF16)
            dof = do_ref[rows, :]
            do = dof.astype(BF16)
            prod = dof * o_ref[rows, :].astype(F32)
            cblk = c_ref[rows, :] * LOG2E

            def one(kh, vh, hd, csh, lse_h):
                sc = lax.dot_general(q, kh, nt_dims, preferred_element_type=F32) - csh
                if diagonal:
                    keep = lax.broadcasted_iota(jnp.int32, (t, t), 0) >= lax.broadcasted_iota(jnp.int32, (t, t), 1)
                    sc = jnp.where(keep, sc, NEG)
                pr = jnp.exp2(sc - (jnp.max(lse_h, axis=1, keepdims=True) - _pick_lane(cblk, hd)))
                dp = lax.dot_general(do, vh, nt_dims, preferred_element_type=F32)
                return pr, dp

            pra, dpa = one(ka, va, 2 * p, csa, lse_ref[0, rows, :])
            prb, dpb = one(kb, vb, 2 * p + 1, csb, lse_ref[1, rows, :])
            dela = jnp.sum(jnp.where(first, prod, 0.0), axis=1, keepdims=True)
            delb = jnp.sum(jnp.where(first, 0.0, prod), axis=1, keepdims=True)
            dsa = pra * (dpa - dela)
            dsb = prb * (dpb - delb)
            dsa16 = dsa.astype(BF16)
            dsb16 = dsb.astype(BF16)
            dva = dva + lax.dot_general(pra.astype(BF16), do, tn_dims, preferred_element_type=F32)
            dvb = dvb + lax.dot_general(prb.astype(BF16), do, tn_dims, preferred_element_type=F32)
            dka = dka + lax.dot_general(dsa16, q, tn_dims, preferred_element_type=F32)
            dkb = dkb + lax.dot_general(dsb16, q, tn_dims, preferred_element_type=F32)
            dqa = jnp.dot(dsa16, k, preferred_element_type=F32)
            dqb = jnp.dot(dsb16, k, preferred_element_type=F32)
            dq_ref[rows, :] += scale * jnp.where(first, dqa, dqb)
            dca = dca - jnp.sum(dsa, axis=0, keepdims=True)
            dcb = dcb - jnp.sum(dsb, axis=0, keepdims=True)
            dcq_ref[rows, :] += jnp.where(lane == 0, jnp.sum(dsa, axis=1, keepdims=True),
                                          jnp.where(lane == 1, jnp.sum(dsb, axis=1, keepdims=True), 0.0))
            return dka, dkb, dva, dvb, dca, dcb

        z = jnp.zeros((t, LANES), F32)
        zr = jnp.zeros((1, t), F32)
        carry = step(j, (z, z, z, z, zr, zr), True)
        dka, dkb, dva, dvb, dca, dcb = lax.fori_loop(j + 1, nq, lambda i, carry: step(i, carry, False), carry)
        dk_ref[...] = jnp.where(first, dka, dkb) * (1.0 / LOG2E)
        dv_ref[...] = jnp.where(first, dva, dvb)
        sub = lax.broadcasted_iota(jnp.int32, (8, t), 0)
        dct_ref[...] = jnp.where(sub == 0, dca, jnp.where(sub == 1, dcb, 0.0))

    full = lambda blk: pl.BlockSpec((s, LANES), blk)
    return pl.pallas_call(
        body, name=name, grid=(N_HEADS // 2, nq),
        in_specs=[full(lambda p, j: (0, qblk + p)),
                  pl.BlockSpec((t, LANES), lambda p, j: (j, qblk + 4 + p)),
                  pl.BlockSpec((t, LANES), lambda p, j: (j, qblk + 8 + p)),
                  full(lambda p, j: (0, 0)),
                  pl.BlockSpec((8, t), lambda p, j: (0, j)),
                  pl.BlockSpec((2, s, LANES), lambda p, j: (p, 0, 0)),
                  full(lambda p, j: (0, p)),
                  full(lambda p, j: (0, p)),
                  pl.BlockSpec((8, LANES), lambda p, j: (0, 0))],
        out_specs=[full(lambda p, j: (0, p)),
                   pl.BlockSpec((t, LANES), lambda p, j: (j, p)),
                   pl.BlockSpec((t, LANES), lambda p, j: (j, p)),
                   pl.BlockSpec((None, 8, t), lambda p, j: (p, 0, j)),
                   pl.BlockSpec((None, s, LANES), lambda p, j: (p, 0, 0))],
        out_shape=[jax.ShapeDtypeStruct((s, D_ATT), F32), jax.ShapeDtypeStruct((s, D_ATT), F32),
                   jax.ShapeDtypeStruct((s, D_ATT), F32), jax.ShapeDtypeStruct((N_HEADS // 2, 8, s), F32),
                   jax.ShapeDtypeStruct((N_HEADS // 2, s, LANES), F32)],
        compiler_params=_params("arbitrary", "arbitrary"),
    )(h, h, h, c, ct, lse, att, datt, after)


def _sconv_fwd(h, w, bgblk, name):
    s = h.shape[0]
    nblk = D_CONV // LANES

    def body(bg_ref, cg_ref, hc_ref, w_ref, y_ref):
        rows = lax.broadcasted_iota(jnp.int32, (s, LANES), 0)
        y_ref[...] = (bg_ref[...] * _conv3(cg_ref[...] * hc_ref[...], w_ref, rows)).astype(BF16)

    col = lambda base: pl.BlockSpec((s, LANES), lambda j: (0, base + j))
    return pl.pallas_call(
        body, name=name, grid=(nblk,),
        in_specs=[col(bgblk), col(bgblk + nblk), col(bgblk + 2 * nblk), pl.BlockSpec((3, LANES), lambda j: (0, j))],
        out_specs=pl.BlockSpec((s, LANES), lambda j: (0, j)),
        out_shape=jax.ShapeDtypeStruct((s, D_CONV), BF16), compiler_params=_params("parallel"),
    )(h, h, h, w)


def _sconv_bwd(h, w, dy, bgblk, name):
    s = h.shape[0]
    nblk = D_CONV // LANES

    def body(bg_ref, cg_ref, hc_ref, w_ref, dy_ref, dbg_ref, dcg_ref, dhc_ref, dw_ref):
        rows = lax.broadcasted_iota(jnp.int32, (s, LANES), 0)
        cg, hc, dy, w = cg_ref[...], hc_ref[...], dy_ref[...], w_ref
        xin = cg * hc
        dbg_ref[...] = (dy * _conv3(xin, w, rows)).astype(BF16)
        dxin, dw_ref[...] = _conv3_bwd(dy * bg_ref[...], xin, w, rows)
        dcg_ref[...] = (dxin * hc).astype(BF16)
        dhc_ref[...] = (dxin * cg).astype(BF16)

    col = lambda base: pl.BlockSpec((s, LANES), lambda j: (0, base + j))
    return pl.pallas_call(
        body, name=name, grid=(nblk,),
        in_specs=[col(bgblk), col(bgblk + nblk), col(bgblk + 2 * nblk), pl.BlockSpec((3, LANES), lambda j: (0, j)), col(0)],
        out_specs=[col(0), col(0), col(0), pl.BlockSpec((8, LANES), lambda j: (0, j))],
        out_shape=[jax.ShapeDtypeStruct((s, D_CONV), BF16)] * 3 + [jax.ShapeDtypeStruct((8, D_CONV), F32)],
        compiler_params=_params("parallel"),
    )(h, h, h, w, dy)


def _sgu_group_masks():
    lane = lax.broadcasted_iota(jnp.int32, (1, D_SGU), 1)
    return [(lane // HEAD_DIM) == g for g in range(N_SGU_GROUPS)]


def _sgu_tril():
    r = lax.broadcasted_iota(jnp.int32, (SGU_CHUNK, SGU_CHUNK), 0)
    c = lax.broadcasted_iota(jnp.int32, (SGU_CHUNK, SGU_CHUNK), 1)
    return r >= c


def _sgu_fwd(h, ln_g, ln_b, w_s, b_full, ublk, name):
    s = h.shape[0]
    tr = _tile(s, 512)
    nch = tr // SGU_CHUNK

    def body(u_ref, v_ref, g_ref, b_ref, w_ref, bf_ref, y_ref):
        masks = _sgu_group_masks()
        tril = _sgu_tril()
        wm = [jnp.where(tril, w_ref[g], 0.0).astype(BF16) for g in range(N_SGU_GROUPS)]
        vn = _layer_norm(_gelu(v_ref[...]), g_ref[...], b_ref[...])
        for ch in range(nch):
            rows = pl.ds(ch * SGU_CHUNK, SGU_CHUNK)
            vc = vn[ch * SGU_CHUNK:(ch + 1) * SGU_CHUNK, :]
            mixed = bf_ref[...]
            for g in range(N_SGU_GROUPS):
                mixed = mixed + jnp.dot(wm[g], jnp.where(masks[g], vc, 0.0).astype(BF16), preferred_element_type=F32)
            y_ref[rows, :] = (_gelu(u_ref[rows, :]) * mixed).astype(BF16)

    row = lambda blk: pl.BlockSpec((tr, D_SGU), lambda i: (i, blk))
    vec = pl.BlockSpec((1, D_SGU), lambda i: (0, 0))
    return pl.pallas_call(
        body, name=name, grid=(s // tr,),
        in_specs=[row(ublk), row(ublk + 1), vec, vec,
                  pl.BlockSpec((N_SGU_GROUPS, SGU_CHUNK, SGU_CHUNK), lambda i: (0, 0, 0)),
                  pl.BlockSpec((SGU_CHUNK, D_SGU), lambda i: (0, 0))],
        out_specs=row(0), out_shape=jax.ShapeDtypeStruct((s, D_SGU), BF16), compiler_params=_params("parallel"),
    )(h, h, ln_g, ln_b, w_s, b_full)


def _sgu_bwd(h, ln_g, ln_b, w_s, b_full, dy, ublk, name):
    s = h.shape[0]
    tr = _tile(s, 512)
    nch = tr // SGU_CHUNK
    nt_dims = (((1,), (1,)), ((), ()))

    def norm(v, g, b):
        return _layer_norm(_gelu(v), g, b)

    def body(u_ref, v_ref, g_ref, b_ref, w_ref, bf_ref, dy_ref, du_ref, dv_ref, dg_ref, db_ref, dw_ref, dbf_ref):
        masks = _sgu_group_masks()
        tril = _sgu_tril()
        wf = [jnp.where(tril, w_ref[g], 0.0) for g in range(N_SGU_GROUPS)]
        wm = [w.astype(BF16) for w in wf]
        wmt = [jnp.transpose(w).astype(BF16) for w in wf]
        vn, vjp = jax.vjp(norm, v_ref[...], g_ref[...], b_ref[...])

        @pl.when(pl.program_id(0) == 0)
        def _():
            dg_ref[...] = jnp.zeros_like(dg_ref)
            db_ref[...] = jnp.zeros_like(db_ref)
            dw_ref[...] = jnp.zeros_like(dw_ref)
            dbf_ref[...] = jnp.zeros_like(dbf_ref)

        dvn_parts = []
        for ch in range(nch):
            rows = pl.ds(ch * SGU_CHUNK, SGU_CHUNK)
            vc = vn[ch * SGU_CHUNK:(ch + 1) * SGU_CHUNK, :]
            vc16 = vc.astype(BF16)
            mixed = bf_ref[...]
            for g in range(N_SGU_GROUPS):
                mixed = mixed + jnp.dot(wm[g], jnp.where(masks[g], vc, 0.0).astype(BF16), preferred_element_type=F32)
            dy = dy_ref[rows, :]
            ug, slope = _gelu_and_slope(u_ref[rows, :])
            du_ref[rows, :] = (dy * mixed * slope).astype(BF16)
            dmixed = dy * ug
            dbf_ref[...] += dmixed
            dvc = jnp.zeros((SGU_CHUNK, D_SGU), F32)
            for g in range(N_SGU_GROUPS):
                dm16 = jnp.where(masks[g], dmixed, 0.0).astype(BF16)
                dw_ref[g] += jnp.where(tril, lax.dot_general(dm16, vc16, nt_dims, preferred_element_type=F32), 0.0)
                dvc = dvc + jnp.dot(wmt[g], dm16, preferred_element_type=F32)
            dvn_parts.append(dvc)
        dv, dg, db = vjp(jnp.concatenate(dvn_parts, axis=0))
        dv_ref[...] = dv.astype(BF16)
        dg_ref[...] += dg
        db_ref[...] += db

    row = lambda blk: pl.BlockSpec((tr, D_SGU), lambda i: (i, blk))
    vec = pl.BlockSpec((1, D_SGU), lambda i: (0, 0))
    wsp = pl.BlockSpec((N_SGU_GROUPS, SGU_CHUNK, SGU_CHUNK), lambda i: (0, 0, 0))
    bsp = pl.BlockSpec((SGU_CHUNK, D_SGU), lambda i: (0, 0))
    return pl.pallas_call(
        body, name=name, grid=(s // tr,),
        in_specs=[row(ublk), row(ublk + 1), vec, vec, wsp, bsp, row(0)],
        out_specs=[row(0), row(0), vec, vec, wsp, bsp],
        out_shape=[jax.ShapeDtypeStruct((s, D_SGU), BF16), jax.ShapeDtypeStruct((s, D_SGU), BF16),
                   jax.ShapeDtypeStruct((1, D_SGU), F32), jax.ShapeDtypeStruct((1, D_SGU), F32),
                   jax.ShapeDtypeStruct((N_SGU_GROUPS, SGU_CHUNK, SGU_CHUNK), F32),
                   jax.ShapeDtypeStruct((SGU_CHUNK, D_SGU), F32)],
        compiler_params=_params("arbitrary"),
    )(h, h, ln_g, ln_b, w_s, b_full, dy)


def _merge_fwd(h, b_gate, att, yc, ys, wa, wc, ws, w_out, x, g_post, g_next, name):
    s, d = att.shape[0], wa.shape[1]
    tm = _tile(s, 512)

    def body(g0_ref, g1_ref, g2_ref, bg_ref, a_ref, c_ref, s_ref, wa_ref, wc_ref, ws_ref, wo_ref, x_ref, gp_ref, gn_ref,
             m_ref, o_ref, x1_ref, xn_ref):
        acc = jax.nn.sigmoid(g0_ref[...] + bg_ref[0:1, :]) * jnp.dot(a_ref[...], wa_ref[...], preferred_element_type=F32)
        acc += jax.nn.sigmoid(g1_ref[...] + bg_ref[1:2, :]) * jnp.dot(c_ref[...], wc_ref[...], preferred_element_type=F32)
        acc += jax.nn.sigmoid(g2_ref[...] + bg_ref[2:3, :]) * jnp.dot(s_ref[...], ws_ref[...], preferred_element_type=F32)
        merged = acc.astype(BF16)
        m_ref[...] = merged
        o = jnp.dot(merged, wo_ref[...], preferred_element_type=F32)
        o_ref[...] = o
        x1 = x_ref[...] + _rms(o, gp_ref[...])
        x1_ref[...] = x1
        xn_ref[...] = _rms(x1, gn_ref[...]).astype(BF16)

    gate = lambda b: pl.BlockSpec((tm, d), lambda i: (i, b))
    act = lambda k: pl.BlockSpec((tm, k), lambda i: (i, 0))
    wgt = lambda k: pl.BlockSpec((k, d), lambda i: (0, 0), pipeline_mode=pl.Buffered(1))
    vec = pl.BlockSpec((1, d), lambda i: (0, 0))
    return pl.pallas_call(
        body, name=name, grid=(s // tm,),
        in_specs=[gate(0), gate(1), gate(2), pl.BlockSpec((3, d), lambda i: (0, 0)),
                  act(D_ATT), act(D_CONV), act(D_SGU), wgt(D_ATT), wgt(D_CONV), wgt(D_SGU), wgt(d), act(d), vec, vec],
        out_specs=[act(d)] * 4,
        out_shape=[jax.ShapeDtypeStruct((s, d), BF16), jax.ShapeDtypeStruct((s, d), F32), jax.ShapeDtypeStruct((s, d), F32),
                   jax.ShapeDtypeStruct((s, d), BF16)],
        compiler_params=_params("parallel"),
    )(h, h, h, b_gate, att, yc, ys, wa, wc, ws, w_out, x, g_post, g_next)


def _merge_bwd(h, b_gate, att, yc, ys, wa, wc, ws, dm, name):
    s, d = att.shape[0], wa.shape[1]
    tm = _tile(s, 512)
    nt_dims, tn_dims = (((1,), (1,)), ((), ())), (((0,), (0,)), ((), ()))
    widths = (D_ATT, D_CONV, D_SGU)

    def body(g0_ref, g1_ref, g2_ref, bg_ref, a_ref, c_ref, s_ref, wa_ref, wc_ref, ws_ref, dm_ref,
             da_ref, dc_ref, ds_ref, ga_ref, gc_ref, gs_ref, dgl_ref, dbg_ref, acc_a, acc_c, acc_s):
        i = pl.program_id(0)

        @pl.when(i == 0)
        def _():
            dbg_ref[...] = jnp.zeros_like(dbg_ref)
            for acc in (acc_a, acc_c, acc_s):
                acc[...] = jnp.zeros_like(acc)

        dm = dm_ref[...]
        sums = []
        for b, (g_ref, x_ref, w_ref, dx_ref, acc) in enumerate((
                (g0_ref, a_ref, wa_ref, da_ref, acc_a), (g1_ref, c_ref, wc_ref, dc_ref, acc_c),
                (g2_ref, s_ref, ws_ref, ds_ref, acc_s))):
            gate = jax.nn.sigmoid(g_ref[...] + bg_ref[b:b + 1, :])
            x, w = x_ref[...], w_ref[...]
            y = jnp.dot(x, w, preferred_element_type=F32)
            dy = (dm * gate).astype(BF16)
            dgl = dm * y * gate * (1.0 - gate)
            dgl_ref[b] = dgl.astype(BF16)
            sums.append(jnp.sum(dgl, axis=0, keepdims=True))
            dx_ref[...] = lax.dot_general(dy, w, nt_dims, preferred_element_type=F32)
            acc[...] += lax.dot_general(x, dy, tn_dims, preferred_element_type=F32)
        sub = lax.broadcasted_iota(jnp.int32, (3, d), 0)
        dbg_ref[...] += jnp.where(sub == 0, sums[0], jnp.where(sub == 1, sums[1], sums[2]))

        @pl.when(i == pl.num_programs(0) - 1)
        def _():
            for g_out, acc in ((ga_ref, acc_a), (gc_ref, acc_c), (gs_ref, acc_s)):
                g_out[...] = acc[...].astype(BF16)

    gate = lambda b: pl.BlockSpec((tm, d), lambda i: (i, b))
    act = lambda k: pl.BlockSpec((tm, k), lambda i: (i, 0))
    wgt = lambda k: pl.BlockSpec((k, d), lambda i: (0, 0), pipeline_mode=pl.Buffered(1))
    res = pl.pallas_call(
        body, name=name, grid=(s // tm,),
        in_specs=[gate(0), gate(1), gate(2), pl.BlockSpec((3, d), lambda i: (0, 0)),
                  act(D_ATT), act(D_CONV), act(D_SGU), wgt(D_ATT), wgt(D_CONV), wgt(D_SGU), act(d)],
        out_specs=[act(k) for k in widths] + [pl.BlockSpec((k, d), lambda i: (0, 0)) for k in widths]
        + [pl.BlockSpec((3, tm, d), lambda i: (0, i, 0)), pl.BlockSpec((3, d), lambda i: (0, 0))],
        out_shape=[jax.ShapeDtypeStruct((s, k), F32) for k in widths] + [jax.ShapeDtypeStruct((k, d), BF16) for k in widths]
        + [jax.ShapeDtypeStruct((3, s, d), BF16), jax.ShapeDtypeStruct((3, d), F32)],
        scratch_shapes=[pltpu.VMEM((k, d), F32) for k in widths],
        compiler_params=_params("arbitrary"),
    )(h, h, h, b_gate, att, yc, ys, wa, wc, ws, dm)
    return res[0:3], res[3:6], res[6], res[7]


def _ffn_act_fwd(hh, cw, name):
    s, dff = hh.shape[0], hh.shape[1] // 2
    nblk = dff // LANES

    def body(a_ref, b_ref, wa_ref, wb_ref, z_ref):
        rows = lax.broadcasted_iota(jnp.int32, (s, LANES), 0)
        z_ref[...] = (_gelu(_conv3(a_ref[...], wa_ref, rows)) * _conv3(b_ref[...], wb_ref, rows)).astype(BF16)

    col = lambda base: pl.BlockSpec((s, LANES), lambda j: (0, base + j))
    wsp = lambda base: pl.BlockSpec((3, LANES), lambda j: (0, base + j))
    return pl.pallas_call(
        body, name=name, grid=(nblk,), in_specs=[col(0), col(nblk), wsp(0), wsp(nblk)], out_specs=col(0),
        out_shape=jax.ShapeDtypeStruct((s, dff), BF16), compiler_params=_params("parallel"),
    )(hh, hh, cw, cw)


def _ffn_act_bwd(hh, cw, dz, name):
    s, dff = hh.shape[0], hh.shape[1] // 2
    nblk = dff // LANES

    def body(a_ref, b_ref, wa_ref, wb_ref, dz_ref, da_ref, db_ref, dwa_ref, dwb_ref):
        rows = lax.broadcasted_iota(jnp.int32, (s, LANES), 0)
        a, b, dz = a_ref[...], b_ref[...], dz_ref[...]
        ga, slope = _gelu_and_slope(_conv3(a, wa_ref, rows))
        da, dwa_ref[...] = _conv3_bwd(dz * _conv3(b, wb_ref, rows) * slope, a, wa_ref, rows)
        db, dwb_ref[...] = _conv3_bwd(dz * ga, b, wb_ref, rows)
        da_ref[...] = da.astype(BF16)
        db_ref[...] = db.astype(BF16)

    col = lambda base: pl.BlockSpec((s, LANES), lambda j: (0, base + j))
    wsp = lambda base: pl.BlockSpec((3, LANES), lambda j: (0, base + j))
    w8 = lambda base: pl.BlockSpec((8, LANES), lambda j: (0, base + j))
    return pl.pallas_call(
        body, name=name, grid=(nblk,), in_specs=[col(0), col(nblk), wsp(0), wsp(nblk), col(0)],
        out_specs=[col(0), col(0), w8(0), w8(0)],
        out_shape=[jax.ShapeDtypeStruct((s, dff), BF16)] * 2 + [jax.ShapeDtypeStruct((8, dff), F32)] * 2,
        compiler_params=_params("parallel"),
    )(hh, hh, cw, cw, dz)


ANY = pl.BlockSpec(memory_space=pl.ANY)


def _place():
    return lax.axis_index("x"), lax.axis_index("y"), lax.axis_index("c")


def _all_gather(arrs, name):
    n = len(arrs)

    def body(*refs):
        ins, outs = refs[:n], refs[n:2 * n]
        send_sems, recv_sems, local_sems = refs[2 * n:]
        x, y, c = _place()
        me, sibling = (x, y, c), (x, y, 1 - c)
        chips = [(1 - x, y), (x, 1 - y), (1 - x, 1 - y)]

        def slab(a, dev):
            return outs[a].at[4 * dev[0] + 2 * dev[1] + dev[2]]

        def copy(a, k, block, to, src=None):
            return pltpu.make_async_remote_copy(
                src_ref=slab(a, block) if src is None else src, dst_ref=slab(a, block),
                send_sem=send_sems.at[7 * a + k], recv_sem=recv_sems.at[7 * a + k], device_id=to, device_id_type=MESH)

        mine = [pltpu.make_async_copy(ins[a], slab(a, me), local_sems.at[a]) for a in range(n)]
        for cp in mine:
            cp.start()
        first = []
        for a in range(n):
            first.append(copy(a, 0, me, sibling, src=ins[a]))
            first += [copy(a, 1 + j, me, (*chip, c), src=ins[a]) for j, chip in enumerate(chips)]
        for cp in first:
            cp.start()
        passed = []
        for a in range(n):
            for j, chip in enumerate(chips):
                copy(a, 1 + j, (*chip, c), me).wait_recv()
                fwd = copy(a, 4 + j, (*chip, c), sibling)
                fwd.start()
                passed.append(fwd)
        for a in range(n):
            copy(a, 0, sibling, me).wait_recv()
            for j, chip in enumerate(chips):
                copy(a, 4 + j, (*chip, 1 - c), me).wait_recv()
        for cp in first + passed:
            cp.wait_send()
        for cp in mine:
            cp.wait()

    return pl.pallas_call(
        body, name=name, in_specs=[ANY] * n, out_specs=[ANY] * n,
        out_shape=[jax.ShapeDtypeStruct((N_DEV,) + a.shape, a.dtype) for a in arrs],
        scratch_shapes=[pltpu.SemaphoreType.DMA((7 * n,)), pltpu.SemaphoreType.DMA((7 * n,)), pltpu.SemaphoreType.DMA((n,))],
    )(*arrs)


HBM = pl.BlockSpec(memory_space=pltpu.HBM)
SEM = pl.BlockSpec(memory_space=pltpu.SEMAPHORE)
EFFECT = pltpu.SideEffectType.DATAFLOW_SIDE_EFFECTING


def _slot(dev):
    return 4 * dev[0] + 2 * dev[1] + dev[2]


def _exchange_copies(src_refs, land_refs, send_sems, recv_sems, src_view, land_view):
    x, y, c = _place()
    me = (x, y, c)
    peers = [(1 - x if r & 4 else x, 1 - y if r & 2 else y, 1 - c if r & 1 else c) for r in range(1, N_DEV)]
    sends, lands = [], []
    for a, (src, land) in enumerate(zip(src_refs, land_refs)):
        for k, peer in enumerate(peers):
            sems = dict(send_sem=send_sems.at[7 * a + k], recv_sem=recv_sems.at[7 * a + k], device_id=peer,
                        device_id_type=MESH)
            sends.append(pltpu.make_async_remote_copy(src_ref=src_view(src, _slot(peer)),
                                                      dst_ref=land_view(land, _slot(me)), **sems))
            lands.append(pltpu.make_async_remote_copy(src_ref=src_view(src, _slot(me)),
                                                      dst_ref=land_view(land, _slot(peer)), **sems))
    return sends, lands


def _own_copies(src_refs, land_refs, local_sems, src_view, land_view):
    me = _slot(_place())
    return [pltpu.make_async_copy(src_view(src, me), land_view(land, me), local_sems.at[a])
            for a, (src, land) in enumerate(zip(src_refs, land_refs))]


def _exchange_start(srcs, lands, after, src_view, land_view, name):
    n = len(srcs)

    def body(*refs):
        src_refs, land_refs = refs[:n], refs[n:2 * n]
        send_sems, recv_sems, local_sems = refs[2 * n + 1:2 * n + 4]
        token = refs[-1]
        sends, _ = _exchange_copies(src_refs, land_refs, send_sems, recv_sems, src_view, land_view)
        for cp in sends + _own_copies(src_refs, land_refs, local_sems, src_view, land_view):
            cp.start()
        token[...] = jnp.zeros_like(token)

    thru = [pltpu.HBM(a.shape, a.dtype) for a in list(srcs) + list(lands)]
    outs = pl.pallas_call(
        body, name=name,
        out_shape=(pltpu.SemaphoreType.DMA((7 * n,)), pltpu.SemaphoreType.DMA((7 * n,)), pltpu.SemaphoreType.DMA((n,)),
                   *thru, jax.ShapeDtypeStruct((8, LANES), F32)),
        in_specs=[HBM] * (2 * n) + [ANY],
        out_specs=(SEM, SEM, SEM, *([HBM] * (2 * n)), pl.BlockSpec(memory_space=pltpu.VMEM)),
        input_output_aliases={i: 3 + i for i in range(2 * n)},
        compiler_params=pltpu.CompilerParams(has_side_effects=EFFECT),
    )(*[pltpu.with_memory_space_constraint(a, pltpu.HBM) for a in list(srcs) + list(lands)], after)
    return outs[:3], list(outs[3:3 + n]), list(outs[3 + n:3 + 2 * n]), outs[-1]


def _exchange_wait(sems, srcs, lands, after, src_view, land_view, name):
    n = len(srcs)

    def body(*refs):
        src_refs, land_refs = refs[:n], refs[n:2 * n]
        send_sems, recv_sems, local_sems = refs[2 * n:2 * n + 3]
        sends, landed = _exchange_copies(src_refs, land_refs, send_sems, recv_sems, src_view, land_view)
        for cp in sends:
            cp.wait_send()
        for cp in landed:
            cp.wait_recv()
        for cp in _own_copies(src_refs, land_refs, local_sems, src_view, land_view):
            cp.wait()

    after = after if isinstance(after, (list, tuple)) else [after]
    outs = pl.pallas_call(
        body, name=name, out_shape=[pltpu.HBM(a.shape, a.dtype) for a in list(srcs) + list(lands)],
        in_specs=[HBM] * (2 * n) + [SEM, SEM, SEM] + [ANY] * len(after), out_specs=[HBM] * (2 * n),
        input_output_aliases={i: i for i in range(2 * n)},
        compiler_params=pltpu.CompilerParams(has_side_effects=EFFECT),
    )(*srcs, *lands, *sems, *after)
    return list(outs[:n]), list(outs[n:])


ADAMW_BLOCK_BYTES = 1 << 19
PACK_ROWS = 256


def _adamw(slabs, w, m, v, name):
    nl, r, c = w.shape
    row_edges = [r] + [t for t in range(8, r, 8) if r % t == 0]
    col_edges = [c] + [t for t in range(LANES, c, LANES) if c % t == 0]
    fits = [(tr * tc, tc, tr) for tr in row_edges for tc in col_edges if tr * tc * 4 <= ADAMW_BLOCK_BYTES]
    _, tc, tr = max(fits) if fits else (0, min(col_edges), min(row_edges))

    def body(s_ref, w_ref, m_ref, v_ref, g_ref, d_ref, nm_ref, nv_ref):
        g = s_ref[0].astype(F32)
        for q in range(1, N_DEV):
            g = g + s_ref[q].astype(F32)
        m_new = ADAM_B1 * m_ref[...] + (1.0 - ADAM_B1) * g
        v_new = ADAM_B2 * v_ref[...] + (1.0 - ADAM_B2) * (g * g)
        m_hat = m_new / (1.0 - ADAM_B1 ** ADAM_STEP)
        v_hat = v_new / (1.0 - ADAM_B2 ** ADAM_STEP)
        g_ref[...] = g
        d_ref[...] = -ADAM_LR * (m_hat / (jnp.sqrt(v_hat) + ADAM_EPS) + ADAM_WD * w_ref[...])
        nm_ref[...] = m_new
        nv_ref[...] = v_new

    blk = pl.BlockSpec((None, tr, tc), lambda l, i, j: (l, i, j))
    return pl.pallas_call(
        body, name=name, grid=(nl, r // tr, c // tc),
        in_specs=[pl.BlockSpec((N_DEV, None, tr, tc), lambda l, i, j: (0, l, i, j)), blk, blk, blk],
        out_specs=[blk] * 4, out_shape=[jax.ShapeDtypeStruct(w.shape, F32)] * 4,
        compiler_params=_params("parallel", "parallel", "parallel"),
    )(slabs, w, m, v)


def _layout(d):
    off = {"gate": 0, "q": 3 * d}
    off["bg"] = off["q"] + 3 * D_ATT
    off["u"] = off["bg"] + 3 * D_CONV
    off["f"] = off["u"] + 2 * D_SGU
    width = -(-(off["f"] + LANES) // 512) * 512
    return off, width


def _pad_w_in(wt, d, token):
    off, width = _layout(d)
    nqkv, nrest = 3 * D_ATT, 3 * D_CONV + 2 * D_SGU
    pad = jnp.zeros((width - off["f"] - N_HEADS, wt.shape[1]), wt.dtype) + token[0, 0].astype(wt.dtype)
    return jnp.concatenate([wt[nqkv + N_HEADS + nrest:], wt[:nqkv], wt[nqkv + N_HEADS:nqkv + N_HEADS + nrest],
                            wt[nqkv:nqkv + N_HEADS], pad], axis=0)


def _unpad_w_in(wtp, d):
    off, _ = _layout(d)
    return jnp.concatenate([wtp[off["q"]:off["bg"]], wtp[off["f"]:off["f"] + N_HEADS], wtp[off["bg"]:off["f"]],
                            wtp[:off["q"]]], axis=0)


def _cols_from_slabs(g):
    return jnp.transpose(g, (1, 0, 2)).reshape(g.shape[1], N_DEV * g.shape[2])


def _cols_to_slabs(w):
    r, c = w.shape[0], w.shape[1] // N_DEV
    return jnp.transpose(w.reshape(r, N_DEV, c), (1, 0, 2))


def kernel(x, pre_mix_g, post_mix_g, pre_ffn_g, post_ffn_g, w_in, b_forget, b_gate, conv_mix_w, sgu_ln_g, sgu_ln_b, sgu_w, sgu_b, w_branch_att, w_branch_conv, w_branch_sgu, w_out, w_ffn_up, conv_ffn_w, w_ffn_down, loss_target, m_pre_mix_g, m_post_mix_g, m_pre_ffn_g, m_post_ffn_g, m_w_in, m_b_forget, m_b_gate, m_conv_mix_w, m_sgu_ln_g, m_sgu_ln_b, m_sgu_w, m_sgu_b, m_w_branch_att, m_w_branch_conv, m_w_branch_sgu, m_w_out, m_w_ffn_up, m_conv_ffn_w, m_w_ffn_down, v_pre_mix_g, v_post_mix_g, v_pre_ffn_g, v_post_ffn_g, v_w_in, v_b_forget, v_b_gate, v_conv_mix_w, v_sgu_ln_g, v_sgu_ln_b, v_sgu_w, v_sgu_b, v_w_branch_att, v_w_branch_conv, v_w_branch_sgu, v_w_out, v_w_ffn_up, v_conv_ffn_w, v_w_ffn_down):
    depth = w_in.shape[0]
    s, d = x.shape[1], x.shape[2]
    dff = w_ffn_down.shape[1] * N_DEV
    off, _ = _layout(d)
    qblk, bgblk, ublk, fblk = off["q"] // LANES, off["bg"] // LANES, off["u"] // D_SGU, off["f"] // LANES
    x0 = x.reshape(s, d)
    target = loss_target.reshape(s, d)
    ncm, ncf = conv_mix_w.shape[2], conv_ffn_w.shape[2]

    lo = d // N_DEV

    whole = lambda ref, slot: ref
    slab = lambda ref, slot: ref.at[slot]

    w_in_t, w_up_t = jnp.transpose(w_in, (0, 2, 1)), jnp.transpose(w_ffn_up, (0, 2, 1))

    def shards_of(l, part):
        if part == "mix":
            small = jnp.concatenate([b_gate[l], conv_mix_w[l], conv_ffn_w[l]], axis=1)
            return [w_in_t[l].astype(BF16), w_branch_att[l].astype(BF16), w_branch_conv[l].astype(BF16),
                    w_branch_sgu[l].astype(BF16), w_out[l].astype(BF16), small]
        return [w_up_t[l].astype(BF16), w_ffn_down[l].astype(BF16)]

    def gather_start(l, part, after):
        shards = shards_of(l, part)
        lands = [lax.empty((N_DEV,) + a.shape, a.dtype) for a in shards]
        return _exchange_start(shards, lands, after, whole, slab, name=f"gather_start_{part}_{l}")

    def gather_finish(l, part, started, after):
        sems, shards, lands, _ = started[part]
        shards, lands = _exchange_wait(sems, shards, lands, after, whole, slab, name=f"gather_wait_{part}_{l}")
        token = jnp.zeros((8, LANES), F32)
        if l + 1 < depth:
            started[part] = gather_start(l + 1, part, lands[0])
            token = started[part][3]
        return lands, token

    def bfull(l):
        return jnp.repeat(jnp.transpose(sgu_b[l]), HEAD_DIM, axis=1)

    def bf_pad(l):
        return jnp.pad(b_forget[l], (0, LANES - N_HEADS)).reshape(1, LANES)

    saved = []
    weights = []
    xin = x0
    xn = _prenorm(x0, pre_mix_g[0:1], name="prenorm_first")
    loss_acc = dy = None
    first = _all_gather(shards_of(0, "mix"), name="gather_first")
    started = {"ffn": gather_start(0, "ffn", first[0])}
    for l in range(depth):
        if l == 0:
            (g_in, g_a, g_c, g_s, g_o, g_small), token = first, started["ffn"][3]
        else:
            (g_in, g_a, g_c, g_s, g_o, g_small), token = gather_finish(l, "mix", started, xin)
        g_small = _cols_from_slabs(g_small).reshape(3, N_DEV, -1)
        w = dict(w_in=_pad_w_in(g_in.reshape(N_DEV * g_in.shape[1], d), d, token), wa=_cols_from_slabs(g_a),
                 wc=_cols_from_slabs(g_c), ws=_cols_from_slabs(g_s), w_out=g_o.reshape(d, d),
                 b_gate=g_small[:, :, :lo].reshape(3, d), cmw=g_small[:, :, lo:lo + ncm].reshape(3, D_CONV),
                 cfw=g_small[:, :, lo + ncm:].reshape(3, 2 * dff))
        weights.append(w)
        h = _mm(xn, w["w_in"], "nt", F32, name="proj_in")
        c, ct = _forget_prep(h, bf_pad(l), fblk, name="forget_prep")
        att, lse = _attention_fwd(h, c, ct, qblk, name="attention_fwd")
        if l == 0 and depth > 1:
            started["mix"] = gather_start(1, "mix", att)
            w["cmw"] = w["cmw"] + started["mix"][3][0, 0]
        yc = _sconv_fwd(h, w["cmw"], bgblk, name="sconv_fwd")
        ys = _sgu_fwd(h, sgu_ln_g[l:l + 1], sgu_ln_b[l:l + 1], sgu_w[l], bfull(l), ublk, name="sgu_fwd")
        (g_up, g_dn), token = gather_finish(l, "ffn", started, ys)
        w["w_up"], w["w_dn"] = g_up.reshape(2 * dff, d), g_dn.reshape(dff, d)
        merged, o, x1, xn2 = _merge_fwd(h, w["b_gate"], att, yc, ys, w["wa"], w["wc"], w["ws"], w["w_out"], xin,
                                        post_mix_g[l:l + 1], pre_ffn_g[l:l + 1] + token[0, 0], name="merge_proj_out")
        hh = _mm(xn2, w["w_up"], "nt", F32, name="ffn_up")
        z = _ffn_act_fwd(hh, w["cfw"], name="ffn_act_fwd")
        layer = dict(xin=xin, xn=xn, h=h, c=c, ct=ct, lse=lse, att=att, yc=yc, ys=ys, merged=merged, o=o, x1=x1,
                     xn2=xn2, hh=hh, z=z)
        if l + 1 < depth:
            layer["f"], xin, xn = _mm_postnorm(z, w["w_dn"], x1, post_ffn_g[l:l + 1], pre_mix_g[l + 1:l + 2],
                                               name="ffn_down")
        else:
            layer["f"], dy, loss_acc = _mm_postnorm_loss(z, w["w_dn"], x1, post_ffn_g[l:l + 1], target,
                                                         name="ffn_down_loss")
        saved.append(layer)
    loss = lax.psum(loss_acc[0, 0] * (0.5 / d), ("x", "y", "c"))

    rep = {k: [None] * depth for k in ("pre_mix_g", "post_mix_g", "pre_ffn_g", "post_ffn_g", "b_forget", "sgu_ln_g",
                                       "sgu_ln_b", "sgu_w", "sgu_b")}
    nsmall = lo + ncm + ncf
    lands = {"win": [lax.empty((N_DEV, depth) + w_in_t.shape[1:], BF16)],
             "mid": [lax.empty((N_DEV, depth) + shp, dt) for shp, dt in (
                 (w_branch_att.shape[1:], BF16), (w_branch_conv.shape[1:], BF16), (w_branch_sgu.shape[1:], BF16),
                 (w_out.shape[1:], BF16), ((3, nsmall), F32))],
             "ffn": [lax.empty((N_DEV, depth) + shp, BF16) for shp in (w_up_t.shape[1:], w_ffn_down.shape[1:])]}
    scatters = {part: [None] * depth for part in lands}

    def scatter_start(l, part, sends, after):
        layer_slab = lambda ref, slot: ref.at[slot, l]
        sems, sends, lands[part], token = _exchange_start(sends, lands[part], after, slab, layer_slab,
                                                          name=f"scatter_start_{part}_{l}")
        scatters[part][l] = (sems, sends, layer_slab)
        return token

    def scatter_finish(part, after):
        for l in range(depth):
            sems, sends, layer_slab = scatters[part][l]
            _, lands[part] = _exchange_wait(sems, sends, lands[part], after, slab, layer_slab,
                                            name=f"scatter_wait_{part}_{l}")
        return lands[part]

    token = jnp.zeros((8, LANES), F32)
    dx = dy
    for l in reversed(range(depth)):
        w, a = weights[l], saved[l]
        df, rep["post_ffn_g"][l] = _postnorm_bwd(a["f"], post_ffn_g[l:l + 1] + token[0, 0], dx, name="postnorm_bwd")
        dz = _mm(df, w["w_dn"], "nt", F32, name="ffn_down_dx")
        g_dn = _mm(a["z"], df, "tn", BF16, name="ffn_down_dw")
        dha, dhb, dcwa, dcwb = _ffn_act_bwd(a["hh"], w["cfw"], dz, name="ffn_act_bwd")
        dhh = jnp.concatenate([dha, dhb], axis=1)
        dcfw = jnp.concatenate([dcwa[0:3], dcwb[0:3]], axis=1)
        g_up = _mm(dhh, a["xn2"], "tn", BF16, name="ffn_up_dw")
        token = scatter_start(l, "ffn", [g_up.reshape(N_DEV, 2 * dff // N_DEV, d), g_dn.reshape(N_DEV, dff // N_DEV, d)],
                              dz)
        dx1, rep["pre_ffn_g"][l] = _mm_prenorm_bwd(dhh, w["w_up"], a["x1"], pre_ffn_g[l:l + 1], dx, token, name="ffn_up_dx")
        do, rep["post_mix_g"][l], dmerged = _postnorm_bwd_mm(a["o"], post_mix_g[l:l + 1], dx1, w["w_out"], name="proj_out_dx")
        g_o = _mm(a["merged"], do, "tn", BF16, name="proj_out_dw")
        (datt, dconv, dsgu), (g_a, g_c, g_s), dgl, dbg = _merge_bwd(
            a["h"], w["b_gate"], a["att"], a["yc"], a["ys"], w["wa"], w["wc"], w["ws"], dmerged, name="merge_bwd")
        dbgate, dcg, dhc, dcmw = _sconv_bwd(a["h"], w["cmw"], dconv, bgblk, name="sconv_bwd")
        sends = [_cols_to_slabs(g_a), _cols_to_slabs(g_c), _cols_to_slabs(g_s), g_o.reshape(N_DEV, d // N_DEV, d),
                 jnp.concatenate([_cols_to_slabs(dbg), _cols_to_slabs(dcmw[0:3]), _cols_to_slabs(dcfw)], axis=2)]
        token = scatter_start(l, "mid", sends, dx1)
        dq, dk, dv, dct4, dcq4 = _attention_bwd(a["h"], a["c"], a["ct"], a["lse"], a["att"], datt, token, qblk,
                                                name="attention_bwd")
        dfl, dbf = _forget_prep_bwd(a["h"], bf_pad(l), dct4, dcq4, fblk, name="forget_prep_bwd")
        rep["b_forget"][l] = dbf[0, :N_HEADS]
        du, dvs, dlg, dlb, dsw, dbfull = _sgu_bwd(a["h"], sgu_ln_g[l:l + 1], sgu_ln_b[l:l + 1], sgu_w[l], bfull(l), dsgu,
                                                  ublk, name="sgu_bwd")
        rep["sgu_ln_g"][l], rep["sgu_ln_b"][l], rep["sgu_w"][l] = dlg, dlb, dsw
        rep["sgu_b"][l] = jnp.transpose(jnp.sum(dbfull.reshape(SGU_CHUNK, N_SGU_GROUPS, HEAD_DIM), axis=2))
        dh = jnp.concatenate([dgl[0], dgl[1], dgl[2], dq.astype(BF16), dk.astype(BF16), dv.astype(BF16), dbgate, dcg, dhc,
                              du, dvs, dfl.astype(BF16), jnp.zeros((s, w["w_in"].shape[0] - off["f"] - LANES), BF16)], axis=1)
        g_in = _mm(dh, a["xn"], "tn", BF16, name="proj_in_dw")
        token = scatter_start(l, "win", [_unpad_w_in(g_in, d).reshape(N_DEV, -1, d)], dx1)
        dx, rep["pre_mix_g"][l] = _mm_prenorm_bwd(dh, w["w_in"], a["xin"], pre_mix_g[l:l + 1], dx1, token, name="proj_in_dx")

    outs = {}
    t3 = lambda arr: jnp.transpose(arr, (0, 2, 1))

    def update(name_, slabs, w_, m_, v_, transposed=False):
        if transposed:
            w_, m_, v_ = t3(w_), t3(m_), t3(v_)
        shp = w_.shape
        w3 = w_.reshape((shp[0], -1, shp[-1])) if w_.ndim >= 3 else w_.reshape((1,) + shp)
        res = _adamw(slabs.reshape((N_DEV,) + w3.shape), w3, m_.reshape(w3.shape), v_.reshape(w3.shape),
                     name="adamw_" + name_)
        outs[name_] = tuple(t3(t.reshape(shp)) if transposed else t.reshape(shp) for t in res)
        return res[0]

    rep_names = ("pre_mix_g", "post_mix_g", "pre_ffn_g", "post_ffn_g", "b_forget", "sgu_ln_g", "sgu_ln_b", "sgu_w", "sgu_b")
    rep_w = dict(pre_mix_g=(pre_mix_g, m_pre_mix_g, v_pre_mix_g), post_mix_g=(post_mix_g, m_post_mix_g, v_post_mix_g),
                 pre_ffn_g=(pre_ffn_g, m_pre_ffn_g, v_pre_ffn_g), post_ffn_g=(post_ffn_g, m_post_ffn_g, v_post_ffn_g),
                 b_forget=(b_forget, m_b_forget, v_b_forget), sgu_ln_g=(sgu_ln_g, m_sgu_ln_g, v_sgu_ln_g),
                 sgu_ln_b=(sgu_ln_b, m_sgu_ln_b, v_sgu_ln_b), sgu_w=(sgu_w, m_sgu_w, v_sgu_w), sgu_b=(sgu_b, m_sgu_b, v_sgu_b))

    def pack(parts):
        rows = [jnp.pad(p.reshape(-1), (0, -p.size % LANES)).reshape(-1, LANES) for p in parts]
        rows = jnp.concatenate(rows, axis=0)
        return jnp.pad(rows, ((0, -rows.shape[0] % PACK_ROWS), (0, 0)))

    part = pack([jnp.stack([g.reshape(rep_w[k][0].shape[1:]) for g in rep[k]]) for k in rep_names])
    small_sems, small_src, small_land, _ = _exchange_start([part], [lax.empty((N_DEV,) + part.shape, F32)], dx, whole, slab,
                                                            name="gather_small_start")

    got_up, got_dn = scatter_finish("ffn", dx)
    done = [update("w_ffn_up", got_up, w_ffn_up, m_w_ffn_up, v_w_ffn_up, transposed=True),
            update("w_ffn_down", got_dn, w_ffn_down, m_w_ffn_down, v_w_ffn_down)]

    got_a, got_c, got_s, got_o, small = scatter_finish("mid", done)
    done = [update("w_branch_att", got_a, w_branch_att, m_w_branch_att, v_w_branch_att),
            update("w_branch_conv", got_c, w_branch_conv, m_w_branch_conv, v_w_branch_conv),
            update("w_branch_sgu", got_s, w_branch_sgu, m_w_branch_sgu, v_w_branch_sgu),
            update("w_out", got_o, w_out, m_w_out, v_w_out),
            update("b_gate", small[..., :lo], b_gate, m_b_gate, v_b_gate),
            update("conv_mix_w", small[..., lo:lo + ncm], conv_mix_w, m_conv_mix_w, v_conv_mix_w),
            update("conv_ffn_w", small[..., lo + ncm:], conv_ffn_w, m_conv_ffn_w, v_conv_ffn_w)]

    _, (gathered,) = _exchange_wait(small_sems, small_src, small_land, done, whole, slab, name="gather_small_wait")
    packed = [pack([rep_w[k][i] for k in rep_names]) for i in range(3)]
    res = _adamw(gathered.reshape(N_DEV, 1, -1, LANES), *[p.reshape(1, -1, LANES) for p in packed], name="adamw_replicated")
    row = 0
    for k in rep_names:
        shp = rep_w[k][0].shape
        size = math.prod(shp)
        nrows = -(-size // LANES)
        outs[k] = tuple(t[0, row:row + nrows].reshape(-1)[:size].reshape(shp) for t in res)
        row += nrows

    (got_in,) = scatter_finish("win", res[0])
    update("w_in", got_in, w_in, m_w_in, v_w_in, transposed=True)

    order = ("pre_mix_g", "post_mix_g", "pre_ffn_g", "post_ffn_g", "w_in", "b_forget", "b_gate", "conv_mix_w", "sgu_ln_g",
             "sgu_ln_b", "sgu_w", "sgu_b", "w_branch_att", "w_branch_conv", "w_branch_sgu", "w_out", "w_ffn_up",
             "conv_ffn_w", "w_ffn_down")
    grad_x = dx.reshape(x.shape)
    return (loss, grad_x, *[outs[k][0] for k in order], *[outs[k][1] for k in order], *[outs[k][2] for k in order],
            *[outs[k][3] for k in order])
```

```python
import math

import jax
import jax.numpy as jnp
from jax import lax
from jax.experimental import pallas as pl
from jax.experimental.pallas import tpu as pltpu

F32 = jnp.float32
BF16 = jnp.bfloat16

N_DEV = 8
HEAD_DIM = 64
N_HEADS = 8
D_ATT = 512
D_CONV = 256
D_SGU = 256
N_SGU_GROUPS = 4
SGU_CHUNK = 128
RMS_EPS = 1e-6
LN_EPS = 1e-5
ADAM_LR = 0.001
ADAM_B1 = 0.9
ADAM_B2 = 0.999
ADAM_EPS = 1e-08
ADAM_WD = 0.01
ADAM_STEP = 10
LANES = 128
VMEM_LIMIT = 56 * 1024 * 1024
ATT_TILE = 512
LOG2E = math.log2(math.e)
NEG = -1e30
MESH = pl.DeviceIdType.MESH


def _params(*sem):
    return pltpu.CompilerParams(dimension_semantics=sem if sem else None, vmem_limit_bytes=VMEM_LIMIT)


def _tile(n, cap):
    if n <= cap:
        return n
    t = cap - cap % LANES
    while n % t:
        t -= LANES
    return t


def _gelu(x):
    return 0.5 * x * (1.0 + jnp.tanh(math.sqrt(2.0 / math.pi) * (x + 0.044715 * (x * x * x))))


def _gelu_and_slope(x):
    k0, k1 = math.sqrt(2.0 / math.pi), 0.044715
    x2 = x * x
    t = jnp.tanh(x * (k0 + (k0 * k1) * x2))
    half = 0.5 * (1.0 + t)
    return x * half, half + (0.5 * x) * (1.0 - t * t) * (k0 + (3.0 * k0 * k1) * x2)


def _rms(x, g):
    r = lax.rsqrt(jnp.mean(x * x, axis=-1, keepdims=True) + RMS_EPS)
    return x * r * g


def _layer_norm(x, g, b):
    mu = jnp.mean(x, axis=-1, keepdims=True)
    xc = x - mu
    var = jnp.mean(xc * xc, axis=-1, keepdims=True)
    return xc * lax.rsqrt(var + LN_EPS) * g + b


def _shift_down(x, k, rows):
    return jnp.where(rows >= k, pltpu.roll(x, k, 0), 0.0)


def _shift_up(x, k, rows):
    s = x.shape[0]
    return jnp.where(rows < s - k, pltpu.roll(x, s - k, 0), 0.0)


def _conv3(x, w_ref, rows):
    return w_ref[2:3, :] * x + w_ref[1:2, :] * _shift_down(x, 1, rows) + w_ref[0:1, :] * _shift_down(x, 2, rows)


def _conv3_bwd(dy, x, w_ref, rows):
    up1, up2 = _shift_up(dy, 1, rows), _shift_up(dy, 2, rows)
    dx = w_ref[2:3, :] * dy + w_ref[1:2, :] * up1 + w_ref[0:1, :] * up2
    d2 = jnp.sum(dy * x, axis=0, keepdims=True)
    d1 = jnp.sum(up1 * x, axis=0, keepdims=True)
    d0 = jnp.sum(up2 * x, axis=0, keepdims=True)
    sub = lax.broadcasted_iota(jnp.int32, (8, x.shape[1]), 0)
    return dx, jnp.where(sub == 0, d0, jnp.where(sub == 1, d1, jnp.where(sub == 2, d2, 0.0)))


MM_VMEM_BUDGET = 40 * 1024 * 1024
MM_TILE_CAP = 1408


def _mm_tiles(m, n, k, out_bytes):
    def edges(d):
        return [t for t in range(LANES, min(d, MM_TILE_CAP) + 1, LANES) if d % t == 0] or [d]

    best = None
    for tm in edges(m):
        for tn in edges(n):
            if 2 * (2 * k * (tm + tn) + tm * tn * out_bytes) > MM_VMEM_BUDGET:
                continue
            for a_outer in (True, False):
                reads = k * m + (m // tm) * k * n if a_outer else k * n + (n // tn) * k * m
                traffic = 2 * reads + m * n * out_bytes
                key = (traffic, -tm * tn)
                if best is None or key < best[0]:
                    best = (key, (tm, tn, a_outer))
    return best[1]


def _mm(a, b, form, out_dtype, name, after=None):
    if form == "nn":
        (m, k), n = a.shape, b.shape[1]
    elif form == "nt":
        (m, k), n = a.shape, b.shape[0]
    else:
        (k, m), n = a.shape, b.shape[1]
    tm, tn, a_outer = _mm_tiles(m, n, k, jnp.dtype(out_dtype).itemsize)
    dims = {"nn": (((1,), (0,)), ((), ())), "nt": (((1,), (1,)), ((), ())), "tn": (((0,), (0,)), ((), ()))}[form]

    def body(a_ref, b_ref, *rest):
        o_ref = rest[-1]
        o_ref[...] = lax.dot_general(a_ref[...], b_ref[...], dims, preferred_element_type=F32).astype(o_ref.dtype)

    ij = (lambda g0, g1: (g0, g1)) if a_outer else (lambda g0, g1: (g1, g0))
    a_spec = (pl.BlockSpec((k, tm), lambda g0, g1: (0, ij(g0, g1)[0])) if form == "tn"
              else pl.BlockSpec((tm, k), lambda g0, g1: (ij(g0, g1)[0], 0)))
    b_spec = (pl.BlockSpec((tn, k), lambda g0, g1: (ij(g0, g1)[1], 0)) if form == "nt"
              else pl.BlockSpec((k, tn), lambda g0, g1: (0, ij(g0, g1)[1])))
    extra = [] if after is None else [pl.BlockSpec((8, LANES), lambda g0, g1: (0, 0))]
    return pl.pallas_call(
        body, name=name, grid=(m // tm, n // tn) if a_outer else (n // tn, m // tm),
        in_specs=[a_spec, b_spec] + extra, out_specs=pl.BlockSpec((tm, tn), lambda g0, g1: ij(g0, g1)),
        out_shape=jax.ShapeDtypeStruct((m, n), out_dtype),
        compiler_params=_params("parallel", "arbitrary"),
    )(a, b, *([] if after is None else [after]))


def _postnorm_bwd_mm(o, g, dx, b, name):
    s, d = o.shape
    n = b.shape[0]
    tm, tn = _tile(s, 512), _tile(n, MM_TILE_CAP)

    def body(o_ref, g_ref, dx_ref, b_ref, do_ref, dg_ref, out_ref):
        i, j = pl.program_id(0), pl.program_id(1)

        @pl.when(j == 0)
        def _():
            _, vjp = jax.vjp(_rms, o_ref[...], g_ref[...])
            d_o, dg = vjp(dx_ref[...])
            do_ref[...] = d_o.astype(BF16)

            @pl.when(i == 0)
            def _():
                dg_ref[...] = jnp.zeros_like(dg_ref)

            dg_ref[...] += dg

        out_ref[...] = lax.dot_general(do_ref[...], b_ref[...], (((1,), (1,)), ((), ())), preferred_element_type=F32)

    row = pl.BlockSpec((tm, d), lambda i, j: (i, 0))
    vec = pl.BlockSpec((1, d), lambda i, j: (0, 0))
    return pl.pallas_call(
        body, name=name, grid=(s // tm, n // tn),
        in_specs=[row, vec, row, pl.BlockSpec((tn, d), lambda i, j: (j, 0))],
        out_specs=[row, vec, pl.BlockSpec((tm, tn), lambda i, j: (i, j))],
        out_shape=[jax.ShapeDtypeStruct((s, d), BF16), jax.ShapeDtypeStruct((1, d), F32), jax.ShapeDtypeStruct((s, n), F32)],
        compiler_params=_params("arbitrary", "arbitrary"),
    )(o, g, dx, b)


def _mm_prenorm_bwd(a, b, x, g, dres, after, name):
    s, k = a.shape
    d = b.shape[1]
    tm = _tile(s, 512)

    def body(a_ref, b_ref, x_ref, g_ref, dres_ref, after_ref, dx_ref, dg_ref):
        dxn = jnp.dot(a_ref[...], b_ref[...], preferred_element_type=F32)
        _, vjp = jax.vjp(_rms, x_ref[...], g_ref[...])
        dx, dg = vjp(dxn)
        dx_ref[...] = dres_ref[...] + dx

        @pl.when(pl.program_id(0) == 0)
        def _():
            dg_ref[...] = jnp.zeros_like(dg_ref)

        dg_ref[...] += dg

    row = pl.BlockSpec((tm, d), lambda i: (i, 0))
    vec = pl.BlockSpec((1, d), lambda i: (0, 0))
    return pl.pallas_call(
        body, name=name, grid=(s // tm,),
        in_specs=[pl.BlockSpec((tm, k), lambda i: (i, 0)),
                  pl.BlockSpec((k, d), lambda i: (0, 0), pipeline_mode=pl.Buffered(1)),
                  row, vec, row, pl.BlockSpec((8, LANES), lambda i: (0, 0))],
        out_specs=[row, vec], out_shape=[jax.ShapeDtypeStruct((s, d), F32), jax.ShapeDtypeStruct((1, d), F32)],
        compiler_params=_params("arbitrary"),
    )(a, b, x, g, dres, after)


def _prenorm(x, g, name):
    s, d = x.shape
    tm = _tile(s, 512)

    def body(x_ref, g_ref, o_ref):
        o_ref[...] = _rms(x_ref[...], g_ref[...]).astype(BF16)

    return pl.pallas_call(
        body, name=name, grid=(s // tm,),
        in_specs=[pl.BlockSpec((tm, d), lambda i: (i, 0)), pl.BlockSpec((1, d), lambda i: (0, 0))],
        out_specs=pl.BlockSpec((tm, d), lambda i: (i, 0)),
        out_shape=jax.ShapeDtypeStruct((s, d), BF16), compiler_params=_params("parallel"),
    )(x, g)


def _mm_postnorm(a, b, x, g_post, g_next, name):
    s, k = a.shape
    d = b.shape[1]
    tm = _tile(s, 512)

    def body(a_ref, b_ref, x_ref, gp_ref, gn_ref, o_ref, x1_ref, xn_ref):
        o = jnp.dot(a_ref[...], b_ref[...], preferred_element_type=F32)
        o_ref[...] = o
        x1 = x_ref[...] + _rms(o, gp_ref[...])
        x1_ref[...] = x1
        xn_ref[...] = _rms(x1, gn_ref[...]).astype(BF16)

    row = pl.BlockSpec((tm, d), lambda i: (i, 0))
    vec = pl.BlockSpec((1, d), lambda i: (0, 0))
    return pl.pallas_call(
        body, name=name, grid=(s // tm,),
        in_specs=[pl.BlockSpec((tm, k), lambda i: (i, 0)),
                  pl.BlockSpec((k, d), lambda i: (0, 0), pipeline_mode=pl.Buffered(1)), row, vec, vec],
        out_specs=[row, row, row],
        out_shape=[jax.ShapeDtypeStruct((s, d), F32), jax.ShapeDtypeStruct((s, d), F32), jax.ShapeDtypeStruct((s, d), BF16)],
        compiler_params=_params("parallel"),
    )(a, b, x, g_post, g_next)


def _mm_postnorm_loss(a, b, x, g_post, target, name):
    s, k = a.shape
    d = b.shape[1]
    tm = _tile(s, 512)

    def body(a_ref, b_ref, x_ref, gp_ref, t_ref, o_ref, dy_ref, acc_ref):
        o = jnp.dot(a_ref[...], b_ref[...], preferred_element_type=F32)
        o_ref[...] = o
        e = x_ref[...] + _rms(o, gp_ref[...]) - t_ref[...]
        dy_ref[...] = e / d

        @pl.when(pl.program_id(0) == 0)
        def _():
            acc_ref[...] = jnp.zeros_like(acc_ref)

        acc_ref[...] += jnp.sum(jnp.sum(e * e, axis=1, keepdims=True), axis=0, keepdims=True)

    row = pl.BlockSpec((tm, d), lambda i: (i, 0))
    return pl.pallas_call(
        body, name=name, grid=(s // tm,),
        in_specs=[pl.BlockSpec((tm, k), lambda i: (i, 0)),
                  pl.BlockSpec((k, d), lambda i: (0, 0), pipeline_mode=pl.Buffered(1)), row,
                  pl.BlockSpec((1, d), lambda i: (0, 0)), row],
        out_specs=[row, row, pl.BlockSpec((1, LANES), lambda i: (0, 0))],
        out_shape=[jax.ShapeDtypeStruct((s, d), F32), jax.ShapeDtypeStruct((s, d), F32), jax.ShapeDtypeStruct((1, LANES), F32)],
        compiler_params=_params("arbitrary"),
    )(a, b, x, g_post, target)


def _postnorm_bwd(o, g, dx, name):
    s, d = o.shape
    tm = _tile(s, 512)

    def body(o_ref, g_ref, dx_ref, do_ref, dg_ref):
        _, vjp = jax.vjp(_rms, o_ref[...], g_ref[...])
        d_o, dg = vjp(dx_ref[...])
        do_ref[...] = d_o.astype(BF16)

        @pl.when(pl.program_id(0) == 0)
        def _():
            dg_ref[...] = jnp.zeros_like(dg_ref)

        dg_ref[...] += dg

    row = pl.BlockSpec((tm, d), lambda i: (i, 0))
    vec = pl.BlockSpec((1, d), lambda i: (0, 0))
    return pl.pallas_call(
        body, name=name, grid=(s // tm,), in_specs=[row, vec, row], out_specs=[row, vec],
        out_shape=[jax.ShapeDtypeStruct((s, d), BF16), jax.ShapeDtypeStruct((1, d), F32)],
        compiler_params=_params("arbitrary"),
    )(o, g, dx)


def _log_sigmoid(z):
    return jnp.minimum(z, 0.0) - jnp.log(1.0 + jnp.exp(-jnp.abs(z)))


def _forget_prep(h, bf_pad, fblk, name):
    s = h.shape[0]

    def body(f_ref, b_ref, c_ref, ct_ref):
        c = _log_sigmoid(f_ref[...] + b_ref[...])
        rows = lax.broadcasted_iota(jnp.int32, c.shape, 0)
        k = 1
        while k < s:
            c = c + _shift_down(c, k, rows)
            k *= 2
        c_ref[...] = c
        ct_ref[...] = jnp.transpose(c)[0:8, :]

    return pl.pallas_call(
        body, name=name, grid=(1,),
        in_specs=[pl.BlockSpec((s, LANES), lambda i: (0, fblk)), pl.BlockSpec((1, LANES), lambda i: (0, 0))],
        out_specs=[pl.BlockSpec((s, LANES), lambda i: (0, 0)), pl.BlockSpec((8, s), lambda i: (0, 0))],
        out_shape=[jax.ShapeDtypeStruct((s, LANES), F32), jax.ShapeDtypeStruct((8, s), F32)],
        compiler_params=_params("arbitrary"),
    )(h, bf_pad)


def _forget_prep_bwd(h, bf_pad, dct, dcq, fblk, name):
    s = h.shape[0]
    pairs = N_HEADS // 2

    def body(f_ref, b_ref, dct_ref, dcq_ref, df_ref, db_ref):
        dct = dct_ref[0]
        dcq = dcq_ref[0]
        for p in range(1, pairs):
            dct = dct + pltpu.roll(dct_ref[p], 2 * p, 0)
            dcq = dcq + pltpu.roll(dcq_ref[p], 2 * p, 1)
        dc = dcq + jnp.transpose(jnp.concatenate([dct, jnp.zeros((LANES - 8, s), F32)], axis=0))
        rows = lax.broadcasted_iota(jnp.int32, dc.shape, 0)
        k = 1
        while k < s:
            dc = dc + _shift_up(dc, k, rows)
            k *= 2
        z = f_ref[...] + b_ref[...]
        lane = lax.broadcasted_iota(jnp.int32, dc.shape, 1)
        df = jnp.where(lane < N_HEADS, dc * jax.nn.sigmoid(-z), 0.0)
        df_ref[...] = df
        db_ref[...] = jnp.sum(df, axis=0, keepdims=True)

    return pl.pallas_call(
        body, name=name, grid=(1,),
        in_specs=[pl.BlockSpec((s, LANES), lambda i: (0, fblk)), pl.BlockSpec((1, LANES), lambda i: (0, 0)),
                  pl.BlockSpec((pairs, 8, s), lambda i: (0, 0, 0)), pl.BlockSpec((pairs, s, LANES), lambda i: (0, 0, 0))],
        out_specs=[pl.BlockSpec((s, LANES), lambda i: (0, 0)), pl.BlockSpec((1, LANES), lambda i: (0, 0))],
        out_shape=[jax.ShapeDtypeStruct((s, LANES), F32), jax.ShapeDtypeStruct((1, LANES), F32)],
        compiler_params=_params("arbitrary"),
    )(h, bf_pad, dct, dcq)


def _pick_lane(blk, idx):
    lane = lax.broadcasted_iota(jnp.int32, blk.shape, 1)
    return jnp.sum(jnp.where(lane == idx, blk, 0.0), axis=1, keepdims=True)


def _pick_row(blk, idx):
    sub = lax.broadcasted_iota(jnp.int32, blk.shape, 0)
    return jnp.sum(jnp.where(sub == idx, blk, 0.0), axis=0, keepdims=True)


def _attention_fwd(h, c, ct, qblk, name):
    s = h.shape[0]
    t = _tile(s, ATT_TILE)
    nq = s // t
    scale = HEAD_DIM ** -0.5
    nt_dims = (((1,), (1,)), ((), ()))

    def body(q_ref, k_ref, v_ref, c_ref, ct_ref, o_ref, lse_ref):
        p = pl.program_id(0)
        i = pl.program_id(1)
        lane = lax.broadcasted_iota(jnp.int32, (1, LANES), 1)
        first = lane < HEAD_DIM
        q = q_ref[...] * (scale * LOG2E)
        qa = jnp.where(first, q, 0.0).astype(BF16)
        qb = jnp.where(first, 0.0, q).astype(BF16)
        cblk = c_ref[...]
        cta = _pick_lane(cblk, 2 * p) * LOG2E
        ctb = _pick_lane(cblk, 2 * p + 1) * LOG2E

        def step(j, carry, diagonal):
            ma, la, mb, lb, acc = carry
            off = pl.multiple_of(j * t, t)
            k = k_ref[pl.ds(off, t), :].astype(BF16)
            v = v_ref[pl.ds(off, t), :].astype(BF16)
            crow = ct_ref[:, pl.ds(off, t)] * LOG2E

            def one(qh, cth, hd, m_old, l_old):
                sc = lax.dot_general(qh, k, nt_dims, preferred_element_type=F32) - _pick_row(crow, hd)
                if diagonal:
                    keep = lax.broadcasted_iota(jnp.int32, (t, t), 0) >= lax.broadcasted_iota(jnp.int32, (t, t), 1)
                    sc = jnp.where(keep, sc, NEG)
                m_new = jnp.maximum(m_old, jnp.max(sc, axis=1, keepdims=True) + cth)
                pr = jnp.exp2(sc - (m_new - cth))
                alpha = jnp.exp2(m_old - m_new)
                l_new = alpha * l_old + jnp.sum(pr, axis=1, keepdims=True)
                pv = jnp.dot(pr.astype(BF16), v, preferred_element_type=F32)
                return m_new, l_new, alpha, pv

            ma2, la2, aa, pva = one(qa, cta, 2 * p, ma, la)
            mb2, lb2, ab, pvb = one(qb, ctb, 2 * p + 1, mb, lb)
            acc = jnp.where(first, aa * acc + pva, ab * acc + pvb)
            return ma2, la2, mb2, lb2, acc

        init = (jnp.full((t, 1), NEG, F32), jnp.zeros((t, 1), F32), jnp.full((t, 1), NEG, F32),
                jnp.zeros((t, 1), F32), jnp.zeros((t, LANES), F32))
        carry = lax.fori_loop(0, i, lambda j, carry: step(j, carry, False), init)
        ma, la, mb, lb, acc = step(i, carry, True)
        o_ref[...] = (acc / jnp.where(first, la, lb)).astype(BF16)
        lse_ref[0] = jnp.broadcast_to(ma + jnp.log2(la), (t, LANES))
        lse_ref[1] = jnp.broadcast_to(mb + jnp.log2(lb), (t, LANES))

    return pl.pallas_call(
        body, name=name, grid=(N_HEADS // 2, nq),
        in_specs=[pl.BlockSpec((t, LANES), lambda p, i: (i, qblk + p)),
                  pl.BlockSpec((s, LANES), lambda p, i: (0, qblk + 4 + p)),
                  pl.BlockSpec((s, LANES), lambda p, i: (0, qblk + 8 + p)),
                  pl.BlockSpec((t, LANES), lambda p, i: (i, 0)),
                  pl.BlockSpec((8, s), lambda p, i: (0, 0))],
        out_specs=[pl.BlockSpec((t, LANES), lambda p, i: (i, p)),
                   pl.BlockSpec((2, t, LANES), lambda p, i: (p, i, 0))],
        out_shape=[jax.ShapeDtypeStruct((s, D_ATT), BF16), jax.ShapeDtypeStruct((N_HEADS, s, LANES), F32)],
        compiler_params=_params("parallel", "arbitrary"),
    )(h, h, h, c, ct)


def _attention_bwd(h, c, ct, lse, att, datt, after, qblk, name):
    s = h.shape[0]
    t = _tile(s, ATT_TILE)
    nq = s // t
    scale = HEAD_DIM ** -0.5
    nt_dims = (((1,), (1,)), ((), ()))
    tn_dims = (((0,), (0,)), ((), ()))

    def body(q_ref, k_ref, v_ref, c_ref, ct_ref, lse_ref, o_ref, do_ref, after_ref, dq_ref, dk_ref, dv_ref, dct_ref, dcq_ref):
        p = pl.program_id(0)
        j = pl.program_id(1)
        lane = lax.broadcasted_iota(jnp.int32, (1, LANES), 1)
        first = lane < HEAD_DIM
        kf = k_ref[...]
        vf = v_ref[...]
        k = kf.astype(BF16)
        ka = jnp.where(first, kf, 0.0).astype(BF16)
        kb = jnp.where(first, 0.0, kf).astype(BF16)
        va = jnp.where(first, vf, 0.0).astype(BF16)
        vb = jnp.where(first, 0.0, vf).astype(BF16)
        crow = ct_ref[...] * LOG2E
        csa = _pick_row(crow, 2 * p)
        csb = _pick_row(crow, 2 * p + 1)

        @pl.when(j == 0)
        def _():
            dq_ref[...] = jnp.zeros_like(dq_ref)
            dcq_ref[...] = jnp.zeros_like(dcq_ref)

        def step(i, carry, diagonal):
            dka, dkb, dva, dvb, dca, dcb = carry
            off = pl.multiple_of(i * t, t)
            rows = pl.ds(off, t)
            q = (q_ref[rows, :] * (scale * LOG2E)).astype(BF16)
            dof = do_ref[rows, :]
            do = dof.astype(BF16)
            prod = dof * o_ref[rows, :].astype(F32)
            cblk = c_ref[rows, :] * LOG2E

            def one(kh, vh, hd, csh, lse_h):
                sc = lax.dot_general(q, kh, nt_dims, preferred_element_type=F32) - csh
                if diagonal:
                    keep = lax.broadcasted_iota(jnp.int32, (t, t), 0) >= lax.broadcasted_iota(jnp.int32, (t, t), 1)
                    sc = jnp.where(keep, sc, NEG)
                pr = jnp.exp2(sc - (jnp.max(lse_h, axis=1, keepdims=True) - _pick_lane(cblk, hd)))
                dp = lax.dot_general(do, vh, nt_dims, preferred_element_type=F32)
                return pr, dp

            pra, dpa = one(ka, va, 2 * p, csa, lse_ref[0, rows, :])
            prb, dpb = one(kb, vb, 2 * p + 1, csb, lse_ref[1, rows, :])
            dela = jnp.sum(jnp.where(first, prod, 0.0), axis=1, keepdims=True)
            delb = jnp.sum(jnp.where(first, 0.0, prod), axis=1, keepdims=True)
            dsa = pra * (dpa - dela)
            dsb = prb * (dpb - delb)
            dsa16 = dsa.astype(BF16)
            dsb16 = dsb.astype(BF16)
            dva = dva + lax.dot_general(pra.astype(BF16), do, tn_dims, preferred_element_type=F32)
            dvb = dvb + lax.dot_general(prb.astype(BF16), do, tn_dims, preferred_element_type=F32)
            dka = dka + lax.dot_general(dsa16, q, tn_dims, preferred_element_type=F32)
            dkb = dkb + lax.dot_general(dsb16, q, tn_dims, preferred_element_type=F32)
            dqa = jnp.dot(dsa16, k, preferred_element_type=F32)
            dqb = jnp.dot(dsb16, k, preferred_element_type=F32)
            dq_ref[rows, :] += scale * jnp.where(first, dqa, dqb)
            dca = dca - jnp.sum(dsa, axis=0, keepdims=True)
            dcb = dcb - jnp.sum(dsb, axis=0, keepdims=True)
            dcq_ref[rows, :] += jnp.where(lane == 0, jnp.sum(dsa, axis=1, keepdims=True),
                                          jnp.where(lane == 1, jnp.sum(dsb, axis=1, keepdims=True), 0.0))
            return dka, dkb, dva, dvb, dca, dcb

        z = jnp.zeros((t, LANES), F32)
        zr = jnp.zeros((1, t), F32)
        carry = step(j, (z, z, z, z, zr, zr), True)
        dka, dkb, dva, dvb, dca, dcb = lax.fori_loop(j + 1, nq, lambda i, carry: step(i, carry, False), carry)
        dk_ref[...] = jnp.where(first, dka, dkb) * (1.0 / LOG2E)
        dv_ref[...] = jnp.where(first, dva, dvb)
        sub = lax.broadcasted_iota(jnp.int32, (8, t), 0)
        dct_ref[...] = jnp.where(sub == 0, dca, jnp.where(sub == 1, dcb, 0.0))

    full = lambda blk: pl.BlockSpec((s, LANES), blk)
    return pl.pallas_call(
        body, name=name, grid=(N_HEADS // 2, nq),
        in_specs=[full(lambda p, j: (0, qblk + p)),
                  pl.BlockSpec((t, LANES), lambda p, j: (j, qblk + 4 + p)),
                  pl.BlockSpec((t, LANES), lambda p, j: (j, qblk + 8 + p)),
                  full(lambda p, j: (0, 0)),
                  pl.BlockSpec((8, t), lambda p, j: (0, j)),
                  pl.BlockSpec((2, s, LANES), lambda p, j: (p, 0, 0)),
                  full(lambda p, j: (0, p)),
                  full(lambda p, j: (0, p)),
                  pl.BlockSpec((8, LANES), lambda p, j: (0, 0))],
        out_specs=[full(lambda p, j: (0, p)),
                   pl.BlockSpec((t, LANES), lambda p, j: (j, p)),
                   pl.BlockSpec((t, LANES), lambda p, j: (j, p)),
                   pl.BlockSpec((None, 8, t), lambda p, j: (p, 0, j)),
                   pl.BlockSpec((None, s, LANES), lambda p, j: (p, 0, 0))],
        out_shape=[jax.ShapeDtypeStruct((s, D_ATT), F32), jax.ShapeDtypeStruct((s, D_ATT), F32),
                   jax.ShapeDtypeStruct((s, D_ATT), F32), jax.ShapeDtypeStruct((N_HEADS // 2, 8, s), F32),
                   jax.ShapeDtypeStruct((N_HEADS // 2, s, LANES), F32)],
        compiler_params=_params("arbitrary", "arbitrary"),
    )(h, h, h, c, ct, lse, att, datt, after)


def _sconv_fwd(h, w, bgblk, name):
    s = h.shape[0]
    nblk = D_CONV // LANES

    def body(bg_ref, cg_ref, hc_ref, w_ref, y_ref):
        rows = lax.broadcasted_iota(jnp.int32, (s, LANES), 0)
        y_ref[...] = (bg_ref[...] * _conv3(cg_ref[...] * hc_ref[...], w_ref, rows)).astype(BF16)

    col = lambda base: pl.BlockSpec((s, LANES), lambda j: (0, base + j))
    return pl.pallas_call(
        body, name=name, grid=(nblk,),
        in_specs=[col(bgblk), col(bgblk + nblk), col(bgblk + 2 * nblk), pl.BlockSpec((3, LANES), lambda j: (0, j))],
        out_specs=pl.BlockSpec((s, LANES), lambda j: (0, j)),
        out_shape=jax.ShapeDtypeStruct((s, D_CONV), BF16), compiler_params=_params("parallel"),
    )(h, h, h, w)


def _sconv_bwd(h, w, dy, bgblk, name):
    s = h.shape[0]
    nblk = D_CONV // LANES

    def body(bg_ref, cg_ref, hc_ref, w_ref, dy_ref, dbg_ref, dcg_ref, dhc_ref, dw_ref):
        rows = lax.broadcasted_iota(jnp.int32, (s, LANES), 0)
        cg, hc, dy, w = cg_ref[...], hc_ref[...], dy_ref[...], w_ref
        xin = cg * hc
        dbg_ref[...] = (dy * _conv3(xin, w, rows)).astype(BF16)
        dxin, dw_ref[...] = _conv3_bwd(dy * bg_ref[...], xin, w, rows)
        dcg_ref[...] = (dxin * hc).astype(BF16)
        dhc_ref[...] = (dxin * cg).astype(BF16)

    col = lambda base: pl.BlockSpec((s, LANES), lambda j: (0, base + j))
    return pl.pallas_call(
        body, name=name, grid=(nblk,),
        in_specs=[col(bgblk), col(bgblk + nblk), col(bgblk + 2 * nblk), pl.BlockSpec((3, LANES), lambda j: (0, j)), col(0)],
        out_specs=[col(0), col(0), col(0), pl.BlockSpec((8, LANES), lambda j: (0, j))],
        out_shape=[jax.ShapeDtypeStruct((s, D_CONV), BF16)] * 3 + [jax.ShapeDtypeStruct((8, D_CONV), F32)],
        compiler_params=_params("parallel"),
    )(h, h, h, w, dy)


def _sgu_group_masks():
    lane = lax.broadcasted_iota(jnp.int32, (1, D_SGU), 1)
    return [(lane // HEAD_DIM) == g for g in range(N_SGU_GROUPS)]


def _sgu_tril():
    r = lax.broadcasted_iota(jnp.int32, (SGU_CHUNK, SGU_CHUNK), 0)
    c = lax.broadcasted_iota(jnp.int32, (SGU_CHUNK, SGU_CHUNK), 1)
    return r >= c


def _sgu_fwd(h, ln_g, ln_b, w_s, b_full, ublk, name):
    s = h.shape[0]
    tr = _tile(s, 512)
    nch = tr // SGU_CHUNK

    def body(u_ref, v_ref, g_ref, b_ref, w_ref, bf_ref, y_ref):
        masks = _sgu_group_masks()
        tril = _sgu_tril()
        wm = [jnp.where(tril, w_ref[g], 0.0).astype(BF16) for g in range(N_SGU_GROUPS)]
        vn = _layer_norm(_gelu(v_ref[...]), g_ref[...], b_ref[...])
        for ch in range(nch):
            rows = pl.ds(ch * SGU_CHUNK, SGU_CHUNK)
            vc = vn[ch * SGU_CHUNK:(ch + 1) * SGU_CHUNK, :]
            mixed = bf_ref[...]
            for g in range(N_SGU_GROUPS):
                mixed = mixed + jnp.dot(wm[g], jnp.where(masks[g], vc, 0.0).astype(BF16), preferred_element_type=F32)
            y_ref[rows, :] = (_gelu(u_ref[rows, :]) * mixed).astype(BF16)

    row = lambda blk: pl.BlockSpec((tr, D_SGU), lambda i: (i, blk))
    vec = pl.BlockSpec((1, D_SGU), lambda i: (0, 0))
    return pl.pallas_call(
        body, name=name, grid=(s // tr,),
        in_specs=[row(ublk), row(ublk + 1), vec, vec,
                  pl.BlockSpec((N_SGU_GROUPS, SGU_CHUNK, SGU_CHUNK), lambda i: (0, 0, 0)),
                  pl.BlockSpec((SGU_CHUNK, D_SGU), lambda i: (0, 0))],
        out_specs=row(0), out_shape=jax.ShapeDtypeStruct((s, D_SGU), BF16), compiler_params=_params("parallel"),
    )(h, h, ln_g, ln_b, w_s, b_full)


def _sgu_bwd(h, ln_g, ln_b, w_s, b_full, dy, ublk, name):
    s = h.shape[0]
    tr = _tile(s, 512)
    nch = tr // SGU_CHUNK
    nt_dims = (((1,), (1,)), ((), ()))

    def norm(v, g, b):
        return _layer_norm(_gelu(v), g, b)

    def body(u_ref, v_ref, g_ref, b_ref, w_ref, bf_ref, dy_ref, du_ref, dv_ref, dg_ref, db_ref, dw_ref, dbf_ref):
        masks = _sgu_group_masks()
        tril = _sgu_tril()
        wf = [jnp.where(tril, w_ref[g], 0.0) for g in range(N_SGU_GROUPS)]
        wm = [w.astype(BF16) for w in wf]
        wmt = [jnp.transpose(w).astype(BF16) for w in wf]
        vn, vjp = jax.vjp(norm, v_ref[...], g_ref[...], b_ref[...])

        @pl.when(pl.program_id(0) == 0)
        def _():
            dg_ref[...] = jnp.zeros_like(dg_ref)
            db_ref[...] = jnp.zeros_like(db_ref)
            dw_ref[...] = jnp.zeros_like(dw_ref)
            dbf_ref[...] = jnp.zeros_like(dbf_ref)

        dvn_parts = []
        for ch in range(nch):
            rows = pl.ds(ch * SGU_CHUNK, SGU_CHUNK)
            vc = vn[ch * SGU_CHUNK:(ch + 1) * SGU_CHUNK, :]
            vc16 = vc.astype(BF16)
            mixed = bf_ref[...]
            for g in range(N_SGU_GROUPS):
                mixed = mixed + jnp.dot(wm[g], jnp.where(masks[g], vc, 0.0).astype(BF16), preferred_element_type=F32)
            dy = dy_ref[rows, :]
            ug, slope = _gelu_and_slope(u_ref[rows, :])
            du_ref[rows, :] = (dy * mixed * slope).astype(BF16)
            dmixed = dy * ug
            dbf_ref[...] += dmixed
            dvc = jnp.zeros((SGU_CHUNK, D_SGU), F32)
            for g in range(N_SGU_GROUPS):
                dm16 = jnp.where(masks[g], dmixed, 0.0).astype(BF16)
                dw_ref[g] += jnp.where(tril, lax.dot_general(dm16, vc16, nt_dims, preferred_element_type=F32), 0.0)
                dvc = dvc + jnp.dot(wmt[g], dm16, preferred_element_type=F32)
            dvn_parts.append(dvc)
        dv, dg, db = vjp(jnp.concatenate(dvn_parts, axis=0))
        dv_ref[...] = dv.astype(BF16)
        dg_ref[...] += dg
        db_ref[...] += db

    row = lambda blk: pl.BlockSpec((tr, D_SGU), lambda i: (i, blk))
    vec = pl.BlockSpec((1, D_SGU), lambda i: (0, 0))
    wsp = pl.BlockSpec((N_SGU_GROUPS, SGU_CHUNK, SGU_CHUNK), lambda i: (0, 0, 0))
    bsp = pl.BlockSpec((SGU_CHUNK, D_SGU), lambda i: (0, 0))
    return pl.pallas_call(
        body, name=name, grid=(s // tr,),
        in_specs=[row(ublk), row(ublk + 1), vec, vec, wsp, bsp, row(0)],
        out_specs=[row(0), row(0), vec, vec, wsp, bsp],
        out_shape=[jax.ShapeDtypeStruct((s, D_SGU), BF16), jax.ShapeDtypeStruct((s, D_SGU), BF16),
                   jax.ShapeDtypeStruct((1, D_SGU), F32), jax.ShapeDtypeStruct((1, D_SGU), F32),
                   jax.ShapeDtypeStruct((N_SGU_GROUPS, SGU_CHUNK, SGU_CHUNK), F32),
                   jax.ShapeDtypeStruct((SGU_CHUNK, D_SGU), F32)],
        compiler_params=_params("arbitrary"),
    )(h, h, ln_g, ln_b, w_s, b_full, dy)


def _merge_fwd(h, b_gate, att, yc, ys, wa, wc, ws, w_out, x, g_post, g_next, name):
    s, d = att.shape[0], wa.shape[1]
    tm = _tile(s, 512)

    def body(g0_ref, g1_ref, g2_ref, bg_ref, a_ref, c_ref, s_ref, wa_ref, wc_ref, ws_ref, wo_ref, x_ref, gp_ref, gn_ref,
             m_ref, o_ref, x1_ref, xn_ref):
        acc = jax.nn.sigmoid(g0_ref[...] + bg_ref[0:1, :]) * jnp.dot(a_ref[...], wa_ref[...], preferred_element_type=F32)
        acc += jax.nn.sigmoid(g1_ref[...] + bg_ref[1:2, :]) * jnp.dot(c_ref[...], wc_ref[...], preferred_element_type=F32)
        acc += jax.nn.sigmoid(g2_ref[...] + bg_ref[2:3, :]) * jnp.dot(s_ref[...], ws_ref[...], preferred_element_type=F32)
        merged = acc.astype(BF16)
        m_ref[...] = merged
        o = jnp.dot(merged, wo_ref[...], preferred_element_type=F32)
        o_ref[...] = o
        x1 = x_ref[...] + _rms(o, gp_ref[...])
        x1_ref[...] = x1
        xn_ref[...] = _rms(x1, gn_ref[...]).astype(BF16)

    gate = lambda b: pl.BlockSpec((tm, d), lambda i: (i, b))
    act = lambda k: pl.BlockSpec((tm, k), lambda i: (i, 0))
    wgt = lambda k: pl.BlockSpec((k, d), lambda i: (0, 0), pipeline_mode=pl.Buffered(1))
    vec = pl.BlockSpec((1, d), lambda i: (0, 0))
    return pl.pallas_call(
        body, name=name, grid=(s // tm,),
        in_specs=[gate(0), gate(1), gate(2), pl.BlockSpec((3, d), lambda i: (0, 0)),
                  act(D_ATT), act(D_CONV), act(D_SGU), wgt(D_ATT), wgt(D_CONV), wgt(D_SGU), wgt(d), act(d), vec, vec],
        out_specs=[act(d)] * 4,
        out_shape=[jax.ShapeDtypeStruct((s, d), BF16), jax.ShapeDtypeStruct((s, d), F32), jax.ShapeDtypeStruct((s, d), F32),
                   jax.ShapeDtypeStruct((s, d), BF16)],
        compiler_params=_params("parallel"),
    )(h, h, h, b_gate, att, yc, ys, wa, wc, ws, w_out, x, g_post, g_next)


def _merge_bwd(h, b_gate, att, yc, ys, wa, wc, ws, dm, name):
    s, d = att.shape[0], wa.shape[1]
    tm = _tile(s, 512)
    nt_dims, tn_dims = (((1,), (1,)), ((), ())), (((0,), (0,)), ((), ()))
    widths = (D_ATT, D_CONV, D_SGU)

    def body(g0_ref, g1_ref, g2_ref, bg_ref, a_ref, c_ref, s_ref, wa_ref, wc_ref, ws_ref, dm_ref,
             da_ref, dc_ref, ds_ref, ga_ref, gc_ref, gs_ref, dgl_ref, dbg_ref, acc_a, acc_c, acc_s):
        i = pl.program_id(0)

        @pl.when(i == 0)
        def _():
            dbg_ref[...] = jnp.zeros_like(dbg_ref)
            for acc in (acc_a, acc_c, acc_s):
                acc[...] = jnp.zeros_like(acc)

        dm = dm_ref[...]
        sums = []
        for b, (g_ref, x_ref, w_ref, dx_ref, acc) in enumerate((
                (g0_ref, a_ref, wa_ref, da_ref, acc_a), (g1_ref, c_ref, wc_ref, dc_ref, acc_c),
                (g2_ref, s_ref, ws_ref, ds_ref, acc_s))):
            gate = jax.nn.sigmoid(g_ref[...] + bg_ref[b:b + 1, :])
            x, w = x_ref[...], w_ref[...]
            y = jnp.dot(x, w, preferred_element_type=F32)
            dy = (dm * gate).astype(BF16)
            dgl = dm * y * gate * (1.0 - gate)
            dgl_ref[b] = dgl.astype(BF16)
            sums.append(jnp.sum(dgl, axis=0, keepdims=True))
            dx_ref[...] = lax.dot_general(dy, w, nt_dims, preferred_element_type=F32)
            acc[...] += lax.dot_general(x, dy, tn_dims, preferred_element_type=F32)
        sub = lax.broadcasted_iota(jnp.int32, (3, d), 0)
        dbg_ref[...] += jnp.where(sub == 0, sums[0], jnp.where(sub == 1, sums[1], sums[2]))

        @pl.when(i == pl.num_programs(0) - 1)
        def _():
            for g_out, acc in ((ga_ref, acc_a), (gc_ref, acc_c), (gs_ref, acc_s)):
                g_out[...] = acc[...].astype(BF16)

    gate = lambda b: pl.BlockSpec((tm, d), lambda i: (i, b))
    act = lambda k: pl.BlockSpec((tm, k), lambda i: (i, 0))
    wgt = lambda k: pl.BlockSpec((k, d), lambda i: (0, 0), pipeline_mode=pl.Buffered(1))
    res = pl.pallas_call(
        body, name=name, grid=(s // tm,),
        in_specs=[gate(0), gate(1), gate(2), pl.BlockSpec((3, d), lambda i: (0, 0)),
                  act(D_ATT), act(D_CONV), act(D_SGU), wgt(D_ATT), wgt(D_CONV), wgt(D_SGU), act(d)],
        out_specs=[act(k) for k in widths] + [pl.BlockSpec((k, d), lambda i: (0, 0)) for k in widths]
        + [pl.BlockSpec((3, tm, d), lambda i: (0, i, 0)), pl.BlockSpec((3, d), lambda i: (0, 0))],
        out_shape=[jax.ShapeDtypeStruct((s, k), F32) for k in widths] + [jax.ShapeDtypeStruct((k, d), BF16) for k in widths]
        + [jax.ShapeDtypeStruct((3, s, d), BF16), jax.ShapeDtypeStruct((3, d), F32)],
        scratch_shapes=[pltpu.VMEM((k, d), F32) for k in widths],
        compiler_params=_params("arbitrary"),
    )(h, h, h, b_gate, att, yc, ys, wa, wc, ws, dm)
    return res[0:3], res[3:6], res[6], res[7]


def _ffn_act_fwd(hh, cw, name):
    s, dff = hh.shape[0], hh.shape[1] // 2
    nblk = dff // LANES

    def body(a_ref, b_ref, wa_ref, wb_ref, z_ref):
        rows = lax.broadcasted_iota(jnp.int32, (s, LANES), 0)
        z_ref[...] = (_gelu(_conv3(a_ref[...], wa_ref, rows)) * _conv3(b_ref[...], wb_ref, rows)).astype(BF16)

    col = lambda base: pl.BlockSpec((s, LANES), lambda j: (0, base + j))
    wsp = lambda base: pl.BlockSpec((3, LANES), lambda j: (0, base + j))
    return pl.pallas_call(
        body, name=name, grid=(nblk,), in_specs=[col(0), col(nblk), wsp(0), wsp(nblk)], out_specs=col(0),
        out_shape=jax.ShapeDtypeStruct((s, dff), BF16), compiler_params=_params("parallel"),
    )(hh, hh, cw, cw)


def _ffn_act_bwd(hh, cw, dz, name):
    s, dff = hh.shape[0], hh.shape[1] // 2
    nblk = dff // LANES

    def body(a_ref, b_ref, wa_ref, wb_ref, dz_ref, da_ref, db_ref, dwa_ref, dwb_ref):
        rows = lax.broadcasted_iota(jnp.int32, (s, LANES), 0)
        a, b, dz = a_ref[...], b_ref[...], dz_ref[...]
        ga, slope = _gelu_and_slope(_conv3(a, wa_ref, rows))
        da, dwa_ref[...] = _conv3_bwd(dz * _conv3(b, wb_ref, rows) * slope, a, wa_ref, rows)
        db, dwb_ref[...] = _conv3_bwd(dz * ga, b, wb_ref, rows)
        da_ref[...] = da.astype(BF16)
        db_ref[...] = db.astype(BF16)

    col = lambda base: pl.BlockSpec((s, LANES), lambda j: (0, base + j))
    wsp = lambda base: pl.BlockSpec((3, LANES), lambda j: (0, base + j))
    w8 = lambda base: pl.BlockSpec((8, LANES), lambda j: (0, base + j))
    return pl.pallas_call(
        body, name=name, grid=(nblk,), in_specs=[col(0), col(nblk), wsp(0), wsp(nblk), col(0)],
        out_specs=[col(0), col(0), w8(0), w8(0)],
        out_shape=[jax.ShapeDtypeStruct((s, dff), BF16)] * 2 + [jax.ShapeDtypeStruct((8, dff), F32)] * 2,
        compiler_params=_params("parallel"),
    )(hh, hh, cw, cw, dz)


ANY = pl.BlockSpec(memory_space=pl.ANY)


def _place():
    return lax.axis_index("x"), lax.axis_index("y"), lax.axis_index("c")


def _all_gather(arrs, name):
    n = len(arrs)

    def body(*refs):
        ins, outs = refs[:n], refs[n:2 * n]
        send_sems, recv_sems, local_sems = refs[2 * n:]
        x, y, c = _place()
        me, sibling = (x, y, c), (x, y, 1 - c)
        chips = [(1 - x, y), (x, 1 - y), (1 - x, 1 - y)]

        def slab(a, dev):
            return outs[a].at[4 * dev[0] + 2 * dev[1] + dev[2]]

        def copy(a, k, block, to, src=None):
            return pltpu.make_async_remote_copy(
                src_ref=slab(a, block) if src is None else src, dst_ref=slab(a, block),
                send_sem=send_sems.at[7 * a + k], recv_sem=recv_sems.at[7 * a + k], device_id=to, device_id_type=MESH)

        mine = [pltpu.make_async_copy(ins[a], slab(a, me), local_sems.at[a]) for a in range(n)]
        for cp in mine:
            cp.start()
        first = []
        for a in range(n):
            first.append(copy(a, 0, me, sibling, src=ins[a]))
            first += [copy(a, 1 + j, me, (*chip, c), src=ins[a]) for j, chip in enumerate(chips)]
        for cp in first:
            cp.start()
        passed = []
        for a in range(n):
            for j, chip in enumerate(chips):
                copy(a, 1 + j, (*chip, c), me).wait_recv()
                fwd = copy(a, 4 + j, (*chip, c), sibling)
                fwd.start()
                passed.append(fwd)
        for a in range(n):
            copy(a, 0, sibling, me).wait_recv()
            for j, chip in enumerate(chips):
                copy(a, 4 + j, (*chip, 1 - c), me).wait_recv()
        for cp in first + passed:
            cp.wait_send()
        for cp in mine:
            cp.wait()

    return pl.pallas_call(
        body, name=name, in_specs=[ANY] * n, out_specs=[ANY] * n,
        out_shape=[jax.ShapeDtypeStruct((N_DEV,) + a.shape, a.dtype) for a in arrs],
        scratch_shapes=[pltpu.SemaphoreType.DMA((7 * n,)), pltpu.SemaphoreType.DMA((7 * n,)), pltpu.SemaphoreType.DMA((n,))],
    )(*arrs)


HBM = pl.BlockSpec(memory_space=pltpu.HBM)
SEM = pl.BlockSpec(memory_space=pltpu.SEMAPHORE)
EFFECT = pltpu.SideEffectType.DATAFLOW_SIDE_EFFECTING


def _slot(dev):
    return 4 * dev[0] + 2 * dev[1] + dev[2]


def _exchange_copies(src_refs, land_refs, send_sems, recv_sems, src_view, land_view):
    x, y, c = _place()
    me = (x, y, c)
    peers = [(1 - x if r & 4 else x, 1 - y if r & 2 else y, 1 - c if r & 1 else c) for r in range(1, N_DEV)]
    sends, lands = [], []
    for a, (src, land) in enumerate(zip(src_refs, land_refs)):
        for k, peer in enumerate(peers):
            sems = dict(send_sem=send_sems.at[7 * a + k], recv_sem=recv_sems.at[7 * a + k], device_id=peer,
                        device_id_type=MESH)
            sends.append(pltpu.make_async_remote_copy(src_ref=src_view(src, _slot(peer)),
                                                      dst_ref=land_view(land, _slot(me)), **sems))
            lands.append(pltpu.make_async_remote_copy(src_ref=src_view(src, _slot(me)),
                                                      dst_ref=land_view(land, _slot(peer)), **sems))
    return sends, lands


def _own_copies(src_refs, land_refs, local_sems, src_view, land_view):
    me = _slot(_place())
    return [pltpu.make_async_copy(src_view(src, me), land_view(land, me), local_sems.at[a])
            for a, (src, land) in enumerate(zip(src_refs, land_refs))]


def _exchange_start(srcs, lands, after, src_view, land_view, name):
    n = len(srcs)

    def body(*refs):
        src_refs, land_refs = refs[:n], refs[n:2 * n]
        send_sems, recv_sems, local_sems = refs[2 * n + 1:2 * n + 4]
        token = refs[-1]
        sends, _ = _exchange_copies(src_refs, land_refs, send_sems, recv_sems, src_view, land_view)
        for cp in sends + _own_copies(src_refs, land_refs, local_sems, src_view, land_view):
            cp.start()
        token[...] = jnp.zeros_like(token)

    thru = [pltpu.HBM(a.shape, a.dtype) for a in list(srcs) + list(lands)]
    outs = pl.pallas_call(
        body, name=name,
        out_shape=(pltpu.SemaphoreType.DMA((7 * n,)), pltpu.SemaphoreType.DMA((7 * n,)), pltpu.SemaphoreType.DMA((n,)),
                   *thru, jax.ShapeDtypeStruct((8, LANES), F32)),
        in_specs=[HBM] * (2 * n) + [ANY],
        out_specs=(SEM, SEM, SEM, *([HBM] * (2 * n)), pl.BlockSpec(memory_space=pltpu.VMEM)),
        input_output_aliases={i: 3 + i for i in range(2 * n)},
        compiler_params=pltpu.CompilerParams(has_side_effects=EFFECT),
    )(*[pltpu.with_memory_space_constraint(a, pltpu.HBM) for a in list(srcs) + list(lands)], after)
    return outs[:3], list(outs[3:3 + n]), list(outs[3 + n:3 + 2 * n]), outs[-1]


def _exchange_wait(sems, srcs, lands, after, src_view, land_view, name):
    n = len(srcs)

    def body(*refs):
        src_refs, land_refs = refs[:n], refs[n:2 * n]
        send_sems, recv_sems, local_sems = refs[2 * n:2 * n + 3]
        sends, landed = _exchange_copies(src_refs, land_refs, send_sems, recv_sems, src_view, land_view)
        for cp in sends:
            cp.wait_send()
        for cp in landed:
            cp.wait_recv()
        for cp in _own_copies(src_refs, land_refs, local_sems, src_view, land_view):
            cp.wait()

    after = after if isinstance(after, (list, tuple)) else [after]
    outs = pl.pallas_call(
        body, name=name, out_shape=[pltpu.HBM(a.shape, a.dtype) for a in list(srcs) + list(lands)],
        in_specs=[HBM] * (2 * n) + [SEM, SEM, SEM] + [ANY] * len(after), out_specs=[HBM] * (2 * n),
        input_output_aliases={i: i for i in range(2 * n)},
        compiler_params=pltpu.CompilerParams(has_side_effects=EFFECT),
    )(*srcs, *lands, *sems, *after)
    return list(outs[:n]), list(outs[n:])


ADAMW_BLOCK_BYTES = 1 << 19
PACK_ROWS = 256


def _adamw(slabs, w, m, v, name):
    nl, r, c = w.shape
    row_edges = [r] + [t for t in range(8, r, 8) if r % t == 0]
    col_edges = [c] + [t for t in range(LANES, c, LANES) if c % t == 0]
    fits = [(tr * tc, tc, tr) for tr in row_edges for tc in col_edges if tr * tc * 4 <= ADAMW_BLOCK_BYTES]
    _, tc, tr = max(fits) if fits else (0, min(col_edges), min(row_edges))

    def body(s_ref, w_ref, m_ref, v_ref, g_ref, d_ref, nm_ref, nv_ref):
        g = s_ref[0].astype(F32)
        for q in range(1, N_DEV):
            g = g + s_ref[q].astype(F32)
        m_new = ADAM_B1 * m_ref[...] + (1.0 - ADAM_B1) * g
        v_new = ADAM_B2 * v_ref[...] + (1.0 - ADAM_B2) * (g * g)
        m_hat = m_new / (1.0 - ADAM_B1 ** ADAM_STEP)
        v_hat = v_new / (1.0 - ADAM_B2 ** ADAM_STEP)
        g_ref[...] = g
        d_ref[...] = -ADAM_LR * (m_hat / (jnp.sqrt(v_hat) + ADAM_EPS) + ADAM_WD * w_ref[...])
        nm_ref[...] = m_new
        nv_ref[...] = v_new

    blk = pl.BlockSpec((None, tr, tc), lambda l, i, j: (l, i, j))
    return pl.pallas_call(
        body, name=name, grid=(nl, r // tr, c // tc),
        in_specs=[pl.BlockSpec((N_DEV, None, tr, tc), lambda l, i, j: (0, l, i, j)), blk, blk, blk],
        out_specs=[blk] * 4, out_shape=[jax.ShapeDtypeStruct(w.shape, F32)] * 4,
        compiler_params=_params("parallel", "parallel", "parallel"),
    )(slabs, w, m, v)


def _layout(d):
    off = {"gate": 0, "q": 3 * d}
    off["bg"] = off["q"] + 3 * D_ATT
    off["u"] = off["bg"] + 3 * D_CONV
    off["f"] = off["u"] + 2 * D_SGU
    width = -(-(off["f"] + LANES) // 512) * 512
    return off, width


def _pad_w_in(wt, d, token):
    off, width = _layout(d)
    nqkv, nrest = 3 * D_ATT, 3 * D_CONV + 2 * D_SGU
    pad = jnp.zeros((width - off["f"] - N_HEADS, wt.shape[1]), wt.dtype) + token[0, 0].astype(wt.dtype)
    return jnp.concatenate([wt[nqkv + N_HEADS + nrest:], wt[:nqkv], wt[nqkv + N_HEADS:nqkv + N_HEADS + nrest],
                            wt[nqkv:nqkv + N_HEADS], pad], axis=0)


def _unpad_w_in(wtp, d):
    off, _ = _layout(d)
    return jnp.concatenate([wtp[off["q"]:off["bg"]], wtp[off["f"]:off["f"] + N_HEADS], wtp[off["bg"]:off["f"]],
                            wtp[:off["q"]]], axis=0)


def _cols_from_slabs(g):
    return jnp.transpose(g, (1, 0, 2)).reshape(g.shape[1], N_DEV * g.shape[2])


def _cols_to_slabs(w):
    r, c = w.shape[0], w.shape[1] // N_DEV
    return jnp.transpose(w.reshape(r, N_DEV, c), (1, 0, 2))


def kernel(x, pre_mix_g, post_mix_g, pre_ffn_g, post_ffn_g, w_in, b_forget, b_gate, conv_mix_w, sgu_ln_g, sgu_ln_b, sgu_w, sgu_b, w_branch_att, w_branch_conv, w_branch_sgu, w_out, w_ffn_up, conv_ffn_w, w_ffn_down, loss_target, m_pre_mix_g, m_post_mix_g, m_pre_ffn_g, m_post_ffn_g, m_w_in, m_b_forget, m_b_gate, m_conv_mix_w, m_sgu_ln_g, m_sgu_ln_b, m_sgu_w, m_sgu_b, m_w_branch_att, m_w_branch_conv, m_w_branch_sgu, m_w_out, m_w_ffn_up, m_conv_ffn_w, m_w_ffn_down, v_pre_mix_g, v_post_mix_g, v_pre_ffn_g, v_post_ffn_g, v_w_in, v_b_forget, v_b_gate, v_conv_mix_w, v_sgu_ln_g, v_sgu_ln_b, v_sgu_w, v_sgu_b, v_w_branch_att, v_w_branch_conv, v_w_branch_sgu, v_w_out, v_w_ffn_up, v_conv_ffn_w, v_w_ffn_down):
    depth = w_in.shape[0]
    s, d = x.shape[1], x.shape[2]
    dff = w_ffn_down.shape[1] * N_DEV
    off, _ = _layout(d)
    qblk, bgblk, ublk, fblk = off["q"] // LANES, off["bg"] // LANES, off["u"] // D_SGU, off["f"] // LANES
    x0 = x.reshape(s, d)
    target = loss_target.reshape(s, d)
    ncm, ncf = conv_mix_w.shape[2], conv_ffn_w.shape[2]

    lo = d // N_DEV

    whole = lambda ref, slot: ref
    slab = lambda ref, slot: ref.at[slot]

    w_in_t, w_up_t = jnp.transpose(w_in, (0, 2, 1)), jnp.transpose(w_ffn_up, (0, 2, 1))

    def shards_of(l, part):
        if part == "mix":
            small = jnp.concatenate([b_gate[l], conv_mix_w[l], conv_ffn_w[l]], axis=1)
            return [w_in_t[l].astype(BF16), w_branch_att[l].astype(BF16), w_branch_conv[l].astype(BF16),
                    w_branch_sgu[l].astype(BF16), w_out[l].astype(BF16), small]
        return [w_up_t[l].astype(BF16), w_ffn_down[l].astype(BF16)]

    def gather_start(l, part, after):
        shards = shards_of(l, part)
        lands = [lax.empty((N_DEV,) + a.shape, a.dtype) for a in shards]
        return _exchange_start(shards, lands, after, whole, slab, name=f"gather_start_{part}_{l}")

    def gather_finish(l, part, started, after):
        sems, shards, lands, _ = started[part]
        shards, lands = _exchange_wait(sems, shards, lands, after, whole, slab, name=f"gather_wait_{part}_{l}")
        token = jnp.zeros((8, LANES), F32)
        if l + 1 < depth:
            started[part] = gather_start(l + 1, part, lands[0])
            token = started[part][3]
        return lands, token

    def bfull(l):
        return jnp.repeat(jnp.transpose(sgu_b[l]), HEAD_DIM, axis=1)

    def bf_pad(l):
        return jnp.pad(b_forget[l], (0, LANES - N_HEADS)).reshape(1, LANES)

    saved = []
    weights = []
    xin = x0
    xn = _prenorm(x0, pre_mix_g[0:1], name="prenorm_first")
    loss_acc = dy = None
    first = _all_gather(shards_of(0, "mix"), name="gather_first")
    started = {"ffn": gather_start(0, "ffn", first[0])}
    for l in range(depth):
        if l == 0:
            (g_in, g_a, g_c, g_s, g_o, g_small), token = first, started["ffn"][3]
        else:
            (g_in, g_a, g_c, g_s, g_o, g_small), token = gather_finish(l, "mix", started, xin)
        g_small = _cols_from_slabs(g_small).reshape(3, N_DEV, -1)
        w = dict(w_in=_pad_w_in(g_in.reshape(N_DEV * g_in.shape[1], d), d, token), wa=_cols_from_slabs(g_a),
                 wc=_cols_from_slabs(g_c), ws=_cols_from_slabs(g_s), w_out=g_o.reshape(d, d),
                 b_gate=g_small[:, :, :lo].reshape(3, d), cmw=g_small[:, :, lo:lo + ncm].reshape(3, D_CONV),
                 cfw=g_small[:, :, lo + ncm:].reshape(3, 2 * dff))
        weights.append(w)
        h = _mm(xn, w["w_in"], "nt", F32, name="proj_in")
        c, ct = _forget_prep(h, bf_pad(l), fblk, name="forget_prep")
        att, lse = _attention_fwd(h, c, ct, qblk, name="attention_fwd")
        if l == 0 and depth > 1:
            started["mix"] = gather_start(1, "mix", att)
            w["cmw"] = w["cmw"] + started["mix"][3][0, 0]
        yc = _sconv_fwd(h, w["cmw"], bgblk, name="sconv_fwd")
        ys = _sgu_fwd(h, sgu_ln_g[l:l + 1], sgu_ln_b[l:l + 1], sgu_w[l], bfull(l), ublk, name="sgu_fwd")
        (g_up, g_dn), token = gather_finish(l, "ffn", started, ys)
        w["w_up"], w["w_dn"] = g_up.reshape(2 * dff, d), g_dn.reshape(dff, d)
        merged, o, x1, xn2 = _merge_fwd(h, w["b_gate"], att, yc, ys, w["wa"], w["wc"], w["ws"], w["w_out"], xin,
                                        post_mix_g[l:l + 1], pre_ffn_g[l:l + 1] + token[0, 0], name="merge_proj_out")
        hh = _mm(xn2, w["w_up"], "nt", F32, name="ffn_up")
        z = _ffn_act_fwd(hh, w["cfw"], name="ffn_act_fwd")
        layer = dict(xin=xin, xn=xn, h=h, c=c, ct=ct, lse=lse, att=att, yc=yc, ys=ys, merged=merged, o=o, x1=x1,
                     xn2=xn2, hh=hh, z=z)
        if l + 1 < depth:
            layer["f"], xin, xn = _mm_postnorm(z, w["w_dn"], x1, post_ffn_g[l:l + 1], pre_mix_g[l + 1:l + 2],
                                               name="ffn_down")
        else:
            layer["f"], dy, loss_acc = _mm_postnorm_loss(z, w["w_dn"], x1, post_ffn_g[l:l + 1], target,
                                                         name="ffn_down_loss")
        saved.append(layer)
    loss = lax.psum(loss_acc[0, 0] * (0.5 / d), ("x", "y", "c"))

    rep = {k: [None] * depth for k in ("pre_mix_g", "post_mix_g", "pre_ffn_g", "post_ffn_g", "b_forget", "sgu_ln_g",
                                       "sgu_ln_b", "sgu_w", "sgu_b")}
    nsmall = lo + ncm + ncf
    lands = {"win": [lax.empty((N_DEV, depth) + w_in_t.shape[1:], BF16)],
             "mid": [lax.empty((N_DEV, depth) + shp, dt) for shp, dt in (
                 (w_branch_att.shape[1:], BF16), (w_branch_conv.shape[1:], BF16), (w_branch_sgu.shape[1:], BF16),
                 (w_out.shape[1:], BF16), ((3, nsmall), F32))],
             "ffn": [lax.empty((N_DEV, depth) + shp, BF16) for shp in (w_up_t.shape[1:], w_ffn_down.shape[1:])]}
    scatters = {part: [None] * depth for part in lands}

    def scatter_start(l, part, sends, after):
        layer_slab = lambda ref, slot: ref.at[slot, l]
        sems, sends, lands[part], token = _exchange_start(sends, lands[part], after, slab, layer_slab,
                                                          name=f"scatter_start_{part}_{l}")
        scatters[part][l] = (sems, sends, layer_slab)
        return token

    def scatter_finish(part, after):
        for l in range(depth):
            sems, sends, layer_slab = scatters[part][l]
            _, lands[part] = _exchange_wait(sems, sends, lands[part], after, slab, layer_slab,
                                            name=f"scatter_wait_{part}_{l}")
        return lands[part]

    token = jnp.zeros((8, LANES), F32)
    dx = dy
    for l in reversed(range(depth)):
        w, a = weights[l], saved[l]
        df, rep["post_ffn_g"][l] = _postnorm_bwd(a["f"], post_ffn_g[l:l + 1] + token[0, 0], dx, name="postnorm_bwd")
        dz = _mm(df, w["w_dn"], "nt", F32, name="ffn_down_dx")
        g_dn = _mm(a["z"], df, "tn", BF16, name="ffn_down_dw")
        dha, dhb, dcwa, dcwb = _ffn_act_bwd(a["hh"], w["cfw"], dz, name="ffn_act_bwd")
        dhh = jnp.concatenate([dha, dhb], axis=1)
        dcfw = jnp.concatenate([dcwa[0:3], dcwb[0:3]], axis=1)
        g_up = _mm(dhh, a["xn2"], "tn", BF16, name="ffn_up_dw")
        token = scatter_start(l, "ffn", [g_up.reshape(N_DEV, 2 * dff // N_DEV, d), g_dn.reshape(N_DEV, dff // N_DEV, d)],
                              dz)
        dx1, rep["pre_ffn_g"][l] = _mm_prenorm_bwd(dhh, w["w_up"], a["x1"], pre_ffn_g[l:l + 1], dx, token, name="ffn_up_dx")
        do, rep["post_mix_g"][l], dmerged = _postnorm_bwd_mm(a["o"], post_mix_g[l:l + 1], dx1, w["w_out"], name="proj_out_dx")
        g_o = _mm(a["merged"], do, "tn", BF16, name="proj_out_dw")
        (datt, dconv, dsgu), (g_a, g_c, g_s), dgl, dbg = _merge_bwd(
            a["h"], w["b_gate"], a["att"], a["yc"], a["ys"], w["wa"], w["wc"], w["ws"], dmerged, name="merge_bwd")
        dbgate, dcg, dhc, dcmw = _sconv_bwd(a["h"], w["cmw"], dconv, bgblk, name="sconv_bwd")
        sends = [_cols_to_slabs(g_a), _cols_to_slabs(g_c), _cols_to_slabs(g_s), g_o.reshape(N_DEV, d // N_DEV, d),
                 jnp.concatenate([_cols_to_slabs(dbg), _cols_to_slabs(dcmw[0:3]), _cols_to_slabs(dcfw)], axis=2)]
        token = scatter_start(l, "mid", sends, dx1)
        dq, dk, dv, dct4, dcq4 = _attention_bwd(a["h"], a["c"], a["ct"], a["lse"], a["att"], datt, token, qblk,
                                                name="attention_bwd")
        dfl, dbf = _forget_prep_bwd(a["h"], bf_pad(l), dct4, dcq4, fblk, name="forget_prep_bwd")
        rep["b_forget"][l] = dbf[0, :N_HEADS]
        du, dvs, dlg, dlb, dsw, dbfull = _sgu_bwd(a["h"], sgu_ln_g[l:l + 1], sgu_ln_b[l:l + 1], sgu_w[l], bfull(l), dsgu,
                                                  ublk, name="sgu_bwd")
        rep["sgu_ln_g"][l], rep["sgu_ln_b"][l], rep["sgu_w"][l] = dlg, dlb, dsw
        rep["sgu_b"][l] = jnp.transpose(jnp.sum(dbfull.reshape(SGU_CHUNK, N_SGU_GROUPS, HEAD_DIM), axis=2))
        dh = jnp.concatenate([dgl[0], dgl[1], dgl[2], dq.astype(BF16), dk.astype(BF16), dv.astype(BF16), dbgate, dcg, dhc,
                              du, dvs, dfl.astype(BF16), jnp.zeros((s, w["w_in"].shape[0] - off["f"] - LANES), BF16)], axis=1)
        g_in = _mm(dh, a["xn"], "tn", BF16, name="proj_in_dw")
        token = scatter_start(l, "win", [_unpad_w_in(g_in, d).reshape(N_DEV, -1, d)], dx1)
        dx, rep["pre_mix_g"][l] = _mm_prenorm_bwd(dh, w["w_in"], a["xin"], pre_mix_g[l:l + 1], dx1, token, name="proj_in_dx")

    outs = {}
    t3 = lambda arr: jnp.transpose(arr, (0, 2, 1))

    def update(name_, slabs, w_, m_, v_, transposed=False):
        if transposed:
            w_, m_, v_ = t3(w_), t3(m_), t3(v_)
        shp = w_.shape
        w3 = w_.reshape((shp[0], -1, shp[-1])) if w_.ndim >= 3 else w_.reshape((1,) + shp)
        res = _adamw(slabs.reshape((N_DEV,) + w3.shape), w3, m_.reshape(w3.shape), v_.reshape(w3.shape),
                     name="adamw_" + name_)
        outs[name_] = tuple(t3(t.reshape(shp)) if transposed else t.reshape(shp) for t in res)
        return res[0]

    rep_names = ("pre_mix_g", "post_mix_g", "pre_ffn_g", "post_ffn_g", "b_forget", "sgu_ln_g", "sgu_ln_b", "sgu_w", "sgu_b")
    rep_w = dict(pre_mix_g=(pre_mix_g, m_pre_mix_g, v_pre_mix_g), post_mix_g=(post_mix_g, m_post_mix_g, v_post_mix_g),
                 pre_ffn_g=(pre_ffn_g, m_pre_ffn_g, v_pre_ffn_g), post_ffn_g=(post_ffn_g, m_post_ffn_g, v_post_ffn_g),
                 b_forget=(b_forget, m_b_forget, v_b_forget), sgu_ln_g=(sgu_ln_g, m_sgu_ln_g, v_sgu_ln_g),
                 sgu_ln_b=(sgu_ln_b, m_sgu_ln_b, v_sgu_ln_b), sgu_w=(sgu_w, m_sgu_w, v_sgu_w), sgu_b=(sgu_b, m_sgu_b, v_sgu_b))

    def pack(parts):
        rows = [jnp.pad(p.reshape(-1), (0, -p.size % LANES)).reshape(-1, LANES) for p in parts]
        rows = jnp.concatenate(rows, axis=0)
        return jnp.pad(rows, ((0, -rows.shape[0] % PACK_ROWS), (0, 0)))

    part = pack([jnp.stack([g.reshape(rep_w[k][0].shape[1:]) for g in rep[k]]) for k in rep_names])
    small_sems, small_src, small_land, _ = _exchange_start([part], [lax.empty((N_DEV,) + part.shape, F32)], dx, whole, slab,
                                                            name="gather_small_start")

    got_up, got_dn = scatter_finish("ffn", dx)
    done = [update("w_ffn_up", got_up, w_ffn_up, m_w_ffn_up, v_w_ffn_up, transposed=True),
            update("w_ffn_down", got_dn, w_ffn_down, m_w_ffn_down, v_w_ffn_down)]

    got_a, got_c, got_s, got_o, small = scatter_finish("mid", done)
    done = [update("w_branch_att", got_a, w_branch_att, m_w_branch_att, v_w_branch_att),
            update("w_branch_conv", got_c, w_branch_conv, m_w_branch_conv, v_w_branch_conv),
            update("w_branch_sgu", got_s, w_branch_sgu, m_w_branch_sgu, v_w_branch_sgu),
            update("w_out", got_o, w_out, m_w_out, v_w_out),
            update("b_gate", small[..., :lo], b_gate, m_b_gate, v_b_gate),
            update("conv_mix_w", small[..., lo:lo + ncm], conv_mix_w, m_conv_mix_w, v_conv_mix_w),
            update("conv_ffn_w", small[..., lo + ncm:], conv_ffn_w, m_conv_ffn_w, v_conv_ffn_w)]

    (got_in,) = scatter_finish("win", done)
    done = update("w_in", got_in, w_in, m_w_in, v_w_in, transposed=True)

    _, (gathered,) = _exchange_wait(small_sems, small_src, small_land, done, whole, slab, name="gather_small_wait")
    packed = [pack([rep_w[k][i] for k in rep_names]) for i in range(3)]
    res = _adamw(gathered.reshape(N_DEV, 1, -1, LANES), *[p.reshape(1, -1, LANES) for p in packed], name="adamw_replicated")
    row = 0
    for k in rep_names:
        shp = rep_w[k][0].shape
        size = math.prod(shp)
        nrows = -(-size // LANES)
        outs[k] = tuple(t[0, row:row + nrows].reshape(-1)[:size].reshape(shp) for t in res)
        row += nrows

    order = ("pre_mix_g", "post_mix_g", "pre_ffn_g", "post_ffn_g", "w_in", "b_forget", "b_gate", "conv_mix_w", "sgu_ln_g",
             "sgu_ln_b", "sgu_w", "sgu_b", "w_branch_att", "w_branch_conv", "w_branch_sgu", "w_out", "w_ffn_up",
             "conv_ffn_w", "w_ffn_down")
    grad_x = dx.reshape(x.shape)
    return (loss, grad_x, *[outs[k][0] for k in order], *[outs[k][1] for k in order], *[outs[k][2] for k in order],
            *[outs[k][3] for k in order])
```

```python
import math

import jax
import jax.numpy as jnp
from jax import lax
from jax.experimental import pallas as pl
from jax.experimental.pallas import tpu as pltpu

F32 = jnp.float32
BF16 = jnp.bfloat16

N_DEV = 8
HEAD_DIM = 64
N_HEADS = 8
D_ATT = 512
D_CONV = 256
D_SGU = 256
N_SGU_GROUPS = 4
SGU_CHUNK = 128
RMS_EPS = 1e-6
LN_EPS = 1e-5
ADAM_LR = 0.001
ADAM_B1 = 0.9
ADAM_B2 = 0.999
ADAM_EPS = 1e-08
ADAM_WD = 0.01
ADAM_STEP = 10
LANES = 128
VMEM_LIMIT = 56 * 1024 * 1024
ATT_TILE = 512
LOG2E = math.log2(math.e)
NEG = -1e30
MESH = pl.DeviceIdType.MESH


def _params(*sem):
    return pltpu.CompilerParams(dimension_semantics=sem if sem else None, vmem_limit_bytes=VMEM_LIMIT)


def _tile(n, cap):
    if n <= cap:
        return n
    t = cap - cap % LANES
    while n % t:
        t -= LANES
    return t


def _gelu(x):
    return 0.5 * x * (1.0 + jnp.tanh(math.sqrt(2.0 / math.pi) * (x + 0.044715 * (x * x * x))))


def _gelu_and_slope(x):
    k0, k1 = math.sqrt(2.0 / math.pi), 0.044715
    x2 = x * x
    t = jnp.tanh(x * (k0 + (k0 * k1) * x2))
    half = 0.5 * (1.0 + t)
    return x * half, half + (0.5 * x) * (1.0 - t * t) * (k0 + (3.0 * k0 * k1) * x2)


def _rms(x, g):
    r = lax.rsqrt(jnp.mean(x * x, axis=-1, keepdims=True) + RMS_EPS)
    return x * r * g


def _layer_norm(x, g, b):
    mu = jnp.mean(x, axis=-1, keepdims=True)
    xc = x - mu
    var = jnp.mean(xc * xc, axis=-1, keepdims=True)
    return xc * lax.rsqrt(var + LN_EPS) * g + b


def _shift_down(x, k, rows):
    return jnp.where(rows >= k, pltpu.roll(x, k, 0), 0.0)


def _shift_up(x, k, rows):
    s = x.shape[0]
    return jnp.where(rows < s - k, pltpu.roll(x, s - k, 0), 0.0)


def _conv3(x, w_ref, rows):
    return w_ref[2:3, :] * x + w_ref[1:2, :] * _shift_down(x, 1, rows) + w_ref[0:1, :] * _shift_down(x, 2, rows)


def _conv3_bwd(dy, x, w_ref, rows):
    up1, up2 = _shift_up(dy, 1, rows), _shift_up(dy, 2, rows)
    dx = w_ref[2:3, :] * dy + w_ref[1:2, :] * up1 + w_ref[0:1, :] * up2
    d2 = jnp.sum(dy * x, axis=0, keepdims=True)
    d1 = jnp.sum(up1 * x, axis=0, keepdims=True)
    d0 = jnp.sum(up2 * x, axis=0, keepdims=True)
    sub = lax.broadcasted_iota(jnp.int32, (8, x.shape[1]), 0)
    return dx, jnp.where(sub == 0, d0, jnp.where(sub == 1, d1, jnp.where(sub == 2, d2, 0.0)))


MM_VMEM_BUDGET = 40 * 1024 * 1024
MM_TILE_CAP = 1408


def _mm_tiles(m, n, k, out_bytes):
    def edges(d):
        return [t for t in range(LANES, min(d, MM_TILE_CAP) + 1, LANES) if d % t == 0] or [d]

    best = None
    for tm in edges(m):
        for tn in edges(n):
            if 2 * (2 * k * (tm + tn) + tm * tn * out_bytes) > MM_VMEM_BUDGET:
                continue
            for a_outer in (True, False):
                reads = k * m + (m // tm) * k * n if a_outer else k * n + (n // tn) * k * m
                traffic = 2 * reads + m * n * out_bytes
                key = (traffic, -tm * tn)
                if best is None or key < best[0]:
                    best = (key, (tm, tn, a_outer))
    return best[1]


def _mm(a, b, form, out_dtype, name, after=None):
    if form == "nn":
        (m, k), n = a.shape, b.shape[1]
    elif form == "nt":
        (m, k), n = a.shape, b.shape[0]
    else:
        (k, m), n = a.shape, b.shape[1]
    tm, tn, a_outer = _mm_tiles(m, n, k, jnp.dtype(out_dtype).itemsize)
    dims = {"nn": (((1,), (0,)), ((), ())), "nt": (((1,), (1,)), ((), ())), "tn": (((0,), (0,)), ((), ()))}[form]

    def body(a_ref, b_ref, *rest):
        o_ref = rest[-1]
        o_ref[...] = lax.dot_general(a_ref[...], b_ref[...], dims, preferred_element_type=F32).astype(o_ref.dtype)

    ij = (lambda g0, g1: (g0, g1)) if a_outer else (lambda g0, g1: (g1, g0))
    a_spec = (pl.BlockSpec((k, tm), lambda g0, g1: (0, ij(g0, g1)[0])) if form == "tn"
              else pl.BlockSpec((tm, k), lambda g0, g1: (ij(g0, g1)[0], 0)))
    b_spec = (pl.BlockSpec((tn, k), lambda g0, g1: (ij(g0, g1)[1], 0)) if form == "nt"
              else pl.BlockSpec((k, tn), lambda g0, g1: (0, ij(g0, g1)[1])))
    extra = [] if after is None else [pl.BlockSpec((8, LANES), lambda g0, g1: (0, 0))]
    return pl.pallas_call(
        body, name=name, grid=(m // tm, n // tn) if a_outer else (n // tn, m // tm),
        in_specs=[a_spec, b_spec] + extra, out_specs=pl.BlockSpec((tm, tn), lambda g0, g1: ij(g0, g1)),
        out_shape=jax.ShapeDtypeStruct((m, n), out_dtype),
        compiler_params=_params("parallel", "arbitrary"),
    )(a, b, *([] if after is None else [after]))


def _postnorm_bwd_mm(o, g, dx, b, name):
    s, d = o.shape
    n = b.shape[0]
    tm, tn = _tile(s, 512), _tile(n, MM_TILE_CAP)

    def body(o_ref, g_ref, dx_ref, b_ref, do_ref, dg_ref, out_ref):
        i, j = pl.program_id(0), pl.program_id(1)

        @pl.when(j == 0)
        def _():
            _, vjp = jax.vjp(_rms, o_ref[...], g_ref[...])
            d_o, dg = vjp(dx_ref[...])
            do_ref[...] = d_o.astype(BF16)

            @pl.when(i == 0)
            def _():
                dg_ref[...] = jnp.zeros_like(dg_ref)

            dg_ref[...] += dg

        out_ref[...] = lax.dot_general(do_ref[...], b_ref[...], (((1,), (1,)), ((), ())), preferred_element_type=F32)

    row = pl.BlockSpec((tm, d), lambda i, j: (i, 0))
    vec = pl.BlockSpec((1, d), lambda i, j: (0, 0))
    return pl.pallas_call(
        body, name=name, grid=(s // tm, n // tn),
        in_specs=[row, vec, row, pl.BlockSpec((tn, d), lambda i, j: (j, 0))],
        out_specs=[row, vec, pl.BlockSpec((tm, tn), lambda i, j: (i, j))],
        out_shape=[jax.ShapeDtypeStruct((s, d), BF16), jax.ShapeDtypeStruct((1, d), F32), jax.ShapeDtypeStruct((s, n), F32)],
        compiler_params=_params("arbitrary", "arbitrary"),
    )(o, g, dx, b)


def _mm_prenorm_bwd(a, b, x, g, dres, after, name):
    s, k = a.shape
    d = b.shape[1]
    tm = _tile(s, 512)

    def body(a_ref, b_ref, x_ref, g_ref, dres_ref, after_ref, dx_ref, dg_ref):
        dxn = jnp.dot(a_ref[...], b_ref[...], preferred_element_type=F32)
        _, vjp = jax.vjp(_rms, x_ref[...], g_ref[...])
        dx, dg = vjp(dxn)
        dx_ref[...] = dres_ref[...] + dx

        @pl.when(pl.program_id(0) == 0)
        def _():
            dg_ref[...] = jnp.zeros_like(dg_ref)

        dg_ref[...] += dg

    row = pl.BlockSpec((tm, d), lambda i: (i, 0))
    vec = pl.BlockSpec((1, d), lambda i: (0, 0))
    return pl.pallas_call(
        body, name=name, grid=(s // tm,),
        in_specs=[pl.BlockSpec((tm, k), lambda i: (i, 0)),
                  pl.BlockSpec((k, d), lambda i: (0, 0), pipeline_mode=pl.Buffered(1)),
                  row, vec, row, pl.BlockSpec((8, LANES), lambda i: (0, 0))],
        out_specs=[row, vec], out_shape=[jax.ShapeDtypeStruct((s, d), F32), jax.ShapeDtypeStruct((1, d), F32)],
        compiler_params=_params("arbitrary"),
    )(a, b, x, g, dres, after)


def _prenorm(x, g, name):
    s, d = x.shape
    tm = _tile(s, 512)

    def body(x_ref, g_ref, o_ref):
        o_ref[...] = _rms(x_ref[...], g_ref[...]).astype(BF16)

    return pl.pallas_call(
        body, name=name, grid=(s // tm,),
        in_specs=[pl.BlockSpec((tm, d), lambda i: (i, 0)), pl.BlockSpec((1, d), lambda i: (0, 0))],
        out_specs=pl.BlockSpec((tm, d), lambda i: (i, 0)),
        out_shape=jax.ShapeDtypeStruct((s, d), BF16), compiler_params=_params("parallel"),
    )(x, g)


def _mm_postnorm(a, b, x, g_post, g_next, name):
    s, k = a.shape
    d = b.shape[1]
    tm = _tile(s, 512)

    def body(a_ref, b_ref, x_ref, gp_ref, gn_ref, o_ref, x1_ref, xn_ref):
        o = jnp.dot(a_ref[...], b_ref[...], preferred_element_type=F32)
        o_ref[...] = o
        x1 = x_ref[...] + _rms(o, gp_ref[...])
        x1_ref[...] = x1
        xn_ref[...] = _rms(x1, gn_ref[...]).astype(BF16)

    row = pl.BlockSpec((tm, d), lambda i: (i, 0))
    vec = pl.BlockSpec((1, d), lambda i: (0, 0))
    return pl.pallas_call(
        body, name=name, grid=(s // tm,),
        in_specs=[pl.BlockSpec((tm, k), lambda i: (i, 0)),
                  pl.BlockSpec((k, d), lambda i: (0, 0), pipeline_mode=pl.Buffered(1)), row, vec, vec],
        out_specs=[row, row, row],
        out_shape=[jax.ShapeDtypeStruct((s, d), F32), jax.ShapeDtypeStruct((s, d), F32), jax.ShapeDtypeStruct((s, d), BF16)],
        compiler_params=_params("parallel"),
    )(a, b, x, g_post, g_next)


def _mm_postnorm_loss(a, b, x, g_post, target, name):
    s, k = a.shape
    d = b.shape[1]
    tm = _tile(s, 512)

    def body(a_ref, b_ref, x_ref, gp_ref, t_ref, o_ref, dy_ref, acc_ref):
        o = jnp.dot(a_ref[...], b_ref[...], preferred_element_type=F32)
        o_ref[...] = o
        e = x_ref[...] + _rms(o, gp_ref[...]) - t_ref[...]
        dy_ref[...] = e / d

        @pl.when(pl.program_id(0) == 0)
        def _():
            acc_ref[...] = jnp.zeros_like(acc_ref)

        acc_ref[...] += jnp.sum(jnp.sum(e * e, axis=1, keepdims=True), axis=0, keepdims=True)

    row = pl.BlockSpec((tm, d), lambda i: (i, 0))
    return pl.pallas_call(
        body, name=name, grid=(s // tm,),
        in_specs=[pl.BlockSpec((tm, k), lambda i: (i, 0)),
                  pl.BlockSpec((k, d), lambda i: (0, 0), pipeline_mode=pl.Buffered(1)), row,
                  pl.BlockSpec((1, d), lambda i: (0, 0)), row],
        out_specs=[row, row, pl.BlockSpec((1, LANES), lambda i: (0, 0))],
        out_shape=[jax.ShapeDtypeStruct((s, d), F32), jax.ShapeDtypeStruct((s, d), F32), jax.ShapeDtypeStruct((1, LANES), F32)],
        compiler_params=_params("arbitrary"),
    )(a, b, x, g_post, target)


def _postnorm_bwd(o, g, dx, name):
    s, d = o.shape
    tm = _tile(s, 512)

    def body(o_ref, g_ref, dx_ref, do_ref, dg_ref):
        _, vjp = jax.vjp(_rms, o_ref[...], g_ref[...])
        d_o, dg = vjp(dx_ref[...])
        do_ref[...] = d_o.astype(BF16)

        @pl.when(pl.program_id(0) == 0)
        def _():
            dg_ref[...] = jnp.zeros_like(dg_ref)

        dg_ref[...] += dg

    row = pl.BlockSpec((tm, d), lambda i: (i, 0))
    vec = pl.BlockSpec((1, d), lambda i: (0, 0))
    return pl.pallas_call(
        body, name=name, grid=(s // tm,), in_specs=[row, vec, row], out_specs=[row, vec],
        out_shape=[jax.ShapeDtypeStruct((s, d), BF16), jax.ShapeDtypeStruct((1, d), F32)],
        compiler_params=_params("arbitrary"),
    )(o, g, dx)


def _log_sigmoid(z):
    return jnp.minimum(z, 0.0) - jnp.log(1.0 + jnp.exp(-jnp.abs(z)))


def _forget_prep(h, bf_pad, fblk, name):
    s = h.shape[0]

    def body(f_ref, b_ref, c_ref, ct_ref):
        c = _log_sigmoid(f_ref[...] + b_ref[...])
        rows = lax.broadcasted_iota(jnp.int32, c.shape, 0)
        k = 1
        while k < s:
            c = c + _shift_down(c, k, rows)
            k *= 2
        c_ref[...] = c
        ct_ref[...] = jnp.transpose(c)[0:8, :]

    return pl.pallas_call(
        body, name=name, grid=(1,),
        in_specs=[pl.BlockSpec((s, LANES), lambda i: (0, fblk)), pl.BlockSpec((1, LANES), lambda i: (0, 0))],
        out_specs=[pl.BlockSpec((s, LANES), lambda i: (0, 0)), pl.BlockSpec((8, s), lambda i: (0, 0))],
        out_shape=[jax.ShapeDtypeStruct((s, LANES), F32), jax.ShapeDtypeStruct((8, s), F32)],
        compiler_params=_params("arbitrary"),
    )(h, bf_pad)


def _forget_prep_bwd(h, bf_pad, dct, dcq, fblk, name):
    s = h.shape[0]
    pairs = N_HEADS // 2

    def body(f_ref, b_ref, dct_ref, dcq_ref, df_ref, db_ref):
        dct = dct_ref[0]
        dcq = dcq_ref[0]
        for p in range(1, pairs):
            dct = dct + pltpu.roll(dct_ref[p], 2 * p, 0)
            dcq = dcq + pltpu.roll(dcq_ref[p], 2 * p, 1)
        dc = dcq + jnp.transpose(jnp.concatenate([dct, jnp.zeros((LANES - 8, s), F32)], axis=0))
        rows = lax.broadcasted_iota(jnp.int32, dc.shape, 0)
        k = 1
        while k < s:
            dc = dc + _shift_up(dc, k, rows)
            k *= 2
        z = f_ref[...] + b_ref[...]
        lane = lax.broadcasted_iota(jnp.int32, dc.shape, 1)
        df = jnp.where(lane < N_HEADS, dc * jax.nn.sigmoid(-z), 0.0)
        df_ref[...] = df
        db_ref[...] = jnp.sum(df, axis=0, keepdims=True)

    return pl.pallas_call(
        body, name=name, grid=(1,),
        in_specs=[pl.BlockSpec((s, LANES), lambda i: (0, fblk)), pl.BlockSpec((1, LANES), lambda i: (0, 0)),
                  pl.BlockSpec((pairs, 8, s), lambda i: (0, 0, 0)), pl.BlockSpec((pairs, s, LANES), lambda i: (0, 0, 0))],
        out_specs=[pl.BlockSpec((s, LANES), lambda i: (0, 0)), pl.BlockSpec((1, LANES), lambda i: (0, 0))],
        out_shape=[jax.ShapeDtypeStruct((s, LANES), F32), jax.ShapeDtypeStruct((1, LANES), F32)],
        compiler_params=_params("arbitrary"),
    )(h, bf_pad, dct, dcq)


def _pick_lane(blk, idx):
    lane = lax.broadcasted_iota(jnp.int32, blk.shape, 1)
    return jnp.sum(jnp.where(lane == idx, blk, 0.0), axis=1, keepdims=True)


def _pick_row(blk, idx):
    sub = lax.broadcasted_iota(jnp.int32, blk.shape, 0)
    return jnp.sum(jnp.where(sub == idx, blk, 0.0), axis=0, keepdims=True)


def _attention_fwd(h, c, ct, qblk, name):
    s = h.shape[0]
    t = _tile(s, ATT_TILE)
    nq = s // t
    scale = HEAD_DIM ** -0.5
    nt_dims = (((1,), (1,)), ((), ()))

    def body(q_ref, k_ref, v_ref, c_ref, ct_ref, o_ref, lse_ref):
        p = pl.program_id(0)
        i = pl.program_id(1)
        lane = lax.broadcasted_iota(jnp.int32, (1, LANES), 1)
        first = lane < HEAD_DIM
        q = q_ref[...] * (scale * LOG2E)
        qa = jnp.where(first, q, 0.0).astype(BF16)
        qb = jnp.where(first, 0.0, q).astype(BF16)
        cblk = c_ref[...]
        cta = _pick_lane(cblk, 2 * p) * LOG2E
        ctb = _pick_lane(cblk, 2 * p + 1) * LOG2E

        def step(j, carry, diagonal):
            ma, la, mb, lb, acc = carry
            off = pl.multiple_of(j * t, t)
            k = k_ref[pl.ds(off, t), :].astype(BF16)
            v = v_ref[pl.ds(off, t), :].astype(BF16)
            crow = ct_ref[:, pl.ds(off, t)] * LOG2E

            def one(qh, cth, hd, m_old, l_old):
                sc = lax.dot_general(qh, k, nt_dims, preferred_element_type=F32) - _pick_row(crow, hd)
                if diagonal:
                    keep = lax.broadcasted_iota(jnp.int32, (t, t), 0) >= lax.broadcasted_iota(jnp.int32, (t, t), 1)
                    sc = jnp.where(keep, sc, NEG)
                m_new = jnp.maximum(m_old, jnp.max(sc, axis=1, keepdims=True) + cth)
                pr = jnp.exp2(sc - (m_new - cth))
                alpha = jnp.exp2(m_old - m_new)
                l_new = alpha * l_old + jnp.sum(pr, axis=1, keepdims=True)
                pv = jnp.dot(pr.astype(BF16), v, preferred_element_type=F32)
                return m_new, l_new, alpha, pv

            ma2, la2, aa, pva = one(qa, cta, 2 * p, ma, la)
            mb2, lb2, ab, pvb = one(qb, ctb, 2 * p + 1, mb, lb)
            acc = jnp.where(first, aa * acc + pva, ab * acc + pvb)
            return ma2, la2, mb2, lb2, acc

        init = (jnp.full((t, 1), NEG, F32), jnp.zeros((t, 1), F32), jnp.full((t, 1), NEG, F32),
                jnp.zeros((t, 1), F32), jnp.zeros((t, LANES), F32))
        carry = lax.fori_loop(0, i, lambda j, carry: step(j, carry, False), init)
        ma, la, mb, lb, acc = step(i, carry, True)
        o_ref[...] = (acc / jnp.where(first, la, lb)).astype(BF16)
        lse_ref[0] = jnp.broadcast_to(ma + jnp.log2(la), (t, LANES))
        lse_ref[1] = jnp.broadcast_to(mb + jnp.log2(lb), (t, LANES))

    return pl.pallas_call(
        body, name=name, grid=(N_HEADS // 2, nq),
        in_specs=[pl.BlockSpec((t, LANES), lambda p, i: (i, qblk + p)),
                  pl.BlockSpec((s, LANES), lambda p, i: (0, qblk + 4 + p)),
                  pl.BlockSpec((s, LANES), lambda p, i: (0, qblk + 8 + p)),
                  pl.BlockSpec((t, LANES), lambda p, i: (i, 0)),
                  pl.BlockSpec((8, s), lambda p, i: (0, 0))],
        out_specs=[pl.BlockSpec((t, LANES), lambda p, i: (i, p)),
                   pl.BlockSpec((2, t, LANES), lambda p, i: (p, i, 0))],
        out_shape=[jax.ShapeDtypeStruct((s, D_ATT), BF16), jax.ShapeDtypeStruct((N_HEADS, s, LANES), F32)],
        compiler_params=_params("parallel", "arbitrary"),
    )(h, h, h, c, ct)


def _attention_bwd(h, c, ct, lse, att, datt, after, qblk, name):
    s = h.shape[0]
    t = _tile(s, ATT_TILE)
    nq = s // t
    scale = HEAD_DIM ** -0.5
    nt_dims = (((1,), (1,)), ((), ()))
    tn_dims = (((0,), (0,)), ((), ()))

    def body(q_ref, k_ref, v_ref, c_ref, ct_ref, lse_ref, o_ref, do_ref, after_ref, dq_ref, dk_ref, dv_ref, dct_ref, dcq_ref):
        p = pl.program_id(0)
        j = pl.program_id(1)
        lane = lax.broadcasted_iota(jnp.int32, (1, LANES), 1)
        first = lane < HEAD_DIM
        kf = k_ref[...]
        vf = v_ref[...]
        k = kf.astype(BF16)
        ka = jnp.where(first, kf, 0.0).astype(BF16)
        kb = jnp.where(first, 0.0, kf).astype(BF16)
        va = jnp.where(first, vf, 0.0).astype(BF16)
        vb = jnp.where(first, 0.0, vf).astype(BF16)
        crow = ct_ref[...] * LOG2E
        csa = _pick_row(crow, 2 * p)
        csb = _pick_row(crow, 2 * p + 1)

        @pl.when(j == 0)
        def _():
            dq_ref[...] = jnp.zeros_like(dq_ref)
            dcq_ref[...] = jnp.zeros_like(dcq_ref)

        def step(i, carry, diagonal):
            dka, dkb, dva, dvb, dca, dcb = carry
            off = pl.multiple_of(i * t, t)
            rows = pl.ds(off, t)
            q = (q_ref[rows, :] * (scale * LOG2E)).astype(BF16)
            dof = do_ref[rows, :]
            do = dof.astype(BF16)
            prod = dof * o_ref[rows, :].astype(F32)
            cblk = c_ref[rows, :] * LOG2E

            def one(kh, vh, hd, csh, lse_h):
                sc = lax.dot_general(q, kh, nt_dims, preferred_element_type=F32) - csh
                if diagonal:
                    keep = lax.broadcasted_iota(jnp.int32, (t, t), 0) >= lax.broadcasted_iota(jnp.int32, (t, t), 1)
                    sc = jnp.where(keep, sc, NEG)
                pr = jnp.exp2(sc - (jnp.max(lse_h, axis=1, keepdims=True) - _pick_lane(cblk, hd)))
                dp = lax.dot_general(do, vh, nt_dims, preferred_element_type=F32)
                return pr, dp

            pra, dpa = one(ka, va, 2 * p, csa, lse_ref[0, rows, :])
            prb, dpb = one(kb, vb, 2 * p + 1, csb, lse_ref[1, rows, :])
            dela = jnp.sum(jnp.where(first, prod, 0.0), axis=1, keepdims=True)
            delb = jnp.sum(jnp.where(first, 0.0, prod), axis=1, keepdims=True)
            dsa = pra * (dpa - dela)
            dsb = prb * (dpb - delb)
            dsa16 = dsa.astype(BF16)
            dsb16 = dsb.astype(BF16)
            dva = dva + lax.dot_general(pra.astype(BF16), do, tn_dims, preferred_element_type=F32)
            dvb = dvb + lax.dot_general(prb.astype(BF16), do, tn_dims, preferred_element_type=F32)
            dka = dka + lax.dot_general(dsa16, q, tn_dims, preferred_element_type=F32)
            dkb = dkb + lax.dot_general(dsb16, q, tn_dims, preferred_element_type=F32)
            dqa = jnp.dot(dsa16, k, preferred_element_type=F32)
            dqb = jnp.dot(dsb16, k, preferred_element_type=F32)
            dq_ref[rows, :] += scale * jnp.where(first, dqa, dqb)
            dca = dca - jnp.sum(dsa, axis=0, keepdims=True)
            dcb = dcb - jnp.sum(dsb, axis=0, keepdims=True)
            dcq_ref[rows, :] += jnp.where(lane == 0, jnp.sum(dsa, axis=1, keepdims=True),
                                          jnp.where(lane == 1, jnp.sum(dsb, axis=1, keepdims=True), 0.0))
            return dka, dkb, dva, dvb, dca, dcb

        z = jnp.zeros((t, LANES), F32)
        zr = jnp.zeros((1, t), F32)
        carry = step(j, (z, z, z, z, zr, zr), True)
        dka, dkb, dva, dvb, dca, dcb = lax.fori_loop(j + 1, nq, lambda i, carry: step(i, carry, False), carry)
        dk_ref[...] = jnp.where(first, dka, dkb) * (1.0 / LOG2E)
        dv_ref[...] = jnp.where(first, dva, dvb)
        sub = lax.broadcasted_iota(jnp.int32, (8, t), 0)
        dct_ref[...] = jnp.where(sub == 0, dca, jnp.where(sub == 1, dcb, 0.0))

    full = lambda blk: pl.BlockSpec((s, LANES), blk)
    return pl.pallas_call(
        body, name=name, grid=(N_HEADS // 2, nq),
        in_specs=[full(lambda p, j: (0, qblk + p)),
                  pl.BlockSpec((t, LANES), lambda p, j: (j, qblk + 4 + p)),
                  pl.BlockSpec((t, LANES), lambda p, j: (j, qblk + 8 + p)),
                  full(lambda p, j: (0, 0)),
                  pl.BlockSpec((8, t), lambda p, j: (0, j)),
                  pl.BlockSpec((2, s, LANES), lambda p, j: (p, 0, 0)),
                  full(lambda p, j: (0, p)),
                  full(lambda p, j: (0, p)),
                  pl.BlockSpec((8, LANES), lambda p, j: (0, 0))],
        out_specs=[full(lambda p, j: (0, p)),
                   pl.BlockSpec((t, LANES), lambda p, j: (j, p)),
                   pl.BlockSpec((t, LANES), lambda p, j: (j, p)),
                   pl.BlockSpec((None, 8, t), lambda p, j: (p, 0, j)),
                   pl.BlockSpec((None, s, LANES), lambda p, j: (p, 0, 0))],
        out_shape=[jax.ShapeDtypeStruct((s, D_ATT), F32), jax.ShapeDtypeStruct((s, D_ATT), F32),
                   jax.ShapeDtypeStruct((s, D_ATT), F32), jax.ShapeDtypeStruct((N_HEADS // 2, 8, s), F32),
                   jax.ShapeDtypeStruct((N_HEADS // 2, s, LANES), F32)],
        compiler_params=_params("arbitrary", "arbitrary"),
    )(h, h, h, c, ct, lse, att, datt, after)


def _sconv_fwd(h, w, bgblk, name):
    s = h.shape[0]
    nblk = D_CONV // LANES

    def body(bg_ref, cg_ref, hc_ref, w_ref, y_ref):
        rows = lax.broadcasted_iota(jnp.int32, (s, LANES), 0)
        y_ref[...] = (bg_ref[...] * _conv3(cg_ref[...] * hc_ref[...], w_ref, rows)).astype(BF16)

    col = lambda base: pl.BlockSpec((s, LANES), lambda j: (0, base + j))
    return pl.pallas_call(
        body, name=name, grid=(nblk,),
        in_specs=[col(bgblk), col(bgblk + nblk), col(bgblk + 2 * nblk), pl.BlockSpec((3, LANES), lambda j: (0, j))],
        out_specs=pl.BlockSpec((s, LANES), lambda j: (0, j)),
        out_shape=jax.ShapeDtypeStruct((s, D_CONV), BF16), compiler_params=_params("parallel"),
    )(h, h, h, w)


def _sconv_bwd(h, w, dy, bgblk, name):
    s = h.shape[0]
    nblk = D_CONV // LANES

    def body(bg_ref, cg_ref, hc_ref, w_ref, dy_ref, dbg_ref, dcg_ref, dhc_ref, dw_ref):
        rows = lax.broadcasted_iota(jnp.int32, (s, LANES), 0)
        cg, hc, dy, w = cg_ref[...], hc_ref[...], dy_ref[...], w_ref
        xin = cg * hc
        dbg_ref[...] = (dy * _conv3(xin, w, rows)).astype(BF16)
        dxin, dw_ref[...] = _conv3_bwd(dy * bg_ref[...], xin, w, rows)
        dcg_ref[...] = (dxin * hc).astype(BF16)
        dhc_ref[...] = (dxin * cg).astype(BF16)

    col = lambda base: pl.BlockSpec((s, LANES), lambda j: (0, base + j))
    return pl.pallas_call(
        body, name=name, grid=(nblk,),
        in_specs=[col(bgblk), col(bgblk + nblk), col(bgblk + 2 * nblk), pl.BlockSpec((3, LANES), lambda j: (0, j)), col(0)],
        out_specs=[col(0), col(0), col(0), pl.BlockSpec((8, LANES), lambda j: (0, j))],
        out_shape=[jax.ShapeDtypeStruct((s, D_CONV), BF16)] * 3 + [jax.ShapeDtypeStruct((8, D_CONV), F32)],
        compiler_params=_params("parallel"),
    )(h, h, h, w, dy)


def _sgu_group_masks():
    lane = lax.broadcasted_iota(jnp.int32, (1, D_SGU), 1)
    return [(lane // HEAD_DIM) == g for g in range(N_SGU_GROUPS)]


def _sgu_tril():
    r = lax.broadcasted_iota(jnp.int32, (SGU_CHUNK, SGU_CHUNK), 0)
    c = lax.broadcasted_iota(jnp.int32, (SGU_CHUNK, SGU_CHUNK), 1)
    return r >= c


def _sgu_fwd(h, ln_g, ln_b, w_s, b_full, ublk, name):
    s = h.shape[0]
    tr = _tile(s, 512)
    nch = tr // SGU_CHUNK

    def body(u_ref, v_ref, g_ref, b_ref, w_ref, bf_ref, y_ref):
        masks = _sgu_group_masks()
        tril = _sgu_tril()
        wm = [jnp.where(tril, w_ref[g], 0.0).astype(BF16) for g in range(N_SGU_GROUPS)]
        vn = _layer_norm(_gelu(v_ref[...]), g_ref[...], b_ref[...])
        for ch in range(nch):
            rows = pl.ds(ch * SGU_CHUNK, SGU_CHUNK)
            vc = vn[ch * SGU_CHUNK:(ch + 1) * SGU_CHUNK, :]
            mixed = bf_ref[...]
            for g in range(N_SGU_GROUPS):
                mixed = mixed + jnp.dot(wm[g], jnp.where(masks[g], vc, 0.0).astype(BF16), preferred_element_type=F32)
            y_ref[rows, :] = (_gelu(u_ref[rows, :]) * mixed).astype(BF16)

    row = lambda blk: pl.BlockSpec((tr, D_SGU), lambda i: (i, blk))
    vec = pl.BlockSpec((1, D_SGU), lambda i: (0, 0))
    return pl.pallas_call(
        body, name=name, grid=(s // tr,),
        in_specs=[row(ublk), row(ublk + 1), vec, vec,
                  pl.BlockSpec((N_SGU_GROUPS, SGU_CHUNK, SGU_CHUNK), lambda i: (0, 0, 0)),
                  pl.BlockSpec((SGU_CHUNK, D_SGU), lambda i: (0, 0))],
        out_specs=row(0), out_shape=jax.ShapeDtypeStruct((s, D_SGU), BF16), compiler_params=_params("parallel"),
    )(h, h, ln_g, ln_b, w_s, b_full)


def _sgu_bwd(h, ln_g, ln_b, w_s, b_full, dy, ublk, name):
    s = h.shape[0]
    tr = _tile(s, 512)
    nch = tr // SGU_CHUNK
    nt_dims = (((1,), (1,)), ((), ()))

    def norm(v, g, b):
        return _layer_norm(_gelu(v), g, b)

    def body(u_ref, v_ref, g_ref, b_ref, w_ref, bf_ref, dy_ref, du_ref, dv_ref, dg_ref, db_ref, dw_ref, dbf_ref):
        masks = _sgu_group_masks()
        tril = _sgu_tril()
        wf = [jnp.where(tril, w_ref[g], 0.0) for g in range(N_SGU_GROUPS)]
        wm = [w.astype(BF16) for w in wf]
        wmt = [jnp.transpose(w).astype(BF16) for w in wf]
        vn, vjp = jax.vjp(norm, v_ref[...], g_ref[...], b_ref[...])

        @pl.when(pl.program_id(0) == 0)
        def _():
            dg_ref[...] = jnp.zeros_like(dg_ref)
            db_ref[...] = jnp.zeros_like(db_ref)
            dw_ref[...] = jnp.zeros_like(dw_ref)
            dbf_ref[...] = jnp.zeros_like(dbf_ref)

        dvn_parts = []
        for ch in range(nch):
            rows = pl.ds(ch * SGU_CHUNK, SGU_CHUNK)
            vc = vn[ch * SGU_CHUNK:(ch + 1) * SGU_CHUNK, :]
            vc16 = vc.astype(BF16)
            mixed = bf_ref[...]
            for g in range(N_SGU_GROUPS):
                mixed = mixed + jnp.dot(wm[g], jnp.where(masks[g], vc, 0.0).astype(BF16), preferred_element_type=F32)
            dy = dy_ref[rows, :]
            ug, slope = _gelu_and_slope(u_ref[rows, :])
            du_ref[rows, :] = (dy * mixed * slope).astype(BF16)
            dmixed = dy * ug
            dbf_ref[...] += dmixed
            dvc = jnp.zeros((SGU_CHUNK, D_SGU), F32)
            for g in range(N_SGU_GROUPS):
                dm16 = jnp.where(masks[g], dmixed, 0.0).astype(BF16)
                dw_ref[g] += jnp.where(tril, lax.dot_general(dm16, vc16, nt_dims, preferred_element_type=F32), 0.0)
                dvc = dvc + jnp.dot(wmt[g], dm16, preferred_element_type=F32)
            dvn_parts.append(dvc)
        dv, dg, db = vjp(jnp.concatenate(dvn_parts, axis=0))
        dv_ref[...] = dv.astype(BF16)
        dg_ref[...] += dg
        db_ref[...] += db

    row = lambda blk: pl.BlockSpec((tr, D_SGU), lambda i: (i, blk))
    vec = pl.BlockSpec((1, D_SGU), lambda i: (0, 0))
    wsp = pl.BlockSpec((N_SGU_GROUPS, SGU_CHUNK, SGU_CHUNK), lambda i: (0, 0, 0))
    bsp = pl.BlockSpec((SGU_CHUNK, D_SGU), lambda i: (0, 0))
    return pl.pallas_call(
        body, name=name, grid=(s // tr,),
        in_specs=[row(ublk), row(ublk + 1), vec, vec, wsp, bsp, row(0)],
        out_specs=[row(0), row(0), vec, vec, wsp, bsp],
        out_shape=[jax.ShapeDtypeStruct((s, D_SGU), BF16), jax.ShapeDtypeStruct((s, D_SGU), BF16),
                   jax.ShapeDtypeStruct((1, D_SGU), F32), jax.ShapeDtypeStruct((1, D_SGU), F32),
                   jax.ShapeDtypeStruct((N_SGU_GROUPS, SGU_CHUNK, SGU_CHUNK), F32),
                   jax.ShapeDtypeStruct((SGU_CHUNK, D_SGU), F32)],
        compiler_params=_params("arbitrary"),
    )(h, h, ln_g, ln_b, w_s, b_full, dy)


def _merge_fwd(h, b_gate, att, yc, ys, wa, wc, ws, w_out, x, g_post, g_next, name):
    s, d = att.shape[0], wa.shape[1]
    tm = _tile(s, 512)

    def body(g0_ref, g1_ref, g2_ref, bg_ref, a_ref, c_ref, s_ref, wa_ref, wc_ref, ws_ref, wo_ref, x_ref, gp_ref, gn_ref,
             m_ref, o_ref, x1_ref, xn_ref):
        acc = jax.nn.sigmoid(g0_ref[...] + bg_ref[0:1, :]) * jnp.dot(a_ref[...], wa_ref[...], preferred_element_type=F32)
        acc += jax.nn.sigmoid(g1_ref[...] + bg_ref[1:2, :]) * jnp.dot(c_ref[...], wc_ref[...], preferred_element_type=F32)
        acc += jax.nn.sigmoid(g2_ref[...] + bg_ref[2:3, :]) * jnp.dot(s_ref[...], ws_ref[...], preferred_element_type=F32)
        merged = acc.astype(BF16)
        m_ref[...] = merged
        o = jnp.dot(merged, wo_ref[...], preferred_element_type=F32)
        o_ref[...] = o
        x1 = x_ref[...] + _rms(o, gp_ref[...])
        x1_ref[...] = x1
        xn_ref[...] = _rms(x1, gn_ref[...]).astype(BF16)

    gate = lambda b: pl.BlockSpec((tm, d), lambda i: (i, b))
    act = lambda k: pl.BlockSpec((tm, k), lambda i: (i, 0))
    wgt = lambda k: pl.BlockSpec((k, d), lambda i: (0, 0), pipeline_mode=pl.Buffered(1))
    vec = pl.BlockSpec((1, d), lambda i: (0, 0))
    return pl.pallas_call(
        body, name=name, grid=(s // tm,),
        in_specs=[gate(0), gate(1), gate(2), pl.BlockSpec((3, d), lambda i: (0, 0)),
                  act(D_ATT), act(D_CONV), act(D_SGU), wgt(D_ATT), wgt(D_CONV), wgt(D_SGU), wgt(d), act(d), vec, vec],
        out_specs=[act(d)] * 4,
        out_shape=[jax.ShapeDtypeStruct((s, d), BF16), jax.ShapeDtypeStruct((s, d), F32), jax.ShapeDtypeStruct((s, d), F32),
                   jax.ShapeDtypeStruct((s, d), BF16)],
        compiler_params=_params("parallel"),
    )(h, h, h, b_gate, att, yc, ys, wa, wc, ws, w_out, x, g_post, g_next)


def _merge_bwd(h, b_gate, att, yc, ys, wa, wc, ws, dm, name):
    s, d = att.shape[0], wa.shape[1]
    tm = _tile(s, 512)
    nt_dims, tn_dims = (((1,), (1,)), ((), ())), (((0,), (0,)), ((), ()))
    widths = (D_ATT, D_CONV, D_SGU)

    def body(g0_ref, g1_ref, g2_ref, bg_ref, a_ref, c_ref, s_ref, wa_ref, wc_ref, ws_ref, dm_ref,
             da_ref, dc_ref, ds_ref, ga_ref, gc_ref, gs_ref, dgl_ref, dbg_ref, acc_a, acc_c, acc_s):
        i = pl.program_id(0)

        @pl.when(i == 0)
        def _():
            dbg_ref[...] = jnp.zeros_like(dbg_ref)
            for acc in (acc_a, acc_c, acc_s):
                acc[...] = jnp.zeros_like(acc)

        dm = dm_ref[...]
        sums = []
        for b, (g_ref, x_ref, w_ref, dx_ref, acc) in enumerate((
                (g0_ref, a_ref, wa_ref, da_ref, acc_a), (g1_ref, c_ref, wc_ref, dc_ref, acc_c),
                (g2_ref, s_ref, ws_ref, ds_ref, acc_s))):
            gate = jax.nn.sigmoid(g_ref[...] + bg_ref[b:b + 1, :])
            x, w = x_ref[...], w_ref[...]
            y = jnp.dot(x, w, preferred_element_type=F32)
            dy = (dm * gate).astype(BF16)
            dgl = dm * y * gate * (1.0 - gate)
            dgl_ref[b] = dgl.astype(BF16)
            sums.append(jnp.sum(dgl, axis=0, keepdims=True))
            dx_ref[...] = lax.dot_general(dy, w, nt_dims, preferred_element_type=F32)
            acc[...] += lax.dot_general(x, dy, tn_dims, preferred_element_type=F32)
        sub = lax.broadcasted_iota(jnp.int32, (3, d), 0)
        dbg_ref[...] += jnp.where(sub == 0, sums[0], jnp.where(sub == 1, sums[1], sums[2]))

        @pl.when(i == pl.num_programs(0) - 1)
        def _():
            for g_out, acc in ((ga_ref, acc_a), (gc_ref, acc_c), (gs_ref, acc_s)):
                g_out[...] = acc[...].astype(BF16)

    gate = lambda b: pl.BlockSpec((tm, d), lambda i: (i, b))
    act = lambda k: pl.BlockSpec((tm, k), lambda i: (i, 0))
    wgt = lambda k: pl.BlockSpec((k, d), lambda i: (0, 0), pipeline_mode=pl.Buffered(1))
    res = pl.pallas_call(
        body, name=name, grid=(s // tm,),
        in_specs=[gate(0), gate(1), gate(2), pl.BlockSpec((3, d), lambda i: (0, 0)),
                  act(D_ATT), act(D_CONV), act(D_SGU), wgt(D_ATT), wgt(D_CONV), wgt(D_SGU), act(d)],
        out_specs=[act(k) for k in widths] + [pl.BlockSpec((k, d), lambda i: (0, 0)) for k in widths]
        + [pl.BlockSpec((3, tm, d), lambda i: (0, i, 0)), pl.BlockSpec((3, d), lambda i: (0, 0))],
        out_shape=[jax.ShapeDtypeStruct((s, k), F32) for k in widths] + [jax.ShapeDtypeStruct((k, d), BF16) for k in widths]
        + [jax.ShapeDtypeStruct((3, s, d), BF16), jax.ShapeDtypeStruct((3, d), F32)],
        scratch_shapes=[pltpu.VMEM((k, d), F32) for k in widths],
        compiler_params=_params("arbitrary"),
    )(h, h, h, b_gate, att, yc, ys, wa, wc, ws, dm)
    return res[0:3], res[3:6], res[6], res[7]


def _ffn_act_fwd(hh, cw, name):
    s, dff = hh.shape[0], hh.shape[1] // 2
    nblk = dff // LANES

    def body(a_ref, b_ref, wa_ref, wb_ref, z_ref):
        rows = lax.broadcasted_iota(jnp.int32, (s, LANES), 0)
        z_ref[...] = (_gelu(_conv3(a_ref[...], wa_ref, rows)) * _conv3(b_ref[...], wb_ref, rows)).astype(BF16)

    col = lambda base: pl.BlockSpec((s, LANES), lambda j: (0, base + j))
    wsp = lambda base: pl.BlockSpec((3, LANES), lambda j: (0, base + j))
    return pl.pallas_call(
        body, name=name, grid=(nblk,), in_specs=[col(0), col(nblk), wsp(0), wsp(nblk)], out_specs=col(0),
        out_shape=jax.ShapeDtypeStruct((s, dff), BF16), compiler_params=_params("parallel"),
    )(hh, hh, cw, cw)


def _ffn_act_bwd(hh, cw, dz, name):
    s, dff = hh.shape[0], hh.shape[1] // 2
    nblk = dff // LANES

    def body(a_ref, b_ref, wa_ref, wb_ref, dz_ref, da_ref, db_ref, dwa_ref, dwb_ref):
        rows = lax.broadcasted_iota(jnp.int32, (s, LANES), 0)
        a, b, dz = a_ref[...], b_ref[...], dz_ref[...]
        ga, slope = _gelu_and_slope(_conv3(a, wa_ref, rows))
        da, dwa_ref[...] = _conv3_bwd(dz * _conv3(b, wb_ref, rows) * slope, a, wa_ref, rows)
        db, dwb_ref[...] = _conv3_bwd(dz * ga, b, wb_ref, rows)
        da_ref[...] = da.astype(BF16)
        db_ref[...] = db.astype(BF16)

    col = lambda base: pl.BlockSpec((s, LANES), lambda j: (0, base + j))
    wsp = lambda base: pl.BlockSpec((3, LANES), lambda j: (0, base + j))
    w8 = lambda base: pl.BlockSpec((8, LANES), lambda j: (0, base + j))
    return pl.pallas_call(
        body, name=name, grid=(nblk,), in_specs=[col(0), col(nblk), wsp(0), wsp(nblk), col(0)],
        out_specs=[col(0), col(0), w8(0), w8(0)],
        out_shape=[jax.ShapeDtypeStruct((s, dff), BF16)] * 2 + [jax.ShapeDtypeStruct((8, dff), F32)] * 2,
        compiler_params=_params("parallel"),
    )(hh, hh, cw, cw, dz)


ANY = pl.BlockSpec(memory_space=pl.ANY)


def _place():
    return lax.axis_index("x"), lax.axis_index("y"), lax.axis_index("c")


def _all_gather(arrs, name):
    n = len(arrs)

    def body(*refs):
        ins, outs = refs[:n], refs[n:2 * n]
        send_sems, recv_sems, local_sems = refs[2 * n:]
        x, y, c = _place()
        me, sibling = (x, y, c), (x, y, 1 - c)
        chips = [(1 - x, y), (x, 1 - y), (1 - x, 1 - y)]

        def slab(a, dev):
            return outs[a].at[4 * dev[0] + 2 * dev[1] + dev[2]]

        def copy(a, k, block, to, src=None):
            return pltpu.make_async_remote_copy(
                src_ref=slab(a, block) if src is None else src, dst_ref=slab(a, block),
                send_sem=send_sems.at[7 * a + k], recv_sem=recv_sems.at[7 * a + k], device_id=to, device_id_type=MESH)

        mine = [pltpu.make_async_copy(ins[a], slab(a, me), local_sems.at[a]) for a in range(n)]
        for cp in mine:
            cp.start()
        first = []
        for a in range(n):
            first.append(copy(a, 0, me, sibling, src=ins[a]))
            first += [copy(a, 1 + j, me, (*chip, c), src=ins[a]) for j, chip in enumerate(chips)]
        for cp in first:
            cp.start()
        passed = []
        for a in range(n):
            for j, chip in enumerate(chips):
                copy(a, 1 + j, (*chip, c), me).wait_recv()
                fwd = copy(a, 4 + j, (*chip, c), sibling)
                fwd.start()
                passed.append(fwd)
        for a in range(n):
            copy(a, 0, sibling, me).wait_recv()
            for j, chip in enumerate(chips):
                copy(a, 4 + j, (*chip, 1 - c), me).wait_recv()
        for cp in first + passed:
            cp.wait_send()
        for cp in mine:
            cp.wait()

    return pl.pallas_call(
        body, name=name, in_specs=[ANY] * n, out_specs=[ANY] * n,
        out_shape=[jax.ShapeDtypeStruct((N_DEV,) + a.shape, a.dtype) for a in arrs],
        scratch_shapes=[pltpu.SemaphoreType.DMA((7 * n,)), pltpu.SemaphoreType.DMA((7 * n,)), pltpu.SemaphoreType.DMA((n,))],
    )(*arrs)


HBM = pl.BlockSpec(memory_space=pltpu.HBM)
SEM = pl.BlockSpec(memory_space=pltpu.SEMAPHORE)
EFFECT = pltpu.SideEffectType.DATAFLOW_SIDE_EFFECTING


def _slot(dev):
    return 4 * dev[0] + 2 * dev[1] + dev[2]


def _exchange_copies(src_refs, land_refs, send_sems, recv_sems, src_view, land_view):
    x, y, c = _place()
    me = (x, y, c)
    peers = [(1 - x if r & 4 else x, 1 - y if r & 2 else y, 1 - c if r & 1 else c) for r in range(1, N_DEV)]
    sends, lands = [], []
    for a, (src, land) in enumerate(zip(src_refs, land_refs)):
        for k, peer in enumerate(peers):
            sems = dict(send_sem=send_sems.at[7 * a + k], recv_sem=recv_sems.at[7 * a + k], device_id=peer,
                        device_id_type=MESH)
            sends.append(pltpu.make_async_remote_copy(src_ref=src_view(src, _slot(peer)),
                                                      dst_ref=land_view(land, _slot(me)), **sems))
            lands.append(pltpu.make_async_remote_copy(src_ref=src_view(src, _slot(me)),
                                                      dst_ref=land_view(land, _slot(peer)), **sems))
    return sends, lands


def _own_copies(src_refs, land_refs, local_sems, src_view, land_view):
    me = _slot(_place())
    return [pltpu.make_async_copy(src_view(src, me), land_view(land, me), local_sems.at[a])
            for a, (src, land) in enumerate(zip(src_refs, land_refs))]


def _exchange_start(srcs, lands, after, src_view, land_view, name):
    n = len(srcs)

    def body(*refs):
        src_refs, land_refs = refs[:n], refs[n:2 * n]
        send_sems, recv_sems, local_sems = refs[2 * n + 1:2 * n + 4]
        token = refs[-1]
        sends, _ = _exchange_copies(src_refs, land_refs, send_sems, recv_sems, src_view, land_view)
        for cp in sends + _own_copies(src_refs, land_refs, local_sems, src_view, land_view):
            cp.start()
        token[...] = jnp.zeros_like(token)

    thru = [pltpu.HBM(a.shape, a.dtype) for a in list(srcs) + list(lands)]
    outs = pl.pallas_call(
        body, name=name,
        out_shape=(pltpu.SemaphoreType.DMA((7 * n,)), pltpu.SemaphoreType.DMA((7 * n,)), pltpu.SemaphoreType.DMA((n,)),
                   *thru, jax.ShapeDtypeStruct((8, LANES), F32)),
        in_specs=[HBM] * (2 * n) + [ANY],
        out_specs=(SEM, SEM, SEM, *([HBM] * (2 * n)), pl.BlockSpec(memory_space=pltpu.VMEM)),
        input_output_aliases={i: 3 + i for i in range(2 * n)},
        compiler_params=pltpu.CompilerParams(has_side_effects=EFFECT),
    )(*[pltpu.with_memory_space_constraint(a, pltpu.HBM) for a in list(srcs) + list(lands)], after)
    return outs[:3], list(outs[3:3 + n]), list(outs[3 + n:3 + 2 * n]), outs[-1]


def _exchange_wait(sems, srcs, lands, after, src_view, land_view, name):
    n = len(srcs)

    def body(*refs):
        src_refs, land_refs = refs[:n], refs[n:2 * n]
        send_sems, recv_sems, local_sems = refs[2 * n:2 * n + 3]
        sends, landed = _exchange_copies(src_refs, land_refs, send_sems, recv_sems, src_view, land_view)
        for cp in sends:
            cp.wait_send()
        for cp in landed:
            cp.wait_recv()
        for cp in _own_copies(src_refs, land_refs, local_sems, src_view, land_view):
            cp.wait()

    after = after if isinstance(after, (list, tuple)) else [after]
    outs = pl.pallas_call(
        body, name=name, out_shape=[pltpu.HBM(a.shape, a.dtype) for a in list(srcs) + list(lands)],
        in_specs=[HBM] * (2 * n) + [SEM, SEM, SEM] + [ANY] * len(after), out_specs=[HBM] * (2 * n),
        input_output_aliases={i: i for i in range(2 * n)},
        compiler_params=pltpu.CompilerParams(has_side_effects=EFFECT),
    )(*srcs, *lands, *sems, *after)
    return list(outs[:n]), list(outs[n:])


ADAMW_BLOCK_BYTES = 1 << 19
PACK_ROWS = 256


def _adamw(slabs, w, m, v, name):
    nl, r, c = w.shape
    row_edges = [r] + [t for t in range(8, r, 8) if r % t == 0]
    col_edges = [c] + [t for t in range(LANES, c, LANES) if c % t == 0]
    fits = [(tr * tc, tc, tr) for tr in row_edges for tc in col_edges if tr * tc * 4 <= ADAMW_BLOCK_BYTES]
    _, tc, tr = max(fits) if fits else (0, min(col_edges), min(row_edges))

    def body(s_ref, w_ref, m_ref, v_ref, g_ref, d_ref, nm_ref, nv_ref):
        g = s_ref[0].astype(F32)
        for q in range(1, N_DEV):
            g = g + s_ref[q].astype(F32)
        m_new = ADAM_B1 * m_ref[...] + (1.0 - ADAM_B1) * g
        v_new = ADAM_B2 * v_ref[...] + (1.0 - ADAM_B2) * (g * g)
        m_hat = m_new / (1.0 - ADAM_B1 ** ADAM_STEP)
        v_hat = v_new / (1.0 - ADAM_B2 ** ADAM_STEP)
        g_ref[...] = g
        d_ref[...] = -ADAM_LR * (m_hat / (jnp.sqrt(v_hat) + ADAM_EPS) + ADAM_WD * w_ref[...])
        nm_ref[...] = m_new
        nv_ref[...] = v_new

    blk = pl.BlockSpec((None, tr, tc), lambda l, i, j: (l, i, j))
    return pl.pallas_call(
        body, name=name, grid=(nl, r // tr, c // tc),
        in_specs=[pl.BlockSpec((N_DEV, None, tr, tc), lambda l, i, j: (0, l, i, j)), blk, blk, blk],
        out_specs=[blk] * 4, out_shape=[jax.ShapeDtypeStruct(w.shape, F32)] * 4,
        compiler_params=_params("parallel", "parallel", "parallel"),
    )(slabs, w, m, v)


def _layout(d):
    off = {"gate": 0, "q": 3 * d}
    off["bg"] = off["q"] + 3 * D_ATT
    off["u"] = off["bg"] + 3 * D_CONV
    off["f"] = off["u"] + 2 * D_SGU
    width = -(-(off["f"] + LANES) // 512) * 512
    return off, width


def _pad_w_in(wt, d, token):
    off, width = _layout(d)
    nqkv, nrest = 3 * D_ATT, 3 * D_CONV + 2 * D_SGU
    pad = jnp.zeros((width - off["f"] - N_HEADS, wt.shape[1]), wt.dtype) + token[0, 0].astype(wt.dtype)
    return jnp.concatenate([wt[nqkv + N_HEADS + nrest:], wt[:nqkv], wt[nqkv + N_HEADS:nqkv + N_HEADS + nrest],
                            wt[nqkv:nqkv + N_HEADS], pad], axis=0)


def _unpad_w_in(wtp, d):
    off, _ = _layout(d)
    return jnp.concatenate([wtp[off["q"]:off["bg"]], wtp[off["f"]:off["f"] + N_HEADS], wtp[off["bg"]:off["f"]],
                            wtp[:off["q"]]], axis=0)


def _cols_from_slabs(g):
    return jnp.transpose(g, (1, 0, 2)).reshape(g.shape[1], N_DEV * g.shape[2])


def _cols_to_slabs(w):
    r, c = w.shape[0], w.shape[1] // N_DEV
    return jnp.transpose(w.reshape(r, N_DEV, c), (1, 0, 2))


def kernel(x, pre_mix_g, post_mix_g, pre_ffn_g, post_ffn_g, w_in, b_forget, b_gate, conv_mix_w, sgu_ln_g, sgu_ln_b, sgu_w, sgu_b, w_branch_att, w_branch_conv, w_branch_sgu, w_out, w_ffn_up, conv_ffn_w, w_ffn_down, loss_target, m_pre_mix_g, m_post_mix_g, m_pre_ffn_g, m_post_ffn_g, m_w_in, m_b_forget, m_b_gate, m_conv_mix_w, m_sgu_ln_g, m_sgu_ln_b, m_sgu_w, m_sgu_b, m_w_branch_att, m_w_branch_conv, m_w_branch_sgu, m_w_out, m_w_ffn_up, m_conv_ffn_w, m_w_ffn_down, v_pre_mix_g, v_post_mix_g, v_pre_ffn_g, v_post_ffn_g, v_w_in, v_b_forget, v_b_gate, v_conv_mix_w, v_sgu_ln_g, v_sgu_ln_b, v_sgu_w, v_sgu_b, v_w_branch_att, v_w_branch_conv, v_w_branch_sgu, v_w_out, v_w_ffn_up, v_conv_ffn_w, v_w_ffn_down):
    depth = w_in.shape[0]
    s, d = x.shape[1], x.shape[2]
    dff = w_ffn_down.shape[1] * N_DEV
    off, _ = _layout(d)
    qblk, bgblk, ublk, fblk = off["q"] // LANES, off["bg"] // LANES, off["u"] // D_SGU, off["f"] // LANES
    x0 = x.reshape(s, d)
    target = loss_target.reshape(s, d)
    ncm, ncf = conv_mix_w.shape[2], conv_ffn_w.shape[2]

    lo = d // N_DEV

    whole = lambda ref, slot: ref
    slab = lambda ref, slot: ref.at[slot]

    w_in_t, w_up_t = jnp.transpose(w_in, (0, 2, 1)), jnp.transpose(w_ffn_up, (0, 2, 1))

    def shards_of(l, part):
        if part == "mix":
            small = jnp.concatenate([b_gate[l], conv_mix_w[l], conv_ffn_w[l]], axis=1)
            return [w_in_t[l].astype(BF16), w_branch_att[l].astype(BF16), w_branch_conv[l].astype(BF16),
                    w_branch_sgu[l].astype(BF16), w_out[l].astype(BF16), small]
        return [w_up_t[l].astype(BF16), w_ffn_down[l].astype(BF16)]

    def gather_start(l, part, after):
        shards = shards_of(l, part)
        lands = [lax.empty((N_DEV,) + a.shape, a.dtype) for a in shards]
        return _exchange_start(shards, lands, after, whole, slab, name=f"gather_start_{part}_{l}")

    def gather_finish(l, part, started, after):
        sems, shards, lands, _ = started[part]
        shards, lands = _exchange_wait(sems, shards, lands, after, whole, slab, name=f"gather_wait_{part}_{l}")
        token = jnp.zeros((8, LANES), F32)
        if l + 1 < depth:
            started[part] = gather_start(l + 1, part, lands[0])
            token = started[part][3]
        return lands, token

    def bfull(l):
        return jnp.repeat(jnp.transpose(sgu_b[l]), HEAD_DIM, axis=1)

    def bf_pad(l):
        return jnp.pad(b_forget[l], (0, LANES - N_HEADS)).reshape(1, LANES)

    saved = []
    weights = []
    xin = x0
    xn = _prenorm(x0, pre_mix_g[0:1], name="prenorm_first")
    loss_acc = dy = None
    shards0 = shards_of(0, "mix")
    (first_in,) = _all_gather(shards0[:1], name="gather_first")
    rest0 = _exchange_start(shards0[1:], [lax.empty((N_DEV,) + a.shape, a.dtype) for a in shards0[1:]], first_in,
                            whole, slab, name="gather_start_rest_0")
    started = {"ffn": gather_start(0, "ffn", rest0[3])}
    for l in range(depth):
        if l == 0:
            g_in, others, token = first_in, None, started["ffn"][3]
        else:
            (g_in, *others), token = gather_finish(l, "mix", started, xin)
        w = dict(w_in=_pad_w_in(g_in.reshape(N_DEV * g_in.shape[1], d), d, token))
        weights.append(w)
        h = _mm(xn, w["w_in"], "nt", F32, name="proj_in")
        c, ct = _forget_prep(h, bf_pad(l), fblk, name="forget_prep")
        att, lse = _attention_fwd(h, c, ct, qblk, name="attention_fwd")
        if l == 0:
            _, others = _exchange_wait(rest0[0], rest0[1], rest0[2], att, whole, slab, name="gather_wait_rest_0")
        g_a, g_c, g_s, g_o, g_small = others
        g_small = _cols_from_slabs(g_small).reshape(3, N_DEV, -1)
        w.update(wa=_cols_from_slabs(g_a), wc=_cols_from_slabs(g_c), ws=_cols_from_slabs(g_s), w_out=g_o.reshape(d, d),
                 b_gate=g_small[:, :, :lo].reshape(3, d), cmw=g_small[:, :, lo:lo + ncm].reshape(3, D_CONV),
                 cfw=g_small[:, :, lo + ncm:].reshape(3, 2 * dff))
        if l == 0 and depth > 1:
            started["mix"] = gather_start(1, "mix", g_a)
            w["cmw"] = w["cmw"] + started["mix"][3][0, 0]
        yc = _sconv_fwd(h, w["cmw"], bgblk, name="sconv_fwd")
        ys = _sgu_fwd(h, sgu_ln_g[l:l + 1], sgu_ln_b[l:l + 1], sgu_w[l], bfull(l), ublk, name="sgu_fwd")
        (g_up, g_dn), token = gather_finish(l, "ffn", started, ys)
        w["w_up"], w["w_dn"] = g_up.reshape(2 * dff, d), g_dn.reshape(dff, d)
        merged, o, x1, xn2 = _merge_fwd(h, w["b_gate"], att, yc, ys, w["wa"], w["wc"], w["ws"], w["w_out"], xin,
                                        post_mix_g[l:l + 1], pre_ffn_g[l:l + 1] + token[0, 0], name="merge_proj_out")
        hh = _mm(xn2, w["w_up"], "nt", F32, name="ffn_up")
        z = _ffn_act_fwd(hh, w["cfw"], name="ffn_act_fwd")
        layer = dict(xin=xin, xn=xn, h=h, c=c, ct=ct, lse=lse, att=att, yc=yc, ys=ys, merged=merged, o=o, x1=x1,
                     xn2=xn2, hh=hh, z=z)
        if l + 1 < depth:
            layer["f"], xin, xn = _mm_postnorm(z, w["w_dn"], x1, post_ffn_g[l:l + 1], pre_mix_g[l + 1:l + 2],
                                               name="ffn_down")
        else:
            layer["f"], dy, loss_acc = _mm_postnorm_loss(z, w["w_dn"], x1, post_ffn_g[l:l + 1], target,
                                                         name="ffn_down_loss")
        saved.append(layer)
    loss = lax.psum(loss_acc[0, 0] * (0.5 / d), ("x", "y", "c"))

    rep = {k: [None] * depth for k in ("pre_mix_g", "post_mix_g", "pre_ffn_g", "post_ffn_g", "b_forget", "sgu_ln_g",
                                       "sgu_ln_b", "sgu_w", "sgu_b")}
    nsmall = lo + ncm + ncf
    lands = {"win": [lax.empty((N_DEV, depth) + w_in_t.shape[1:], BF16)],
             "mid": [lax.empty((N_DEV, depth) + shp, dt) for shp, dt in (
                 (w_branch_att.shape[1:], BF16), (w_branch_conv.shape[1:], BF16), (w_branch_sgu.shape[1:], BF16),
                 (w_out.shape[1:], BF16), ((3, nsmall), F32))],
             "ffn": [lax.empty((N_DEV, depth) + shp, BF16) for shp in (w_up_t.shape[1:], w_ffn_down.shape[1:])]}
    scatters = {part: [None] * depth for part in lands}

    def scatter_start(l, part, sends, after):
        layer_slab = lambda ref, slot: ref.at[slot, l]
        sems, sends, lands[part], token = _exchange_start(sends, lands[part], after, slab, layer_slab,
                                                          name=f"scatter_start_{part}_{l}")
        scatters[part][l] = (sems, sends, layer_slab)
        return token

    def scatter_finish(part, after):
        for l in range(depth):
            sems, sends, layer_slab = scatters[part][l]
            _, lands[part] = _exchange_wait(sems, sends, lands[part], after, slab, layer_slab,
                                            name=f"scatter_wait_{part}_{l}")
        return lands[part]

    token = jnp.zeros((8, LANES), F32)
    dx = dy
    for l in reversed(range(depth)):
        w, a = weights[l], saved[l]
        df, rep["post_ffn_g"][l] = _postnorm_bwd(a["f"], post_ffn_g[l:l + 1] + token[0, 0], dx, name="postnorm_bwd")
        dz = _mm(df, w["w_dn"], "nt", F32, name="ffn_down_dx")
        g_dn = _mm(a["z"], df, "tn", BF16, name="ffn_down_dw")
        dha, dhb, dcwa, dcwb = _ffn_act_bwd(a["hh"], w["cfw"], dz, name="ffn_act_bwd")
        dhh = jnp.concatenate([dha, dhb], axis=1)
        dcfw = jnp.concatenate([dcwa[0:3], dcwb[0:3]], axis=1)
        g_up = _mm(dhh, a["xn2"], "tn", BF16, name="ffn_up_dw")
        token = scatter_start(l, "ffn", [g_up.reshape(N_DEV, 2 * dff // N_DEV, d), g_dn.reshape(N_DEV, dff // N_DEV, d)],
                              dz)
        dx1, rep["pre_ffn_g"][l] = _mm_prenorm_bwd(dhh, w["w_up"], a["x1"], pre_ffn_g[l:l + 1], dx, token, name="ffn_up_dx")
        do, rep["post_mix_g"][l], dmerged = _postnorm_bwd_mm(a["o"], post_mix_g[l:l + 1], dx1, w["w_out"], name="proj_out_dx")
        g_o = _mm(a["merged"], do, "tn", BF16, name="proj_out_dw")
        (datt, dconv, dsgu), (g_a, g_c, g_s), dgl, dbg = _merge_bwd(
            a["h"], w["b_gate"], a["att"], a["yc"], a["ys"], w["wa"], w["wc"], w["ws"], dmerged, name="merge_bwd")
        dbgate, dcg, dhc, dcmw = _sconv_bwd(a["h"], w["cmw"], dconv, bgblk, name="sconv_bwd")
        sends = [_cols_to_slabs(g_a), _cols_to_slabs(g_c), _cols_to_slabs(g_s), g_o.reshape(N_DEV, d // N_DEV, d),
                 jnp.concatenate([_cols_to_slabs(dbg), _cols_to_slabs(dcmw[0:3]), _cols_to_slabs(dcfw)], axis=2)]
        token = scatter_start(l, "mid", sends, dx1)
        dq, dk, dv, dct4, dcq4 = _attention_bwd(a["h"], a["c"], a["ct"], a["lse"], a["att"], datt, token, qblk,
                                                name="attention_bwd")
        dfl, dbf = _forget_prep_bwd(a["h"], bf_pad(l), dct4, dcq4, fblk, name="forget_prep_bwd")
        rep["b_forget"][l] = dbf[0, :N_HEADS]
        du, dvs, dlg, dlb, dsw, dbfull = _sgu_bwd(a["h"], sgu_ln_g[l:l + 1], sgu_ln_b[l:l + 1], sgu_w[l], bfull(l), dsgu,
                                                  ublk, name="sgu_bwd")
        rep["sgu_ln_g"][l], rep["sgu_ln_b"][l], rep["sgu_w"][l] = dlg, dlb, dsw
        rep["sgu_b"][l] = jnp.transpose(jnp.sum(dbfull.reshape(SGU_CHUNK, N_SGU_GROUPS, HEAD_DIM), axis=2))
        dh = jnp.concatenate([dgl[0], dgl[1], dgl[2], dq.astype(BF16), dk.astype(BF16), dv.astype(BF16), dbgate, dcg, dhc,
                              du, dvs, dfl.astype(BF16), jnp.zeros((s, w["w_in"].shape[0] - off["f"] - LANES), BF16)], axis=1)
        g_in = _mm(dh, a["xn"], "tn", BF16, name="proj_in_dw")
        token = scatter_start(l, "win", [_unpad_w_in(g_in, d).reshape(N_DEV, -1, d)], dx1)
        dx, rep["pre_mix_g"][l] = _mm_prenorm_bwd(dh, w["w_in"], a["xin"], pre_mix_g[l:l + 1], dx1, token, name="proj_in_dx")

    outs = {}
    t3 = lambda arr: jnp.transpose(arr, (0, 2, 1))

    def update(name_, slabs, w_, m_, v_, transposed=False):
        if transposed:
            w_, m_, v_ = t3(w_), t3(m_), t3(v_)
        shp = w_.shape
        w3 = w_.reshape((shp[0], -1, shp[-1])) if w_.ndim >= 3 else w_.reshape((1,) + shp)
        res = _adamw(slabs.reshape((N_DEV,) + w3.shape), w3, m_.reshape(w3.shape), v_.reshape(w3.shape),
                     name="adamw_" + name_)
        outs[name_] = tuple(t3(t.reshape(shp)) if transposed else t.reshape(shp) for t in res)
        return res[0]

    rep_names = ("pre_mix_g", "post_mix_g", "pre_ffn_g", "post_ffn_g", "b_forget", "sgu_ln_g", "sgu_ln_b", "sgu_w", "sgu_b")
    rep_w = dict(pre_mix_g=(pre_mix_g, m_pre_mix_g, v_pre_mix_g), post_mix_g=(post_mix_g, m_post_mix_g, v_post_mix_g),
                 pre_ffn_g=(pre_ffn_g, m_pre_ffn_g, v_pre_ffn_g), post_ffn_g=(post_ffn_g, m_post_ffn_g, v_post_ffn_g),
                 b_forget=(b_forget, m_b_forget, v_b_forget), sgu_ln_g=(sgu_ln_g, m_sgu_ln_g, v_sgu_ln_g),
                 sgu_ln_b=(sgu_ln_b, m_sgu_ln_b, v_sgu_ln_b), sgu_w=(sgu_w, m_sgu_w, v_sgu_w), sgu_b=(sgu_b, m_sgu_b, v_sgu_b))

    def pack(parts):
        rows = [jnp.pad(p.reshape(-1), (0, -p.size % LANES)).reshape(-1, LANES) for p in parts]
        rows = jnp.concatenate(rows, axis=0)
        return jnp.pad(rows, ((0, -rows.shape[0] % PACK_ROWS), (0, 0)))

    part = pack([jnp.stack([g.reshape(rep_w[k][0].shape[1:]) for g in rep[k]]) for k in rep_names])
    small_sems, small_src, small_land, _ = _exchange_start([part], [lax.empty((N_DEV,) + part.shape, F32)], dx, whole, slab,
                                                            name="gather_small_start")

    got_up, got_dn = scatter_finish("ffn", dx)
    done = [update("w_ffn_up", got_up, w_ffn_up, m_w_ffn_up, v_w_ffn_up, transposed=True),
            update("w_ffn_down", got_dn, w_ffn_down, m_w_ffn_down, v_w_ffn_down)]

    got_a, got_c, got_s, got_o, small = scatter_finish("mid", done)
    done = [update("w_branch_att", got_a, w_branch_att, m_w_branch_att, v_w_branch_att),
            update("w_branch_conv", got_c, w_branch_conv, m_w_branch_conv, v_w_branch_conv),
            update("w_branch_sgu", got_s, w_branch_sgu, m_w_branch_sgu, v_w_branch_sgu),
            update("w_out", got_o, w_out, m_w_out, v_w_out),
            update("b_gate", small[..., :lo], b_gate, m_b_gate, v_b_gate),
            update("conv_mix_w", small[..., lo:lo + ncm], conv_mix_w, m_conv_mix_w, v_conv_mix_w),
            update("conv_ffn_w", small[..., lo + ncm:], conv_ffn_w, m_conv_ffn_w, v_conv_ffn_w)]

    _, (gathered,) = _exchange_wait(small_sems, small_src, small_land, done, whole, slab, name="gather_small_wait")
    packed = [pack([rep_w[k][i] for k in rep_names]) for i in range(3)]
    res = _adamw(gathered.reshape(N_DEV, 1, -1, LANES), *[p.reshape(1, -1, LANES) for p in packed], name="adamw_replicated")
    row = 0
    for k in rep_names:
        shp = rep_w[k][0].shape
        size = math.prod(shp)
        nrows = -(-size // LANES)
        outs[k] = tuple(t[0, row:row + nrows].reshape(-1)[:size].reshape(shp) for t in res)
        row += nrows

    (got_in,) = scatter_finish("win", res[0])
    update("w_in", got_in, w_in, m_w_in, v_w_in, transposed=True)

    order = ("pre_mix_g", "post_mix_g", "pre_ffn_g", "post_ffn_g", "w_in", "b_forget", "b_gate", "conv_mix_w", "sgu_ln_g",
             "sgu_ln_b", "sgu_w", "sgu_b", "w_branch_att", "w_branch_conv", "w_branch_sgu", "w_out", "w_ffn_up",
             "conv_ffn_w", "w_ffn_down")
    grad_x = dx.reshape(x.shape)
    return (loss, grad_x, *[outs[k][0] for k in order], *[outs[k][1] for k in order], *[outs[k][2] for k in order],
            *[outs[k][3] for k in order])
```
